```python
import math
import jax, jax.numpy as jnp
from jax import lax
import numpy as np

D_MODEL = 1024
BATCH = 16
SEQ = 2048
DEPTH = 2

CHUNK = 64
HEAD_DIM = 64
N_MAIN_HEADS = 12
N_MEM_HEADS = 4
MAIN_WIDTH = N_MAIN_HEADS * HEAD_DIM
MEM_WIDTH = N_MEM_HEADS * HEAD_DIM
MIX_WIDTH = MAIN_WIDTH + MEM_WIDTH
N_MEM = 256
D_FF = 2816
CONV_WIDTH = 3
Q_BLOCK = 128
N_A_LAYERS = DEPTH // 2
N_B_LAYERS = DEPTH - N_A_LAYERS
A_IN_WIDTH = 3 * MAIN_WIDTH + N_MAIN_HEADS + MEM_WIDTH
B_IN_WIDTH = MAIN_WIDTH + MEM_WIDTH
FORGET_BIAS_INIT = 3.0
FORGET_W_SCALE = 0.1
EPS = 1e-6

kernel_name = "yoco_fox_stickbreak_memory_convffn"


def rmsnorm(x, g):
    xf = x.astype(jnp.float32)
    y = xf * lax.rsqrt(jnp.mean(xf * xf, axis=-1, keepdims=True) + EPS)
    return (y * g.astype(jnp.float32)).astype(x.dtype)


def split_heads(t, n_heads):
    b, s, _ = t.shape
    return t.reshape(b, s, n_heads, HEAD_DIM).transpose(0, 2, 1, 3)


def merge_heads(t):
    b, n, s, d = t.shape
    return t.transpose(0, 2, 1, 3).reshape(b, s, n * d)


def forgetting_attention(q, k, v, log_f):
    seq = q.shape[2]
    c = jnp.cumsum(log_f, axis=-1)
    scale = HEAD_DIM ** -0.5
    outs = []
    for i in range(seq // Q_BLOCK):
        q0, q1 = i * Q_BLOCK, (i + 1) * Q_BLOCK
        kb, vb = k[:, :, :q1], v[:, :, :q1]
        logits = jnp.einsum('bhqd,bhkd->bhqk', q[:, :, q0:q1], kb).astype(jnp.float32) * scale
        logits = logits + c[:, :, q0:q1, None] - c[:, :, None, :q1]
        t_idx = jnp.arange(q0, q1)[:, None]
        s_idx = jnp.arange(q1)[None, :]
        logits = jnp.where(s_idx <= t_idx, logits, -jnp.inf)
        p = jax.nn.softmax(logits, axis=-1)
        outs.append(jnp.einsum('bhqk,bhkd->bhqd', p.astype(vb.dtype), vb))
    return jnp.concatenate(outs, axis=2)


def stick_breaking_attention(q, k, v):
    seq = q.shape[2]
    scale = HEAD_DIM ** -0.5
    outs = []
    for i in range(seq // Q_BLOCK):
        q0, q1 = i * Q_BLOCK, (i + 1) * Q_BLOCK
        kb, vb = k[:, :, :q1], v[:, :, :q1]
        z = jnp.einsum('bhqd,bhkd->bhqk', q[:, :, q0:q1], kb).astype(jnp.float32) * scale
        t_idx = jnp.arange(q0, q1)[:, None]
        s_idx = jnp.arange(q1)[None, :]
        causal = s_idx < t_idx
        log_1m_beta = jnp.where(causal, jax.nn.log_sigmoid(-z), 0.0)
        cum = jnp.cumsum(log_1m_beta, axis=-1)
        log_a = jax.nn.log_sigmoid(z) + cum[..., -1:] - cum
        a = jnp.where(causal, jnp.exp(log_a), 0.0)
        outs.append(jnp.einsum('bhqk,bhkd->bhqd', a.astype(vb.dtype), vb))
    return jnp.concatenate(outs, axis=2)


def memory_attention(q, mem_k, mem_v):
    logits = jnp.einsum('bhqd,bhmd->bhqm', q, mem_k).astype(jnp.float32) * (HEAD_DIM ** -0.5)
    p = jax.nn.softmax(logits, axis=-1)
    return jnp.einsum('bhqm,bhmd->bhqd', p.astype(mem_v.dtype), mem_v)


def conv_ffn(h, w_up, conv_w, conv_b, w_down):
    u = h @ w_up
    s = u.shape[1]
    up = jnp.pad(u, ((0, 0), (CONV_WIDTH - 1, 0), (0, 0)))
    uc = conv_b
    for j in range(CONV_WIDTH):
        uc = uc + conv_w[j] * up[:, j:j + s]
    gate, val = jnp.split(uc, 2, axis=-1)
    return (jax.nn.silu(gate) * val) @ w_down


def _fwd_setup_inputs(seed: int = 0) -> dict:
    key = jax.random.key(seed)
    ks = jax.random.split(key, 17)
    f32 = jnp.float32

    def nrm(k, shape, scale):
        return jax.random.normal(k, shape, f32) * scale

    def gain(k, shape):
        return 1.0 + 0.05 * jax.random.normal(k, shape, f32)

    x = nrm(ks[0], (BATCH, SEQ, D_MODEL), 1.0)
    mem = nrm(ks[1], (BATCH, N_MEM, D_MODEL), 1.0)
    ln_mix_g = gain(ks[2], (DEPTH, D_MODEL))
    w_in_a = nrm(ks[3], (N_A_LAYERS, D_MODEL, A_IN_WIDTH), D_MODEL ** -0.5)
    w_in_a = w_in_a.at[:, :, 3 * MAIN_WIDTH:3 * MAIN_WIDTH + N_MAIN_HEADS].multiply(FORGET_W_SCALE)
    b_f_a = FORGET_BIAS_INIT + 0.5 * jax.random.normal(ks[4], (N_A_LAYERS, N_MAIN_HEADS), f32)
    w_in_b = nrm(ks[5], (N_B_LAYERS, D_MODEL, B_IN_WIDTH), D_MODEL ** -0.5)
    ln_kv_g = gain(ks[6], (D_MODEL,))
    w_kv = nrm(ks[7], (D_MODEL, 2 * MAIN_WIDTH), D_MODEL ** -0.5)
    ln_mem_g = gain(ks[8], (DEPTH, D_MODEL))
    w_memkv = nrm(ks[9], (DEPTH, D_MODEL, 2 * MEM_WIDTH), D_MODEL ** -0.5)
    w_out = nrm(ks[10], (DEPTH, MIX_WIDTH, D_MODEL), MIX_WIDTH ** -0.5)
    ln_ffn_g = gain(ks[11], (DEPTH, D_MODEL))
    w_up = nrm(ks[12], (DEPTH, D_MODEL, 2 * D_FF), D_MODEL ** -0.5)
    conv_w = nrm(ks[13], (DEPTH, CONV_WIDTH, 2 * D_FF), CONV_WIDTH ** -0.5)
    conv_b = nrm(ks[14], (DEPTH, 2 * D_FF), 0.02)
    w_down = nrm(ks[15], (DEPTH, D_FF, D_MODEL), D_FF ** -0.5)
    final_g = gain(ks[16], (D_MODEL,))
    return {'x': x, 'mem': mem, 'ln_mix_g': ln_mix_g, 'w_in_a': w_in_a, 'b_f_a': b_f_a,
            'w_in_b': w_in_b, 'ln_kv_g': ln_kv_g, 'w_kv': w_kv, 'ln_mem_g': ln_mem_g,
            'w_memkv': w_memkv, 'w_out': w_out, 'ln_ffn_g': ln_ffn_g, 'w_up': w_up,
            'conv_w': conv_w, 'conv_b': conv_b, 'w_down': w_down, 'final_g': final_g}


def _fwd_reference(x, mem, ln_mix_g, w_in_a, b_f_a, w_in_b, ln_kv_g, w_kv, ln_mem_g,
              w_memkv, w_out, ln_ffn_g, w_up, conv_w, conv_b, w_down, final_g):
    k_sh = None
    v_sh = None
    for layer in range(DEPTH):
        h = rmsnorm(x, ln_mix_g[layer])
        if layer < N_A_LAYERS:
            proj = h @ w_in_a[layer]
            q, k, v, f_logit, q_mem = jnp.split(
                proj, [MAIN_WIDTH, 2 * MAIN_WIDTH, 3 * MAIN_WIDTH, 3 * MAIN_WIDTH + N_MAIN_HEADS], axis=-1)
            log_f = jax.nn.log_sigmoid((f_logit + b_f_a[layer]).astype(jnp.float32))
            o_main = forgetting_attention(split_heads(q, N_MAIN_HEADS), split_heads(k, N_MAIN_HEADS),
                                          split_heads(v, N_MAIN_HEADS), log_f.transpose(0, 2, 1))
        else:
            if layer == N_A_LAYERS:
                kv = rmsnorm(x, ln_kv_g) @ w_kv
                k_s, v_s = jnp.split(kv, 2, axis=-1)
                k_sh = split_heads(k_s, N_MAIN_HEADS)
                v_sh = split_heads(v_s, N_MAIN_HEADS)
            proj = h @ w_in_b[layer - N_A_LAYERS]
            q, q_mem = jnp.split(proj, [MAIN_WIDTH], axis=-1)
            o_main = stick_breaking_attention(split_heads(q, N_MAIN_HEADS), k_sh, v_sh)
        mem_kv = rmsnorm(mem, ln_mem_g[layer]) @ w_memkv[layer]
        mk, mv = jnp.split(mem_kv, 2, axis=-1)
        o_mem = memory_attention(split_heads(q_mem, N_MEM_HEADS), split_heads(mk, N_MEM_HEADS),
                                 split_heads(mv, N_MEM_HEADS))
        o = jnp.concatenate([merge_heads(o_main), merge_heads(o_mem)], axis=-1) @ w_out[layer]
        x = x + o
        x = x + conv_ffn(rmsnorm(x, ln_ffn_g[layer]), w_up[layer], conv_w[layer], conv_b[layer], w_down[layer])
    return rmsnorm(x, final_g)


import jax as _jax
import jax.numpy as _jnp

TWIN_FORMAT = 'train_step'
FWD_PARAMS = ['x', 'mem', 'ln_mix_g', 'w_in_a', 'b_f_a', 'w_in_b', 'ln_kv_g', 'w_kv', 'ln_mem_g', 'w_memkv', 'w_out', 'ln_ffn_g', 'w_up', 'conv_w', 'conv_b', 'w_down', 'final_g']
TWIN_WEIGHTS = ['ln_mix_g', 'w_in_a', 'b_f_a', 'w_in_b', 'ln_kv_g', 'w_kv', 'ln_mem_g', 'w_memkv', 'w_out', 'ln_ffn_g', 'w_up', 'conv_w', 'conv_b', 'w_down', 'final_g']
TWIN_DIFF_INPUT = 'x'
TWIN_INPUTS = ['x', 'mem', 'ln_mix_g', 'w_in_a', 'b_f_a', 'w_in_b', 'ln_kv_g', 'w_kv', 'ln_mem_g', 'w_memkv', 'w_out', 'ln_ffn_g', 'w_up', 'conv_w', 'conv_b', 'w_down', 'final_g', 'loss_target', 'm_ln_mix_g', 'm_w_in_a', 'm_b_f_a', 'm_w_in_b', 'm_ln_kv_g', 'm_w_kv', 'm_ln_mem_g', 'm_w_memkv', 'm_w_out', 'm_ln_ffn_g', 'm_w_up', 'm_conv_w', 'm_conv_b', 'm_w_down', 'm_final_g', 'v_ln_mix_g', 'v_w_in_a', 'v_b_f_a', 'v_w_in_b', 'v_ln_kv_g', 'v_w_kv', 'v_ln_mem_g', 'v_w_memkv', 'v_w_out', 'v_ln_ffn_g', 'v_w_up', 'v_conv_w', 'v_conv_b', 'v_w_down', 'v_final_g']
TWIN_OUTPUTS = ['loss', 'grad_x', 'grad_ln_mix_g', 'grad_w_in_a', 'grad_b_f_a', 'grad_w_in_b', 'grad_ln_kv_g', 'grad_w_kv', 'grad_ln_mem_g', 'grad_w_memkv', 'grad_w_out', 'grad_ln_ffn_g', 'grad_w_up', 'grad_conv_w', 'grad_conv_b', 'grad_w_down', 'grad_final_g', 'delta_ln_mix_g', 'delta_w_in_a', 'delta_b_f_a', 'delta_w_in_b', 'delta_ln_kv_g', 'delta_w_kv', 'delta_ln_mem_g', 'delta_w_memkv', 'delta_w_out', 'delta_ln_ffn_g', 'delta_w_up', 'delta_conv_w', 'delta_conv_b', 'delta_w_down', 'delta_final_g', 'new_m_ln_mix_g', 'new_m_w_in_a', 'new_m_b_f_a', 'new_m_w_in_b', 'new_m_ln_kv_g', 'new_m_w_kv', 'new_m_ln_mem_g', 'new_m_w_memkv', 'new_m_w_out', 'new_m_ln_ffn_g', 'new_m_w_up', 'new_m_conv_w', 'new_m_conv_b', 'new_m_w_down', 'new_m_final_g', 'new_v_ln_mix_g', 'new_v_w_in_a', 'new_v_b_f_a', 'new_v_w_in_b', 'new_v_ln_kv_g', 'new_v_w_kv', 'new_v_ln_mem_g', 'new_v_w_memkv', 'new_v_w_out', 'new_v_ln_ffn_g', 'new_v_w_up', 'new_v_conv_w', 'new_v_conv_b', 'new_v_w_down', 'new_v_final_g']
TWIN_LEAF_KINDS = {'loss': 'loss', 'grad_x': 'grad_x', 'grad_ln_mix_g': 'grad_w', 'grad_w_in_a': 'grad_w', 'grad_b_f_a': 'grad_w', 'grad_w_in_b': 'grad_w', 'grad_ln_kv_g': 'grad_w', 'grad_w_kv': 'grad_w', 'grad_ln_mem_g': 'grad_w', 'grad_w_memkv': 'grad_w', 'grad_w_out': 'grad_w', 'grad_ln_ffn_g': 'grad_w', 'grad_w_up': 'grad_w', 'grad_conv_w': 'grad_w', 'grad_conv_b': 'grad_w', 'grad_w_down': 'grad_w', 'grad_final_g': 'grad_w', 'delta_ln_mix_g': 'delta_w', 'delta_w_in_a': 'delta_w', 'delta_b_f_a': 'delta_w', 'delta_w_in_b': 'delta_w', 'delta_ln_kv_g': 'delta_w', 'delta_w_kv': 'delta_w', 'delta_ln_mem_g': 'delta_w', 'delta_w_memkv': 'delta_w', 'delta_w_out': 'delta_w', 'delta_ln_ffn_g': 'delta_w', 'delta_w_up': 'delta_w', 'delta_conv_w': 'delta_w', 'delta_conv_b': 'delta_w', 'delta_w_down': 'delta_w', 'delta_final_g': 'delta_w', 'new_m_ln_mix_g': 'new_m', 'new_m_w_in_a': 'new_m', 'new_m_b_f_a': 'new_m', 'new_m_w_in_b': 'new_m', 'new_m_ln_kv_g': 'new_m', 'new_m_w_kv': 'new_m', 'new_m_ln_mem_g': 'new_m', 'new_m_w_memkv': 'new_m', 'new_m_w_out': 'new_m', 'new_m_ln_ffn_g': 'new_m', 'new_m_w_up': 'new_m', 'new_m_conv_w': 'new_m', 'new_m_conv_b': 'new_m', 'new_m_w_down': 'new_m', 'new_m_final_g': 'new_m', 'new_v_ln_mix_g': 'new_v', 'new_v_w_in_a': 'new_v', 'new_v_b_f_a': 'new_v', 'new_v_w_in_b': 'new_v', 'new_v_ln_kv_g': 'new_v', 'new_v_w_kv': 'new_v', 'new_v_ln_mem_g': 'new_v', 'new_v_w_memkv': 'new_v', 'new_v_w_out': 'new_v', 'new_v_ln_ffn_g': 'new_v', 'new_v_w_up': 'new_v', 'new_v_conv_w': 'new_v', 'new_v_conv_b': 'new_v', 'new_v_w_down': 'new_v', 'new_v_final_g': 'new_v'}


def _forward(args):
    return _fwd_reference(*[args[k] for k in FWD_PARAMS])


def _output_shape():
    out = _jax.eval_shape(lambda: _forward(_fwd_setup_inputs(0)))
    return out.shape, out.dtype

N_MICROBATCH = 1
ADAM_LR = 0.001
ADAM_B1 = 0.9
ADAM_B2 = 0.999
ADAM_EPS = 1e-08
ADAM_WD = 0.01
ADAM_STEP = 10
PER_EXAMPLE_BATCH_AXIS = {'x': 0, 'mem': 0, 'loss_target': 0}
SHARED_INPUTS = []
_WEIGHT_DTYPES = {'ln_mix_g': _jnp.float32, 'w_in_a': _jnp.float32, 'b_f_a': _jnp.float32, 'w_in_b': _jnp.float32, 'ln_kv_g': _jnp.float32, 'w_kv': _jnp.float32, 'ln_mem_g': _jnp.float32, 'w_memkv': _jnp.float32, 'w_out': _jnp.float32, 'ln_ffn_g': _jnp.float32, 'w_up': _jnp.float32, 'conv_w': _jnp.float32, 'conv_b': _jnp.float32, 'w_down': _jnp.float32, 'final_g': _jnp.float32}
MOMENT_SCALE = {'ln_mix_g': 6.625205e-02, 'w_in_a': 5.557575e-02, 'b_f_a': 6.193457e-01, 'w_in_b': 3.873221e-02, 'ln_kv_g': 9.178984e-02, 'w_kv': 7.419791e-02, 'ln_mem_g': 1.523534e-02, 'w_memkv': 2.074669e-02, 'w_out': 7.188558e-02, 'ln_ffn_g': 1.318348e-01, 'w_up': 5.649061e-02, 'conv_w': 5.576313e-02, 'conv_b': 5.625151e-02, 'w_down': 9.277709e-02, 'final_g': 3.208607e+01}


def _to_microbatches(a, axis):
    t = _jnp.moveaxis(a, axis, 0)
    t = t.reshape((N_MICROBATCH, t.shape[0] // N_MICROBATCH) + t.shape[1:])
    return _jnp.moveaxis(t, 1, axis + 1)


def setup_inputs(seed: int = 0) -> dict:
    inp = _fwd_setup_inputs(seed)
    key = _jax.random.fold_in(_jax.random.key(seed), 7919)
    shape, _ = _output_shape()
    out = dict(inp)
    out["loss_target"] = _jax.random.normal(_jax.random.fold_in(key, 0), shape, _jnp.float32)
    for i, name in enumerate(TWIN_WEIGHTS):
        w = inp[name].astype(_jnp.float32)
        if MOMENT_SCALE is None:
            s = _jnp.sqrt(_jnp.mean(_jnp.square(w)) + 1e-30)
        else:
            s = MOMENT_SCALE[name]
        km, kv = _jax.random.split(_jax.random.fold_in(key, i + 1))
        out[name] = w
        out["m_" + name] = s * _jax.random.normal(km, w.shape, _jnp.float32)
        out["v_" + name] = (s * s) * _jax.random.uniform(kv, w.shape, _jnp.float32, 0.5, 1.5)
    if N_MICROBATCH > 1:
        for name, axis in PER_EXAMPLE_BATCH_AXIS.items():
            out[name] = _to_microbatches(out[name], axis)
    return {'x': out['x'], 'mem': out['mem'], 'ln_mix_g': out['ln_mix_g'], 'w_in_a': out['w_in_a'], 'b_f_a': out['b_f_a'], 'w_in_b': out['w_in_b'], 'ln_kv_g': out['ln_kv_g'], 'w_kv': out['w_kv'], 'ln_mem_g': out['ln_mem_g'], 'w_memkv': out['w_memkv'], 'w_out': out['w_out'], 'ln_ffn_g': out['ln_ffn_g'], 'w_up': out['w_up'], 'conv_w': out['conv_w'], 'conv_b': out['conv_b'], 'w_down': out['w_down'], 'final_g': out['final_g'], 'loss_target': out['loss_target'], 'm_ln_mix_g': out['m_ln_mix_g'], 'm_w_in_a': out['m_w_in_a'], 'm_b_f_a': out['m_b_f_a'], 'm_w_in_b': out['m_w_in_b'], 'm_ln_kv_g': out['m_ln_kv_g'], 'm_w_kv': out['m_w_kv'], 'm_ln_mem_g': out['m_ln_mem_g'], 'm_w_memkv': out['m_w_memkv'], 'm_w_out': out['m_w_out'], 'm_ln_ffn_g': out['m_ln_ffn_g'], 'm_w_up': out['m_w_up'], 'm_conv_w': out['m_conv_w'], 'm_conv_b': out['m_conv_b'], 'm_w_down': out['m_w_down'], 'm_final_g': out['m_final_g'], 'v_ln_mix_g': out['v_ln_mix_g'], 'v_w_in_a': out['v_w_in_a'], 'v_b_f_a': out['v_b_f_a'], 'v_w_in_b': out['v_w_in_b'], 'v_ln_kv_g': out['v_ln_kv_g'], 'v_w_kv': out['v_w_kv'], 'v_ln_mem_g': out['v_ln_mem_g'], 'v_w_memkv': out['v_w_memkv'], 'v_w_out': out['v_w_out'], 'v_ln_ffn_g': out['v_ln_ffn_g'], 'v_w_up': out['v_w_up'], 'v_conv_w': out['v_conv_w'], 'v_conv_b': out['v_conv_b'], 'v_w_down': out['v_w_down'], 'v_final_g': out['v_final_g']}


def _loss(weights, diff, rest, loss_target):
    with _jax.named_scope("forward"):
        args = {**rest, TWIN_DIFF_INPUT: diff, **{k: w.astype(_WEIGHT_DTYPES[k]) for k, w in weights.items()}}
        y = _forward(args)
    with _jax.named_scope("loss_head"):
        err = _jnp.square(y.astype(_jnp.float32) - loss_target)
        return 0.5 * _jnp.sum(_jnp.mean(err, axis=-1)) if err.ndim else 0.5 * err


def _adamw(w, g, m, v):
    m = ADAM_B1 * m + (1.0 - ADAM_B1) * g
    v = ADAM_B2 * v + (1.0 - ADAM_B2) * _jnp.square(g)
    m_hat = m / (1.0 - ADAM_B1 ** ADAM_STEP)
    v_hat = v / (1.0 - ADAM_B2 ** ADAM_STEP)
    delta = -ADAM_LR * (m_hat / (_jnp.sqrt(v_hat) + ADAM_EPS) + ADAM_WD * w)
    return delta, m, v


def reference(x, mem, ln_mix_g, w_in_a, b_f_a, w_in_b, ln_kv_g, w_kv, ln_mem_g, w_memkv, w_out, ln_ffn_g, w_up, conv_w, conv_b, w_down, final_g, loss_target, m_ln_mix_g, m_w_in_a, m_b_f_a, m_w_in_b, m_ln_kv_g, m_w_kv, m_ln_mem_g, m_w_memkv, m_w_out, m_ln_ffn_g, m_w_up, m_conv_w, m_conv_b, m_w_down, m_final_g, v_ln_mix_g, v_w_in_a, v_b_f_a, v_w_in_b, v_ln_kv_g, v_w_kv, v_ln_mem_g, v_w_memkv, v_w_out, v_ln_ffn_g, v_w_up, v_conv_w, v_conv_b, v_w_down, v_final_g):
    given = dict(x=x, mem=mem, ln_mix_g=ln_mix_g, w_in_a=w_in_a, b_f_a=b_f_a, w_in_b=w_in_b, ln_kv_g=ln_kv_g, w_kv=w_kv, ln_mem_g=ln_mem_g, w_memkv=w_memkv, w_out=w_out, ln_ffn_g=ln_ffn_g, w_up=w_up, conv_w=conv_w, conv_b=conv_b, w_down=w_down, final_g=final_g, loss_target=loss_target, m_ln_mix_g=m_ln_mix_g, m_w_in_a=m_w_in_a, m_b_f_a=m_b_f_a, m_w_in_b=m_w_in_b, m_ln_kv_g=m_ln_kv_g, m_w_kv=m_w_kv, m_ln_mem_g=m_ln_mem_g, m_w_memkv=m_w_memkv, m_w_out=m_w_out, m_ln_ffn_g=m_ln_ffn_g, m_w_up=m_w_up, m_conv_w=m_conv_w, m_conv_b=m_conv_b, m_w_down=m_w_down, m_final_g=m_final_g, v_ln_mix_g=v_ln_mix_g, v_w_in_a=v_w_in_a, v_b_f_a=v_b_f_a, v_w_in_b=v_w_in_b, v_ln_kv_g=v_ln_kv_g, v_w_kv=v_w_kv, v_ln_mem_g=v_ln_mem_g, v_w_memkv=v_w_memkv, v_w_out=v_w_out, v_ln_ffn_g=v_ln_ffn_g, v_w_up=v_w_up, v_conv_w=v_conv_w, v_conv_b=v_conv_b, v_w_down=v_w_down, v_final_g=v_final_g)
    weights = {n: given[n] for n in TWIN_WEIGHTS}
    shared = {n: given[n] for n in SHARED_INPUTS}
    per_example = {n: given[n] for n in ['x', 'mem']}
    grad_fn = _jax.value_and_grad(_loss, argnums=(0, 1))

    def one_microbatch(ex, loss_target):
        ex = dict(ex)
        diff = ex.pop(TWIN_DIFF_INPUT)
        return grad_fn(weights, diff, {**shared, **ex}, loss_target)

    if N_MICROBATCH == 1:
        loss, (grad_w, grad_x) = one_microbatch(per_example, given["loss_target"])
    else:
        def body(carry, xs):
            loss_sum, grad_sum = carry
            l_k, (gw_k, gx_k) = one_microbatch(xs[0], xs[1])
            with _jax.named_scope("update"):
                return (loss_sum + l_k, _jax.tree.map(_jnp.add, grad_sum, gw_k)), gx_k

        init = (_jnp.zeros((), _jnp.float32), _jax.tree.map(_jnp.zeros_like, weights))
        (loss, grad_w), grad_x = _jax.lax.scan(body, init, (per_example, given["loss_target"]))
    with _jax.named_scope("update"):
        delta_w, new_m, new_v = {}, {}, {}
        for n in TWIN_WEIGHTS:
            delta_w[n], new_m[n], new_v[n] = _adamw(weights[n], grad_w[n], given["m_" + n], given["v_" + n])
    return (loss, grad_x, *[grad_w[n] for n in TWIN_WEIGHTS], *[delta_w[n] for n in TWIN_WEIGHTS],
            *[new_m[n] for n in TWIN_WEIGHTS], *[new_v[n] for n in TWIN_WEIGHTS])
```

```python
import functools
import math

import jax
import jax.numpy as jnp
from jax import lax
from jax.experimental import pallas as pl
from jax.experimental.pallas import tpu as pltpu

F32 = jnp.float32
MXU_DTYPE = jnp.bfloat16
ACT_DTYPE = jnp.bfloat16

HEAD_DIM = 64
N_MAIN_HEADS = 12
N_MEM_HEADS = 4
MAIN_WIDTH = N_MAIN_HEADS * HEAD_DIM
MEM_WIDTH = N_MEM_HEADS * HEAD_DIM
EPS = 1e-6
SCALE = HEAD_DIM ** -0.5
NEG_BIG = -1e30
LANES = 128
PACK_COLS = 1024
N_CHIPS = 4

ADAM_LR = 0.001
ADAM_B1 = 0.9
ADAM_B2 = 0.999
ADAM_EPS = 1e-08
ADAM_WD = 0.01
ADAM_STEP = 10

MESH = pl.DeviceIdType.MESH
ANY = pl.BlockSpec(memory_space=pl.ANY)

PARAM_SHARD_AXIS = {
    'ln_mix_g': None, 'w_in_a': 2, 'b_f_a': None, 'w_in_b': 1, 'ln_kv_g': None, 'w_kv': 1,
    'ln_mem_g': None, 'w_memkv': 1, 'w_out': 1, 'ln_ffn_g': None, 'w_up': 2, 'conv_w': 2,
    'conv_b': None, 'w_down': 1, 'final_g': None,
}
PARAM_NAMES = list(PARAM_SHARD_AXIS)
F32_GATHERED = ('conv_w',)


def _tile(n, pref, unit=LANES):
    if n <= pref:
        return n
    best = None
    for t in range(unit, pref + 1, unit):
        if n % t == 0:
            best = t
    assert best is not None, (n, pref)
    return best


def _params(*sem, vmem_mb=None):
    kw = {}
    if sem:
        kw['dimension_semantics'] = sem
    if vmem_mb is not None:
        kw['vmem_limit_bytes'] = vmem_mb * 1024 * 1024
    return pltpu.CompilerParams(**kw)


def _dot(a, b, dims):
    return lax.dot_general(a.astype(MXU_DTYPE), b.astype(MXU_DTYPE), (dims, ((), ())),
                           preferred_element_type=F32)


NN = ((1,), (0,))
NT = ((1,), (1,))
TN = ((0,), (0,))


def _matmul(a, b, mode, out_dtype, res=None, name='mm'):
    if mode == 'nn':
        (M, K), (K2, N) = a.shape, b.shape
    elif mode == 'nt':
        (M, K), (N, K2) = a.shape, b.shape
    else:
        (K, M), (K2, N) = a.shape, b.shape
    assert K == K2, (a.shape, b.shape, mode)
    tm, tn, tk = _tile(M, 512), _tile(N, 512), _tile(K, 512)
    nk = K // tk
    dims = {'nn': NN, 'nt': NT, 'tn': TN}[mode]

    def body(*refs):
        if res is None:
            a_ref, b_ref, o_ref, acc = refs
        else:
            a_ref, b_ref, r_ref, o_ref, acc = refs
        k = pl.program_id(2)

        @pl.when(k == 0)
        def _():
            acc[...] = jnp.zeros_like(acc)

        acc[...] += _dot(a_ref[...], b_ref[...], dims)

        @pl.when(k == nk - 1)
        def _():
            out = acc[...]
            if res is not None:
                out = out + r_ref[...]
            o_ref[...] = out.astype(out_dtype)

    if mode == 'tn':
        a_spec = pl.BlockSpec((tk, tm), lambda i, j, k: (k, i))
    else:
        a_spec = pl.BlockSpec((tm, tk), lambda i, j, k: (i, k))
    if mode == 'nt':
        b_spec = pl.BlockSpec((tn, tk), lambda i, j, k: (j, k))
    else:
        b_spec = pl.BlockSpec((tk, tn), lambda i, j, k: (k, j))
    o_spec = pl.BlockSpec((tm, tn), lambda i, j, k: (i, j))
    in_specs = [a_spec, b_spec] + ([o_spec] if res is not None else [])
    args = (a, b) + ((res,) if res is not None else ())
    return pl.pallas_call(
        body, name=name, grid=(M // tm, N // tn, nk),
        in_specs=in_specs, out_specs=o_spec,
        out_shape=jax.ShapeDtypeStruct((M, N), out_dtype),
        scratch_shapes=[pltpu.VMEM((tm, tn), F32)],
        compiler_params=_params('parallel', 'parallel', 'arbitrary'),
    )(*args)


def _rms_fwd(x, g, name):
    T, D = x.shape
    tr = _tile(T, 512)

    def body(x_ref, g_ref, o_ref):
        xv = x_ref[...]
        r = lax.rsqrt(jnp.mean(xv * xv, axis=-1, keepdims=True) + EPS)
        o_ref[...] = (xv * r * g_ref[...]).astype(ACT_DTYPE)

    return pl.pallas_call(
        body, name=name, grid=(T // tr,),
        in_specs=[pl.BlockSpec((tr, D), lambda i: (i, 0)), pl.BlockSpec((1, D), lambda i: (0, 0))],
        out_specs=pl.BlockSpec((tr, D), lambda i: (i, 0)),
        out_shape=jax.ShapeDtypeStruct((T, D), ACT_DTYPE),
        compiler_params=_params('parallel'),
    )(x, g)


def _rms_bwd(x, g, dh, dres, name):
    T, D = x.shape
    tr = _tile(T, 512)
    want_dx = dres is not None

    def body(*refs):
        if want_dx:
            x_ref, g_ref, dh_ref, dres_ref, dx_ref, dg_ref = refs
        else:
            x_ref, g_ref, dh_ref, dg_ref = refs
        i = pl.program_id(0)

        @pl.when(i == 0)
        def _():
            dg_ref[...] = jnp.zeros_like(dg_ref)

        xv = x_ref[...]
        dhv = dh_ref[...].astype(F32)
        r = lax.rsqrt(jnp.mean(xv * xv, axis=-1, keepdims=True) + EPS)
        n = xv * r
        dg_ref[...] += jnp.sum(dhv * n, axis=0, keepdims=True)
        if want_dx:
            dn = dhv * g_ref[...]
            dx = r * (dn - n * jnp.mean(dn * n, axis=-1, keepdims=True))
            dx_ref[...] = dres_ref[...] + dx

    row = pl.BlockSpec((tr, D), lambda i: (i, 0))
    vec = pl.BlockSpec((1, D), lambda i: (0, 0))
    if want_dx:
        return pl.pallas_call(
            body, name=name, grid=(T // tr,),
            in_specs=[row, vec, row, row], out_specs=[row, vec],
            out_shape=[jax.ShapeDtypeStruct((T, D), F32), jax.ShapeDtypeStruct((1, D), F32)],
            compiler_params=_params('arbitrary'),
        )(x, g, dh, dres)
    dg = pl.pallas_call(
        body, name=name, grid=(T // tr,),
        in_specs=[row, vec, row], out_specs=vec,
        out_shape=jax.ShapeDtypeStruct((1, D), F32),
        compiler_params=_params('arbitrary'),
    )(x, g, dh)
    return None, dg


def _final_loss(x, g, tgt, name='final_loss'):
    T, D = x.shape
    tr = _tile(T, 512)

    def body(x_ref, g_ref, t_ref, loss_ref, dx_ref, dg_ref):
        i = pl.program_id(0)

        @pl.when(i == 0)
        def _():
            loss_ref[...] = jnp.zeros_like(loss_ref)
            dg_ref[...] = jnp.zeros_like(dg_ref)

        xv = x_ref[...]
        gv = g_ref[...]
        r = lax.rsqrt(jnp.mean(xv * xv, axis=-1, keepdims=True) + EPS)
        n = xv * r
        e = n * gv - t_ref[...]
        per_tok = jnp.mean(e * e, axis=-1, keepdims=True)
        loss_ref[...] += 0.5 * jnp.sum(per_tok, axis=0, keepdims=True)
        dy = e * (1.0 / D)
        dg_ref[...] += jnp.sum(dy * n, axis=0, keepdims=True)
        dn = dy * gv
        dx_ref[...] = r * (dn - n * jnp.mean(dn * n, axis=-1, keepdims=True))

    row = pl.BlockSpec((tr, D), lambda i: (i, 0))
    vec = pl.BlockSpec((1, D), lambda i: (0, 0))
    one = pl.BlockSpec((1, 1), lambda i: (0, 0))
    return pl.pallas_call(
        body, name=name, grid=(T // tr,),
        in_specs=[row, vec, row], out_specs=[one, row, vec],
        out_shape=[jax.ShapeDtypeStruct((1, 1), F32), jax.ShapeDtypeStruct((T, D), F32),
                   jax.ShapeDtypeStruct((1, D), F32)],
        compiler_params=_params('arbitrary'),
    )(x, g, tgt)


def _log_sigmoid(z):
    return jnp.minimum(z, 0.0) - jnp.log(1.0 + jnp.exp(-jnp.abs(z)))


def _tri(n, rel):
    j = lax.broadcasted_iota(jnp.int32, (n, n), 0)
    s = lax.broadcasted_iota(jnp.int32, (n, n), 1)
    return rel(j, s).astype(MXU_DTYPE)


def _split_dot(x, tri, terms):
    if MXU_DTYPE == F32:
        return jnp.dot(x, tri, preferred_element_type=F32)
    out = None
    rem = x
    for _ in range(terms):
        piece = rem.astype(MXU_DTYPE)
        part = jnp.dot(piece, tri, preferred_element_type=F32)
        out = part if out is None else out + part
        rem = rem - piece.astype(F32)
    return out


def _gate_fwd(zt, bcol, name='gate_fwd'):
    BH, S = zt.shape
    nb = S // LANES

    def body(z_ref, b_ref, c_ref):
        tri = _tri(LANES, lambda j, s: j <= s)
        carry = jnp.zeros((BH, 1), F32)
        for i in range(nb):
            sl = slice(i * LANES, (i + 1) * LANES)
            logf = _log_sigmoid(z_ref[:, sl] + b_ref[...])
            cs = _split_dot(logf, tri, 3) + carry
            c_ref[:, sl] = cs
            carry = cs[:, LANES - 1:LANES]

    return pl.pallas_call(body, name=name, out_shape=jax.ShapeDtypeStruct((BH, S), F32))(zt, bcol)


def _gate_bwd(zt, bcol, dc, name='gate_bwd'):
    BH, S = zt.shape
    nb = S // LANES

    def body(z_ref, b_ref, dc_ref, dz_ref, db_ref):
        tri = _tri(LANES, lambda j, s: j >= s)
        carry = jnp.zeros((BH, 1), F32)
        dsum = jnp.zeros((BH, 1), F32)
        for i in reversed(range(nb)):
            sl = slice(i * LANES, (i + 1) * LANES)
            rs = _split_dot(dc_ref[:, sl], tri, 3) + carry
            carry = rs[:, 0:1]
            z = z_ref[:, sl] + b_ref[...]
            dz = rs * (1.0 - 1.0 / (1.0 + jnp.exp(-z)))
            dz_ref[:, sl] = dz
            dsum = dsum + jnp.sum(dz, axis=-1, keepdims=True)
        db_ref[...] = dsum

    return pl.pallas_call(
        body, name=name,
        out_shape=[jax.ShapeDtypeStruct((BH, S), F32), jax.ShapeDtypeStruct((BH, 1), F32)],
    )(zt, bcol, dc)


FOX_BLOCK = 256


def _fox_specs(S, bq):
    nk = S // bq
    qs = pl.BlockSpec((None, bq, HEAD_DIM), lambda b, i: (b, i, 0))
    kv = pl.BlockSpec((None, S, HEAD_DIM), lambda b, i: (b, 0, 0))
    col = pl.BlockSpec((None, bq, 1), lambda b, i: (b, i, 0))
    rowv = pl.BlockSpec((None, nk, 1, bq), lambda b, i: (b, 0, 0, 0))
    return qs, kv, col, rowv


def _fox_logits(q, ks, cc, crow, i, kb, bq):
    s = _dot(q, ks, NT) * SCALE + cc - crow
    row = i * bq + lax.broadcasted_iota(jnp.int32, (bq, bq), 0)
    col = kb * bq + lax.broadcasted_iota(jnp.int32, (bq, bq), 1)
    return s, col <= row


def _fox_fwd(q, k, v, ccol, crow, name='fox_fwd'):
    BH, S, _ = q.shape
    bq = min(FOX_BLOCK, S)

    def body(q_ref, k_ref, v_ref, cc_ref, cr_ref, o_ref, lse_ref):
        i = pl.program_id(1)
        qv = q_ref[...]
        cc = cc_ref[...]

        def step(kb, carry):
            m, l, acc = carry
            sl = pl.ds(pl.multiple_of(kb * bq, bq), bq)
            s, mask = _fox_logits(qv, k_ref[sl, :], cc, cr_ref[kb], i, kb, bq)
            s = jnp.where(mask, s, NEG_BIG)
            m_new = jnp.maximum(m, jnp.max(s, axis=-1, keepdims=True))
            p = jnp.exp(s - m_new)
            alpha = jnp.exp(m - m_new)
            l = alpha * l + jnp.sum(p, axis=-1, keepdims=True)
            acc = alpha * acc + _dot(p, v_ref[sl, :], NN)
            return m_new, l, acc

        init = (jnp.full((bq, 1), NEG_BIG, F32), jnp.zeros((bq, 1), F32), jnp.zeros((bq, HEAD_DIM), F32))
        m, l, acc = lax.fori_loop(0, i + 1, step, init)
        o_ref[...] = (acc / l).astype(ACT_DTYPE)
        lse_ref[...] = m + jnp.log(l)

    qs, kv, col, rowv = _fox_specs(S, bq)
    return pl.pallas_call(
        body, name=name, grid=(BH, S // bq),
        in_specs=[qs, kv, kv, col, rowv], out_specs=[qs, col],
        out_shape=[jax.ShapeDtypeStruct((BH, S, HEAD_DIM), ACT_DTYPE), jax.ShapeDtypeStruct((BH, S, 1), F32)],
        compiler_params=_params('parallel', 'arbitrary'),
    )(q, k, v, ccol, crow)


def _fox_bwd(q, k, v, ccol, crow, o, lse, do, name='fox_bwd'):
    BH, S, _ = q.shape
    bq = min(FOX_BLOCK, S)

    def body(q_ref, k_ref, v_ref, cc_ref, cr_ref, o_ref, lse_ref, do_ref, dq_ref, dk_ref, dv_ref, dcc_ref, dcr_ref):
        i = pl.program_id(1)

        @pl.when(i == 0)
        def _():
            dk_ref[...] = jnp.zeros_like(dk_ref)
            dv_ref[...] = jnp.zeros_like(dv_ref)
            dcr_ref[...] = jnp.zeros_like(dcr_ref)

        qv = q_ref[...]
        dov = do_ref[...]
        cc = cc_ref[...]
        lse = lse_ref[...]
        dsum = jnp.sum(dov.astype(F32) * o_ref[...].astype(F32), axis=-1, keepdims=True)

        def step(kb, carry):
            dq, dcc = carry
            sl = pl.ds(pl.multiple_of(kb * bq, bq), bq)
            ks = k_ref[sl, :]
            s, mask = _fox_logits(qv, ks, cc, cr_ref[kb], i, kb, bq)
            p = jnp.where(mask, jnp.exp(s - lse), 0.0)
            dp = _dot(dov, v_ref[sl, :], NT)
            ds = p * (dp - dsum)
            dq = dq + _dot(ds, ks, NN)
            dk_ref[sl, :] += SCALE * _dot(ds, qv, TN)
            dv_ref[sl, :] += _dot(p, dov, TN)
            dcr_ref[kb] -= jnp.sum(ds, axis=0, keepdims=True)
            return dq, dcc + jnp.sum(ds, axis=-1, keepdims=True)

        dq, dcc = lax.fori_loop(0, i + 1, step, (jnp.zeros((bq, HEAD_DIM), F32), jnp.zeros((bq, 1), F32)))
        dq_ref[...] = dq * SCALE
        dcc_ref[...] = dcc

    qs, kv, col, rowv = _fox_specs(S, bq)
    full = jax.ShapeDtypeStruct((BH, S, HEAD_DIM), F32)
    return pl.pallas_call(
        body, name=name, grid=(BH, S // bq),
        in_specs=[qs, kv, kv, col, rowv, qs, col, qs], out_specs=[qs, kv, kv, col, rowv],
        out_shape=[full, full, full, jax.ShapeDtypeStruct(ccol.shape, F32), jax.ShapeDtypeStruct(crow.shape, F32)],
        compiler_params=_params('parallel', 'arbitrary'),
    )(q, k, v, ccol, crow, o, lse, do)


SB_BLOCK = 128


def _sb_block(qv, ks, i, kb, bq):
    z = _dot(qv, ks, NT) * SCALE
    row = i * bq + lax.broadcasted_iota(jnp.int32, (bq, bq), 0)
    col = kb * bq + lax.broadcasted_iota(jnp.int32, (bq, bq), 1)
    mask = col < row
    a = _log_sigmoid(z)
    l = jnp.where(mask, a - z, 0.0)
    return mask, a, l


def _sb_fwd(q, k, v, name='sb_fwd'):
    BH, S, _ = q.shape
    bq = min(SB_BLOCK, S)

    def body(q_ref, k_ref, v_ref, o_ref):
        i = pl.program_id(1)
        qv = q_ref[...]
        tri = _tri(bq, lambda j, s: j > s)

        def step(n, carry):
            acc, right = carry
            kb = i - n
            sl = pl.ds(pl.multiple_of(kb * bq, bq), bq)
            mask, a, l = _sb_block(qv, k_ref[sl, :], i, kb, bq)
            rsum = _split_dot(l, tri, 2) + right
            w = jnp.where(mask, jnp.exp(a + rsum), 0.0)
            acc = acc + _dot(w, v_ref[sl, :], NN)
            return acc, right + jnp.sum(l, axis=-1, keepdims=True)

        acc, _ = lax.fori_loop(0, i + 1, step, (jnp.zeros((bq, HEAD_DIM), F32), jnp.zeros((bq, 1), F32)))
        o_ref[...] = acc.astype(ACT_DTYPE)

    qs, kv, _, _ = _fox_specs(S, bq)
    return pl.pallas_call(
        body, name=name, grid=(BH, S // bq),
        in_specs=[qs, kv, kv], out_specs=qs,
        out_shape=jax.ShapeDtypeStruct((BH, S, HEAD_DIM), ACT_DTYPE),
        compiler_params=_params('parallel', 'arbitrary'),
    )(q, k, v)


def _sb_bwd(q, k, v, do, name='sb_bwd'):
    BH, S, _ = q.shape
    bq = min(SB_BLOCK, S)

    def body(q_ref, k_ref, v_ref, do_ref, dq_ref, dk_ref, dv_ref):
        i = pl.program_id(1)

        @pl.when(i == 0)
        def _():
            dk_ref[...] = jnp.zeros_like(dk_ref)
            dv_ref[...] = jnp.zeros_like(dv_ref)

        qv = q_ref[...]
        dov = do_ref[...]
        tri_incl = _tri(bq, lambda j, s: j <= s)
        tri_excl = _tri(bq, lambda j, s: j < s)

        def total(kb, tot):
            sl = pl.ds(pl.multiple_of(kb * bq, bq), bq)
            _, _, l = _sb_block(qv, k_ref[sl, :], i, kb, bq)
            return tot + jnp.sum(l, axis=-1, keepdims=True)

        tot = lax.fori_loop(0, i + 1, total, jnp.zeros((bq, 1), F32))

        def step(kb, carry):
            dq, left_l, left_g = carry
            sl = pl.ds(pl.multiple_of(kb * bq, bq), bq)
            ks = k_ref[sl, :]
            mask, a, l = _sb_block(qv, ks, i, kb, bq)
            cum = _split_dot(l, tri_incl, 2) + left_l
            w = jnp.where(mask, jnp.exp(a + tot - cum), 0.0)
            g = w * _dot(dov, v_ref[sl, :], NT)
            h = _split_dot(g, tri_excl, 2) + left_g
            beta = jnp.exp(a)
            dz = jnp.where(mask, g * (1.0 - beta) - h * beta, 0.0)
            dq = dq + _dot(dz, ks, NN)
            dk_ref[sl, :] += SCALE * _dot(dz, qv, TN)
            dv_ref[sl, :] += _dot(w, dov, TN)
            return dq, left_l + jnp.sum(l, axis=-1, keepdims=True), left_g + jnp.sum(g, axis=-1, keepdims=True)

        zero = jnp.zeros((bq, 1), F32)
        dq, _, _ = lax.fori_loop(0, i + 1, step, (jnp.zeros((bq, HEAD_DIM), F32), zero, zero))
        dq_ref[...] = dq * SCALE

    qs, kv, _, _ = _fox_specs(S, bq)
    full = jax.ShapeDtypeStruct((BH, S, HEAD_DIM), F32)
    return pl.pallas_call(
        body, name=name, grid=(BH, S // bq),
        in_specs=[qs, kv, kv, qs], out_specs=[qs, kv, kv],
        out_shape=[full, full, full],
        compiler_params=_params('parallel', 'arbitrary'),
    )(q, k, v, do)


def _mem_specs(S, M, bq):
    qs = pl.BlockSpec((None, bq, HEAD_DIM), lambda b, i: (b, i, 0))
    ms = pl.BlockSpec((None, M, HEAD_DIM), lambda b, i: (b, 0, 0))
    return qs, ms


def _mem_probs(qv, mk):
    s = _dot(qv, mk, NT) * SCALE
    p = jnp.exp(s - jnp.max(s, axis=-1, keepdims=True))
    return p / jnp.sum(p, axis=-1, keepdims=True)


def _mem_fwd(q, mk, mv, name='mem_fwd'):
    BH, S, _ = q.shape
    M = mk.shape[1]
    bq = _tile(S, 512)

    def body(q_ref, mk_ref, mv_ref, o_ref):
        p = _mem_probs(q_ref[...], mk_ref[...])
        o_ref[...] = _dot(p, mv_ref[...], NN).astype(ACT_DTYPE)

    qs, ms = _mem_specs(S, M, bq)
    return pl.pallas_call(
        body, name=name, grid=(BH, S // bq),
        in_specs=[qs, ms, ms], out_specs=qs,
        out_shape=jax.ShapeDtypeStruct((BH, S, HEAD_DIM), ACT_DTYPE),
        compiler_params=_params('parallel', 'parallel'),
    )(q, mk, mv)


def _mem_bwd(q, mk, mv, do, name='mem_bwd'):
    BH, S, _ = q.shape
    M = mk.shape[1]
    bq = _tile(S, 512)

    def body(q_ref, mk_ref, mv_ref, do_ref, dq_ref, dmk_ref, dmv_ref):
        i = pl.program_id(1)

        @pl.when(i == 0)
        def _():
            dmk_ref[...] = jnp.zeros_like(dmk_ref)
            dmv_ref[...] = jnp.zeros_like(dmv_ref)

        qv = q_ref[...]
        mkv = mk_ref[...]
        dov = do_ref[...]
        p = _mem_probs(qv, mkv)
        dp = _dot(dov, mv_ref[...], NT)
        ds = p * (dp - jnp.sum(p * dp, axis=-1, keepdims=True))
        dq_ref[...] = SCALE * _dot(ds, mkv, NN)
        dmk_ref[...] += SCALE * _dot(ds, qv, TN)
        dmv_ref[...] += _dot(p, dov, TN)

    qs, ms = _mem_specs(S, M, bq)
    return pl.pallas_call(
        body, name=name, grid=(BH, S // bq),
        in_specs=[qs, ms, ms, qs], out_specs=[qs, ms, ms],
        out_shape=[jax.ShapeDtypeStruct((BH, S, HEAD_DIM), F32), jax.ShapeDtypeStruct((BH, M, HEAD_DIM), F32),
                   jax.ShapeDtypeStruct((BH, M, HEAD_DIM), F32)],
        compiler_params=_params('parallel', 'arbitrary'),
    )(q, mk, mv, do)


def _shift_down(u, n):
    t = lax.broadcasted_iota(jnp.int32, u.shape, 0)
    return jnp.where(t >= n, pltpu.roll(u, n, 0), 0.0)


def _shift_up(u, n):
    S = u.shape[0]
    t = lax.broadcasted_iota(jnp.int32, u.shape, 0)
    return jnp.where(t < S - n, pltpu.roll(u, S - n, 0), 0.0)


def _conv(u, u1, u2, w, b):
    return b + w[0:1, :] * u2 + w[1:2, :] * u1 + w[2:3, :] * u


def _conv_specs(S, nf):
    ug = pl.BlockSpec((None, S, LANES), lambda b, j: (b, 0, j))
    uv = pl.BlockSpec((None, S, LANES), lambda b, j: (b, 0, j + nf))
    wg = pl.BlockSpec((3, LANES), lambda b, j: (0, j))
    wv = pl.BlockSpec((3, LANES), lambda b, j: (0, j + nf))
    bg = pl.BlockSpec((1, LANES), lambda b, j: (0, j))
    bv = pl.BlockSpec((1, LANES), lambda b, j: (0, j + nf))
    return ug, uv, wg, wv, bg, bv


def _conv_fwd(u, cw, cb, name='conv_fwd'):
    B, S, F2 = u.shape
    F = F2 // 2
    nf = F // LANES

    def body(ug_ref, uv_ref, wg_ref, wv_ref, bg_ref, bv_ref, y_ref):
        ug = ug_ref[...].astype(F32)
        uv = uv_ref[...].astype(F32)
        gate = _conv(ug, _shift_down(ug, 1), _shift_down(ug, 2), wg_ref[...], bg_ref[...])
        val = _conv(uv, _shift_down(uv, 1), _shift_down(uv, 2), wv_ref[...], bv_ref[...])
        y_ref[...] = (gate / (1.0 + jnp.exp(-gate)) * val).astype(ACT_DTYPE)

    specs = _conv_specs(S, nf)
    return pl.pallas_call(
        body, name=name, grid=(B, nf), in_specs=list(specs), out_specs=specs[0],
        out_shape=jax.ShapeDtypeStruct((B, S, F), ACT_DTYPE),
        compiler_params=_params('parallel', 'parallel'),
    )(u, u, cw, cw, cb, cb)


def _conv_bwd(u, cw, cb, dy, name='conv_bwd'):
    B, S, F2 = u.shape
    F = F2 // 2
    nf = F // LANES

    def body(ug_ref, uv_ref, wg_ref, wv_ref, bg_ref, bv_ref, dy_ref,
             dug_ref, duv_ref, dwg_ref, dwv_ref, dbg_ref, dbv_ref):
        b = pl.program_id(1)

        @pl.when(b == 0)
        def _():
            for r in (dwg_ref, dwv_ref, dbg_ref, dbv_ref):
                r[...] = jnp.zeros_like(r)

        ug = ug_ref[...].astype(F32)
        uv = uv_ref[...].astype(F32)
        ug1, ug2 = _shift_down(ug, 1), _shift_down(ug, 2)
        uv1, uv2 = _shift_down(uv, 1), _shift_down(uv, 2)
        wg, wv = wg_ref[...], wv_ref[...]
        gate = _conv(ug, ug1, ug2, wg, bg_ref[...])
        val = _conv(uv, uv1, uv2, wv, bv_ref[...])
        dyv = dy_ref[...].astype(F32)
        sg = 1.0 / (1.0 + jnp.exp(-gate))
        dval = dyv * (gate * sg)
        dgate = dyv * val * (sg * (1.0 + gate * (1.0 - sg)))

        def back(d, x, x1, x2, w, du_ref, dw_ref, db_ref):
            db_ref[...] += jnp.sum(d, axis=0, keepdims=True)
            dw_ref[...] += jnp.concatenate(
                [jnp.sum(d * x2, axis=0, keepdims=True), jnp.sum(d * x1, axis=0, keepdims=True),
                 jnp.sum(d * x, axis=0, keepdims=True)], axis=0)
            du = w[2:3, :] * d + w[1:2, :] * _shift_up(d, 1) + w[0:1, :] * _shift_up(d, 2)
            du_ref[...] = du.astype(ACT_DTYPE)

        back(dgate, ug, ug1, ug2, wg, dug_ref, dwg_ref, dbg_ref)
        back(dval, uv, uv1, uv2, wv, duv_ref, dwv_ref, dbv_ref)

    def swap(spec_fn):
        return lambda j, b: spec_fn(b, j)

    ug, uv, wg, wv, bg, bv = _conv_specs(S, nf)
    ins = [pl.BlockSpec(s.block_shape, swap(s.index_map)) for s in (ug, uv, wg, wv, bg, bv, ug)]
    outs = [ins[0], ins[0], ins[2], ins[2], ins[4], ins[4]]
    return pl.pallas_call(
        body, name=name, grid=(nf, B), in_specs=ins, out_specs=outs,
        out_shape=[jax.ShapeDtypeStruct((B, S, F), ACT_DTYPE), jax.ShapeDtypeStruct((B, S, F), ACT_DTYPE),
                   jax.ShapeDtypeStruct((3, F), F32), jax.ShapeDtypeStruct((3, F), F32),
                   jax.ShapeDtypeStruct((1, F), F32), jax.ShapeDtypeStruct((1, F), F32)],
        compiler_params=_params('parallel', 'arbitrary'),
    )(u, u, cw, cw, cb, cb, dy)


def _adamw(w, g, m, v, name):
    R, C = w.shape
    tr = R
    if R * C * 4 > (1 << 19) and R % 8 == 0:
        tr = 8
        for t in range(8, R + 1, 8):
            if R % t == 0 and t * C * 4 <= (1 << 19):
                tr = t

    def body(w_ref, g_ref, m_ref, v_ref, d_ref, nm_ref, nv_ref):
        gv = g_ref[...]
        nm = ADAM_B1 * m_ref[...] + (1.0 - ADAM_B1) * gv
        nv = ADAM_B2 * v_ref[...] + (1.0 - ADAM_B2) * (gv * gv)
        m_hat = nm / (1.0 - ADAM_B1 ** ADAM_STEP)
        v_hat = nv / (1.0 - ADAM_B2 ** ADAM_STEP)
        d_ref[...] = -ADAM_LR * (m_hat / (jnp.sqrt(v_hat) + ADAM_EPS) + ADAM_WD * w_ref[...])
        nm_ref[...] = nm
        nv_ref[...] = nv

    blk = pl.BlockSpec((tr, C), lambda i: (i, 0))
    shp = jax.ShapeDtypeStruct((R, C), F32)
    return pl.pallas_call(
        body, name=name, grid=(R // tr,), in_specs=[blk] * 4, out_specs=[blk] * 3, out_shape=[shp] * 3,
        compiler_params=_params('parallel'),
    )(w, g, m, v)


def _my_place():
    return lax.axis_index('x'), lax.axis_index('y'), lax.axis_index('c')


def _other_chips(x, y):
    return [(1 - x, y), (x, 1 - y), (1 - x, 1 - y)]


def _remote(src, dst, send_sem, recv_sem, to):
    return pltpu.make_async_remote_copy(src_ref=src, dst_ref=dst, send_sem=send_sem, recv_sem=recv_sem,
                                        device_id=to, device_id_type=MESH)


def _gather_chips(p, name='gather_chips'):
    R, C = p.shape
    Rh = R // 2
    assert R % 2 == 0

    def body(p_ref, o_ref, send_sems, recv_sems, local_sem):
        x, y, c = _my_place()
        chip = 2 * x + y
        sibling = (x, y, 1 - c)
        others = _other_chips(x, y)

        def rows(k, half):
            return o_ref.at[k, pl.ds(half * Rh, Rh), :]

        mine = pltpu.make_async_copy(p_ref, o_ref.at[chip], local_sem)
        mine.start()
        first = []
        for j, (ox, oy) in enumerate(others):
            cp = _remote(p_ref.at[pl.ds(c * Rh, Rh), :], rows(chip, c), send_sems.at[j], recv_sems.at[j], (ox, oy, c))
            cp.start()
            first.append(cp)
        passed = []
        for j, (ox, oy) in enumerate(others):
            k = 2 * ox + oy
            _remote(rows(k, c), rows(k, c), send_sems.at[j], recv_sems.at[j], (ox, oy, c)).wait_recv()
            fw = _remote(rows(k, c), rows(k, c), send_sems.at[3 + j], recv_sems.at[3 + j], sibling)
            fw.start()
            passed.append(fw)
        for j, (ox, oy) in enumerate(others):
            k = 2 * ox + oy
            _remote(rows(k, 1 - c), rows(k, 1 - c), send_sems.at[3 + j], recv_sems.at[3 + j], sibling).wait_recv()
        for cp in first + passed:
            cp.wait_send()
        mine.wait()

    return pl.pallas_call(
        body, name=name, in_specs=[ANY], out_specs=ANY,
        out_shape=jax.ShapeDtypeStruct((N_CHIPS, R, C), p.dtype),
        scratch_shapes=[pltpu.SemaphoreType.DMA((6,)), pltpu.SemaphoreType.DMA((6,)), pltpu.SemaphoreType.DMA],
    )(p)


def _swap_halves(g, name='swap_halves'):
    n, R, C = g.shape
    Rh = R // 2

    def body(g_ref, o_ref, send_sem, recv_sem):
        x, y, c = _my_place()
        cp = _remote(g_ref.at[:, pl.ds((1 - c) * Rh, Rh), :], o_ref, send_sem, recv_sem, (x, y, 1 - c))
        cp.start()
        cp.wait()

    return pl.pallas_call(
        body, name=name, in_specs=[ANY], out_specs=ANY,
        out_shape=jax.ShapeDtypeStruct((n, Rh, C), g.dtype),
        scratch_shapes=[pltpu.SemaphoreType.DMA, pltpu.SemaphoreType.DMA],
    )(g)


def _add_half(g, other, c_arr, name='add_half'):
    n, R, C = g.shape
    Rh = R // 2
    tr = _tile(Rh, 512, 8)
    nrt = Rh // tr

    def body(c_ref, g_ref, o_ref, q_ref):
        q_ref[...] = g_ref[...] + o_ref[...]

    gs = pltpu.PrefetchScalarGridSpec(
        num_scalar_prefetch=1, grid=(n, nrt),
        in_specs=[pl.BlockSpec((None, tr, C), lambda j, r, c_ref: (j, c_ref[0] * nrt + r, 0)),
                  pl.BlockSpec((None, tr, C), lambda j, r, c_ref: (j, r, 0))],
        out_specs=pl.BlockSpec((None, tr, C), lambda j, r, c_ref: (j, r, 0)))
    return pl.pallas_call(
        body, name=name, grid_spec=gs, out_shape=jax.ShapeDtypeStruct((n, Rh, C), F32),
        compiler_params=_params('parallel', 'parallel'),
    )(c_arr, g, other)


def _scatter_chips(q, name='scatter_chips'):
    n, Rh, C = q.shape

    def body(q_ref, o_ref, send_sems, recv_sems):
        x, y, c = _my_place()
        cps = []
        for j, (ox, oy) in enumerate(_other_chips(x, y)):
            cp = _remote(q_ref.at[2 * ox + oy], o_ref.at[j], send_sems.at[j], recv_sems.at[j], (ox, oy, c))
            cp.start()
            cps.append(cp)
        for cp in cps:
            cp.wait_recv()
        for cp in cps:
            cp.wait_send()

    return pl.pallas_call(
        body, name=name, in_specs=[ANY], out_specs=ANY,
        out_shape=jax.ShapeDtypeStruct((3, Rh, C), q.dtype),
        scratch_shapes=[pltpu.SemaphoreType.DMA((3,)), pltpu.SemaphoreType.DMA((3,))],
    )(q)


def _add_chips(q, got, chip_arr, name='add_chips'):
    n, Rh, C = q.shape
    tr = _tile(Rh, 512, 8)

    def body(k_ref, q_ref, gx_ref, gy_ref, gxy_ref, o_ref):
        o_ref[...] = (q_ref[...] + gxy_ref[...]) + (gx_ref[...] + gy_ref[...])

    def got_spec(j):
        return pl.BlockSpec((None, tr, C), lambda r, k_ref: (j, r, 0))

    gs = pltpu.PrefetchScalarGridSpec(
        num_scalar_prefetch=1, grid=(Rh // tr,),
        in_specs=[pl.BlockSpec((None, tr, C), lambda r, k_ref: (k_ref[0], r, 0)), got_spec(0), got_spec(1), got_spec(2)],
        out_specs=pl.BlockSpec((tr, C), lambda r, k_ref: (r, 0)))
    return pl.pallas_call(
        body, name=name, grid_spec=gs, out_shape=jax.ShapeDtypeStruct((Rh, C), F32),
        compiler_params=_params('parallel'),
    )(chip_arr, q, got, got, got)


def _join_halves(r, name='join_halves'):
    Rh, C = r.shape

    def body(r_ref, o_ref, send_sem, recv_sem, local_sem):
        x, y, c = _my_place()
        mine = pltpu.make_async_copy(r_ref, o_ref.at[c], local_sem)
        mine.start()
        cp = _remote(r_ref, o_ref.at[c], send_sem, recv_sem, (x, y, 1 - c))
        cp.start()
        cp.wait()
        mine.wait()

    return pl.pallas_call(
        body, name=name, in_specs=[ANY], out_specs=ANY,
        out_shape=jax.ShapeDtypeStruct((2, Rh, C), r.dtype),
        scratch_shapes=[pltpu.SemaphoreType.DMA, pltpu.SemaphoreType.DMA, pltpu.SemaphoreType.DMA],
    )(r)


def _pad_rows(flat, mult):
    n = flat.shape[0]
    rows = -(-n // PACK_COLS)
    rows = -(-rows // mult) * mult
    return jnp.pad(flat, (0, rows * PACK_COLS - n)).reshape(rows, PACK_COLS)


def _pack_weights(shards):
    parts = []
    for name in PARAM_NAMES:
        if PARAM_SHARD_AXIS[name] is None:
            continue
        w = shards[name]
        if name in F32_GATHERED:
            parts.append(lax.bitcast_convert_type(w, jnp.bfloat16).reshape(-1))
        else:
            parts.append(w.astype(jnp.bfloat16).reshape(-1))
    return _pad_rows(jnp.concatenate(parts), 32)


def _unpack_weights(gathered, shards):
    flat = gathered.reshape(N_CHIPS, -1)
    full = {}
    off = 0
    for name in PARAM_NAMES:
        axis = PARAM_SHARD_AXIS[name]
        if axis is None:
            continue
        shape = shards[name].shape
        n = math.prod(shape)
        if name in F32_GATHERED:
            pieces = [lax.bitcast_convert_type(flat[k, off:off + 2 * n].reshape(shape + (2,)), F32)
                      for k in range(N_CHIPS)]
            off += 2 * n
        else:
            pieces = [flat[k, off:off + n].reshape(shape) for k in range(N_CHIPS)]
            off += n
        full[name] = jnp.concatenate(pieces, axis=axis)
    return full


def _pack_grads(grads, shards):
    slots = []
    for k in range(N_CHIPS):
        parts = []
        for name in PARAM_NAMES:
            axis = PARAM_SHARD_AXIS[name]
            g = grads[name]
            if axis is not None:
                size = shards[name].shape[axis]
                g = lax.slice_in_dim(g, k * size, (k + 1) * size, axis=axis)
            parts.append(g.astype(F32).reshape(-1))
        slots.append(_pad_rows(jnp.concatenate(parts), 32))
    return jnp.stack(slots)


def _unpack_grads(packed, shards):
    flat = packed.reshape(-1)
    out = {}
    off = 0
    for name in PARAM_NAMES:
        shape = shards[name].shape
        n = math.prod(shape)
        out[name] = flat[off:off + n].reshape(shape)
        off += n
    return out


def _heads(t, B, S, n):
    return t.reshape(B, S, n, HEAD_DIM).transpose(0, 2, 1, 3).reshape(B * n, S, HEAD_DIM)


def _unheads(t, B, S, n):
    return t.reshape(B, n, S, HEAD_DIM).transpose(0, 2, 1, 3).reshape(B * S, n * HEAD_DIM)


def _mem_kv_fwd(mem2, g, w, B, tag):
    hm = _rms_fwd(mem2, g, name=f'rms_mem_{tag}')
    mkv = _matmul(hm, w, 'nn', ACT_DTYPE, name=f'mm_memkv_{tag}')
    M = mem2.shape[0] // B
    mk = _heads(mkv[:, :MEM_WIDTH], B, M, N_MEM_HEADS)
    mv = _heads(mkv[:, MEM_WIDTH:], B, M, N_MEM_HEADS)
    return hm, mk, mv


def _mem_kv_bwd(mem2, g, w, hm, dmk, dmv, B, tag):
    M = mem2.shape[0] // B
    dmkv = jnp.concatenate([_unheads(dmk, B, M, N_MEM_HEADS), _unheads(dmv, B, M, N_MEM_HEADS)], axis=1)
    dw = _matmul(hm, dmkv, 'tn', F32, name=f'mm_memkv_dw_{tag}')
    dhm = _matmul(dmkv, w, 'nt', F32, name=f'mm_memkv_dx_{tag}')
    _, dg = _rms_bwd(mem2, g, dhm, None, name=f'rms_mem_bwd_{tag}')
    return dw, dg


def _ffn_fwd(x, g, w_up, cw, cb, w_down, B, S, tag):
    T = x.shape[0]
    h2 = _rms_fwd(x, g, name=f'rms_ffn_{tag}')
    u = _matmul(h2, w_up, 'nn', ACT_DTYPE, name=f'mm_up_{tag}')
    y = _conv_fwd(u.reshape(B, S, -1), cw, cb, name=f'conv_fwd_{tag}').reshape(T, -1)
    x2 = _matmul(y, w_down, 'nn', F32, res=x, name=f'mm_down_{tag}')
    return x2, (h2, u, y)


def _ffn_bwd(dx2, x, g, w_up, cw, cb, w_down, saved, B, S, tag):
    h2, u, y = saved
    T = x.shape[0]
    dy = _matmul(dx2, w_down, 'nt', ACT_DTYPE, name=f'mm_down_dx_{tag}')
    dw_down = _matmul(y, dx2, 'tn', F32, name=f'mm_down_dw_{tag}')
    dug, duv, dcwg, dcwv, dcbg, dcbv = _conv_bwd(u.reshape(B, S, -1), cw, cb, dy.reshape(B, S, -1),
                                                  name=f'conv_bwd_{tag}')
    du = jnp.concatenate([dug.reshape(T, -1), duv.reshape(T, -1)], axis=1)
    dh2 = _matmul(du, w_up, 'nt', F32, name=f'mm_up_dx_{tag}')
    dw_up = _matmul(h2, du, 'tn', F32, name=f'mm_up_dw_{tag}')
    dx, dg = _rms_bwd(x, g, dh2, dx2, name=f'rms_ffn_bwd_{tag}')
    dcw = jnp.concatenate([dcwg, dcwv], axis=1)
    dcb = jnp.concatenate([dcbg, dcbv], axis=1)
    return dx, dg, dw_up, dcw, dcb, dw_down


def _step(x, mem, tgt, W):
    B, S, D = x.shape
    T = B * S
    x0 = x.reshape(T, D)
    mem2 = mem.reshape(-1, D)
    tgt2 = tgt.reshape(T, D)
    row = lambda v: v.reshape(1, -1)
    q3 = 3 * MAIN_WIDTH

    w_in_a = W['w_in_a'][0]
    wa_main = jnp.concatenate([w_in_a[:, :q3], w_in_a[:, q3 + N_MAIN_HEADS:]], axis=1)
    wa_gate = jnp.pad(w_in_a[:, q3:q3 + N_MAIN_HEADS], ((0, 0), (0, LANES - N_MAIN_HEADS)))
    w_in_b = W['w_in_b'][0]
    bcol = jnp.tile(W['b_f_a'][0], B).reshape(B * N_MAIN_HEADS, 1)
    nkb = S // min(FOX_BLOCK, S)

    h1a = _rms_fwd(x0, row(W['ln_mix_g'][0]), name='rms_mix_a')
    pa = _matmul(h1a, wa_main, 'nn', ACT_DTYPE, name='mm_in_a')
    flog = _matmul(h1a, wa_gate, 'nn', F32, name='mm_gate_a')
    qa = _heads(pa[:, :MAIN_WIDTH], B, S, N_MAIN_HEADS)
    ka = _heads(pa[:, MAIN_WIDTH:2 * MAIN_WIDTH], B, S, N_MAIN_HEADS)
    va = _heads(pa[:, 2 * MAIN_WIDTH:q3], B, S, N_MAIN_HEADS)
    qma = _heads(pa[:, q3:], B, S, N_MEM_HEADS)
    zt = flog[:, :N_MAIN_HEADS].reshape(B, S, N_MAIN_HEADS).transpose(0, 2, 1).reshape(B * N_MAIN_HEADS, S)
    cum = _gate_fwd(zt, bcol)
    ccol = cum.reshape(B * N_MAIN_HEADS, S, 1)
    crow = cum.reshape(B * N_MAIN_HEADS, nkb, 1, S // nkb)
    oa, lse = _fox_fwd(qa, ka, va, ccol, crow)
    hma, mka, mva = _mem_kv_fwd(mem2, row(W['ln_mem_g'][0]), W['w_memkv'][0], B, 'a')
    oma = _mem_fwd(qma, mka, mva, name='mem_fwd_a')
    ocat_a = jnp.concatenate([_unheads(oa, B, S, N_MAIN_HEADS), _unheads(oma, B, S, N_MEM_HEADS)], axis=1)
    x1 = _matmul(ocat_a, W['w_out'][0], 'nn', F32, res=x0, name='mm_out_a')
    x2, ffn_a = _ffn_fwd(x1, row(W['ln_ffn_g'][0]), W['w_up'][0], W['conv_w'][0], row(W['conv_b'][0]),
                         W['w_down'][0], B, S, 'a')
    hkv = _rms_fwd(x2, row(W['ln_kv_g']), name='rms_kv')
    kvs = _matmul(hkv, W['w_kv'], 'nn', ACT_DTYPE, name='mm_kv')
    kb = _heads(kvs[:, :MAIN_WIDTH], B, S, N_MAIN_HEADS)
    vb = _heads(kvs[:, MAIN_WIDTH:], B, S, N_MAIN_HEADS)
    h1b = _rms_fwd(x2, row(W['ln_mix_g'][1]), name='rms_mix_b')
    pb = _matmul(h1b, w_in_b, 'nn', ACT_DTYPE, name='mm_in_b')
    qb = _heads(pb[:, :MAIN_WIDTH], B, S, N_MAIN_HEADS)
    qmb = _heads(pb[:, MAIN_WIDTH:], B, S, N_MEM_HEADS)
    ob = _sb_fwd(qb, kb, vb)
    hmb, mkb, mvb = _mem_kv_fwd(mem2, row(W['ln_mem_g'][1]), W['w_memkv'][1], B, 'b')
    omb = _mem_fwd(qmb, mkb, mvb, name='mem_fwd_b')
    ocat_b = jnp.concatenate([_unheads(ob, B, S, N_MAIN_HEADS), _unheads(omb, B, S, N_MEM_HEADS)], axis=1)
    x3 = _matmul(ocat_b, W['w_out'][1], 'nn', F32, res=x2, name='mm_out_b')
    x4, ffn_b = _ffn_fwd(x3, row(W['ln_ffn_g'][1]), W['w_up'][1], W['conv_w'][1], row(W['conv_b'][1]),
                         W['w_down'][1], B, S, 'b')
    loss, dx4, d_final_g = _final_loss(x4, row(W['final_g']), tgt2)

    dx3, dg_ffn_b, dw_up_b, dcw_b, dcb_b, dw_down_b = _ffn_bwd(
        dx4, x3, row(W['ln_ffn_g'][1]), W['w_up'][1], W['conv_w'][1], row(W['conv_b'][1]), W['w_down'][1],
        ffn_b, B, S, 'b')
    docat = _matmul(dx3, W['w_out'][1], 'nt', ACT_DTYPE, name='mm_out_dx_b')
    dw_out_b = _matmul(ocat_b, dx3, 'tn', F32, name='mm_out_dw_b')
    dob = _heads(docat[:, :MAIN_WIDTH], B, S, N_MAIN_HEADS)
    domb = _heads(docat[:, MAIN_WIDTH:], B, S, N_MEM_HEADS)
    dqb, dkb, dvb = _sb_bwd(qb, kb, vb, dob)
    dqmb, dmkb, dmvb = _mem_bwd(qmb, mkb, mvb, domb, name='mem_bwd_b')
    dw_memkv_b, dg_mem_b = _mem_kv_bwd(mem2, row(W['ln_mem_g'][1]), W['w_memkv'][1], hmb, dmkb, dmvb, B, 'b')
    dpb = jnp.concatenate([_unheads(dqb, B, S, N_MAIN_HEADS), _unheads(dqmb, B, S, N_MEM_HEADS)],
                          axis=1).astype(ACT_DTYPE)
    dh1b = _matmul(dpb, w_in_b, 'nt', F32, name='mm_in_dx_b')
    dw_in_b = _matmul(h1b, dpb, 'tn', F32, name='mm_in_dw_b')
    dx2, dg_mix_b = _rms_bwd(x2, row(W['ln_mix_g'][1]), dh1b, dx3, name='rms_mix_bwd_b')
    dkvs = jnp.concatenate([_unheads(dkb, B, S, N_MAIN_HEADS), _unheads(dvb, B, S, N_MAIN_HEADS)],
                           axis=1).astype(ACT_DTYPE)
    dhkv = _matmul(dkvs, W['w_kv'], 'nt', F32, name='mm_kv_dx')
    dw_kv = _matmul(hkv, dkvs, 'tn', F32, name='mm_kv_dw')
    dx2, dg_kv = _rms_bwd(x2, row(W['ln_kv_g']), dhkv, dx2, name='rms_kv_bwd')

    dx1, dg_ffn_a, dw_up_a, dcw_a, dcb_a, dw_down_a = _ffn_bwd(
        dx2, x1, row(W['ln_ffn_g'][0]), W['w_up'][0], W['conv_w'][0], row(W['conv_b'][0]), W['w_down'][0],
        ffn_a, B, S, 'a')
    docat = _matmul(dx1, W['w_out'][0], 'nt', ACT_DTYPE, name='mm_out_dx_a')
    dw_out_a = _matmul(ocat_a, dx1, 'tn', F32, name='mm_out_dw_a')
    doa = _heads(docat[:, :MAIN_WIDTH], B, S, N_MAIN_HEADS)
    doma = _heads(docat[:, MAIN_WIDTH:], B, S, N_MEM_HEADS)
    dqa, dka, dva, dccol, dcrow = _fox_bwd(qa, ka, va, ccol, crow, oa, lse, doa)
    dzt, dbrow = _gate_bwd(zt, bcol, dccol.reshape(B * N_MAIN_HEADS, S) + dcrow.reshape(B * N_MAIN_HEADS, S))
    dqma, dmka, dmva = _mem_bwd(qma, mka, mva, doma, name='mem_bwd_a')
    dw_memkv_a, dg_mem_a = _mem_kv_bwd(mem2, row(W['ln_mem_g'][0]), W['w_memkv'][0], hma, dmka, dmva, B, 'a')
    dpa = jnp.concatenate([_unheads(dqa, B, S, N_MAIN_HEADS), _unheads(dka, B, S, N_MAIN_HEADS),
                           _unheads(dva, B, S, N_MAIN_HEADS), _unheads(dqma, B, S, N_MEM_HEADS)],
                          axis=1).astype(ACT_DTYPE)
    dflog = dzt.reshape(B, N_MAIN_HEADS, S).transpose(0, 2, 1).reshape(T, N_MAIN_HEADS)
    dflog = jnp.pad(dflog, ((0, 0), (0, LANES - N_MAIN_HEADS)))
    dh1a = _matmul(dpa, wa_main, 'nt', F32, name='mm_in_dx_a')
    dh1a = _matmul(dflog, wa_gate, 'nt', F32, res=dh1a, name='mm_gate_dx_a')
    dwa_main = _matmul(h1a, dpa, 'tn', F32, name='mm_in_dw_a')
    dwa_gate = _matmul(h1a, dflog, 'tn', F32, name='mm_gate_dw_a')
    dx0, dg_mix_a = _rms_bwd(x0, row(W['ln_mix_g'][0]), dh1a, dx1, name='rms_mix_bwd_a')

    dw_in_a = jnp.concatenate([dwa_main[:, :q3], dwa_gate[:, :N_MAIN_HEADS], dwa_main[:, q3:]], axis=1)
    grads = {
        'ln_mix_g': jnp.concatenate([dg_mix_a, dg_mix_b], axis=0),
        'w_in_a': dw_in_a[None],
        'b_f_a': dbrow.reshape(B, N_MAIN_HEADS).sum(axis=0)[None],
        'w_in_b': dw_in_b[None],
        'ln_kv_g': dg_kv[0],
        'w_kv': dw_kv,
        'ln_mem_g': jnp.concatenate([dg_mem_a, dg_mem_b], axis=0),
        'w_memkv': jnp.stack([dw_memkv_a, dw_memkv_b]),
        'w_out': jnp.stack([dw_out_a, dw_out_b]),
        'ln_ffn_g': jnp.concatenate([dg_ffn_a, dg_ffn_b], axis=0),
        'w_up': jnp.stack([dw_up_a, dw_up_b]),
        'conv_w': jnp.stack([dcw_a, dcw_b]),
        'conv_b': jnp.concatenate([dcb_a, dcb_b], axis=0),
        'w_down': jnp.stack([dw_down_a, dw_down_b]),
        'final_g': d_final_g[0],
    }
    return loss, dx0.reshape(B, S, D), grads


def _reduce_grads(packed):
    x, y, c = _my_place()
    c_arr = jnp.reshape(c, (1,)).astype(jnp.int32)
    chip_arr = jnp.reshape(2 * x + y, (1,)).astype(jnp.int32)
    other = _swap_halves(packed)
    q = _add_half(packed, other, c_arr)
    got = _scatter_chips(q)
    mine = _add_chips(q, got, chip_arr)
    both = _join_halves(mine)
    return both.reshape(-1, PACK_COLS)


def kernel(x, mem, ln_mix_g, w_in_a, b_f_a, w_in_b, ln_kv_g, w_kv, ln_mem_g, w_memkv, w_out, ln_ffn_g, w_up, conv_w, conv_b, w_down, final_g, loss_target, m_ln_mix_g, m_w_in_a, m_b_f_a, m_w_in_b, m_ln_kv_g, m_w_kv, m_ln_mem_g, m_w_memkv, m_w_out, m_ln_ffn_g, m_w_up, m_conv_w, m_conv_b, m_w_down, m_final_g, v_ln_mix_g, v_w_in_a, v_b_f_a, v_w_in_b, v_ln_kv_g, v_w_kv, v_ln_mem_g, v_w_memkv, v_w_out, v_ln_ffn_g, v_w_up, v_conv_w, v_conv_b, v_w_down, v_final_g):
    shards = dict(ln_mix_g=ln_mix_g, w_in_a=w_in_a, b_f_a=b_f_a, w_in_b=w_in_b, ln_kv_g=ln_kv_g, w_kv=w_kv,
                  ln_mem_g=ln_mem_g, w_memkv=w_memkv, w_out=w_out, ln_ffn_g=ln_ffn_g, w_up=w_up, conv_w=conv_w,
                  conv_b=conv_b, w_down=w_down, final_g=final_g)
    moments_m = dict(ln_mix_g=m_ln_mix_g, w_in_a=m_w_in_a, b_f_a=m_b_f_a, w_in_b=m_w_in_b, ln_kv_g=m_ln_kv_g,
                     w_kv=m_w_kv, ln_mem_g=m_ln_mem_g, w_memkv=m_w_memkv, w_out=m_w_out, ln_ffn_g=m_ln_ffn_g,
                     w_up=m_w_up, conv_w=m_conv_w, conv_b=m_conv_b, w_down=m_w_down, final_g=m_final_g)
    moments_v = dict(ln_mix_g=v_ln_mix_g, w_in_a=v_w_in_a, b_f_a=v_b_f_a, w_in_b=v_w_in_b, ln_kv_g=v_ln_kv_g,
                     w_kv=v_w_kv, ln_mem_g=v_ln_mem_g, w_memkv=v_w_memkv, w_out=v_w_out, ln_ffn_g=v_ln_ffn_g,
                     w_up=v_w_up, conv_w=v_conv_w, conv_b=v_conv_b, w_down=v_w_down, final_g=v_final_g)

    gathered = _gather_chips(_pack_weights(shards))
    W = dict(shards)
    W.update(_unpack_weights(gathered, shards))

    loss_part, grad_x, grads = _step(x, mem, loss_target, W)
    loss = lax.psum(loss_part[0, 0], ('x', 'y', 'c'))

    g_shard = _unpack_grads(_reduce_grads(_pack_grads(grads, shards)), shards)

    deltas, new_m, new_v = {}, {}, {}
    for name in PARAM_NAMES:
        w = shards[name]
        two_d = (-1, w.shape[-1])
        d, nm, nv = _adamw(w.reshape(two_d), g_shard[name].reshape(two_d), moments_m[name].reshape(two_d),
                           moments_v[name].reshape(two_d), name=f'adamw_{name}')
        deltas[name], new_m[name], new_v[name] = d.reshape(w.shape), nm.reshape(w.shape), nv.reshape(w.shape)

    return (loss, grad_x, *[g_shard[n] for n in PARAM_NAMES], *[deltas[n] for n in PARAM_NAMES],
            *[new_m[n] for n in PARAM_NAMES], *[new_v[n] for n in PARAM_NAMES])
```

```python
import functools
import math

import jax
import jax.numpy as jnp
from jax import lax
from jax.experimental import pallas as pl
from jax.experimental.pallas import tpu as pltpu

F32 = jnp.float32
MXU_DTYPE = jnp.bfloat16
ACT_DTYPE = jnp.bfloat16

HEAD_DIM = 64
N_MAIN_HEADS = 12
N_MEM_HEADS = 4
MAIN_WIDTH = N_MAIN_HEADS * HEAD_DIM
MEM_WIDTH = N_MEM_HEADS * HEAD_DIM
EPS = 1e-6
SCALE = HEAD_DIM ** -0.5
NEG_BIG = -1e30
LANES = 128
PACK_COLS = 1024
N_CHIPS = 4

ADAM_LR = 0.001
ADAM_B1 = 0.9
ADAM_B2 = 0.999
ADAM_EPS = 1e-08
ADAM_WD = 0.01
ADAM_STEP = 10

MESH = pl.DeviceIdType.MESH
ANY = pl.BlockSpec(memory_space=pl.ANY)

PARAM_SHARD_AXIS = {
    'ln_mix_g': None, 'w_in_a': 2, 'b_f_a': None, 'w_in_b': 1, 'ln_kv_g': None, 'w_kv': 1,
    'ln_mem_g': None, 'w_memkv': 1, 'w_out': 1, 'ln_ffn_g': None, 'w_up': 2, 'conv_w': 2,
    'conv_b': None, 'w_down': 1, 'final_g': None,
}
PARAM_NAMES = list(PARAM_SHARD_AXIS)
F32_GATHERED = ('conv_w',)


def _tile(n, pref, unit=LANES):
    if n <= pref:
        return n
    best = None
    for t in range(unit, pref + 1, unit):
        if n % t == 0:
            best = t
    assert best is not None, (n, pref)
    return best


MM_ACC_ELEMS = 768 * 1024
MM_VMEM_MB = 48


def _out_tiles(M, N):
    def divisors(n, cap):
        if n <= LANES:
            return [n]
        return [t for t in range(LANES, min(n, cap) + 1, LANES) if n % t == 0]

    best = None
    for tm in divisors(M, 1024):
        for tn in divisors(N, 2048):
            if tm * tn <= MM_ACC_ELEMS and (best is None or (tm * tn, tn) > (best[0] * best[1], best[1])):
                best = (tm, tn)
    assert best is not None, (M, N)
    return best


def _params(*sem, vmem_mb=None):
    kw = {}
    if sem:
        kw['dimension_semantics'] = sem
    if vmem_mb is not None:
        kw['vmem_limit_bytes'] = vmem_mb * 1024 * 1024
    return pltpu.CompilerParams(**kw)


def _dot(a, b, dims):
    return lax.dot_general(a.astype(MXU_DTYPE), b.astype(MXU_DTYPE), (dims, ((), ())),
                           preferred_element_type=F32)


NN = ((1,), (0,))
NT = ((1,), (1,))
TN = ((0,), (0,))


def _matmul(a, b, mode, out_dtype, res=None, name='mm'):
    if mode == 'nn':
        (M, K), (K2, N) = a.shape, b.shape
    elif mode == 'nt':
        (M, K), (N, K2) = a.shape, b.shape
    else:
        (K, M), (K2, N) = a.shape, b.shape
    assert K == K2, (a.shape, b.shape, mode)
    tm, tn = _out_tiles(M, N)
    tk = _tile(K, 1024)
    nk = K // tk
    dims = {'nn': NN, 'nt': NT, 'tn': TN}[mode]

    def body(*refs):
        if res is None:
            a_ref, b_ref, o_ref, acc = refs
        else:
            a_ref, b_ref, r_ref, o_ref, acc = refs
        k = pl.program_id(2)

        @pl.when(k == 0)
        def _():
            acc[...] = jnp.zeros_like(acc)

        acc[...] += _dot(a_ref[...], b_ref[...], dims)

        @pl.when(k == nk - 1)
        def _():
            out = acc[...]
            if res is not None:
                out = out + r_ref[...]
            o_ref[...] = out.astype(out_dtype)

    if mode == 'tn':
        a_spec = pl.BlockSpec((tk, tm), lambda i, j, k: (k, i))
    else:
        a_spec = pl.BlockSpec((tm, tk), lambda i, j, k: (i, k))
    if mode == 'nt':
        b_spec = pl.BlockSpec((tn, tk), lambda i, j, k: (j, k))
    else:
        b_spec = pl.BlockSpec((tk, tn), lambda i, j, k: (k, j))
    o_spec = pl.BlockSpec((tm, tn), lambda i, j, k: (i, j))
    in_specs = [a_spec, b_spec] + ([o_spec] if res is not None else [])
    args = (a, b) + ((res,) if res is not None else ())
    return pl.pallas_call(
        body, name=name, grid=(M // tm, N // tn, nk),
        in_specs=in_specs, out_specs=o_spec,
        out_shape=jax.ShapeDtypeStruct((M, N), out_dtype),
        scratch_shapes=[pltpu.VMEM((tm, tn), F32)],
        compiler_params=_params('parallel', 'parallel', 'arbitrary', vmem_mb=MM_VMEM_MB),
    )(*args)


def _rms_fwd(x, g, name):
    T, D = x.shape
    tr = _tile(T, 512)

    def body(x_ref, g_ref, o_ref):
        xv = x_ref[...]
        r = lax.rsqrt(jnp.mean(xv * xv, axis=-1, keepdims=True) + EPS)
        o_ref[...] = (xv * r * g_ref[...]).astype(ACT_DTYPE)

    return pl.pallas_call(
        body, name=name, grid=(T // tr,),
        in_specs=[pl.BlockSpec((tr, D), lambda i: (i, 0)), pl.BlockSpec((1, D), lambda i: (0, 0))],
        out_specs=pl.BlockSpec((tr, D), lambda i: (i, 0)),
        out_shape=jax.ShapeDtypeStruct((T, D), ACT_DTYPE),
        compiler_params=_params('parallel'),
    )(x, g)


def _rms_bwd(x, g, dh, dres, name):
    T, D = x.shape
    tr = _tile(T, 512)
    want_dx = dres is not None

    def body(*refs):
        if want_dx:
            x_ref, g_ref, dh_ref, dres_ref, dx_ref, dg_ref = refs
        else:
            x_ref, g_ref, dh_ref, dg_ref = refs
        i = pl.program_id(0)

        @pl.when(i == 0)
        def _():
            dg_ref[...] = jnp.zeros_like(dg_ref)

        xv = x_ref[...]
        dhv = dh_ref[...].astype(F32)
        r = lax.rsqrt(jnp.mean(xv * xv, axis=-1, keepdims=True) + EPS)
        n = xv * r
        dg_ref[...] += jnp.sum(dhv * n, axis=0, keepdims=True)
        if want_dx:
            dn = dhv * g_ref[...]
            dx = r * (dn - n * jnp.mean(dn * n, axis=-1, keepdims=True))
            dx_ref[...] = dres_ref[...] + dx

    row = pl.BlockSpec((tr, D), lambda i: (i, 0))
    vec = pl.BlockSpec((1, D), lambda i: (0, 0))
    if want_dx:
        return pl.pallas_call(
            body, name=name, grid=(T // tr,),
            in_specs=[row, vec, row, row], out_specs=[row, vec],
            out_shape=[jax.ShapeDtypeStruct((T, D), F32), jax.ShapeDtypeStruct((1, D), F32)],
            compiler_params=_params('arbitrary'),
        )(x, g, dh, dres)
    dg = pl.pallas_call(
        body, name=name, grid=(T // tr,),
        in_specs=[row, vec, row], out_specs=vec,
        out_shape=jax.ShapeDtypeStruct((1, D), F32),
        compiler_params=_params('arbitrary'),
    )(x, g, dh)
    return None, dg


def _final_loss(x, g, tgt, name='final_loss'):
    T, D = x.shape
    tr = _tile(T, 512)

    def body(x_ref, g_ref, t_ref, loss_ref, dx_ref, dg_ref):
        i = pl.program_id(0)

        @pl.when(i == 0)
        def _():
            loss_ref[...] = jnp.zeros_like(loss_ref)
            dg_ref[...] = jnp.zeros_like(dg_ref)

        xv = x_ref[...]
        gv = g_ref[...]
        r = lax.rsqrt(jnp.mean(xv * xv, axis=-1, keepdims=True) + EPS)
        n = xv * r
        e = n * gv - t_ref[...]
        per_tok = jnp.mean(e * e, axis=-1, keepdims=True)
        loss_ref[...] += 0.5 * jnp.sum(per_tok, axis=0, keepdims=True)
        dy = e * (1.0 / D)
        dg_ref[...] += jnp.sum(dy * n, axis=0, keepdims=True)
        dn = dy * gv
        dx_ref[...] = r * (dn - n * jnp.mean(dn * n, axis=-1, keepdims=True))

    row = pl.BlockSpec((tr, D), lambda i: (i, 0))
    vec = pl.BlockSpec((1, D), lambda i: (0, 0))
    one = pl.BlockSpec((1, 1), lambda i: (0, 0))
    return pl.pallas_call(
        body, name=name, grid=(T // tr,),
        in_specs=[row, vec, row], out_specs=[one, row, vec],
        out_shape=[jax.ShapeDtypeStruct((1, 1), F32), jax.ShapeDtypeStruct((T, D), F32),
                   jax.ShapeDtypeStruct((1, D), F32)],
        compiler_params=_params('arbitrary'),
    )(x, g, tgt)


def _log_sigmoid(z):
    return jnp.minimum(z, 0.0) - jnp.log(1.0 + jnp.exp(-jnp.abs(z)))


def _tri(n, rel):
    j = lax.broadcasted_iota(jnp.int32, (n, n), 0)
    s = lax.broadcasted_iota(jnp.int32, (n, n), 1)
    return rel(j, s).astype(MXU_DTYPE)


def _split_dot(x, tri, terms):
    if MXU_DTYPE == F32:
        return jnp.dot(x, tri, preferred_element_type=F32)
    out = None
    rem = x
    for _ in range(terms):
        piece = rem.astype(MXU_DTYPE)
        part = jnp.dot(piece, tri, preferred_element_type=F32)
        out = part if out is None else out + part
        rem = rem - piece.astype(F32)
    return out


def _gate_fwd(zt, bcol, name='gate_fwd'):
    BH, S = zt.shape
    nb = S // LANES

    def body(z_ref, b_ref, c_ref):
        tri = _tri(LANES, lambda j, s: j <= s)
        carry = jnp.zeros((BH, 1), F32)
        for i in range(nb):
            sl = slice(i * LANES, (i + 1) * LANES)
            logf = _log_sigmoid(z_ref[:, sl] + b_ref[...])
            cs = _split_dot(logf, tri, 3) + carry
            c_ref[:, sl] = cs
            carry = cs[:, LANES - 1:LANES]

    return pl.pallas_call(body, name=name, out_shape=jax.ShapeDtypeStruct((BH, S), F32))(zt, bcol)


def _gate_bwd(zt, bcol, dc, name='gate_bwd'):
    BH, S = zt.shape
    nb = S // LANES

    def body(z_ref, b_ref, dc_ref, dz_ref, db_ref):
        tri = _tri(LANES, lambda j, s: j >= s)
        carry = jnp.zeros((BH, 1), F32)
        dsum = jnp.zeros((BH, 1), F32)
        for i in reversed(range(nb)):
            sl = slice(i * LANES, (i + 1) * LANES)
            rs = _split_dot(dc_ref[:, sl], tri, 3) + carry
            carry = rs[:, 0:1]
            z = z_ref[:, sl] + b_ref[...]
            dz = rs * (1.0 - 1.0 / (1.0 + jnp.exp(-z)))
            dz_ref[:, sl] = dz
            dsum = dsum + jnp.sum(dz, axis=-1, keepdims=True)
        db_ref[...] = dsum

    return pl.pallas_call(
        body, name=name,
        out_shape=[jax.ShapeDtypeStruct((BH, S), F32), jax.ShapeDtypeStruct((BH, 1), F32)],
    )(zt, bcol, dc)


FOX_BLOCK = 256


PAIR = 2 * HEAD_DIM
N_MAIN_PAIRS = N_MAIN_HEADS // 2
N_MEM_PAIRS = N_MEM_HEADS // 2


def _lane0(shape):
    return lax.broadcasted_iota(jnp.int32, shape, len(shape) - 1) < HEAD_DIM


def _per_head(x):
    first = _lane0(x.shape)
    zero = jnp.zeros_like(x)
    return jnp.where(first, x, zero), jnp.where(first, zero, x)


def _pick(first, a, b):
    return jnp.where(first, a, b)


def _q_spec(bq, nq, off):
    return pl.BlockSpec((bq, PAIR), lambda b, j, i: (b * nq + i, off + j))


def _seq_spec(S, off):
    return pl.BlockSpec((S, PAIR), lambda b, j, i: (b, off + j))


def _gate_specs(bq, nk):
    col = pl.BlockSpec((2, bq, 1), lambda b, j, i: (b * N_MAIN_PAIRS + j, i, 0))
    rowv = pl.BlockSpec((2, nk, 1, bq), lambda b, j, i: (b * N_MAIN_PAIRS + j, 0, 0, 0))
    return col, rowv


def _causal(i, kb, bq, strict):
    row = i * bq + lax.broadcasted_iota(jnp.int32, (bq, bq), 0)
    col = kb * bq + lax.broadcasted_iota(jnp.int32, (bq, bq), 1)
    return (col < row) if strict else (col <= row)


def _fox_fwd(qkv, offs, B, S, ccol, crow, name='fox_fwd'):
    bq = min(FOX_BLOCK, S)
    nq = S // bq

    def body(q_ref, k_ref, v_ref, cc_ref, cr_ref, o_ref, lse_ref):
        i = pl.program_id(2)
        qh = _per_head(q_ref[...])
        first = _lane0((bq, PAIR))

        def step(kb, carry):
            m, l, acc = carry
            sl = pl.ds(pl.multiple_of(kb * bq, bq), bq)
            ks, vs = k_ref[sl, :], v_ref[sl, :]
            mask = _causal(i, kb, bq, False)
            m_new, l_new, alpha, pv = [], [], [], []
            for h in range(2):
                s = _dot(qh[h], ks, NT) * SCALE + cc_ref[h] - cr_ref[h, kb]
                s = jnp.where(mask, s, NEG_BIG)
                mh = jnp.maximum(m[h], jnp.max(s, axis=-1, keepdims=True))
                p = jnp.exp(s - mh)
                ah = jnp.exp(m[h] - mh)
                m_new.append(mh)
                alpha.append(ah)
                l_new.append(ah * l[h] + jnp.sum(p, axis=-1, keepdims=True))
                pv.append(_dot(p, vs, NN))
            acc = _pick(first, alpha[0], alpha[1]) * acc + _pick(first, pv[0], pv[1])
            return tuple(m_new), tuple(l_new), acc

        neg = jnp.full((bq, 1), NEG_BIG, F32)
        zero = jnp.zeros((bq, 1), F32)
        m, l, acc = lax.fori_loop(0, i + 1, step, ((neg, neg), (zero, zero), jnp.zeros((bq, PAIR), F32)))
        o_ref[...] = (acc / _pick(first, l[0], l[1])).astype(ACT_DTYPE)
        for h in range(2):
            lse_ref[h] = m[h] + jnp.log(l[h])

    col, rowv = _gate_specs(bq, nq)
    return pl.pallas_call(
        body, name=name, grid=(B, N_MAIN_PAIRS, nq),
        in_specs=[_q_spec(bq, nq, offs[0]), _seq_spec(S, offs[1]), _seq_spec(S, offs[2]), col, rowv],
        out_specs=[_q_spec(bq, nq, 0), col],
        out_shape=[jax.ShapeDtypeStruct((B * S, MAIN_WIDTH), ACT_DTYPE),
                   jax.ShapeDtypeStruct((B * N_MAIN_HEADS, S, 1), F32)],
        compiler_params=_params('parallel', 'parallel', 'arbitrary'),
    )(*qkv, ccol, crow)


def _fox_bwd(qkv, offs, B, S, ccol, crow, o, lse, do, name='fox_bwd'):
    bq = min(FOX_BLOCK, S)
    nq = S // bq

    def body(q_ref, k_ref, v_ref, cc_ref, cr_ref, o_ref, lse_ref, do_ref,
             dq_ref, dk_ref, dv_ref, dcc_ref, dcr_ref, dk_acc, dv_acc):
        i = pl.program_id(2)

        @pl.when(i == 0)
        def _():
            dk_acc[...] = jnp.zeros_like(dk_acc)
            dv_acc[...] = jnp.zeros_like(dv_acc)
            dcr_ref[...] = jnp.zeros_like(dcr_ref)

        qv = q_ref[...]
        dov = do_ref[...]
        qh = _per_head(qv)
        doh = _per_head(dov)
        first = _lane0((bq, PAIR))
        prod = dov.astype(F32) * o_ref[...].astype(F32)
        dsum = [jnp.sum(t, axis=-1, keepdims=True) for t in _per_head(prod)]

        def step(kb, carry):
            dq, dcc = carry
            sl = pl.ds(pl.multiple_of(kb * bq, bq), bq)
            ks, vs = k_ref[sl, :], v_ref[sl, :]
            mask = _causal(i, kb, bq, False)
            dqh, dkh, dvh, dcc_new = [], [], [], []
            for h in range(2):
                s = _dot(qh[h], ks, NT) * SCALE + cc_ref[h] - cr_ref[h, kb]
                p = jnp.where(mask, jnp.exp(s - lse_ref[h]), 0.0)
                ds = p * (_dot(doh[h], vs, NT) - dsum[h])
                dqh.append(_dot(ds, ks, NN))
                dkh.append(_dot(ds, qv, TN))
                dvh.append(_dot(p, dov, TN))
                dcr_ref[h, kb] -= jnp.sum(ds, axis=0, keepdims=True)
                dcc_new.append(dcc[h] + jnp.sum(ds, axis=-1, keepdims=True))
            dk_acc[sl, :] += SCALE * _pick(first, dkh[0], dkh[1])
            dv_acc[sl, :] += _pick(first, dvh[0], dvh[1])
            return dq + _pick(first, dqh[0], dqh[1]), tuple(dcc_new)

        zero = jnp.zeros((bq, 1), F32)
        dq, dcc = lax.fori_loop(0, i + 1, step, (jnp.zeros((bq, PAIR), F32), (zero, zero)))
        dq_ref[...] = (dq * SCALE).astype(ACT_DTYPE)
        for h in range(2):
            dcc_ref[h] = dcc[h]

        @pl.when(i == nq - 1)
        def _():
            dk_ref[...] = dk_acc[...].astype(ACT_DTYPE)
            dv_ref[...] = dv_acc[...].astype(ACT_DTYPE)

    col, rowv = _gate_specs(bq, nq)
    qs, seq = _q_spec(bq, nq, 0), _seq_spec(S, 0)
    full = jax.ShapeDtypeStruct((B * S, MAIN_WIDTH), ACT_DTYPE)
    return pl.pallas_call(
        body, name=name, grid=(B, N_MAIN_PAIRS, nq),
        in_specs=[_q_spec(bq, nq, offs[0]), _seq_spec(S, offs[1]), _seq_spec(S, offs[2]), col, rowv, qs, col, qs],
        out_specs=[qs, seq, seq, col, rowv],
        out_shape=[full, full, full, jax.ShapeDtypeStruct(ccol.shape, F32), jax.ShapeDtypeStruct(crow.shape, F32)],
        scratch_shapes=[pltpu.VMEM((S, PAIR), F32), pltpu.VMEM((S, PAIR), F32)],
        compiler_params=_params('parallel', 'parallel', 'arbitrary'),
    )(*qkv, ccol, crow, o, lse, do)


SB_BLOCK = 256


def _sb_block(qv, ks, i, kb, bq):
    z = _dot(qv, ks, NT) * SCALE
    row = i * bq + lax.broadcasted_iota(jnp.int32, (bq, bq), 0)
    col = kb * bq + lax.broadcasted_iota(jnp.int32, (bq, bq), 1)
    mask = col < row
    a = _log_sigmoid(z)
    l = jnp.where(mask, a - z, 0.0)
    return mask, a, l


def _sb_fwd(qkv, offs, B, S, name='sb_fwd'):
    bq = min(SB_BLOCK, S)
    nq = S // bq

    def body(q_ref, k_ref, v_ref, o_ref):
        i = pl.program_id(2)
        qh = _per_head(q_ref[...])
        first = _lane0((bq, PAIR))
        tri = _tri(bq, lambda j, s: j > s)

        def step(n, carry):
            acc, right = carry
            kb = i - n
            sl = pl.ds(pl.multiple_of(kb * bq, bq), bq)
            ks, vs = k_ref[sl, :], v_ref[sl, :]
            pv, right_new = [], []
            for h in range(2):
                mask, a, l = _sb_block(qh[h], ks, i, kb, bq)
                rsum = _split_dot(l, tri, 2) + right[h]
                w = jnp.where(mask, jnp.exp(a + rsum), 0.0)
                pv.append(_dot(w, vs, NN))
                right_new.append(right[h] + jnp.sum(l, axis=-1, keepdims=True))
            return acc + _pick(first, pv[0], pv[1]), tuple(right_new)

        zero = jnp.zeros((bq, 1), F32)
        acc, _ = lax.fori_loop(0, i + 1, step, (jnp.zeros((bq, PAIR), F32), (zero, zero)))
        o_ref[...] = acc.astype(ACT_DTYPE)

    return pl.pallas_call(
        body, name=name, grid=(B, N_MAIN_PAIRS, nq),
        in_specs=[_q_spec(bq, nq, offs[0]), _seq_spec(S, offs[1]), _seq_spec(S, offs[2])],
        out_specs=_q_spec(bq, nq, 0),
        out_shape=jax.ShapeDtypeStruct((B * S, MAIN_WIDTH), ACT_DTYPE),
        compiler_params=_params('parallel', 'parallel', 'arbitrary'),
    )(*qkv)


def _sb_bwd(qkv, offs, B, S, do, name='sb_bwd'):
    bq = min(SB_BLOCK, S)
    nq = S // bq

    def body(q_ref, k_ref, v_ref, do_ref, dq_ref, dk_ref, dv_ref, dk_acc, dv_acc):
        i = pl.program_id(2)

        @pl.when(i == 0)
        def _():
            dk_acc[...] = jnp.zeros_like(dk_acc)
            dv_acc[...] = jnp.zeros_like(dv_acc)

        qv = q_ref[...]
        dov = do_ref[...]
        qh = _per_head(qv)
        doh = _per_head(dov)
        first = _lane0((bq, PAIR))
        tri_incl = _tri(bq, lambda j, s: j <= s)
        tri_excl = _tri(bq, lambda j, s: j < s)

        def total(kb, tot):
            sl = pl.ds(pl.multiple_of(kb * bq, bq), bq)
            ks = k_ref[sl, :]
            return tuple(tot[h] + jnp.sum(_sb_block(qh[h], ks, i, kb, bq)[2], axis=-1, keepdims=True)
                         for h in range(2))

        zero = jnp.zeros((bq, 1), F32)
        tot = lax.fori_loop(0, i + 1, total, (zero, zero))

        def step(kb, carry):
            dq, left_l, left_g = carry
            sl = pl.ds(pl.multiple_of(kb * bq, bq), bq)
            ks, vs = k_ref[sl, :], v_ref[sl, :]
            dqh, dkh, dvh, new_l, new_g = [], [], [], [], []
            for h in range(2):
                mask, a, l = _sb_block(qh[h], ks, i, kb, bq)
                cum = _split_dot(l, tri_incl, 2) + left_l[h]
                w = jnp.where(mask, jnp.exp(a + tot[h] - cum), 0.0)
                g = w * _dot(doh[h], vs, NT)
                hsum = _split_dot(g, tri_excl, 1) + left_g[h]
                beta = jnp.exp(a)
                dz = jnp.where(mask, g * (1.0 - beta) - hsum * beta, 0.0)
                dqh.append(_dot(dz, ks, NN))
                dkh.append(_dot(dz, qv, TN))
                dvh.append(_dot(w, dov, TN))
                new_l.append(left_l[h] + jnp.sum(l, axis=-1, keepdims=True))
                new_g.append(left_g[h] + jnp.sum(g, axis=-1, keepdims=True))
            dk_acc[sl, :] += SCALE * _pick(first, dkh[0], dkh[1])
            dv_acc[sl, :] += _pick(first, dvh[0], dvh[1])
            return dq + _pick(first, dqh[0], dqh[1]), tuple(new_l), tuple(new_g)

        dq, _, _ = lax.fori_loop(0, i + 1, step, (jnp.zeros((bq, PAIR), F32), (zero, zero), (zero, zero)))
        dq_ref[...] = (dq * SCALE).astype(ACT_DTYPE)

        @pl.when(i == nq - 1)
        def _():
            dk_ref[...] = dk_acc[...].astype(ACT_DTYPE)
            dv_ref[...] = dv_acc[...].astype(ACT_DTYPE)

    qs, seq = _q_spec(bq, nq, 0), _seq_spec(S, 0)
    full = jax.ShapeDtypeStruct((B * S, MAIN_WIDTH), ACT_DTYPE)
    return pl.pallas_call(
        body, name=name, grid=(B, N_MAIN_PAIRS, nq),
        in_specs=[_q_spec(bq, nq, offs[0]), _seq_spec(S, offs[1]), _seq_spec(S, offs[2]), qs],
        out_specs=[qs, seq, seq], out_shape=[full, full, full],
        scratch_shapes=[pltpu.VMEM((S, PAIR), F32), pltpu.VMEM((S, PAIR), F32)],
        compiler_params=_params('parallel', 'parallel', 'arbitrary'),
    )(*qkv, do)


def _mem_probs(qv, mk):
    s = _dot(qv, mk, NT) * SCALE
    p = jnp.exp(s - jnp.max(s, axis=-1, keepdims=True))
    return p / jnp.sum(p, axis=-1, keepdims=True)


def _mem_fwd(q, q_off, mkv, B, S, name='mem_fwd'):
    M = mkv.shape[0] // B
    bq = _tile(S, 512)
    nq = S // bq

    def body(q_ref, mk_ref, mv_ref, o_ref):
        first = _lane0((bq, PAIR))
        mk, mv = mk_ref[...], mv_ref[...]
        out = [_dot(_mem_probs(qh, mk), mv, NN) for qh in _per_head(q_ref[...])]
        o_ref[...] = _pick(first, out[0], out[1]).astype(ACT_DTYPE)

    return pl.pallas_call(
        body, name=name, grid=(B, N_MEM_PAIRS, nq),
        in_specs=[_q_spec(bq, nq, q_off), _seq_spec(M, 0), _seq_spec(M, N_MEM_PAIRS)],
        out_specs=_q_spec(bq, nq, 0),
        out_shape=jax.ShapeDtypeStruct((B * S, MEM_WIDTH), ACT_DTYPE),
        compiler_params=_params('parallel', 'parallel', 'parallel'),
    )(q, mkv, mkv)


def _mem_bwd(q, q_off, mkv, B, S, do, do_off, name='mem_bwd'):
    M = mkv.shape[0] // B
    bq = _tile(S, 512)
    nq = S // bq

    def body(q_ref, mk_ref, mv_ref, do_ref, dq_ref, dmk_ref, dmv_ref):
        i = pl.program_id(2)

        @pl.when(i == 0)
        def _():
            dmk_ref[...] = jnp.zeros_like(dmk_ref)
            dmv_ref[...] = jnp.zeros_like(dmv_ref)

        qv = q_ref[...]
        dov = do_ref[...]
        mk, mv = mk_ref[...], mv_ref[...]
        first = _lane0((bq, PAIR))
        first_m = _lane0((M, PAIR))
        dqh, dkh, dvh = [], [], []
        for qh, doh in zip(_per_head(qv), _per_head(dov)):
            p = _mem_probs(qh, mk)
            dp = _dot(doh, mv, NT)
            ds = p * (dp - jnp.sum(p * dp, axis=-1, keepdims=True))
            dqh.append(_dot(ds, mk, NN))
            dkh.append(_dot(ds, qv, TN))
            dvh.append(_dot(p, dov, TN))
        dq_ref[...] = (SCALE * _pick(first, dqh[0], dqh[1])).astype(ACT_DTYPE)
        dmk_ref[...] += SCALE * _pick(first_m, dkh[0], dkh[1])
        dmv_ref[...] += _pick(first_m, dvh[0], dvh[1])

    mem_out = jax.ShapeDtypeStruct((B * M, MEM_WIDTH), F32)
    return pl.pallas_call(
        body, name=name, grid=(B, N_MEM_PAIRS, nq),
        in_specs=[_q_spec(bq, nq, q_off), _seq_spec(M, 0), _seq_spec(M, N_MEM_PAIRS), _q_spec(bq, nq, do_off)],
        out_specs=[_q_spec(bq, nq, 0), _seq_spec(M, 0), _seq_spec(M, 0)],
        out_shape=[jax.ShapeDtypeStruct((B * S, MEM_WIDTH), ACT_DTYPE), mem_out, mem_out],
        compiler_params=_params('parallel', 'parallel', 'arbitrary'),
    )(q, mkv, mkv, do)


def _shift_down(u, n):
    t = lax.broadcasted_iota(jnp.int32, u.shape, 0)
    return jnp.where(t >= n, pltpu.roll(u, n, 0), 0.0)


def _shift_up(u, n):
    S = u.shape[0]
    t = lax.broadcasted_iota(jnp.int32, u.shape, 0)
    return jnp.where(t < S - n, pltpu.roll(u, S - n, 0), 0.0)


def _conv(u, u1, u2, w, b):
    return b + w[0:1, :] * u2 + w[1:2, :] * u1 + w[2:3, :] * u


def _conv_specs(S, nf):
    ug = pl.BlockSpec((None, S, LANES), lambda b, j: (b, 0, j))
    uv = pl.BlockSpec((None, S, LANES), lambda b, j: (b, 0, j + nf))
    wg = pl.BlockSpec((3, LANES), lambda b, j: (0, j))
    wv = pl.BlockSpec((3, LANES), lambda b, j: (0, j + nf))
    bg = pl.BlockSpec((1, LANES), lambda b, j: (0, j))
    bv = pl.BlockSpec((1, LANES), lambda b, j: (0, j + nf))
    return ug, uv, wg, wv, bg, bv


def _conv_fwd(u, cw, cb, name='conv_fwd'):
    B, S, F2 = u.shape
    F = F2 // 2
    nf = F // LANES

    def body(ug_ref, uv_ref, wg_ref, wv_ref, bg_ref, bv_ref, y_ref):
        ug = ug_ref[...].astype(F32)
        uv = uv_ref[...].astype(F32)
        gate = _conv(ug, _shift_down(ug, 1), _shift_down(ug, 2), wg_ref[...], bg_ref[...])
        val = _conv(uv, _shift_down(uv, 1), _shift_down(uv, 2), wv_ref[...], bv_ref[...])
        y_ref[...] = (gate / (1.0 + jnp.exp(-gate)) * val).astype(ACT_DTYPE)

    specs = _conv_specs(S, nf)
    return pl.pallas_call(
        body, name=name, grid=(B, nf), in_specs=list(specs), out_specs=specs[0],
        out_shape=jax.ShapeDtypeStruct((B, S, F), ACT_DTYPE),
        compiler_params=_params('parallel', 'parallel'),
    )(u, u, cw, cw, cb, cb)


def _conv_bwd(u, cw, cb, dy, name='conv_bwd'):
    B, S, F2 = u.shape
    F = F2 // 2
    nf = F // LANES

    def body(ug_ref, uv_ref, wg_ref, wv_ref, bg_ref, bv_ref, dy_ref,
             dug_ref, duv_ref, dwg_ref, dwv_ref, dbg_ref, dbv_ref):
        b = pl.program_id(1)

        @pl.when(b == 0)
        def _():
            for r in (dwg_ref, dwv_ref, dbg_ref, dbv_ref):
                r[...] = jnp.zeros_like(r)

        ug = ug_ref[...].astype(F32)
        uv = uv_ref[...].astype(F32)
        ug1, ug2 = _shift_down(ug, 1), _shift_down(ug, 2)
        uv1, uv2 = _shift_down(uv, 1), _shift_down(uv, 2)
        wg, wv = wg_ref[...], wv_ref[...]
        gate = _conv(ug, ug1, ug2, wg, bg_ref[...])
        val = _conv(uv, uv1, uv2, wv, bv_ref[...])
        dyv = dy_ref[...].astype(F32)
        sg = 1.0 / (1.0 + jnp.exp(-gate))
        dval = dyv * (gate * sg)
        dgate = dyv * val * (sg * (1.0 + gate * (1.0 - sg)))

        def back(d, x, x1, x2, w, du_ref, dw_ref, db_ref):
            db_ref[...] += jnp.sum(d, axis=0, keepdims=True)
            dw_ref[...] += jnp.concatenate(
                [jnp.sum(d * x2, axis=0, keepdims=True), jnp.sum(d * x1, axis=0, keepdims=True),
                 jnp.sum(d * x, axis=0, keepdims=True)], axis=0)
            du = w[2:3, :] * d + w[1:2, :] * _shift_up(d, 1) + w[0:1, :] * _shift_up(d, 2)
            du_ref[...] = du.astype(ACT_DTYPE)

        back(dgate, ug, ug1, ug2, wg, dug_ref, dwg_ref, dbg_ref)
        back(dval, uv, uv1, uv2, wv, duv_ref, dwv_ref, dbv_ref)

    def swap(spec_fn):
        return lambda j, b: spec_fn(b, j)

    ug, uv, wg, wv, bg, bv = _conv_specs(S, nf)
    ins = [pl.BlockSpec(s.block_shape, swap(s.index_map)) for s in (ug, uv, wg, wv, bg, bv, ug)]
    outs = [ins[0], ins[0], ins[2], ins[2], ins[4], ins[4]]
    return pl.pallas_call(
        body, name=name, grid=(nf, B), in_specs=ins, out_specs=outs,
        out_shape=[jax.ShapeDtypeStruct((B, S, F), ACT_DTYPE), jax.ShapeDtypeStruct((B, S, F), ACT_DTYPE),
                   jax.ShapeDtypeStruct((3, F), F32), jax.ShapeDtypeStruct((3, F), F32),
                   jax.ShapeDtypeStruct((1, F), F32), jax.ShapeDtypeStruct((1, F), F32)],
        compiler_params=_params('parallel', 'arbitrary'),
    )(u, u, cw, cw, cb, cb, dy)


def _adamw(w, g, m, v, name):
    R, C = w.shape
    tr = R
    if R * C * 4 > (1 << 19) and R % 8 == 0:
        tr = 8
        for t in range(8, R + 1, 8):
            if R % t == 0 and t * C * 4 <= (1 << 19):
                tr = t

    def body(w_ref, g_ref, m_ref, v_ref, d_ref, nm_ref, nv_ref):
        gv = g_ref[...]
        nm = ADAM_B1 * m_ref[...] + (1.0 - ADAM_B1) * gv
        nv = ADAM_B2 * v_ref[...] + (1.0 - ADAM_B2) * (gv * gv)
        m_hat = nm / (1.0 - ADAM_B1 ** ADAM_STEP)
        v_hat = nv / (1.0 - ADAM_B2 ** ADAM_STEP)
        d_ref[...] = -ADAM_LR * (m_hat / (jnp.sqrt(v_hat) + ADAM_EPS) + ADAM_WD * w_ref[...])
        nm_ref[...] = nm
        nv_ref[...] = nv

    blk = pl.BlockSpec((tr, C), lambda i: (i, 0))
    shp = jax.ShapeDtypeStruct((R, C), F32)
    return pl.pallas_call(
        body, name=name, grid=(R // tr,), in_specs=[blk] * 4, out_specs=[blk] * 3, out_shape=[shp] * 3,
        compiler_params=_params('parallel'),
    )(w, g, m, v)


def _my_place():
    return lax.axis_index('x'), lax.axis_index('y'), lax.axis_index('c')


def _other_chips(x, y):
    return [(1 - x, y), (x, 1 - y), (1 - x, 1 - y)]


def _remote(src, dst, send_sem, recv_sem, to):
    return pltpu.make_async_remote_copy(src_ref=src, dst_ref=dst, send_sem=send_sem, recv_sem=recv_sem,
                                        device_id=to, device_id_type=MESH)


def _gather_chips(p, name='gather_chips'):
    R, C = p.shape
    Rh = R // 2
    assert R % 2 == 0

    def body(p_ref, o_ref, send_sems, recv_sems, local_sem):
        x, y, c = _my_place()
        chip = 2 * x + y
        sibling = (x, y, 1 - c)
        others = _other_chips(x, y)

        def rows(k, half):
            return o_ref.at[k, pl.ds(half * Rh, Rh), :]

        mine = pltpu.make_async_copy(p_ref, o_ref.at[chip], local_sem)
        mine.start()
        first = []
        for j, (ox, oy) in enumerate(others):
            cp = _remote(p_ref.at[pl.ds(c * Rh, Rh), :], rows(chip, c), send_sems.at[j], recv_sems.at[j], (ox, oy, c))
            cp.start()
            first.append(cp)
        passed = []
        for j, (ox, oy) in enumerate(others):
            k = 2 * ox + oy
            _remote(rows(k, c), rows(k, c), send_sems.at[j], recv_sems.at[j], (ox, oy, c)).wait_recv()
            fw = _remote(rows(k, c), rows(k, c), send_sems.at[3 + j], recv_sems.at[3 + j], sibling)
            fw.start()
            passed.append(fw)
        for j, (ox, oy) in enumerate(others):
            k = 2 * ox + oy
            _remote(rows(k, 1 - c), rows(k, 1 - c), send_sems.at[3 + j], recv_sems.at[3 + j], sibling).wait_recv()
        for cp in first + passed:
            cp.wait_send()
        mine.wait()

    return pl.pallas_call(
        body, name=name, in_specs=[ANY], out_specs=ANY,
        out_shape=jax.ShapeDtypeStruct((N_CHIPS, R, C), p.dtype),
        scratch_shapes=[pltpu.SemaphoreType.DMA((6,)), pltpu.SemaphoreType.DMA((6,)), pltpu.SemaphoreType.DMA],
    )(p)


def _swap_halves(g, name='swap_halves'):
    n, R, C = g.shape
    Rh = R // 2

    def body(g_ref, o_ref, send_sem, recv_sem):
        x, y, c = _my_place()
        cp = _remote(g_ref.at[:, pl.ds((1 - c) * Rh, Rh), :], o_ref, send_sem, recv_sem, (x, y, 1 - c))
        cp.start()
        cp.wait()

    return pl.pallas_call(
        body, name=name, in_specs=[ANY], out_specs=ANY,
        out_shape=jax.ShapeDtypeStruct((n, Rh, C), g.dtype),
        scratch_shapes=[pltpu.SemaphoreType.DMA, pltpu.SemaphoreType.DMA],
    )(g)


def _add_half(g, other, c_arr, name='add_half'):
    n, R, C = g.shape
    Rh = R // 2
    tr = _tile(Rh, 512, 8)
    nrt = Rh // tr

    def body(c_ref, g_ref, o_ref, q_ref):
        q_ref[...] = g_ref[...] + o_ref[...]

    gs = pltpu.PrefetchScalarGridSpec(
        num_scalar_prefetch=1, grid=(n, nrt),
        in_specs=[pl.BlockSpec((None, tr, C), lambda j, r, c_ref: (j, c_ref[0] * nrt + r, 0)),
                  pl.BlockSpec((None, tr, C), lambda j, r, c_ref: (j, r, 0))],
        out_specs=pl.BlockSpec((None, tr, C), lambda j, r, c_ref: (j, r, 0)))
    return pl.pallas_call(
        body, name=name, grid_spec=gs, out_shape=jax.ShapeDtypeStruct((n, Rh, C), F32),
        compiler_params=_params('parallel', 'parallel'),
    )(c_arr, g, other)


def _scatter_chips(q, name='scatter_chips'):
    n, Rh, C = q.shape

    def body(q_ref, o_ref, send_sems, recv_sems):
        x, y, c = _my_place()
        cps = []
        for j, (ox, oy) in enumerate(_other_chips(x, y)):
            cp = _remote(q_ref.at[2 * ox + oy], o_ref.at[j], send_sems.at[j], recv_sems.at[j], (ox, oy, c))
            cp.start()
            cps.append(cp)
        for cp in cps:
            cp.wait_recv()
        for cp in cps:
            cp.wait_send()

    return pl.pallas_call(
        body, name=name, in_specs=[ANY], out_specs=ANY,
        out_shape=jax.ShapeDtypeStruct((3, Rh, C), q.dtype),
        scratch_shapes=[pltpu.SemaphoreType.DMA((3,)), pltpu.SemaphoreType.DMA((3,))],
    )(q)


def _add_chips(q, got, chip_arr, name='add_chips'):
    n, Rh, C = q.shape
    tr = _tile(Rh, 512, 8)

    def body(k_ref, q_ref, gx_ref, gy_ref, gxy_ref, o_ref):
        o_ref[...] = (q_ref[...] + gxy_ref[...]) + (gx_ref[...] + gy_ref[...])

    def got_spec(j):
        return pl.BlockSpec((None, tr, C), lambda r, k_ref: (j, r, 0))

    gs = pltpu.PrefetchScalarGridSpec(
        num_scalar_prefetch=1, grid=(Rh // tr,),
        in_specs=[pl.BlockSpec((None, tr, C), lambda r, k_ref: (k_ref[0], r, 0)), got_spec(0), got_spec(1), got_spec(2)],
        out_specs=pl.BlockSpec((tr, C), lambda r, k_ref: (r, 0)))
    return pl.pallas_call(
        body, name=name, grid_spec=gs, out_shape=jax.ShapeDtypeStruct((Rh, C), F32),
        compiler_params=_params('parallel'),
    )(chip_arr, q, got, got, got)


def _join_halves(r, name='join_halves'):
    Rh, C = r.shape

    def body(r_ref, o_ref, send_sem, recv_sem, local_sem):
        x, y, c = _my_place()
        mine = pltpu.make_async_copy(r_ref, o_ref.at[c], local_sem)
        mine.start()
        cp = _remote(r_ref, o_ref.at[c], send_sem, recv_sem, (x, y, 1 - c))
        cp.start()
        cp.wait()
        mine.wait()

    return pl.pallas_call(
        body, name=name, in_specs=[ANY], out_specs=ANY,
        out_shape=jax.ShapeDtypeStruct((2, Rh, C), r.dtype),
        scratch_shapes=[pltpu.SemaphoreType.DMA, pltpu.SemaphoreType.DMA, pltpu.SemaphoreType.DMA],
    )(r)


def _pad_rows(flat, mult):
    n = flat.shape[0]
    rows = -(-n // PACK_COLS)
    rows = -(-rows // mult) * mult
    return jnp.pad(flat, (0, rows * PACK_COLS - n)).reshape(rows, PACK_COLS)


def _pack_weights(shards):
    parts = []
    for name in PARAM_NAMES:
        if PARAM_SHARD_AXIS[name] is None:
            continue
        w = shards[name]
        if name in F32_GATHERED:
            parts.append(lax.bitcast_convert_type(w, jnp.bfloat16).reshape(-1))
        else:
            parts.append(w.astype(jnp.bfloat16).reshape(-1))
    return _pad_rows(jnp.concatenate(parts), 32)


def _unpack_weights(gathered, shards):
    flat = gathered.reshape(N_CHIPS, -1)
    full = {}
    off = 0
    for name in PARAM_NAMES:
        axis = PARAM_SHARD_AXIS[name]
        if axis is None:
            continue
        shape = shards[name].shape
        n = math.prod(shape)
        if name in F32_GATHERED:
            pieces = [lax.bitcast_convert_type(flat[k, off:off + 2 * n].reshape(shape + (2,)), F32)
                      for k in range(N_CHIPS)]
            off += 2 * n
        else:
            pieces = [flat[k, off:off + n].reshape(shape) for k in range(N_CHIPS)]
            off += n
        full[name] = jnp.concatenate(pieces, axis=axis)
    return full


def _pack_grads(grads, shards):
    slots = []
    for k in range(N_CHIPS):
        parts = []
        for name in PARAM_NAMES:
            axis = PARAM_SHARD_AXIS[name]
            g = grads[name]
            if axis is not None:
                size = shards[name].shape[axis]
                g = lax.slice_in_dim(g, k * size, (k + 1) * size, axis=axis)
            parts.append(g.astype(F32).reshape(-1))
        slots.append(_pad_rows(jnp.concatenate(parts), 512))
    return jnp.stack(slots)


def _unpack_grads(packed, shards):
    flat = packed.reshape(-1)
    out = {}
    off = 0
    for name in PARAM_NAMES:
        shape = shards[name].shape
        n = math.prod(shape)
        out[name] = flat[off:off + n].reshape(shape)
        off += n
    return out


def _gate_rows(t, B, S):
    return t.reshape(B, S, N_MAIN_HEADS).transpose(0, 2, 1).reshape(B * N_MAIN_HEADS, S)


def _gate_cols(t, B, S):
    return t.reshape(B, N_MAIN_HEADS, S).transpose(0, 2, 1).reshape(B * S, N_MAIN_HEADS)


def _mem_kv_fwd(mem2, g, w, tag):
    hm = _rms_fwd(mem2, g, name=f'rms_mem_{tag}')
    mkv = _matmul(hm, w, 'nn', ACT_DTYPE, name=f'mm_memkv_{tag}')
    return hm, mkv


def _mem_kv_bwd(mem2, g, w, hm, dmk, dmv, tag):
    dmkv = jnp.concatenate([dmk, dmv], axis=1)
    dw = _matmul(hm, dmkv, 'tn', F32, name=f'mm_memkv_dw_{tag}')
    dhm = _matmul(dmkv, w, 'nt', F32, name=f'mm_memkv_dx_{tag}')
    _, dg = _rms_bwd(mem2, g, dhm, None, name=f'rms_mem_bwd_{tag}')
    return dw, dg


def _ffn_fwd(x, g, w_up, cw, cb, w_down, B, S, tag):
    T = x.shape[0]
    h2 = _rms_fwd(x, g, name=f'rms_ffn_{tag}')
    u = _matmul(h2, w_up, 'nn', ACT_DTYPE, name=f'mm_up_{tag}')
    y = _conv_fwd(u.reshape(B, S, -1), cw, cb, name=f'conv_fwd_{tag}').reshape(T, -1)
    x2 = _matmul(y, w_down, 'nn', F32, res=x, name=f'mm_down_{tag}')
    return x2, (h2, u, y)


def _ffn_bwd(dx2, x, g, w_up, cw, cb, w_down, saved, B, S, tag):
    h2, u, y = saved
    T = x.shape[0]
    dy = _matmul(dx2, w_down, 'nt', ACT_DTYPE, name=f'mm_down_dx_{tag}')
    dw_down = _matmul(y, dx2, 'tn', F32, name=f'mm_down_dw_{tag}')
    dug, duv, dcwg, dcwv, dcbg, dcbv = _conv_bwd(u.reshape(B, S, -1), cw, cb, dy.reshape(B, S, -1),
                                                  name=f'conv_bwd_{tag}')
    du = jnp.concatenate([dug.reshape(T, -1), duv.reshape(T, -1)], axis=1)
    dh2 = _matmul(du, w_up, 'nt', F32, name=f'mm_up_dx_{tag}')
    dw_up = _matmul(h2, du, 'tn', F32, name=f'mm_up_dw_{tag}')
    dx, dg = _rms_bwd(x, g, dh2, dx2, name=f'rms_ffn_bwd_{tag}')
    dcw = jnp.concatenate([dcwg, dcwv], axis=1)
    dcb = jnp.concatenate([dcbg, dcbv], axis=1)
    return dx, dg, dw_up, dcw, dcb, dw_down


def _step(x, mem, tgt, W):
    B, S, D = x.shape
    T = B * S
    x0 = x.reshape(T, D)
    mem2 = mem.reshape(-1, D)
    tgt2 = tgt.reshape(T, D)
    row = lambda v: v.reshape(1, -1)
    q3 = 3 * MAIN_WIDTH

    w_in_a = W['w_in_a'][0]
    wa_main = jnp.concatenate([w_in_a[:, :q3], w_in_a[:, q3 + N_MAIN_HEADS:]], axis=1)
    wa_gate = jnp.pad(w_in_a[:, q3:q3 + N_MAIN_HEADS], ((0, 0), (0, LANES - N_MAIN_HEADS)))
    w_in_b = W['w_in_b'][0]
    bcol = jnp.tile(W['b_f_a'][0], B).reshape(B * N_MAIN_HEADS, 1)
    nkb = S // min(FOX_BLOCK, S)

    h1a = _rms_fwd(x0, row(W['ln_mix_g'][0]), name='rms_mix_a')
    pa = _matmul(h1a, wa_main, 'nn', ACT_DTYPE, name='mm_in_a')
    flog = _matmul(h1a, wa_gate, 'nn', F32, name='mm_gate_a')
    qkv_a = (pa, pa, pa)
    offs_a = (0, N_MAIN_PAIRS, 2 * N_MAIN_PAIRS)
    qm_off_a = 3 * N_MAIN_PAIRS
    zt = _gate_rows(flog[:, :N_MAIN_HEADS], B, S)
    cum = _gate_fwd(zt, bcol)
    ccol = cum.reshape(B * N_MAIN_HEADS, S, 1)
    crow = cum.reshape(B * N_MAIN_HEADS, nkb, 1, S // nkb)
    oa, lse = _fox_fwd(qkv_a, offs_a, B, S, ccol, crow)
    hma, mkva = _mem_kv_fwd(mem2, row(W['ln_mem_g'][0]), W['w_memkv'][0], 'a')
    oma = _mem_fwd(pa, qm_off_a, mkva, B, S, name='mem_fwd_a')
    ocat_a = jnp.concatenate([oa, oma], axis=1)
    x1 = _matmul(ocat_a, W['w_out'][0], 'nn', F32, res=x0, name='mm_out_a')
    x2, ffn_a = _ffn_fwd(x1, row(W['ln_ffn_g'][0]), W['w_up'][0], W['conv_w'][0], row(W['conv_b'][0]),
                         W['w_down'][0], B, S, 'a')
    hkv = _rms_fwd(x2, row(W['ln_kv_g']), name='rms_kv')
    kvs = _matmul(hkv, W['w_kv'], 'nn', ACT_DTYPE, name='mm_kv')
    h1b = _rms_fwd(x2, row(W['ln_mix_g'][1]), name='rms_mix_b')
    pb = _matmul(h1b, w_in_b, 'nn', ACT_DTYPE, name='mm_in_b')
    qkv_b = (pb, kvs, kvs)
    offs_b = (0, 0, N_MAIN_PAIRS)
    qm_off_b = N_MAIN_PAIRS
    ob = _sb_fwd(qkv_b, offs_b, B, S)
    hmb, mkvb = _mem_kv_fwd(mem2, row(W['ln_mem_g'][1]), W['w_memkv'][1], 'b')
    omb = _mem_fwd(pb, qm_off_b, mkvb, B, S, name='mem_fwd_b')
    ocat_b = jnp.concatenate([ob, omb], axis=1)
    x3 = _matmul(ocat_b, W['w_out'][1], 'nn', F32, res=x2, name='mm_out_b')
    x4, ffn_b = _ffn_fwd(x3, row(W['ln_ffn_g'][1]), W['w_up'][1], W['conv_w'][1], row(W['conv_b'][1]),
                         W['w_down'][1], B, S, 'b')
    loss, dx4, d_final_g = _final_loss(x4, row(W['final_g']), tgt2)

    dx3, dg_ffn_b, dw_up_b, dcw_b, dcb_b, dw_down_b = _ffn_bwd(
        dx4, x3, row(W['ln_ffn_g'][1]), W['w_up'][1], W['conv_w'][1], row(W['conv_b'][1]), W['w_down'][1],
        ffn_b, B, S, 'b')
    docat = _matmul(dx3, W['w_out'][1], 'nt', ACT_DTYPE, name='mm_out_dx_b')
    dw_out_b = _matmul(ocat_b, dx3, 'tn', F32, name='mm_out_dw_b')
    dqb, dkb, dvb = _sb_bwd(qkv_b, offs_b, B, S, docat)
    dqmb, dmkb, dmvb = _mem_bwd(pb, qm_off_b, mkvb, B, S, docat, N_MAIN_PAIRS, name='mem_bwd_b')
    dw_memkv_b, dg_mem_b = _mem_kv_bwd(mem2, row(W['ln_mem_g'][1]), W['w_memkv'][1], hmb, dmkb, dmvb, 'b')
    dpb = jnp.concatenate([dqb, dqmb], axis=1)
    dh1b = _matmul(dpb, w_in_b, 'nt', F32, name='mm_in_dx_b')
    dw_in_b = _matmul(h1b, dpb, 'tn', F32, name='mm_in_dw_b')
    dx2, dg_mix_b = _rms_bwd(x2, row(W['ln_mix_g'][1]), dh1b, dx3, name='rms_mix_bwd_b')
    dkvs = jnp.concatenate([dkb, dvb], axis=1)
    dhkv = _matmul(dkvs, W['w_kv'], 'nt', F32, name='mm_kv_dx')
    dw_kv = _matmul(hkv, dkvs, 'tn', F32, name='mm_kv_dw')
    dx2, dg_kv = _rms_bwd(x2, row(W['ln_kv_g']), dhkv, dx2, name='rms_kv_bwd')

    dx1, dg_ffn_a, dw_up_a, dcw_a, dcb_a, dw_down_a = _ffn_bwd(
        dx2, x1, row(W['ln_ffn_g'][0]), W['w_up'][0], W['conv_w'][0], row(W['conv_b'][0]), W['w_down'][0],
        ffn_a, B, S, 'a')
    docat = _matmul(dx1, W['w_out'][0], 'nt', ACT_DTYPE, name='mm_out_dx_a')
    dw_out_a = _matmul(ocat_a, dx1, 'tn', F32, name='mm_out_dw_a')
    dqa, dka, dva, dccol, dcrow = _fox_bwd(qkv_a, offs_a, B, S, ccol, crow, oa, lse, docat)
    dzt, dbrow = _gate_bwd(zt, bcol, dccol.reshape(B * N_MAIN_HEADS, S) + dcrow.reshape(B * N_MAIN_HEADS, S))
    dqma, dmka, dmva = _mem_bwd(pa, qm_off_a, mkva, B, S, docat, N_MAIN_PAIRS, name='mem_bwd_a')
    dw_memkv_a, dg_mem_a = _mem_kv_bwd(mem2, row(W['ln_mem_g'][0]), W['w_memkv'][0], hma, dmka, dmva, 'a')
    dpa = jnp.concatenate([dqa, dka, dva, dqma], axis=1)
    dflog = jnp.pad(_gate_cols(dzt, B, S), ((0, 0), (0, LANES - N_MAIN_HEADS)))
    dh1a = _matmul(dpa, wa_main, 'nt', F32, name='mm_in_dx_a')
    dh1a = _matmul(dflog, wa_gate, 'nt', F32, res=dh1a, name='mm_gate_dx_a')
    dwa_main = _matmul(h1a, dpa, 'tn', F32, name='mm_in_dw_a')
    dwa_gate = _matmul(h1a, dflog, 'tn', F32, name='mm_gate_dw_a')
    dx0, dg_mix_a = _rms_bwd(x0, row(W['ln_mix_g'][0]), dh1a, dx1, name='rms_mix_bwd_a')

    dw_in_a = jnp.concatenate([dwa_main[:, :q3], dwa_gate[:, :N_MAIN_HEADS], dwa_main[:, q3:]], axis=1)
    grads = {
        'ln_mix_g': jnp.concatenate([dg_mix_a, dg_mix_b], axis=0),
        'w_in_a': dw_in_a[None],
        'b_f_a': dbrow.reshape(B, N_MAIN_HEADS).sum(axis=0)[None],
        'w_in_b': dw_in_b[None],
        'ln_kv_g': dg_kv[0],
        'w_kv': dw_kv,
        'ln_mem_g': jnp.concatenate([dg_mem_a, dg_mem_b], axis=0),
        'w_memkv': jnp.stack([dw_memkv_a, dw_memkv_b]),
        'w_out': jnp.stack([dw_out_a, dw_out_b]),
        'ln_ffn_g': jnp.concatenate([dg_ffn_a, dg_ffn_b], axis=0),
        'w_up': jnp.stack([dw_up_a, dw_up_b]),
        'conv_w': jnp.stack([dcw_a, dcw_b]),
        'conv_b': jnp.concatenate([dcb_a, dcb_b], axis=0),
        'w_down': jnp.stack([dw_down_a, dw_down_b]),
        'final_g': d_final_g[0],
    }
    return loss, dx0.reshape(B, S, D), grads


def _reduce_grads(packed):
    x, y, c = _my_place()
    c_arr = jnp.reshape(c, (1,)).astype(jnp.int32)
    chip_arr = jnp.reshape(2 * x + y, (1,)).astype(jnp.int32)
    other = _swap_halves(packed)
    q = _add_half(packed, other, c_arr)
    got = _scatter_chips(q)
    mine = _add_chips(q, got, chip_arr)
    both = _join_halves(mine)
    return both.reshape(-1, PACK_COLS)


def kernel(x, mem, ln_mix_g, w_in_a, b_f_a, w_in_b, ln_kv_g, w_kv, ln_mem_g, w_memkv, w_out, ln_ffn_g, w_up, conv_w, conv_b, w_down, final_g, loss_target, m_ln_mix_g, m_w_in_a, m_b_f_a, m_w_in_b, m_ln_kv_g, m_w_kv, m_ln_mem_g, m_w_memkv, m_w_out, m_ln_ffn_g, m_w_up, m_conv_w, m_conv_b, m_w_down, m_final_g, v_ln_mix_g, v_w_in_a, v_b_f_a, v_w_in_b, v_ln_kv_g, v_w_kv, v_ln_mem_g, v_w_memkv, v_w_out, v_ln_ffn_g, v_w_up, v_conv_w, v_conv_b, v_w_down, v_final_g):
    shards = dict(ln_mix_g=ln_mix_g, w_in_a=w_in_a, b_f_a=b_f_a, w_in_b=w_in_b, ln_kv_g=ln_kv_g, w_kv=w_kv,
                  ln_mem_g=ln_mem_g, w_memkv=w_memkv, w_out=w_out, ln_ffn_g=ln_ffn_g, w_up=w_up, conv_w=conv_w,
                  conv_b=conv_b, w_down=w_down, final_g=final_g)
    moments_m = dict(ln_mix_g=m_ln_mix_g, w_in_a=m_w_in_a, b_f_a=m_b_f_a, w_in_b=m_w_in_b, ln_kv_g=m_ln_kv_g,
                     w_kv=m_w_kv, ln_mem_g=m_ln_mem_g, w_memkv=m_w_memkv, w_out=m_w_out, ln_ffn_g=m_ln_ffn_g,
                     w_up=m_w_up, conv_w=m_conv_w, conv_b=m_conv_b, w_down=m_w_down, final_g=m_final_g)
    moments_v = dict(ln_mix_g=v_ln_mix_g, w_in_a=v_w_in_a, b_f_a=v_b_f_a, w_in_b=v_w_in_b, ln_kv_g=v_ln_kv_g,
                     w_kv=v_w_kv, ln_mem_g=v_ln_mem_g, w_memkv=v_w_memkv, w_out=v_w_out, ln_ffn_g=v_ln_ffn_g,
                     w_up=v_w_up, conv_w=v_conv_w, conv_b=v_conv_b, w_down=v_w_down, final_g=v_final_g)

    gathered = _gather_chips(_pack_weights(shards))
    W = dict(shards)
    W.update(_unpack_weights(gathered, shards))

    loss_part, grad_x, grads = _step(x, mem, loss_target, W)
    loss = lax.psum(loss_part[0, 0], ('x', 'y', 'c'))

    g_shard = _unpack_grads(_reduce_grads(_pack_grads(grads, shards)), shards)

    deltas, new_m, new_v = {}, {}, {}
    for name in PARAM_NAMES:
        w = shards[name]
        two_d = (-1, w.shape[-1])
        d, nm, nv = _adamw(w.reshape(two_d), g_shard[name].reshape(two_d), moments_m[name].reshape(two_d),
                           moments_v[name].reshape(two_d), name=f'adamw_{name}')
        deltas[name], new_m[name], new_v[name] = d.reshape(w.shape), nm.reshape(w.shape), nv.reshape(w.shape)

    return (loss, grad_x, *[g_shard[n] for n in PARAM_NAMES], *[deltas[n] for n in PARAM_NAMES],
            *[new_m[n] for n in PARAM_NAMES], *[new_v[n] for n in PARAM_NAMES])
```

```python
import functools
import math

import jax
import jax.numpy as jnp
from jax import lax
from jax.experimental import pallas as pl
from jax.experimental.pallas import tpu as pltpu

F32 = jnp.float32
MXU_DTYPE = jnp.bfloat16
ACT_DTYPE = jnp.bfloat16

HEAD_DIM = 64
N_MAIN_HEADS = 12
N_MEM_HEADS = 4
MAIN_WIDTH = N_MAIN_HEADS * HEAD_DIM
MEM_WIDTH = N_MEM_HEADS * HEAD_DIM
EPS = 1e-6
SCALE = HEAD_DIM ** -0.5
NEG_BIG = -1e30
LANES = 128
PACK_COLS = 1024
N_CHIPS = 4

ADAM_LR = 0.001
ADAM_B1 = 0.9
ADAM_B2 = 0.999
ADAM_EPS = 1e-08
ADAM_WD = 0.01
ADAM_STEP = 10

MESH = pl.DeviceIdType.MESH
ANY = pl.BlockSpec(memory_space=pl.ANY)

PARAM_SHARD_AXIS = {
    'ln_mix_g': None, 'w_in_a': 2, 'b_f_a': None, 'w_in_b': 1, 'ln_kv_g': None, 'w_kv': 1,
    'ln_mem_g': None, 'w_memkv': 1, 'w_out': 1, 'ln_ffn_g': None, 'w_up': 2, 'conv_w': 2,
    'conv_b': None, 'w_down': 1, 'final_g': None,
}
PARAM_NAMES = list(PARAM_SHARD_AXIS)
F32_GATHERED = ('conv_w',)


def _tile(n, pref, unit=LANES):
    if n <= pref:
        return n
    best = None
    for t in range(unit, pref + 1, unit):
        if n % t == 0:
            best = t
    assert best is not None, (n, pref)
    return best


MM_ACC_ELEMS = 768 * 1024
MM_VMEM_MB = 48


def _out_tiles(M, N):
    def divisors(n, cap):
        if n <= LANES:
            return [n]
        return [t for t in range(LANES, min(n, cap) + 1, LANES) if n % t == 0]

    best = None
    for tm in divisors(M, 1024):
        for tn in divisors(N, 2048):
            if tm * tn <= MM_ACC_ELEMS and (best is None or (tm * tn, tn) > (best[0] * best[1], best[1])):
                best = (tm, tn)
    assert best is not None, (M, N)
    return best


def _params(*sem, vmem_mb=None):
    kw = {}
    if sem:
        kw['dimension_semantics'] = sem
    if vmem_mb is not None:
        kw['vmem_limit_bytes'] = vmem_mb * 1024 * 1024
    return pltpu.CompilerParams(**kw)


def _dot(a, b, dims):
    return lax.dot_general(a.astype(MXU_DTYPE), b.astype(MXU_DTYPE), (dims, ((), ())),
                           preferred_element_type=F32)


NN = ((1,), (0,))
NT = ((1,), (1,))
TN = ((0,), (0,))


def _matmul(a, b, mode, out_dtype, res=None, slots=1, name='mm'):
    if mode == 'nn':
        (M, K), (K2, N) = a.shape, b.shape
    elif mode == 'nt':
        (M, K), (N, K2) = a.shape, b.shape
    else:
        (K, M), (K2, N) = a.shape, b.shape
    assert K == K2 and N % slots == 0, (a.shape, b.shape, mode, slots)
    slot_cols = N // slots
    tm, tn = _out_tiles(M, slot_cols)
    per_slot = slot_cols // tn
    tk = _tile(K, 1024)
    nk = K // tk
    dims = {'nn': NN, 'nt': NT, 'tn': TN}[mode]

    def body(*refs):
        if res is None:
            a_ref, b_ref, o_ref, acc = refs
        else:
            a_ref, b_ref, r_ref, o_ref, acc = refs
        k = pl.program_id(2)

        @pl.when(k == 0)
        def _():
            acc[...] = jnp.zeros_like(acc)

        acc[...] += _dot(a_ref[...], b_ref[...], dims)

        @pl.when(k == nk - 1)
        def _():
            out = acc[...]
            if res is not None:
                out = out + r_ref[...]
            o_ref[...] = out.astype(out_dtype)

    if mode == 'tn':
        a_spec = pl.BlockSpec((tk, tm), lambda i, j, k: (k, i))
    else:
        a_spec = pl.BlockSpec((tm, tk), lambda i, j, k: (i, k))
    if mode == 'nt':
        b_spec = pl.BlockSpec((tn, tk), lambda i, j, k: (j, k))
    else:
        b_spec = pl.BlockSpec((tk, tn), lambda i, j, k: (k, j))
    if slots == 1:
        o_spec = pl.BlockSpec((tm, tn), lambda i, j, k: (i, j))
        out_shape = jax.ShapeDtypeStruct((M, N), out_dtype)
    else:
        assert res is None
        o_spec = pl.BlockSpec((None, tm, tn), lambda i, j, k: (j // per_slot, i, j % per_slot))
        out_shape = jax.ShapeDtypeStruct((slots, M, slot_cols), out_dtype)
    in_specs = [a_spec, b_spec] + ([o_spec] if res is not None else [])
    args = (a, b) + ((res,) if res is not None else ())
    return pl.pallas_call(
        body, name=name, grid=(M // tm, N // tn, nk),
        in_specs=in_specs, out_specs=o_spec,
        out_shape=out_shape,
        scratch_shapes=[pltpu.VMEM((tm, tn), F32)],
        compiler_params=_params('parallel', 'parallel', 'arbitrary', vmem_mb=MM_VMEM_MB),
    )(*args)


def _rms_fwd(x, g, name):
    T, D = x.shape
    tr = _tile(T, 512)

    def body(x_ref, g_ref, o_ref):
        xv = x_ref[...]
        r = lax.rsqrt(jnp.mean(xv * xv, axis=-1, keepdims=True) + EPS)
        o_ref[...] = (xv * r * g_ref[...]).astype(ACT_DTYPE)

    return pl.pallas_call(
        body, name=name, grid=(T // tr,),
        in_specs=[pl.BlockSpec((tr, D), lambda i: (i, 0)), pl.BlockSpec((1, D), lambda i: (0, 0))],
        out_specs=pl.BlockSpec((tr, D), lambda i: (i, 0)),
        out_shape=jax.ShapeDtypeStruct((T, D), ACT_DTYPE),
        compiler_params=_params('parallel'),
    )(x, g)


def _rms_bwd(x, g, dh, dres, name):
    T, D = x.shape
    tr = _tile(T, 512)
    want_dx = dres is not None

    def body(*refs):
        if want_dx:
            x_ref, g_ref, dh_ref, dres_ref, dx_ref, dg_ref = refs
        else:
            x_ref, g_ref, dh_ref, dg_ref = refs
        i = pl.program_id(0)

        @pl.when(i == 0)
        def _():
            dg_ref[...] = jnp.zeros_like(dg_ref)

        xv = x_ref[...]
        dhv = dh_ref[...].astype(F32)
        r = lax.rsqrt(jnp.mean(xv * xv, axis=-1, keepdims=True) + EPS)
        n = xv * r
        dg_ref[...] += jnp.sum(dhv * n, axis=0, keepdims=True)
        if want_dx:
            dn = dhv * g_ref[...]
            dx = r * (dn - n * jnp.mean(dn * n, axis=-1, keepdims=True))
            dx_ref[...] = dres_ref[...] + dx

    row = pl.BlockSpec((tr, D), lambda i: (i, 0))
    vec = pl.BlockSpec((1, D), lambda i: (0, 0))
    if want_dx:
        return pl.pallas_call(
            body, name=name, grid=(T // tr,),
            in_specs=[row, vec, row, row], out_specs=[row, vec],
            out_shape=[jax.ShapeDtypeStruct((T, D), F32), jax.ShapeDtypeStruct((1, D), F32)],
            compiler_params=_params('arbitrary'),
        )(x, g, dh, dres)
    dg = pl.pallas_call(
        body, name=name, grid=(T // tr,),
        in_specs=[row, vec, row], out_specs=vec,
        out_shape=jax.ShapeDtypeStruct((1, D), F32),
        compiler_params=_params('arbitrary'),
    )(x, g, dh)
    return None, dg


def _final_loss(x, g, tgt, name='final_loss'):
    T, D = x.shape
    tr = _tile(T, 512)

    def body(x_ref, g_ref, t_ref, loss_ref, dx_ref, dg_ref):
        i = pl.program_id(0)

        @pl.when(i == 0)
        def _():
            loss_ref[...] = jnp.zeros_like(loss_ref)
            dg_ref[...] = jnp.zeros_like(dg_ref)

        xv = x_ref[...]
        gv = g_ref[...]
        r = lax.rsqrt(jnp.mean(xv * xv, axis=-1, keepdims=True) + EPS)
        n = xv * r
        e = n * gv - t_ref[...]
        per_tok = jnp.mean(e * e, axis=-1, keepdims=True)
        loss_ref[...] += 0.5 * jnp.sum(per_tok, axis=0, keepdims=True)
        dy = e * (1.0 / D)
        dg_ref[...] += jnp.sum(dy * n, axis=0, keepdims=True)
        dn = dy * gv
        dx_ref[...] = r * (dn - n * jnp.mean(dn * n, axis=-1, keepdims=True))

    row = pl.BlockSpec((tr, D), lambda i: (i, 0))
    vec = pl.BlockSpec((1, D), lambda i: (0, 0))
    one = pl.BlockSpec((1, 1), lambda i: (0, 0))
    return pl.pallas_call(
        body, name=name, grid=(T // tr,),
        in_specs=[row, vec, row], out_specs=[one, row, vec],
        out_shape=[jax.ShapeDtypeStruct((1, 1), F32), jax.ShapeDtypeStruct((T, D), F32),
                   jax.ShapeDtypeStruct((1, D), F32)],
        compiler_params=_params('arbitrary'),
    )(x, g, tgt)


def _log_sigmoid(z):
    return jnp.minimum(z, 0.0) - jnp.log(1.0 + jnp.exp(-jnp.abs(z)))


def _tri(n, rel):
    j = lax.broadcasted_iota(jnp.int32, (n, n), 0)
    s = lax.broadcasted_iota(jnp.int32, (n, n), 1)
    return rel(j, s).astype(MXU_DTYPE)


def _split_dot(x, tri, terms):
    if MXU_DTYPE == F32:
        return jnp.dot(x, tri, preferred_element_type=F32)
    out = None
    rem = x
    for _ in range(terms):
        piece = rem.astype(MXU_DTYPE)
        part = jnp.dot(piece, tri, preferred_element_type=F32)
        out = part if out is None else out + part
        rem = rem - piece.astype(F32)
    return out


def _gate_fwd(zt, bcol, name='gate_fwd'):
    BH, S = zt.shape
    nb = S // LANES

    def body(z_ref, b_ref, c_ref):
        tri = _tri(LANES, lambda j, s: j <= s)
        carry = jnp.zeros((BH, 1), F32)
        for i in range(nb):
            sl = slice(i * LANES, (i + 1) * LANES)
            logf = _log_sigmoid(z_ref[:, sl] + b_ref[...])
            cs = _split_dot(logf, tri, 3) + carry
            c_ref[:, sl] = cs
            carry = cs[:, LANES - 1:LANES]

    return pl.pallas_call(body, name=name, out_shape=jax.ShapeDtypeStruct((BH, S), F32))(zt, bcol)


def _gate_bwd(zt, bcol, dc, name='gate_bwd'):
    BH, S = zt.shape
    nb = S // LANES

    def body(z_ref, b_ref, dc_ref, dz_ref, db_ref):
        tri = _tri(LANES, lambda j, s: j >= s)
        carry = jnp.zeros((BH, 1), F32)
        dsum = jnp.zeros((BH, 1), F32)
        for i in reversed(range(nb)):
            sl = slice(i * LANES, (i + 1) * LANES)
            rs = _split_dot(dc_ref[:, sl], tri, 3) + carry
            carry = rs[:, 0:1]
            z = z_ref[:, sl] + b_ref[...]
            dz = rs * (1.0 - 1.0 / (1.0 + jnp.exp(-z)))
            dz_ref[:, sl] = dz
            dsum = dsum + jnp.sum(dz, axis=-1, keepdims=True)
        db_ref[...] = dsum

    return pl.pallas_call(
        body, name=name,
        out_shape=[jax.ShapeDtypeStruct((BH, S), F32), jax.ShapeDtypeStruct((BH, 1), F32)],
    )(zt, bcol, dc)


FOX_BLOCK = 256


PAIR = 2 * HEAD_DIM
N_MAIN_PAIRS = N_MAIN_HEADS // 2
N_MEM_PAIRS = N_MEM_HEADS // 2


def _lane0(shape):
    return lax.broadcasted_iota(jnp.int32, shape, len(shape) - 1) < HEAD_DIM


def _per_head(x):
    first = _lane0(x.shape)
    zero = jnp.zeros_like(x)
    return jnp.where(first, x, zero), jnp.where(first, zero, x)


def _pick(first, a, b):
    return jnp.where(first, a, b)


def _q_spec(bq, nq, off):
    return pl.BlockSpec((bq, PAIR), lambda b, j, i: (b * nq + i, off + j))


def _seq_spec(S, off):
    return pl.BlockSpec((S, PAIR), lambda b, j, i: (b, off + j))


def _gate_specs(bq, nk):
    col = pl.BlockSpec((2, bq, 1), lambda b, j, i: (b * N_MAIN_PAIRS + j, i, 0))
    rowv = pl.BlockSpec((2, nk, 1, bq), lambda b, j, i: (b * N_MAIN_PAIRS + j, 0, 0, 0))
    return col, rowv


def _causal(i, kb, bq, strict):
    row = i * bq + lax.broadcasted_iota(jnp.int32, (bq, bq), 0)
    col = kb * bq + lax.broadcasted_iota(jnp.int32, (bq, bq), 1)
    return (col < row) if strict else (col <= row)


def _fox_fwd(qkv, offs, B, S, ccol, crow, name='fox_fwd'):
    bq = min(FOX_BLOCK, S)
    nq = S // bq

    def body(q_ref, k_ref, v_ref, cc_ref, cr_ref, o_ref, lse_ref):
        i = pl.program_id(2)
        qh = _per_head(q_ref[...])
        first = _lane0((bq, PAIR))

        def step(kb, carry):
            m, l, acc = carry
            sl = pl.ds(pl.multiple_of(kb * bq, bq), bq)
            ks, vs = k_ref[sl, :], v_ref[sl, :]
            mask = _causal(i, kb, bq, False)
            m_new, l_new, alpha, pv = [], [], [], []
            for h in range(2):
                s = _dot(qh[h], ks, NT) * SCALE + cc_ref[h] - cr_ref[h, kb]
                s = jnp.where(mask, s, NEG_BIG)
                mh = jnp.maximum(m[h], jnp.max(s, axis=-1, keepdims=True))
                p = jnp.exp(s - mh)
                ah = jnp.exp(m[h] - mh)
                m_new.append(mh)
                alpha.append(ah)
                l_new.append(ah * l[h] + jnp.sum(p, axis=-1, keepdims=True))
                pv.append(_dot(p, vs, NN))
            acc = _pick(first, alpha[0], alpha[1]) * acc + _pick(first, pv[0], pv[1])
            return tuple(m_new), tuple(l_new), acc

        neg = jnp.full((bq, 1), NEG_BIG, F32)
        zero = jnp.zeros((bq, 1), F32)
        m, l, acc = lax.fori_loop(0, i + 1, step, ((neg, neg), (zero, zero), jnp.zeros((bq, PAIR), F32)))
        o_ref[...] = (acc / _pick(first, l[0], l[1])).astype(ACT_DTYPE)
        for h in range(2):
            lse_ref[h] = m[h] + jnp.log(l[h])

    col, rowv = _gate_specs(bq, nq)
    return pl.pallas_call(
        body, name=name, grid=(B, N_MAIN_PAIRS, nq),
        in_specs=[_q_spec(bq, nq, offs[0]), _seq_spec(S, offs[1]), _seq_spec(S, offs[2]), col, rowv],
        out_specs=[_q_spec(bq, nq, 0), col],
        out_shape=[jax.ShapeDtypeStruct((B * S, MAIN_WIDTH), ACT_DTYPE),
                   jax.ShapeDtypeStruct((B * N_MAIN_HEADS, S, 1), F32)],
        compiler_params=_params('parallel', 'parallel', 'arbitrary'),
    )(*qkv, ccol, crow)


def _fox_bwd(qkv, offs, B, S, ccol, crow, o, lse, do, name='fox_bwd'):
    bq = min(FOX_BLOCK, S)
    nq = S // bq

    def body(q_ref, k_ref, v_ref, cc_ref, cr_ref, o_ref, lse_ref, do_ref,
             dq_ref, dk_ref, dv_ref, dcc_ref, dcr_ref, dk_acc, dv_acc):
        i = pl.program_id(2)

        @pl.when(i == 0)
        def _():
            dk_acc[...] = jnp.zeros_like(dk_acc)
            dv_acc[...] = jnp.zeros_like(dv_acc)
            dcr_ref[...] = jnp.zeros_like(dcr_ref)

        qv = q_ref[...]
        dov = do_ref[...]
        qh = _per_head(qv)
        doh = _per_head(dov)
        first = _lane0((bq, PAIR))
        prod = dov.astype(F32) * o_ref[...].astype(F32)
        dsum = [jnp.sum(t, axis=-1, keepdims=True) for t in _per_head(prod)]

        def step(kb, carry):
            dq, dcc = carry
            sl = pl.ds(pl.multiple_of(kb * bq, bq), bq)
            ks, vs = k_ref[sl, :], v_ref[sl, :]
            mask = _causal(i, kb, bq, False)
            dqh, dkh, dvh, dcc_new = [], [], [], []
            for h in range(2):
                s = _dot(qh[h], ks, NT) * SCALE + cc_ref[h] - cr_ref[h, kb]
                p = jnp.where(mask, jnp.exp(s - lse_ref[h]), 0.0)
                ds = p * (_dot(doh[h], vs, NT) - dsum[h])
                dqh.append(_dot(ds, ks, NN))
                dkh.append(_dot(ds, qv, TN))
                dvh.append(_dot(p, dov, TN))
                dcr_ref[h, kb] -= jnp.sum(ds, axis=0, keepdims=True)
                dcc_new.append(dcc[h] + jnp.sum(ds, axis=-1, keepdims=True))
            dk_acc[sl, :] += SCALE * _pick(first, dkh[0], dkh[1])
            dv_acc[sl, :] += _pick(first, dvh[0], dvh[1])
            return dq + _pick(first, dqh[0], dqh[1]), tuple(dcc_new)

        zero = jnp.zeros((bq, 1), F32)
        dq, dcc = lax.fori_loop(0, i + 1, step, (jnp.zeros((bq, PAIR), F32), (zero, zero)))
        dq_ref[...] = (dq * SCALE).astype(ACT_DTYPE)
        for h in range(2):
            dcc_ref[h] = dcc[h]

        @pl.when(i == nq - 1)
        def _():
            dk_ref[...] = dk_acc[...].astype(ACT_DTYPE)
            dv_ref[...] = dv_acc[...].astype(ACT_DTYPE)

    col, rowv = _gate_specs(bq, nq)
    qs, seq = _q_spec(bq, nq, 0), _seq_spec(S, 0)
    full = jax.ShapeDtypeStruct((B * S, MAIN_WIDTH), ACT_DTYPE)
    return pl.pallas_call(
        body, name=name, grid=(B, N_MAIN_PAIRS, nq),
        in_specs=[_q_spec(bq, nq, offs[0]), _seq_spec(S, offs[1]), _seq_spec(S, offs[2]), col, rowv, qs, col, qs],
        out_specs=[qs, seq, seq, col, rowv],
        out_shape=[full, full, full, jax.ShapeDtypeStruct(ccol.shape, F32), jax.ShapeDtypeStruct(crow.shape, F32)],
        scratch_shapes=[pltpu.VMEM((S, PAIR), F32), pltpu.VMEM((S, PAIR), F32)],
        compiler_params=_params('parallel', 'parallel', 'arbitrary'),
    )(*qkv, ccol, crow, o, lse, do)


SB_BLOCK = 256


def _sb_block(qv, ks, i, kb, bq):
    z = _dot(qv, ks, NT) * SCALE
    row = i * bq + lax.broadcasted_iota(jnp.int32, (bq, bq), 0)
    col = kb * bq + lax.broadcasted_iota(jnp.int32, (bq, bq), 1)
    mask = col < row
    a = _log_sigmoid(z)
    l = jnp.where(mask, a - z, 0.0)
    return mask, a, l


def _sb_fwd(qkv, offs, B, S, name='sb_fwd'):
    bq = min(SB_BLOCK, S)
    nq = S // bq

    def body(q_ref, k_ref, v_ref, o_ref):
        i = pl.program_id(2)
        qh = _per_head(q_ref[...])
        first = _lane0((bq, PAIR))
        tri = _tri(bq, lambda j, s: j > s)

        def step(n, carry):
            acc, right = carry
            kb = i - n
            sl = pl.ds(pl.multiple_of(kb * bq, bq), bq)
            ks, vs = k_ref[sl, :], v_ref[sl, :]
            pv, right_new = [], []
            for h in range(2):
                mask, a, l = _sb_block(qh[h], ks, i, kb, bq)
                rsum = _split_dot(l, tri, 2) + right[h]
                w = jnp.where(mask, jnp.exp(a + rsum), 0.0)
                pv.append(_dot(w, vs, NN))
                right_new.append(right[h] + jnp.sum(l, axis=-1, keepdims=True))
            return acc + _pick(first, pv[0], pv[1]), tuple(right_new)

        zero = jnp.zeros((bq, 1), F32)
        acc, _ = lax.fori_loop(0, i + 1, step, (jnp.zeros((bq, PAIR), F32), (zero, zero)))
        o_ref[...] = acc.astype(ACT_DTYPE)

    return pl.pallas_call(
        body, name=name, grid=(B, N_MAIN_PAIRS, nq),
        in_specs=[_q_spec(bq, nq, offs[0]), _seq_spec(S, offs[1]), _seq_spec(S, offs[2])],
        out_specs=_q_spec(bq, nq, 0),
        out_shape=jax.ShapeDtypeStruct((B * S, MAIN_WIDTH), ACT_DTYPE),
        compiler_params=_params('parallel', 'parallel', 'arbitrary'),
    )(*qkv)


def _sb_bwd(qkv, offs, B, S, do, name='sb_bwd'):
    bq = min(SB_BLOCK, S)
    nq = S // bq

    def body(q_ref, k_ref, v_ref, do_ref, dq_ref, dk_ref, dv_ref, dk_acc, dv_acc):
        i = pl.program_id(2)

        @pl.when(i == 0)
        def _():
            dk_acc[...] = jnp.zeros_like(dk_acc)
            dv_acc[...] = jnp.zeros_like(dv_acc)

        qv = q_ref[...]
        dov = do_ref[...]
        qh = _per_head(qv)
        doh = _per_head(dov)
        first = _lane0((bq, PAIR))
        tri_incl = _tri(bq, lambda j, s: j <= s)
        tri_excl = _tri(bq, lambda j, s: j < s)

        def total(kb, tot):
            sl = pl.ds(pl.multiple_of(kb * bq, bq), bq)
            ks = k_ref[sl, :]
            return tuple(tot[h] + jnp.sum(_sb_block(qh[h], ks, i, kb, bq)[2], axis=-1, keepdims=True)
                         for h in range(2))

        zero = jnp.zeros((bq, 1), F32)
        tot = lax.fori_loop(0, i + 1, total, (zero, zero))

        def step(kb, carry):
            dq, left_l, left_g = carry
            sl = pl.ds(pl.multiple_of(kb * bq, bq), bq)
            ks, vs = k_ref[sl, :], v_ref[sl, :]
            dqh, dkh, dvh, new_l, new_g = [], [], [], [], []
            for h in range(2):
                mask, a, l = _sb_block(qh[h], ks, i, kb, bq)
                cum = _split_dot(l, tri_incl, 2) + left_l[h]
                w = jnp.where(mask, jnp.exp(a + tot[h] - cum), 0.0)
                g = w * _dot(doh[h], vs, NT)
                hsum = _split_dot(g, tri_excl, 1) + left_g[h]
                beta = jnp.exp(a)
                dz = jnp.where(mask, g * (1.0 - beta) - hsum * beta, 0.0)
                dqh.append(_dot(dz, ks, NN))
                dkh.append(_dot(dz, qv, TN))
                dvh.append(_dot(w, dov, TN))
                new_l.append(left_l[h] + jnp.sum(l, axis=-1, keepdims=True))
                new_g.append(left_g[h] + jnp.sum(g, axis=-1, keepdims=True))
            dk_acc[sl, :] += SCALE * _pick(first, dkh[0], dkh[1])
            dv_acc[sl, :] += _pick(first, dvh[0], dvh[1])
            return dq + _pick(first, dqh[0], dqh[1]), tuple(new_l), tuple(new_g)

        dq, _, _ = lax.fori_loop(0, i + 1, step, (jnp.zeros((bq, PAIR), F32), (zero, zero), (zero, zero)))
        dq_ref[...] = (dq * SCALE).astype(ACT_DTYPE)

        @pl.when(i == nq - 1)
        def _():
            dk_ref[...] = dk_acc[...].astype(ACT_DTYPE)
            dv_ref[...] = dv_acc[...].astype(ACT_DTYPE)

    qs, seq = _q_spec(bq, nq, 0), _seq_spec(S, 0)
    full = jax.ShapeDtypeStruct((B * S, MAIN_WIDTH), ACT_DTYPE)
    return pl.pallas_call(
        body, name=name, grid=(B, N_MAIN_PAIRS, nq),
        in_specs=[_q_spec(bq, nq, offs[0]), _seq_spec(S, offs[1]), _seq_spec(S, offs[2]), qs],
        out_specs=[qs, seq, seq], out_shape=[full, full, full],
        scratch_shapes=[pltpu.VMEM((S, PAIR), F32), pltpu.VMEM((S, PAIR), F32)],
        compiler_params=_params('parallel', 'parallel', 'arbitrary'),
    )(*qkv, do)


def _mem_probs(qv, mk):
    s = _dot(qv, mk, NT) * SCALE
    p = jnp.exp(s - jnp.max(s, axis=-1, keepdims=True))
    return p / jnp.sum(p, axis=-1, keepdims=True)


def _mem_fwd(q, q_off, mkv, B, S, name='mem_fwd'):
    M = mkv.shape[0] // B
    bq = _tile(S, 512)
    nq = S // bq

    def body(q_ref, mk_ref, mv_ref, o_ref):
        first = _lane0((bq, PAIR))
        mk, mv = mk_ref[...], mv_ref[...]
        out = [_dot(_mem_probs(qh, mk), mv, NN) for qh in _per_head(q_ref[...])]
        o_ref[...] = _pick(first, out[0], out[1]).astype(ACT_DTYPE)

    return pl.pallas_call(
        body, name=name, grid=(B, N_MEM_PAIRS, nq),
        in_specs=[_q_spec(bq, nq, q_off), _seq_spec(M, 0), _seq_spec(M, N_MEM_PAIRS)],
        out_specs=_q_spec(bq, nq, 0),
        out_shape=jax.ShapeDtypeStruct((B * S, MEM_WIDTH), ACT_DTYPE),
        compiler_params=_params('parallel', 'parallel', 'parallel'),
    )(q, mkv, mkv)


def _mem_bwd(q, q_off, mkv, B, S, do, do_off, name='mem_bwd'):
    M = mkv.shape[0] // B
    bq = _tile(S, 512)
    nq = S // bq

    def body(q_ref, mk_ref, mv_ref, do_ref, dq_ref, dmk_ref, dmv_ref):
        i = pl.program_id(2)

        @pl.when(i == 0)
        def _():
            dmk_ref[...] = jnp.zeros_like(dmk_ref)
            dmv_ref[...] = jnp.zeros_like(dmv_ref)

        qv = q_ref[...]
        dov = do_ref[...]
        mk, mv = mk_ref[...], mv_ref[...]
        first = _lane0((bq, PAIR))
        first_m = _lane0((M, PAIR))
        dqh, dkh, dvh = [], [], []
        for qh, doh in zip(_per_head(qv), _per_head(dov)):
            p = _mem_probs(qh, mk)
            dp = _dot(doh, mv, NT)
            ds = p * (dp - jnp.sum(p * dp, axis=-1, keepdims=True))
            dqh.append(_dot(ds, mk, NN))
            dkh.append(_dot(ds, qv, TN))
            dvh.append(_dot(p, dov, TN))
        dq_ref[...] = (SCALE * _pick(first, dqh[0], dqh[1])).astype(ACT_DTYPE)
        dmk_ref[...] += SCALE * _pick(first_m, dkh[0], dkh[1])
        dmv_ref[...] += _pick(first_m, dvh[0], dvh[1])

    mem_out = jax.ShapeDtypeStruct((B * M, MEM_WIDTH), F32)
    return pl.pallas_call(
        body, name=name, grid=(B, N_MEM_PAIRS, nq),
        in_specs=[_q_spec(bq, nq, q_off), _seq_spec(M, 0), _seq_spec(M, N_MEM_PAIRS), _q_spec(bq, nq, do_off)],
        out_specs=[_q_spec(bq, nq, 0), _seq_spec(M, 0), _seq_spec(M, 0)],
        out_shape=[jax.ShapeDtypeStruct((B * S, MEM_WIDTH), ACT_DTYPE), mem_out, mem_out],
        compiler_params=_params('parallel', 'parallel', 'arbitrary'),
    )(q, mkv, mkv, do)


def _shift_down(u, n):
    t = lax.broadcasted_iota(jnp.int32, u.shape, 0)
    return jnp.where(t >= n, pltpu.roll(u, n, 0), 0.0)


def _shift_up(u, n):
    S = u.shape[0]
    t = lax.broadcasted_iota(jnp.int32, u.shape, 0)
    return jnp.where(t < S - n, pltpu.roll(u, S - n, 0), 0.0)


def _conv(u, u1, u2, w, b):
    return b + w[0:1, :] * u2 + w[1:2, :] * u1 + w[2:3, :] * u


def _conv_specs(S, nf):
    ug = pl.BlockSpec((None, S, LANES), lambda b, j: (b, 0, j))
    uv = pl.BlockSpec((None, S, LANES), lambda b, j: (b, 0, j + nf))
    wg = pl.BlockSpec((3, LANES), lambda b, j: (0, j))
    wv = pl.BlockSpec((3, LANES), lambda b, j: (0, j + nf))
    bg = pl.BlockSpec((1, LANES), lambda b, j: (0, j))
    bv = pl.BlockSpec((1, LANES), lambda b, j: (0, j + nf))
    return ug, uv, wg, wv, bg, bv


def _conv_fwd(u, cw, cb, name='conv_fwd'):
    B, S, F2 = u.shape
    F = F2 // 2
    nf = F // LANES

    def body(ug_ref, uv_ref, wg_ref, wv_ref, bg_ref, bv_ref, y_ref):
        ug = ug_ref[...].astype(F32)
        uv = uv_ref[...].astype(F32)
        gate = _conv(ug, _shift_down(ug, 1), _shift_down(ug, 2), wg_ref[...], bg_ref[...])
        val = _conv(uv, _shift_down(uv, 1), _shift_down(uv, 2), wv_ref[...], bv_ref[...])
        y_ref[...] = (gate / (1.0 + jnp.exp(-gate)) * val).astype(ACT_DTYPE)

    specs = _conv_specs(S, nf)
    return pl.pallas_call(
        body, name=name, grid=(B, nf), in_specs=list(specs), out_specs=specs[0],
        out_shape=jax.ShapeDtypeStruct((B, S, F), ACT_DTYPE),
        compiler_params=_params('parallel', 'parallel'),
    )(u, u, cw, cw, cb, cb)


def _conv_bwd(u, cw, cb, dy, name='conv_bwd'):
    B, S, F2 = u.shape
    F = F2 // 2
    nf = F // LANES

    def body(ug_ref, uv_ref, wg_ref, wv_ref, bg_ref, bv_ref, dy_ref,
             dug_ref, duv_ref, dwg_ref, dwv_ref, dbg_ref, dbv_ref):
        b = pl.program_id(1)

        @pl.when(b == 0)
        def _():
            for r in (dwg_ref, dwv_ref, dbg_ref, dbv_ref):
                r[...] = jnp.zeros_like(r)

        ug = ug_ref[...].astype(F32)
        uv = uv_ref[...].astype(F32)
        ug1, ug2 = _shift_down(ug, 1), _shift_down(ug, 2)
        uv1, uv2 = _shift_down(uv, 1), _shift_down(uv, 2)
        wg, wv = wg_ref[...], wv_ref[...]
        gate = _conv(ug, ug1, ug2, wg, bg_ref[...])
        val = _conv(uv, uv1, uv2, wv, bv_ref[...])
        dyv = dy_ref[...].astype(F32)
        sg = 1.0 / (1.0 + jnp.exp(-gate))
        dval = dyv * (gate * sg)
        dgate = dyv * val * (sg * (1.0 + gate * (1.0 - sg)))

        def back(d, x, x1, x2, w, du_ref, dw_ref, db_ref):
            db_ref[...] += jnp.sum(d, axis=0, keepdims=True)
            dw_ref[...] += jnp.concatenate(
                [jnp.sum(d * x2, axis=0, keepdims=True), jnp.sum(d * x1, axis=0, keepdims=True),
                 jnp.sum(d * x, axis=0, keepdims=True)], axis=0)
            du = w[2:3, :] * d + w[1:2, :] * _shift_up(d, 1) + w[0:1, :] * _shift_up(d, 2)
            du_ref[...] = du.astype(ACT_DTYPE)

        back(dgate, ug, ug1, ug2, wg, dug_ref, dwg_ref, dbg_ref)
        back(dval, uv, uv1, uv2, wv, duv_ref, dwv_ref, dbv_ref)

    def swap(spec_fn):
        return lambda j, b: spec_fn(b, j)

    ug, uv, wg, wv, bg, bv = _conv_specs(S, nf)
    ins = [pl.BlockSpec(s.block_shape, swap(s.index_map)) for s in (ug, uv, wg, wv, bg, bv, ug)]
    outs = [ins[0], ins[0], ins[2], ins[2], ins[4], ins[4]]
    return pl.pallas_call(
        body, name=name, grid=(nf, B), in_specs=ins, out_specs=outs,
        out_shape=[jax.ShapeDtypeStruct((B, S, F), ACT_DTYPE), jax.ShapeDtypeStruct((B, S, F), ACT_DTYPE),
                   jax.ShapeDtypeStruct((3, F), F32), jax.ShapeDtypeStruct((3, F), F32),
                   jax.ShapeDtypeStruct((1, F), F32), jax.ShapeDtypeStruct((1, F), F32)],
        compiler_params=_params('parallel', 'arbitrary'),
    )(u, u, cw, cw, cb, cb, dy)


def _adamw(w, g, m, v, name):
    R, C = w.shape
    tr = R
    if R * C * 4 > (1 << 19) and R % 8 == 0:
        tr = 8
        for t in range(8, R + 1, 8):
            if R % t == 0 and t * C * 4 <= (1 << 19):
                tr = t

    def body(w_ref, g_ref, m_ref, v_ref, d_ref, nm_ref, nv_ref):
        gv = g_ref[...]
        nm = ADAM_B1 * m_ref[...] + (1.0 - ADAM_B1) * gv
        nv = ADAM_B2 * v_ref[...] + (1.0 - ADAM_B2) * (gv * gv)
        m_hat = nm / (1.0 - ADAM_B1 ** ADAM_STEP)
        v_hat = nv / (1.0 - ADAM_B2 ** ADAM_STEP)
        d_ref[...] = -ADAM_LR * (m_hat / (jnp.sqrt(v_hat) + ADAM_EPS) + ADAM_WD * w_ref[...])
        nm_ref[...] = nm
        nv_ref[...] = nv

    blk = pl.BlockSpec((tr, C), lambda i: (i, 0))
    shp = jax.ShapeDtypeStruct((R, C), F32)
    return pl.pallas_call(
        body, name=name, grid=(R // tr,), in_specs=[blk] * 4, out_specs=[blk] * 3, out_shape=[shp] * 3,
        compiler_params=_params('parallel'),
    )(w, g, m, v)


def _my_place():
    return lax.axis_index('x'), lax.axis_index('y'), lax.axis_index('c')


def _other_chips(x, y):
    return [(1 - x, y), (x, 1 - y), (1 - x, 1 - y)]


def _remote(src, dst, send_sem, recv_sem, to):
    return pltpu.make_async_remote_copy(src_ref=src, dst_ref=dst, send_sem=send_sem, recv_sem=recv_sem,
                                        device_id=to, device_id_type=MESH)


def _hbm_call(body, n_in, out_shapes, scratch, name):
    return pl.pallas_call(body, name=name, in_specs=[ANY] * n_in, out_specs=[ANY] * len(out_shapes),
                          out_shape=out_shapes, scratch_shapes=scratch)


def _gather_weights(shards, kinds, split, name='gather_weights'):
    n = len(shards)
    out_shapes = []
    for s, kind in zip(shards, kinds):
        L, r, c = s.shape
        shape = {'rows': (L, 4 * r, c), 'cols': (L, r, 4 * c), 'stack': (4 * L, r, c)}[kind]
        out_shapes.append(jax.ShapeDtypeStruct(shape, s.dtype))

    def body(*refs):
        ins, outs = refs[:n], refs[n:2 * n]
        send_sems, recv_sems, local_sems = refs[2 * n:]
        x, y, c = _my_place()
        chip = 2 * x + y
        sibling = (x, y, 1 - c)
        others = _other_chips(x, y)

        def window(a, k, half):
            L, r, cols = shards[a].shape
            first, count = (0, r) if half is None else (half * (r // 2), r // 2)
            if kinds[a] == 'rows':
                return outs[a].at[:, pl.ds(k * r + first, count), :]
            if kinds[a] == 'cols':
                return outs[a].at[:, pl.ds(first, count), pl.ds(pl.multiple_of(k * cols, LANES), cols)]
            return outs[a].at[pl.ds(k * L, L), pl.ds(first, count), :]

        def mine(a, half):
            r = shards[a].shape[1]
            return ins[a] if half is None else ins[a].at[:, pl.ds(half * (r // 2), r // 2), :]

        local = []
        for a in range(n):
            cp = pltpu.make_async_copy(ins[a], window(a, chip, None), local_sems.at[a])
            cp.start()
            local.append(cp)
        sent = []
        for a in range(n):
            half = c if split[a] else None
            for j, (ox, oy) in enumerate(others):
                cp = _remote(mine(a, half), window(a, chip, half), send_sems.at[6 * a + j], recv_sems.at[6 * a + j],
                             (ox, oy, c))
                cp.start()
                sent.append(cp)
        for a in range(n):
            half = c if split[a] else None
            for j, (ox, oy) in enumerate(others):
                k = 2 * ox + oy
                got = window(a, k, half)
                _remote(got, got, send_sems.at[6 * a + j], recv_sems.at[6 * a + j], (ox, oy, c)).wait_recv()
                if split[a]:
                    fw = _remote(got, got, send_sems.at[6 * a + 3 + j], recv_sems.at[6 * a + 3 + j], sibling)
                    fw.start()
                    sent.append(fw)
        for a in range(n):
            if split[a]:
                for j, (ox, oy) in enumerate(others):
                    theirs = window(a, 2 * ox + oy, 1 - c)
                    _remote(theirs, theirs, send_sems.at[6 * a + 3 + j], recv_sems.at[6 * a + 3 + j],
                            sibling).wait_recv()
        for cp in sent:
            cp.wait_send()
        for cp in local:
            cp.wait()

    scratch = [pltpu.SemaphoreType.DMA((6 * n,)), pltpu.SemaphoreType.DMA((6 * n,)), pltpu.SemaphoreType.DMA((n,))]
    return _hbm_call(body, n, out_shapes, scratch, name)(*shards)


def _swap_cores(gs, name='swap_cores'):
    n = len(gs)
    out_shapes = [jax.ShapeDtypeStruct((g.shape[0], g.shape[1] // 2, g.shape[2]), g.dtype) for g in gs]

    def body(*refs):
        ins, outs = refs[:n], refs[n:2 * n]
        send_sems, recv_sems = refs[2 * n:]
        x, y, c = _my_place()
        cps = []
        for a in range(n):
            rh = gs[a].shape[1] // 2
            cp = _remote(ins[a].at[:, pl.ds((1 - c) * rh, rh), :], outs[a], send_sems.at[a], recv_sems.at[a],
                         (x, y, 1 - c))
            cp.start()
            cps.append(cp)
        for cp in cps:
            cp.wait()

    scratch = [pltpu.SemaphoreType.DMA((n,)), pltpu.SemaphoreType.DMA((n,))]
    return _hbm_call(body, n, out_shapes, scratch, name)(*gs)


SUM_BLOCK_BYTES = 2 * 1024 * 1024


def _sum_rows(rh, cols):
    return _tile(rh, max(16, SUM_BLOCK_BYTES // (4 * cols)), 16)


def _add_cores(g, other, c_arr, wire_dtype, name):
    n, r, cols = g.shape
    rh = r // 2
    tr = _sum_rows(rh, cols)
    nrt = rh // tr

    def body(c_ref, g_ref, o_ref, q_ref):
        q_ref[...] = (g_ref[...] + o_ref[...]).astype(wire_dtype)

    gs = pltpu.PrefetchScalarGridSpec(
        num_scalar_prefetch=1, grid=(n, nrt),
        in_specs=[pl.BlockSpec((None, tr, cols), lambda j, i, c_ref: (j, c_ref[0] * nrt + i, 0)),
                  pl.BlockSpec((None, tr, cols), lambda j, i, c_ref: (j, i, 0))],
        out_specs=pl.BlockSpec((None, tr, cols), lambda j, i, c_ref: (j, i, 0)))
    return pl.pallas_call(
        body, name=name, grid_spec=gs, out_shape=jax.ShapeDtypeStruct((n, rh, cols), wire_dtype),
        compiler_params=_params('parallel', 'parallel'),
    )(c_arr, g, other)


def _send_chips(qs, name='send_chips'):
    n = len(qs)
    out_shapes = [jax.ShapeDtypeStruct((3,) + q.shape[1:], q.dtype) for q in qs]

    def body(*refs):
        ins, outs = refs[:n], refs[n:2 * n]
        send_sems, recv_sems = refs[2 * n:]
        x, y, c = _my_place()
        cps = []
        for a in range(n):
            for j, (ox, oy) in enumerate(_other_chips(x, y)):
                cp = _remote(ins[a].at[2 * ox + oy], outs[a].at[j], send_sems.at[3 * a + j], recv_sems.at[3 * a + j],
                             (ox, oy, c))
                cp.start()
                cps.append(cp)
        for cp in cps:
            cp.wait_recv()
        for cp in cps:
            cp.wait_send()

    scratch = [pltpu.SemaphoreType.DMA((3 * n,)), pltpu.SemaphoreType.DMA((3 * n,))]
    return _hbm_call(body, n, out_shapes, scratch, name)(*qs)


def _sum_chips(q, got, chip_arr, name):
    n, rh, cols = q.shape
    tr = _sum_rows(rh, cols)

    def body(k_ref, q_ref, gx_ref, gy_ref, gxy_ref, o_ref):
        f = lambda r: r[...].astype(F32)
        o_ref[...] = (f(q_ref) + f(gxy_ref)) + (f(gx_ref) + f(gy_ref))

    def got_spec(j):
        return pl.BlockSpec((None, tr, cols), lambda i, k_ref: (j, i, 0))

    gs = pltpu.PrefetchScalarGridSpec(
        num_scalar_prefetch=1, grid=(rh // tr,),
        in_specs=[pl.BlockSpec((None, tr, cols), lambda i, k_ref: (k_ref[0], i, 0)),
                  got_spec(0), got_spec(1), got_spec(2)],
        out_specs=pl.BlockSpec((tr, cols), lambda i, k_ref: (i, 0)))
    return pl.pallas_call(
        body, name=name, grid_spec=gs, out_shape=jax.ShapeDtypeStruct((rh, cols), F32),
        compiler_params=_params('parallel'),
    )(chip_arr, q, got, got, got)


def _join_cores(rs, places, out_shapes, name='join_cores'):
    n = len(rs)

    def body(*refs):
        ins, outs = refs[:n], refs[n:n + len(out_shapes)]
        send_sems, recv_sems, local_sems = refs[n + len(out_shapes):]
        x, y, c = _my_place()
        cps = []
        for a, (o, layer) in enumerate(places):
            dst = outs[o].at[layer, c]
            mine = pltpu.make_async_copy(ins[a], dst, local_sems.at[a])
            mine.start()
            cp = _remote(ins[a], dst, send_sems.at[a], recv_sems.at[a], (x, y, 1 - c))
            cp.start()
            cps.append((mine, cp))
        for mine, cp in cps:
            cp.wait()
            mine.wait()

    scratch = [pltpu.SemaphoreType.DMA((n,)), pltpu.SemaphoreType.DMA((n,)), pltpu.SemaphoreType.DMA((n,))]
    return _hbm_call(body, n, out_shapes, scratch, name)(*rs)


def _gate_rows(t, B, S):
    return t.reshape(B, S, N_MAIN_HEADS).transpose(0, 2, 1).reshape(B * N_MAIN_HEADS, S)


def _gate_cols(t, B, S):
    return t.reshape(B, N_MAIN_HEADS, S).transpose(0, 2, 1).reshape(B * S, N_MAIN_HEADS)


def _mem_kv_fwd(mem2, g, w, tag):
    hm = _rms_fwd(mem2, g, name=f'rms_mem_{tag}')
    mkv = _matmul(hm, w, 'nn', ACT_DTYPE, name=f'mm_memkv_{tag}')
    return hm, mkv


def _mem_kv_bwd(mem2, g, w, hm, dmk, dmv, tag):
    dmkv = jnp.concatenate([dmk, dmv], axis=1)
    dw = _matmul(hm, dmkv, 'tn', F32, name=f'mm_memkv_dw_{tag}')
    dhm = _matmul(dmkv, w, 'nt', F32, name=f'mm_memkv_dx_{tag}')
    _, dg = _rms_bwd(mem2, g, dhm, None, name=f'rms_mem_bwd_{tag}')
    return dw, dg


def _ffn_fwd(x, g, w_up, cw, cb, w_down, B, S, tag):
    T = x.shape[0]
    h2 = _rms_fwd(x, g, name=f'rms_ffn_{tag}')
    u = _matmul(h2, w_up, 'nn', ACT_DTYPE, name=f'mm_up_{tag}')
    y = _conv_fwd(u.reshape(B, S, -1), cw, cb, name=f'conv_fwd_{tag}').reshape(T, -1)
    x2 = _matmul(y, w_down, 'nn', F32, res=x, name=f'mm_down_{tag}')
    return x2, (h2, u, y)


def _ffn_bwd(dx2, x, g, w_up, cw, cb, w_down, saved, B, S, tag):
    h2, u, y = saved
    T = x.shape[0]
    dy = _matmul(dx2, w_down, 'nt', ACT_DTYPE, name=f'mm_down_dx_{tag}')
    dw_down = _matmul(y, dx2, 'tn', F32, name=f'mm_down_dw_{tag}')
    dug, duv, dcwg, dcwv, dcbg, dcbv = _conv_bwd(u.reshape(B, S, -1), cw, cb, dy.reshape(B, S, -1),
                                                  name=f'conv_bwd_{tag}')
    du = jnp.concatenate([dug.reshape(T, -1), duv.reshape(T, -1)], axis=1)
    dh2 = _matmul(du, w_up, 'nt', F32, name=f'mm_up_dx_{tag}')
    dw_up = _matmul(h2, du, 'tn', F32, slots=N_CHIPS, name=f'mm_up_dw_{tag}')
    dx, dg = _rms_bwd(x, g, dh2, dx2, name=f'rms_ffn_bwd_{tag}')
    dcw = jnp.concatenate([dcwg, dcwv], axis=1)
    dcb = jnp.concatenate([dcbg, dcbv], axis=1)
    return dx, dg, dw_up, dcw, dcb, dw_down


def _step(x, mem, tgt, W):
    B, S, D = x.shape
    T = B * S
    x0 = x.reshape(T, D)
    mem2 = mem.reshape(-1, D)
    tgt2 = tgt.reshape(T, D)
    row = lambda v: v.reshape(1, -1)
    q3 = 3 * MAIN_WIDTH

    w_in_a = W['w_in_a'][0]
    wa_main = jnp.concatenate([w_in_a[:, :q3], w_in_a[:, q3 + N_MAIN_HEADS:]], axis=1)
    wa_gate = jnp.pad(w_in_a[:, q3:q3 + N_MAIN_HEADS], ((0, 0), (0, LANES - N_MAIN_HEADS)))
    w_in_b = W['w_in_b'][0]
    bcol = jnp.tile(W['b_f_a'][0], B).reshape(B * N_MAIN_HEADS, 1)
    nkb = S // min(FOX_BLOCK, S)

    h1a = _rms_fwd(x0, row(W['ln_mix_g'][0]), name='rms_mix_a')
    pa = _matmul(h1a, wa_main, 'nn', ACT_DTYPE, name='mm_in_a')
    flog = _matmul(h1a, wa_gate, 'nn', F32, name='mm_gate_a')
    qkv_a = (pa, pa, pa)
    offs_a = (0, N_MAIN_PAIRS, 2 * N_MAIN_PAIRS)
    qm_off_a = 3 * N_MAIN_PAIRS
    zt = _gate_rows(flog[:, :N_MAIN_HEADS], B, S)
    cum = _gate_fwd(zt, bcol)
    ccol = cum.reshape(B * N_MAIN_HEADS, S, 1)
    crow = cum.reshape(B * N_MAIN_HEADS, nkb, 1, S // nkb)
    oa, lse = _fox_fwd(qkv_a, offs_a, B, S, ccol, crow)
    hma, mkva = _mem_kv_fwd(mem2, row(W['ln_mem_g'][0]), W['w_memkv'][0], 'a')
    oma = _mem_fwd(pa, qm_off_a, mkva, B, S, name='mem_fwd_a')
    ocat_a = jnp.concatenate([oa, oma], axis=1)
    x1 = _matmul(ocat_a, W['w_out'][0], 'nn', F32, res=x0, name='mm_out_a')
    x2, ffn_a = _ffn_fwd(x1, row(W['ln_ffn_g'][0]), W['w_up'][0], W['conv_w'][0], row(W['conv_b'][0]),
                         W['w_down'][0], B, S, 'a')
    hkv = _rms_fwd(x2, row(W['ln_kv_g']), name='rms_kv')
    kvs = _matmul(hkv, W['w_kv'], 'nn', ACT_DTYPE, name='mm_kv')
    h1b = _rms_fwd(x2, row(W['ln_mix_g'][1]), name='rms_mix_b')
    pb = _matmul(h1b, w_in_b, 'nn', ACT_DTYPE, name='mm_in_b')
    qkv_b = (pb, kvs, kvs)
    offs_b = (0, 0, N_MAIN_PAIRS)
    qm_off_b = N_MAIN_PAIRS
    ob = _sb_fwd(qkv_b, offs_b, B, S)
    hmb, mkvb = _mem_kv_fwd(mem2, row(W['ln_mem_g'][1]), W['w_memkv'][1], 'b')
    omb = _mem_fwd(pb, qm_off_b, mkvb, B, S, name='mem_fwd_b')
    ocat_b = jnp.concatenate([ob, omb], axis=1)
    x3 = _matmul(ocat_b, W['w_out'][1], 'nn', F32, res=x2, name='mm_out_b')
    x4, ffn_b = _ffn_fwd(x3, row(W['ln_ffn_g'][1]), W['w_up'][1], W['conv_w'][1], row(W['conv_b'][1]),
                         W['w_down'][1], B, S, 'b')
    loss, dx4, d_final_g = _final_loss(x4, row(W['final_g']), tgt2)

    dx3, dg_ffn_b, dw_up_b, dcw_b, dcb_b, dw_down_b = _ffn_bwd(
        dx4, x3, row(W['ln_ffn_g'][1]), W['w_up'][1], W['conv_w'][1], row(W['conv_b'][1]), W['w_down'][1],
        ffn_b, B, S, 'b')
    docat = _matmul(dx3, W['w_out'][1], 'nt', ACT_DTYPE, name='mm_out_dx_b')
    dw_out_b = _matmul(ocat_b, dx3, 'tn', F32, name='mm_out_dw_b')
    dqb, dkb, dvb = _sb_bwd(qkv_b, offs_b, B, S, docat)
    dqmb, dmkb, dmvb = _mem_bwd(pb, qm_off_b, mkvb, B, S, docat, N_MAIN_PAIRS, name='mem_bwd_b')
    dw_memkv_b, dg_mem_b = _mem_kv_bwd(mem2, row(W['ln_mem_g'][1]), W['w_memkv'][1], hmb, dmkb, dmvb, 'b')
    dpb = jnp.concatenate([dqb, dqmb], axis=1)
    dh1b = _matmul(dpb, w_in_b, 'nt', F32, name='mm_in_dx_b')
    dw_in_b = _matmul(h1b, dpb, 'tn', F32, name='mm_in_dw_b')
    dx2, dg_mix_b = _rms_bwd(x2, row(W['ln_mix_g'][1]), dh1b, dx3, name='rms_mix_bwd_b')
    dkvs = jnp.concatenate([dkb, dvb], axis=1)
    dhkv = _matmul(dkvs, W['w_kv'], 'nt', F32, name='mm_kv_dx')
    dw_kv = _matmul(hkv, dkvs, 'tn', F32, slots=N_CHIPS, name='mm_kv_dw')
    dx2, dg_kv = _rms_bwd(x2, row(W['ln_kv_g']), dhkv, dx2, name='rms_kv_bwd')

    dx1, dg_ffn_a, dw_up_a, dcw_a, dcb_a, dw_down_a = _ffn_bwd(
        dx2, x1, row(W['ln_ffn_g'][0]), W['w_up'][0], W['conv_w'][0], row(W['conv_b'][0]), W['w_down'][0],
        ffn_a, B, S, 'a')
    docat = _matmul(dx1, W['w_out'][0], 'nt', ACT_DTYPE, name='mm_out_dx_a')
    dw_out_a = _matmul(ocat_a, dx1, 'tn', F32, name='mm_out_dw_a')
    dqa, dka, dva, dccol, dcrow = _fox_bwd(qkv_a, offs_a, B, S, ccol, crow, oa, lse, docat)
    dzt, dbrow = _gate_bwd(zt, bcol, dccol.reshape(B * N_MAIN_HEADS, S) + dcrow.reshape(B * N_MAIN_HEADS, S))
    dqma, dmka, dmva = _mem_bwd(pa, qm_off_a, mkva, B, S, docat, N_MAIN_PAIRS, name='mem_bwd_a')
    dw_memkv_a, dg_mem_a = _mem_kv_bwd(mem2, row(W['ln_mem_g'][0]), W['w_memkv'][0], hma, dmka, dmva, 'a')
    dpa = jnp.concatenate([dqa, dka, dva, dqma], axis=1)
    dflog = jnp.pad(_gate_cols(dzt, B, S), ((0, 0), (0, LANES - N_MAIN_HEADS)))
    dh1a = _matmul(dpa, wa_main, 'nt', F32, name='mm_in_dx_a')
    dh1a = _matmul(dflog, wa_gate, 'nt', F32, res=dh1a, name='mm_gate_dx_a')
    dwa_main = _matmul(h1a, dpa, 'tn', F32, name='mm_in_dw_a')
    dwa_gate = _matmul(h1a, dflog, 'tn', F32, name='mm_gate_dw_a')
    dx0, dg_mix_a = _rms_bwd(x0, row(W['ln_mix_g'][0]), dh1a, dx1, name='rms_mix_bwd_a')

    dw_in_a = jnp.concatenate([dwa_main[:, :q3], dwa_gate[:, :N_MAIN_HEADS], dwa_main[:, q3:]], axis=1)
    def by_rows(dw):
        return dw.reshape(N_CHIPS, dw.shape[0] // N_CHIPS, dw.shape[1])

    grads = {
        'ln_mix_g': jnp.concatenate([dg_mix_a, dg_mix_b], axis=0),
        'w_in_a': dw_in_a[None],
        'b_f_a': dbrow.reshape(B, N_MAIN_HEADS).sum(axis=0)[None],
        'w_in_b': [by_rows(dw_in_b)],
        'ln_kv_g': dg_kv[0],
        'w_kv': [dw_kv],
        'ln_mem_g': jnp.concatenate([dg_mem_a, dg_mem_b], axis=0),
        'w_memkv': [by_rows(dw_memkv_a), by_rows(dw_memkv_b)],
        'w_out': [by_rows(dw_out_a), by_rows(dw_out_b)],
        'ln_ffn_g': jnp.concatenate([dg_ffn_a, dg_ffn_b], axis=0),
        'w_up': [dw_up_a, dw_up_b],
        'conv_w': jnp.stack([dcw_a, dcw_b]),
        'conv_b': jnp.concatenate([dcb_a, dcb_b], axis=0),
        'w_down': [by_rows(dw_down_a), by_rows(dw_down_b)],
        'final_g': d_final_g[0],
    }
    return loss, dx0.reshape(B, S, D), grads


BLOCKED = ('w_in_b', 'w_kv', 'w_memkv', 'w_out', 'w_up', 'w_down')
MISC_ROWS = 32


def _misc_names():
    return [n for n in PARAM_NAMES if PARAM_SHARD_AXIS[n] is None] + ['conv_w']


def _reduce_grads(grads, shards):
    x, y, c = _my_place()
    c_arr = jnp.reshape(c, (1,)).astype(jnp.int32)
    chip_arr = jnp.reshape(2 * x + y, (1,)).astype(jnp.int32)

    a_cols = shards['w_in_a'].shape[2]
    a_pad = -(-a_cols // LANES) * LANES
    dw_in_a = grads['w_in_a'][0]
    in_a = jnp.stack([jnp.pad(dw_in_a[:, k * a_cols:(k + 1) * a_cols], ((0, 0), (0, a_pad - a_cols)))
                      for k in range(N_CHIPS)])
    conv_cols = shards['conv_w'].shape[2]
    misc = []
    for k in range(N_CHIPS):
        parts = [grads[n].reshape(-1) for n in _misc_names()[:-1]]
        parts.append(grads['conv_w'][:, :, k * conv_cols:(k + 1) * conv_cols].reshape(-1))
        flat = jnp.concatenate(parts)
        assert flat.shape[0] <= MISC_ROWS * PACK_COLS
        misc.append(jnp.pad(flat, (0, MISC_ROWS * PACK_COLS - flat.shape[0])).reshape(MISC_ROWS, PACK_COLS))
    arrays = [in_a, jnp.stack(misc)]
    places = [(0, 0), (1, 0)]
    layers = [1, 1]
    for n in BLOCKED:
        for layer, g in enumerate(grads[n]):
            arrays.append(g)
            places.append((len(layers), layer))
        layers.append(len(grads[n]))

    wire = [jnp.bfloat16, F32] + [jnp.bfloat16] * (len(arrays) - 2)
    others = _swap_cores(arrays)
    qs = [_add_cores(g, o, c_arr, wire[i], name=f'add_cores_{i}') for i, (g, o) in enumerate(zip(arrays, others))]
    gots = _send_chips(qs)
    sums = [_sum_chips(q, g, chip_arr, name=f'sum_chips_{i}') for i, (q, g) in enumerate(zip(qs, gots))]
    out_shapes, seen = [], set()
    for (o, _), s in zip(places, sums):
        if o not in seen:
            seen.add(o)
            out_shapes.append(jax.ShapeDtypeStruct((layers[o], 2) + s.shape, F32))
    joined = _join_cores(sums, places, out_shapes)

    out = {}
    in_a_sum, misc_sum = joined[0], joined[1]
    out['w_in_a'] = in_a_sum.reshape(1, -1, a_pad)[:, :, :a_cols]
    flat = misc_sum.reshape(-1)
    off = 0
    for n in _misc_names():
        size = math.prod(shards[n].shape)
        out[n] = flat[off:off + size].reshape(shards[n].shape)
        off += size
    for n, j in zip(BLOCKED, joined[2:]):
        out[n] = j.reshape(shards[n].shape)
    return out


def kernel(x, mem, ln_mix_g, w_in_a, b_f_a, w_in_b, ln_kv_g, w_kv, ln_mem_g, w_memkv, w_out, ln_ffn_g, w_up, conv_w, conv_b, w_down, final_g, loss_target, m_ln_mix_g, m_w_in_a, m_b_f_a, m_w_in_b, m_ln_kv_g, m_w_kv, m_ln_mem_g, m_w_memkv, m_w_out, m_ln_ffn_g, m_w_up, m_conv_w, m_conv_b, m_w_down, m_final_g, v_ln_mix_g, v_w_in_a, v_b_f_a, v_w_in_b, v_ln_kv_g, v_w_kv, v_ln_mem_g, v_w_memkv, v_w_out, v_ln_ffn_g, v_w_up, v_conv_w, v_conv_b, v_w_down, v_final_g):
    shards = dict(ln_mix_g=ln_mix_g, w_in_a=w_in_a, b_f_a=b_f_a, w_in_b=w_in_b, ln_kv_g=ln_kv_g, w_kv=w_kv,
                  ln_mem_g=ln_mem_g, w_memkv=w_memkv, w_out=w_out, ln_ffn_g=ln_ffn_g, w_up=w_up, conv_w=conv_w,
                  conv_b=conv_b, w_down=w_down, final_g=final_g)
    moments_m = dict(ln_mix_g=m_ln_mix_g, w_in_a=m_w_in_a, b_f_a=m_b_f_a, w_in_b=m_w_in_b, ln_kv_g=m_ln_kv_g,
                     w_kv=m_w_kv, ln_mem_g=m_ln_mem_g, w_memkv=m_w_memkv, w_out=m_w_out, ln_ffn_g=m_ln_ffn_g,
                     w_up=m_w_up, conv_w=m_conv_w, conv_b=m_conv_b, w_down=m_w_down, final_g=m_final_g)
    moments_v = dict(ln_mix_g=v_ln_mix_g, w_in_a=v_w_in_a, b_f_a=v_b_f_a, w_in_b=v_w_in_b, ln_kv_g=v_ln_kv_g,
                     w_kv=v_w_kv, ln_mem_g=v_ln_mem_g, w_memkv=v_w_memkv, w_out=v_w_out, ln_ffn_g=v_ln_ffn_g,
                     w_up=v_w_up, conv_w=v_conv_w, conv_b=v_conv_b, w_down=v_w_down, final_g=v_final_g)

    kinds = {'w_in_a': 'stack', 'w_in_b': 'rows', 'w_kv': 'cols', 'w_memkv': 'rows', 'w_out': 'rows', 'w_up': 'cols',
             'w_down': 'rows', 'conv_w': 'cols'}
    blocks = []
    for n in kinds:
        w = shards[n].reshape((-1,) + shards[n].shape[-2:])
        blocks.append(w if n in F32_GATHERED else w.astype(jnp.bfloat16))
    full = _gather_weights(blocks, list(kinds.values()), [n not in F32_GATHERED for n in kinds])
    W = dict(shards)
    for n, f in zip(kinds, full):
        W[n] = f
    W['w_in_a'] = jnp.concatenate([W['w_in_a'][k] for k in range(N_CHIPS)], axis=1)[None]
    W['w_kv'] = W['w_kv'][0]

    loss_part, grad_x, grads = _step(x, mem, loss_target, W)
    loss = lax.psum(loss_part[0, 0], ('x', 'y', 'c'))

    g_shard = _reduce_grads(grads, shards)

    deltas, new_m, new_v = {}, {}, {}
    for name in PARAM_NAMES:
        w = shards[name]
        two_d = (-1, w.shape[-1])
        d, nm, nv = _adamw(w.reshape(two_d), g_shard[name].reshape(two_d), moments_m[name].reshape(two_d),
                           moments_v[name].reshape(two_d), name=f'adamw_{name}')
        deltas[name], new_m[name], new_v[name] = d.reshape(w.shape), nm.reshape(w.shape), nv.reshape(w.shape)

    return (loss, grad_x, *[g_shard[n] for n in PARAM_NAMES], *[deltas[n] for n in PARAM_NAMES],
            *[new_m[n] for n in PARAM_NAMES], *[new_v[n] for n in PARAM_NAMES])
```

```python
import functools
import math

import jax
import jax.numpy as jnp
from jax import lax
from jax.experimental import pallas as pl
from jax.experimental.pallas import tpu as pltpu

F32 = jnp.float32
MXU_DTYPE = jnp.bfloat16
ACT_DTYPE = jnp.bfloat16

HEAD_DIM = 64
N_MAIN_HEADS = 12
N_MEM_HEADS = 4
MAIN_WIDTH = N_MAIN_HEADS * HEAD_DIM
MEM_WIDTH = N_MEM_HEADS * HEAD_DIM
EPS = 1e-6
SCALE = HEAD_DIM ** -0.5
NEG_BIG = -1e30
LANES = 128
PACK_COLS = 1024
N_CHIPS = 4

ADAM_LR = 0.001
ADAM_B1 = 0.9
ADAM_B2 = 0.999
ADAM_EPS = 1e-08
ADAM_WD = 0.01
ADAM_STEP = 10

MESH = pl.DeviceIdType.MESH
ANY = pl.BlockSpec(memory_space=pl.ANY)

PARAM_SHARD_AXIS = {
    'ln_mix_g': None, 'w_in_a': 2, 'b_f_a': None, 'w_in_b': 1, 'ln_kv_g': None, 'w_kv': 1,
    'ln_mem_g': None, 'w_memkv': 1, 'w_out': 1, 'ln_ffn_g': None, 'w_up': 2, 'conv_w': 2,
    'conv_b': None, 'w_down': 1, 'final_g': None,
}
PARAM_NAMES = list(PARAM_SHARD_AXIS)
F32_GATHERED = ('conv_w',)


def _tile(n, pref, unit=LANES):
    if n <= pref:
        return n
    best = None
    for t in range(unit, pref + 1, unit):
        if n % t == 0:
            best = t
    assert best is not None, (n, pref)
    return best


MM_ACC_ELEMS = 768 * 1024
MM_VMEM_MB = 48


def _out_tiles(M, N):
    def divisors(n, cap):
        if n <= LANES:
            return [n]
        return [t for t in range(LANES, min(n, cap) + 1, LANES) if n % t == 0]

    best = None
    for tm in divisors(M, 1024):
        for tn in divisors(N, 2048):
            if tm * tn <= MM_ACC_ELEMS and (best is None or (tm * tn, tn) > (best[0] * best[1], best[1])):
                best = (tm, tn)
    assert best is not None, (M, N)
    return best


def _params(*sem, vmem_mb=None):
    kw = {}
    if sem:
        kw['dimension_semantics'] = sem
    if vmem_mb is not None:
        kw['vmem_limit_bytes'] = vmem_mb * 1024 * 1024
    return pltpu.CompilerParams(**kw)


def _dot(a, b, dims):
    return lax.dot_general(a.astype(MXU_DTYPE), b.astype(MXU_DTYPE), (dims, ((), ())),
                           preferred_element_type=F32)


NN = ((1,), (0,))
NT = ((1,), (1,))
TN = ((0,), (0,))


def _matmul(a, b, mode, out_dtype, res=None, slots=1, name='mm'):
    if mode == 'nn':
        (M, K), (K2, N) = a.shape, b.shape
    elif mode == 'nt':
        (M, K), (N, K2) = a.shape, b.shape
    else:
        (K, M), (K2, N) = a.shape, b.shape
    assert K == K2 and N % slots == 0, (a.shape, b.shape, mode, slots)
    slot_cols = N // slots
    tm, tn = _out_tiles(M, slot_cols)
    per_slot = slot_cols // tn
    tk = _tile(K, 1024)
    nk = K // tk
    dims = {'nn': NN, 'nt': NT, 'tn': TN}[mode]

    def body(*refs):
        if res is None:
            a_ref, b_ref, o_ref, acc = refs
        else:
            a_ref, b_ref, r_ref, o_ref, acc = refs
        k = pl.program_id(2)

        @pl.when(k == 0)
        def _():
            acc[...] = jnp.zeros_like(acc)

        acc[...] += _dot(a_ref[...], b_ref[...], dims)

        @pl.when(k == nk - 1)
        def _():
            out = acc[...]
            if res is not None:
                out = out + r_ref[...]
            o_ref[...] = out.astype(out_dtype)

    if mode == 'tn':
        a_spec = pl.BlockSpec((tk, tm), lambda i, j, k: (k, i))
    else:
        a_spec = pl.BlockSpec((tm, tk), lambda i, j, k: (i, k))
    if mode == 'nt':
        b_spec = pl.BlockSpec((tn, tk), lambda i, j, k: (j, k))
    else:
        b_spec = pl.BlockSpec((tk, tn), lambda i, j, k: (k, j))
    if slots == 1:
        o_spec = pl.BlockSpec((tm, tn), lambda i, j, k: (i, j))
        out_shape = jax.ShapeDtypeStruct((M, N), out_dtype)
    else:
        assert res is None
        o_spec = pl.BlockSpec((None, tm, tn), lambda i, j, k: (j // per_slot, i, j % per_slot))
        out_shape = jax.ShapeDtypeStruct((slots, M, slot_cols), out_dtype)
    in_specs = [a_spec, b_spec] + ([o_spec] if res is not None else [])
    args = (a, b) + ((res,) if res is not None else ())
    return pl.pallas_call(
        body, name=name, grid=(M // tm, N // tn, nk),
        in_specs=in_specs, out_specs=o_spec,
        out_shape=out_shape,
        scratch_shapes=[pltpu.VMEM((tm, tn), F32)],
        compiler_params=_params('parallel', 'parallel', 'arbitrary', vmem_mb=MM_VMEM_MB),
    )(*args)


def _rms_fwd(x, g, name):
    T, D = x.shape
    tr = _tile(T, 512)

    def body(x_ref, g_ref, o_ref):
        xv = x_ref[...]
        r = lax.rsqrt(jnp.mean(xv * xv, axis=-1, keepdims=True) + EPS)
        o_ref[...] = (xv * r * g_ref[...]).astype(ACT_DTYPE)

    return pl.pallas_call(
        body, name=name, grid=(T // tr,),
        in_specs=[pl.BlockSpec((tr, D), lambda i: (i, 0)), pl.BlockSpec((1, D), lambda i: (0, 0))],
        out_specs=pl.BlockSpec((tr, D), lambda i: (i, 0)),
        out_shape=jax.ShapeDtypeStruct((T, D), ACT_DTYPE),
        compiler_params=_params('parallel'),
    )(x, g)


def _rms_bwd(x, g, dh, dres, name):
    T, D = x.shape
    tr = _tile(T, 512)
    want_dx = dres is not None

    def body(*refs):
        if want_dx:
            x_ref, g_ref, dh_ref, dres_ref, dx_ref, dg_ref = refs
        else:
            x_ref, g_ref, dh_ref, dg_ref = refs
        i = pl.program_id(0)

        @pl.when(i == 0)
        def _():
            dg_ref[...] = jnp.zeros_like(dg_ref)

        xv = x_ref[...]
        dhv = dh_ref[...].astype(F32)
        r = lax.rsqrt(jnp.mean(xv * xv, axis=-1, keepdims=True) + EPS)
        n = xv * r
        dg_ref[...] += jnp.sum(dhv * n, axis=0, keepdims=True)
        if want_dx:
            dn = dhv * g_ref[...]
            dx = r * (dn - n * jnp.mean(dn * n, axis=-1, keepdims=True))
            dx_ref[...] = dres_ref[...] + dx

    row = pl.BlockSpec((tr, D), lambda i: (i, 0))
    vec = pl.BlockSpec((1, D), lambda i: (0, 0))
    if want_dx:
        return pl.pallas_call(
            body, name=name, grid=(T // tr,),
            in_specs=[row, vec, row, row], out_specs=[row, vec],
            out_shape=[jax.ShapeDtypeStruct((T, D), F32), jax.ShapeDtypeStruct((1, D), F32)],
            compiler_params=_params('arbitrary'),
        )(x, g, dh, dres)
    dg = pl.pallas_call(
        body, name=name, grid=(T // tr,),
        in_specs=[row, vec, row], out_specs=vec,
        out_shape=jax.ShapeDtypeStruct((1, D), F32),
        compiler_params=_params('arbitrary'),
    )(x, g, dh)
    return None, dg


def _final_loss(x, g, tgt, name='final_loss'):
    T, D = x.shape
    tr = _tile(T, 512)

    def body(x_ref, g_ref, t_ref, loss_ref, dx_ref, dg_ref):
        i = pl.program_id(0)

        @pl.when(i == 0)
        def _():
            loss_ref[...] = jnp.zeros_like(loss_ref)
            dg_ref[...] = jnp.zeros_like(dg_ref)

        xv = x_ref[...]
        gv = g_ref[...]
        r = lax.rsqrt(jnp.mean(xv * xv, axis=-1, keepdims=True) + EPS)
        n = xv * r
        e = n * gv - t_ref[...]
        per_tok = jnp.mean(e * e, axis=-1, keepdims=True)
        loss_ref[...] += 0.5 * jnp.sum(per_tok, axis=0, keepdims=True)
        dy = e * (1.0 / D)
        dg_ref[...] += jnp.sum(dy * n, axis=0, keepdims=True)
        dn = dy * gv
        dx_ref[...] = r * (dn - n * jnp.mean(dn * n, axis=-1, keepdims=True))

    row = pl.BlockSpec((tr, D), lambda i: (i, 0))
    vec = pl.BlockSpec((1, D), lambda i: (0, 0))
    one = pl.BlockSpec((1, 1), lambda i: (0, 0))
    return pl.pallas_call(
        body, name=name, grid=(T // tr,),
        in_specs=[row, vec, row], out_specs=[one, row, vec],
        out_shape=[jax.ShapeDtypeStruct((1, 1), F32), jax.ShapeDtypeStruct((T, D), F32),
                   jax.ShapeDtypeStruct((1, D), F32)],
        compiler_params=_params('arbitrary'),
    )(x, g, tgt)


def _log_sigmoid(z):
    return jnp.minimum(z, 0.0) - jnp.log(1.0 + jnp.exp(-jnp.abs(z)))


def _tri(n, rel):
    j = lax.broadcasted_iota(jnp.int32, (n, n), 0)
    s = lax.broadcasted_iota(jnp.int32, (n, n), 1)
    return rel(j, s).astype(MXU_DTYPE)


def _split_dot(x, tri, terms):
    if MXU_DTYPE == F32:
        return jnp.dot(x, tri, preferred_element_type=F32)
    out = None
    rem = x
    for _ in range(terms):
        piece = rem.astype(MXU_DTYPE)
        part = jnp.dot(piece, tri, preferred_element_type=F32)
        out = part if out is None else out + part
        rem = rem - piece.astype(F32)
    return out


def _gate_fwd(zt, bcol, name='gate_fwd'):
    BH, S = zt.shape
    nb = S // LANES

    def body(z_ref, b_ref, c_ref):
        tri = _tri(LANES, lambda j, s: j <= s)
        carry = jnp.zeros((BH, 1), F32)
        for i in range(nb):
            sl = slice(i * LANES, (i + 1) * LANES)
            logf = _log_sigmoid(z_ref[:, sl] + b_ref[...])
            cs = _split_dot(logf, tri, 3) + carry
            c_ref[:, sl] = cs
            carry = cs[:, LANES - 1:LANES]

    return pl.pallas_call(body, name=name, out_shape=jax.ShapeDtypeStruct((BH, S), F32))(zt, bcol)


def _gate_bwd(zt, bcol, dc, name='gate_bwd'):
    BH, S = zt.shape
    nb = S // LANES

    def body(z_ref, b_ref, dc_ref, dz_ref, db_ref):
        tri = _tri(LANES, lambda j, s: j >= s)
        carry = jnp.zeros((BH, 1), F32)
        dsum = jnp.zeros((BH, 1), F32)
        for i in reversed(range(nb)):
            sl = slice(i * LANES, (i + 1) * LANES)
            rs = _split_dot(dc_ref[:, sl], tri, 3) + carry
            carry = rs[:, 0:1]
            z = z_ref[:, sl] + b_ref[...]
            dz = rs * (1.0 - 1.0 / (1.0 + jnp.exp(-z)))
            dz_ref[:, sl] = dz
            dsum = dsum + jnp.sum(dz, axis=-1, keepdims=True)
        db_ref[...] = dsum

    return pl.pallas_call(
        body, name=name,
        out_shape=[jax.ShapeDtypeStruct((BH, S), F32), jax.ShapeDtypeStruct((BH, 1), F32)],
    )(zt, bcol, dc)


FOX_BLOCK = 256


PAIR = 2 * HEAD_DIM
N_MAIN_PAIRS = N_MAIN_HEADS // 2
N_MEM_PAIRS = N_MEM_HEADS // 2


def _lane0(shape):
    return lax.broadcasted_iota(jnp.int32, shape, len(shape) - 1) < HEAD_DIM


def _per_head(x):
    first = _lane0(x.shape)
    zero = jnp.zeros_like(x)
    return jnp.where(first, x, zero), jnp.where(first, zero, x)


def _pick(first, a, b):
    return jnp.where(first, a, b)


def _q_spec(bq, nq, off):
    return pl.BlockSpec((bq, PAIR), lambda b, j, i: (b * nq + i, off + j))


def _seq_spec(S, off):
    return pl.BlockSpec((S, PAIR), lambda b, j, i: (b, off + j))


def _gate_specs(bq, nk):
    col = pl.BlockSpec((2, bq, 1), lambda b, j, i: (b * N_MAIN_PAIRS + j, i, 0))
    rowv = pl.BlockSpec((2, nk, 1, bq), lambda b, j, i: (b * N_MAIN_PAIRS + j, 0, 0, 0))
    return col, rowv


def _causal(i, kb, bq, strict):
    row = i * bq + lax.broadcasted_iota(jnp.int32, (bq, bq), 0)
    col = kb * bq + lax.broadcasted_iota(jnp.int32, (bq, bq), 1)
    return (col < row) if strict else (col <= row)


def _fox_fwd(qkv, offs, B, S, ccol, crow, name='fox_fwd'):
    bq = min(FOX_BLOCK, S)
    nq = S // bq

    def body(q_ref, k_ref, v_ref, cc_ref, cr_ref, o_ref, lse_ref):
        i = pl.program_id(2)
        qh = _per_head(q_ref[...])
        first = _lane0((bq, PAIR))

        def step(kb, carry):
            m, l, acc = carry
            sl = pl.ds(pl.multiple_of(kb * bq, bq), bq)
            ks, vs = k_ref[sl, :], v_ref[sl, :]
            mask = _causal(i, kb, bq, False)
            m_new, l_new, alpha, pv = [], [], [], []
            for h in range(2):
                s = _dot(qh[h], ks, NT) * SCALE + cc_ref[h] - cr_ref[h, kb]
                s = jnp.where(mask, s, NEG_BIG)
                mh = jnp.maximum(m[h], jnp.max(s, axis=-1, keepdims=True))
                p = jnp.exp(s - mh)
                ah = jnp.exp(m[h] - mh)
                m_new.append(mh)
                alpha.append(ah)
                l_new.append(ah * l[h] + jnp.sum(p, axis=-1, keepdims=True))
                pv.append(_dot(p, vs, NN))
            acc = _pick(first, alpha[0], alpha[1]) * acc + _pick(first, pv[0], pv[1])
            return tuple(m_new), tuple(l_new), acc

        neg = jnp.full((bq, 1), NEG_BIG, F32)
        zero = jnp.zeros((bq, 1), F32)
        m, l, acc = lax.fori_loop(0, i + 1, step, ((neg, neg), (zero, zero), jnp.zeros((bq, PAIR), F32)))
        o_ref[...] = (acc / _pick(first, l[0], l[1])).astype(ACT_DTYPE)
        for h in range(2):
            lse_ref[h] = m[h] + jnp.log(l[h])

    col, rowv = _gate_specs(bq, nq)
    return pl.pallas_call(
        body, name=name, grid=(B, N_MAIN_PAIRS, nq),
        in_specs=[_q_spec(bq, nq, offs[0]), _seq_spec(S, offs[1]), _seq_spec(S, offs[2]), col, rowv],
        out_specs=[_q_spec(bq, nq, 0), col],
        out_shape=[jax.ShapeDtypeStruct((B * S, MAIN_WIDTH), ACT_DTYPE),
                   jax.ShapeDtypeStruct((B * N_MAIN_HEADS, S, 1), F32)],
        compiler_params=_params('parallel', 'parallel', 'arbitrary'),
    )(*qkv, ccol, crow)


def _fox_bwd(qkv, offs, B, S, ccol, crow, o, lse, do, name='fox_bwd'):
    bq = min(FOX_BLOCK, S)
    nq = S // bq

    def body(q_ref, k_ref, v_ref, cc_ref, cr_ref, o_ref, lse_ref, do_ref,
             dq_ref, dk_ref, dv_ref, dcc_ref, dcr_ref, dk_acc, dv_acc):
        i = pl.program_id(2)

        @pl.when(i == 0)
        def _():
            dk_acc[...] = jnp.zeros_like(dk_acc)
            dv_acc[...] = jnp.zeros_like(dv_acc)
            dcr_ref[...] = jnp.zeros_like(dcr_ref)

        qv = q_ref[...]
        dov = do_ref[...]
        qh = _per_head(qv)
        doh = _per_head(dov)
        first = _lane0((bq, PAIR))
        prod = dov.astype(F32) * o_ref[...].astype(F32)
        dsum = [jnp.sum(t, axis=-1, keepdims=True) for t in _per_head(prod)]

        def step(kb, carry):
            dq, dcc = carry
            sl = pl.ds(pl.multiple_of(kb * bq, bq), bq)
            ks, vs = k_ref[sl, :], v_ref[sl, :]
            mask = _causal(i, kb, bq, False)
            dqh, dkh, dvh, dcc_new = [], [], [], []
            for h in range(2):
                s = _dot(qh[h], ks, NT) * SCALE + cc_ref[h] - cr_ref[h, kb]
                p = jnp.where(mask, jnp.exp(s - lse_ref[h]), 0.0)
                ds = p * (_dot(doh[h], vs, NT) - dsum[h])
                dqh.append(_dot(ds, ks, NN))
                dkh.append(_dot(ds, qv, TN))
                dvh.append(_dot(p, dov, TN))
                dcr_ref[h, kb] -= jnp.sum(ds, axis=0, keepdims=True)
                dcc_new.append(dcc[h] + jnp.sum(ds, axis=-1, keepdims=True))
            dk_acc[sl, :] += SCALE * _pick(first, dkh[0], dkh[1])
            dv_acc[sl, :] += _pick(first, dvh[0], dvh[1])
            return dq + _pick(first, dqh[0], dqh[1]), tuple(dcc_new)

        zero = jnp.zeros((bq, 1), F32)
        dq, dcc = lax.fori_loop(0, i + 1, step, (jnp.zeros((bq, PAIR), F32), (zero, zero)))
        dq_ref[...] = (dq * SCALE).astype(ACT_DTYPE)
        for h in range(2):
            dcc_ref[h] = dcc[h]

        @pl.when(i == nq - 1)
        def _():
            dk_ref[...] = dk_acc[...].astype(ACT_DTYPE)
            dv_ref[...] = dv_acc[...].astype(ACT_DTYPE)

    col, rowv = _gate_specs(bq, nq)
    qs, seq = _q_spec(bq, nq, 0), _seq_spec(S, 0)
    full = jax.ShapeDtypeStruct((B * S, MAIN_WIDTH), ACT_DTYPE)
    return pl.pallas_call(
        body, name=name, grid=(B, N_MAIN_PAIRS, nq),
        in_specs=[_q_spec(bq, nq, offs[0]), _seq_spec(S, offs[1]), _seq_spec(S, offs[2]), col, rowv, qs, col, qs],
        out_specs=[qs, seq, seq, col, rowv],
        out_shape=[full, full, full, jax.ShapeDtypeStruct(ccol.shape, F32), jax.ShapeDtypeStruct(crow.shape, F32)],
        scratch_shapes=[pltpu.VMEM((S, PAIR), F32), pltpu.VMEM((S, PAIR), F32)],
        compiler_params=_params('parallel', 'parallel', 'arbitrary'),
    )(*qkv, ccol, crow, o, lse, do)


SB_BLOCK = 256


def _sb_block(qv, ks, i, kb, bq):
    z = _dot(qv, ks, NT) * SCALE
    row = i * bq + lax.broadcasted_iota(jnp.int32, (bq, bq), 0)
    col = kb * bq + lax.broadcasted_iota(jnp.int32, (bq, bq), 1)
    mask = col < row
    a = _log_sigmoid(z)
    l = jnp.where(mask, a - z, 0.0)
    return mask, a, l


def _sb_fwd(qkv, offs, B, S, name='sb_fwd'):
    bq = min(SB_BLOCK, S)
    nq = S // bq

    def body(q_ref, k_ref, v_ref, o_ref):
        i = pl.program_id(2)
        qh = _per_head(q_ref[...])
        first = _lane0((bq, PAIR))
        tri = _tri(bq, lambda j, s: j > s)

        def step(n, carry):
            acc, right = carry
            kb = i - n
            sl = pl.ds(pl.multiple_of(kb * bq, bq), bq)
            ks, vs = k_ref[sl, :], v_ref[sl, :]
            pv, right_new = [], []
            for h in range(2):
                mask, a, l = _sb_block(qh[h], ks, i, kb, bq)
                rsum = _split_dot(l, tri, 2) + right[h]
                w = jnp.where(mask, jnp.exp(a + rsum), 0.0)
                pv.append(_dot(w, vs, NN))
                right_new.append(right[h] + jnp.sum(l, axis=-1, keepdims=True))
            return acc + _pick(first, pv[0], pv[1]), tuple(right_new)

        zero = jnp.zeros((bq, 1), F32)
        acc, _ = lax.fori_loop(0, i + 1, step, (jnp.zeros((bq, PAIR), F32), (zero, zero)))
        o_ref[...] = acc.astype(ACT_DTYPE)

    return pl.pallas_call(
        body, name=name, grid=(B, N_MAIN_PAIRS, nq),
        in_specs=[_q_spec(bq, nq, offs[0]), _seq_spec(S, offs[1]), _seq_spec(S, offs[2])],
        out_specs=_q_spec(bq, nq, 0),
        out_shape=jax.ShapeDtypeStruct((B * S, MAIN_WIDTH), ACT_DTYPE),
        compiler_params=_params('parallel', 'parallel', 'arbitrary'),
    )(*qkv)


def _sb_bwd(qkv, offs, B, S, do, name='sb_bwd'):
    bq = min(SB_BLOCK, S)
    nq = S // bq

    def body(q_ref, k_ref, v_ref, do_ref, dq_ref, dk_ref, dv_ref, dk_acc, dv_acc):
        i = pl.program_id(2)

        @pl.when(i == 0)
        def _():
            dk_acc[...] = jnp.zeros_like(dk_acc)
            dv_acc[...] = jnp.zeros_like(dv_acc)

        qv = q_ref[...]
        dov = do_ref[...]
        qh = _per_head(qv)
        doh = _per_head(dov)
        first = _lane0((bq, PAIR))
        tri_incl = _tri(bq, lambda j, s: j <= s)
        tri_excl = _tri(bq, lambda j, s: j < s)

        def total(kb, tot):
            sl = pl.ds(pl.multiple_of(kb * bq, bq), bq)
            ks = k_ref[sl, :]
            return tuple(tot[h] + jnp.sum(_sb_block(qh[h], ks, i, kb, bq)[2], axis=-1, keepdims=True)
                         for h in range(2))

        zero = jnp.zeros((bq, 1), F32)
        tot = lax.fori_loop(0, i + 1, total, (zero, zero))

        def step(kb, carry):
            dq, left_l, left_g = carry
            sl = pl.ds(pl.multiple_of(kb * bq, bq), bq)
            ks, vs = k_ref[sl, :], v_ref[sl, :]
            dqh, dkh, dvh, new_l, new_g = [], [], [], [], []
            for h in range(2):
                mask, a, l = _sb_block(qh[h], ks, i, kb, bq)
                cum = _split_dot(l, tri_incl, 2) + left_l[h]
                w = jnp.where(mask, jnp.exp(a + tot[h] - cum), 0.0)
                g = w * _dot(doh[h], vs, NT)
                hsum = _split_dot(g, tri_excl, 1) + left_g[h]
                beta = jnp.exp(a)
                dz = jnp.where(mask, g * (1.0 - beta) - hsum * beta, 0.0)
                dqh.append(_dot(dz, ks, NN))
                dkh.append(_dot(dz, qv, TN))
                dvh.append(_dot(w, dov, TN))
                new_l.append(left_l[h] + jnp.sum(l, axis=-1, keepdims=True))
                new_g.append(left_g[h] + jnp.sum(g, axis=-1, keepdims=True))
            dk_acc[sl, :] += SCALE * _pick(first, dkh[0], dkh[1])
            dv_acc[sl, :] += _pick(first, dvh[0], dvh[1])
            return dq + _pick(first, dqh[0], dqh[1]), tuple(new_l), tuple(new_g)

        dq, _, _ = lax.fori_loop(0, i + 1, step, (jnp.zeros((bq, PAIR), F32), (zero, zero), (zero, zero)))
        dq_ref[...] = (dq * SCALE).astype(ACT_DTYPE)

        @pl.when(i == nq - 1)
        def _():
            dk_ref[...] = dk_acc[...].astype(ACT_DTYPE)
            dv_ref[...] = dv_acc[...].astype(ACT_DTYPE)

    qs, seq = _q_spec(bq, nq, 0), _seq_spec(S, 0)
    full = jax.ShapeDtypeStruct((B * S, MAIN_WIDTH), ACT_DTYPE)
    return pl.pallas_call(
        body, name=name, grid=(B, N_MAIN_PAIRS, nq),
        in_specs=[_q_spec(bq, nq, offs[0]), _seq_spec(S, offs[1]), _seq_spec(S, offs[2]), qs],
        out_specs=[qs, seq, seq], out_shape=[full, full, full],
        scratch_shapes=[pltpu.VMEM((S, PAIR), F32), pltpu.VMEM((S, PAIR), F32)],
        compiler_params=_params('parallel', 'parallel', 'arbitrary'),
    )(*qkv, do)


def _mem_probs(qv, mk):
    s = _dot(qv, mk, NT) * SCALE
    p = jnp.exp(s - jnp.max(s, axis=-1, keepdims=True))
    return p / jnp.sum(p, axis=-1, keepdims=True)


def _mem_fwd(q, q_off, mkv, B, S, name='mem_fwd'):
    M = mkv.shape[0] // B
    bq = _tile(S, 512)
    nq = S // bq

    def body(q_ref, mk_ref, mv_ref, o_ref):
        first = _lane0((bq, PAIR))
        mk, mv = mk_ref[...], mv_ref[...]
        out = [_dot(_mem_probs(qh, mk), mv, NN) for qh in _per_head(q_ref[...])]
        o_ref[...] = _pick(first, out[0], out[1]).astype(ACT_DTYPE)

    return pl.pallas_call(
        body, name=name, grid=(B, N_MEM_PAIRS, nq),
        in_specs=[_q_spec(bq, nq, q_off), _seq_spec(M, 0), _seq_spec(M, N_MEM_PAIRS)],
        out_specs=_q_spec(bq, nq, 0),
        out_shape=jax.ShapeDtypeStruct((B * S, MEM_WIDTH), ACT_DTYPE),
        compiler_params=_params('parallel', 'parallel', 'parallel'),
    )(q, mkv, mkv)


def _mem_bwd(q, q_off, mkv, B, S, do, do_off, name='mem_bwd'):
    M = mkv.shape[0] // B
    bq = _tile(S, 512)
    nq = S // bq

    def body(q_ref, mk_ref, mv_ref, do_ref, dq_ref, dmk_ref, dmv_ref):
        i = pl.program_id(2)

        @pl.when(i == 0)
        def _():
            dmk_ref[...] = jnp.zeros_like(dmk_ref)
            dmv_ref[...] = jnp.zeros_like(dmv_ref)

        qv = q_ref[...]
        dov = do_ref[...]
        mk, mv = mk_ref[...], mv_ref[...]
        first = _lane0((bq, PAIR))
        first_m = _lane0((M, PAIR))
        dqh, dkh, dvh = [], [], []
        for qh, doh in zip(_per_head(qv), _per_head(dov)):
            p = _mem_probs(qh, mk)
            dp = _dot(doh, mv, NT)
            ds = p * (dp - jnp.sum(p * dp, axis=-1, keepdims=True))
            dqh.append(_dot(ds, mk, NN))
            dkh.append(_dot(ds, qv, TN))
            dvh.append(_dot(p, dov, TN))
        dq_ref[...] = (SCALE * _pick(first, dqh[0], dqh[1])).astype(ACT_DTYPE)
        dmk_ref[...] += SCALE * _pick(first_m, dkh[0], dkh[1])
        dmv_ref[...] += _pick(first_m, dvh[0], dvh[1])

    mem_out = jax.ShapeDtypeStruct((B * M, MEM_WIDTH), F32)
    return pl.pallas_call(
        body, name=name, grid=(B, N_MEM_PAIRS, nq),
        in_specs=[_q_spec(bq, nq, q_off), _seq_spec(M, 0), _seq_spec(M, N_MEM_PAIRS), _q_spec(bq, nq, do_off)],
        out_specs=[_q_spec(bq, nq, 0), _seq_spec(M, 0), _seq_spec(M, 0)],
        out_shape=[jax.ShapeDtypeStruct((B * S, MEM_WIDTH), ACT_DTYPE), mem_out, mem_out],
        compiler_params=_params('parallel', 'parallel', 'arbitrary'),
    )(q, mkv, mkv, do)


def _shift_down(u, n):
    t = lax.broadcasted_iota(jnp.int32, u.shape, 0)
    return jnp.where(t >= n, pltpu.roll(u, n, 0), 0.0)


def _shift_up(u, n):
    S = u.shape[0]
    t = lax.broadcasted_iota(jnp.int32, u.shape, 0)
    return jnp.where(t < S - n, pltpu.roll(u, S - n, 0), 0.0)


def _conv(u, u1, u2, w, b):
    return b + w[0:1, :] * u2 + w[1:2, :] * u1 + w[2:3, :] * u


def _conv_specs(S, nf):
    ug = pl.BlockSpec((None, S, LANES), lambda b, j: (b, 0, j))
    uv = pl.BlockSpec((None, S, LANES), lambda b, j: (b, 0, j + nf))
    wg = pl.BlockSpec((3, LANES), lambda b, j: (0, j))
    wv = pl.BlockSpec((3, LANES), lambda b, j: (0, j + nf))
    bg = pl.BlockSpec((1, LANES), lambda b, j: (0, j))
    bv = pl.BlockSpec((1, LANES), lambda b, j: (0, j + nf))
    return ug, uv, wg, wv, bg, bv


def _conv_fwd(u, cw, cb, name='conv_fwd'):
    B, S, F2 = u.shape
    F = F2 // 2
    nf = F // LANES

    def body(ug_ref, uv_ref, wg_ref, wv_ref, bg_ref, bv_ref, y_ref):
        ug = ug_ref[...].astype(F32)
        uv = uv_ref[...].astype(F32)
        gate = _conv(ug, _shift_down(ug, 1), _shift_down(ug, 2), wg_ref[...], bg_ref[...])
        val = _conv(uv, _shift_down(uv, 1), _shift_down(uv, 2), wv_ref[...], bv_ref[...])
        y_ref[...] = (gate / (1.0 + jnp.exp(-gate)) * val).astype(ACT_DTYPE)

    specs = _conv_specs(S, nf)
    return pl.pallas_call(
        body, name=name, grid=(B, nf), in_specs=list(specs), out_specs=specs[0],
        out_shape=jax.ShapeDtypeStruct((B, S, F), ACT_DTYPE),
        compiler_params=_params('parallel', 'parallel'),
    )(u, u, cw, cw, cb, cb)


def _conv_bwd(u, cw, cb, dy, name='conv_bwd'):
    B, S, F2 = u.shape
    F = F2 // 2
    nf = F // LANES

    def body(ug_ref, uv_ref, wg_ref, wv_ref, bg_ref, bv_ref, dy_ref,
             dug_ref, duv_ref, dwg_ref, dwv_ref, dbg_ref, dbv_ref):
        b = pl.program_id(1)

        @pl.when(b == 0)
        def _():
            for r in (dwg_ref, dwv_ref, dbg_ref, dbv_ref):
                r[...] = jnp.zeros_like(r)

        ug = ug_ref[...].astype(F32)
        uv = uv_ref[...].astype(F32)
        ug1, ug2 = _shift_down(ug, 1), _shift_down(ug, 2)
        uv1, uv2 = _shift_down(uv, 1), _shift_down(uv, 2)
        wg, wv = wg_ref[...], wv_ref[...]
        gate = _conv(ug, ug1, ug2, wg, bg_ref[...])
        val = _conv(uv, uv1, uv2, wv, bv_ref[...])
        dyv = dy_ref[...].astype(F32)
        sg = 1.0 / (1.0 + jnp.exp(-gate))
        dval = dyv * (gate * sg)
        dgate = dyv * val * (sg * (1.0 + gate * (1.0 - sg)))

        def back(d, x, x1, x2, w, du_ref, dw_ref, db_ref):
            db_ref[...] += jnp.sum(d, axis=0, keepdims=True)
            dw_ref[...] += jnp.concatenate(
                [jnp.sum(d * x2, axis=0, keepdims=True), jnp.sum(d * x1, axis=0, keepdims=True),
                 jnp.sum(d * x, axis=0, keepdims=True)], axis=0)
            du = w[2:3, :] * d + w[1:2, :] * _shift_up(d, 1) + w[0:1, :] * _shift_up(d, 2)
            du_ref[...] = du.astype(ACT_DTYPE)

        back(dgate, ug, ug1, ug2, wg, dug_ref, dwg_ref, dbg_ref)
        back(dval, uv, uv1, uv2, wv, duv_ref, dwv_ref, dbv_ref)

    def swap(spec_fn):
        return lambda j, b: spec_fn(b, j)

    ug, uv, wg, wv, bg, bv = _conv_specs(S, nf)
    ins = [pl.BlockSpec(s.block_shape, swap(s.index_map)) for s in (ug, uv, wg, wv, bg, bv, ug)]
    outs = [ins[0], ins[0], ins[2], ins[2], ins[4], ins[4]]
    return pl.pallas_call(
        body, name=name, grid=(nf, B), in_specs=ins, out_specs=outs,
        out_shape=[jax.ShapeDtypeStruct((B, S, F), ACT_DTYPE), jax.ShapeDtypeStruct((B, S, F), ACT_DTYPE),
                   jax.ShapeDtypeStruct((3, F), F32), jax.ShapeDtypeStruct((3, F), F32),
                   jax.ShapeDtypeStruct((1, F), F32), jax.ShapeDtypeStruct((1, F), F32)],
        compiler_params=_params('parallel', 'arbitrary'),
    )(u, u, cw, cw, cb, cb, dy)


ADAM_BLOCK_BYTES = 512 * 1024


def _adamw(w, g, m, v, layer, earlier, name):
    L, r, c = w.shape
    tr = r
    if r * c * 4 > ADAM_BLOCK_BYTES and r % 8 == 0:
        tr = 8
        for t in range(8, r + 1, 8):
            if r % t == 0 and t * c * 4 <= ADAM_BLOCK_BYTES:
                tr = t

    def body(w_ref, g_ref, m_ref, v_ref, *rest):
        go_ref, d_ref, nm_ref, nv_ref = rest[-4:]
        gv = g_ref[...]
        nm = ADAM_B1 * m_ref[...] + (1.0 - ADAM_B1) * gv
        nv = ADAM_B2 * v_ref[...] + (1.0 - ADAM_B2) * (gv * gv)
        m_hat = nm / (1.0 - ADAM_B1 ** ADAM_STEP)
        v_hat = nv / (1.0 - ADAM_B2 ** ADAM_STEP)
        d_ref[...] = -ADAM_LR * (m_hat / (jnp.sqrt(v_hat) + ADAM_EPS) + ADAM_WD * w_ref[...])
        nm_ref[...] = nm
        nv_ref[...] = nv
        go_ref[...] = gv

    lay = pl.BlockSpec((None, tr, c), lambda i: (layer, i, 0))
    one = pl.BlockSpec((tr, c), lambda i: (i, 0))
    shp = jax.ShapeDtypeStruct((L, r, c), F32)
    in_specs = [lay, one, lay, lay]
    args = (w, g, m, v)
    aliases = {}
    if earlier is not None:
        in_specs += [ANY] * 4
        args += tuple(earlier)
        aliases = {4 + k: k for k in range(4)}
    return pl.pallas_call(
        body, name=name, grid=(r // tr,), in_specs=in_specs, out_specs=[lay] * 4, out_shape=[shp] * 4,
        input_output_aliases=aliases, compiler_params=_params('parallel'),
    )(*args)


def _my_place():
    return lax.axis_index('x'), lax.axis_index('y'), lax.axis_index('c')


def _other_chips(x, y):
    return [(1 - x, y), (x, 1 - y), (1 - x, 1 - y)]


def _remote(src, dst, send_sem, recv_sem, to):
    return pltpu.make_async_remote_copy(src_ref=src, dst_ref=dst, send_sem=send_sem, recv_sem=recv_sem,
                                        device_id=to, device_id_type=MESH)


def _hbm_call(body, n_in, out_shapes, scratch, name, aliases=None):
    return pl.pallas_call(body, name=name, in_specs=[ANY] * n_in, out_specs=[ANY] * len(out_shapes),
                          out_shape=out_shapes, scratch_shapes=scratch, input_output_aliases=aliases or {})


def _full_shape(shard_shape, kind):
    L, r, c = shard_shape
    return {'rows': (L, N_CHIPS * r, c), 'cols': (L, r, N_CHIPS * c), 'stack': (N_CHIPS * L, r, c)}[kind]


def _place_block(w, kind, out_dtype, chip_arr, name):
    L, r, c = w.shape
    tr = r if r % 16 else _tile(r, max(16, SUM_BLOCK_BYTES // (4 * c)), 16)
    nrt = r // tr

    def body(k_ref, w_ref, o_ref):
        o_ref[...] = w_ref[...].astype(out_dtype)

    out_map = {'rows': lambda l, i, k_ref: (l, k_ref[0] * nrt + i, 0),
               'cols': lambda l, i, k_ref: (l, i, k_ref[0]),
               'stack': lambda l, i, k_ref: (k_ref[0] * L + l, i, 0)}[kind]
    gs = pltpu.PrefetchScalarGridSpec(
        num_scalar_prefetch=1, grid=(L, nrt),
        in_specs=[pl.BlockSpec((None, tr, c), lambda l, i, k_ref: (l, i, 0))],
        out_specs=pl.BlockSpec((None, tr, c), out_map))
    return pl.pallas_call(
        body, name=name, grid_spec=gs, out_shape=jax.ShapeDtypeStruct(_full_shape(w.shape, kind), out_dtype),
        compiler_params=_params('parallel', 'parallel'),
    )(chip_arr, w)


def _gather_weights(fulls, shard_shapes, kinds, split, name='gather_weights'):
    n = len(fulls)
    out_shapes = [jax.ShapeDtypeStruct(f.shape, f.dtype) for f in fulls]

    def body(*refs):
        outs = refs[n:2 * n]
        send_sems, recv_sems = refs[2 * n:]
        x, y, c = _my_place()
        chip = 2 * x + y
        sibling = (x, y, 1 - c)
        others = _other_chips(x, y)

        def window(a, k, half):
            L, r, cols = shard_shapes[a]
            first, count = (0, r) if half is None else (half * (r // 2), r // 2)
            if kinds[a] == 'rows':
                return outs[a].at[:, pl.ds(k * r + first, count), :]
            if kinds[a] == 'cols':
                return outs[a].at[:, pl.ds(first, count), pl.ds(pl.multiple_of(k * cols, LANES), cols)]
            return outs[a].at[pl.ds(k * L, L), pl.ds(first, count), :]

        sent = []
        for a in range(n):
            half = c if split[a] else None
            for j, (ox, oy) in enumerate(others):
                cp = _remote(window(a, chip, half), window(a, chip, half), send_sems.at[6 * a + j],
                             recv_sems.at[6 * a + j], (ox, oy, c))
                cp.start()
                sent.append(cp)
        for a in range(n):
            half = c if split[a] else None
            for j, (ox, oy) in enumerate(others):
                k = 2 * ox + oy
                got = window(a, k, half)
                _remote(got, got, send_sems.at[6 * a + j], recv_sems.at[6 * a + j], (ox, oy, c)).wait_recv()
                if split[a]:
                    fw = _remote(got, got, send_sems.at[6 * a + 3 + j], recv_sems.at[6 * a + 3 + j], sibling)
                    fw.start()
                    sent.append(fw)
        for a in range(n):
            if split[a]:
                for j, (ox, oy) in enumerate(others):
                    theirs = window(a, 2 * ox + oy, 1 - c)
                    _remote(theirs, theirs, send_sems.at[6 * a + 3 + j], recv_sems.at[6 * a + 3 + j],
                            sibling).wait_recv()
        for cp in sent:
            cp.wait_send()

    scratch = [pltpu.SemaphoreType.DMA((6 * n,)), pltpu.SemaphoreType.DMA((6 * n,))]
    return _hbm_call(body, n, out_shapes, scratch, name, aliases={a: a for a in range(n)})(*fulls)


def _swap_cores(gs, name='swap_cores'):
    n = len(gs)
    out_shapes = [jax.ShapeDtypeStruct((g.shape[0], g.shape[1] // 2, g.shape[2]), g.dtype) for g in gs]

    def body(*refs):
        ins, outs = refs[:n], refs[n:2 * n]
        send_sems, recv_sems = refs[2 * n:]
        x, y, c = _my_place()
        cps = []
        for a in range(n):
            rh = gs[a].shape[1] // 2
            cp = _remote(ins[a].at[:, pl.ds((1 - c) * rh, rh), :], outs[a], send_sems.at[a], recv_sems.at[a],
                         (x, y, 1 - c))
            cp.start()
            cps.append(cp)
        for cp in cps:
            cp.wait()

    scratch = [pltpu.SemaphoreType.DMA((n,)), pltpu.SemaphoreType.DMA((n,))]
    return _hbm_call(body, n, out_shapes, scratch, name)(*gs)


SUM_BLOCK_BYTES = 2 * 1024 * 1024


def _sum_rows(rh, cols):
    return _tile(rh, max(16, SUM_BLOCK_BYTES // (4 * cols)), 16)


def _add_cores(g, other, c_arr, wire_dtype, name):
    n, r, cols = g.shape
    rh = r // 2
    tr = _sum_rows(rh, cols)
    nrt = rh // tr

    def body(c_ref, g_ref, o_ref, q_ref):
        q_ref[...] = (g_ref[...] + o_ref[...]).astype(wire_dtype)

    gs = pltpu.PrefetchScalarGridSpec(
        num_scalar_prefetch=1, grid=(n, nrt),
        in_specs=[pl.BlockSpec((None, tr, cols), lambda j, i, c_ref: (j, c_ref[0] * nrt + i, 0)),
                  pl.BlockSpec((None, tr, cols), lambda j, i, c_ref: (j, i, 0))],
        out_specs=pl.BlockSpec((None, tr, cols), lambda j, i, c_ref: (j, i, 0)))
    return pl.pallas_call(
        body, name=name, grid_spec=gs, out_shape=jax.ShapeDtypeStruct((n, rh, cols), wire_dtype),
        compiler_params=_params('parallel', 'parallel'),
    )(c_arr, g, other)


def _send_chips(qs, name='send_chips'):
    n = len(qs)
    out_shapes = [jax.ShapeDtypeStruct((3,) + q.shape[1:], q.dtype) for q in qs]

    def body(*refs):
        ins, outs = refs[:n], refs[n:2 * n]
        send_sems, recv_sems = refs[2 * n:]
        x, y, c = _my_place()
        cps = []
        for a in range(n):
            for j, (ox, oy) in enumerate(_other_chips(x, y)):
                cp = _remote(ins[a].at[2 * ox + oy], outs[a].at[j], send_sems.at[3 * a + j], recv_sems.at[3 * a + j],
                             (ox, oy, c))
                cp.start()
                cps.append(cp)
        for cp in cps:
            cp.wait_recv()
        for cp in cps:
            cp.wait_send()

    scratch = [pltpu.SemaphoreType.DMA((3 * n,)), pltpu.SemaphoreType.DMA((3 * n,))]
    return _hbm_call(body, n, out_shapes, scratch, name)(*qs)


def _sum_chips(q, got, place_arr, name):
    n, rh, cols = q.shape
    tr = _sum_rows(rh, cols)

    def body(p_ref, q_ref, gx_ref, gy_ref, gxy_ref, o_ref):
        f = lambda r: r[...].astype(F32)
        o_ref[...] = (f(q_ref) + f(gxy_ref)) + (f(gx_ref) + f(gy_ref))

    def got_spec(j):
        return pl.BlockSpec((None, tr, cols), lambda i, p_ref: (j, i, 0))

    gs = pltpu.PrefetchScalarGridSpec(
        num_scalar_prefetch=1, grid=(rh // tr,),
        in_specs=[pl.BlockSpec((None, tr, cols), lambda i, p_ref: (p_ref[0], i, 0)),
                  got_spec(0), got_spec(1), got_spec(2)],
        out_specs=pl.BlockSpec((None, tr, cols), lambda i, p_ref: (p_ref[1], i, 0)))
    return pl.pallas_call(
        body, name=name, grid_spec=gs, out_shape=jax.ShapeDtypeStruct((2, rh, cols), F32),
        compiler_params=_params('parallel'),
    )(place_arr, q, got, got, got)


def _join_cores(rs, name='join_cores'):
    n = len(rs)
    out_shapes = [jax.ShapeDtypeStruct(r.shape, r.dtype) for r in rs]

    def body(*refs):
        outs = refs[n:2 * n]
        send_sems, recv_sems = refs[2 * n:]
        x, y, c = _my_place()
        cps = []
        for a in range(n):
            cp = _remote(outs[a].at[c], outs[a].at[c], send_sems.at[a], recv_sems.at[a], (x, y, 1 - c))
            cp.start()
            cps.append(cp)
        for cp in cps:
            cp.wait()

    scratch = [pltpu.SemaphoreType.DMA((n,)), pltpu.SemaphoreType.DMA((n,))]
    return _hbm_call(body, n, out_shapes, scratch, name, aliases={a: a for a in range(n)})(*rs)


def _gate_rows(t, B, S):
    return t.reshape(B, S, N_MAIN_HEADS).transpose(0, 2, 1).reshape(B * N_MAIN_HEADS, S)


def _gate_cols(t, B, S):
    return t.reshape(B, N_MAIN_HEADS, S).transpose(0, 2, 1).reshape(B * S, N_MAIN_HEADS)


def _mem_kv_fwd(mem2, g, w, tag):
    hm = _rms_fwd(mem2, g, name=f'rms_mem_{tag}')
    mkv = _matmul(hm, w, 'nn', ACT_DTYPE, name=f'mm_memkv_{tag}')
    return hm, mkv


def _mem_kv_bwd(mem2, g, w, hm, dmk, dmv, tag):
    dmkv = jnp.concatenate([dmk, dmv], axis=1)
    dw = _matmul(hm, dmkv, 'tn', F32, name=f'mm_memkv_dw_{tag}')
    dhm = _matmul(dmkv, w, 'nt', F32, name=f'mm_memkv_dx_{tag}')
    _, dg = _rms_bwd(mem2, g, dhm, None, name=f'rms_mem_bwd_{tag}')
    return dw, dg


def _ffn_fwd(x, g, w_up, cw, cb, w_down, B, S, tag):
    T = x.shape[0]
    h2 = _rms_fwd(x, g, name=f'rms_ffn_{tag}')
    u = _matmul(h2, w_up, 'nn', ACT_DTYPE, name=f'mm_up_{tag}')
    y = _conv_fwd(u.reshape(B, S, -1), cw, cb, name=f'conv_fwd_{tag}').reshape(T, -1)
    x2 = _matmul(y, w_down, 'nn', F32, res=x, name=f'mm_down_{tag}')
    return x2, (h2, u, y)


def _ffn_bwd(dx2, x, g, w_up, cw, cb, w_down, saved, B, S, tag):
    h2, u, y = saved
    T = x.shape[0]
    dy = _matmul(dx2, w_down, 'nt', ACT_DTYPE, name=f'mm_down_dx_{tag}')
    dw_down = _matmul(y, dx2, 'tn', F32, name=f'mm_down_dw_{tag}')
    dug, duv, dcwg, dcwv, dcbg, dcbv = _conv_bwd(u.reshape(B, S, -1), cw, cb, dy.reshape(B, S, -1),
                                                  name=f'conv_bwd_{tag}')
    du = jnp.concatenate([dug.reshape(T, -1), duv.reshape(T, -1)], axis=1)
    dh2 = _matmul(du, w_up, 'nt', F32, name=f'mm_up_dx_{tag}')
    dw_up = _matmul(h2, du, 'tn', F32, slots=N_CHIPS, name=f'mm_up_dw_{tag}')
    dx, dg = _rms_bwd(x, g, dh2, dx2, name=f'rms_ffn_bwd_{tag}')
    dcw = jnp.concatenate([dcwg, dcwv], axis=1)
    dcb = jnp.concatenate([dcbg, dcbv], axis=1)
    return dx, dg, dw_up, dcw, dcb, dw_down


def _step(x, mem, tgt, W):
    B, S, D = x.shape
    T = B * S
    x0 = x.reshape(T, D)
    mem2 = mem.reshape(-1, D)
    tgt2 = tgt.reshape(T, D)
    row = lambda v: v.reshape(1, -1)
    q3 = 3 * MAIN_WIDTH

    w_in_a = W['w_in_a'][0]
    wa_main = jnp.concatenate([w_in_a[:, :q3], w_in_a[:, q3 + N_MAIN_HEADS:]], axis=1)
    wa_gate = jnp.pad(w_in_a[:, q3:q3 + N_MAIN_HEADS], ((0, 0), (0, LANES - N_MAIN_HEADS)))
    w_in_b = W['w_in_b'][0]
    bcol = jnp.tile(W['b_f_a'][0], B).reshape(B * N_MAIN_HEADS, 1)
    nkb = S // min(FOX_BLOCK, S)

    h1a = _rms_fwd(x0, row(W['ln_mix_g'][0]), name='rms_mix_a')
    pa = _matmul(h1a, wa_main, 'nn', ACT_DTYPE, name='mm_in_a')
    flog = _matmul(h1a, wa_gate, 'nn', F32, name='mm_gate_a')
    qkv_a = (pa, pa, pa)
    offs_a = (0, N_MAIN_PAIRS, 2 * N_MAIN_PAIRS)
    qm_off_a = 3 * N_MAIN_PAIRS
    zt = _gate_rows(flog[:, :N_MAIN_HEADS], B, S)
    cum = _gate_fwd(zt, bcol)
    ccol = cum.reshape(B * N_MAIN_HEADS, S, 1)
    crow = cum.reshape(B * N_MAIN_HEADS, nkb, 1, S // nkb)
    oa, lse = _fox_fwd(qkv_a, offs_a, B, S, ccol, crow)
    hma, mkva = _mem_kv_fwd(mem2, row(W['ln_mem_g'][0]), W['w_memkv'][0], 'a')
    oma = _mem_fwd(pa, qm_off_a, mkva, B, S, name='mem_fwd_a')
    ocat_a = jnp.concatenate([oa, oma], axis=1)
    x1 = _matmul(ocat_a, W['w_out'][0], 'nn', F32, res=x0, name='mm_out_a')
    x2, ffn_a = _ffn_fwd(x1, row(W['ln_ffn_g'][0]), W['w_up'][0], W['conv_w'][0], row(W['conv_b'][0]),
                         W['w_down'][0], B, S, 'a')
    hkv = _rms_fwd(x2, row(W['ln_kv_g']), name='rms_kv')
    kvs = _matmul(hkv, W['w_kv'], 'nn', ACT_DTYPE, name='mm_kv')
    h1b = _rms_fwd(x2, row(W['ln_mix_g'][1]), name='rms_mix_b')
    pb = _matmul(h1b, w_in_b, 'nn', ACT_DTYPE, name='mm_in_b')
    qkv_b = (pb, kvs, kvs)
    offs_b = (0, 0, N_MAIN_PAIRS)
    qm_off_b = N_MAIN_PAIRS
    ob = _sb_fwd(qkv_b, offs_b, B, S)
    hmb, mkvb = _mem_kv_fwd(mem2, row(W['ln_mem_g'][1]), W['w_memkv'][1], 'b')
    omb = _mem_fwd(pb, qm_off_b, mkvb, B, S, name='mem_fwd_b')
    ocat_b = jnp.concatenate([ob, omb], axis=1)
    x3 = _matmul(ocat_b, W['w_out'][1], 'nn', F32, res=x2, name='mm_out_b')
    x4, ffn_b = _ffn_fwd(x3, row(W['ln_ffn_g'][1]), W['w_up'][1], W['conv_w'][1], row(W['conv_b'][1]),
                         W['w_down'][1], B, S, 'b')
    loss, dx4, d_final_g = _final_loss(x4, row(W['final_g']), tgt2)

    dx3, dg_ffn_b, dw_up_b, dcw_b, dcb_b, dw_down_b = _ffn_bwd(
        dx4, x3, row(W['ln_ffn_g'][1]), W['w_up'][1], W['conv_w'][1], row(W['conv_b'][1]), W['w_down'][1],
        ffn_b, B, S, 'b')
    docat = _matmul(dx3, W['w_out'][1], 'nt', ACT_DTYPE, name='mm_out_dx_b')
    dw_out_b = _matmul(ocat_b, dx3, 'tn', F32, name='mm_out_dw_b')
    dqb, dkb, dvb = _sb_bwd(qkv_b, offs_b, B, S, docat)
    dqmb, dmkb, dmvb = _mem_bwd(pb, qm_off_b, mkvb, B, S, docat, N_MAIN_PAIRS, name='mem_bwd_b')
    dw_memkv_b, dg_mem_b = _mem_kv_bwd(mem2, row(W['ln_mem_g'][1]), W['w_memkv'][1], hmb, dmkb, dmvb, 'b')
    dpb = jnp.concatenate([dqb, dqmb], axis=1)
    dh1b = _matmul(dpb, w_in_b, 'nt', F32, name='mm_in_dx_b')
    dw_in_b = _matmul(h1b, dpb, 'tn', F32, name='mm_in_dw_b')
    dx2, dg_mix_b = _rms_bwd(x2, row(W['ln_mix_g'][1]), dh1b, dx3, name='rms_mix_bwd_b')
    dkvs = jnp.concatenate([dkb, dvb], axis=1)
    dhkv = _matmul(dkvs, W['w_kv'], 'nt', F32, name='mm_kv_dx')
    dw_kv = _matmul(hkv, dkvs, 'tn', F32, slots=N_CHIPS, name='mm_kv_dw')
    dx2, dg_kv = _rms_bwd(x2, row(W['ln_kv_g']), dhkv, dx2, name='rms_kv_bwd')

    dx1, dg_ffn_a, dw_up_a, dcw_a, dcb_a, dw_down_a = _ffn_bwd(
        dx2, x1, row(W['ln_ffn_g'][0]), W['w_up'][0], W['conv_w'][0], row(W['conv_b'][0]), W['w_down'][0],
        ffn_a, B, S, 'a')
    docat = _matmul(dx1, W['w_out'][0], 'nt', ACT_DTYPE, name='mm_out_dx_a')
    dw_out_a = _matmul(ocat_a, dx1, 'tn', F32, name='mm_out_dw_a')
    dqa, dka, dva, dccol, dcrow = _fox_bwd(qkv_a, offs_a, B, S, ccol, crow, oa, lse, docat)
    dzt, dbrow = _gate_bwd(zt, bcol, dccol.reshape(B * N_MAIN_HEADS, S) + dcrow.reshape(B * N_MAIN_HEADS, S))
    dqma, dmka, dmva = _mem_bwd(pa, qm_off_a, mkva, B, S, docat, N_MAIN_PAIRS, name='mem_bwd_a')
    dw_memkv_a, dg_mem_a = _mem_kv_bwd(mem2, row(W['ln_mem_g'][0]), W['w_memkv'][0], hma, dmka, dmva, 'a')
    dpa = jnp.concatenate([dqa, dka, dva, dqma], axis=1)
    dflog = jnp.pad(_gate_cols(dzt, B, S), ((0, 0), (0, LANES - N_MAIN_HEADS)))
    dh1a = _matmul(dpa, wa_main, 'nt', F32, name='mm_in_dx_a')
    dh1a = _matmul(dflog, wa_gate, 'nt', F32, res=dh1a, name='mm_gate_dx_a')
    dwa_main = _matmul(h1a, dpa, 'tn', F32, name='mm_in_dw_a')
    dwa_gate = _matmul(h1a, dflog, 'tn', F32, name='mm_gate_dw_a')
    dx0, dg_mix_a = _rms_bwd(x0, row(W['ln_mix_g'][0]), dh1a, dx1, name='rms_mix_bwd_a')

    dw_in_a = jnp.concatenate([dwa_main[:, :q3], dwa_gate[:, :N_MAIN_HEADS], dwa_main[:, q3:]], axis=1)
    def by_rows(dw):
        return dw.reshape(N_CHIPS, dw.shape[0] // N_CHIPS, dw.shape[1])

    grads = {
        'ln_mix_g': jnp.concatenate([dg_mix_a, dg_mix_b], axis=0),
        'w_in_a': dw_in_a[None],
        'b_f_a': dbrow.reshape(B, N_MAIN_HEADS).sum(axis=0)[None],
        'w_in_b': [by_rows(dw_in_b)],
        'ln_kv_g': dg_kv[0],
        'w_kv': [dw_kv],
        'ln_mem_g': jnp.concatenate([dg_mem_a, dg_mem_b], axis=0),
        'w_memkv': [by_rows(dw_memkv_a), by_rows(dw_memkv_b)],
        'w_out': [by_rows(dw_out_a), by_rows(dw_out_b)],
        'ln_ffn_g': jnp.concatenate([dg_ffn_a, dg_ffn_b], axis=0),
        'w_up': [dw_up_a, dw_up_b],
        'conv_w': jnp.stack([dcw_a, dcw_b]),
        'conv_b': jnp.concatenate([dcb_a, dcb_b], axis=0),
        'w_down': [by_rows(dw_down_a), by_rows(dw_down_b)],
        'final_g': d_final_g[0],
    }
    return loss, dx0.reshape(B, S, D), grads


BLOCKED = ('w_in_b', 'w_kv', 'w_memkv', 'w_out', 'w_up', 'w_down')
MISC_ROWS = 32


def _misc_names():
    return [n for n in PARAM_NAMES if PARAM_SHARD_AXIS[n] is None] + ['conv_w']


def _reduce_grads(grads, shards):
    x, y, c = _my_place()
    c_arr = jnp.reshape(c, (1,)).astype(jnp.int32)
    place_arr = jnp.stack([2 * x + y, c]).astype(jnp.int32)

    a_cols = shards['w_in_a'].shape[2]
    a_pad = -(-a_cols // LANES) * LANES
    dw_in_a = grads['w_in_a'][0]
    in_a = jnp.stack([jnp.pad(dw_in_a[:, k * a_cols:(k + 1) * a_cols], ((0, 0), (0, a_pad - a_cols)))
                      for k in range(N_CHIPS)])
    conv_cols = shards['conv_w'].shape[2]
    misc = []
    for k in range(N_CHIPS):
        parts = [grads[n].reshape(-1) for n in _misc_names()[:-1]]
        parts.append(grads['conv_w'][:, :, k * conv_cols:(k + 1) * conv_cols].reshape(-1))
        flat = jnp.concatenate(parts)
        assert flat.shape[0] <= MISC_ROWS * PACK_COLS
        misc.append(jnp.pad(flat, (0, MISC_ROWS * PACK_COLS - flat.shape[0])).reshape(MISC_ROWS, PACK_COLS))
    arrays = [in_a, jnp.stack(misc)]
    owners = [('w_in_a', 0), ('misc', 0)]
    for n in BLOCKED:
        for layer, g in enumerate(grads[n]):
            arrays.append(g)
            owners.append((n, layer))

    wire = [jnp.bfloat16, F32] + [jnp.bfloat16] * (len(arrays) - 2)
    others = _swap_cores(arrays)
    qs = [_add_cores(g, o, c_arr, wire[i], name=f'add_cores_{i}') for i, (g, o) in enumerate(zip(arrays, others))]
    gots = _send_chips(qs)
    sums = [_sum_chips(q, g, place_arr, name=f'sum_chips_{i}') for i, (q, g) in enumerate(zip(qs, gots))]
    joined = _join_cores(sums)

    out = {n: [] for n in PARAM_NAMES}
    for (n, layer), j in zip(owners, joined):
        rows = j.reshape(-1, j.shape[-1])
        if n == 'w_in_a':
            out[n].append(rows[:, :a_cols])
        elif n == 'misc':
            flat = rows.reshape(-1)
            off = 0
            for name in _misc_names():
                shape = shards[name].shape
                size = math.prod(shape)
                out[name].append(flat[off:off + size].reshape(-1, shape[-1]))
                off += size
        else:
            out[n].append(rows)
    return out


def kernel(x, mem, ln_mix_g, w_in_a, b_f_a, w_in_b, ln_kv_g, w_kv, ln_mem_g, w_memkv, w_out, ln_ffn_g, w_up, conv_w, conv_b, w_down, final_g, loss_target, m_ln_mix_g, m_w_in_a, m_b_f_a, m_w_in_b, m_ln_kv_g, m_w_kv, m_ln_mem_g, m_w_memkv, m_w_out, m_ln_ffn_g, m_w_up, m_conv_w, m_conv_b, m_w_down, m_final_g, v_ln_mix_g, v_w_in_a, v_b_f_a, v_w_in_b, v_ln_kv_g, v_w_kv, v_ln_mem_g, v_w_memkv, v_w_out, v_ln_ffn_g, v_w_up, v_conv_w, v_conv_b, v_w_down, v_final_g):
    shards = dict(ln_mix_g=ln_mix_g, w_in_a=w_in_a, b_f_a=b_f_a, w_in_b=w_in_b, ln_kv_g=ln_kv_g, w_kv=w_kv,
                  ln_mem_g=ln_mem_g, w_memkv=w_memkv, w_out=w_out, ln_ffn_g=ln_ffn_g, w_up=w_up, conv_w=conv_w,
                  conv_b=conv_b, w_down=w_down, final_g=final_g)
    moments_m = dict(ln_mix_g=m_ln_mix_g, w_in_a=m_w_in_a, b_f_a=m_b_f_a, w_in_b=m_w_in_b, ln_kv_g=m_ln_kv_g,
                     w_kv=m_w_kv, ln_mem_g=m_ln_mem_g, w_memkv=m_w_memkv, w_out=m_w_out, ln_ffn_g=m_ln_ffn_g,
                     w_up=m_w_up, conv_w=m_conv_w, conv_b=m_conv_b, w_down=m_w_down, final_g=m_final_g)
    moments_v = dict(ln_mix_g=v_ln_mix_g, w_in_a=v_w_in_a, b_f_a=v_b_f_a, w_in_b=v_w_in_b, ln_kv_g=v_ln_kv_g,
                     w_kv=v_w_kv, ln_mem_g=v_ln_mem_g, w_memkv=v_w_memkv, w_out=v_w_out, ln_ffn_g=v_ln_ffn_g,
                     w_up=v_w_up, conv_w=v_conv_w, conv_b=v_conv_b, w_down=v_w_down, final_g=v_final_g)

    kinds = {'w_in_a': 'stack', 'w_in_b': 'rows', 'w_kv': 'cols', 'w_memkv': 'rows', 'w_out': 'rows', 'w_up': 'cols',
             'w_down': 'rows', 'conv_w': 'cols'}
    mx, my, _ = _my_place()
    chip_arr = jnp.reshape(2 * mx + my, (1,)).astype(jnp.int32)
    placed, shard_shapes = [], []
    for n, kind in kinds.items():
        w = shards[n].reshape((-1,) + shards[n].shape[-2:])
        shard_shapes.append(w.shape)
        placed.append(_place_block(w, kind, F32 if n in F32_GATHERED else jnp.bfloat16, chip_arr, name=f'place_{n}'))
    full = _gather_weights(placed, shard_shapes, list(kinds.values()), [n not in F32_GATHERED for n in kinds])
    W = dict(shards)
    for n, f in zip(kinds, full):
        W[n] = f
    W['w_in_a'] = jnp.concatenate([W['w_in_a'][k] for k in range(N_CHIPS)], axis=1)[None]
    W['w_kv'] = W['w_kv'][0]

    loss_part, grad_x, grads = _step(x, mem, loss_target, W)
    loss = lax.psum(loss_part[0, 0], ('x', 'y', 'c'))

    g_layers = _reduce_grads(grads, shards)

    results = {}
    for name in PARAM_NAMES:
        w = shards[name]
        layers = len(g_layers[name])
        as_layers = (layers, -1, w.shape[-1])
        w3, m3, v3 = (t.reshape(as_layers) for t in (w, moments_m[name], moments_v[name]))
        res = None
        for layer, g in enumerate(g_layers[name]):
            res = _adamw(w3, g, m3, v3, layer, res, name=f'adamw_{name}_{layer}')
        results[name] = [t.reshape(w.shape) for t in res]

    return (loss, grad_x, *[results[n][k] for k in range(4) for n in PARAM_NAMES])
```

```python
import functools
import math

import jax
import jax.numpy as jnp
from jax import lax
from jax.experimental import pallas as pl
from jax.experimental.pallas import tpu as pltpu

F32 = jnp.float32
MXU_DTYPE = jnp.bfloat16
ACT_DTYPE = jnp.bfloat16

HEAD_DIM = 64
N_MAIN_HEADS = 12
N_MEM_HEADS = 4
MAIN_WIDTH = N_MAIN_HEADS * HEAD_DIM
MEM_WIDTH = N_MEM_HEADS * HEAD_DIM
EPS = 1e-6
SCALE = HEAD_DIM ** -0.5
NEG_BIG = -1e30
LANES = 128
PACK_COLS = 1024
N_CHIPS = 4

ADAM_LR = 0.001
ADAM_B1 = 0.9
ADAM_B2 = 0.999
ADAM_EPS = 1e-08
ADAM_WD = 0.01
ADAM_STEP = 10

MESH = pl.DeviceIdType.MESH
ANY = pl.BlockSpec(memory_space=pl.ANY)

PARAM_SHARD_AXIS = {
    'ln_mix_g': None, 'w_in_a': 2, 'b_f_a': None, 'w_in_b': 1, 'ln_kv_g': None, 'w_kv': 1,
    'ln_mem_g': None, 'w_memkv': 1, 'w_out': 1, 'ln_ffn_g': None, 'w_up': 2, 'conv_w': 2,
    'conv_b': None, 'w_down': 1, 'final_g': None,
}
PARAM_NAMES = list(PARAM_SHARD_AXIS)
F32_GATHERED = ('conv_w',)


def _tile(n, pref, unit=LANES):
    if n <= pref:
        return n
    best = None
    for t in range(unit, pref + 1, unit):
        if n % t == 0:
            best = t
    assert best is not None, (n, pref)
    return best


MM_ACC_ELEMS = 768 * 1024
MM_VMEM_MB = 48


def _out_tiles(M, N):
    def divisors(n, cap):
        if n <= LANES:
            return [n]
        return [t for t in range(LANES, min(n, cap) + 1, LANES) if n % t == 0]

    best = None
    for tm in divisors(M, 1024):
        for tn in divisors(N, 2048):
            if tm * tn <= MM_ACC_ELEMS and (best is None or (tm * tn, tn) > (best[0] * best[1], best[1])):
                best = (tm, tn)
    assert best is not None, (M, N)
    return best


def _params(*sem, vmem_mb=None):
    kw = {}
    if sem:
        kw['dimension_semantics'] = sem
    if vmem_mb is not None:
        kw['vmem_limit_bytes'] = vmem_mb * 1024 * 1024
    return pltpu.CompilerParams(**kw)


def _dot(a, b, dims):
    return lax.dot_general(a.astype(MXU_DTYPE), b.astype(MXU_DTYPE), (dims, ((), ())),
                           preferred_element_type=F32)


NN = ((1,), (0,))
NT = ((1,), (1,))
TN = ((0,), (0,))


def _matmul(a, b, mode, out_dtype, res=None, slots=1, name='mm'):
    if mode == 'nn':
        (M, K), (K2, N) = a.shape, b.shape
    elif mode == 'nt':
        (M, K), (N, K2) = a.shape, b.shape
    else:
        (K, M), (K2, N) = a.shape, b.shape
    assert K == K2 and N % slots == 0, (a.shape, b.shape, mode, slots)
    slot_cols = N // slots
    tm, tn = _out_tiles(M, slot_cols)
    per_slot = slot_cols // tn
    tk = _tile(K, 1024)
    nk = K // tk
    dims = {'nn': NN, 'nt': NT, 'tn': TN}[mode]

    def body(*refs):
        if res is None:
            a_ref, b_ref, o_ref, acc = refs
        else:
            a_ref, b_ref, r_ref, o_ref, acc = refs
        k = pl.program_id(2)

        @pl.when(k == 0)
        def _():
            acc[...] = jnp.zeros_like(acc)

        acc[...] += _dot(a_ref[...], b_ref[...], dims)

        @pl.when(k == nk - 1)
        def _():
            out = acc[...]
            if res is not None:
                out = out + r_ref[...]
            o_ref[...] = out.astype(out_dtype)

    if mode == 'tn':
        a_spec = pl.BlockSpec((tk, tm), lambda i, j, k: (k, i))
    else:
        a_spec = pl.BlockSpec((tm, tk), lambda i, j, k: (i, k))
    if mode == 'nt':
        b_spec = pl.BlockSpec((tn, tk), lambda i, j, k: (j, k))
    else:
        b_spec = pl.BlockSpec((tk, tn), lambda i, j, k: (k, j))
    if slots == 1:
        o_spec = pl.BlockSpec((tm, tn), lambda i, j, k: (i, j))
        out_shape = jax.ShapeDtypeStruct((M, N), out_dtype)
    else:
        assert res is None
        o_spec = pl.BlockSpec((None, tm, tn), lambda i, j, k: (j // per_slot, i, j % per_slot))
        out_shape = jax.ShapeDtypeStruct((slots, M, slot_cols), out_dtype)
    in_specs = [a_spec, b_spec] + ([o_spec] if res is not None else [])
    args = (a, b) + ((res,) if res is not None else ())
    return pl.pallas_call(
        body, name=name, grid=(M // tm, N // tn, nk),
        in_specs=in_specs, out_specs=o_spec,
        out_shape=out_shape,
        scratch_shapes=[pltpu.VMEM((tm, tn), F32)],
        compiler_params=_params('parallel', 'parallel', 'arbitrary', vmem_mb=MM_VMEM_MB),
    )(*args)


def _rms_fwd(x, g, name):
    T, D = x.shape
    tr = _tile(T, 512)

    def body(x_ref, g_ref, o_ref):
        xv = x_ref[...]
        r = lax.rsqrt(jnp.mean(xv * xv, axis=-1, keepdims=True) + EPS)
        o_ref[...] = (xv * r * g_ref[...]).astype(ACT_DTYPE)

    return pl.pallas_call(
        body, name=name, grid=(T // tr,),
        in_specs=[pl.BlockSpec((tr, D), lambda i: (i, 0)), pl.BlockSpec((1, D), lambda i: (0, 0))],
        out_specs=pl.BlockSpec((tr, D), lambda i: (i, 0)),
        out_shape=jax.ShapeDtypeStruct((T, D), ACT_DTYPE),
        compiler_params=_params('parallel'),
    )(x, g)


def _rms_bwd(x, g, dh, dres, name):
    T, D = x.shape
    tr = _tile(T, 512)
    want_dx = dres is not None

    def body(*refs):
        if want_dx:
            x_ref, g_ref, dh_ref, dres_ref, dx_ref, dg_ref = refs
        else:
            x_ref, g_ref, dh_ref, dg_ref = refs
        i = pl.program_id(0)

        @pl.when(i == 0)
        def _():
            dg_ref[...] = jnp.zeros_like(dg_ref)

        xv = x_ref[...]
        dhv = dh_ref[...].astype(F32)
        r = lax.rsqrt(jnp.mean(xv * xv, axis=-1, keepdims=True) + EPS)
        n = xv * r
        dg_ref[...] += jnp.sum(dhv * n, axis=0, keepdims=True)
        if want_dx:
            dn = dhv * g_ref[...]
            dx = r * (dn - n * jnp.mean(dn * n, axis=-1, keepdims=True))
            dx_ref[...] = dres_ref[...] + dx

    row = pl.BlockSpec((tr, D), lambda i: (i, 0))
    vec = pl.BlockSpec((1, D), lambda i: (0, 0))
    if want_dx:
        return pl.pallas_call(
            body, name=name, grid=(T // tr,),
            in_specs=[row, vec, row, row], out_specs=[row, vec],
            out_shape=[jax.ShapeDtypeStruct((T, D), F32), jax.ShapeDtypeStruct((1, D), F32)],
            compiler_params=_params('arbitrary'),
        )(x, g, dh, dres)
    dg = pl.pallas_call(
        body, name=name, grid=(T // tr,),
        in_specs=[row, vec, row], out_specs=vec,
        out_shape=jax.ShapeDtypeStruct((1, D), F32),
        compiler_params=_params('arbitrary'),
    )(x, g, dh)
    return None, dg


def _final_loss(x, g, tgt, name='final_loss'):
    T, D = x.shape
    tr = _tile(T, 512)

    def body(x_ref, g_ref, t_ref, loss_ref, dx_ref, dg_ref):
        i = pl.program_id(0)

        @pl.when(i == 0)
        def _():
            loss_ref[...] = jnp.zeros_like(loss_ref)
            dg_ref[...] = jnp.zeros_like(dg_ref)

        xv = x_ref[...]
        gv = g_ref[...]
        r = lax.rsqrt(jnp.mean(xv * xv, axis=-1, keepdims=True) + EPS)
        n = xv * r
        e = n * gv - t_ref[...]
        per_tok = jnp.mean(e * e, axis=-1, keepdims=True)
        loss_ref[...] += 0.5 * jnp.sum(per_tok, axis=0, keepdims=True)
        dy = e * (1.0 / D)
        dg_ref[...] += jnp.sum(dy * n, axis=0, keepdims=True)
        dn = dy * gv
        dx_ref[...] = r * (dn - n * jnp.mean(dn * n, axis=-1, keepdims=True))

    row = pl.BlockSpec((tr, D), lambda i: (i, 0))
    vec = pl.BlockSpec((1, D), lambda i: (0, 0))
    one = pl.BlockSpec((1, 1), lambda i: (0, 0))
    return pl.pallas_call(
        body, name=name, grid=(T // tr,),
        in_specs=[row, vec, row], out_specs=[one, row, vec],
        out_shape=[jax.ShapeDtypeStruct((1, 1), F32), jax.ShapeDtypeStruct((T, D), F32),
                   jax.ShapeDtypeStruct((1, D), F32)],
        compiler_params=_params('arbitrary'),
    )(x, g, tgt)


def _log_sigmoid(z):
    return jnp.minimum(z, 0.0) - jnp.log(1.0 + jnp.exp(-jnp.abs(z)))


def _tri(n, rel):
    j = lax.broadcasted_iota(jnp.int32, (n, n), 0)
    s = lax.broadcasted_iota(jnp.int32, (n, n), 1)
    return rel(j, s).astype(MXU_DTYPE)


def _split_dot(x, tri, terms):
    if MXU_DTYPE == F32:
        return jnp.dot(x, tri, preferred_element_type=F32)
    out = None
    rem = x
    for _ in range(terms):
        piece = rem.astype(MXU_DTYPE)
        part = jnp.dot(piece, tri, preferred_element_type=F32)
        out = part if out is None else out + part
        rem = rem - piece.astype(F32)
    return out


def _gate_fwd(zt, bcol, name='gate_fwd'):
    BH, S = zt.shape
    nb = S // LANES

    def body(z_ref, b_ref, c_ref):
        tri = _tri(LANES, lambda j, s: j <= s)
        carry = jnp.zeros((BH, 1), F32)
        for i in range(nb):
            sl = slice(i * LANES, (i + 1) * LANES)
            logf = _log_sigmoid(z_ref[:, sl] + b_ref[...])
            cs = _split_dot(logf, tri, 3) + carry
            c_ref[:, sl] = cs
            carry = cs[:, LANES - 1:LANES]

    return pl.pallas_call(body, name=name, out_shape=jax.ShapeDtypeStruct((BH, S), F32))(zt, bcol)


def _gate_bwd(zt, bcol, dc, name='gate_bwd'):
    BH, S = zt.shape
    nb = S // LANES

    def body(z_ref, b_ref, dc_ref, dz_ref, db_ref):
        tri = _tri(LANES, lambda j, s: j >= s)
        carry = jnp.zeros((BH, 1), F32)
        dsum = jnp.zeros((BH, 1), F32)
        for i in reversed(range(nb)):
            sl = slice(i * LANES, (i + 1) * LANES)
            rs = _split_dot(dc_ref[:, sl], tri, 3) + carry
            carry = rs[:, 0:1]
            z = z_ref[:, sl] + b_ref[...]
            dz = rs * (1.0 - 1.0 / (1.0 + jnp.exp(-z)))
            dz_ref[:, sl] = dz
            dsum = dsum + jnp.sum(dz, axis=-1, keepdims=True)
        db_ref[...] = dsum

    return pl.pallas_call(
        body, name=name,
        out_shape=[jax.ShapeDtypeStruct((BH, S), F32), jax.ShapeDtypeStruct((BH, 1), F32)],
    )(zt, bcol, dc)


FOX_BLOCK = 256


PAIR = 2 * HEAD_DIM
N_MAIN_PAIRS = N_MAIN_HEADS // 2
N_MEM_PAIRS = N_MEM_HEADS // 2


def _lane0(shape):
    return lax.broadcasted_iota(jnp.int32, shape, len(shape) - 1) < HEAD_DIM


def _per_head(x):
    first = _lane0(x.shape)
    zero = jnp.zeros_like(x)
    return jnp.where(first, x, zero), jnp.where(first, zero, x)


def _pick(first, a, b):
    return jnp.where(first, a, b)


def _q_spec(bq, nq, off):
    return pl.BlockSpec((bq, PAIR), lambda b, j, i: (b * nq + i, off + j))


def _seq_spec(S, off):
    return pl.BlockSpec((S, PAIR), lambda b, j, i: (b, off + j))


def _gate_specs(bq, nk):
    col = pl.BlockSpec((2, bq, 1), lambda b, j, i: (b * N_MAIN_PAIRS + j, i, 0))
    rowv = pl.BlockSpec((2, nk, 1, bq), lambda b, j, i: (b * N_MAIN_PAIRS + j, 0, 0, 0))
    return col, rowv


def _causal(i, kb, bq, strict):
    row = i * bq + lax.broadcasted_iota(jnp.int32, (bq, bq), 0)
    col = kb * bq + lax.broadcasted_iota(jnp.int32, (bq, bq), 1)
    return (col < row) if strict else (col <= row)


def _fox_fwd(qkv, offs, B, S, ccol, crow, beside=None, name='fox_fwd'):
    bq = min(FOX_BLOCK, S)
    nq = S // bq

    def body(q_ref, k_ref, v_ref, cc_ref, cr_ref, o_ref, lse_ref):
        i = pl.program_id(2)
        qh = _per_head(q_ref[...])
        first = _lane0((bq, PAIR))

        def step(kb, carry):
            m, l, acc = carry
            sl = pl.ds(pl.multiple_of(kb * bq, bq), bq)
            ks, vs = k_ref[sl, :], v_ref[sl, :]
            mask = _causal(i, kb, bq, False)
            m_new, l_new, alpha, pv = [], [], [], []
            for h in range(2):
                s = _dot(qh[h], ks, NT) * SCALE + cc_ref[h] - cr_ref[h, kb]
                s = jnp.where(mask, s, NEG_BIG)
                mh = jnp.maximum(m[h], jnp.max(s, axis=-1, keepdims=True))
                p = jnp.exp(s - mh)
                ah = jnp.exp(m[h] - mh)
                m_new.append(mh)
                alpha.append(ah)
                l_new.append(ah * l[h] + jnp.sum(p, axis=-1, keepdims=True))
                pv.append(_dot(p, vs, NN))
            acc = _pick(first, alpha[0], alpha[1]) * acc + _pick(first, pv[0], pv[1])
            return tuple(m_new), tuple(l_new), acc

        neg = jnp.full((bq, 1), NEG_BIG, F32)
        zero = jnp.zeros((bq, 1), F32)
        m, l, acc = lax.fori_loop(0, i + 1, step, ((neg, neg), (zero, zero), jnp.zeros((bq, PAIR), F32)))
        o_ref[...] = (acc / _pick(first, l[0], l[1])).astype(ACT_DTYPE)
        for h in range(2):
            lse_ref[h] = m[h] + jnp.log(l[h])

    col, rowv = _gate_specs(bq, nq)
    return _call_beside(
        body, name, (B, N_MAIN_PAIRS, nq),
        [_q_spec(bq, nq, offs[0]), _seq_spec(S, offs[1]), _seq_spec(S, offs[2]), col, rowv],
        [_q_spec(bq, nq, 0), col],
        [jax.ShapeDtypeStruct((B * S, MAIN_WIDTH), ACT_DTYPE), jax.ShapeDtypeStruct((B * N_MAIN_HEADS, S, 1), F32)],
        [], (*qkv, ccol, crow), ('parallel', 'parallel', 'arbitrary'), beside)


def _fox_bwd(qkv, offs, B, S, ccol, crow, o, lse, do, beside=None, name='fox_bwd'):
    bq = min(FOX_BLOCK, S)
    nq = S // bq

    def body(q_ref, k_ref, v_ref, cc_ref, cr_ref, o_ref, lse_ref, do_ref,
             dq_ref, dk_ref, dv_ref, dcc_ref, dcr_ref, dk_acc, dv_acc):
        i = pl.program_id(2)

        @pl.when(i == 0)
        def _():
            dk_acc[...] = jnp.zeros_like(dk_acc)
            dv_acc[...] = jnp.zeros_like(dv_acc)
            dcr_ref[...] = jnp.zeros_like(dcr_ref)

        qv = q_ref[...]
        dov = do_ref[...]
        qh = _per_head(qv)
        doh = _per_head(dov)
        first = _lane0((bq, PAIR))
        prod = dov.astype(F32) * o_ref[...].astype(F32)
        dsum = [jnp.sum(t, axis=-1, keepdims=True) for t in _per_head(prod)]

        def step(kb, carry):
            dq, dcc = carry
            sl = pl.ds(pl.multiple_of(kb * bq, bq), bq)
            ks, vs = k_ref[sl, :], v_ref[sl, :]
            mask = _causal(i, kb, bq, False)
            dqh, dkh, dvh, dcc_new = [], [], [], []
            for h in range(2):
                s = _dot(qh[h], ks, NT) * SCALE + cc_ref[h] - cr_ref[h, kb]
                p = jnp.where(mask, jnp.exp(s - lse_ref[h]), 0.0)
                ds = p * (_dot(doh[h], vs, NT) - dsum[h])
                dqh.append(_dot(ds, ks, NN))
                dkh.append(_dot(ds, qv, TN))
                dvh.append(_dot(p, dov, TN))
                dcr_ref[h, kb] -= jnp.sum(ds, axis=0, keepdims=True)
                dcc_new.append(dcc[h] + jnp.sum(ds, axis=-1, keepdims=True))
            dk_acc[sl, :] += SCALE * _pick(first, dkh[0], dkh[1])
            dv_acc[sl, :] += _pick(first, dvh[0], dvh[1])
            return dq + _pick(first, dqh[0], dqh[1]), tuple(dcc_new)

        zero = jnp.zeros((bq, 1), F32)
        dq, dcc = lax.fori_loop(0, i + 1, step, (jnp.zeros((bq, PAIR), F32), (zero, zero)))
        dq_ref[...] = (dq * SCALE).astype(ACT_DTYPE)
        for h in range(2):
            dcc_ref[h] = dcc[h]

        @pl.when(i == nq - 1)
        def _():
            dk_ref[...] = dk_acc[...].astype(ACT_DTYPE)
            dv_ref[...] = dv_acc[...].astype(ACT_DTYPE)

    col, rowv = _gate_specs(bq, nq)
    qs, seq = _q_spec(bq, nq, 0), _seq_spec(S, 0)
    full = jax.ShapeDtypeStruct((B * S, MAIN_WIDTH), ACT_DTYPE)
    return _call_beside(
        body, name, (B, N_MAIN_PAIRS, nq),
        [_q_spec(bq, nq, offs[0]), _seq_spec(S, offs[1]), _seq_spec(S, offs[2]), col, rowv, qs, col, qs],
        [qs, seq, seq, col, rowv],
        [full, full, full, jax.ShapeDtypeStruct(ccol.shape, F32), jax.ShapeDtypeStruct(crow.shape, F32)],
        [pltpu.VMEM((S, PAIR), F32), pltpu.VMEM((S, PAIR), F32)],
        (*qkv, ccol, crow, o, lse, do), ('parallel', 'parallel', 'arbitrary'), beside)


SB_BLOCK = 256


def _sb_block(qv, ks, i, kb, bq):
    z = _dot(qv, ks, NT) * SCALE
    row = i * bq + lax.broadcasted_iota(jnp.int32, (bq, bq), 0)
    col = kb * bq + lax.broadcasted_iota(jnp.int32, (bq, bq), 1)
    mask = col < row
    a = _log_sigmoid(z)
    l = jnp.where(mask, a - z, 0.0)
    return mask, a, l


def _sb_fwd(qkv, offs, B, S, name='sb_fwd'):
    bq = min(SB_BLOCK, S)
    nq = S // bq

    def body(q_ref, k_ref, v_ref, o_ref):
        i = pl.program_id(2)
        qh = _per_head(q_ref[...])
        first = _lane0((bq, PAIR))
        tri = _tri(bq, lambda j, s: j > s)

        def step(n, carry):
            acc, right = carry
            kb = i - n
            sl = pl.ds(pl.multiple_of(kb * bq, bq), bq)
            ks, vs = k_ref[sl, :], v_ref[sl, :]
            pv, right_new = [], []
            for h in range(2):
                mask, a, l = _sb_block(qh[h], ks, i, kb, bq)
                rsum = _split_dot(l, tri, 2) + right[h]
                w = jnp.where(mask, jnp.exp(a + rsum), 0.0)
                pv.append(_dot(w, vs, NN))
                right_new.append(right[h] + jnp.sum(l, axis=-1, keepdims=True))
            return acc + _pick(first, pv[0], pv[1]), tuple(right_new)

        zero = jnp.zeros((bq, 1), F32)
        acc, _ = lax.fori_loop(0, i + 1, step, (jnp.zeros((bq, PAIR), F32), (zero, zero)))
        o_ref[...] = acc.astype(ACT_DTYPE)

    return pl.pallas_call(
        body, name=name, grid=(B, N_MAIN_PAIRS, nq),
        in_specs=[_q_spec(bq, nq, offs[0]), _seq_spec(S, offs[1]), _seq_spec(S, offs[2])],
        out_specs=_q_spec(bq, nq, 0),
        out_shape=jax.ShapeDtypeStruct((B * S, MAIN_WIDTH), ACT_DTYPE),
        compiler_params=_params('parallel', 'parallel', 'arbitrary'),
    )(*qkv)


def _sb_bwd(qkv, offs, B, S, do, name='sb_bwd'):
    bq = min(SB_BLOCK, S)
    nq = S // bq

    def body(q_ref, k_ref, v_ref, do_ref, dq_ref, dk_ref, dv_ref, dk_acc, dv_acc):
        i = pl.program_id(2)

        @pl.when(i == 0)
        def _():
            dk_acc[...] = jnp.zeros_like(dk_acc)
            dv_acc[...] = jnp.zeros_like(dv_acc)

        qv = q_ref[...]
        dov = do_ref[...]
        qh = _per_head(qv)
        doh = _per_head(dov)
        first = _lane0((bq, PAIR))
        tri_incl = _tri(bq, lambda j, s: j <= s)
        tri_excl = _tri(bq, lambda j, s: j < s)

        def total(kb, tot):
            sl = pl.ds(pl.multiple_of(kb * bq, bq), bq)
            ks = k_ref[sl, :]
            return tuple(tot[h] + jnp.sum(_sb_block(qh[h], ks, i, kb, bq)[2], axis=-1, keepdims=True)
                         for h in range(2))

        zero = jnp.zeros((bq, 1), F32)
        tot = lax.fori_loop(0, i + 1, total, (zero, zero))

        def step(kb, carry):
            dq, left_l, left_g = carry
            sl = pl.ds(pl.multiple_of(kb * bq, bq), bq)
            ks, vs = k_ref[sl, :], v_ref[sl, :]
            dqh, dkh, dvh, new_l, new_g = [], [], [], [], []
            for h in range(2):
                mask, a, l = _sb_block(qh[h], ks, i, kb, bq)
                cum = _split_dot(l, tri_incl, 2) + left_l[h]
                w = jnp.where(mask, jnp.exp(a + tot[h] - cum), 0.0)
                g = w * _dot(doh[h], vs, NT)
                hsum = _split_dot(g, tri_excl, 1) + left_g[h]
                beta = jnp.exp(a)
                dz = jnp.where(mask, g * (1.0 - beta) - hsum * beta, 0.0)
                dqh.append(_dot(dz, ks, NN))
                dkh.append(_dot(dz, qv, TN))
                dvh.append(_dot(w, dov, TN))
                new_l.append(left_l[h] + jnp.sum(l, axis=-1, keepdims=True))
                new_g.append(left_g[h] + jnp.sum(g, axis=-1, keepdims=True))
            dk_acc[sl, :] += SCALE * _pick(first, dkh[0], dkh[1])
            dv_acc[sl, :] += _pick(first, dvh[0], dvh[1])
            return dq + _pick(first, dqh[0], dqh[1]), tuple(new_l), tuple(new_g)

        dq, _, _ = lax.fori_loop(0, i + 1, step, (jnp.zeros((bq, PAIR), F32), (zero, zero), (zero, zero)))
        dq_ref[...] = (dq * SCALE).astype(ACT_DTYPE)

        @pl.when(i == nq - 1)
        def _():
            dk_ref[...] = dk_acc[...].astype(ACT_DTYPE)
            dv_ref[...] = dv_acc[...].astype(ACT_DTYPE)

    qs, seq = _q_spec(bq, nq, 0), _seq_spec(S, 0)
    full = jax.ShapeDtypeStruct((B * S, MAIN_WIDTH), ACT_DTYPE)
    return pl.pallas_call(
        body, name=name, grid=(B, N_MAIN_PAIRS, nq),
        in_specs=[_q_spec(bq, nq, offs[0]), _seq_spec(S, offs[1]), _seq_spec(S, offs[2]), qs],
        out_specs=[qs, seq, seq], out_shape=[full, full, full],
        scratch_shapes=[pltpu.VMEM((S, PAIR), F32), pltpu.VMEM((S, PAIR), F32)],
        compiler_params=_params('parallel', 'parallel', 'arbitrary'),
    )(*qkv, do)


def _mem_probs(qv, mk):
    s = _dot(qv, mk, NT) * SCALE
    p = jnp.exp(s - jnp.max(s, axis=-1, keepdims=True))
    return p / jnp.sum(p, axis=-1, keepdims=True)


def _mem_fwd(q, q_off, mkv, B, S, name='mem_fwd'):
    M = mkv.shape[0] // B
    bq = _tile(S, 512)
    nq = S // bq

    def body(q_ref, mk_ref, mv_ref, o_ref):
        first = _lane0((bq, PAIR))
        mk, mv = mk_ref[...], mv_ref[...]
        out = [_dot(_mem_probs(qh, mk), mv, NN) for qh in _per_head(q_ref[...])]
        o_ref[...] = _pick(first, out[0], out[1]).astype(ACT_DTYPE)

    return pl.pallas_call(
        body, name=name, grid=(B, N_MEM_PAIRS, nq),
        in_specs=[_q_spec(bq, nq, q_off), _seq_spec(M, 0), _seq_spec(M, N_MEM_PAIRS)],
        out_specs=_q_spec(bq, nq, 0),
        out_shape=jax.ShapeDtypeStruct((B * S, MEM_WIDTH), ACT_DTYPE),
        compiler_params=_params('parallel', 'parallel', 'parallel'),
    )(q, mkv, mkv)


def _mem_bwd(q, q_off, mkv, B, S, do, do_off, name='mem_bwd'):
    M = mkv.shape[0] // B
    bq = _tile(S, 512)
    nq = S // bq

    def body(q_ref, mk_ref, mv_ref, do_ref, dq_ref, dmk_ref, dmv_ref):
        i = pl.program_id(2)

        @pl.when(i == 0)
        def _():
            dmk_ref[...] = jnp.zeros_like(dmk_ref)
            dmv_ref[...] = jnp.zeros_like(dmv_ref)

        qv = q_ref[...]
        dov = do_ref[...]
        mk, mv = mk_ref[...], mv_ref[...]
        first = _lane0((bq, PAIR))
        first_m = _lane0((M, PAIR))
        dqh, dkh, dvh = [], [], []
        for qh, doh in zip(_per_head(qv), _per_head(dov)):
            p = _mem_probs(qh, mk)
            dp = _dot(doh, mv, NT)
            ds = p * (dp - jnp.sum(p * dp, axis=-1, keepdims=True))
            dqh.append(_dot(ds, mk, NN))
            dkh.append(_dot(ds, qv, TN))
            dvh.append(_dot(p, dov, TN))
        dq_ref[...] = (SCALE * _pick(first, dqh[0], dqh[1])).astype(ACT_DTYPE)
        dmk_ref[...] += SCALE * _pick(first_m, dkh[0], dkh[1])
        dmv_ref[...] += _pick(first_m, dvh[0], dvh[1])

    mem_out = jax.ShapeDtypeStruct((B * M, MEM_WIDTH), F32)
    return pl.pallas_call(
        body, name=name, grid=(B, N_MEM_PAIRS, nq),
        in_specs=[_q_spec(bq, nq, q_off), _seq_spec(M, 0), _seq_spec(M, N_MEM_PAIRS), _q_spec(bq, nq, do_off)],
        out_specs=[_q_spec(bq, nq, 0), _seq_spec(M, 0), _seq_spec(M, 0)],
        out_shape=[jax.ShapeDtypeStruct((B * S, MEM_WIDTH), ACT_DTYPE), mem_out, mem_out],
        compiler_params=_params('parallel', 'parallel', 'arbitrary'),
    )(q, mkv, mkv, do)


def _shift_down(u, n):
    t = lax.broadcasted_iota(jnp.int32, u.shape, 0)
    return jnp.where(t >= n, pltpu.roll(u, n, 0), 0.0)


def _shift_up(u, n):
    S = u.shape[0]
    t = lax.broadcasted_iota(jnp.int32, u.shape, 0)
    return jnp.where(t < S - n, pltpu.roll(u, S - n, 0), 0.0)


def _conv(u, u1, u2, w, b):
    return b + w[0:1, :] * u2 + w[1:2, :] * u1 + w[2:3, :] * u


def _conv_specs(S, nf):
    ug = pl.BlockSpec((None, S, LANES), lambda b, j: (b, 0, j))
    uv = pl.BlockSpec((None, S, LANES), lambda b, j: (b, 0, j + nf))
    wg = pl.BlockSpec((3, LANES), lambda b, j: (0, j))
    wv = pl.BlockSpec((3, LANES), lambda b, j: (0, j + nf))
    bg = pl.BlockSpec((1, LANES), lambda b, j: (0, j))
    bv = pl.BlockSpec((1, LANES), lambda b, j: (0, j + nf))
    return ug, uv, wg, wv, bg, bv


def _conv_fwd(u, cw, cb, name='conv_fwd'):
    B, S, F2 = u.shape
    F = F2 // 2
    nf = F // LANES

    def body(ug_ref, uv_ref, wg_ref, wv_ref, bg_ref, bv_ref, y_ref):
        ug = ug_ref[...].astype(F32)
        uv = uv_ref[...].astype(F32)
        gate = _conv(ug, _shift_down(ug, 1), _shift_down(ug, 2), wg_ref[...], bg_ref[...])
        val = _conv(uv, _shift_down(uv, 1), _shift_down(uv, 2), wv_ref[...], bv_ref[...])
        y_ref[...] = (gate / (1.0 + jnp.exp(-gate)) * val).astype(ACT_DTYPE)

    specs = _conv_specs(S, nf)
    return pl.pallas_call(
        body, name=name, grid=(B, nf), in_specs=list(specs), out_specs=specs[0],
        out_shape=jax.ShapeDtypeStruct((B, S, F), ACT_DTYPE),
        compiler_params=_params('parallel', 'parallel'),
    )(u, u, cw, cw, cb, cb)


def _conv_bwd(u, cw, cb, dy, name='conv_bwd'):
    B, S, F2 = u.shape
    F = F2 // 2
    nf = F // LANES

    def body(ug_ref, uv_ref, wg_ref, wv_ref, bg_ref, bv_ref, dy_ref,
             dug_ref, duv_ref, dwg_ref, dwv_ref, dbg_ref, dbv_ref):
        b = pl.program_id(1)

        @pl.when(b == 0)
        def _():
            for r in (dwg_ref, dwv_ref, dbg_ref, dbv_ref):
                r[...] = jnp.zeros_like(r)

        ug = ug_ref[...].astype(F32)
        uv = uv_ref[...].astype(F32)
        ug1, ug2 = _shift_down(ug, 1), _shift_down(ug, 2)
        uv1, uv2 = _shift_down(uv, 1), _shift_down(uv, 2)
        wg, wv = wg_ref[...], wv_ref[...]
        gate = _conv(ug, ug1, ug2, wg, bg_ref[...])
        val = _conv(uv, uv1, uv2, wv, bv_ref[...])
        dyv = dy_ref[...].astype(F32)
        sg = 1.0 / (1.0 + jnp.exp(-gate))
        dval = dyv * (gate * sg)
        dgate = dyv * val * (sg * (1.0 + gate * (1.0 - sg)))

        def back(d, x, x1, x2, w, du_ref, dw_ref, db_ref):
            db_ref[...] += jnp.sum(d, axis=0, keepdims=True)
            dw_ref[...] += jnp.concatenate(
                [jnp.sum(d * x2, axis=0, keepdims=True), jnp.sum(d * x1, axis=0, keepdims=True),
                 jnp.sum(d * x, axis=0, keepdims=True)], axis=0)
            du = w[2:3, :] * d + w[1:2, :] * _shift_up(d, 1) + w[0:1, :] * _shift_up(d, 2)
            du_ref[...] = du.astype(ACT_DTYPE)

        back(dgate, ug, ug1, ug2, wg, dug_ref, dwg_ref, dbg_ref)
        back(dval, uv, uv1, uv2, wv, duv_ref, dwv_ref, dbv_ref)

    def swap(spec_fn):
        return lambda j, b: spec_fn(b, j)

    ug, uv, wg, wv, bg, bv = _conv_specs(S, nf)
    ins = [pl.BlockSpec(s.block_shape, swap(s.index_map)) for s in (ug, uv, wg, wv, bg, bv, ug)]
    outs = [ins[0], ins[0], ins[2], ins[2], ins[4], ins[4]]
    return pl.pallas_call(
        body, name=name, grid=(nf, B), in_specs=ins, out_specs=outs,
        out_shape=[jax.ShapeDtypeStruct((B, S, F), ACT_DTYPE), jax.ShapeDtypeStruct((B, S, F), ACT_DTYPE),
                   jax.ShapeDtypeStruct((3, F), F32), jax.ShapeDtypeStruct((3, F), F32),
                   jax.ShapeDtypeStruct((1, F), F32), jax.ShapeDtypeStruct((1, F), F32)],
        compiler_params=_params('parallel', 'arbitrary'),
    )(u, u, cw, cw, cb, cb, dy)


ADAM_BLOCK_BYTES = 512 * 1024


def _adamw(w, g, m, v, layer, earlier, name):
    L, r, c = w.shape
    tr = r
    if r * c * 4 > ADAM_BLOCK_BYTES and r % 8 == 0:
        tr = 8
        for t in range(8, r + 1, 8):
            if r % t == 0 and t * c * 4 <= ADAM_BLOCK_BYTES:
                tr = t

    def body(w_ref, g_ref, m_ref, v_ref, *rest):
        go_ref, d_ref, nm_ref, nv_ref = rest[-4:]
        gv = g_ref[...]
        nm = ADAM_B1 * m_ref[...] + (1.0 - ADAM_B1) * gv
        nv = ADAM_B2 * v_ref[...] + (1.0 - ADAM_B2) * (gv * gv)
        m_hat = nm / (1.0 - ADAM_B1 ** ADAM_STEP)
        v_hat = nv / (1.0 - ADAM_B2 ** ADAM_STEP)
        d_ref[...] = -ADAM_LR * (m_hat / (jnp.sqrt(v_hat) + ADAM_EPS) + ADAM_WD * w_ref[...])
        nm_ref[...] = nm
        nv_ref[...] = nv
        go_ref[...] = gv

    lay = pl.BlockSpec((None, tr, c), lambda i: (layer, i, 0))
    one = pl.BlockSpec((tr, c), lambda i: (i, 0))
    shp = jax.ShapeDtypeStruct((L, r, c), F32)
    in_specs = [lay, one, lay, lay]
    args = (w, g, m, v)
    aliases = {}
    if earlier is not None:
        in_specs += [ANY] * 4
        args += tuple(earlier)
        aliases = {4 + k: k for k in range(4)}
    return pl.pallas_call(
        body, name=name, grid=(r // tr,), in_specs=in_specs, out_specs=[lay] * 4, out_shape=[shp] * 4,
        input_output_aliases=aliases, compiler_params=_params('parallel'),
    )(*args)


def _my_place():
    return lax.axis_index('x'), lax.axis_index('y'), lax.axis_index('c')


def _other_chips(x, y):
    return [(1 - x, y), (x, 1 - y), (1 - x, 1 - y)]


def _remote(src, dst, send_sem, recv_sem, to):
    return pltpu.make_async_remote_copy(src_ref=src, dst_ref=dst, send_sem=send_sem, recv_sem=recv_sem,
                                        device_id=to, device_id_type=MESH)


def _hbm_call(body, n_in, out_shapes, scratch, name, aliases=None):
    return pl.pallas_call(body, name=name, in_specs=[ANY] * n_in, out_specs=[ANY] * len(out_shapes),
                          out_shape=out_shapes, scratch_shapes=scratch, input_output_aliases=aliases or {})


def _full_shape(shard_shape, kind):
    L, r, c = shard_shape
    return {'rows': (L, N_CHIPS * r, c), 'cols': (L, r, N_CHIPS * c), 'stack': (N_CHIPS * L, r, c)}[kind]


def _place_block(w, kind, out_dtype, chip_arr, name):
    L, r, c = w.shape
    tr = r if r % 16 else _tile(r, max(16, SUM_BLOCK_BYTES // (4 * c)), 16)
    nrt = r // tr

    def body(k_ref, w_ref, o_ref):
        o_ref[...] = w_ref[...].astype(out_dtype)

    out_map = {'rows': lambda l, i, k_ref: (l, k_ref[0] * nrt + i, 0),
               'cols': lambda l, i, k_ref: (l, i, k_ref[0]),
               'stack': lambda l, i, k_ref: (k_ref[0] * L + l, i, 0)}[kind]
    gs = pltpu.PrefetchScalarGridSpec(
        num_scalar_prefetch=1, grid=(L, nrt),
        in_specs=[pl.BlockSpec((None, tr, c), lambda l, i, k_ref: (l, i, 0))],
        out_specs=pl.BlockSpec((None, tr, c), out_map))
    return pl.pallas_call(
        body, name=name, grid_spec=gs, out_shape=jax.ShapeDtypeStruct(_full_shape(w.shape, kind), out_dtype),
        compiler_params=_params('parallel', 'parallel'),
    )(chip_arr, w)


class _Exchange:
    def __init__(self, inputs, out_shapes, aliases, scratch, start, finish):
        self.inputs, self.out_shapes, self.aliases, self.scratch = list(inputs), list(out_shapes), aliases, scratch
        self.start, self.finish = start, finish


def _run_exchange(ex, name):
    n_in, n_out = len(ex.inputs), len(ex.out_shapes)

    def body(*refs):
        parts = refs[:n_in], refs[n_in:n_in + n_out], refs[n_in + n_out:]
        ex.start(*parts)
        ex.finish(*parts)

    return _hbm_call(body, n_in, ex.out_shapes, ex.scratch, name, aliases=ex.aliases)(*ex.inputs)


def _call_beside(body, name, grid, in_specs, out_specs, out_shape, scratch, args, semantics, beside):
    if beside is None:
        outs = pl.pallas_call(body, name=name, grid=grid, in_specs=in_specs, out_specs=out_specs, out_shape=out_shape,
                              scratch_shapes=scratch, compiler_params=_params(*semantics))(*args)
        return outs, None
    n_in, n_out, n_scr = len(in_specs), len(out_specs), len(scratch)
    b_in, b_out = len(beside.inputs), len(beside.out_shapes)

    def carrier(*refs):
        cuts = [n_in, b_in, n_out, b_out, n_scr]
        parts, at = [], 0
        for size in cuts:
            parts.append(refs[at:at + size])
            at += size
        ins, ex_ins, outs, ex_outs, scr = parts
        ex_scr = refs[at:]
        ids = [pl.program_id(d) for d in range(len(grid))]
        first = functools.reduce(jnp.logical_and, [i == 0 for i in ids])
        last = functools.reduce(jnp.logical_and, [i == g - 1 for i, g in zip(ids, grid)])

        @pl.when(first)
        def _():
            beside.start(ex_ins, ex_outs, ex_scr)

        body(*ins, *outs, *scr)

        @pl.when(last)
        def _():
            beside.finish(ex_ins, ex_outs, ex_scr)

    res = pl.pallas_call(
        carrier, name=name, grid=grid, in_specs=list(in_specs) + [ANY] * b_in,
        out_specs=list(out_specs) + [ANY] * b_out, out_shape=list(out_shape) + beside.out_shapes,
        scratch_shapes=list(scratch) + beside.scratch,
        input_output_aliases={n_in + i: n_out + o for i, o in beside.aliases.items()},
        compiler_params=_params(*['arbitrary'] * len(grid)),
    )(*args, *beside.inputs)
    return res[:n_out], res[n_out:]


def _gather_exchange(fulls, shard_shapes, kinds, split):
    n = len(fulls)

    def plan(outs, send_sems, recv_sems):
        x, y, c = _my_place()
        chip = 2 * x + y
        sibling = (x, y, 1 - c)
        others = _other_chips(x, y)

        def window(a, k, half):
            L, r, cols = shard_shapes[a]
            first, count = (0, r) if half is None else (half * (r // 2), r // 2)
            if kinds[a] == 'rows':
                return outs[a].at[:, pl.ds(k * r + first, count), :]
            if kinds[a] == 'cols':
                return outs[a].at[:, pl.ds(first, count), pl.ds(pl.multiple_of(k * cols, LANES), cols)]
            return outs[a].at[pl.ds(k * L, L), pl.ds(first, count), :]

        sends, arrivals, forwards, forwarded = [], [], [], []
        for a in range(n):
            half = c if split[a] else None
            for j, (ox, oy) in enumerate(others):
                sems = (send_sems.at[6 * a + j], recv_sems.at[6 * a + j], (ox, oy, c))
                sends.append(_remote(window(a, chip, half), window(a, chip, half), *sems))
                got = window(a, 2 * ox + oy, half)
                arrivals.append(_remote(got, got, *sems))
                if split[a]:
                    sems = (send_sems.at[6 * a + 3 + j], recv_sems.at[6 * a + 3 + j], sibling)
                    forwards.append(_remote(got, got, *sems))
                    theirs = window(a, 2 * ox + oy, 1 - c)
                    forwarded.append(_remote(theirs, theirs, *sems))
                else:
                    forwards.append(None)
        return sends, arrivals, forwards, forwarded

    def start(ins, outs, scratch):
        sends, _, _, _ = plan(outs, *scratch)
        for cp in sends:
            cp.start()

    def finish(ins, outs, scratch):
        sends, arrivals, forwards, forwarded = plan(outs, *scratch)
        for arrived, fw in zip(arrivals, forwards):
            arrived.wait_recv()
            if fw is not None:
                fw.start()
        for cp in forwarded:
            cp.wait_recv()
        for cp in sends + [fw for fw in forwards if fw is not None]:
            cp.wait_send()

    scratch = [pltpu.SemaphoreType.DMA((6 * n,)), pltpu.SemaphoreType.DMA((6 * n,))]
    out_shapes = [jax.ShapeDtypeStruct(f.shape, f.dtype) for f in fulls]
    return _Exchange(fulls, out_shapes, {a: a for a in range(n)}, scratch, start, finish)


def _swap_cores(gs, name='swap_cores'):
    n = len(gs)
    out_shapes = [jax.ShapeDtypeStruct((g.shape[0], g.shape[1] // 2, g.shape[2]), g.dtype) for g in gs]

    def body(*refs):
        ins, outs = refs[:n], refs[n:2 * n]
        send_sems, recv_sems = refs[2 * n:]
        x, y, c = _my_place()
        cps = []
        for a in range(n):
            rh = gs[a].shape[1] // 2
            cp = _remote(ins[a].at[:, pl.ds((1 - c) * rh, rh), :], outs[a], send_sems.at[a], recv_sems.at[a],
                         (x, y, 1 - c))
            cp.start()
            cps.append(cp)
        for cp in cps:
            cp.wait()

    scratch = [pltpu.SemaphoreType.DMA((n,)), pltpu.SemaphoreType.DMA((n,))]
    return _hbm_call(body, n, out_shapes, scratch, name)(*gs)


SUM_BLOCK_BYTES = 2 * 1024 * 1024


def _sum_rows(rh, cols):
    return _tile(rh, max(16, SUM_BLOCK_BYTES // (4 * cols)), 16)


def _add_cores(g, other, c_arr, wire_dtype, name):
    n, r, cols = g.shape
    rh = r // 2
    tr = _sum_rows(rh, cols)
    nrt = rh // tr

    def body(c_ref, g_ref, o_ref, q_ref):
        q_ref[...] = (g_ref[...] + o_ref[...]).astype(wire_dtype)

    gs = pltpu.PrefetchScalarGridSpec(
        num_scalar_prefetch=1, grid=(n, nrt),
        in_specs=[pl.BlockSpec((None, tr, cols), lambda j, i, c_ref: (j, c_ref[0] * nrt + i, 0)),
                  pl.BlockSpec((None, tr, cols), lambda j, i, c_ref: (j, i, 0))],
        out_specs=pl.BlockSpec((None, tr, cols), lambda j, i, c_ref: (j, i, 0)))
    return pl.pallas_call(
        body, name=name, grid_spec=gs, out_shape=jax.ShapeDtypeStruct((n, rh, cols), wire_dtype),
        compiler_params=_params('parallel', 'parallel'),
    )(c_arr, g, other)


def _send_exchange(qs):
    n = len(qs)

    def plan(ins, outs, send_sems, recv_sems):
        x, y, c = _my_place()
        return [_remote(ins[a].at[2 * ox + oy], outs[a].at[j], send_sems.at[3 * a + j], recv_sems.at[3 * a + j],
                        (ox, oy, c))
                for a in range(n) for j, (ox, oy) in enumerate(_other_chips(x, y))]

    def start(ins, outs, scratch):
        for cp in plan(ins, outs, *scratch):
            cp.start()

    def finish(ins, outs, scratch):
        cps = plan(ins, outs, *scratch)
        for cp in cps:
            cp.wait_recv()
        for cp in cps:
            cp.wait_send()

    scratch = [pltpu.SemaphoreType.DMA((3 * n,)), pltpu.SemaphoreType.DMA((3 * n,))]
    out_shapes = [jax.ShapeDtypeStruct((3,) + q.shape[1:], q.dtype) for q in qs]
    return _Exchange(qs, out_shapes, {}, scratch, start, finish)


def _sum_chips(q, got, place_arr, name):
    n, rh, cols = q.shape
    tr = _sum_rows(rh, cols)

    def body(p_ref, q_ref, gx_ref, gy_ref, gxy_ref, o_ref):
        f = lambda r: r[...].astype(F32)
        o_ref[...] = (f(q_ref) + f(gxy_ref)) + (f(gx_ref) + f(gy_ref))

    def got_spec(j):
        return pl.BlockSpec((None, tr, cols), lambda i, p_ref: (j, i, 0))

    gs = pltpu.PrefetchScalarGridSpec(
        num_scalar_prefetch=1, grid=(rh // tr,),
        in_specs=[pl.BlockSpec((None, tr, cols), lambda i, p_ref: (p_ref[0], i, 0)),
                  got_spec(0), got_spec(1), got_spec(2)],
        out_specs=pl.BlockSpec((None, tr, cols), lambda i, p_ref: (p_ref[1], i, 0)))
    return pl.pallas_call(
        body, name=name, grid_spec=gs, out_shape=jax.ShapeDtypeStruct((2, rh, cols), F32),
        compiler_params=_params('parallel'),
    )(place_arr, q, got, got, got)


def _join_cores(rs, name='join_cores'):
    n = len(rs)
    out_shapes = [jax.ShapeDtypeStruct(r.shape, r.dtype) for r in rs]

    def body(*refs):
        outs = refs[n:2 * n]
        send_sems, recv_sems = refs[2 * n:]
        x, y, c = _my_place()
        cps = []
        for a in range(n):
            cp = _remote(outs[a].at[c], outs[a].at[c], send_sems.at[a], recv_sems.at[a], (x, y, 1 - c))
            cp.start()
            cps.append(cp)
        for cp in cps:
            cp.wait()

    scratch = [pltpu.SemaphoreType.DMA((n,)), pltpu.SemaphoreType.DMA((n,))]
    return _hbm_call(body, n, out_shapes, scratch, name, aliases={a: a for a in range(n)})(*rs)


def _gate_rows(t, B, S):
    return t.reshape(B, S, N_MAIN_HEADS).transpose(0, 2, 1).reshape(B * N_MAIN_HEADS, S)


def _gate_cols(t, B, S):
    return t.reshape(B, N_MAIN_HEADS, S).transpose(0, 2, 1).reshape(B * S, N_MAIN_HEADS)


def _mem_kv_fwd(mem2, g, w, tag):
    hm = _rms_fwd(mem2, g, name=f'rms_mem_{tag}')
    mkv = _matmul(hm, w, 'nn', ACT_DTYPE, name=f'mm_memkv_{tag}')
    return hm, mkv


def _mem_kv_bwd(mem2, g, w, hm, dmk, dmv, tag):
    dmkv = jnp.concatenate([dmk, dmv], axis=1)
    dw = _matmul(hm, dmkv, 'tn', F32, name=f'mm_memkv_dw_{tag}')
    dhm = _matmul(dmkv, w, 'nt', F32, name=f'mm_memkv_dx_{tag}')
    _, dg = _rms_bwd(mem2, g, dhm, None, name=f'rms_mem_bwd_{tag}')
    return dw, dg


def _ffn_fwd(x, g, w_up, cw, cb, w_down, B, S, tag):
    T = x.shape[0]
    h2 = _rms_fwd(x, g, name=f'rms_ffn_{tag}')
    u = _matmul(h2, w_up, 'nn', ACT_DTYPE, name=f'mm_up_{tag}')
    y = _conv_fwd(u.reshape(B, S, -1), cw, cb, name=f'conv_fwd_{tag}').reshape(T, -1)
    x2 = _matmul(y, w_down, 'nn', F32, res=x, name=f'mm_down_{tag}')
    return x2, (h2, u, y)


def _ffn_bwd(dx2, x, g, w_up, cw, cb, w_down, saved, B, S, tag):
    h2, u, y = saved
    T = x.shape[0]
    dy = _matmul(dx2, w_down, 'nt', ACT_DTYPE, name=f'mm_down_dx_{tag}')
    dw_down = _matmul(y, dx2, 'tn', F32, name=f'mm_down_dw_{tag}')
    dug, duv, dcwg, dcwv, dcbg, dcbv = _conv_bwd(u.reshape(B, S, -1), cw, cb, dy.reshape(B, S, -1),
                                                  name=f'conv_bwd_{tag}')
    du = jnp.concatenate([dug.reshape(T, -1), duv.reshape(T, -1)], axis=1)
    dh2 = _matmul(du, w_up, 'nt', F32, name=f'mm_up_dx_{tag}')
    dw_up = _matmul(h2, du, 'tn', F32, slots=N_CHIPS, name=f'mm_up_dw_{tag}')
    dx, dg = _rms_bwd(x, g, dh2, dx2, name=f'rms_ffn_bwd_{tag}')
    dcw = jnp.concatenate([dcwg, dcwv], axis=1)
    dcb = jnp.concatenate([dcbg, dcbv], axis=1)
    return dx, dg, dw_up, dcw, dcb, dw_down


def _step(x, mem, tgt, W, late_weights=None, reduce_early=None):
    B, S, D = x.shape
    T = B * S
    x0 = x.reshape(T, D)
    mem2 = mem.reshape(-1, D)
    tgt2 = tgt.reshape(T, D)
    row = lambda v: v.reshape(1, -1)
    q3 = 3 * MAIN_WIDTH

    w_in_a = W['w_in_a'][0]
    wa_main = jnp.concatenate([w_in_a[:, :q3], w_in_a[:, q3 + N_MAIN_HEADS:]], axis=1)
    wa_gate = jnp.pad(w_in_a[:, q3:q3 + N_MAIN_HEADS], ((0, 0), (0, LANES - N_MAIN_HEADS)))
    bcol = jnp.tile(W['b_f_a'][0], B).reshape(B * N_MAIN_HEADS, 1)
    nkb = S // min(FOX_BLOCK, S)

    h1a = _rms_fwd(x0, row(W['ln_mix_g'][0]), name='rms_mix_a')
    pa = _matmul(h1a, wa_main, 'nn', ACT_DTYPE, name='mm_in_a')
    flog = _matmul(h1a, wa_gate, 'nn', F32, name='mm_gate_a')
    qkv_a = (pa, pa, pa)
    offs_a = (0, N_MAIN_PAIRS, 2 * N_MAIN_PAIRS)
    qm_off_a = 3 * N_MAIN_PAIRS
    zt = _gate_rows(flog[:, :N_MAIN_HEADS], B, S)
    cum = _gate_fwd(zt, bcol)
    ccol = cum.reshape(B * N_MAIN_HEADS, S, 1)
    crow = cum.reshape(B * N_MAIN_HEADS, nkb, 1, S // nkb)
    (oa, lse), late = _fox_fwd(qkv_a, offs_a, B, S, ccol, crow, beside=late_weights[0] if late_weights else None)
    if late_weights:
        W = {**W, **late_weights[1](late)}
    w_in_b = W['w_in_b'][0]
    hma, mkva = _mem_kv_fwd(mem2, row(W['ln_mem_g'][0]), W['w_memkv'][0], 'a')
    oma = _mem_fwd(pa, qm_off_a, mkva, B, S, name='mem_fwd_a')
    ocat_a = jnp.concatenate([oa, oma], axis=1)
    x1 = _matmul(ocat_a, W['w_out'][0], 'nn', F32, res=x0, name='mm_out_a')
    x2, ffn_a = _ffn_fwd(x1, row(W['ln_ffn_g'][0]), W['w_up'][0], W['conv_w'][0], row(W['conv_b'][0]),
                         W['w_down'][0], B, S, 'a')
    hkv = _rms_fwd(x2, row(W['ln_kv_g']), name='rms_kv')
    kvs = _matmul(hkv, W['w_kv'], 'nn', ACT_DTYPE, name='mm_kv')
    h1b = _rms_fwd(x2, row(W['ln_mix_g'][1]), name='rms_mix_b')
    pb = _matmul(h1b, w_in_b, 'nn', ACT_DTYPE, name='mm_in_b')
    qkv_b = (pb, kvs, kvs)
    offs_b = (0, 0, N_MAIN_PAIRS)
    qm_off_b = N_MAIN_PAIRS
    ob = _sb_fwd(qkv_b, offs_b, B, S)
    hmb, mkvb = _mem_kv_fwd(mem2, row(W['ln_mem_g'][1]), W['w_memkv'][1], 'b')
    omb = _mem_fwd(pb, qm_off_b, mkvb, B, S, name='mem_fwd_b')
    ocat_b = jnp.concatenate([ob, omb], axis=1)
    x3 = _matmul(ocat_b, W['w_out'][1], 'nn', F32, res=x2, name='mm_out_b')
    x4, ffn_b = _ffn_fwd(x3, row(W['ln_ffn_g'][1]), W['w_up'][1], W['conv_w'][1], row(W['conv_b'][1]),
                         W['w_down'][1], B, S, 'b')
    loss, dx4, d_final_g = _final_loss(x4, row(W['final_g']), tgt2)

    dx3, dg_ffn_b, dw_up_b, dcw_b, dcb_b, dw_down_b = _ffn_bwd(
        dx4, x3, row(W['ln_ffn_g'][1]), W['w_up'][1], W['conv_w'][1], row(W['conv_b'][1]), W['w_down'][1],
        ffn_b, B, S, 'b')
    docat = _matmul(dx3, W['w_out'][1], 'nt', ACT_DTYPE, name='mm_out_dx_b')
    dw_out_b = _matmul(ocat_b, dx3, 'tn', F32, name='mm_out_dw_b')
    dqb, dkb, dvb = _sb_bwd(qkv_b, offs_b, B, S, docat)
    dqmb, dmkb, dmvb = _mem_bwd(pb, qm_off_b, mkvb, B, S, docat, N_MAIN_PAIRS, name='mem_bwd_b')
    dw_memkv_b, dg_mem_b = _mem_kv_bwd(mem2, row(W['ln_mem_g'][1]), W['w_memkv'][1], hmb, dmkb, dmvb, 'b')
    dpb = jnp.concatenate([dqb, dqmb], axis=1)
    dh1b = _matmul(dpb, w_in_b, 'nt', F32, name='mm_in_dx_b')
    dw_in_b = _matmul(h1b, dpb, 'tn', F32, name='mm_in_dw_b')
    dx2, dg_mix_b = _rms_bwd(x2, row(W['ln_mix_g'][1]), dh1b, dx3, name='rms_mix_bwd_b')
    dkvs = jnp.concatenate([dkb, dvb], axis=1)
    dhkv = _matmul(dkvs, W['w_kv'], 'nt', F32, name='mm_kv_dx')
    dw_kv = _matmul(hkv, dkvs, 'tn', F32, slots=N_CHIPS, name='mm_kv_dw')
    dx2, dg_kv = _rms_bwd(x2, row(W['ln_kv_g']), dhkv, dx2, name='rms_kv_bwd')

    dx1, dg_ffn_a, dw_up_a, dcw_a, dcb_a, dw_down_a = _ffn_bwd(
        dx2, x1, row(W['ln_ffn_g'][0]), W['w_up'][0], W['conv_w'][0], row(W['conv_b'][0]), W['w_down'][0],
        ffn_a, B, S, 'a')
    docat = _matmul(dx1, W['w_out'][0], 'nt', ACT_DTYPE, name='mm_out_dx_a')
    dw_out_a = _matmul(ocat_a, dx1, 'tn', F32, name='mm_out_dw_a')

    def by_rows(dw):
        return dw.reshape(N_CHIPS, dw.shape[0] // N_CHIPS, dw.shape[1])

    grads = {
        'w_in_b': [by_rows(dw_in_b)],
        'w_kv': [dw_kv],
        'w_out': [by_rows(dw_out_a), by_rows(dw_out_b)],
        'w_up': [dw_up_a, dw_up_b],
        'w_down': [by_rows(dw_down_a), by_rows(dw_down_b)],
    }
    early = [(n, layer, g) for n, gs in grads.items() for layer, g in enumerate(gs)]
    early.append(('w_memkv', 1, by_rows(dw_memkv_b)))
    beside = reduce_early(early) if reduce_early else None
    (dqa, dka, dva, dccol, dcrow), crossed = _fox_bwd(qkv_a, offs_a, B, S, ccol, crow, oa, lse, docat, beside=beside)
    dzt, dbrow = _gate_bwd(zt, bcol, dccol.reshape(B * N_MAIN_HEADS, S) + dcrow.reshape(B * N_MAIN_HEADS, S))
    dqma, dmka, dmva = _mem_bwd(pa, qm_off_a, mkva, B, S, docat, N_MAIN_PAIRS, name='mem_bwd_a')
    dw_memkv_a, dg_mem_a = _mem_kv_bwd(mem2, row(W['ln_mem_g'][0]), W['w_memkv'][0], hma, dmka, dmva, 'a')
    dpa = jnp.concatenate([dqa, dka, dva, dqma], axis=1)
    dflog = jnp.pad(_gate_cols(dzt, B, S), ((0, 0), (0, LANES - N_MAIN_HEADS)))
    dh1a = _matmul(dpa, wa_main, 'nt', F32, name='mm_in_dx_a')
    dh1a = _matmul(dflog, wa_gate, 'nt', F32, res=dh1a, name='mm_gate_dx_a')
    dwa_main = _matmul(h1a, dpa, 'tn', F32, name='mm_in_dw_a')
    dwa_gate = _matmul(h1a, dflog, 'tn', F32, name='mm_gate_dw_a')
    dx0, dg_mix_a = _rms_bwd(x0, row(W['ln_mix_g'][0]), dh1a, dx1, name='rms_mix_bwd_a')

    dw_in_a = jnp.concatenate([dwa_main[:, :q3], dwa_gate[:, :N_MAIN_HEADS], dwa_main[:, q3:]], axis=1)
    grads.update({
        'ln_mix_g': jnp.concatenate([dg_mix_a, dg_mix_b], axis=0),
        'w_in_a': dw_in_a[None],
        'b_f_a': dbrow.reshape(B, N_MAIN_HEADS).sum(axis=0)[None],
        'ln_kv_g': dg_kv[0],
        'ln_mem_g': jnp.concatenate([dg_mem_a, dg_mem_b], axis=0),
        'w_memkv': [by_rows(dw_memkv_a), early[-1][2]],
        'ln_ffn_g': jnp.concatenate([dg_ffn_a, dg_ffn_b], axis=0),
        'conv_w': jnp.stack([dcw_a, dcw_b]),
        'conv_b': jnp.concatenate([dcb_a, dcb_b], axis=0),
        'final_g': d_final_g[0],
    })
    return loss, dx0.reshape(B, S, D), grads, crossed


BLOCKED = ('w_in_b', 'w_kv', 'w_memkv', 'w_out', 'w_up', 'w_down')
MISC_ROWS = 32


def _misc_names():
    return [n for n in PARAM_NAMES if PARAM_SHARD_AXIS[n] is None] + ['conv_w']


def _reduce_begin(arrays, wire, tag):
    _, _, c = _my_place()
    c_arr = jnp.reshape(c, (1,)).astype(jnp.int32)
    others = _swap_cores(arrays, name=f'swap_cores_{tag}')
    return [_add_cores(g, o, c_arr, wire[i], name=f'add_cores_{tag}_{i}')
            for i, (g, o) in enumerate(zip(arrays, others))]


def _reduce_end(qs, crossed, tag):
    x, y, c = _my_place()
    place_arr = jnp.stack([2 * x + y, c]).astype(jnp.int32)
    sums = [_sum_chips(q, g, place_arr, name=f'sum_chips_{tag}_{i}') for i, (q, g) in enumerate(zip(qs, crossed))]
    return [j.reshape(-1, j.shape[-1]) for j in _join_cores(sums, name=f'join_cores_{tag}')]


def _pack_late(grads, shards):
    a_cols = shards['w_in_a'].shape[2]
    a_pad = -(-a_cols // LANES) * LANES
    dw_in_a = grads['w_in_a'][0]
    in_a = jnp.stack([jnp.pad(dw_in_a[:, k * a_cols:(k + 1) * a_cols], ((0, 0), (0, a_pad - a_cols)))
                      for k in range(N_CHIPS)])
    conv_cols = shards['conv_w'].shape[2]
    misc = []
    for k in range(N_CHIPS):
        parts = [grads[n].reshape(-1) for n in _misc_names()[:-1]]
        parts.append(grads['conv_w'][:, :, k * conv_cols:(k + 1) * conv_cols].reshape(-1))
        flat = jnp.concatenate(parts)
        assert flat.shape[0] <= MISC_ROWS * PACK_COLS
        misc.append(jnp.pad(flat, (0, MISC_ROWS * PACK_COLS - flat.shape[0])).reshape(MISC_ROWS, PACK_COLS))
    return in_a, jnp.stack(misc)


def _unpack_misc(rows, shards):
    flat = rows.reshape(-1)
    out, off = {}, 0
    for name in _misc_names():
        shape = shards[name].shape
        size = math.prod(shape)
        out[name] = flat[off:off + size].reshape(-1, shape[-1])
        off += size
    return out


def kernel(x, mem, ln_mix_g, w_in_a, b_f_a, w_in_b, ln_kv_g, w_kv, ln_mem_g, w_memkv, w_out, ln_ffn_g, w_up, conv_w, conv_b, w_down, final_g, loss_target, m_ln_mix_g, m_w_in_a, m_b_f_a, m_w_in_b, m_ln_kv_g, m_w_kv, m_ln_mem_g, m_w_memkv, m_w_out, m_ln_ffn_g, m_w_up, m_conv_w, m_conv_b, m_w_down, m_final_g, v_ln_mix_g, v_w_in_a, v_b_f_a, v_w_in_b, v_ln_kv_g, v_w_kv, v_ln_mem_g, v_w_memkv, v_w_out, v_ln_ffn_g, v_w_up, v_conv_w, v_conv_b, v_w_down, v_final_g):
    shards = dict(ln_mix_g=ln_mix_g, w_in_a=w_in_a, b_f_a=b_f_a, w_in_b=w_in_b, ln_kv_g=ln_kv_g, w_kv=w_kv,
                  ln_mem_g=ln_mem_g, w_memkv=w_memkv, w_out=w_out, ln_ffn_g=ln_ffn_g, w_up=w_up, conv_w=conv_w,
                  conv_b=conv_b, w_down=w_down, final_g=final_g)
    moments_m = dict(ln_mix_g=m_ln_mix_g, w_in_a=m_w_in_a, b_f_a=m_b_f_a, w_in_b=m_w_in_b, ln_kv_g=m_ln_kv_g,
                     w_kv=m_w_kv, ln_mem_g=m_ln_mem_g, w_memkv=m_w_memkv, w_out=m_w_out, ln_ffn_g=m_ln_ffn_g,
                     w_up=m_w_up, conv_w=m_conv_w, conv_b=m_conv_b, w_down=m_w_down, final_g=m_final_g)
    moments_v = dict(ln_mix_g=v_ln_mix_g, w_in_a=v_w_in_a, b_f_a=v_b_f_a, w_in_b=v_w_in_b, ln_kv_g=v_ln_kv_g,
                     w_kv=v_w_kv, ln_mem_g=v_ln_mem_g, w_memkv=v_w_memkv, w_out=v_w_out, ln_ffn_g=v_ln_ffn_g,
                     w_up=v_w_up, conv_w=v_conv_w, conv_b=v_conv_b, w_down=v_w_down, final_g=v_final_g)

    kinds = {'w_in_a': 'stack', 'w_in_b': 'rows', 'w_kv': 'cols', 'w_memkv': 'rows', 'w_out': 'rows', 'w_up': 'cols',
             'w_down': 'rows', 'conv_w': 'cols'}
    mx, my, _ = _my_place()
    chip_arr = jnp.reshape(2 * mx + my, (1,)).astype(jnp.int32)
    placed, shard_shapes = {}, {}
    for n, kind in kinds.items():
        w = shards[n].reshape((-1,) + shards[n].shape[-2:])
        shard_shapes[n] = w.shape
        placed[n] = _place_block(w, kind, F32 if n in F32_GATHERED else jnp.bfloat16, chip_arr, name=f'place_{n}')

    def gather(names):
        return _gather_exchange([placed[n] for n in names], [shard_shapes[n] for n in names],
                                [kinds[n] for n in names], [n not in F32_GATHERED for n in names])

    def as_weights(names, full):
        out = dict(zip(names, full))
        if 'w_in_a' in out:
            out['w_in_a'] = jnp.concatenate([out['w_in_a'][k] for k in range(N_CHIPS)], axis=1)[None]
        if 'w_kv' in out:
            out['w_kv'] = out['w_kv'][0]
        return out

    first = ['w_in_a', 'conv_w']
    late = [n for n in kinds if n not in first]
    W = {**shards, **as_weights(first, _run_exchange(gather(first), 'gather_first'))}

    early = {}

    def reduce_early(items):
        early['owners'] = [(n, layer) for n, layer, _ in items]
        early['qs'] = _reduce_begin([g for _, _, g in items], [jnp.bfloat16] * len(items), 'early')
        return _send_exchange(early['qs'])

    loss_part, grad_x, grads, crossed = _step(x, mem, loss_target, W, (gather(late), functools.partial(as_weights, late)),
                                              reduce_early)
    loss = lax.psum(loss_part[0, 0], ('x', 'y', 'c'))

    g_layers = {n: [None] * (len(grads[n]) if n in BLOCKED else 1) for n in PARAM_NAMES}
    for (n, layer), g in zip(early['owners'], _reduce_end(early['qs'], crossed, 'early')):
        g_layers[n][layer] = g
    in_a, misc = _pack_late(grads, shards)
    qs = _reduce_begin([in_a, misc, grads['w_memkv'][0]], [jnp.bfloat16, F32, jnp.bfloat16], 'late')
    in_a_sum, misc_sum, memkv_sum = _reduce_end(qs, _run_exchange(_send_exchange(qs), 'send_chips_late'), 'late')
    g_layers['w_in_a'][0] = in_a_sum[:, :shards['w_in_a'].shape[2]]
    g_layers['w_memkv'][0] = memkv_sum
    for n, g in _unpack_misc(misc_sum, shards).items():
        g_layers[n][0] = g

    results = {}
    for name in PARAM_NAMES:
        w = shards[name]
        layers = len(g_layers[name])
        as_layers = (layers, -1, w.shape[-1])
        w3, m3, v3 = (t.reshape(as_layers) for t in (w, moments_m[name], moments_v[name]))
        res = None
        for layer, g in enumerate(g_layers[name]):
            res = _adamw(w3, g, m3, v3, layer, res, name=f'adamw_{name}_{layer}')
        results[name] = [t.reshape(w.shape) for t in res]

    return (loss, grad_x, *[results[n][k] for k in range(4) for n in PARAM_NAMES])
```

```python
import functools
import math

import jax
import jax.numpy as jnp
from jax import lax
from jax.experimental import pallas as pl
from jax.experimental.pallas import tpu as pltpu

F32 = jnp.float32
MXU_DTYPE = jnp.bfloat16
ACT_DTYPE = jnp.bfloat16

HEAD_DIM = 64
N_MAIN_HEADS = 12
N_MEM_HEADS = 4
MAIN_WIDTH = N_MAIN_HEADS * HEAD_DIM
MEM_WIDTH = N_MEM_HEADS * HEAD_DIM
EPS = 1e-6
SCALE = HEAD_DIM ** -0.5
NEG_BIG = -1e30
LANES = 128
PACK_COLS = 1024
N_CHIPS = 4

ADAM_LR = 0.001
ADAM_B1 = 0.9
ADAM_B2 = 0.999
ADAM_EPS = 1e-08
ADAM_WD = 0.01
ADAM_STEP = 10

MESH = pl.DeviceIdType.MESH
ANY = pl.BlockSpec(memory_space=pl.ANY)

PARAM_SHARD_AXIS = {
    'ln_mix_g': None, 'w_in_a': 2, 'b_f_a': None, 'w_in_b': 1, 'ln_kv_g': None, 'w_kv': 1,
    'ln_mem_g': None, 'w_memkv': 1, 'w_out': 1, 'ln_ffn_g': None, 'w_up': 2, 'conv_w': 2,
    'conv_b': None, 'w_down': 1, 'final_g': None,
}
PARAM_NAMES = list(PARAM_SHARD_AXIS)
F32_GATHERED = ('conv_w',)


def _tile(n, pref, unit=LANES):
    if n <= pref:
        return n
    best = None
    for t in range(unit, pref + 1, unit):
        if n % t == 0:
            best = t
    assert best is not None, (n, pref)
    return best


MM_ACC_ELEMS = 768 * 1024
MM_K_TILE = 2048
MM_VMEM_MB = 48


def _out_tiles(M, N):
    def divisors(n, cap):
        if n <= LANES:
            return [n]
        return [t for t in range(LANES, min(n, cap) + 1, LANES) if n % t == 0]

    best = None
    for tm in divisors(M, 1536):
        for tn in divisors(N, 2048):
            if tm * tn <= MM_ACC_ELEMS and (best is None or (tm * tn, tn) > (best[0] * best[1], best[1])):
                best = (tm, tn)
    assert best is not None, (M, N)
    return best


def _params(*sem, vmem_mb=None):
    kw = {}
    if sem:
        kw['dimension_semantics'] = sem
    if vmem_mb is not None:
        kw['vmem_limit_bytes'] = vmem_mb * 1024 * 1024
    return pltpu.CompilerParams(**kw)


def _dot(a, b, dims):
    return lax.dot_general(a.astype(MXU_DTYPE), b.astype(MXU_DTYPE), (dims, ((), ())),
                           preferred_element_type=F32)


NN = ((1,), (0,))
NT = ((1,), (1,))
TN = ((0,), (0,))


def _matmul(a, b, mode, out_dtype, res=None, slots=1, name='mm'):
    if mode == 'nn':
        (M, K), (K2, N) = a.shape, b.shape
    elif mode == 'nt':
        (M, K), (N, K2) = a.shape, b.shape
    else:
        (K, M), (K2, N) = a.shape, b.shape
    assert K == K2 and N % slots == 0, (a.shape, b.shape, mode, slots)
    slot_cols = N // slots
    tm, tn = _out_tiles(M, slot_cols)
    per_slot = slot_cols // tn
    tk = _tile(K, MM_K_TILE)
    nk = K // tk
    dims = {'nn': NN, 'nt': NT, 'tn': TN}[mode]
    a_again = a.size * a.dtype.itemsize * (N // tn)
    b_again = b.size * b.dtype.itemsize * (M // tm)
    m_inner = nk == 1 and a_again < b_again

    def body(*refs):
        if res is None:
            (a_ref, b_ref, o_ref), r_ref = refs[:3], None
        else:
            a_ref, b_ref, r_ref, o_ref = refs[:4]

        def finish(out):
            if r_ref is not None:
                out = out + r_ref[...]
            o_ref[...] = out.astype(out_dtype)

        if nk == 1:
            finish(_dot(a_ref[...], b_ref[...], dims))
            return
        acc = refs[-1]
        k = pl.program_id(2)

        @pl.when(k == 0)
        def _():
            acc[...] = jnp.zeros_like(acc)

        acc[...] += _dot(a_ref[...], b_ref[...], dims)

        @pl.when(k == nk - 1)
        def _():
            finish(acc[...])

    def spec(shape, index):
        return pl.BlockSpec(shape, (lambda j, i, k: index(i, j, k)) if m_inner else index)

    a_spec = spec((tk, tm), lambda i, j, k: (k, i)) if mode == 'tn' else spec((tm, tk), lambda i, j, k: (i, k))
    b_spec = spec((tn, tk), lambda i, j, k: (j, k)) if mode == 'nt' else spec((tk, tn), lambda i, j, k: (k, j))
    if slots == 1:
        o_spec = spec((tm, tn), lambda i, j, k: (i, j))
        out_shape = jax.ShapeDtypeStruct((M, N), out_dtype)
    else:
        assert res is None
        o_spec = spec((None, tm, tn), lambda i, j, k: (j // per_slot, i, j % per_slot))
        out_shape = jax.ShapeDtypeStruct((slots, M, slot_cols), out_dtype)
    in_specs = [a_spec, b_spec] + ([o_spec] if res is not None else [])
    args = (a, b) + ((res,) if res is not None else ())
    return pl.pallas_call(
        body, name=name, grid=(N // tn, M // tm, nk) if m_inner else (M // tm, N // tn, nk),
        in_specs=in_specs, out_specs=o_spec,
        out_shape=out_shape,
        scratch_shapes=[] if nk == 1 else [pltpu.VMEM((tm, tn), F32)],
        compiler_params=_params('parallel', 'parallel', 'arbitrary', vmem_mb=MM_VMEM_MB),
    )(*args)


def _rms_fwd(x, g, name):
    T, D = x.shape
    tr = _tile(T, 512)

    def body(x_ref, g_ref, o_ref):
        xv = x_ref[...]
        r = lax.rsqrt(jnp.mean(xv * xv, axis=-1, keepdims=True) + EPS)
        o_ref[...] = (xv * r * g_ref[...]).astype(ACT_DTYPE)

    return pl.pallas_call(
        body, name=name, grid=(T // tr,),
        in_specs=[pl.BlockSpec((tr, D), lambda i: (i, 0)), pl.BlockSpec((1, D), lambda i: (0, 0))],
        out_specs=pl.BlockSpec((tr, D), lambda i: (i, 0)),
        out_shape=jax.ShapeDtypeStruct((T, D), ACT_DTYPE),
        compiler_params=_params('parallel'),
    )(x, g)


def _rms_bwd(x, g, dh, dres, name):
    T, D = x.shape
    tr = _tile(T, 512)
    want_dx = dres is not None

    def body(*refs):
        if want_dx:
            x_ref, g_ref, dh_ref, dres_ref, dx_ref, dg_ref = refs
        else:
            x_ref, g_ref, dh_ref, dg_ref = refs
        i = pl.program_id(0)

        @pl.when(i == 0)
        def _():
            dg_ref[...] = jnp.zeros_like(dg_ref)

        xv = x_ref[...]
        dhv = dh_ref[...].astype(F32)
        r = lax.rsqrt(jnp.mean(xv * xv, axis=-1, keepdims=True) + EPS)
        n = xv * r
        dg_ref[...] += jnp.sum(dhv * n, axis=0, keepdims=True)
        if want_dx:
            dn = dhv * g_ref[...]
            dx = r * (dn - n * jnp.mean(dn * n, axis=-1, keepdims=True))
            dx_ref[...] = dres_ref[...] + dx

    row = pl.BlockSpec((tr, D), lambda i: (i, 0))
    vec = pl.BlockSpec((1, D), lambda i: (0, 0))
    if want_dx:
        return pl.pallas_call(
            body, name=name, grid=(T // tr,),
            in_specs=[row, vec, row, row], out_specs=[row, vec],
            out_shape=[jax.ShapeDtypeStruct((T, D), F32), jax.ShapeDtypeStruct((1, D), F32)],
            compiler_params=_params('arbitrary'),
        )(x, g, dh, dres)
    dg = pl.pallas_call(
        body, name=name, grid=(T // tr,),
        in_specs=[row, vec, row], out_specs=vec,
        out_shape=jax.ShapeDtypeStruct((1, D), F32),
        compiler_params=_params('arbitrary'),
    )(x, g, dh)
    return None, dg


def _final_loss(x, g, tgt, name='final_loss'):
    T, D = x.shape
    tr = _tile(T, 512)

    def body(x_ref, g_ref, t_ref, loss_ref, dx_ref, dg_ref):
        i = pl.program_id(0)

        @pl.when(i == 0)
        def _():
            loss_ref[...] = jnp.zeros_like(loss_ref)
            dg_ref[...] = jnp.zeros_like(dg_ref)

        xv = x_ref[...]
        gv = g_ref[...]
        r = lax.rsqrt(jnp.mean(xv * xv, axis=-1, keepdims=True) + EPS)
        n = xv * r
        e = n * gv - t_ref[...]
        per_tok = jnp.mean(e * e, axis=-1, keepdims=True)
        loss_ref[...] += 0.5 * jnp.sum(per_tok, axis=0, keepdims=True)
        dy = e * (1.0 / D)
        dg_ref[...] += jnp.sum(dy * n, axis=0, keepdims=True)
        dn = dy * gv
        dx_ref[...] = r * (dn - n * jnp.mean(dn * n, axis=-1, keepdims=True))

    row = pl.BlockSpec((tr, D), lambda i: (i, 0))
    vec = pl.BlockSpec((1, D), lambda i: (0, 0))
    one = pl.BlockSpec((1, 1), lambda i: (0, 0))
    return pl.pallas_call(
        body, name=name, grid=(T // tr,),
        in_specs=[row, vec, row], out_specs=[one, row, vec],
        out_shape=[jax.ShapeDtypeStruct((1, 1), F32), jax.ShapeDtypeStruct((T, D), F32),
                   jax.ShapeDtypeStruct((1, D), F32)],
        compiler_params=_params('arbitrary'),
    )(x, g, tgt)


def _log_sigmoid(z):
    return jnp.minimum(z, 0.0) - jnp.log(1.0 + jnp.exp(-jnp.abs(z)))


def _tri(n, rel):
    j = lax.broadcasted_iota(jnp.int32, (n, n), 0)
    s = lax.broadcasted_iota(jnp.int32, (n, n), 1)
    return rel(j, s).astype(MXU_DTYPE)


def _split_dot(x, tri, terms):
    if MXU_DTYPE == F32:
        return jnp.dot(x, tri, preferred_element_type=F32)
    out = None
    rem = x
    for _ in range(terms):
        piece = rem.astype(MXU_DTYPE)
        part = jnp.dot(piece, tri, preferred_element_type=F32)
        out = part if out is None else out + part
        rem = rem - piece.astype(F32)
    return out


def _gate_fwd(zt, bcol, name='gate_fwd'):
    BH, S = zt.shape
    nb = S // LANES

    def body(z_ref, b_ref, c_ref):
        tri = _tri(LANES, lambda j, s: j <= s)
        carry = jnp.zeros((BH, 1), F32)
        for i in range(nb):
            sl = slice(i * LANES, (i + 1) * LANES)
            logf = _log_sigmoid(z_ref[:, sl] + b_ref[...])
            cs = _split_dot(logf, tri, 3) + carry
            c_ref[:, sl] = cs
            carry = cs[:, LANES - 1:LANES]

    return pl.pallas_call(body, name=name, out_shape=jax.ShapeDtypeStruct((BH, S), F32))(zt, bcol)


def _gate_bwd(zt, bcol, dc, name='gate_bwd'):
    BH, S = zt.shape
    nb = S // LANES

    def body(z_ref, b_ref, dc_ref, dz_ref, db_ref):
        tri = _tri(LANES, lambda j, s: j >= s)
        carry = jnp.zeros((BH, 1), F32)
        dsum = jnp.zeros((BH, 1), F32)
        for i in reversed(range(nb)):
            sl = slice(i * LANES, (i + 1) * LANES)
            rs = _split_dot(dc_ref[:, sl], tri, 3) + carry
            carry = rs[:, 0:1]
            z = z_ref[:, sl] + b_ref[...]
            dz = rs * (1.0 - 1.0 / (1.0 + jnp.exp(-z)))
            dz_ref[:, sl] = dz
            dsum = dsum + jnp.sum(dz, axis=-1, keepdims=True)
        db_ref[...] = dsum

    return pl.pallas_call(
        body, name=name,
        out_shape=[jax.ShapeDtypeStruct((BH, S), F32), jax.ShapeDtypeStruct((BH, 1), F32)],
    )(zt, bcol, dc)


FOX_BLOCK = 256


PAIR = 2 * HEAD_DIM
N_MAIN_PAIRS = N_MAIN_HEADS // 2
N_MEM_PAIRS = N_MEM_HEADS // 2


def _lane0(shape):
    return lax.broadcasted_iota(jnp.int32, shape, len(shape) - 1) < HEAD_DIM


def _per_head(x):
    first = _lane0(x.shape)
    zero = jnp.zeros_like(x)
    return jnp.where(first, x, zero), jnp.where(first, zero, x)


def _pick(first, a, b):
    return jnp.where(first, a, b)


GROUP = 2
MAIN_STEPS = N_MAIN_PAIRS // GROUP


def _lanes(p):
    return slice(p * PAIR, (p + 1) * PAIR)


def _q_spec(bq, nq, off, group=1):
    assert off % group == 0
    return pl.BlockSpec((bq, group * PAIR), lambda b, j, i: (b * nq + i, off // group + j))


def _seq_spec(S, off, group=1):
    assert off % group == 0
    return pl.BlockSpec((S, group * PAIR), lambda b, j, i: (b, off // group + j))


def _gate_specs(bq, nk):
    col = pl.BlockSpec((2 * GROUP, bq, 1), lambda b, j, i: (b * MAIN_STEPS + j, i, 0))
    rowv = pl.BlockSpec((2 * GROUP, nk, 1, bq), lambda b, j, i: (b * MAIN_STEPS + j, 0, 0, 0))
    return col, rowv


def _diagonal_mask(bq, strict):
    row = lax.broadcasted_iota(jnp.int32, (bq, bq), 0)
    col = lax.broadcasted_iota(jnp.int32, (bq, bq), 1)
    return (col < row) if strict else (col <= row)


def _scaled(q):
    assert math.log2(SCALE).is_integer()
    return q * jnp.asarray(SCALE, q.dtype)


def _fox_fwd(qkv, offs, B, S, ccol, crow, beside=None, name='fox_fwd'):
    bq = min(FOX_BLOCK, S)
    nq = S // bq

    def body(q_ref, k_ref, v_ref, cc_ref, cr_ref, o_ref, lse_ref):
        i = pl.program_id(2)
        qv = _scaled(q_ref[...])
        qh = [_per_head(qv[:, _lanes(p)]) for p in range(GROUP)]
        first = _lane0((bq, PAIR))

        def step(kb, carry, mask=None):
            m, l, acc = carry
            sl = pl.ds(pl.multiple_of(kb * bq, bq), bq)
            m_new, l_new, acc_new = [], [], []
            for p in range(GROUP):
                ks, vs = k_ref[sl, _lanes(p)], v_ref[sl, _lanes(p)]
                alpha, pv = [], []
                for h in range(2):
                    n = 2 * p + h
                    s = _dot(qh[p][h], ks, NT) + cc_ref[n] - cr_ref[n, kb]
                    if mask is not None:
                        s = jnp.where(mask, s, NEG_BIG)
                    mh = jnp.maximum(m[n], jnp.max(s, axis=-1, keepdims=True))
                    pr = jnp.exp(s - mh)
                    ah = jnp.exp(m[n] - mh)
                    m_new.append(mh)
                    alpha.append(ah)
                    l_new.append(ah * l[n] + jnp.sum(pr, axis=-1, keepdims=True))
                    pv.append(_dot(pr, vs, NN))
                acc_new.append(_pick(first, alpha[0], alpha[1]) * acc[p] + _pick(first, pv[0], pv[1]))
            return tuple(m_new), tuple(l_new), tuple(acc_new)

        negs = tuple(jnp.full((bq, 1), NEG_BIG, F32) for _ in range(2 * GROUP))
        zeros = tuple(jnp.zeros((bq, 1), F32) for _ in range(2 * GROUP))
        acc0 = tuple(jnp.zeros((bq, PAIR), F32) for _ in range(GROUP))
        m, l, acc = step(i, lax.fori_loop(0, i, step, (negs, zeros, acc0)), _diagonal_mask(bq, False))
        for p in range(GROUP):
            o_ref[:, _lanes(p)] = (acc[p] / _pick(first, l[2 * p], l[2 * p + 1])).astype(ACT_DTYPE)
        for n in range(2 * GROUP):
            lse_ref[n] = m[n] + jnp.log(l[n])

    col, rowv = _gate_specs(bq, nq)
    return _call_beside(
        body, name, (B, MAIN_STEPS, nq),
        [_q_spec(bq, nq, offs[0], GROUP), _seq_spec(S, offs[1], GROUP), _seq_spec(S, offs[2], GROUP), col, rowv],
        [_q_spec(bq, nq, 0, GROUP), col],
        [jax.ShapeDtypeStruct((B * S, MAIN_WIDTH), ACT_DTYPE), jax.ShapeDtypeStruct((B * N_MAIN_HEADS, S, 1), F32)],
        [], (*qkv, ccol, crow), ('parallel', 'parallel', 'arbitrary'), beside)


def _fox_bwd(qkv, offs, B, S, ccol, crow, o, lse, do, beside=None, name='fox_bwd'):
    bq = min(FOX_BLOCK, S)
    nq = S // bq

    def body(q_ref, k_ref, v_ref, cc_ref, cr_ref, o_ref, lse_ref, do_ref,
             dq_ref, dk_ref, dv_ref, dcc_ref, dcr_ref, dk_acc, dv_acc):
        i = pl.program_id(2)

        @pl.when(i == 0)
        def _():
            dk_acc[...] = jnp.zeros_like(dk_acc)
            dv_acc[...] = jnp.zeros_like(dv_acc)
            dcr_ref[...] = jnp.zeros_like(dcr_ref)

        qv = _scaled(q_ref[...])
        dov = do_ref[...]
        qp = [qv[:, _lanes(p)] for p in range(GROUP)]
        dop = [dov[:, _lanes(p)] for p in range(GROUP)]
        qh = [_per_head(t) for t in qp]
        doh = [_per_head(t) for t in dop]
        first = _lane0((bq, PAIR))
        prod = dov.astype(F32) * o_ref[...].astype(F32)
        dsum = [jnp.sum(t, axis=-1, keepdims=True) for p in range(GROUP) for t in _per_head(prod[:, _lanes(p)])]

        def step(kb, carry, mask=None):
            dq, dcc = carry
            sl = pl.ds(pl.multiple_of(kb * bq, bq), bq)
            dq_new, dcc_new = [], []
            for p in range(GROUP):
                ks, vs = k_ref[sl, _lanes(p)], v_ref[sl, _lanes(p)]
                dqh, dkh, dvh = [], [], []
                for h in range(2):
                    n = 2 * p + h
                    s = _dot(qh[p][h], ks, NT) + cc_ref[n] - cr_ref[n, kb]
                    pr = jnp.exp(s - lse_ref[n])
                    if mask is not None:
                        pr = jnp.where(mask, pr, 0.0)
                    ds = pr * (_dot(doh[p][h], vs, NT) - dsum[n])
                    dqh.append(_dot(ds, ks, NN))
                    dkh.append(_dot(ds, qp[p], TN))
                    dvh.append(_dot(pr, dop[p], TN))
                    dcr_ref[n, kb] -= jnp.sum(ds, axis=0, keepdims=True)
                    dcc_new.append(dcc[n] + jnp.sum(ds, axis=-1, keepdims=True))
                dk_acc[sl, _lanes(p)] += _pick(first, dkh[0], dkh[1])
                dv_acc[sl, _lanes(p)] += _pick(first, dvh[0], dvh[1])
                dq_new.append(dq[p] + _pick(first, dqh[0], dqh[1]))
            return tuple(dq_new), tuple(dcc_new)

        zeros = tuple(jnp.zeros((bq, 1), F32) for _ in range(2 * GROUP))
        dq0 = tuple(jnp.zeros((bq, PAIR), F32) for _ in range(GROUP))
        dq, dcc = step(i, lax.fori_loop(0, i, step, (dq0, zeros)), _diagonal_mask(bq, False))
        for p in range(GROUP):
            dq_ref[:, _lanes(p)] = (dq[p] * SCALE).astype(ACT_DTYPE)
        for n in range(2 * GROUP):
            dcc_ref[n] = dcc[n]

        @pl.when(i == nq - 1)
        def _():
            dk_ref[...] = dk_acc[...].astype(ACT_DTYPE)
            dv_ref[...] = dv_acc[...].astype(ACT_DTYPE)

    col, rowv = _gate_specs(bq, nq)
    qs, seq = _q_spec(bq, nq, 0, GROUP), _seq_spec(S, 0, GROUP)
    full = jax.ShapeDtypeStruct((B * S, MAIN_WIDTH), ACT_DTYPE)
    wide = pltpu.VMEM((S, GROUP * PAIR), F32)
    return _call_beside(
        body, name, (B, MAIN_STEPS, nq),
        [_q_spec(bq, nq, offs[0], GROUP), _seq_spec(S, offs[1], GROUP), _seq_spec(S, offs[2], GROUP), col, rowv,
         qs, col, qs],
        [qs, seq, seq, col, rowv],
        [full, full, full, jax.ShapeDtypeStruct(ccol.shape, F32), jax.ShapeDtypeStruct(crow.shape, F32)],
        [wide, wide],
        (*qkv, ccol, crow, o, lse, do), ('parallel', 'parallel', 'arbitrary'), beside)


SB_BLOCK = 256


SB_SUM_TERMS = 2


def _sb_block(q_scaled, ks, mask):
    z = _dot(q_scaled, ks, NT)
    a = _log_sigmoid(z)
    l = a - z
    return a, (l if mask is None else jnp.where(mask, l, 0.0))


def _sb_fwd(qkv, offs, B, S, name='sb_fwd'):
    bq = min(SB_BLOCK, S)
    nq = S // bq

    def body(q_ref, k_ref, v_ref, o_ref):
        i = pl.program_id(2)
        qv = _scaled(q_ref[...])
        qh = [_per_head(qv[:, _lanes(p)]) for p in range(GROUP)]
        first = _lane0((bq, PAIR))
        tri = _tri(bq, lambda j, s: j > s)

        def step(kb, carry, mask=None):
            acc, right = carry
            sl = pl.ds(pl.multiple_of(kb * bq, bq), bq)
            acc_new, right_new = [], []
            for p in range(GROUP):
                ks, vs = k_ref[sl, _lanes(p)], v_ref[sl, _lanes(p)]
                pv = []
                for h in range(2):
                    n = 2 * p + h
                    a, l = _sb_block(qh[p][h], ks, mask)
                    w = jnp.exp(a + _split_dot(l, tri, SB_SUM_TERMS) + right[n])
                    if mask is not None:
                        w = jnp.where(mask, w, 0.0)
                    pv.append(_dot(w, vs, NN))
                    right_new.append(right[n] + jnp.sum(l, axis=-1, keepdims=True))
                acc_new.append(acc[p] + _pick(first, pv[0], pv[1]))
            return tuple(acc_new), tuple(right_new)

        zeros = tuple(jnp.zeros((bq, 1), F32) for _ in range(2 * GROUP))
        acc0 = tuple(jnp.zeros((bq, PAIR), F32) for _ in range(GROUP))
        carry = step(i, (acc0, zeros), _diagonal_mask(bq, True))
        acc, _ = lax.fori_loop(0, i, lambda n, c: step(i - 1 - n, c), carry)
        for p in range(GROUP):
            o_ref[:, _lanes(p)] = acc[p].astype(ACT_DTYPE)

    return pl.pallas_call(
        body, name=name, grid=(B, MAIN_STEPS, nq),
        in_specs=[_q_spec(bq, nq, offs[0], GROUP), _seq_spec(S, offs[1], GROUP), _seq_spec(S, offs[2], GROUP)],
        out_specs=_q_spec(bq, nq, 0, GROUP),
        out_shape=jax.ShapeDtypeStruct((B * S, MAIN_WIDTH), ACT_DTYPE),
        compiler_params=_params('parallel', 'parallel', 'arbitrary'),
    )(*qkv)


def _sb_bwd(qkv, offs, B, S, do, name='sb_bwd'):
    bq = min(SB_BLOCK, S)
    nq = S // bq

    def body(q_ref, k_ref, v_ref, do_ref, dq_ref, dk_ref, dv_ref, dk_acc, dv_acc):
        i = pl.program_id(2)

        @pl.when(i == 0)
        def _():
            dk_acc[...] = jnp.zeros_like(dk_acc)
            dv_acc[...] = jnp.zeros_like(dv_acc)

        qv = _scaled(q_ref[...])
        dov = do_ref[...]
        qp = [qv[:, _lanes(p)] for p in range(GROUP)]
        dop = [dov[:, _lanes(p)] for p in range(GROUP)]
        qh = [_per_head(t) for t in qp]
        doh = [_per_head(t) for t in dop]
        heads = [(p, h) for p in range(GROUP) for h in range(2)]
        first = _lane0((bq, PAIR))
        diagonal = _diagonal_mask(bq, True)
        tri_incl = _tri(bq, lambda j, s: j <= s)
        tri_excl = _tri(bq, lambda j, s: j < s)

        def total(kb, tot, mask=None):
            sl = pl.ds(pl.multiple_of(kb * bq, bq), bq)
            return tuple(t + jnp.sum(_sb_block(qh[p][h], k_ref[sl, _lanes(p)], mask)[1], axis=-1, keepdims=True)
                         for t, (p, h) in zip(tot, heads))

        zeros = tuple(jnp.zeros((bq, 1), F32) for _ in heads)
        tot = total(i, lax.fori_loop(0, i, total, zeros), diagonal)

        def step(kb, carry, mask=None):
            dq, rest_l, left_g = carry
            sl = pl.ds(pl.multiple_of(kb * bq, bq), bq)
            new_dq, new_l, new_g = [], [], []
            for p in range(GROUP):
                ks, vs = k_ref[sl, _lanes(p)], v_ref[sl, _lanes(p)]
                dqh, dkh, dvh = [], [], []
                for h in range(2):
                    n = 2 * p + h
                    a, l = _sb_block(qh[p][h], ks, mask)
                    w = jnp.exp(a - _split_dot(l, tri_incl, SB_SUM_TERMS) + rest_l[n])
                    if mask is not None:
                        w = jnp.where(mask, w, 0.0)
                    g = w * _dot(doh[p][h], vs, NT)
                    beta = jnp.exp(a)
                    dz = g - beta * (g + _split_dot(g, tri_excl, 1) + left_g[n])
                    if mask is not None:
                        dz = jnp.where(mask, dz, 0.0)
                    dqh.append(_dot(dz, ks, NN))
                    dkh.append(_dot(dz, qp[p], TN))
                    dvh.append(_dot(w, dop[p], TN))
                    new_l.append(rest_l[n] - jnp.sum(l, axis=-1, keepdims=True))
                    new_g.append(left_g[n] + jnp.sum(g, axis=-1, keepdims=True))
                dk_acc[sl, _lanes(p)] += _pick(first, dkh[0], dkh[1])
                dv_acc[sl, _lanes(p)] += _pick(first, dvh[0], dvh[1])
                new_dq.append(dq[p] + _pick(first, dqh[0], dqh[1]))
            return tuple(new_dq), tuple(new_l), tuple(new_g)

        dq0 = tuple(jnp.zeros((bq, PAIR), F32) for _ in range(GROUP))
        dq, _, _ = step(i, lax.fori_loop(0, i, step, (dq0, tot, zeros)), diagonal)
        for p in range(GROUP):
            dq_ref[:, _lanes(p)] = (dq[p] * SCALE).astype(ACT_DTYPE)

        @pl.when(i == nq - 1)
        def _():
            dk_ref[...] = dk_acc[...].astype(ACT_DTYPE)
            dv_ref[...] = dv_acc[...].astype(ACT_DTYPE)

    qs, seq = _q_spec(bq, nq, 0, GROUP), _seq_spec(S, 0, GROUP)
    full = jax.ShapeDtypeStruct((B * S, MAIN_WIDTH), ACT_DTYPE)
    wide = pltpu.VMEM((S, GROUP * PAIR), F32)
    return pl.pallas_call(
        body, name=name, grid=(B, MAIN_STEPS, nq),
        in_specs=[_q_spec(bq, nq, offs[0], GROUP), _seq_spec(S, offs[1], GROUP), _seq_spec(S, offs[2], GROUP), qs],
        out_specs=[qs, seq, seq], out_shape=[full, full, full],
        scratch_shapes=[wide, wide],
        compiler_params=_params('parallel', 'parallel', 'arbitrary'),
    )(*qkv, do)


def _mem_probs(qv, mk):
    s = _dot(qv, mk, NT) * SCALE
    p = jnp.exp(s - jnp.max(s, axis=-1, keepdims=True))
    return p / jnp.sum(p, axis=-1, keepdims=True)


def _mem_fwd(q, q_off, mkv, B, S, name='mem_fwd'):
    M = mkv.shape[0] // B
    bq = _tile(S, 512)
    nq = S // bq

    def body(q_ref, mk_ref, mv_ref, o_ref):
        first = _lane0((bq, PAIR))
        mk, mv = mk_ref[...], mv_ref[...]
        out = [_dot(_mem_probs(qh, mk), mv, NN) for qh in _per_head(q_ref[...])]
        o_ref[...] = _pick(first, out[0], out[1]).astype(ACT_DTYPE)

    return pl.pallas_call(
        body, name=name, grid=(B, N_MEM_PAIRS, nq),
        in_specs=[_q_spec(bq, nq, q_off), _seq_spec(M, 0), _seq_spec(M, N_MEM_PAIRS)],
        out_specs=_q_spec(bq, nq, 0),
        out_shape=jax.ShapeDtypeStruct((B * S, MEM_WIDTH), ACT_DTYPE),
        compiler_params=_params('parallel', 'parallel', 'parallel'),
    )(q, mkv, mkv)


def _mem_bwd(q, q_off, mkv, B, S, do, do_off, name='mem_bwd'):
    M = mkv.shape[0] // B
    bq = _tile(S, 512)
    nq = S // bq

    def body(q_ref, mk_ref, mv_ref, do_ref, dq_ref, dmk_ref, dmv_ref):
        i = pl.program_id(2)

        @pl.when(i == 0)
        def _():
            dmk_ref[...] = jnp.zeros_like(dmk_ref)
            dmv_ref[...] = jnp.zeros_like(dmv_ref)

        qv = q_ref[...]
        dov = do_ref[...]
        mk, mv = mk_ref[...], mv_ref[...]
        first = _lane0((bq, PAIR))
        first_m = _lane0((M, PAIR))
        dqh, dkh, dvh = [], [], []
        for qh, doh in zip(_per_head(qv), _per_head(dov)):
            p = _mem_probs(qh, mk)
            dp = _dot(doh, mv, NT)
            ds = p * (dp - jnp.sum(p * dp, axis=-1, keepdims=True))
            dqh.append(_dot(ds, mk, NN))
            dkh.append(_dot(ds, qv, TN))
            dvh.append(_dot(p, dov, TN))
        dq_ref[...] = (SCALE * _pick(first, dqh[0], dqh[1])).astype(ACT_DTYPE)
        dmk_ref[...] += SCALE * _pick(first_m, dkh[0], dkh[1])
        dmv_ref[...] += _pick(first_m, dvh[0], dvh[1])

    mem_out = jax.ShapeDtypeStruct((B * M, MEM_WIDTH), F32)
    return pl.pallas_call(
        body, name=name, grid=(B, N_MEM_PAIRS, nq),
        in_specs=[_q_spec(bq, nq, q_off), _seq_spec(M, 0), _seq_spec(M, N_MEM_PAIRS), _q_spec(bq, nq, do_off)],
        out_specs=[_q_spec(bq, nq, 0), _seq_spec(M, 0), _seq_spec(M, 0)],
        out_shape=[jax.ShapeDtypeStruct((B * S, MEM_WIDTH), ACT_DTYPE), mem_out, mem_out],
        compiler_params=_params('parallel', 'parallel', 'arbitrary'),
    )(q, mkv, mkv, do)


def _shift_down(u, n):
    t = lax.broadcasted_iota(jnp.int32, u.shape, 0)
    return jnp.where(t >= n, pltpu.roll(u, n, 0), 0.0)


def _shift_up(u, n):
    S = u.shape[0]
    t = lax.broadcasted_iota(jnp.int32, u.shape, 0)
    return jnp.where(t < S - n, pltpu.roll(u, S - n, 0), 0.0)


def _conv(u, u1, u2, w, b):
    return b + w[0:1, :] * u2 + w[1:2, :] * u1 + w[2:3, :] * u


def _conv_specs(S, nf):
    ug = pl.BlockSpec((None, S, LANES), lambda b, j: (b, 0, j))
    uv = pl.BlockSpec((None, S, LANES), lambda b, j: (b, 0, j + nf))
    wg = pl.BlockSpec((3, LANES), lambda b, j: (0, j))
    wv = pl.BlockSpec((3, LANES), lambda b, j: (0, j + nf))
    bg = pl.BlockSpec((1, LANES), lambda b, j: (0, j))
    bv = pl.BlockSpec((1, LANES), lambda b, j: (0, j + nf))
    return ug, uv, wg, wv, bg, bv


def _conv_fwd(u, cw, cb, name='conv_fwd'):
    B, S, F2 = u.shape
    F = F2 // 2
    nf = F // LANES

    def body(ug_ref, uv_ref, wg_ref, wv_ref, bg_ref, bv_ref, y_ref):
        ug = ug_ref[...].astype(F32)
        uv = uv_ref[...].astype(F32)
        gate = _conv(ug, _shift_down(ug, 1), _shift_down(ug, 2), wg_ref[...], bg_ref[...])
        val = _conv(uv, _shift_down(uv, 1), _shift_down(uv, 2), wv_ref[...], bv_ref[...])
        y_ref[...] = (gate / (1.0 + jnp.exp(-gate)) * val).astype(ACT_DTYPE)

    specs = _conv_specs(S, nf)
    return pl.pallas_call(
        body, name=name, grid=(B, nf), in_specs=list(specs), out_specs=specs[0],
        out_shape=jax.ShapeDtypeStruct((B, S, F), ACT_DTYPE),
        compiler_params=_params('parallel', 'parallel'),
    )(u, u, cw, cw, cb, cb)


def _conv_bwd(u, cw, cb, dy, name='conv_bwd'):
    B, S, F2 = u.shape
    F = F2 // 2
    nf = F // LANES

    def body(ug_ref, uv_ref, wg_ref, wv_ref, bg_ref, bv_ref, dy_ref,
             dug_ref, duv_ref, dwg_ref, dwv_ref, dbg_ref, dbv_ref):
        b = pl.program_id(1)

        @pl.when(b == 0)
        def _():
            for r in (dwg_ref, dwv_ref, dbg_ref, dbv_ref):
                r[...] = jnp.zeros_like(r)

        ug = ug_ref[...].astype(F32)
        uv = uv_ref[...].astype(F32)
        ug1, ug2 = _shift_down(ug, 1), _shift_down(ug, 2)
        uv1, uv2 = _shift_down(uv, 1), _shift_down(uv, 2)
        wg, wv = wg_ref[...], wv_ref[...]
        gate = _conv(ug, ug1, ug2, wg, bg_ref[...])
        val = _conv(uv, uv1, uv2, wv, bv_ref[...])
        dyv = dy_ref[...].astype(F32)
        sg = 1.0 / (1.0 + jnp.exp(-gate))
        dval = dyv * (gate * sg)
        dgate = dyv * val * (sg * (1.0 + gate * (1.0 - sg)))

        def back(d, x, x1, x2, w, du_ref, dw_ref, db_ref):
            db_ref[...] += jnp.sum(d, axis=0, keepdims=True)
            dw_ref[...] += jnp.concatenate(
                [jnp.sum(d * x2, axis=0, keepdims=True), jnp.sum(d * x1, axis=0, keepdims=True),
                 jnp.sum(d * x, axis=0, keepdims=True)], axis=0)
            du = w[2:3, :] * d + w[1:2, :] * _shift_up(d, 1) + w[0:1, :] * _shift_up(d, 2)
            du_ref[...] = du.astype(ACT_DTYPE)

        back(dgate, ug, ug1, ug2, wg, dug_ref, dwg_ref, dbg_ref)
        back(dval, uv, uv1, uv2, wv, duv_ref, dwv_ref, dbv_ref)

    def swap(spec_fn):
        return lambda j, b: spec_fn(b, j)

    ug, uv, wg, wv, bg, bv = _conv_specs(S, nf)
    ins = [pl.BlockSpec(s.block_shape, swap(s.index_map)) for s in (ug, uv, wg, wv, bg, bv, ug)]
    outs = [ins[0], ins[0], ins[2], ins[2], ins[4], ins[4]]
    return pl.pallas_call(
        body, name=name, grid=(nf, B), in_specs=ins, out_specs=outs,
        out_shape=[jax.ShapeDtypeStruct((B, S, F), ACT_DTYPE), jax.ShapeDtypeStruct((B, S, F), ACT_DTYPE),
                   jax.ShapeDtypeStruct((3, F), F32), jax.ShapeDtypeStruct((3, F), F32),
                   jax.ShapeDtypeStruct((1, F), F32), jax.ShapeDtypeStruct((1, F), F32)],
        compiler_params=_params('parallel', 'arbitrary'),
    )(u, u, cw, cw, cb, cb, dy)


ADAM_BLOCK_BYTES = 512 * 1024


def _adamw(w, g, m, v, layer, earlier, name):
    L, r, c = w.shape
    tr = r
    if r * c * 4 > ADAM_BLOCK_BYTES and r % 8 == 0:
        tr = 8
        for t in range(8, r + 1, 8):
            if r % t == 0 and t * c * 4 <= ADAM_BLOCK_BYTES:
                tr = t

    def body(w_ref, g_ref, m_ref, v_ref, *rest):
        go_ref, d_ref, nm_ref, nv_ref = rest[-4:]
        gv = g_ref[...]
        nm = ADAM_B1 * m_ref[...] + (1.0 - ADAM_B1) * gv
        nv = ADAM_B2 * v_ref[...] + (1.0 - ADAM_B2) * (gv * gv)
        m_hat = nm / (1.0 - ADAM_B1 ** ADAM_STEP)
        v_hat = nv / (1.0 - ADAM_B2 ** ADAM_STEP)
        d_ref[...] = -ADAM_LR * (m_hat / (jnp.sqrt(v_hat) + ADAM_EPS) + ADAM_WD * w_ref[...])
        nm_ref[...] = nm
        nv_ref[...] = nv
        go_ref[...] = gv

    lay = pl.BlockSpec((None, tr, c), lambda i: (layer, i, 0))
    one = pl.BlockSpec((tr, c), lambda i: (i, 0))
    shp = jax.ShapeDtypeStruct((L, r, c), F32)
    in_specs = [lay, one, lay, lay]
    args = (w, g, m, v)
    aliases = {}
    if earlier is not None:
        in_specs += [ANY] * 4
        args += tuple(earlier)
        aliases = {4 + k: k for k in range(4)}
    return pl.pallas_call(
        body, name=name, grid=(r // tr,), in_specs=in_specs, out_specs=[lay] * 4, out_shape=[shp] * 4,
        input_output_aliases=aliases, compiler_params=_params('parallel'),
    )(*args)


def _my_place():
    return lax.axis_index('x'), lax.axis_index('y'), lax.axis_index('c')


def _other_chips(x, y):
    return [(1 - x, y), (x, 1 - y), (1 - x, 1 - y)]


def _remote(src, dst, send_sem, recv_sem, to):
    return pltpu.make_async_remote_copy(src_ref=src, dst_ref=dst, send_sem=send_sem, recv_sem=recv_sem,
                                        device_id=to, device_id_type=MESH)


def _hbm_call(body, n_in, out_shapes, scratch, name, aliases=None):
    return pl.pallas_call(body, name=name, in_specs=[ANY] * n_in, out_specs=[ANY] * len(out_shapes),
                          out_shape=out_shapes, scratch_shapes=scratch, input_output_aliases=aliases or {})


def _full_shape(shard_shape, kind):
    L, r, c = shard_shape
    return {'rows': (L, N_CHIPS * r, c), 'cols': (L, r, N_CHIPS * c), 'stack': (N_CHIPS * L, r, c)}[kind]


def _place_block(w, kind, out_dtype, chip_arr, name):
    L, r, c = w.shape
    tr = r if r % 16 else _tile(r, max(16, SUM_BLOCK_BYTES // (4 * c)), 16)
    nrt = r // tr

    def body(k_ref, w_ref, o_ref):
        o_ref[...] = w_ref[...].astype(out_dtype)

    out_map = {'rows': lambda l, i, k_ref: (l, k_ref[0] * nrt + i, 0),
               'cols': lambda l, i, k_ref: (l, i, k_ref[0]),
               'stack': lambda l, i, k_ref: (k_ref[0] * L + l, i, 0)}[kind]
    gs = pltpu.PrefetchScalarGridSpec(
        num_scalar_prefetch=1, grid=(L, nrt),
        in_specs=[pl.BlockSpec((None, tr, c), lambda l, i, k_ref: (l, i, 0))],
        out_specs=pl.BlockSpec((None, tr, c), out_map))
    return pl.pallas_call(
        body, name=name, grid_spec=gs, out_shape=jax.ShapeDtypeStruct(_full_shape(w.shape, kind), out_dtype),
        compiler_params=_params('parallel', 'parallel'),
    )(chip_arr, w)


class _Exchange:
    def __init__(self, inputs, out_shapes, aliases, scratch, start, finish):
        self.inputs, self.out_shapes, self.aliases, self.scratch = list(inputs), list(out_shapes), aliases, scratch
        self.start, self.finish = start, finish


def _run_exchange(ex, name):
    n_in, n_out = len(ex.inputs), len(ex.out_shapes)

    def body(*refs):
        parts = refs[:n_in], refs[n_in:n_in + n_out], refs[n_in + n_out:]
        ex.start(*parts)
        ex.finish(*parts)

    return _hbm_call(body, n_in, ex.out_shapes, ex.scratch, name, aliases=ex.aliases)(*ex.inputs)


def _call_beside(body, name, grid, in_specs, out_specs, out_shape, scratch, args, semantics, beside):
    if beside is None:
        outs = pl.pallas_call(body, name=name, grid=grid, in_specs=in_specs, out_specs=out_specs, out_shape=out_shape,
                              scratch_shapes=scratch, compiler_params=_params(*semantics))(*args)
        return outs, None
    n_in, n_out, n_scr = len(in_specs), len(out_specs), len(scratch)
    b_in, b_out = len(beside.inputs), len(beside.out_shapes)

    def carrier(*refs):
        cuts = [n_in, b_in, n_out, b_out, n_scr]
        parts, at = [], 0
        for size in cuts:
            parts.append(refs[at:at + size])
            at += size
        ins, ex_ins, outs, ex_outs, scr = parts
        ex_scr = refs[at:]
        ids = [pl.program_id(d) for d in range(len(grid))]
        first = functools.reduce(jnp.logical_and, [i == 0 for i in ids])
        last = functools.reduce(jnp.logical_and, [i == g - 1 for i, g in zip(ids, grid)])

        @pl.when(first)
        def _():
            beside.start(ex_ins, ex_outs, ex_scr)

        body(*ins, *outs, *scr)

        @pl.when(last)
        def _():
            beside.finish(ex_ins, ex_outs, ex_scr)

    res = pl.pallas_call(
        carrier, name=name, grid=grid, in_specs=list(in_specs) + [ANY] * b_in,
        out_specs=list(out_specs) + [ANY] * b_out, out_shape=list(out_shape) + beside.out_shapes,
        scratch_shapes=list(scratch) + beside.scratch,
        input_output_aliases={n_in + i: n_out + o for i, o in beside.aliases.items()},
        compiler_params=_params(*['arbitrary'] * len(grid)),
    )(*args, *beside.inputs)
    return res[:n_out], res[n_out:]


def _gather_exchange(fulls, shard_shapes, kinds, split):
    n = len(fulls)

    def plan(outs, send_sems, recv_sems):
        x, y, c = _my_place()
        chip = 2 * x + y
        sibling = (x, y, 1 - c)
        others = _other_chips(x, y)

        def window(a, k, half):
            L, r, cols = shard_shapes[a]
            first, count = (0, r) if half is None else (half * (r // 2), r // 2)
            if kinds[a] == 'rows':
                return outs[a].at[:, pl.ds(k * r + first, count), :]
            if kinds[a] == 'cols':
                return outs[a].at[:, pl.ds(first, count), pl.ds(pl.multiple_of(k * cols, LANES), cols)]
            return outs[a].at[pl.ds(k * L, L), pl.ds(first, count), :]

        sends, arrivals, forwards, forwarded = [], [], [], []
        for a in range(n):
            half = c if split[a] else None
            for j, (ox, oy) in enumerate(others):
                sems = (send_sems.at[6 * a + j], recv_sems.at[6 * a + j], (ox, oy, c))
                sends.append(_remote(window(a, chip, half), window(a, chip, half), *sems))
                got = window(a, 2 * ox + oy, half)
                arrivals.append(_remote(got, got, *sems))
                if split[a]:
                    sems = (send_sems.at[6 * a + 3 + j], recv_sems.at[6 * a + 3 + j], sibling)
                    forwards.append(_remote(got, got, *sems))
                    theirs = window(a, 2 * ox + oy, 1 - c)
                    forwarded.append(_remote(theirs, theirs, *sems))
                else:
                    forwards.append(None)
        return sends, arrivals, forwards, forwarded

    def start(ins, outs, scratch):
        sends, _, _, _ = plan(outs, *scratch)
        for cp in sends:
            cp.start()

    def finish(ins, outs, scratch):
        sends, arrivals, forwards, forwarded = plan(outs, *scratch)
        for arrived, fw in zip(arrivals, forwards):
            arrived.wait_recv()
            if fw is not None:
                fw.start()
        for cp in forwarded:
            cp.wait_recv()
        for cp in sends + [fw for fw in forwards if fw is not None]:
            cp.wait_send()

    scratch = [pltpu.SemaphoreType.DMA((6 * n,)), pltpu.SemaphoreType.DMA((6 * n,))]
    out_shapes = [jax.ShapeDtypeStruct(f.shape, f.dtype) for f in fulls]
    return _Exchange(fulls, out_shapes, {a: a for a in range(n)}, scratch, start, finish)


def _swap_cores(gs, name='swap_cores'):
    n = len(gs)
    out_shapes = [jax.ShapeDtypeStruct((g.shape[0], g.shape[1] // 2, g.shape[2]), g.dtype) for g in gs]

    def body(*refs):
        ins, outs = refs[:n], refs[n:2 * n]
        send_sems, recv_sems = refs[2 * n:]
        x, y, c = _my_place()
        cps = []
        for a in range(n):
            rh = gs[a].shape[1] // 2
            cp = _remote(ins[a].at[:, pl.ds((1 - c) * rh, rh), :], outs[a], send_sems.at[a], recv_sems.at[a],
                         (x, y, 1 - c))
            cp.start()
            cps.append(cp)
        for cp in cps:
            cp.wait()

    scratch = [pltpu.SemaphoreType.DMA((n,)), pltpu.SemaphoreType.DMA((n,))]
    return _hbm_call(body, n, out_shapes, scratch, name)(*gs)


SUM_BLOCK_BYTES = 2 * 1024 * 1024


def _sum_rows(rh, cols):
    return _tile(rh, max(16, SUM_BLOCK_BYTES // (4 * cols)), 16)


def _add_cores(g, other, c_arr, wire_dtype, name):
    n, r, cols = g.shape
    rh = r // 2
    tr = _sum_rows(rh, cols)
    nrt = rh // tr

    def body(c_ref, g_ref, o_ref, q_ref):
        q_ref[...] = (g_ref[...] + o_ref[...]).astype(wire_dtype)

    gs = pltpu.PrefetchScalarGridSpec(
        num_scalar_prefetch=1, grid=(n, nrt),
        in_specs=[pl.BlockSpec((None, tr, cols), lambda j, i, c_ref: (j, c_ref[0] * nrt + i, 0)),
                  pl.BlockSpec((None, tr, cols), lambda j, i, c_ref: (j, i, 0))],
        out_specs=pl.BlockSpec((None, tr, cols), lambda j, i, c_ref: (j, i, 0)))
    return pl.pallas_call(
        body, name=name, grid_spec=gs, out_shape=jax.ShapeDtypeStruct((n, rh, cols), wire_dtype),
        compiler_params=_params('parallel', 'parallel'),
    )(c_arr, g, other)


def _send_exchange(qs):
    n = len(qs)

    def plan(ins, outs, send_sems, recv_sems):
        x, y, c = _my_place()
        return [_remote(ins[a].at[2 * ox + oy], outs[a].at[j], send_sems.at[3 * a + j], recv_sems.at[3 * a + j],
                        (ox, oy, c))
                for a in range(n) for j, (ox, oy) in enumerate(_other_chips(x, y))]

    def start(ins, outs, scratch):
        for cp in plan(ins, outs, *scratch):
            cp.start()

    def finish(ins, outs, scratch):
        cps = plan(ins, outs, *scratch)
        for cp in cps:
            cp.wait_recv()
        for cp in cps:
            cp.wait_send()

    scratch = [pltpu.SemaphoreType.DMA((3 * n,)), pltpu.SemaphoreType.DMA((3 * n,))]
    out_shapes = [jax.ShapeDtypeStruct((3,) + q.shape[1:], q.dtype) for q in qs]
    return _Exchange(qs, out_shapes, {}, scratch, start, finish)


def _sum_chips(q, got, place_arr, name):
    n, rh, cols = q.shape
    tr = _sum_rows(rh, cols)

    def body(p_ref, q_ref, gx_ref, gy_ref, gxy_ref, o_ref):
        f = lambda r: r[...].astype(F32)
        o_ref[...] = (f(q_ref) + f(gxy_ref)) + (f(gx_ref) + f(gy_ref))

    def got_spec(j):
        return pl.BlockSpec((None, tr, cols), lambda i, p_ref: (j, i, 0))

    gs = pltpu.PrefetchScalarGridSpec(
        num_scalar_prefetch=1, grid=(rh // tr,),
        in_specs=[pl.BlockSpec((None, tr, cols), lambda i, p_ref: (p_ref[0], i, 0)),
                  got_spec(0), got_spec(1), got_spec(2)],
        out_specs=pl.BlockSpec((None, tr, cols), lambda i, p_ref: (p_ref[1], i, 0)))
    return pl.pallas_call(
        body, name=name, grid_spec=gs, out_shape=jax.ShapeDtypeStruct((2, rh, cols), F32),
        compiler_params=_params('parallel'),
    )(place_arr, q, got, got, got)


def _join_cores(rs, name='join_cores'):
    n = len(rs)
    out_shapes = [jax.ShapeDtypeStruct(r.shape, r.dtype) for r in rs]

    def body(*refs):
        outs = refs[n:2 * n]
        send_sems, recv_sems = refs[2 * n:]
        x, y, c = _my_place()
        cps = []
        for a in range(n):
            cp = _remote(outs[a].at[c], outs[a].at[c], send_sems.at[a], recv_sems.at[a], (x, y, 1 - c))
            cp.start()
            cps.append(cp)
        for cp in cps:
            cp.wait()

    scratch = [pltpu.SemaphoreType.DMA((n,)), pltpu.SemaphoreType.DMA((n,))]
    return _hbm_call(body, n, out_shapes, scratch, name, aliases={a: a for a in range(n)})(*rs)


def _gate_rows(t, B, S):
    return t.reshape(B, S, N_MAIN_HEADS).transpose(0, 2, 1).reshape(B * N_MAIN_HEADS, S)


def _gate_cols(t, B, S):
    return t.reshape(B, N_MAIN_HEADS, S).transpose(0, 2, 1).reshape(B * S, N_MAIN_HEADS)


def _mem_kv_fwd(mem2, g, w, tag):
    hm = _rms_fwd(mem2, g, name=f'rms_mem_{tag}')
    mkv = _matmul(hm, w, 'nn', ACT_DTYPE, name=f'mm_memkv_{tag}')
    return hm, mkv


def _mem_kv_bwd(mem2, g, w, hm, dmk, dmv, tag):
    dmkv = jnp.concatenate([dmk, dmv], axis=1)
    dw = _matmul(hm, dmkv, 'tn', F32, name=f'mm_memkv_dw_{tag}')
    dhm = _matmul(dmkv, w, 'nt', F32, name=f'mm_memkv_dx_{tag}')
    _, dg = _rms_bwd(mem2, g, dhm, None, name=f'rms_mem_bwd_{tag}')
    return dw, dg


def _ffn_fwd(x, g, w_up, cw, cb, w_down, B, S, tag):
    T = x.shape[0]
    h2 = _rms_fwd(x, g, name=f'rms_ffn_{tag}')
    u = _matmul(h2, w_up, 'nn', ACT_DTYPE, name=f'mm_up_{tag}')
    y = _conv_fwd(u.reshape(B, S, -1), cw, cb, name=f'conv_fwd_{tag}').reshape(T, -1)
    x2 = _matmul(y, w_down, 'nn', F32, res=x, name=f'mm_down_{tag}')
    return x2, (h2, u, y)


def _ffn_bwd(dx2, x, g, w_up, cw, cb, w_down, saved, B, S, tag):
    h2, u, y = saved
    T = x.shape[0]
    dy = _matmul(dx2, w_down, 'nt', ACT_DTYPE, name=f'mm_down_dx_{tag}')
    dw_down = _matmul(y, dx2, 'tn', F32, name=f'mm_down_dw_{tag}')
    dug, duv, dcwg, dcwv, dcbg, dcbv = _conv_bwd(u.reshape(B, S, -1), cw, cb, dy.reshape(B, S, -1),
                                                  name=f'conv_bwd_{tag}')
    du = jnp.concatenate([dug.reshape(T, -1), duv.reshape(T, -1)], axis=1)
    dh2 = _matmul(du, w_up, 'nt', F32, name=f'mm_up_dx_{tag}')
    dw_up = _matmul(h2, du, 'tn', F32, slots=N_CHIPS, name=f'mm_up_dw_{tag}')
    dx, dg = _rms_bwd(x, g, dh2, dx2, name=f'rms_ffn_bwd_{tag}')
    dcw = jnp.concatenate([dcwg, dcwv], axis=1)
    dcb = jnp.concatenate([dcbg, dcbv], axis=1)
    return dx, dg, dw_up, dcw, dcb, dw_down


def _step(x, mem, tgt, W, late_weights=None, reduce_early=None):
    B, S, D = x.shape
    T = B * S
    x0 = x.reshape(T, D)
    mem2 = mem.reshape(-1, D)
    tgt2 = tgt.reshape(T, D)
    row = lambda v: v.reshape(1, -1)
    q3 = 3 * MAIN_WIDTH

    w_in_a = W['w_in_a'][0]
    wa_main = jnp.concatenate([w_in_a[:, :q3], w_in_a[:, q3 + N_MAIN_HEADS:]], axis=1)
    wa_gate = jnp.pad(w_in_a[:, q3:q3 + N_MAIN_HEADS], ((0, 0), (0, LANES - N_MAIN_HEADS)))
    bcol = jnp.tile(W['b_f_a'][0], B).reshape(B * N_MAIN_HEADS, 1)
    nkb = S // min(FOX_BLOCK, S)

    h1a = _rms_fwd(x0, row(W['ln_mix_g'][0]), name='rms_mix_a')
    pa = _matmul(h1a, wa_main, 'nn', ACT_DTYPE, name='mm_in_a')
    flog = _matmul(h1a, wa_gate, 'nn', F32, name='mm_gate_a')
    qkv_a = (pa, pa, pa)
    offs_a = (0, N_MAIN_PAIRS, 2 * N_MAIN_PAIRS)
    qm_off_a = 3 * N_MAIN_PAIRS
    zt = _gate_rows(flog[:, :N_MAIN_HEADS], B, S)
    cum = _gate_fwd(zt, bcol)
    ccol = cum.reshape(B * N_MAIN_HEADS, S, 1)
    crow = cum.reshape(B * N_MAIN_HEADS, nkb, 1, S // nkb)
    (oa, lse), late = _fox_fwd(qkv_a, offs_a, B, S, ccol, crow, beside=late_weights[0] if late_weights else None)
    if late_weights:
        W = {**W, **late_weights[1](late)}
    w_in_b = W['w_in_b'][0]
    hma, mkva = _mem_kv_fwd(mem2, row(W['ln_mem_g'][0]), W['w_memkv'][0], 'a')
    oma = _mem_fwd(pa, qm_off_a, mkva, B, S, name='mem_fwd_a')
    ocat_a = jnp.concatenate([oa, oma], axis=1)
    x1 = _matmul(ocat_a, W['w_out'][0], 'nn', F32, res=x0, name='mm_out_a')
    x2, ffn_a = _ffn_fwd(x1, row(W['ln_ffn_g'][0]), W['w_up'][0], W['conv_w'][0], row(W['conv_b'][0]),
                         W['w_down'][0], B, S, 'a')
    hkv = _rms_fwd(x2, row(W['ln_kv_g']), name='rms_kv')
    kvs = _matmul(hkv, W['w_kv'], 'nn', ACT_DTYPE, name='mm_kv')
    h1b = _rms_fwd(x2, row(W['ln_mix_g'][1]), name='rms_mix_b')
    pb = _matmul(h1b, w_in_b, 'nn', ACT_DTYPE, name='mm_in_b')
    qkv_b = (pb, kvs, kvs)
    offs_b = (0, 0, N_MAIN_PAIRS)
    qm_off_b = N_MAIN_PAIRS
    ob = _sb_fwd(qkv_b, offs_b, B, S)
    hmb, mkvb = _mem_kv_fwd(mem2, row(W['ln_mem_g'][1]), W['w_memkv'][1], 'b')
    omb = _mem_fwd(pb, qm_off_b, mkvb, B, S, name='mem_fwd_b')
    ocat_b = jnp.concatenate([ob, omb], axis=1)
    x3 = _matmul(ocat_b, W['w_out'][1], 'nn', F32, res=x2, name='mm_out_b')
    x4, ffn_b = _ffn_fwd(x3, row(W['ln_ffn_g'][1]), W['w_up'][1], W['conv_w'][1], row(W['conv_b'][1]),
                         W['w_down'][1], B, S, 'b')
    loss, dx4, d_final_g = _final_loss(x4, row(W['final_g']), tgt2)

    dx3, dg_ffn_b, dw_up_b, dcw_b, dcb_b, dw_down_b = _ffn_bwd(
        dx4, x3, row(W['ln_ffn_g'][1]), W['w_up'][1], W['conv_w'][1], row(W['conv_b'][1]), W['w_down'][1],
        ffn_b, B, S, 'b')
    docat = _matmul(dx3, W['w_out'][1], 'nt', ACT_DTYPE, name='mm_out_dx_b')
    dw_out_b = _matmul(ocat_b, dx3, 'tn', F32, name='mm_out_dw_b')
    dqb, dkb, dvb = _sb_bwd(qkv_b, offs_b, B, S, docat)
    dqmb, dmkb, dmvb = _mem_bwd(pb, qm_off_b, mkvb, B, S, docat, N_MAIN_PAIRS, name='mem_bwd_b')
    dw_memkv_b, dg_mem_b = _mem_kv_bwd(mem2, row(W['ln_mem_g'][1]), W['w_memkv'][1], hmb, dmkb, dmvb, 'b')
    dpb = jnp.concatenate([dqb, dqmb], axis=1)
    dh1b = _matmul(dpb, w_in_b, 'nt', F32, name='mm_in_dx_b')
    dw_in_b = _matmul(h1b, dpb, 'tn', F32, name='mm_in_dw_b')
    dx2, dg_mix_b = _rms_bwd(x2, row(W['ln_mix_g'][1]), dh1b, dx3, name='rms_mix_bwd_b')
    dkvs = jnp.concatenate([dkb, dvb], axis=1)
    dhkv = _matmul(dkvs, W['w_kv'], 'nt', F32, name='mm_kv_dx')
    dw_kv = _matmul(hkv, dkvs, 'tn', F32, slots=N_CHIPS, name='mm_kv_dw')
    dx2, dg_kv = _rms_bwd(x2, row(W['ln_kv_g']), dhkv, dx2, name='rms_kv_bwd')

    dx1, dg_ffn_a, dw_up_a, dcw_a, dcb_a, dw_down_a = _ffn_bwd(
        dx2, x1, row(W['ln_ffn_g'][0]), W['w_up'][0], W['conv_w'][0], row(W['conv_b'][0]), W['w_down'][0],
        ffn_a, B, S, 'a')
    docat = _matmul(dx1, W['w_out'][0], 'nt', ACT_DTYPE, name='mm_out_dx_a')
    dw_out_a = _matmul(ocat_a, dx1, 'tn', F32, name='mm_out_dw_a')

    def by_rows(dw):
        return dw.reshape(N_CHIPS, dw.shape[0] // N_CHIPS, dw.shape[1])

    grads = {
        'w_in_b': [by_rows(dw_in_b)],
        'w_kv': [dw_kv],
        'w_out': [by_rows(dw_out_a), by_rows(dw_out_b)],
        'w_up': [dw_up_a, dw_up_b],
        'w_down': [by_rows(dw_down_a), by_rows(dw_down_b)],
    }
    early = [(n, layer, g) for n, gs in grads.items() for layer, g in enumerate(gs)]
    early.append(('w_memkv', 1, by_rows(dw_memkv_b)))
    beside = reduce_early(early) if reduce_early else None
    (dqa, dka, dva, dccol, dcrow), crossed = _fox_bwd(qkv_a, offs_a, B, S, ccol, crow, oa, lse, docat, beside=beside)
    dzt, dbrow = _gate_bwd(zt, bcol, dccol.reshape(B * N_MAIN_HEADS, S) + dcrow.reshape(B * N_MAIN_HEADS, S))
    dqma, dmka, dmva = _mem_bwd(pa, qm_off_a, mkva, B, S, docat, N_MAIN_PAIRS, name='mem_bwd_a')
    dw_memkv_a, dg_mem_a = _mem_kv_bwd(mem2, row(W['ln_mem_g'][0]), W['w_memkv'][0], hma, dmka, dmva, 'a')
    dpa = jnp.concatenate([dqa, dka, dva, dqma], axis=1)
    dflog = jnp.pad(_gate_cols(dzt, B, S), ((0, 0), (0, LANES - N_MAIN_HEADS)))
    dh1a = _matmul(dpa, wa_main, 'nt', F32, name='mm_in_dx_a')
    dh1a = _matmul(dflog, wa_gate, 'nt', F32, res=dh1a, name='mm_gate_dx_a')
    dwa_main = _matmul(h1a, dpa, 'tn', F32, name='mm_in_dw_a')
    dwa_gate = _matmul(h1a, dflog, 'tn', F32, name='mm_gate_dw_a')
    dx0, dg_mix_a = _rms_bwd(x0, row(W['ln_mix_g'][0]), dh1a, dx1, name='rms_mix_bwd_a')

    dw_in_a = jnp.concatenate([dwa_main[:, :q3], dwa_gate[:, :N_MAIN_HEADS], dwa_main[:, q3:]], axis=1)
    grads.update({
        'ln_mix_g': jnp.concatenate([dg_mix_a, dg_mix_b], axis=0),
        'w_in_a': dw_in_a[None],
        'b_f_a': dbrow.reshape(B, N_MAIN_HEADS).sum(axis=0)[None],
        'ln_kv_g': dg_kv[0],
        'ln_mem_g': jnp.concatenate([dg_mem_a, dg_mem_b], axis=0),
        'w_memkv': [by_rows(dw_memkv_a), early[-1][2]],
        'ln_ffn_g': jnp.concatenate([dg_ffn_a, dg_ffn_b], axis=0),
        'conv_w': jnp.stack([dcw_a, dcw_b]),
        'conv_b': jnp.concatenate([dcb_a, dcb_b], axis=0),
        'final_g': d_final_g[0],
    })
    return loss, dx0.reshape(B, S, D), grads, crossed


BLOCKED = ('w_in_b', 'w_kv', 'w_memkv', 'w_out', 'w_up', 'w_down')
MISC_ROWS = 32


def _misc_names():
    return [n for n in PARAM_NAMES if PARAM_SHARD_AXIS[n] is None] + ['conv_w']


def _reduce_begin(arrays, wire, tag):
    _, _, c = _my_place()
    c_arr = jnp.reshape(c, (1,)).astype(jnp.int32)
    others = _swap_cores(arrays, name=f'swap_cores_{tag}')
    return [_add_cores(g, o, c_arr, wire[i], name=f'add_cores_{tag}_{i}')
            for i, (g, o) in enumerate(zip(arrays, others))]


def _reduce_end(qs, crossed, tag):
    x, y, c = _my_place()
    place_arr = jnp.stack([2 * x + y, c]).astype(jnp.int32)
    sums = [_sum_chips(q, g, place_arr, name=f'sum_chips_{tag}_{i}') for i, (q, g) in enumerate(zip(qs, crossed))]
    return [j.reshape(-1, j.shape[-1]) for j in _join_cores(sums, name=f'join_cores_{tag}')]


def _pack_late(grads, shards):
    a_cols = shards['w_in_a'].shape[2]
    a_pad = -(-a_cols // LANES) * LANES
    dw_in_a = grads['w_in_a'][0]
    in_a = jnp.stack([jnp.pad(dw_in_a[:, k * a_cols:(k + 1) * a_cols], ((0, 0), (0, a_pad - a_cols)))
                      for k in range(N_CHIPS)])
    conv_cols = shards['conv_w'].shape[2]
    misc = []
    for k in range(N_CHIPS):
        parts = [grads[n].reshape(-1) for n in _misc_names()[:-1]]
        parts.append(grads['conv_w'][:, :, k * conv_cols:(k + 1) * conv_cols].reshape(-1))
        flat = jnp.concatenate(parts)
        assert flat.shape[0] <= MISC_ROWS * PACK_COLS
        misc.append(jnp.pad(flat, (0, MISC_ROWS * PACK_COLS - flat.shape[0])).reshape(MISC_ROWS, PACK_COLS))
    return in_a, jnp.stack(misc)


def _unpack_misc(rows, shards):
    flat = rows.reshape(-1)
    out, off = {}, 0
    for name in _misc_names():
        shape = shards[name].shape
        size = math.prod(shape)
        out[name] = flat[off:off + size].reshape(-1, shape[-1])
        off += size
    return out


def kernel(x, mem, ln_mix_g, w_in_a, b_f_a, w_in_b, ln_kv_g, w_kv, ln_mem_g, w_memkv, w_out, ln_ffn_g, w_up, conv_w, conv_b, w_down, final_g, loss_target, m_ln_mix_g, m_w_in_a, m_b_f_a, m_w_in_b, m_ln_kv_g, m_w_kv, m_ln_mem_g, m_w_memkv, m_w_out, m_ln_ffn_g, m_w_up, m_conv_w, m_conv_b, m_w_down, m_final_g, v_ln_mix_g, v_w_in_a, v_b_f_a, v_w_in_b, v_ln_kv_g, v_w_kv, v_ln_mem_g, v_w_memkv, v_w_out, v_ln_ffn_g, v_w_up, v_conv_w, v_conv_b, v_w_down, v_final_g):
    shards = dict(ln_mix_g=ln_mix_g, w_in_a=w_in_a, b_f_a=b_f_a, w_in_b=w_in_b, ln_kv_g=ln_kv_g, w_kv=w_kv,
                  ln_mem_g=ln_mem_g, w_memkv=w_memkv, w_out=w_out, ln_ffn_g=ln_ffn_g, w_up=w_up, conv_w=conv_w,
                  conv_b=conv_b, w_down=w_down, final_g=final_g)
    moments_m = dict(ln_mix_g=m_ln_mix_g, w_in_a=m_w_in_a, b_f_a=m_b_f_a, w_in_b=m_w_in_b, ln_kv_g=m_ln_kv_g,
                     w_kv=m_w_kv, ln_mem_g=m_ln_mem_g, w_memkv=m_w_memkv, w_out=m_w_out, ln_ffn_g=m_ln_ffn_g,
                     w_up=m_w_up, conv_w=m_conv_w, conv_b=m_conv_b, w_down=m_w_down, final_g=m_final_g)
    moments_v = dict(ln_mix_g=v_ln_mix_g, w_in_a=v_w_in_a, b_f_a=v_b_f_a, w_in_b=v_w_in_b, ln_kv_g=v_ln_kv_g,
                     w_kv=v_w_kv, ln_mem_g=v_ln_mem_g, w_memkv=v_w_memkv, w_out=v_w_out, ln_ffn_g=v_ln_ffn_g,
                     w_up=v_w_up, conv_w=v_conv_w, conv_b=v_conv_b, w_down=v_w_down, final_g=v_final_g)

    kinds = {'w_in_a': 'stack', 'w_in_b': 'rows', 'w_kv': 'cols', 'w_memkv': 'rows', 'w_out': 'rows', 'w_up': 'cols',
             'w_down': 'rows', 'conv_w': 'cols'}
    mx, my, _ = _my_place()
    chip_arr = jnp.reshape(2 * mx + my, (1,)).astype(jnp.int32)
    placed, shard_shapes = {}, {}
    for n, kind in kinds.items():
        w = shards[n].reshape((-1,) + shards[n].shape[-2:])
        shard_shapes[n] = w.shape
        placed[n] = _place_block(w, kind, F32 if n in F32_GATHERED else jnp.bfloat16, chip_arr, name=f'place_{n}')

    def gather(names):
        return _gather_exchange([placed[n] for n in names], [shard_shapes[n] for n in names],
                                [kinds[n] for n in names], [n not in F32_GATHERED for n in names])

    def as_weights(names, full):
        out = dict(zip(names, full))
        if 'w_in_a' in out:
            out['w_in_a'] = jnp.concatenate([out['w_in_a'][k] for k in range(N_CHIPS)], axis=1)[None]
        if 'w_kv' in out:
            out['w_kv'] = out['w_kv'][0]
        return out

    first = ['w_in_a', 'conv_w']
    late = [n for n in kinds if n not in first]
    W = {**shards, **as_weights(first, _run_exchange(gather(first), 'gather_first'))}

    early = {}

    def reduce_early(items):
        early['owners'] = [(n, layer) for n, layer, _ in items]
        early['qs'] = _reduce_begin([g for _, _, g in items], [jnp.bfloat16] * len(items), 'early')
        return _send_exchange(early['qs'])

    loss_part, grad_x, grads, crossed = _step(x, mem, loss_target, W, (gather(late), functools.partial(as_weights, late)),
                                              reduce_early)
    loss = lax.psum(loss_part[0, 0], ('x', 'y', 'c'))

    g_layers = {n: [None] * (len(grads[n]) if n in BLOCKED else 1) for n in PARAM_NAMES}
    for (n, layer), g in zip(early['owners'], _reduce_end(early['qs'], crossed, 'early')):
        g_layers[n][layer] = g
    in_a, misc = _pack_late(grads, shards)
    qs = _reduce_begin([in_a, misc, grads['w_memkv'][0]], [jnp.bfloat16, F32, jnp.bfloat16], 'late')
    in_a_sum, misc_sum, memkv_sum = _reduce_end(qs, _run_exchange(_send_exchange(qs), 'send_chips_late'), 'late')
    g_layers['w_in_a'][0] = in_a_sum[:, :shards['w_in_a'].shape[2]]
    g_layers['w_memkv'][0] = memkv_sum
    for n, g in _unpack_misc(misc_sum, shards).items():
        g_layers[n][0] = g

    results = {}
    for name in PARAM_NAMES:
        w = shards[name]
        layers = len(g_layers[name])
        as_layers = (layers, -1, w.shape[-1])
        w3, m3, v3 = (t.reshape(as_layers) for t in (w, moments_m[name], moments_v[name]))
        res = None
        for layer, g in enumerate(g_layers[name]):
            res = _adamw(w3, g, m3, v3, layer, res, name=f'adamw_{name}_{layer}')
        results[name] = [t.reshape(w.shape) for t in res]

    return (loss, grad_x, *[results[n][k] for k in range(4) for n in PARAM_NAMES])
```

```python
import functools
import math

import jax
import jax.numpy as jnp
from jax import lax
from jax.experimental import pallas as pl
from jax.experimental.pallas import tpu as pltpu

F32 = jnp.float32
MXU_DTYPE = jnp.bfloat16
ACT_DTYPE = jnp.bfloat16

HEAD_DIM = 64
N_MAIN_HEADS = 12
N_MEM_HEADS = 4
MAIN_WIDTH = N_MAIN_HEADS * HEAD_DIM
MEM_WIDTH = N_MEM_HEADS * HEAD_DIM
EPS = 1e-6
SCALE = HEAD_DIM ** -0.5
NEG_BIG = -1e30
LANES = 128
PACK_COLS = 1024
N_CHIPS = 4

ADAM_LR = 0.001
ADAM_B1 = 0.9
ADAM_B2 = 0.999
ADAM_EPS = 1e-08
ADAM_WD = 0.01
ADAM_STEP = 10

MESH = pl.DeviceIdType.MESH
ANY = pl.BlockSpec(memory_space=pl.ANY)

PARAM_SHARD_AXIS = {
    'ln_mix_g': None, 'w_in_a': 2, 'b_f_a': None, 'w_in_b': 1, 'ln_kv_g': None, 'w_kv': 1,
    'ln_mem_g': None, 'w_memkv': 1, 'w_out': 1, 'ln_ffn_g': None, 'w_up': 2, 'conv_w': 2,
    'conv_b': None, 'w_down': 1, 'final_g': None,
}
PARAM_NAMES = list(PARAM_SHARD_AXIS)
F32_GATHERED = ('conv_w',)


def _tile(n, pref, unit=LANES):
    if n <= pref:
        return n
    best = None
    for t in range(unit, pref + 1, unit):
        if n % t == 0:
            best = t
    assert best is not None, (n, pref)
    return best


MM_ACC_ELEMS = 768 * 1024
MM_K_TILE = 2048
MM_VMEM_MB = 48


def _out_tiles(M, N):
    def divisors(n, cap):
        if n <= LANES:
            return [n]
        return [t for t in range(LANES, min(n, cap) + 1, LANES) if n % t == 0]

    best = None
    for tm in divisors(M, 1536):
        for tn in divisors(N, 2048):
            if tm * tn <= MM_ACC_ELEMS and (best is None or (tm * tn, tn) > (best[0] * best[1], best[1])):
                best = (tm, tn)
    assert best is not None, (M, N)
    return best


def _params(*sem, vmem_mb=None):
    kw = {}
    if sem:
        kw['dimension_semantics'] = sem
    if vmem_mb is not None:
        kw['vmem_limit_bytes'] = vmem_mb * 1024 * 1024
    return pltpu.CompilerParams(**kw)


def _dot(a, b, dims):
    return lax.dot_general(a.astype(MXU_DTYPE), b.astype(MXU_DTYPE), (dims, ((), ())),
                           preferred_element_type=F32)


NN = ((1,), (0,))
NT = ((1,), (1,))
TN = ((0,), (0,))


def _matmul(a, b, mode, out_dtype, res=None, slots=1, name='mm'):
    if mode == 'nn':
        (M, K), (K2, N) = a.shape, b.shape
    elif mode == 'nt':
        (M, K), (N, K2) = a.shape, b.shape
    else:
        (K, M), (K2, N) = a.shape, b.shape
    assert K == K2 and N % slots == 0, (a.shape, b.shape, mode, slots)
    slot_cols = N // slots
    tm, tn = _out_tiles(M, slot_cols)
    per_slot = slot_cols // tn
    tk = _tile(K, MM_K_TILE)
    nk = K // tk
    dims = {'nn': NN, 'nt': NT, 'tn': TN}[mode]
    a_again = a.size * a.dtype.itemsize * (N // tn)
    b_again = b.size * b.dtype.itemsize * (M // tm)
    m_inner = nk == 1 and a_again < b_again

    def body(*refs):
        if res is None:
            (a_ref, b_ref, o_ref), r_ref = refs[:3], None
        else:
            a_ref, b_ref, r_ref, o_ref = refs[:4]

        def finish(out):
            if r_ref is not None:
                out = out + r_ref[...]
            o_ref[...] = out.astype(out_dtype)

        if nk == 1:
            finish(_dot(a_ref[...], b_ref[...], dims))
            return
        acc = refs[-1]
        k = pl.program_id(2)

        @pl.when(k == 0)
        def _():
            acc[...] = jnp.zeros_like(acc)

        acc[...] += _dot(a_ref[...], b_ref[...], dims)

        @pl.when(k == nk - 1)
        def _():
            finish(acc[...])

    def spec(shape, index):
        return pl.BlockSpec(shape, (lambda j, i, k: index(i, j, k)) if m_inner else index)

    a_spec = spec((tk, tm), lambda i, j, k: (k, i)) if mode == 'tn' else spec((tm, tk), lambda i, j, k: (i, k))
    b_spec = spec((tn, tk), lambda i, j, k: (j, k)) if mode == 'nt' else spec((tk, tn), lambda i, j, k: (k, j))
    if slots == 1:
        o_spec = spec((tm, tn), lambda i, j, k: (i, j))
        out_shape = jax.ShapeDtypeStruct((M, N), out_dtype)
    else:
        assert res is None
        o_spec = spec((None, tm, tn), lambda i, j, k: (j // per_slot, i, j % per_slot))
        out_shape = jax.ShapeDtypeStruct((slots, M, slot_cols), out_dtype)
    in_specs = [a_spec, b_spec] + ([o_spec] if res is not None else [])
    args = (a, b) + ((res,) if res is not None else ())
    return pl.pallas_call(
        body, name=name, grid=(N // tn, M // tm, nk) if m_inner else (M // tm, N // tn, nk),
        in_specs=in_specs, out_specs=o_spec,
        out_shape=out_shape,
        scratch_shapes=[] if nk == 1 else [pltpu.VMEM((tm, tn), F32)],
        compiler_params=_params('parallel', 'parallel', 'arbitrary', vmem_mb=MM_VMEM_MB),
    )(*args)


def _rms_fwd(x, g, name):
    T, D = x.shape
    tr = _tile(T, 512)

    def body(x_ref, g_ref, o_ref):
        xv = x_ref[...]
        r = lax.rsqrt(jnp.mean(xv * xv, axis=-1, keepdims=True) + EPS)
        o_ref[...] = (xv * r * g_ref[...]).astype(ACT_DTYPE)

    return pl.pallas_call(
        body, name=name, grid=(T // tr,),
        in_specs=[pl.BlockSpec((tr, D), lambda i: (i, 0)), pl.BlockSpec((1, D), lambda i: (0, 0))],
        out_specs=pl.BlockSpec((tr, D), lambda i: (i, 0)),
        out_shape=jax.ShapeDtypeStruct((T, D), ACT_DTYPE),
        compiler_params=_params('parallel'),
    )(x, g)


def _rms_bwd(x, g, dh, dres, name):
    T, D = x.shape
    tr = _tile(T, 512)
    want_dx = dres is not None

    def body(*refs):
        if want_dx:
            x_ref, g_ref, dh_ref, dres_ref, dx_ref, dg_ref = refs
        else:
            x_ref, g_ref, dh_ref, dg_ref = refs
        i = pl.program_id(0)

        @pl.when(i == 0)
        def _():
            dg_ref[...] = jnp.zeros_like(dg_ref)

        xv = x_ref[...]
        dhv = dh_ref[...].astype(F32)
        r = lax.rsqrt(jnp.mean(xv * xv, axis=-1, keepdims=True) + EPS)
        n = xv * r
        dg_ref[...] += jnp.sum(dhv * n, axis=0, keepdims=True)
        if want_dx:
            dn = dhv * g_ref[...]
            dx = r * (dn - n * jnp.mean(dn * n, axis=-1, keepdims=True))
            dx_ref[...] = dres_ref[...] + dx

    row = pl.BlockSpec((tr, D), lambda i: (i, 0))
    vec = pl.BlockSpec((1, D), lambda i: (0, 0))
    if want_dx:
        return pl.pallas_call(
            body, name=name, grid=(T // tr,),
            in_specs=[row, vec, row, row], out_specs=[row, vec],
            out_shape=[jax.ShapeDtypeStruct((T, D), F32), jax.ShapeDtypeStruct((1, D), F32)],
            compiler_params=_params('arbitrary'),
        )(x, g, dh, dres)
    dg = pl.pallas_call(
        body, name=name, grid=(T // tr,),
        in_specs=[row, vec, row], out_specs=vec,
        out_shape=jax.ShapeDtypeStruct((1, D), F32),
        compiler_params=_params('arbitrary'),
    )(x, g, dh)
    return None, dg


def _final_loss(x, g, tgt, name='final_loss'):
    T, D = x.shape
    tr = _tile(T, 512)

    def body(x_ref, g_ref, t_ref, loss_ref, dx_ref, dg_ref):
        i = pl.program_id(0)

        @pl.when(i == 0)
        def _():
            loss_ref[...] = jnp.zeros_like(loss_ref)
            dg_ref[...] = jnp.zeros_like(dg_ref)

        xv = x_ref[...]
        gv = g_ref[...]
        r = lax.rsqrt(jnp.mean(xv * xv, axis=-1, keepdims=True) + EPS)
        n = xv * r
        e = n * gv - t_ref[...]
        per_tok = jnp.mean(e * e, axis=-1, keepdims=True)
        loss_ref[...] += 0.5 * jnp.sum(per_tok, axis=0, keepdims=True)
        dy = e * (1.0 / D)
        dg_ref[...] += jnp.sum(dy * n, axis=0, keepdims=True)
        dn = dy * gv
        dx_ref[...] = r * (dn - n * jnp.mean(dn * n, axis=-1, keepdims=True))

    row = pl.BlockSpec((tr, D), lambda i: (i, 0))
    vec = pl.BlockSpec((1, D), lambda i: (0, 0))
    one = pl.BlockSpec((1, 1), lambda i: (0, 0))
    return pl.pallas_call(
        body, name=name, grid=(T // tr,),
        in_specs=[row, vec, row], out_specs=[one, row, vec],
        out_shape=[jax.ShapeDtypeStruct((1, 1), F32), jax.ShapeDtypeStruct((T, D), F32),
                   jax.ShapeDtypeStruct((1, D), F32)],
        compiler_params=_params('arbitrary'),
    )(x, g, tgt)


def _log_sigmoid(z):
    return jnp.minimum(z, 0.0) - jnp.log(1.0 + jnp.exp(-jnp.abs(z)))


def _tri(n, rel):
    j = lax.broadcasted_iota(jnp.int32, (n, n), 0)
    s = lax.broadcasted_iota(jnp.int32, (n, n), 1)
    return rel(j, s).astype(MXU_DTYPE)


def _split_dot(x, tri, terms):
    if MXU_DTYPE == F32:
        return jnp.dot(x, tri, preferred_element_type=F32)
    out = None
    rem = x
    for _ in range(terms):
        piece = rem.astype(MXU_DTYPE)
        part = jnp.dot(piece, tri, preferred_element_type=F32)
        out = part if out is None else out + part
        rem = rem - piece.astype(F32)
    return out


def _gate_fwd(zt, bcol, name='gate_fwd'):
    BH, S = zt.shape
    nb = S // LANES

    def body(z_ref, b_ref, c_ref):
        tri = _tri(LANES, lambda j, s: j <= s)
        carry = jnp.zeros((BH, 1), F32)
        for i in range(nb):
            sl = slice(i * LANES, (i + 1) * LANES)
            logf = _log_sigmoid(z_ref[:, sl] + b_ref[...])
            cs = _split_dot(logf, tri, 3) + carry
            c_ref[:, sl] = cs
            carry = cs[:, LANES - 1:LANES]

    return pl.pallas_call(body, name=name, out_shape=jax.ShapeDtypeStruct((BH, S), F32))(zt, bcol)


def _gate_bwd(zt, bcol, dc, name='gate_bwd'):
    BH, S = zt.shape
    nb = S // LANES

    def body(z_ref, b_ref, dc_ref, dz_ref, db_ref):
        tri = _tri(LANES, lambda j, s: j >= s)
        carry = jnp.zeros((BH, 1), F32)
        dsum = jnp.zeros((BH, 1), F32)
        for i in reversed(range(nb)):
            sl = slice(i * LANES, (i + 1) * LANES)
            rs = _split_dot(dc_ref[:, sl], tri, 3) + carry
            carry = rs[:, 0:1]
            z = z_ref[:, sl] + b_ref[...]
            dz = rs * (1.0 - 1.0 / (1.0 + jnp.exp(-z)))
            dz_ref[:, sl] = dz
            dsum = dsum + jnp.sum(dz, axis=-1, keepdims=True)
        db_ref[...] = dsum

    return pl.pallas_call(
        body, name=name,
        out_shape=[jax.ShapeDtypeStruct((BH, S), F32), jax.ShapeDtypeStruct((BH, 1), F32)],
    )(zt, bcol, dc)


FOX_BLOCK = 256


PAIR = 2 * HEAD_DIM
N_MAIN_PAIRS = N_MAIN_HEADS // 2
N_MEM_PAIRS = N_MEM_HEADS // 2


def _lane0(shape):
    return lax.broadcasted_iota(jnp.int32, shape, len(shape) - 1) < HEAD_DIM


def _per_head(x):
    first = _lane0(x.shape)
    zero = jnp.zeros_like(x)
    return jnp.where(first, x, zero), jnp.where(first, zero, x)


def _pick(first, a, b):
    return jnp.where(first, a, b)


GROUP = 2
MAIN_STEPS = N_MAIN_PAIRS // GROUP


def _lanes(p):
    return slice(p * PAIR, (p + 1) * PAIR)


def _q_spec(bq, nq, off, group=1):
    assert off % group == 0
    return pl.BlockSpec((bq, group * PAIR), lambda b, j, i: (b * nq + i, off // group + j))


def _seq_spec(S, off, group=1):
    assert off % group == 0
    return pl.BlockSpec((S, group * PAIR), lambda b, j, i: (b, off // group + j))


def _gate_specs(bq, nk):
    col = pl.BlockSpec((2 * GROUP, bq, 1), lambda b, j, i: (b * MAIN_STEPS + j, i, 0))
    rowv = pl.BlockSpec((2 * GROUP, nk, 1, bq), lambda b, j, i: (b * MAIN_STEPS + j, 0, 0, 0))
    return col, rowv


def _diagonal_mask(bq, strict):
    row = lax.broadcasted_iota(jnp.int32, (bq, bq), 0)
    col = lax.broadcasted_iota(jnp.int32, (bq, bq), 1)
    return (col < row) if strict else (col <= row)


def _scaled(q):
    assert math.log2(SCALE).is_integer()
    return q * jnp.asarray(SCALE, q.dtype)


def _fox_fwd(qkv, offs, B, S, ccol, crow, beside=None, name='fox_fwd'):
    bq = min(FOX_BLOCK, S)
    nq = S // bq

    def body(q_ref, k_ref, v_ref, cc_ref, cr_ref, o_ref, lse_ref):
        i = pl.program_id(2)
        qv = _scaled(q_ref[...])
        qh = [_per_head(qv[:, _lanes(p)]) for p in range(GROUP)]
        first = _lane0((bq, PAIR))

        def step(kb, carry, mask=None):
            m, l, acc = carry
            sl = pl.ds(pl.multiple_of(kb * bq, bq), bq)
            m_new, l_new, acc_new = [], [], []
            for p in range(GROUP):
                ks, vs = k_ref[sl, _lanes(p)], v_ref[sl, _lanes(p)]
                alpha, pv = [], []
                for h in range(2):
                    n = 2 * p + h
                    s = _dot(qh[p][h], ks, NT) + cc_ref[n] - cr_ref[n, kb]
                    if mask is not None:
                        s = jnp.where(mask, s, NEG_BIG)
                    mh = jnp.maximum(m[n], jnp.max(s, axis=-1, keepdims=True))
                    pr = jnp.exp(s - mh)
                    ah = jnp.exp(m[n] - mh)
                    m_new.append(mh)
                    alpha.append(ah)
                    l_new.append(ah * l[n] + jnp.sum(pr, axis=-1, keepdims=True))
                    pv.append(_dot(pr, vs, NN))
                acc_new.append(_pick(first, alpha[0], alpha[1]) * acc[p] + _pick(first, pv[0], pv[1]))
            return tuple(m_new), tuple(l_new), tuple(acc_new)

        negs = tuple(jnp.full((bq, 1), NEG_BIG, F32) for _ in range(2 * GROUP))
        zeros = tuple(jnp.zeros((bq, 1), F32) for _ in range(2 * GROUP))
        acc0 = tuple(jnp.zeros((bq, PAIR), F32) for _ in range(GROUP))
        m, l, acc = step(i, lax.fori_loop(0, i, step, (negs, zeros, acc0)), _diagonal_mask(bq, False))
        for p in range(GROUP):
            o_ref[:, _lanes(p)] = (acc[p] / _pick(first, l[2 * p], l[2 * p + 1])).astype(ACT_DTYPE)
        for n in range(2 * GROUP):
            lse_ref[n] = m[n] + jnp.log(l[n])

    col, rowv = _gate_specs(bq, nq)
    return _call_beside(
        body, name, (B, MAIN_STEPS, nq),
        [_q_spec(bq, nq, offs[0], GROUP), _seq_spec(S, offs[1], GROUP), _seq_spec(S, offs[2], GROUP), col, rowv],
        [_q_spec(bq, nq, 0, GROUP), col],
        [jax.ShapeDtypeStruct((B * S, MAIN_WIDTH), ACT_DTYPE), jax.ShapeDtypeStruct((B * N_MAIN_HEADS, S, 1), F32)],
        [], (*qkv, ccol, crow), ('parallel', 'parallel', 'arbitrary'), beside)


def _fox_bwd(qkv, offs, B, S, ccol, crow, o, lse, do, beside=None, name='fox_bwd'):
    bq = min(FOX_BLOCK, S)
    nq = S // bq

    def body(q_ref, k_ref, v_ref, cc_ref, cr_ref, o_ref, lse_ref, do_ref,
             dq_ref, dk_ref, dv_ref, dcc_ref, dcr_ref, dk_acc, dv_acc):
        i = pl.program_id(2)

        @pl.when(i == 0)
        def _():
            dk_acc[...] = jnp.zeros_like(dk_acc)
            dv_acc[...] = jnp.zeros_like(dv_acc)
            dcr_ref[...] = jnp.zeros_like(dcr_ref)

        qv = _scaled(q_ref[...])
        dov = do_ref[...]
        qp = [qv[:, _lanes(p)] for p in range(GROUP)]
        dop = [dov[:, _lanes(p)] for p in range(GROUP)]
        qh = [_per_head(t) for t in qp]
        doh = [_per_head(t) for t in dop]
        first = _lane0((bq, PAIR))
        prod = dov.astype(F32) * o_ref[...].astype(F32)
        dsum = [jnp.sum(t, axis=-1, keepdims=True) for p in range(GROUP) for t in _per_head(prod[:, _lanes(p)])]

        def step(kb, carry, mask=None):
            dq, dcc = carry
            sl = pl.ds(pl.multiple_of(kb * bq, bq), bq)
            dq_new, dcc_new = [], []
            for p in range(GROUP):
                ks, vs = k_ref[sl, _lanes(p)], v_ref[sl, _lanes(p)]
                dqh, dkh, dvh = [], [], []
                for h in range(2):
                    n = 2 * p + h
                    s = _dot(qh[p][h], ks, NT) + cc_ref[n] - cr_ref[n, kb]
                    pr = jnp.exp(s - lse_ref[n])
                    if mask is not None:
                        pr = jnp.where(mask, pr, 0.0)
                    ds = pr * (_dot(doh[p][h], vs, NT) - dsum[n])
                    dqh.append(_dot(ds, ks, NN))
                    dkh.append(_dot(ds, qp[p], TN))
                    dvh.append(_dot(pr, dop[p], TN))
                    dcr_ref[n, kb] -= jnp.sum(ds, axis=0, keepdims=True)
                    dcc_new.append(dcc[n] + jnp.sum(ds, axis=-1, keepdims=True))
                dk_acc[sl, _lanes(p)] += _pick(first, dkh[0], dkh[1])
                dv_acc[sl, _lanes(p)] += _pick(first, dvh[0], dvh[1])
                dq_new.append(dq[p] + _pick(first, dqh[0], dqh[1]))
            return tuple(dq_new), tuple(dcc_new)

        zeros = tuple(jnp.zeros((bq, 1), F32) for _ in range(2 * GROUP))
        dq0 = tuple(jnp.zeros((bq, PAIR), F32) for _ in range(GROUP))
        dq, dcc = step(i, lax.fori_loop(0, i, step, (dq0, zeros)), _diagonal_mask(bq, False))
        for p in range(GROUP):
            dq_ref[:, _lanes(p)] = (dq[p] * SCALE).astype(ACT_DTYPE)
        for n in range(2 * GROUP):
            dcc_ref[n] = dcc[n]

        @pl.when(i == nq - 1)
        def _():
            dk_ref[...] = dk_acc[...].astype(ACT_DTYPE)
            dv_ref[...] = dv_acc[...].astype(ACT_DTYPE)

    col, rowv = _gate_specs(bq, nq)
    qs, seq = _q_spec(bq, nq, 0, GROUP), _seq_spec(S, 0, GROUP)
    full = jax.ShapeDtypeStruct((B * S, MAIN_WIDTH), ACT_DTYPE)
    wide = pltpu.VMEM((S, GROUP * PAIR), F32)
    return _call_beside(
        body, name, (B, MAIN_STEPS, nq),
        [_q_spec(bq, nq, offs[0], GROUP), _seq_spec(S, offs[1], GROUP), _seq_spec(S, offs[2], GROUP), col, rowv,
         qs, col, qs],
        [qs, seq, seq, col, rowv],
        [full, full, full, jax.ShapeDtypeStruct(ccol.shape, F32), jax.ShapeDtypeStruct(crow.shape, F32)],
        [wide, wide],
        (*qkv, ccol, crow, o, lse, do), ('parallel', 'parallel', 'arbitrary'), beside)


SB_BLOCK = 256


SB_SUM_TERMS = 2


def _sb_block(q_scaled, ks, mask):
    z = _dot(q_scaled, ks, NT)
    a = _log_sigmoid(z)
    l = a - z
    return a, (l if mask is None else jnp.where(mask, l, 0.0))


def _sb_fwd(qkv, offs, B, S, name='sb_fwd'):
    bq = min(SB_BLOCK, S)
    nq = S // bq

    def body(q_ref, k_ref, v_ref, o_ref, tot_ref):
        i = pl.program_id(2)
        qv = _scaled(q_ref[...])
        qh = [_per_head(qv[:, _lanes(p)]) for p in range(GROUP)]
        first = _lane0((bq, PAIR))
        tri = _tri(bq, lambda j, s: j > s)

        def step(kb, carry, mask=None):
            acc, right = carry
            sl = pl.ds(pl.multiple_of(kb * bq, bq), bq)
            acc_new, right_new = [], []
            for p in range(GROUP):
                ks, vs = k_ref[sl, _lanes(p)], v_ref[sl, _lanes(p)]
                pv = []
                for h in range(2):
                    n = 2 * p + h
                    a, l = _sb_block(qh[p][h], ks, mask)
                    w = jnp.exp(a + _split_dot(l, tri, SB_SUM_TERMS) + right[n])
                    if mask is not None:
                        w = jnp.where(mask, w, 0.0)
                    pv.append(_dot(w, vs, NN))
                    right_new.append(right[n] + jnp.sum(l, axis=-1, keepdims=True))
                acc_new.append(acc[p] + _pick(first, pv[0], pv[1]))
            return tuple(acc_new), tuple(right_new)

        zeros = tuple(jnp.zeros((bq, 1), F32) for _ in range(2 * GROUP))
        acc0 = tuple(jnp.zeros((bq, PAIR), F32) for _ in range(GROUP))
        carry = step(i, (acc0, zeros), _diagonal_mask(bq, True))
        acc, total = lax.fori_loop(0, i, lambda n, c: step(i - 1 - n, c), carry)
        for p in range(GROUP):
            o_ref[:, _lanes(p)] = acc[p].astype(ACT_DTYPE)
        for n in range(2 * GROUP):
            tot_ref[n] = total[n]

    col, _ = _gate_specs(bq, nq)
    return pl.pallas_call(
        body, name=name, grid=(B, MAIN_STEPS, nq),
        in_specs=[_q_spec(bq, nq, offs[0], GROUP), _seq_spec(S, offs[1], GROUP), _seq_spec(S, offs[2], GROUP)],
        out_specs=[_q_spec(bq, nq, 0, GROUP), col],
        out_shape=[jax.ShapeDtypeStruct((B * S, MAIN_WIDTH), ACT_DTYPE),
                   jax.ShapeDtypeStruct((B * N_MAIN_HEADS, S, 1), F32)],
        compiler_params=_params('parallel', 'parallel', 'arbitrary'),
    )(*qkv)


def _sb_bwd(qkv, offs, B, S, tot, do, name='sb_bwd'):
    bq = min(SB_BLOCK, S)
    nq = S // bq

    def body(q_ref, k_ref, v_ref, tot_ref, do_ref, dq_ref, dk_ref, dv_ref, dk_acc, dv_acc):
        i = pl.program_id(2)

        @pl.when(i == 0)
        def _():
            dk_acc[...] = jnp.zeros_like(dk_acc)
            dv_acc[...] = jnp.zeros_like(dv_acc)

        qv = _scaled(q_ref[...])
        dov = do_ref[...]
        qp = [qv[:, _lanes(p)] for p in range(GROUP)]
        dop = [dov[:, _lanes(p)] for p in range(GROUP)]
        qh = [_per_head(t) for t in qp]
        doh = [_per_head(t) for t in dop]
        heads = [(p, h) for p in range(GROUP) for h in range(2)]
        first = _lane0((bq, PAIR))
        diagonal = _diagonal_mask(bq, True)
        tri_incl = _tri(bq, lambda j, s: j <= s)
        tri_excl = _tri(bq, lambda j, s: j < s)
        zeros = tuple(jnp.zeros((bq, 1), F32) for _ in heads)
        tot = tuple(tot_ref[n] for n in range(len(heads)))

        def step(kb, carry, mask=None):
            dq, rest_l, left_g = carry
            sl = pl.ds(pl.multiple_of(kb * bq, bq), bq)
            new_dq, new_l, new_g = [], [], []
            for p in range(GROUP):
                ks, vs = k_ref[sl, _lanes(p)], v_ref[sl, _lanes(p)]
                dqh, dkh, dvh = [], [], []
                for h in range(2):
                    n = 2 * p + h
                    a, l = _sb_block(qh[p][h], ks, mask)
                    w = jnp.exp(a - _split_dot(l, tri_incl, SB_SUM_TERMS) + rest_l[n])
                    if mask is not None:
                        w = jnp.where(mask, w, 0.0)
                    g = w * _dot(doh[p][h], vs, NT)
                    beta = jnp.exp(a)
                    dz = g - beta * (g + _split_dot(g, tri_excl, 1) + left_g[n])
                    if mask is not None:
                        dz = jnp.where(mask, dz, 0.0)
                    dqh.append(_dot(dz, ks, NN))
                    dkh.append(_dot(dz, qp[p], TN))
                    dvh.append(_dot(w, dop[p], TN))
                    new_l.append(rest_l[n] - jnp.sum(l, axis=-1, keepdims=True))
                    new_g.append(left_g[n] + jnp.sum(g, axis=-1, keepdims=True))
                dk_acc[sl, _lanes(p)] += _pick(first, dkh[0], dkh[1])
                dv_acc[sl, _lanes(p)] += _pick(first, dvh[0], dvh[1])
                new_dq.append(dq[p] + _pick(first, dqh[0], dqh[1]))
            return tuple(new_dq), tuple(new_l), tuple(new_g)

        dq0 = tuple(jnp.zeros((bq, PAIR), F32) for _ in range(GROUP))
        dq, _, _ = step(i, lax.fori_loop(0, i, step, (dq0, tot, zeros)), diagonal)
        for p in range(GROUP):
            dq_ref[:, _lanes(p)] = (dq[p] * SCALE).astype(ACT_DTYPE)

        @pl.when(i == nq - 1)
        def _():
            dk_ref[...] = dk_acc[...].astype(ACT_DTYPE)
            dv_ref[...] = dv_acc[...].astype(ACT_DTYPE)

    qs, seq = _q_spec(bq, nq, 0, GROUP), _seq_spec(S, 0, GROUP)
    full = jax.ShapeDtypeStruct((B * S, MAIN_WIDTH), ACT_DTYPE)
    wide = pltpu.VMEM((S, GROUP * PAIR), F32)
    col, _ = _gate_specs(bq, nq)
    return pl.pallas_call(
        body, name=name, grid=(B, MAIN_STEPS, nq),
        in_specs=[_q_spec(bq, nq, offs[0], GROUP), _seq_spec(S, offs[1], GROUP), _seq_spec(S, offs[2], GROUP), col,
                  qs],
        out_specs=[qs, seq, seq], out_shape=[full, full, full],
        scratch_shapes=[wide, wide],
        compiler_params=_params('parallel', 'parallel', 'arbitrary'),
    )(*qkv, tot, do)


def _mem_probs(qv, mk):
    s = _dot(qv, mk, NT) * SCALE
    p = jnp.exp(s - jnp.max(s, axis=-1, keepdims=True))
    return p / jnp.sum(p, axis=-1, keepdims=True)


def _mem_fwd(q, q_off, mkv, B, S, name='mem_fwd'):
    M = mkv.shape[0] // B
    bq = _tile(S, 512)
    nq = S // bq

    def body(q_ref, mk_ref, mv_ref, o_ref):
        first = _lane0((bq, PAIR))
        mk, mv = mk_ref[...], mv_ref[...]
        out = [_dot(_mem_probs(qh, mk), mv, NN) for qh in _per_head(q_ref[...])]
        o_ref[...] = _pick(first, out[0], out[1]).astype(ACT_DTYPE)

    return pl.pallas_call(
        body, name=name, grid=(B, N_MEM_PAIRS, nq),
        in_specs=[_q_spec(bq, nq, q_off), _seq_spec(M, 0), _seq_spec(M, N_MEM_PAIRS)],
        out_specs=_q_spec(bq, nq, 0),
        out_shape=jax.ShapeDtypeStruct((B * S, MEM_WIDTH), ACT_DTYPE),
        compiler_params=_params('parallel', 'parallel', 'parallel'),
    )(q, mkv, mkv)


def _mem_bwd(q, q_off, mkv, B, S, do, do_off, name='mem_bwd'):
    M = mkv.shape[0] // B
    bq = _tile(S, 512)
    nq = S // bq

    def body(q_ref, mk_ref, mv_ref, do_ref, dq_ref, dmk_ref, dmv_ref):
        i = pl.program_id(2)

        @pl.when(i == 0)
        def _():
            dmk_ref[...] = jnp.zeros_like(dmk_ref)
            dmv_ref[...] = jnp.zeros_like(dmv_ref)

        qv = q_ref[...]
        dov = do_ref[...]
        mk, mv = mk_ref[...], mv_ref[...]
        first = _lane0((bq, PAIR))
        first_m = _lane0((M, PAIR))
        dqh, dkh, dvh = [], [], []
        for qh, doh in zip(_per_head(qv), _per_head(dov)):
            p = _mem_probs(qh, mk)
            dp = _dot(doh, mv, NT)
            ds = p * (dp - jnp.sum(p * dp, axis=-1, keepdims=True))
            dqh.append(_dot(ds, mk, NN))
            dkh.append(_dot(ds, qv, TN))
            dvh.append(_dot(p, dov, TN))
        dq_ref[...] = (SCALE * _pick(first, dqh[0], dqh[1])).astype(ACT_DTYPE)
        dmk_ref[...] += SCALE * _pick(first_m, dkh[0], dkh[1])
        dmv_ref[...] += _pick(first_m, dvh[0], dvh[1])

    mem_out = jax.ShapeDtypeStruct((B * M, MEM_WIDTH), F32)
    return pl.pallas_call(
        body, name=name, grid=(B, N_MEM_PAIRS, nq),
        in_specs=[_q_spec(bq, nq, q_off), _seq_spec(M, 0), _seq_spec(M, N_MEM_PAIRS), _q_spec(bq, nq, do_off)],
        out_specs=[_q_spec(bq, nq, 0), _seq_spec(M, 0), _seq_spec(M, 0)],
        out_shape=[jax.ShapeDtypeStruct((B * S, MEM_WIDTH), ACT_DTYPE), mem_out, mem_out],
        compiler_params=_params('parallel', 'parallel', 'arbitrary'),
    )(q, mkv, mkv, do)


HALO = 8
CONV_CHUNK = 256


def _conv_chunk(scr, start, rows, w, b):
    at = HALO + start
    return (b + w[0:1, :] * scr[at - 2:at - 2 + rows, :] + w[1:2, :] * scr[at - 1:at - 1 + rows, :]
            + w[2:3, :] * scr[at:at + rows, :])


def _fill_frames(scr, ref):
    scr[0:HALO, :] = jnp.zeros((HALO, scr.shape[1]), F32)
    scr[HALO:, :] = ref[...].astype(F32)


def _sigmoid(x):
    return 0.5 + 0.5 * jnp.tanh(0.5 * x)


def _fold8(x):
    return jnp.sum(x.reshape(x.shape[0] // 8, 8, x.shape[1]), axis=0)


def _conv_specs(S, nf):
    ug = pl.BlockSpec((None, S, LANES), lambda b, j: (b, 0, j))
    uv = pl.BlockSpec((None, S, LANES), lambda b, j: (b, 0, j + nf))
    wg = pl.BlockSpec((3, LANES), lambda b, j: (0, j))
    wv = pl.BlockSpec((3, LANES), lambda b, j: (0, j + nf))
    bg = pl.BlockSpec((1, LANES), lambda b, j: (0, j))
    bv = pl.BlockSpec((1, LANES), lambda b, j: (0, j + nf))
    return ug, uv, wg, wv, bg, bv


def _conv_fwd(u, cw, cb, name='conv_fwd'):
    B, S, F2 = u.shape
    F = F2 // 2
    nf = F // LANES

    ch = min(CONV_CHUNK, S)

    def body(ug_ref, uv_ref, wg_ref, wv_ref, bg_ref, bv_ref, y_ref, g_scr, v_scr):
        _fill_frames(g_scr, ug_ref)
        _fill_frames(v_scr, uv_ref)
        wg, wv, bg, bv = wg_ref[...], wv_ref[...], bg_ref[...], bv_ref[...]
        for start in range(0, S, ch):
            gate = _conv_chunk(g_scr, start, ch, wg, bg)
            val = _conv_chunk(v_scr, start, ch, wv, bv)
            y_ref[start:start + ch, :] = (gate * _sigmoid(gate) * val).astype(ACT_DTYPE)

    specs = _conv_specs(S, nf)
    frames = pltpu.VMEM((HALO + S, LANES), F32)
    return pl.pallas_call(
        body, name=name, grid=(B, nf), in_specs=list(specs), out_specs=specs[0],
        out_shape=jax.ShapeDtypeStruct((B, S, F), ACT_DTYPE), scratch_shapes=[frames, frames],
        compiler_params=_params('parallel', 'parallel'),
    )(u, u, cw, cw, cb, cb)


def _conv_bwd(u, cw, cb, dy, name='conv_bwd'):
    B, S, F2 = u.shape
    F = F2 // 2
    nf = F // LANES

    ch = min(CONV_CHUNK, S)

    def body(ug_ref, uv_ref, wg_ref, wv_ref, bg_ref, bv_ref, dy_ref,
             dug_ref, duv_ref, dwg_ref, dwv_ref, dbg_ref, dbv_ref, g_scr, v_scr, dg_scr, dv_scr):
        b = pl.program_id(1)

        @pl.when(b == 0)
        def _():
            for r in (dwg_ref, dwv_ref, dbg_ref, dbv_ref):
                r[...] = jnp.zeros_like(r)

        _fill_frames(g_scr, ug_ref)
        _fill_frames(v_scr, uv_ref)
        wg, wv, bg, bv = wg_ref[...], wv_ref[...], bg_ref[...], bv_ref[...]
        for scr in (dg_scr, dv_scr):
            scr[S:, :] = jnp.zeros((HALO, LANES), F32)
        for start in range(0, S, ch):
            gate = _conv_chunk(g_scr, start, ch, wg, bg)
            val = _conv_chunk(v_scr, start, ch, wv, bv)
            dyv = dy_ref[start:start + ch, :].astype(F32)
            sg = _sigmoid(gate)
            dv_scr[start:start + ch, :] = dyv * (gate * sg)
            dg_scr[start:start + ch, :] = dyv * val * (sg * (1.0 + gate * (1.0 - sg)))

        for u_scr, d_scr, w, du_ref, dw_ref, db_ref in ((g_scr, dg_scr, wg, dug_ref, dwg_ref, dbg_ref),
                                                         (v_scr, dv_scr, wv, duv_ref, dwv_ref, dbv_ref)):
            sums = [jnp.zeros((8, LANES), F32) for _ in range(4)]
            for start in range(0, S, ch):
                x = u_scr[HALO + start:HALO + start + ch, :]
                d = [d_scr[start + n:start + n + ch, :] for n in range(3)]
                du_ref[start:start + ch, :] = (w[2:3, :] * d[0] + w[1:2, :] * d[1] + w[0:1, :] * d[2]).astype(ACT_DTYPE)
                sums = [sums[0] + _fold8(x * d[2]), sums[1] + _fold8(x * d[1]), sums[2] + _fold8(x * d[0]),
                        sums[3] + _fold8(d[0])]
            total = [jnp.sum(s, axis=0, keepdims=True) for s in sums]
            dw_ref[...] += jnp.concatenate(total[:3], axis=0)
            db_ref[...] += total[3]

    def swap(spec_fn):
        return lambda j, b: spec_fn(b, j)

    ug, uv, wg, wv, bg, bv = _conv_specs(S, nf)
    ins = [pl.BlockSpec(s.block_shape, swap(s.index_map)) for s in (ug, uv, wg, wv, bg, bv, ug)]
    outs = [ins[0], ins[0], ins[2], ins[2], ins[4], ins[4]]
    frames = pltpu.VMEM((HALO + S, LANES), F32)
    return pl.pallas_call(
        body, name=name, grid=(nf, B), in_specs=ins, out_specs=outs, scratch_shapes=[frames] * 4,
        out_shape=[jax.ShapeDtypeStruct((B, S, F), ACT_DTYPE), jax.ShapeDtypeStruct((B, S, F), ACT_DTYPE),
                   jax.ShapeDtypeStruct((3, F), F32), jax.ShapeDtypeStruct((3, F), F32),
                   jax.ShapeDtypeStruct((1, F), F32), jax.ShapeDtypeStruct((1, F), F32)],
        compiler_params=_params('parallel', 'arbitrary'),
    )(u, u, cw, cw, cb, cb, dy)


ADAM_BLOCK_BYTES = 512 * 1024


def _adamw(w, g, m, v, layer, earlier, name):
    L, r, c = w.shape
    tr = r
    if r * c * 4 > ADAM_BLOCK_BYTES and r % 8 == 0:
        tr = 8
        for t in range(8, r + 1, 8):
            if r % t == 0 and t * c * 4 <= ADAM_BLOCK_BYTES:
                tr = t

    def body(w_ref, g_ref, m_ref, v_ref, *rest):
        go_ref, d_ref, nm_ref, nv_ref = rest[-4:]
        gv = g_ref[...]
        nm = ADAM_B1 * m_ref[...] + (1.0 - ADAM_B1) * gv
        nv = ADAM_B2 * v_ref[...] + (1.0 - ADAM_B2) * (gv * gv)
        m_hat = nm / (1.0 - ADAM_B1 ** ADAM_STEP)
        v_hat = nv / (1.0 - ADAM_B2 ** ADAM_STEP)
        d_ref[...] = -ADAM_LR * (m_hat / (jnp.sqrt(v_hat) + ADAM_EPS) + ADAM_WD * w_ref[...])
        nm_ref[...] = nm
        nv_ref[...] = nv
        go_ref[...] = gv

    lay = pl.BlockSpec((None, tr, c), lambda i: (layer, i, 0))
    one = pl.BlockSpec((tr, c), lambda i: (i, 0))
    shp = jax.ShapeDtypeStruct((L, r, c), F32)
    in_specs = [lay, one, lay, lay]
    args = (w, g, m, v)
    aliases = {}
    if earlier is not None:
        in_specs += [ANY] * 4
        args += tuple(earlier)
        aliases = {4 + k: k for k in range(4)}
    return pl.pallas_call(
        body, name=name, grid=(r // tr,), in_specs=in_specs, out_specs=[lay] * 4, out_shape=[shp] * 4,
        input_output_aliases=aliases, compiler_params=_params('parallel'),
    )(*args)


def _my_place():
    return lax.axis_index('x'), lax.axis_index('y'), lax.axis_index('c')


def _other_chips(x, y):
    return [(1 - x, y), (x, 1 - y), (1 - x, 1 - y)]


def _remote(src, dst, send_sem, recv_sem, to):
    return pltpu.make_async_remote_copy(src_ref=src, dst_ref=dst, send_sem=send_sem, recv_sem=recv_sem,
                                        device_id=to, device_id_type=MESH)


def _hbm_call(body, n_in, out_shapes, scratch, name, aliases=None):
    return pl.pallas_call(body, name=name, in_specs=[ANY] * n_in, out_specs=[ANY] * len(out_shapes),
                          out_shape=out_shapes, scratch_shapes=scratch, input_output_aliases=aliases or {})


def _full_shape(shard_shape, kind):
    L, r, c = shard_shape
    return {'rows': (L, N_CHIPS * r, c), 'cols': (L, r, N_CHIPS * c), 'stack': (N_CHIPS * L, r, c)}[kind]


def _place_block(w, kind, out_dtype, chip_arr, name):
    L, r, c = w.shape
    tr = r if r % 16 else _tile(r, max(16, SUM_BLOCK_BYTES // (4 * c)), 16)
    nrt = r // tr

    def body(k_ref, w_ref, o_ref):
        o_ref[...] = w_ref[...].astype(out_dtype)

    out_map = {'rows': lambda l, i, k_ref: (l, k_ref[0] * nrt + i, 0),
               'cols': lambda l, i, k_ref: (l, i, k_ref[0]),
               'stack': lambda l, i, k_ref: (k_ref[0] * L + l, i, 0)}[kind]
    gs = pltpu.PrefetchScalarGridSpec(
        num_scalar_prefetch=1, grid=(L, nrt),
        in_specs=[pl.BlockSpec((None, tr, c), lambda l, i, k_ref: (l, i, 0))],
        out_specs=pl.BlockSpec((None, tr, c), out_map))
    return pl.pallas_call(
        body, name=name, grid_spec=gs, out_shape=jax.ShapeDtypeStruct(_full_shape(w.shape, kind), out_dtype),
        compiler_params=_params('parallel', 'parallel'),
    )(chip_arr, w)


class _Exchange:
    def __init__(self, inputs, out_shapes, aliases, scratch, start, finish):
        self.inputs, self.out_shapes, self.aliases, self.scratch = list(inputs), list(out_shapes), aliases, scratch
        self.start, self.finish = start, finish


def _run_exchange(ex, name):
    n_in, n_out = len(ex.inputs), len(ex.out_shapes)

    def body(*refs):
        parts = refs[:n_in], refs[n_in:n_in + n_out], refs[n_in + n_out:]
        ex.start(*parts)
        ex.finish(*parts)

    return _hbm_call(body, n_in, ex.out_shapes, ex.scratch, name, aliases=ex.aliases)(*ex.inputs)


def _call_beside(body, name, grid, in_specs, out_specs, out_shape, scratch, args, semantics, beside):
    if beside is None:
        outs = pl.pallas_call(body, name=name, grid=grid, in_specs=in_specs, out_specs=out_specs, out_shape=out_shape,
                              scratch_shapes=scratch, compiler_params=_params(*semantics))(*args)
        return outs, None
    n_in, n_out, n_scr = len(in_specs), len(out_specs), len(scratch)
    b_in, b_out = len(beside.inputs), len(beside.out_shapes)

    def carrier(*refs):
        cuts = [n_in, b_in, n_out, b_out, n_scr]
        parts, at = [], 0
        for size in cuts:
            parts.append(refs[at:at + size])
            at += size
        ins, ex_ins, outs, ex_outs, scr = parts
        ex_scr = refs[at:]
        ids = [pl.program_id(d) for d in range(len(grid))]
        first = functools.reduce(jnp.logical_and, [i == 0 for i in ids])
        last = functools.reduce(jnp.logical_and, [i == g - 1 for i, g in zip(ids, grid)])

        @pl.when(first)
        def _():
            beside.start(ex_ins, ex_outs, ex_scr)

        body(*ins, *outs, *scr)

        @pl.when(last)
        def _():
            beside.finish(ex_ins, ex_outs, ex_scr)

    res = pl.pallas_call(
        carrier, name=name, grid=grid, in_specs=list(in_specs) + [ANY] * b_in,
        out_specs=list(out_specs) + [ANY] * b_out, out_shape=list(out_shape) + beside.out_shapes,
        scratch_shapes=list(scratch) + beside.scratch,
        input_output_aliases={n_in + i: n_out + o for i, o in beside.aliases.items()},
        compiler_params=_params(*['arbitrary'] * len(grid)),
    )(*args, *beside.inputs)
    return res[:n_out], res[n_out:]


def _gather_exchange(fulls, shard_shapes, kinds, split):
    n = len(fulls)

    def plan(outs, send_sems, recv_sems):
        x, y, c = _my_place()
        chip = 2 * x + y
        sibling = (x, y, 1 - c)
        others = _other_chips(x, y)

        def window(a, k, half):
            L, r, cols = shard_shapes[a]
            first, count = (0, r) if half is None else (half * (r // 2), r // 2)
            if kinds[a] == 'rows':
                return outs[a].at[:, pl.ds(k * r + first, count), :]
            if kinds[a] == 'cols':
                return outs[a].at[:, pl.ds(first, count), pl.ds(pl.multiple_of(k * cols, LANES), cols)]
            return outs[a].at[pl.ds(k * L, L), pl.ds(first, count), :]

        sends, arrivals, forwards, forwarded = [], [], [], []
        for a in range(n):
            half = c if split[a] else None
            for j, (ox, oy) in enumerate(others):
                sems = (send_sems.at[6 * a + j], recv_sems.at[6 * a + j], (ox, oy, c))
                sends.append(_remote(window(a, chip, half), window(a, chip, half), *sems))
                got = window(a, 2 * ox + oy, half)
                arrivals.append(_remote(got, got, *sems))
                if split[a]:
                    sems = (send_sems.at[6 * a + 3 + j], recv_sems.at[6 * a + 3 + j], sibling)
                    forwards.append(_remote(got, got, *sems))
                    theirs = window(a, 2 * ox + oy, 1 - c)
                    forwarded.append(_remote(theirs, theirs, *sems))
                else:
                    forwards.append(None)
        return sends, arrivals, forwards, forwarded

    def start(ins, outs, scratch):
        sends, _, _, _ = plan(outs, *scratch)
        for cp in sends:
            cp.start()

    def finish(ins, outs, scratch):
        sends, arrivals, forwards, forwarded = plan(outs, *scratch)
        for arrived, fw in zip(arrivals, forwards):
            arrived.wait_recv()
            if fw is not None:
                fw.start()
        for cp in forwarded:
            cp.wait_recv()
        for cp in sends + [fw for fw in forwards if fw is not None]:
            cp.wait_send()

    scratch = [pltpu.SemaphoreType.DMA((6 * n,)), pltpu.SemaphoreType.DMA((6 * n,))]
    out_shapes = [jax.ShapeDtypeStruct(f.shape, f.dtype) for f in fulls]
    return _Exchange(fulls, out_shapes, {a: a for a in range(n)}, scratch, start, finish)


def _swap_cores(gs, name='swap_cores'):
    n = len(gs)
    out_shapes = [jax.ShapeDtypeStruct((g.shape[0], g.shape[1] // 2, g.shape[2]), g.dtype) for g in gs]

    def body(*refs):
        ins, outs = refs[:n], refs[n:2 * n]
        send_sems, recv_sems = refs[2 * n:]
        x, y, c = _my_place()
        cps = []
        for a in range(n):
            rh = gs[a].shape[1] // 2
            cp = _remote(ins[a].at[:, pl.ds((1 - c) * rh, rh), :], outs[a], send_sems.at[a], recv_sems.at[a],
                         (x, y, 1 - c))
            cp.start()
            cps.append(cp)
        for cp in cps:
            cp.wait()

    scratch = [pltpu.SemaphoreType.DMA((n,)), pltpu.SemaphoreType.DMA((n,))]
    return _hbm_call(body, n, out_shapes, scratch, name)(*gs)


SUM_BLOCK_BYTES = 2 * 1024 * 1024


def _sum_rows(rh, cols):
    return _tile(rh, max(16, SUM_BLOCK_BYTES // (4 * cols)), 16)


def _add_cores(g, other, c_arr, wire_dtype, name):
    n, r, cols = g.shape
    rh = r // 2
    tr = _sum_rows(rh, cols)
    nrt = rh // tr

    def body(c_ref, g_ref, o_ref, q_ref):
        q_ref[...] = (g_ref[...] + o_ref[...]).astype(wire_dtype)

    gs = pltpu.PrefetchScalarGridSpec(
        num_scalar_prefetch=1, grid=(n, nrt),
        in_specs=[pl.BlockSpec((None, tr, cols), lambda j, i, c_ref: (j, c_ref[0] * nrt + i, 0)),
                  pl.BlockSpec((None, tr, cols), lambda j, i, c_ref: (j, i, 0))],
        out_specs=pl.BlockSpec((None, tr, cols), lambda j, i, c_ref: (j, i, 0)))
    return pl.pallas_call(
        body, name=name, grid_spec=gs, out_shape=jax.ShapeDtypeStruct((n, rh, cols), wire_dtype),
        compiler_params=_params('parallel', 'parallel'),
    )(c_arr, g, other)


def _send_exchange(qs):
    n = len(qs)

    def plan(ins, outs, send_sems, recv_sems):
        x, y, c = _my_place()
        return [_remote(ins[a].at[2 * ox + oy], outs[a].at[j], send_sems.at[3 * a + j], recv_sems.at[3 * a + j],
                        (ox, oy, c))
                for a in range(n) for j, (ox, oy) in enumerate(_other_chips(x, y))]

    def start(ins, outs, scratch):
        for cp in plan(ins, outs, *scratch):
            cp.start()

    def finish(ins, outs, scratch):
        cps = plan(ins, outs, *scratch)
        for cp in cps:
            cp.wait_recv()
        for cp in cps:
            cp.wait_send()

    scratch = [pltpu.SemaphoreType.DMA((3 * n,)), pltpu.SemaphoreType.DMA((3 * n,))]
    out_shapes = [jax.ShapeDtypeStruct((3,) + q.shape[1:], q.dtype) for q in qs]
    return _Exchange(qs, out_shapes, {}, scratch, start, finish)


def _sum_chips(q, got, place_arr, name):
    n, rh, cols = q.shape
    tr = _sum_rows(rh, cols)

    def body(p_ref, q_ref, gx_ref, gy_ref, gxy_ref, o_ref):
        f = lambda r: r[...].astype(F32)
        o_ref[...] = (f(q_ref) + f(gxy_ref)) + (f(gx_ref) + f(gy_ref))

    def got_spec(j):
        return pl.BlockSpec((None, tr, cols), lambda i, p_ref: (j, i, 0))

    gs = pltpu.PrefetchScalarGridSpec(
        num_scalar_prefetch=1, grid=(rh // tr,),
        in_specs=[pl.BlockSpec((None, tr, cols), lambda i, p_ref: (p_ref[0], i, 0)),
                  got_spec(0), got_spec(1), got_spec(2)],
        out_specs=pl.BlockSpec((None, tr, cols), lambda i, p_ref: (p_ref[1], i, 0)))
    return pl.pallas_call(
        body, name=name, grid_spec=gs, out_shape=jax.ShapeDtypeStruct((2, rh, cols), F32),
        compiler_params=_params('parallel'),
    )(place_arr, q, got, got, got)


def _join_cores(rs, name='join_cores'):
    n = len(rs)
    out_shapes = [jax.ShapeDtypeStruct(r.shape, r.dtype) for r in rs]

    def body(*refs):
        outs = refs[n:2 * n]
        send_sems, recv_sems = refs[2 * n:]
        x, y, c = _my_place()
        cps = []
        for a in range(n):
            cp = _remote(outs[a].at[c], outs[a].at[c], send_sems.at[a], recv_sems.at[a], (x, y, 1 - c))
            cp.start()
            cps.append(cp)
        for cp in cps:
            cp.wait()

    scratch = [pltpu.SemaphoreType.DMA((n,)), pltpu.SemaphoreType.DMA((n,))]
    return _hbm_call(body, n, out_shapes, scratch, name, aliases={a: a for a in range(n)})(*rs)


def _gate_rows(t, B, S):
    return t.reshape(B, S, N_MAIN_HEADS).transpose(0, 2, 1).reshape(B * N_MAIN_HEADS, S)


def _gate_cols(t, B, S):
    return t.reshape(B, N_MAIN_HEADS, S).transpose(0, 2, 1).reshape(B * S, N_MAIN_HEADS)


def _mem_kv_fwd(mem2, g, w, tag):
    hm = _rms_fwd(mem2, g, name=f'rms_mem_{tag}')
    mkv = _matmul(hm, w, 'nn', ACT_DTYPE, name=f'mm_memkv_{tag}')
    return hm, mkv


def _mem_kv_bwd(mem2, g, w, hm, dmk, dmv, tag):
    dmkv = jnp.concatenate([dmk, dmv], axis=1)
    dw = _matmul(hm, dmkv, 'tn', F32, name=f'mm_memkv_dw_{tag}')
    dhm = _matmul(dmkv, w, 'nt', F32, name=f'mm_memkv_dx_{tag}')
    _, dg = _rms_bwd(mem2, g, dhm, None, name=f'rms_mem_bwd_{tag}')
    return dw, dg


def _ffn_fwd(x, g, w_up, cw, cb, w_down, B, S, tag):
    T = x.shape[0]
    h2 = _rms_fwd(x, g, name=f'rms_ffn_{tag}')
    u = _matmul(h2, w_up, 'nn', ACT_DTYPE, name=f'mm_up_{tag}')
    y = _conv_fwd(u.reshape(B, S, -1), cw, cb, name=f'conv_fwd_{tag}').reshape(T, -1)
    x2 = _matmul(y, w_down, 'nn', F32, res=x, name=f'mm_down_{tag}')
    return x2, (h2, u, y)


def _ffn_bwd(dx2, x, g, w_up, cw, cb, w_down, saved, B, S, tag):
    h2, u, y = saved
    T = x.shape[0]
    dy = _matmul(dx2, w_down, 'nt', ACT_DTYPE, name=f'mm_down_dx_{tag}')
    dw_down = _matmul(y, dx2, 'tn', F32, name=f'mm_down_dw_{tag}')
    dug, duv, dcwg, dcwv, dcbg, dcbv = _conv_bwd(u.reshape(B, S, -1), cw, cb, dy.reshape(B, S, -1),
                                                  name=f'conv_bwd_{tag}')
    du = jnp.concatenate([dug.reshape(T, -1), duv.reshape(T, -1)], axis=1)
    dh2 = _matmul(du, w_up, 'nt', F32, name=f'mm_up_dx_{tag}')
    dw_up = _matmul(h2, du, 'tn', F32, slots=N_CHIPS, name=f'mm_up_dw_{tag}')
    dx, dg = _rms_bwd(x, g, dh2, dx2, name=f'rms_ffn_bwd_{tag}')
    dcw = jnp.concatenate([dcwg, dcwv], axis=1)
    dcb = jnp.concatenate([dcbg, dcbv], axis=1)
    return dx, dg, dw_up, dcw, dcb, dw_down


def _step(x, mem, tgt, W, late_weights=None, reduce_early=None):
    B, S, D = x.shape
    T = B * S
    x0 = x.reshape(T, D)
    mem2 = mem.reshape(-1, D)
    tgt2 = tgt.reshape(T, D)
    row = lambda v: v.reshape(1, -1)
    q3 = 3 * MAIN_WIDTH

    w_in_a = W['w_in_a'][0]
    wa_main = jnp.concatenate([w_in_a[:, :q3], w_in_a[:, q3 + N_MAIN_HEADS:]], axis=1)
    wa_gate = jnp.pad(w_in_a[:, q3:q3 + N_MAIN_HEADS], ((0, 0), (0, LANES - N_MAIN_HEADS)))
    bcol = jnp.tile(W['b_f_a'][0], B).reshape(B * N_MAIN_HEADS, 1)
    nkb = S // min(FOX_BLOCK, S)

    h1a = _rms_fwd(x0, row(W['ln_mix_g'][0]), name='rms_mix_a')
    pa = _matmul(h1a, wa_main, 'nn', ACT_DTYPE, name='mm_in_a')
    flog = _matmul(h1a, wa_gate, 'nn', F32, name='mm_gate_a')
    qkv_a = (pa, pa, pa)
    offs_a = (0, N_MAIN_PAIRS, 2 * N_MAIN_PAIRS)
    qm_off_a = 3 * N_MAIN_PAIRS
    zt = _gate_rows(flog[:, :N_MAIN_HEADS], B, S)
    cum = _gate_fwd(zt, bcol)
    ccol = cum.reshape(B * N_MAIN_HEADS, S, 1)
    crow = cum.reshape(B * N_MAIN_HEADS, nkb, 1, S // nkb)
    (oa, lse), late = _fox_fwd(qkv_a, offs_a, B, S, ccol, crow, beside=late_weights[0] if late_weights else None)
    if late_weights:
        W = {**W, **late_weights[1](late)}
    w_in_b = W['w_in_b'][0]
    hma, mkva = _mem_kv_fwd(mem2, row(W['ln_mem_g'][0]), W['w_memkv'][0], 'a')
    oma = _mem_fwd(pa, qm_off_a, mkva, B, S, name='mem_fwd_a')
    ocat_a = jnp.concatenate([oa, oma], axis=1)
    x1 = _matmul(ocat_a, W['w_out'][0], 'nn', F32, res=x0, name='mm_out_a')
    x2, ffn_a = _ffn_fwd(x1, row(W['ln_ffn_g'][0]), W['w_up'][0], W['conv_w'][0], row(W['conv_b'][0]),
                         W['w_down'][0], B, S, 'a')
    hkv = _rms_fwd(x2, row(W['ln_kv_g']), name='rms_kv')
    kvs = _matmul(hkv, W['w_kv'], 'nn', ACT_DTYPE, name='mm_kv')
    h1b = _rms_fwd(x2, row(W['ln_mix_g'][1]), name='rms_mix_b')
    pb = _matmul(h1b, w_in_b, 'nn', ACT_DTYPE, name='mm_in_b')
    qkv_b = (pb, kvs, kvs)
    offs_b = (0, 0, N_MAIN_PAIRS)
    qm_off_b = N_MAIN_PAIRS
    ob, tot_b = _sb_fwd(qkv_b, offs_b, B, S)
    hmb, mkvb = _mem_kv_fwd(mem2, row(W['ln_mem_g'][1]), W['w_memkv'][1], 'b')
    omb = _mem_fwd(pb, qm_off_b, mkvb, B, S, name='mem_fwd_b')
    ocat_b = jnp.concatenate([ob, omb], axis=1)
    x3 = _matmul(ocat_b, W['w_out'][1], 'nn', F32, res=x2, name='mm_out_b')
    x4, ffn_b = _ffn_fwd(x3, row(W['ln_ffn_g'][1]), W['w_up'][1], W['conv_w'][1], row(W['conv_b'][1]),
                         W['w_down'][1], B, S, 'b')
    loss, dx4, d_final_g = _final_loss(x4, row(W['final_g']), tgt2)

    dx3, dg_ffn_b, dw_up_b, dcw_b, dcb_b, dw_down_b = _ffn_bwd(
        dx4, x3, row(W['ln_ffn_g'][1]), W['w_up'][1], W['conv_w'][1], row(W['conv_b'][1]), W['w_down'][1],
        ffn_b, B, S, 'b')
    docat = _matmul(dx3, W['w_out'][1], 'nt', ACT_DTYPE, name='mm_out_dx_b')
    dw_out_b = _matmul(ocat_b, dx3, 'tn', F32, name='mm_out_dw_b')
    dqb, dkb, dvb = _sb_bwd(qkv_b, offs_b, B, S, tot_b, docat)
    dqmb, dmkb, dmvb = _mem_bwd(pb, qm_off_b, mkvb, B, S, docat, N_MAIN_PAIRS, name='mem_bwd_b')
    dw_memkv_b, dg_mem_b = _mem_kv_bwd(mem2, row(W['ln_mem_g'][1]), W['w_memkv'][1], hmb, dmkb, dmvb, 'b')
    dpb = jnp.concatenate([dqb, dqmb], axis=1)
    dh1b = _matmul(dpb, w_in_b, 'nt', F32, name='mm_in_dx_b')
    dw_in_b = _matmul(h1b, dpb, 'tn', F32, name='mm_in_dw_b')
    dx2, dg_mix_b = _rms_bwd(x2, row(W['ln_mix_g'][1]), dh1b, dx3, name='rms_mix_bwd_b')
    dkvs = jnp.concatenate([dkb, dvb], axis=1)
    dhkv = _matmul(dkvs, W['w_kv'], 'nt', F32, name='mm_kv_dx')
    dw_kv = _matmul(hkv, dkvs, 'tn', F32, slots=N_CHIPS, name='mm_kv_dw')
    dx2, dg_kv = _rms_bwd(x2, row(W['ln_kv_g']), dhkv, dx2, name='rms_kv_bwd')

    dx1, dg_ffn_a, dw_up_a, dcw_a, dcb_a, dw_down_a = _ffn_bwd(
        dx2, x1, row(W['ln_ffn_g'][0]), W['w_up'][0], W['conv_w'][0], row(W['conv_b'][0]), W['w_down'][0],
        ffn_a, B, S, 'a')
    docat = _matmul(dx1, W['w_out'][0], 'nt', ACT_DTYPE, name='mm_out_dx_a')
    dw_out_a = _matmul(ocat_a, dx1, 'tn', F32, name='mm_out_dw_a')

    def by_rows(dw):
        return dw.reshape(N_CHIPS, dw.shape[0] // N_CHIPS, dw.shape[1])

    grads = {
        'w_in_b': [by_rows(dw_in_b)],
        'w_kv': [dw_kv],
        'w_out': [by_rows(dw_out_a), by_rows(dw_out_b)],
        'w_up': [dw_up_a, dw_up_b],
        'w_down': [by_rows(dw_down_a), by_rows(dw_down_b)],
    }
    early = [(n, layer, g) for n, gs in grads.items() for layer, g in enumerate(gs)]
    early.append(('w_memkv', 1, by_rows(dw_memkv_b)))
    beside = reduce_early(early) if reduce_early else None
    (dqa, dka, dva, dccol, dcrow), crossed = _fox_bwd(qkv_a, offs_a, B, S, ccol, crow, oa, lse, docat, beside=beside)
    dzt, dbrow = _gate_bwd(zt, bcol, dccol.reshape(B * N_MAIN_HEADS, S) + dcrow.reshape(B * N_MAIN_HEADS, S))
    dqma, dmka, dmva = _mem_bwd(pa, qm_off_a, mkva, B, S, docat, N_MAIN_PAIRS, name='mem_bwd_a')
    dw_memkv_a, dg_mem_a = _mem_kv_bwd(mem2, row(W['ln_mem_g'][0]), W['w_memkv'][0], hma, dmka, dmva, 'a')
    dpa = jnp.concatenate([dqa, dka, dva, dqma], axis=1)
    dflog = jnp.pad(_gate_cols(dzt, B, S), ((0, 0), (0, LANES - N_MAIN_HEADS)))
    dh1a = _matmul(dpa, wa_main, 'nt', F32, name='mm_in_dx_a')
    dh1a = _matmul(dflog, wa_gate, 'nt', F32, res=dh1a, name='mm_gate_dx_a')
    dwa_main = _matmul(h1a, dpa, 'tn', F32, name='mm_in_dw_a')
    dwa_gate = _matmul(h1a, dflog, 'tn', F32, name='mm_gate_dw_a')
    dx0, dg_mix_a = _rms_bwd(x0, row(W['ln_mix_g'][0]), dh1a, dx1, name='rms_mix_bwd_a')

    dw_in_a = jnp.concatenate([dwa_main[:, :q3], dwa_gate[:, :N_MAIN_HEADS], dwa_main[:, q3:]], axis=1)
    grads.update({
        'ln_mix_g': jnp.concatenate([dg_mix_a, dg_mix_b], axis=0),
        'w_in_a': dw_in_a[None],
        'b_f_a': dbrow.reshape(B, N_MAIN_HEADS).sum(axis=0)[None],
        'ln_kv_g': dg_kv[0],
        'ln_mem_g': jnp.concatenate([dg_mem_a, dg_mem_b], axis=0),
        'w_memkv': [by_rows(dw_memkv_a), early[-1][2]],
        'ln_ffn_g': jnp.concatenate([dg_ffn_a, dg_ffn_b], axis=0),
        'conv_w': jnp.stack([dcw_a, dcw_b]),
        'conv_b': jnp.concatenate([dcb_a, dcb_b], axis=0),
        'final_g': d_final_g[0],
    })
    return loss, dx0.reshape(B, S, D), grads, crossed


BLOCKED = ('w_in_b', 'w_kv', 'w_memkv', 'w_out', 'w_up', 'w_down')
MISC_ROWS = 32


def _misc_names():
    return [n for n in PARAM_NAMES if PARAM_SHARD_AXIS[n] is None] + ['conv_w']


def _reduce_begin(arrays, wire, tag):
    _, _, c = _my_place()
    c_arr = jnp.reshape(c, (1,)).astype(jnp.int32)
    others = _swap_cores(arrays, name=f'swap_cores_{tag}')
    return [_add_cores(g, o, c_arr, wire[i], name=f'add_cores_{tag}_{i}')
            for i, (g, o) in enumerate(zip(arrays, others))]


def _reduce_end(qs, crossed, tag):
    x, y, c = _my_place()
    place_arr = jnp.stack([2 * x + y, c]).astype(jnp.int32)
    sums = [_sum_chips(q, g, place_arr, name=f'sum_chips_{tag}_{i}') for i, (q, g) in enumerate(zip(qs, crossed))]
    return [j.reshape(-1, j.shape[-1]) for j in _join_cores(sums, name=f'join_cores_{tag}')]


def _pack_late(grads, shards):
    a_cols = shards['w_in_a'].shape[2]
    a_pad = -(-a_cols // LANES) * LANES
    dw_in_a = grads['w_in_a'][0]
    in_a = jnp.stack([jnp.pad(dw_in_a[:, k * a_cols:(k + 1) * a_cols], ((0, 0), (0, a_pad - a_cols)))
                      for k in range(N_CHIPS)])
    conv_cols = shards['conv_w'].shape[2]
    misc = []
    for k in range(N_CHIPS):
        parts = [grads[n].reshape(-1) for n in _misc_names()[:-1]]
        parts.append(grads['conv_w'][:, :, k * conv_cols:(k + 1) * conv_cols].reshape(-1))
        flat = jnp.concatenate(parts)
        assert flat.shape[0] <= MISC_ROWS * PACK_COLS
        misc.append(jnp.pad(flat, (0, MISC_ROWS * PACK_COLS - flat.shape[0])).reshape(MISC_ROWS, PACK_COLS))
    return in_a, jnp.stack(misc)


def _unpack_misc(rows, shards):
    flat = rows.reshape(-1)
    out, off = {}, 0
    for name in _misc_names():
        shape = shards[name].shape
        size = math.prod(shape)
        out[name] = flat[off:off + size].reshape(-1, shape[-1])
        off += size
    return out


def kernel(x, mem, ln_mix_g, w_in_a, b_f_a, w_in_b, ln_kv_g, w_kv, ln_mem_g, w_memkv, w_out, ln_ffn_g, w_up, conv_w, conv_b, w_down, final_g, loss_target, m_ln_mix_g, m_w_in_a, m_b_f_a, m_w_in_b, m_ln_kv_g, m_w_kv, m_ln_mem_g, m_w_memkv, m_w_out, m_ln_ffn_g, m_w_up, m_conv_w, m_conv_b, m_w_down, m_final_g, v_ln_mix_g, v_w_in_a, v_b_f_a, v_w_in_b, v_ln_kv_g, v_w_kv, v_ln_mem_g, v_w_memkv, v_w_out, v_ln_ffn_g, v_w_up, v_conv_w, v_conv_b, v_w_down, v_final_g):
    shards = dict(ln_mix_g=ln_mix_g, w_in_a=w_in_a, b_f_a=b_f_a, w_in_b=w_in_b, ln_kv_g=ln_kv_g, w_kv=w_kv,
                  ln_mem_g=ln_mem_g, w_memkv=w_memkv, w_out=w_out, ln_ffn_g=ln_ffn_g, w_up=w_up, conv_w=conv_w,
                  conv_b=conv_b, w_down=w_down, final_g=final_g)
    moments_m = dict(ln_mix_g=m_ln_mix_g, w_in_a=m_w_in_a, b_f_a=m_b_f_a, w_in_b=m_w_in_b, ln_kv_g=m_ln_kv_g,
                     w_kv=m_w_kv, ln_mem_g=m_ln_mem_g, w_memkv=m_w_memkv, w_out=m_w_out, ln_ffn_g=m_ln_ffn_g,
                     w_up=m_w_up, conv_w=m_conv_w, conv_b=m_conv_b, w_down=m_w_down, final_g=m_final_g)
    moments_v = dict(ln_mix_g=v_ln_mix_g, w_in_a=v_w_in_a, b_f_a=v_b_f_a, w_in_b=v_w_in_b, ln_kv_g=v_ln_kv_g,
                     w_kv=v_w_kv, ln_mem_g=v_ln_mem_g, w_memkv=v_w_memkv, w_out=v_w_out, ln_ffn_g=v_ln_ffn_g,
                     w_up=v_w_up, conv_w=v_conv_w, conv_b=v_conv_b, w_down=v_w_down, final_g=v_final_g)

    kinds = {'w_in_a': 'stack', 'w_in_b': 'rows', 'w_kv': 'cols', 'w_memkv': 'rows', 'w_out': 'rows', 'w_up': 'cols',
             'w_down': 'rows', 'conv_w': 'cols'}
    mx, my, _ = _my_place()
    chip_arr = jnp.reshape(2 * mx + my, (1,)).astype(jnp.int32)
    placed, shard_shapes = {}, {}
    for n, kind in kinds.items():
        w = shards[n].reshape((-1,) + shards[n].shape[-2:])
        shard_shapes[n] = w.shape
        placed[n] = _place_block(w, kind, F32 if n in F32_GATHERED else jnp.bfloat16, chip_arr, name=f'place_{n}')

    def gather(names):
        return _gather_exchange([placed[n] for n in names], [shard_shapes[n] for n in names],
                                [kinds[n] for n in names], [n not in F32_GATHERED for n in names])

    def as_weights(names, full):
        out = dict(zip(names, full))
        if 'w_in_a' in out:
            out['w_in_a'] = jnp.concatenate([out['w_in_a'][k] for k in range(N_CHIPS)], axis=1)[None]
        if 'w_kv' in out:
            out['w_kv'] = out['w_kv'][0]
        return out

    first = ['w_in_a', 'conv_w']
    late = [n for n in kinds if n not in first]
    W = {**shards, **as_weights(first, _run_exchange(gather(first), 'gather_first'))}

    early = {}

    def reduce_early(items):
        early['owners'] = [(n, layer) for n, layer, _ in items]
        early['qs'] = _reduce_begin([g for _, _, g in items], [jnp.bfloat16] * len(items), 'early')
        return _send_exchange(early['qs'])

    loss_part, grad_x, grads, crossed = _step(x, mem, loss_target, W, (gather(late), functools.partial(as_weights, late)),
                                              reduce_early)
    loss = lax.psum(loss_part[0, 0], ('x', 'y', 'c'))

    g_layers = {n: [None] * (len(grads[n]) if n in BLOCKED else 1) for n in PARAM_NAMES}
    for (n, layer), g in zip(early['owners'], _reduce_end(early['qs'], crossed, 'early')):
        g_layers[n][layer] = g
    in_a, misc = _pack_late(grads, shards)
    qs = _reduce_begin([in_a, misc, grads['w_memkv'][0]], [jnp.bfloat16, F32, jnp.bfloat16], 'late')
    in_a_sum, misc_sum, memkv_sum = _reduce_end(qs, _run_exchange(_send_exchange(qs), 'send_chips_late'), 'late')
    g_layers['w_in_a'][0] = in_a_sum[:, :shards['w_in_a'].shape[2]]
    g_layers['w_memkv'][0] = memkv_sum
    for n, g in _unpack_misc(misc_sum, shards).items():
        g_layers[n][0] = g

    results = {}
    for name in PARAM_NAMES:
        w = shards[name]
        layers = len(g_layers[name])
        as_layers = (layers, -1, w.shape[-1])
        w3, m3, v3 = (t.reshape(as_layers) for t in (w, moments_m[name], moments_v[name]))
        res = None
        for layer, g in enumerate(g_layers[name]):
            res = _adamw(w3, g, m3, v3, layer, res, name=f'adamw_{name}_{layer}')
        results[name] = [t.reshape(w.shape) for t in res]

    return (loss, grad_x, *[results[n][k] for k in range(4) for n in PARAM_NAMES])
```

```python
import functools
import math

import jax
import jax.numpy as jnp
from jax import lax
from jax.experimental import pallas as pl
from jax.experimental.pallas import tpu as pltpu

F32 = jnp.float32
MXU_DTYPE = jnp.bfloat16
ACT_DTYPE = jnp.bfloat16

HEAD_DIM = 64
N_MAIN_HEADS = 12
N_MEM_HEADS = 4
MAIN_WIDTH = N_MAIN_HEADS * HEAD_DIM
MEM_WIDTH = N_MEM_HEADS * HEAD_DIM
EPS = 1e-6
SCALE = HEAD_DIM ** -0.5
NEG_BIG = -1e30
LANES = 128
PACK_COLS = 1024
N_CHIPS = 4

ADAM_LR = 0.001
ADAM_B1 = 0.9
ADAM_B2 = 0.999
ADAM_EPS = 1e-08
ADAM_WD = 0.01
ADAM_STEP = 10

MESH = pl.DeviceIdType.MESH
ANY = pl.BlockSpec(memory_space=pl.ANY)

PARAM_SHARD_AXIS = {
    'ln_mix_g': None, 'w_in_a': 2, 'b_f_a': None, 'w_in_b': 1, 'ln_kv_g': None, 'w_kv': 1,
    'ln_mem_g': None, 'w_memkv': 1, 'w_out': 1, 'ln_ffn_g': None, 'w_up': 2, 'conv_w': 2,
    'conv_b': None, 'w_down': 1, 'final_g': None,
}
PARAM_NAMES = list(PARAM_SHARD_AXIS)
F32_GATHERED = ('conv_w',)


def _tile(n, pref, unit=LANES):
    if n <= pref:
        return n
    best = None
    for t in range(unit, pref + 1, unit):
        if n % t == 0:
            best = t
    assert best is not None, (n, pref)
    return best


MM_ACC_ELEMS = 768 * 1024
MM_K_TILE = 2048
MM_VMEM_MB = 48


def _out_tiles(M, N):
    def divisors(n, cap):
        if n <= LANES:
            return [n]
        return [t for t in range(LANES, min(n, cap) + 1, LANES) if n % t == 0]

    best = None
    for tm in divisors(M, 1536):
        for tn in divisors(N, 2048):
            if tm * tn <= MM_ACC_ELEMS and (best is None or (tm * tn, tn) > (best[0] * best[1], best[1])):
                best = (tm, tn)
    assert best is not None, (M, N)
    return best


def _params(*sem, vmem_mb=None):
    kw = {}
    if sem:
        kw['dimension_semantics'] = sem
    if vmem_mb is not None:
        kw['vmem_limit_bytes'] = vmem_mb * 1024 * 1024
    return pltpu.CompilerParams(**kw)


def _dot(a, b, dims):
    return lax.dot_general(a.astype(MXU_DTYPE), b.astype(MXU_DTYPE), (dims, ((), ())),
                           preferred_element_type=F32)


NN = ((1,), (0,))
NT = ((1,), (1,))
TN = ((0,), (0,))


def _matmul(a, b, mode, out_dtype, res=None, slots=1, name='mm'):
    if mode == 'nn':
        (M, K), (K2, N) = a.shape, b.shape
    elif mode == 'nt':
        (M, K), (N, K2) = a.shape, b.shape
    else:
        (K, M), (K2, N) = a.shape, b.shape
    assert K == K2 and N % slots == 0, (a.shape, b.shape, mode, slots)
    slot_cols = N // slots
    tm, tn = _out_tiles(M, slot_cols)
    per_slot = slot_cols // tn
    tk = _tile(K, MM_K_TILE)
    nk = K // tk
    dims = {'nn': NN, 'nt': NT, 'tn': TN}[mode]
    a_again = a.size * a.dtype.itemsize * (N // tn)
    b_again = b.size * b.dtype.itemsize * (M // tm)
    m_inner = nk == 1 and a_again < b_again

    def body(*refs):
        if res is None:
            (a_ref, b_ref, o_ref), r_ref = refs[:3], None
        else:
            a_ref, b_ref, r_ref, o_ref = refs[:4]

        def finish(out):
            if r_ref is not None:
                out = out + r_ref[...]
            o_ref[...] = out.astype(out_dtype)

        if nk == 1:
            finish(_dot(a_ref[...], b_ref[...], dims))
            return
        acc = refs[-1]
        k = pl.program_id(2)

        @pl.when(k == 0)
        def _():
            acc[...] = jnp.zeros_like(acc)

        acc[...] += _dot(a_ref[...], b_ref[...], dims)

        @pl.when(k == nk - 1)
        def _():
            finish(acc[...])

    def spec(shape, index):
        return pl.BlockSpec(shape, (lambda j, i, k: index(i, j, k)) if m_inner else index)

    a_spec = spec((tk, tm), lambda i, j, k: (k, i)) if mode == 'tn' else spec((tm, tk), lambda i, j, k: (i, k))
    b_spec = spec((tn, tk), lambda i, j, k: (j, k)) if mode == 'nt' else spec((tk, tn), lambda i, j, k: (k, j))
    if slots == 1:
        o_spec = spec((tm, tn), lambda i, j, k: (i, j))
        out_shape = jax.ShapeDtypeStruct((M, N), out_dtype)
    else:
        assert res is None
        o_spec = spec((None, tm, tn), lambda i, j, k: (j // per_slot, i, j % per_slot))
        out_shape = jax.ShapeDtypeStruct((slots, M, slot_cols), out_dtype)
    in_specs = [a_spec, b_spec] + ([o_spec] if res is not None else [])
    args = (a, b) + ((res,) if res is not None else ())
    return pl.pallas_call(
        body, name=name, grid=(N // tn, M // tm, nk) if m_inner else (M // tm, N // tn, nk),
        in_specs=in_specs, out_specs=o_spec,
        out_shape=out_shape,
        scratch_shapes=[] if nk == 1 else [pltpu.VMEM((tm, tn), F32)],
        compiler_params=_params('parallel', 'parallel', 'arbitrary', vmem_mb=MM_VMEM_MB),
    )(*args)


def _rms_fwd(x, g, name):
    T, D = x.shape
    tr = _tile(T, 512)

    def body(x_ref, g_ref, o_ref):
        xv = x_ref[...]
        r = lax.rsqrt(jnp.mean(xv * xv, axis=-1, keepdims=True) + EPS)
        o_ref[...] = (xv * r * g_ref[...]).astype(ACT_DTYPE)

    return pl.pallas_call(
        body, name=name, grid=(T // tr,),
        in_specs=[pl.BlockSpec((tr, D), lambda i: (i, 0)), pl.BlockSpec((1, D), lambda i: (0, 0))],
        out_specs=pl.BlockSpec((tr, D), lambda i: (i, 0)),
        out_shape=jax.ShapeDtypeStruct((T, D), ACT_DTYPE),
        compiler_params=_params('parallel'),
    )(x, g)


def _rms_bwd(x, g, dh, dres, name):
    T, D = x.shape
    tr = _tile(T, 512)
    want_dx = dres is not None

    def body(*refs):
        if want_dx:
            x_ref, g_ref, dh_ref, dres_ref, dx_ref, dg_ref = refs
        else:
            x_ref, g_ref, dh_ref, dg_ref = refs
        i = pl.program_id(0)

        @pl.when(i == 0)
        def _():
            dg_ref[...] = jnp.zeros_like(dg_ref)

        xv = x_ref[...]
        dhv = dh_ref[...].astype(F32)
        r = lax.rsqrt(jnp.mean(xv * xv, axis=-1, keepdims=True) + EPS)
        n = xv * r
        dg_ref[...] += jnp.sum(dhv * n, axis=0, keepdims=True)
        if want_dx:
            dn = dhv * g_ref[...]
            dx = r * (dn - n * jnp.mean(dn * n, axis=-1, keepdims=True))
            dx_ref[...] = dres_ref[...] + dx

    row = pl.BlockSpec((tr, D), lambda i: (i, 0))
    vec = pl.BlockSpec((1, D), lambda i: (0, 0))
    if want_dx:
        return pl.pallas_call(
            body, name=name, grid=(T // tr,),
            in_specs=[row, vec, row, row], out_specs=[row, vec],
            out_shape=[jax.ShapeDtypeStruct((T, D), F32), jax.ShapeDtypeStruct((1, D), F32)],
            compiler_params=_params('arbitrary'),
        )(x, g, dh, dres)
    dg = pl.pallas_call(
        body, name=name, grid=(T // tr,),
        in_specs=[row, vec, row], out_specs=vec,
        out_shape=jax.ShapeDtypeStruct((1, D), F32),
        compiler_params=_params('arbitrary'),
    )(x, g, dh)
    return None, dg


def _final_loss(x, g, tgt, name='final_loss'):
    T, D = x.shape
    tr = _tile(T, 512)

    def body(x_ref, g_ref, t_ref, loss_ref, dx_ref, dg_ref):
        i = pl.program_id(0)

        @pl.when(i == 0)
        def _():
            loss_ref[...] = jnp.zeros_like(loss_ref)
            dg_ref[...] = jnp.zeros_like(dg_ref)

        xv = x_ref[...]
        gv = g_ref[...]
        r = lax.rsqrt(jnp.mean(xv * xv, axis=-1, keepdims=True) + EPS)
        n = xv * r
        e = n * gv - t_ref[...]
        per_tok = jnp.mean(e * e, axis=-1, keepdims=True)
        loss_ref[...] += 0.5 * jnp.sum(per_tok, axis=0, keepdims=True)
        dy = e * (1.0 / D)
        dg_ref[...] += jnp.sum(dy * n, axis=0, keepdims=True)
        dn = dy * gv
        dx_ref[...] = r * (dn - n * jnp.mean(dn * n, axis=-1, keepdims=True))

    row = pl.BlockSpec((tr, D), lambda i: (i, 0))
    vec = pl.BlockSpec((1, D), lambda i: (0, 0))
    one = pl.BlockSpec((1, 1), lambda i: (0, 0))
    return pl.pallas_call(
        body, name=name, grid=(T // tr,),
        in_specs=[row, vec, row], out_specs=[one, row, vec],
        out_shape=[jax.ShapeDtypeStruct((1, 1), F32), jax.ShapeDtypeStruct((T, D), F32),
                   jax.ShapeDtypeStruct((1, D), F32)],
        compiler_params=_params('arbitrary'),
    )(x, g, tgt)


def _log_sigmoid(z):
    return jnp.minimum(z, 0.0) - jnp.log(1.0 + jnp.exp(-jnp.abs(z)))


def _tri(n, rel):
    j = lax.broadcasted_iota(jnp.int32, (n, n), 0)
    s = lax.broadcasted_iota(jnp.int32, (n, n), 1)
    return rel(j, s).astype(MXU_DTYPE)


def _split_dot(x, tri, terms):
    if MXU_DTYPE == F32:
        return jnp.dot(x, tri, preferred_element_type=F32)
    out = None
    rem = x
    for _ in range(terms):
        piece = rem.astype(MXU_DTYPE)
        part = jnp.dot(piece, tri, preferred_element_type=F32)
        out = part if out is None else out + part
        rem = rem - piece.astype(F32)
    return out


def _gate_fwd(zt, bcol, name='gate_fwd'):
    BH, S = zt.shape
    nb = S // LANES

    def body(z_ref, b_ref, c_ref):
        tri = _tri(LANES, lambda j, s: j <= s)
        carry = jnp.zeros((BH, 1), F32)
        for i in range(nb):
            sl = slice(i * LANES, (i + 1) * LANES)
            logf = _log_sigmoid(z_ref[:, sl] + b_ref[...])
            cs = _split_dot(logf, tri, 3) + carry
            c_ref[:, sl] = cs
            carry = cs[:, LANES - 1:LANES]

    return pl.pallas_call(body, name=name, out_shape=jax.ShapeDtypeStruct((BH, S), F32))(zt, bcol)


def _gate_bwd(zt, bcol, dc, name='gate_bwd'):
    BH, S = zt.shape
    nb = S // LANES

    def body(z_ref, b_ref, dc_ref, dz_ref, db_ref):
        tri = _tri(LANES, lambda j, s: j >= s)
        carry = jnp.zeros((BH, 1), F32)
        dsum = jnp.zeros((BH, 1), F32)
        for i in reversed(range(nb)):
            sl = slice(i * LANES, (i + 1) * LANES)
            rs = _split_dot(dc_ref[:, sl], tri, 3) + carry
            carry = rs[:, 0:1]
            z = z_ref[:, sl] + b_ref[...]
            dz = rs * (1.0 - 1.0 / (1.0 + jnp.exp(-z)))
            dz_ref[:, sl] = dz
            dsum = dsum + jnp.sum(dz, axis=-1, keepdims=True)
        db_ref[...] = dsum

    return pl.pallas_call(
        body, name=name,
        out_shape=[jax.ShapeDtypeStruct((BH, S), F32), jax.ShapeDtypeStruct((BH, 1), F32)],
    )(zt, bcol, dc)


FOX_ROWS, FOX_KEYS = 256, 512


PAIR = 2 * HEAD_DIM
N_MAIN_PAIRS = N_MAIN_HEADS // 2
N_MEM_PAIRS = N_MEM_HEADS // 2


def _lane0(shape):
    return lax.broadcasted_iota(jnp.int32, shape, len(shape) - 1) < HEAD_DIM


def _per_head(x):
    first = _lane0(x.shape)
    zero = jnp.zeros_like(x)
    return jnp.where(first, x, zero), jnp.where(first, zero, x)


def _pick(first, a, b):
    return jnp.where(first, a, b)


GROUP = 2
MAIN_STEPS = N_MAIN_PAIRS // GROUP


def _lanes(p):
    return slice(p * PAIR, (p + 1) * PAIR)


def _q_spec(bq, nq, off, group=1):
    assert off % group == 0
    return pl.BlockSpec((bq, group * PAIR), lambda b, j, i: (b * nq + i, off // group + j))


def _seq_spec(S, off, group=1):
    assert off % group == 0
    return pl.BlockSpec((S, group * PAIR), lambda b, j, i: (b, off // group + j))


def _gate_specs(bq, nk, bk):
    col = pl.BlockSpec((2 * GROUP, bq, 1), lambda b, j, i: (b * MAIN_STEPS + j, i, 0))
    rowv = pl.BlockSpec((2 * GROUP, nk, 1, bk), lambda b, j, i: (b * MAIN_STEPS + j, 0, 0, 0))
    return col, rowv


def _blocks(S, rows, keys):
    bq, bk = min(rows, S), min(keys, S)
    assert bk % bq == 0 and S % bk == 0
    return bq, bk


def _diagonal(i, bq, bk, strict):
    per = bk // bq
    row = lax.broadcasted_iota(jnp.int32, (bq, bk), 0) + (i % per) * bq
    col = lax.broadcasted_iota(jnp.int32, (bq, bk), 1)
    return i // per, ((col < row) if strict else (col <= row))


def _scaled(q):
    assert math.log2(SCALE).is_integer()
    return q * jnp.asarray(SCALE, q.dtype)


def _fox_fwd(qkv, offs, B, S, ccol, crow, beside=None, name='fox_fwd'):
    bq, bk = _blocks(S, FOX_ROWS, FOX_KEYS)
    nq = S // bq

    def body(q_ref, k_ref, v_ref, cc_ref, cr_ref, o_ref, lse_ref):
        i = pl.program_id(2)
        qv = _scaled(q_ref[...])
        qh = [_per_head(qv[:, _lanes(p)]) for p in range(GROUP)]
        first = _lane0((bq, PAIR))

        def step(kb, carry, mask=None):
            m, l, acc = carry
            sl = pl.ds(pl.multiple_of(kb * bk, bk), bk)
            m_new, l_new, acc_new = [], [], []
            for p in range(GROUP):
                ks, vs = k_ref[sl, _lanes(p)], v_ref[sl, _lanes(p)]
                alpha, pv = [], []
                for h in range(2):
                    n = 2 * p + h
                    s = _dot(qh[p][h], ks, NT) + cc_ref[n] - cr_ref[n, kb]
                    if mask is not None:
                        s = jnp.where(mask, s, NEG_BIG)
                    mh = jnp.maximum(m[n], jnp.max(s, axis=-1, keepdims=True))
                    pr = jnp.exp(s - mh)
                    ah = jnp.exp(m[n] - mh)
                    m_new.append(mh)
                    alpha.append(ah)
                    l_new.append(ah * l[n] + jnp.sum(pr, axis=-1, keepdims=True))
                    pv.append(_dot(pr, vs, NN))
                acc_new.append(_pick(first, alpha[0], alpha[1]) * acc[p] + _pick(first, pv[0], pv[1]))
            return tuple(m_new), tuple(l_new), tuple(acc_new)

        negs = tuple(jnp.full((bq, 1), NEG_BIG, F32) for _ in range(2 * GROUP))
        zeros = tuple(jnp.zeros((bq, 1), F32) for _ in range(2 * GROUP))
        acc0 = tuple(jnp.zeros((bq, PAIR), F32) for _ in range(GROUP))
        last, mask = _diagonal(i, bq, bk, False)
        m, l, acc = step(last, lax.fori_loop(0, last, step, (negs, zeros, acc0)), mask)
        for p in range(GROUP):
            o_ref[:, _lanes(p)] = (acc[p] / _pick(first, l[2 * p], l[2 * p + 1])).astype(ACT_DTYPE)
        for n in range(2 * GROUP):
            lse_ref[n] = m[n] + jnp.log(l[n])

    col, rowv = _gate_specs(bq, S // bk, bk)
    return _call_beside(
        body, name, (B, MAIN_STEPS, nq),
        [_q_spec(bq, nq, offs[0], GROUP), _seq_spec(S, offs[1], GROUP), _seq_spec(S, offs[2], GROUP), col, rowv],
        [_q_spec(bq, nq, 0, GROUP), col],
        [jax.ShapeDtypeStruct((B * S, MAIN_WIDTH), ACT_DTYPE), jax.ShapeDtypeStruct((B * N_MAIN_HEADS, S, 1), F32)],
        [], (*qkv, ccol, crow), ('parallel', 'parallel', 'arbitrary'), beside)


def _fox_bwd(qkv, offs, B, S, ccol, crow, o, lse, do, beside=None, name='fox_bwd'):
    bq, bk = _blocks(S, FOX_ROWS, FOX_KEYS)
    nq = S // bq

    def body(q_ref, k_ref, v_ref, cc_ref, cr_ref, o_ref, lse_ref, do_ref,
             dq_ref, dk_ref, dv_ref, dcc_ref, dcr_ref, dk_acc, dv_acc):
        i = pl.program_id(2)

        @pl.when(i == 0)
        def _():
            dk_acc[...] = jnp.zeros_like(dk_acc)
            dv_acc[...] = jnp.zeros_like(dv_acc)
            dcr_ref[...] = jnp.zeros_like(dcr_ref)

        qv = _scaled(q_ref[...])
        dov = do_ref[...]
        qp = [qv[:, _lanes(p)] for p in range(GROUP)]
        dop = [dov[:, _lanes(p)] for p in range(GROUP)]
        qh = [_per_head(t) for t in qp]
        doh = [_per_head(t) for t in dop]
        first, first_k = _lane0((bq, PAIR)), _lane0((bk, PAIR))
        prod = dov.astype(F32) * o_ref[...].astype(F32)
        dsum = [jnp.sum(t, axis=-1, keepdims=True) for p in range(GROUP) for t in _per_head(prod[:, _lanes(p)])]

        def step(kb, carry, mask=None):
            dq, dcc = carry
            sl = pl.ds(pl.multiple_of(kb * bk, bk), bk)
            dq_new, dcc_new = [], []
            for p in range(GROUP):
                ks, vs = k_ref[sl, _lanes(p)], v_ref[sl, _lanes(p)]
                dqh, dkh, dvh = [], [], []
                for h in range(2):
                    n = 2 * p + h
                    s = _dot(qh[p][h], ks, NT) + cc_ref[n] - cr_ref[n, kb]
                    pr = jnp.exp(s - lse_ref[n])
                    if mask is not None:
                        pr = jnp.where(mask, pr, 0.0)
                    ds = pr * (_dot(doh[p][h], vs, NT) - dsum[n])
                    dqh.append(_dot(ds, ks, NN))
                    dkh.append(_dot(ds, qp[p], TN))
                    dvh.append(_dot(pr, dop[p], TN))
                    dcr_ref[n, kb] -= jnp.sum(ds, axis=0, keepdims=True)
                    dcc_new.append(dcc[n] + jnp.sum(ds, axis=-1, keepdims=True))
                dk_acc[sl, _lanes(p)] += _pick(first_k, dkh[0], dkh[1])
                dv_acc[sl, _lanes(p)] += _pick(first_k, dvh[0], dvh[1])
                dq_new.append(dq[p] + _pick(first, dqh[0], dqh[1]))
            return tuple(dq_new), tuple(dcc_new)

        zeros = tuple(jnp.zeros((bq, 1), F32) for _ in range(2 * GROUP))
        dq0 = tuple(jnp.zeros((bq, PAIR), F32) for _ in range(GROUP))
        last, mask = _diagonal(i, bq, bk, False)
        dq, dcc = step(last, lax.fori_loop(0, last, step, (dq0, zeros)), mask)
        for p in range(GROUP):
            dq_ref[:, _lanes(p)] = (dq[p] * SCALE).astype(ACT_DTYPE)
        for n in range(2 * GROUP):
            dcc_ref[n] = dcc[n]

        @pl.when(i == nq - 1)
        def _():
            dk_ref[...] = dk_acc[...].astype(ACT_DTYPE)
            dv_ref[...] = dv_acc[...].astype(ACT_DTYPE)

    col, rowv = _gate_specs(bq, S // bk, bk)
    qs, seq = _q_spec(bq, nq, 0, GROUP), _seq_spec(S, 0, GROUP)
    full = jax.ShapeDtypeStruct((B * S, MAIN_WIDTH), ACT_DTYPE)
    wide = pltpu.VMEM((S, GROUP * PAIR), F32)
    return _call_beside(
        body, name, (B, MAIN_STEPS, nq),
        [_q_spec(bq, nq, offs[0], GROUP), _seq_spec(S, offs[1], GROUP), _seq_spec(S, offs[2], GROUP), col, rowv,
         qs, col, qs],
        [qs, seq, seq, col, rowv],
        [full, full, full, jax.ShapeDtypeStruct(ccol.shape, F32), jax.ShapeDtypeStruct(crow.shape, F32)],
        [wide, wide],
        (*qkv, ccol, crow, o, lse, do), ('parallel', 'parallel', 'arbitrary'), beside)


SB_ROWS, SB_KEYS = 256, 512


SB_SUM_TERMS = 2


def _sb_block(q_scaled, ks, mask):
    z = _dot(q_scaled, ks, NT)
    a = _log_sigmoid(z)
    l = a - z
    return a, (l if mask is None else jnp.where(mask, l, 0.0))


def _sb_fwd(qkv, offs, B, S, name='sb_fwd'):
    bq, bk = _blocks(S, SB_ROWS, SB_KEYS)
    nq = S // bq

    def body(q_ref, k_ref, v_ref, o_ref, tot_ref):
        i = pl.program_id(2)
        qv = _scaled(q_ref[...])
        qh = [_per_head(qv[:, _lanes(p)]) for p in range(GROUP)]
        first = _lane0((bq, PAIR))
        tri = _tri(bk, lambda j, s: j > s)

        def step(kb, carry, mask=None):
            acc, right = carry
            sl = pl.ds(pl.multiple_of(kb * bk, bk), bk)
            acc_new, right_new = [], []
            for p in range(GROUP):
                ks, vs = k_ref[sl, _lanes(p)], v_ref[sl, _lanes(p)]
                pv = []
                for h in range(2):
                    n = 2 * p + h
                    a, l = _sb_block(qh[p][h], ks, mask)
                    w = jnp.exp(a + _split_dot(l, tri, SB_SUM_TERMS) + right[n])
                    if mask is not None:
                        w = jnp.where(mask, w, 0.0)
                    pv.append(_dot(w, vs, NN))
                    right_new.append(right[n] + jnp.sum(l, axis=-1, keepdims=True))
                acc_new.append(acc[p] + _pick(first, pv[0], pv[1]))
            return tuple(acc_new), tuple(right_new)

        zeros = tuple(jnp.zeros((bq, 1), F32) for _ in range(2 * GROUP))
        acc0 = tuple(jnp.zeros((bq, PAIR), F32) for _ in range(GROUP))
        last, mask = _diagonal(i, bq, bk, True)
        carry = step(last, (acc0, zeros), mask)
        acc, total = lax.fori_loop(0, last, lambda n, c: step(last - 1 - n, c), carry)
        for p in range(GROUP):
            o_ref[:, _lanes(p)] = acc[p].astype(ACT_DTYPE)
        for n in range(2 * GROUP):
            tot_ref[n] = total[n]

    col, _ = _gate_specs(bq, S // bk, bk)
    return pl.pallas_call(
        body, name=name, grid=(B, MAIN_STEPS, nq),
        in_specs=[_q_spec(bq, nq, offs[0], GROUP), _seq_spec(S, offs[1], GROUP), _seq_spec(S, offs[2], GROUP)],
        out_specs=[_q_spec(bq, nq, 0, GROUP), col],
        out_shape=[jax.ShapeDtypeStruct((B * S, MAIN_WIDTH), ACT_DTYPE),
                   jax.ShapeDtypeStruct((B * N_MAIN_HEADS, S, 1), F32)],
        compiler_params=_params('parallel', 'parallel', 'arbitrary'),
    )(*qkv)


def _sb_bwd(qkv, offs, B, S, tot, do, name='sb_bwd'):
    bq, bk = _blocks(S, SB_ROWS, SB_KEYS)
    nq = S // bq

    def body(q_ref, k_ref, v_ref, tot_ref, do_ref, dq_ref, dk_ref, dv_ref, dk_acc, dv_acc):
        i = pl.program_id(2)

        @pl.when(i == 0)
        def _():
            dk_acc[...] = jnp.zeros_like(dk_acc)
            dv_acc[...] = jnp.zeros_like(dv_acc)

        qv = _scaled(q_ref[...])
        dov = do_ref[...]
        qp = [qv[:, _lanes(p)] for p in range(GROUP)]
        dop = [dov[:, _lanes(p)] for p in range(GROUP)]
        qh = [_per_head(t) for t in qp]
        doh = [_per_head(t) for t in dop]
        heads = [(p, h) for p in range(GROUP) for h in range(2)]
        first, first_k = _lane0((bq, PAIR)), _lane0((bk, PAIR))
        last, diagonal = _diagonal(i, bq, bk, True)
        tri_incl = _tri(bk, lambda j, s: j <= s)
        tri_excl = _tri(bk, lambda j, s: j < s)
        zeros = tuple(jnp.zeros((bq, 1), F32) for _ in heads)
        tot = tuple(tot_ref[n] for n in range(len(heads)))

        def step(kb, carry, mask=None):
            dq, rest_l, left_g = carry
            sl = pl.ds(pl.multiple_of(kb * bk, bk), bk)
            new_dq, new_l, new_g = [], [], []
            for p in range(GROUP):
                ks, vs = k_ref[sl, _lanes(p)], v_ref[sl, _lanes(p)]
                dqh, dkh, dvh = [], [], []
                for h in range(2):
                    n = 2 * p + h
                    a, l = _sb_block(qh[p][h], ks, mask)
                    w = jnp.exp(a - _split_dot(l, tri_incl, SB_SUM_TERMS) + rest_l[n])
                    if mask is not None:
                        w = jnp.where(mask, w, 0.0)
                    g = w * _dot(doh[p][h], vs, NT)
                    beta = jnp.exp(a)
                    dz = g - beta * (g + _split_dot(g, tri_excl, 1) + left_g[n])
                    if mask is not None:
                        dz = jnp.where(mask, dz, 0.0)
                    dqh.append(_dot(dz, ks, NN))
                    dkh.append(_dot(dz, qp[p], TN))
                    dvh.append(_dot(w, dop[p], TN))
                    new_l.append(rest_l[n] - jnp.sum(l, axis=-1, keepdims=True))
                    new_g.append(left_g[n] + jnp.sum(g, axis=-1, keepdims=True))
                dk_acc[sl, _lanes(p)] += _pick(first_k, dkh[0], dkh[1])
                dv_acc[sl, _lanes(p)] += _pick(first_k, dvh[0], dvh[1])
                new_dq.append(dq[p] + _pick(first, dqh[0], dqh[1]))
            return tuple(new_dq), tuple(new_l), tuple(new_g)

        dq0 = tuple(jnp.zeros((bq, PAIR), F32) for _ in range(GROUP))
        dq, _, _ = step(last, lax.fori_loop(0, last, step, (dq0, tot, zeros)), diagonal)
        for p in range(GROUP):
            dq_ref[:, _lanes(p)] = (dq[p] * SCALE).astype(ACT_DTYPE)

        @pl.when(i == nq - 1)
        def _():
            dk_ref[...] = dk_acc[...].astype(ACT_DTYPE)
            dv_ref[...] = dv_acc[...].astype(ACT_DTYPE)

    qs, seq = _q_spec(bq, nq, 0, GROUP), _seq_spec(S, 0, GROUP)
    full = jax.ShapeDtypeStruct((B * S, MAIN_WIDTH), ACT_DTYPE)
    wide = pltpu.VMEM((S, GROUP * PAIR), F32)
    col, _ = _gate_specs(bq, S // bk, bk)
    return pl.pallas_call(
        body, name=name, grid=(B, MAIN_STEPS, nq),
        in_specs=[_q_spec(bq, nq, offs[0], GROUP), _seq_spec(S, offs[1], GROUP), _seq_spec(S, offs[2], GROUP), col,
                  qs],
        out_specs=[qs, seq, seq], out_shape=[full, full, full],
        scratch_shapes=[wide, wide],
        compiler_params=_params('parallel', 'parallel', 'arbitrary'),
    )(*qkv, tot, do)


def _mem_probs(qv, mk):
    s = _dot(qv, mk, NT) * SCALE
    p = jnp.exp(s - jnp.max(s, axis=-1, keepdims=True))
    return p / jnp.sum(p, axis=-1, keepdims=True)


def _mem_fwd(q, q_off, mkv, B, S, name='mem_fwd'):
    M = mkv.shape[0] // B
    bq = _tile(S, 512)
    nq = S // bq

    def body(q_ref, mk_ref, mv_ref, o_ref):
        first = _lane0((bq, PAIR))
        mk, mv = mk_ref[...], mv_ref[...]
        out = [_dot(_mem_probs(qh, mk), mv, NN) for qh in _per_head(q_ref[...])]
        o_ref[...] = _pick(first, out[0], out[1]).astype(ACT_DTYPE)

    return pl.pallas_call(
        body, name=name, grid=(B, N_MEM_PAIRS, nq),
        in_specs=[_q_spec(bq, nq, q_off), _seq_spec(M, 0), _seq_spec(M, N_MEM_PAIRS)],
        out_specs=_q_spec(bq, nq, 0),
        out_shape=jax.ShapeDtypeStruct((B * S, MEM_WIDTH), ACT_DTYPE),
        compiler_params=_params('parallel', 'parallel', 'parallel'),
    )(q, mkv, mkv)


def _mem_bwd(q, q_off, mkv, B, S, do, do_off, name='mem_bwd'):
    M = mkv.shape[0] // B
    bq = _tile(S, 512)
    nq = S // bq

    def body(q_ref, mk_ref, mv_ref, do_ref, dq_ref, dmk_ref, dmv_ref):
        i = pl.program_id(2)

        @pl.when(i == 0)
        def _():
            dmk_ref[...] = jnp.zeros_like(dmk_ref)
            dmv_ref[...] = jnp.zeros_like(dmv_ref)

        qv = q_ref[...]
        dov = do_ref[...]
        mk, mv = mk_ref[...], mv_ref[...]
        first = _lane0((bq, PAIR))
        first_m = _lane0((M, PAIR))
        dqh, dkh, dvh = [], [], []
        for qh, doh in zip(_per_head(qv), _per_head(dov)):
            p = _mem_probs(qh, mk)
            dp = _dot(doh, mv, NT)
            ds = p * (dp - jnp.sum(p * dp, axis=-1, keepdims=True))
            dqh.append(_dot(ds, mk, NN))
            dkh.append(_dot(ds, qv, TN))
            dvh.append(_dot(p, dov, TN))
        dq_ref[...] = (SCALE * _pick(first, dqh[0], dqh[1])).astype(ACT_DTYPE)
        dmk_ref[...] += SCALE * _pick(first_m, dkh[0], dkh[1])
        dmv_ref[...] += _pick(first_m, dvh[0], dvh[1])

    mem_out = jax.ShapeDtypeStruct((B * M, MEM_WIDTH), F32)
    return pl.pallas_call(
        body, name=name, grid=(B, N_MEM_PAIRS, nq),
        in_specs=[_q_spec(bq, nq, q_off), _seq_spec(M, 0), _seq_spec(M, N_MEM_PAIRS), _q_spec(bq, nq, do_off)],
        out_specs=[_q_spec(bq, nq, 0), _seq_spec(M, 0), _seq_spec(M, 0)],
        out_shape=[jax.ShapeDtypeStruct((B * S, MEM_WIDTH), ACT_DTYPE), mem_out, mem_out],
        compiler_params=_params('parallel', 'parallel', 'arbitrary'),
    )(q, mkv, mkv, do)


HALO = 8
CONV_CHUNK = 256


def _conv_chunk(scr, start, rows, w, b):
    at = HALO + start
    return (b + w[0:1, :] * scr[at - 2:at - 2 + rows, :] + w[1:2, :] * scr[at - 1:at - 1 + rows, :]
            + w[2:3, :] * scr[at:at + rows, :])


def _fill_frames(scr, ref):
    scr[0:HALO, :] = jnp.zeros((HALO, scr.shape[1]), F32)
    scr[HALO:, :] = ref[...].astype(F32)


def _sigmoid(x):
    return 0.5 + 0.5 * jnp.tanh(0.5 * x)


def _fold8(x):
    return jnp.sum(x.reshape(x.shape[0] // 8, 8, x.shape[1]), axis=0)


def _conv_specs(S, nf):
    ug = pl.BlockSpec((None, S, LANES), lambda b, j: (b, 0, j))
    uv = pl.BlockSpec((None, S, LANES), lambda b, j: (b, 0, j + nf))
    wg = pl.BlockSpec((3, LANES), lambda b, j: (0, j))
    wv = pl.BlockSpec((3, LANES), lambda b, j: (0, j + nf))
    bg = pl.BlockSpec((1, LANES), lambda b, j: (0, j))
    bv = pl.BlockSpec((1, LANES), lambda b, j: (0, j + nf))
    return ug, uv, wg, wv, bg, bv


def _conv_fwd(u, cw, cb, name='conv_fwd'):
    B, S, F2 = u.shape
    F = F2 // 2
    nf = F // LANES

    ch = min(CONV_CHUNK, S)

    def body(ug_ref, uv_ref, wg_ref, wv_ref, bg_ref, bv_ref, y_ref, g_scr, v_scr):
        _fill_frames(g_scr, ug_ref)
        _fill_frames(v_scr, uv_ref)
        wg, wv, bg, bv = wg_ref[...], wv_ref[...], bg_ref[...], bv_ref[...]
        for start in range(0, S, ch):
            gate = _conv_chunk(g_scr, start, ch, wg, bg)
            val = _conv_chunk(v_scr, start, ch, wv, bv)
            y_ref[start:start + ch, :] = (gate * _sigmoid(gate) * val).astype(ACT_DTYPE)

    specs = _conv_specs(S, nf)
    frames = pltpu.VMEM((HALO + S, LANES), F32)
    return pl.pallas_call(
        body, name=name, grid=(B, nf), in_specs=list(specs), out_specs=specs[0],
        out_shape=jax.ShapeDtypeStruct((B, S, F), ACT_DTYPE), scratch_shapes=[frames, frames],
        compiler_params=_params('parallel', 'parallel'),
    )(u, u, cw, cw, cb, cb)


def _conv_bwd(u, cw, cb, dy, name='conv_bwd'):
    B, S, F2 = u.shape
    F = F2 // 2
    nf = F // LANES

    ch = min(CONV_CHUNK, S)

    def body(ug_ref, uv_ref, wg_ref, wv_ref, bg_ref, bv_ref, dy_ref,
             dug_ref, duv_ref, dwg_ref, dwv_ref, dbg_ref, dbv_ref, g_scr, v_scr, dg_scr, dv_scr):
        b = pl.program_id(1)

        @pl.when(b == 0)
        def _():
            for r in (dwg_ref, dwv_ref, dbg_ref, dbv_ref):
                r[...] = jnp.zeros_like(r)

        _fill_frames(g_scr, ug_ref)
        _fill_frames(v_scr, uv_ref)
        wg, wv, bg, bv = wg_ref[...], wv_ref[...], bg_ref[...], bv_ref[...]
        for scr in (dg_scr, dv_scr):
            scr[S:, :] = jnp.zeros((HALO, LANES), F32)
        for start in range(0, S, ch):
            gate = _conv_chunk(g_scr, start, ch, wg, bg)
            val = _conv_chunk(v_scr, start, ch, wv, bv)
            dyv = dy_ref[start:start + ch, :].astype(F32)
            sg = _sigmoid(gate)
            dv_scr[start:start + ch, :] = dyv * (gate * sg)
            dg_scr[start:start + ch, :] = dyv * val * (sg * (1.0 + gate * (1.0 - sg)))

        for u_scr, d_scr, w, du_ref, dw_ref, db_ref in ((g_scr, dg_scr, wg, dug_ref, dwg_ref, dbg_ref),
                                                         (v_scr, dv_scr, wv, duv_ref, dwv_ref, dbv_ref)):
            sums = [jnp.zeros((8, LANES), F32) for _ in range(4)]
            for start in range(0, S, ch):
                x = u_scr[HALO + start:HALO + start + ch, :]
                d = [d_scr[start + n:start + n + ch, :] for n in range(3)]
                du_ref[start:start + ch, :] = (w[2:3, :] * d[0] + w[1:2, :] * d[1] + w[0:1, :] * d[2]).astype(ACT_DTYPE)
                sums = [sums[0] + _fold8(x * d[2]), sums[1] + _fold8(x * d[1]), sums[2] + _fold8(x * d[0]),
                        sums[3] + _fold8(d[0])]
            total = [jnp.sum(s, axis=0, keepdims=True) for s in sums]
            dw_ref[...] += jnp.concatenate(total[:3], axis=0)
            db_ref[...] += total[3]

    def swap(spec_fn):
        return lambda j, b: spec_fn(b, j)

    ug, uv, wg, wv, bg, bv = _conv_specs(S, nf)
    ins = [pl.BlockSpec(s.block_shape, swap(s.index_map)) for s in (ug, uv, wg, wv, bg, bv, ug)]
    outs = [ins[0], ins[0], ins[2], ins[2], ins[4], ins[4]]
    frames = pltpu.VMEM((HALO + S, LANES), F32)
    return pl.pallas_call(
        body, name=name, grid=(nf, B), in_specs=ins, out_specs=outs, scratch_shapes=[frames] * 4,
        out_shape=[jax.ShapeDtypeStruct((B, S, F), ACT_DTYPE), jax.ShapeDtypeStruct((B, S, F), ACT_DTYPE),
                   jax.ShapeDtypeStruct((3, F), F32), jax.ShapeDtypeStruct((3, F), F32),
                   jax.ShapeDtypeStruct((1, F), F32), jax.ShapeDtypeStruct((1, F), F32)],
        compiler_params=_params('parallel', 'arbitrary'),
    )(u, u, cw, cw, cb, cb, dy)


ADAM_BLOCK_BYTES = 512 * 1024


def _adamw(w, g, m, v, layer, earlier, name):
    L, r, c = w.shape
    tr = r
    if r * c * 4 > ADAM_BLOCK_BYTES and r % 8 == 0:
        tr = 8
        for t in range(8, r + 1, 8):
            if r % t == 0 and t * c * 4 <= ADAM_BLOCK_BYTES:
                tr = t

    def body(w_ref, g_ref, m_ref, v_ref, *rest):
        go_ref, d_ref, nm_ref, nv_ref = rest[-4:]
        gv = g_ref[...]
        nm = ADAM_B1 * m_ref[...] + (1.0 - ADAM_B1) * gv
        nv = ADAM_B2 * v_ref[...] + (1.0 - ADAM_B2) * (gv * gv)
        m_hat = nm / (1.0 - ADAM_B1 ** ADAM_STEP)
        v_hat = nv / (1.0 - ADAM_B2 ** ADAM_STEP)
        d_ref[...] = -ADAM_LR * (m_hat / (jnp.sqrt(v_hat) + ADAM_EPS) + ADAM_WD * w_ref[...])
        nm_ref[...] = nm
        nv_ref[...] = nv
        go_ref[...] = gv

    lay = pl.BlockSpec((None, tr, c), lambda i: (layer, i, 0))
    one = pl.BlockSpec((tr, c), lambda i: (i, 0))
    shp = jax.ShapeDtypeStruct((L, r, c), F32)
    in_specs = [lay, one, lay, lay]
    args = (w, g, m, v)
    aliases = {}
    if earlier is not None:
        in_specs += [ANY] * 4
        args += tuple(earlier)
        aliases = {4 + k: k for k in range(4)}
    return pl.pallas_call(
        body, name=name, grid=(r // tr,), in_specs=in_specs, out_specs=[lay] * 4, out_shape=[shp] * 4,
        input_output_aliases=aliases, compiler_params=_params('parallel'),
    )(*args)


def _my_place():
    return lax.axis_index('x'), lax.axis_index('y'), lax.axis_index('c')


def _other_chips(x, y):
    return [(1 - x, y), (x, 1 - y), (1 - x, 1 - y)]


def _remote(src, dst, send_sem, recv_sem, to):
    return pltpu.make_async_remote_copy(src_ref=src, dst_ref=dst, send_sem=send_sem, recv_sem=recv_sem,
                                        device_id=to, device_id_type=MESH)


def _hbm_call(body, n_in, out_shapes, scratch, name, aliases=None):
    return pl.pallas_call(body, name=name, in_specs=[ANY] * n_in, out_specs=[ANY] * len(out_shapes),
                          out_shape=out_shapes, scratch_shapes=scratch, input_output_aliases=aliases or {})


def _full_shape(shard_shape, kind):
    L, r, c = shard_shape
    return {'rows': (L, N_CHIPS * r, c), 'cols': (L, r, N_CHIPS * c), 'stack': (N_CHIPS * L, r, c)}[kind]


def _place_block(w, kind, out_dtype, chip_arr, name):
    L, r, c = w.shape
    tr = r if r % 16 else _tile(r, max(16, SUM_BLOCK_BYTES // (4 * c)), 16)
    nrt = r // tr

    def body(k_ref, w_ref, o_ref):
        o_ref[...] = w_ref[...].astype(out_dtype)

    out_map = {'rows': lambda l, i, k_ref: (l, k_ref[0] * nrt + i, 0),
               'cols': lambda l, i, k_ref: (l, i, k_ref[0]),
               'stack': lambda l, i, k_ref: (k_ref[0] * L + l, i, 0)}[kind]
    gs = pltpu.PrefetchScalarGridSpec(
        num_scalar_prefetch=1, grid=(L, nrt),
        in_specs=[pl.BlockSpec((None, tr, c), lambda l, i, k_ref: (l, i, 0))],
        out_specs=pl.BlockSpec((None, tr, c), out_map))
    return pl.pallas_call(
        body, name=name, grid_spec=gs, out_shape=jax.ShapeDtypeStruct(_full_shape(w.shape, kind), out_dtype),
        compiler_params=_params('parallel', 'parallel'),
    )(chip_arr, w)


class _Exchange:
    def __init__(self, inputs, out_shapes, aliases, scratch, start, finish):
        self.inputs, self.out_shapes, self.aliases, self.scratch = list(inputs), list(out_shapes), aliases, scratch
        self.start, self.finish = start, finish


def _run_exchange(ex, name):
    n_in, n_out = len(ex.inputs), len(ex.out_shapes)

    def body(*refs):
        parts = refs[:n_in], refs[n_in:n_in + n_out], refs[n_in + n_out:]
        ex.start(*parts)
        ex.finish(*parts)

    return _hbm_call(body, n_in, ex.out_shapes, ex.scratch, name, aliases=ex.aliases)(*ex.inputs)


def _call_beside(body, name, grid, in_specs, out_specs, out_shape, scratch, args, semantics, beside):
    if beside is None:
        outs = pl.pallas_call(body, name=name, grid=grid, in_specs=in_specs, out_specs=out_specs, out_shape=out_shape,
                              scratch_shapes=scratch, compiler_params=_params(*semantics))(*args)
        return outs, None
    n_in, n_out, n_scr = len(in_specs), len(out_specs), len(scratch)
    b_in, b_out = len(beside.inputs), len(beside.out_shapes)

    def carrier(*refs):
        cuts = [n_in, b_in, n_out, b_out, n_scr]
        parts, at = [], 0
        for size in cuts:
            parts.append(refs[at:at + size])
            at += size
        ins, ex_ins, outs, ex_outs, scr = parts
        ex_scr = refs[at:]
        ids = [pl.program_id(d) for d in range(len(grid))]
        first = functools.reduce(jnp.logical_and, [i == 0 for i in ids])
        last = functools.reduce(jnp.logical_and, [i == g - 1 for i, g in zip(ids, grid)])

        @pl.when(first)
        def _():
            beside.start(ex_ins, ex_outs, ex_scr)

        body(*ins, *outs, *scr)

        @pl.when(last)
        def _():
            beside.finish(ex_ins, ex_outs, ex_scr)

    res = pl.pallas_call(
        carrier, name=name, grid=grid, in_specs=list(in_specs) + [ANY] * b_in,
        out_specs=list(out_specs) + [ANY] * b_out, out_shape=list(out_shape) + beside.out_shapes,
        scratch_shapes=list(scratch) + beside.scratch,
        input_output_aliases={n_in + i: n_out + o for i, o in beside.aliases.items()},
        compiler_params=_params(*['arbitrary'] * len(grid)),
    )(*args, *beside.inputs)
    return res[:n_out], res[n_out:]


def _gather_exchange(fulls, shard_shapes, kinds, split):
    n = len(fulls)

    def plan(outs, send_sems, recv_sems):
        x, y, c = _my_place()
        chip = 2 * x + y
        sibling = (x, y, 1 - c)
        others = _other_chips(x, y)

        def window(a, k, half):
            L, r, cols = shard_shapes[a]
            first, count = (0, r) if half is None else (half * (r // 2), r // 2)
            if kinds[a] == 'rows':
                return outs[a].at[:, pl.ds(k * r + first, count), :]
            if kinds[a] == 'cols':
                return outs[a].at[:, pl.ds(first, count), pl.ds(pl.multiple_of(k * cols, LANES), cols)]
            return outs[a].at[pl.ds(k * L, L), pl.ds(first, count), :]

        sends, arrivals, forwards, forwarded = [], [], [], []
        for a in range(n):
            half = c if split[a] else None
            for j, (ox, oy) in enumerate(others):
                sems = (send_sems.at[6 * a + j], recv_sems.at[6 * a + j], (ox, oy, c))
                sends.append(_remote(window(a, chip, half), window(a, chip, half), *sems))
                got = window(a, 2 * ox + oy, half)
                arrivals.append(_remote(got, got, *sems))
                if split[a]:
                    sems = (send_sems.at[6 * a + 3 + j], recv_sems.at[6 * a + 3 + j], sibling)
                    forwards.append(_remote(got, got, *sems))
                    theirs = window(a, 2 * ox + oy, 1 - c)
                    forwarded.append(_remote(theirs, theirs, *sems))
                else:
                    forwards.append(None)
        return sends, arrivals, forwards, forwarded

    def start(ins, outs, scratch):
        sends, _, _, _ = plan(outs, *scratch)
        for cp in sends:
            cp.start()

    def finish(ins, outs, scratch):
        sends, arrivals, forwards, forwarded = plan(outs, *scratch)
        for arrived, fw in zip(arrivals, forwards):
            arrived.wait_recv()
            if fw is not None:
                fw.start()
        for cp in forwarded:
            cp.wait_recv()
        for cp in sends + [fw for fw in forwards if fw is not None]:
            cp.wait_send()

    scratch = [pltpu.SemaphoreType.DMA((6 * n,)), pltpu.SemaphoreType.DMA((6 * n,))]
    out_shapes = [jax.ShapeDtypeStruct(f.shape, f.dtype) for f in fulls]
    return _Exchange(fulls, out_shapes, {a: a for a in range(n)}, scratch, start, finish)


def _swap_cores(gs, name='swap_cores'):
    n = len(gs)
    out_shapes = [jax.ShapeDtypeStruct((g.shape[0], g.shape[1] // 2, g.shape[2]), g.dtype) for g in gs]

    def body(*refs):
        ins, outs = refs[:n], refs[n:2 * n]
        send_sems, recv_sems = refs[2 * n:]
        x, y, c = _my_place()
        cps = []
        for a in range(n):
            rh = gs[a].shape[1] // 2
            cp = _remote(ins[a].at[:, pl.ds((1 - c) * rh, rh), :], outs[a], send_sems.at[a], recv_sems.at[a],
                         (x, y, 1 - c))
            cp.start()
            cps.append(cp)
        for cp in cps:
            cp.wait()

    scratch = [pltpu.SemaphoreType.DMA((n,)), pltpu.SemaphoreType.DMA((n,))]
    return _hbm_call(body, n, out_shapes, scratch, name)(*gs)


SUM_BLOCK_BYTES = 2 * 1024 * 1024


def _sum_rows(rh, cols):
    return _tile(rh, max(16, SUM_BLOCK_BYTES // (4 * cols)), 16)


def _add_cores(g, other, c_arr, wire_dtype, name):
    n, r, cols = g.shape
    rh = r // 2
    tr = _sum_rows(rh, cols)
    nrt = rh // tr

    def body(c_ref, g_ref, o_ref, q_ref):
        q_ref[...] = (g_ref[...] + o_ref[...]).astype(wire_dtype)

    gs = pltpu.PrefetchScalarGridSpec(
        num_scalar_prefetch=1, grid=(n, nrt),
        in_specs=[pl.BlockSpec((None, tr, cols), lambda j, i, c_ref: (j, c_ref[0] * nrt + i, 0)),
                  pl.BlockSpec((None, tr, cols), lambda j, i, c_ref: (j, i, 0))],
        out_specs=pl.BlockSpec((None, tr, cols), lambda j, i, c_ref: (j, i, 0)))
    return pl.pallas_call(
        body, name=name, grid_spec=gs, out_shape=jax.ShapeDtypeStruct((n, rh, cols), wire_dtype),
        compiler_params=_params('parallel', 'parallel'),
    )(c_arr, g, other)


def _send_exchange(qs):
    n = len(qs)

    def plan(ins, outs, send_sems, recv_sems):
        x, y, c = _my_place()
        return [_remote(ins[a].at[2 * ox + oy], outs[a].at[j], send_sems.at[3 * a + j], recv_sems.at[3 * a + j],
                        (ox, oy, c))
                for a in range(n) for j, (ox, oy) in enumerate(_other_chips(x, y))]

    def start(ins, outs, scratch):
        for cp in plan(ins, outs, *scratch):
            cp.start()

    def finish(ins, outs, scratch):
        cps = plan(ins, outs, *scratch)
        for cp in cps:
            cp.wait_recv()
        for cp in cps:
            cp.wait_send()

    scratch = [pltpu.SemaphoreType.DMA((3 * n,)), pltpu.SemaphoreType.DMA((3 * n,))]
    out_shapes = [jax.ShapeDtypeStruct((3,) + q.shape[1:], q.dtype) for q in qs]
    return _Exchange(qs, out_shapes, {}, scratch, start, finish)


def _sum_chips(q, got, place_arr, name):
    n, rh, cols = q.shape
    tr = _sum_rows(rh, cols)

    def body(p_ref, q_ref, gx_ref, gy_ref, gxy_ref, o_ref):
        f = lambda r: r[...].astype(F32)
        o_ref[...] = (f(q_ref) + f(gxy_ref)) + (f(gx_ref) + f(gy_ref))

    def got_spec(j):
        return pl.BlockSpec((None, tr, cols), lambda i, p_ref: (j, i, 0))

    gs = pltpu.PrefetchScalarGridSpec(
        num_scalar_prefetch=1, grid=(rh // tr,),
        in_specs=[pl.BlockSpec((None, tr, cols), lambda i, p_ref: (p_ref[0], i, 0)),
                  got_spec(0), got_spec(1), got_spec(2)],
        out_specs=pl.BlockSpec((None, tr, cols), lambda i, p_ref: (p_ref[1], i, 0)))
    return pl.pallas_call(
        body, name=name, grid_spec=gs, out_shape=jax.ShapeDtypeStruct((2, rh, cols), F32),
        compiler_params=_params('parallel'),
    )(place_arr, q, got, got, got)


def _join_cores(rs, name='join_cores'):
    n = len(rs)
    out_shapes = [jax.ShapeDtypeStruct(r.shape, r.dtype) for r in rs]

    def body(*refs):
        outs = refs[n:2 * n]
        send_sems, recv_sems = refs[2 * n:]
        x, y, c = _my_place()
        cps = []
        for a in range(n):
            cp = _remote(outs[a].at[c], outs[a].at[c], send_sems.at[a], recv_sems.at[a], (x, y, 1 - c))
            cp.start()
            cps.append(cp)
        for cp in cps:
            cp.wait()

    scratch = [pltpu.SemaphoreType.DMA((n,)), pltpu.SemaphoreType.DMA((n,))]
    return _hbm_call(body, n, out_shapes, scratch, name, aliases={a: a for a in range(n)})(*rs)


def _gate_rows(t, B, S):
    return t.reshape(B, S, N_MAIN_HEADS).transpose(0, 2, 1).reshape(B * N_MAIN_HEADS, S)


def _gate_cols(t, B, S):
    return t.reshape(B, N_MAIN_HEADS, S).transpose(0, 2, 1).reshape(B * S, N_MAIN_HEADS)


def _mem_kv_fwd(mem2, g, w, tag):
    hm = _rms_fwd(mem2, g, name=f'rms_mem_{tag}')
    mkv = _matmul(hm, w, 'nn', ACT_DTYPE, name=f'mm_memkv_{tag}')
    return hm, mkv


def _mem_kv_bwd(mem2, g, w, hm, dmk, dmv, tag):
    dmkv = jnp.concatenate([dmk, dmv], axis=1)
    dw = _matmul(hm, dmkv, 'tn', F32, name=f'mm_memkv_dw_{tag}')
    dhm = _matmul(dmkv, w, 'nt', F32, name=f'mm_memkv_dx_{tag}')
    _, dg = _rms_bwd(mem2, g, dhm, None, name=f'rms_mem_bwd_{tag}')
    return dw, dg


def _ffn_fwd(x, g, w_up, cw, cb, w_down, B, S, tag):
    T = x.shape[0]
    h2 = _rms_fwd(x, g, name=f'rms_ffn_{tag}')
    u = _matmul(h2, w_up, 'nn', ACT_DTYPE, name=f'mm_up_{tag}')
    y = _conv_fwd(u.reshape(B, S, -1), cw, cb, name=f'conv_fwd_{tag}').reshape(T, -1)
    x2 = _matmul(y, w_down, 'nn', F32, res=x, name=f'mm_down_{tag}')
    return x2, (h2, u, y)


def _ffn_bwd(dx2, x, g, w_up, cw, cb, w_down, saved, B, S, tag):
    h2, u, y = saved
    T = x.shape[0]
    dy = _matmul(dx2, w_down, 'nt', ACT_DTYPE, name=f'mm_down_dx_{tag}')
    dw_down = _matmul(y, dx2, 'tn', F32, name=f'mm_down_dw_{tag}')
    dug, duv, dcwg, dcwv, dcbg, dcbv = _conv_bwd(u.reshape(B, S, -1), cw, cb, dy.reshape(B, S, -1),
                                                  name=f'conv_bwd_{tag}')
    du = jnp.concatenate([dug.reshape(T, -1), duv.reshape(T, -1)], axis=1)
    dh2 = _matmul(du, w_up, 'nt', F32, name=f'mm_up_dx_{tag}')
    dw_up = _matmul(h2, du, 'tn', F32, slots=N_CHIPS, name=f'mm_up_dw_{tag}')
    dx, dg = _rms_bwd(x, g, dh2, dx2, name=f'rms_ffn_bwd_{tag}')
    dcw = jnp.concatenate([dcwg, dcwv], axis=1)
    dcb = jnp.concatenate([dcbg, dcbv], axis=1)
    return dx, dg, dw_up, dcw, dcb, dw_down


def _step(x, mem, tgt, W, late_weights=None, reduce_early=None):
    B, S, D = x.shape
    T = B * S
    x0 = x.reshape(T, D)
    mem2 = mem.reshape(-1, D)
    tgt2 = tgt.reshape(T, D)
    row = lambda v: v.reshape(1, -1)
    q3 = 3 * MAIN_WIDTH

    w_in_a = W['w_in_a'][0]
    wa_main = jnp.concatenate([w_in_a[:, :q3], w_in_a[:, q3 + N_MAIN_HEADS:]], axis=1)
    wa_gate = jnp.pad(w_in_a[:, q3:q3 + N_MAIN_HEADS], ((0, 0), (0, LANES - N_MAIN_HEADS)))
    bcol = jnp.tile(W['b_f_a'][0], B).reshape(B * N_MAIN_HEADS, 1)
    nkb = S // _blocks(S, FOX_ROWS, FOX_KEYS)[1]

    h1a = _rms_fwd(x0, row(W['ln_mix_g'][0]), name='rms_mix_a')
    pa = _matmul(h1a, wa_main, 'nn', ACT_DTYPE, name='mm_in_a')
    flog = _matmul(h1a, wa_gate, 'nn', F32, name='mm_gate_a')
    qkv_a = (pa, pa, pa)
    offs_a = (0, N_MAIN_PAIRS, 2 * N_MAIN_PAIRS)
    qm_off_a = 3 * N_MAIN_PAIRS
    zt = _gate_rows(flog[:, :N_MAIN_HEADS], B, S)
    cum = _gate_fwd(zt, bcol)
    ccol = cum.reshape(B * N_MAIN_HEADS, S, 1)
    crow = cum.reshape(B * N_MAIN_HEADS, nkb, 1, S // nkb)
    (oa, lse), late = _fox_fwd(qkv_a, offs_a, B, S, ccol, crow, beside=late_weights[0] if late_weights else None)
    if late_weights:
        W = {**W, **late_weights[1](late)}
    w_in_b = W['w_in_b'][0]
    hma, mkva = _mem_kv_fwd(mem2, row(W['ln_mem_g'][0]), W['w_memkv'][0], 'a')
    oma = _mem_fwd(pa, qm_off_a, mkva, B, S, name='mem_fwd_a')
    ocat_a = jnp.concatenate([oa, oma], axis=1)
    x1 = _matmul(ocat_a, W['w_out'][0], 'nn', F32, res=x0, name='mm_out_a')
    x2, ffn_a = _ffn_fwd(x1, row(W['ln_ffn_g'][0]), W['w_up'][0], W['conv_w'][0], row(W['conv_b'][0]),
                         W['w_down'][0], B, S, 'a')
    hkv = _rms_fwd(x2, row(W['ln_kv_g']), name='rms_kv')
    kvs = _matmul(hkv, W['w_kv'], 'nn', ACT_DTYPE, name='mm_kv')
    h1b = _rms_fwd(x2, row(W['ln_mix_g'][1]), name='rms_mix_b')
    pb = _matmul(h1b, w_in_b, 'nn', ACT_DTYPE, name='mm_in_b')
    qkv_b = (pb, kvs, kvs)
    offs_b = (0, 0, N_MAIN_PAIRS)
    qm_off_b = N_MAIN_PAIRS
    ob, tot_b = _sb_fwd(qkv_b, offs_b, B, S)
    hmb, mkvb = _mem_kv_fwd(mem2, row(W['ln_mem_g'][1]), W['w_memkv'][1], 'b')
    omb = _mem_fwd(pb, qm_off_b, mkvb, B, S, name='mem_fwd_b')
    ocat_b = jnp.concatenate([ob, omb], axis=1)
    x3 = _matmul(ocat_b, W['w_out'][1], 'nn', F32, res=x2, name='mm_out_b')
    x4, ffn_b = _ffn_fwd(x3, row(W['ln_ffn_g'][1]), W['w_up'][1], W['conv_w'][1], row(W['conv_b'][1]),
                         W['w_down'][1], B, S, 'b')
    loss, dx4, d_final_g = _final_loss(x4, row(W['final_g']), tgt2)

    dx3, dg_ffn_b, dw_up_b, dcw_b, dcb_b, dw_down_b = _ffn_bwd(
        dx4, x3, row(W['ln_ffn_g'][1]), W['w_up'][1], W['conv_w'][1], row(W['conv_b'][1]), W['w_down'][1],
        ffn_b, B, S, 'b')
    docat = _matmul(dx3, W['w_out'][1], 'nt', ACT_DTYPE, name='mm_out_dx_b')
    dw_out_b = _matmul(ocat_b, dx3, 'tn', F32, name='mm_out_dw_b')
    dqb, dkb, dvb = _sb_bwd(qkv_b, offs_b, B, S, tot_b, docat)
    dqmb, dmkb, dmvb = _mem_bwd(pb, qm_off_b, mkvb, B, S, docat, N_MAIN_PAIRS, name='mem_bwd_b')
    dw_memkv_b, dg_mem_b = _mem_kv_bwd(mem2, row(W['ln_mem_g'][1]), W['w_memkv'][1], hmb, dmkb, dmvb, 'b')
    dpb = jnp.concatenate([dqb, dqmb], axis=1)
    dh1b = _matmul(dpb, w_in_b, 'nt', F32, name='mm_in_dx_b')
    dw_in_b = _matmul(h1b, dpb, 'tn', F32, name='mm_in_dw_b')
    dx2, dg_mix_b = _rms_bwd(x2, row(W['ln_mix_g'][1]), dh1b, dx3, name='rms_mix_bwd_b')
    dkvs = jnp.concatenate([dkb, dvb], axis=1)
    dhkv = _matmul(dkvs, W['w_kv'], 'nt', F32, name='mm_kv_dx')
    dw_kv = _matmul(hkv, dkvs, 'tn', F32, slots=N_CHIPS, name='mm_kv_dw')
    dx2, dg_kv = _rms_bwd(x2, row(W['ln_kv_g']), dhkv, dx2, name='rms_kv_bwd')

    dx1, dg_ffn_a, dw_up_a, dcw_a, dcb_a, dw_down_a = _ffn_bwd(
        dx2, x1, row(W['ln_ffn_g'][0]), W['w_up'][0], W['conv_w'][0], row(W['conv_b'][0]), W['w_down'][0],
        ffn_a, B, S, 'a')
    docat = _matmul(dx1, W['w_out'][0], 'nt', ACT_DTYPE, name='mm_out_dx_a')
    dw_out_a = _matmul(ocat_a, dx1, 'tn', F32, name='mm_out_dw_a')

    def by_rows(dw):
        return dw.reshape(N_CHIPS, dw.shape[0] // N_CHIPS, dw.shape[1])

    grads = {
        'w_in_b': [by_rows(dw_in_b)],
        'w_kv': [dw_kv],
        'w_out': [by_rows(dw_out_a), by_rows(dw_out_b)],
        'w_up': [dw_up_a, dw_up_b],
        'w_down': [by_rows(dw_down_a), by_rows(dw_down_b)],
    }
    early = [(n, layer, g) for n, gs in grads.items() for layer, g in enumerate(gs)]
    early.append(('w_memkv', 1, by_rows(dw_memkv_b)))
    beside = reduce_early(early) if reduce_early else None
    (dqa, dka, dva, dccol, dcrow), crossed = _fox_bwd(qkv_a, offs_a, B, S, ccol, crow, oa, lse, docat, beside=beside)
    dzt, dbrow = _gate_bwd(zt, bcol, dccol.reshape(B * N_MAIN_HEADS, S) + dcrow.reshape(B * N_MAIN_HEADS, S))
    dqma, dmka, dmva = _mem_bwd(pa, qm_off_a, mkva, B, S, docat, N_MAIN_PAIRS, name='mem_bwd_a')
    dw_memkv_a, dg_mem_a = _mem_kv_bwd(mem2, row(W['ln_mem_g'][0]), W['w_memkv'][0], hma, dmka, dmva, 'a')
    dpa = jnp.concatenate([dqa, dka, dva, dqma], axis=1)
    dflog = jnp.pad(_gate_cols(dzt, B, S), ((0, 0), (0, LANES - N_MAIN_HEADS)))
    dh1a = _matmul(dpa, wa_main, 'nt', F32, name='mm_in_dx_a')
    dh1a = _matmul(dflog, wa_gate, 'nt', F32, res=dh1a, name='mm_gate_dx_a')
    dwa_main = _matmul(h1a, dpa, 'tn', F32, name='mm_in_dw_a')
    dwa_gate = _matmul(h1a, dflog, 'tn', F32, name='mm_gate_dw_a')
    dx0, dg_mix_a = _rms_bwd(x0, row(W['ln_mix_g'][0]), dh1a, dx1, name='rms_mix_bwd_a')

    dw_in_a = jnp.concatenate([dwa_main[:, :q3], dwa_gate[:, :N_MAIN_HEADS], dwa_main[:, q3:]], axis=1)
    grads.update({
        'ln_mix_g': jnp.concatenate([dg_mix_a, dg_mix_b], axis=0),
        'w_in_a': dw_in_a[None],
        'b_f_a': dbrow.reshape(B, N_MAIN_HEADS).sum(axis=0)[None],
        'ln_kv_g': dg_kv[0],
        'ln_mem_g': jnp.concatenate([dg_mem_a, dg_mem_b], axis=0),
        'w_memkv': [by_rows(dw_memkv_a), early[-1][2]],
        'ln_ffn_g': jnp.concatenate([dg_ffn_a, dg_ffn_b], axis=0),
        'conv_w': jnp.stack([dcw_a, dcw_b]),
        'conv_b': jnp.concatenate([dcb_a, dcb_b], axis=0),
        'final_g': d_final_g[0],
    })
    return loss, dx0.reshape(B, S, D), grads, crossed


BLOCKED = ('w_in_b', 'w_kv', 'w_memkv', 'w_out', 'w_up', 'w_down')
MISC_ROWS = 32


def _misc_names():
    return [n for n in PARAM_NAMES if PARAM_SHARD_AXIS[n] is None] + ['conv_w']


def _reduce_begin(arrays, wire, tag):
    _, _, c = _my_place()
    c_arr = jnp.reshape(c, (1,)).astype(jnp.int32)
    others = _swap_cores(arrays, name=f'swap_cores_{tag}')
    return [_add_cores(g, o, c_arr, wire[i], name=f'add_cores_{tag}_{i}')
            for i, (g, o) in enumerate(zip(arrays, others))]


def _reduce_end(qs, crossed, tag):
    x, y, c = _my_place()
    place_arr = jnp.stack([2 * x + y, c]).astype(jnp.int32)
    sums = [_sum_chips(q, g, place_arr, name=f'sum_chips_{tag}_{i}') for i, (q, g) in enumerate(zip(qs, crossed))]
    return [j.reshape(-1, j.shape[-1]) for j in _join_cores(sums, name=f'join_cores_{tag}')]


def _pack_late(grads, shards):
    a_cols = shards['w_in_a'].shape[2]
    a_pad = -(-a_cols // LANES) * LANES
    dw_in_a = grads['w_in_a'][0]
    in_a = jnp.stack([jnp.pad(dw_in_a[:, k * a_cols:(k + 1) * a_cols], ((0, 0), (0, a_pad - a_cols)))
                      for k in range(N_CHIPS)])
    conv_cols = shards['conv_w'].shape[2]
    misc = []
    for k in range(N_CHIPS):
        parts = [grads[n].reshape(-1) for n in _misc_names()[:-1]]
        parts.append(grads['conv_w'][:, :, k * conv_cols:(k + 1) * conv_cols].reshape(-1))
        flat = jnp.concatenate(parts)
        assert flat.shape[0] <= MISC_ROWS * PACK_COLS
        misc.append(jnp.pad(flat, (0, MISC_ROWS * PACK_COLS - flat.shape[0])).reshape(MISC_ROWS, PACK_COLS))
    return in_a, jnp.stack(misc)


def _unpack_misc(rows, shards):
    flat = rows.reshape(-1)
    out, off = {}, 0
    for name in _misc_names():
        shape = shards[name].shape
        size = math.prod(shape)
        out[name] = flat[off:off + size].reshape(-1, shape[-1])
        off += size
    return out


def kernel(x, mem, ln_mix_g, w_in_a, b_f_a, w_in_b, ln_kv_g, w_kv, ln_mem_g, w_memkv, w_out, ln_ffn_g, w_up, conv_w, conv_b, w_down, final_g, loss_target, m_ln_mix_g, m_w_in_a, m_b_f_a, m_w_in_b, m_ln_kv_g, m_w_kv, m_ln_mem_g, m_w_memkv, m_w_out, m_ln_ffn_g, m_w_up, m_conv_w, m_conv_b, m_w_down, m_final_g, v_ln_mix_g, v_w_in_a, v_b_f_a, v_w_in_b, v_ln_kv_g, v_w_kv, v_ln_mem_g, v_w_memkv, v_w_out, v_ln_ffn_g, v_w_up, v_conv_w, v_conv_b, v_w_down, v_final_g):
    shards = dict(ln_mix_g=ln_mix_g, w_in_a=w_in_a, b_f_a=b_f_a, w_in_b=w_in_b, ln_kv_g=ln_kv_g, w_kv=w_kv,
                  ln_mem_g=ln_mem_g, w_memkv=w_memkv, w_out=w_out, ln_ffn_g=ln_ffn_g, w_up=w_up, conv_w=conv_w,
                  conv_b=conv_b, w_down=w_down, final_g=final_g)
    moments_m = dict(ln_mix_g=m_ln_mix_g, w_in_a=m_w_in_a, b_f_a=m_b_f_a, w_in_b=m_w_in_b, ln_kv_g=m_ln_kv_g,
                     w_kv=m_w_kv, ln_mem_g=m_ln_mem_g, w_memkv=m_w_memkv, w_out=m_w_out, ln_ffn_g=m_ln_ffn_g,
                     w_up=m_w_up, conv_w=m_conv_w, conv_b=m_conv_b, w_down=m_w_down, final_g=m_final_g)
    moments_v = dict(ln_mix_g=v_ln_mix_g, w_in_a=v_w_in_a, b_f_a=v_b_f_a, w_in_b=v_w_in_b, ln_kv_g=v_ln_kv_g,
                     w_kv=v_w_kv, ln_mem_g=v_ln_mem_g, w_memkv=v_w_memkv, w_out=v_w_out, ln_ffn_g=v_ln_ffn_g,
                     w_up=v_w_up, conv_w=v_conv_w, conv_b=v_conv_b, w_down=v_w_down, final_g=v_final_g)

    kinds = {'w_in_a': 'stack', 'w_in_b': 'rows', 'w_kv': 'cols', 'w_memkv': 'rows', 'w_out': 'rows', 'w_up': 'cols',
             'w_down': 'rows', 'conv_w': 'cols'}
    mx, my, _ = _my_place()
    chip_arr = jnp.reshape(2 * mx + my, (1,)).astype(jnp.int32)
    placed, shard_shapes = {}, {}
    for n, kind in kinds.items():
        w = shards[n].reshape((-1,) + shards[n].shape[-2:])
        shard_shapes[n] = w.shape
        placed[n] = _place_block(w, kind, F32 if n in F32_GATHERED else jnp.bfloat16, chip_arr, name=f'place_{n}')

    def gather(names):
        return _gather_exchange([placed[n] for n in names], [shard_shapes[n] for n in names],
                                [kinds[n] for n in names], [n not in F32_GATHERED for n in names])

    def as_weights(names, full):
        out = dict(zip(names, full))
        if 'w_in_a' in out:
            out['w_in_a'] = jnp.concatenate([out['w_in_a'][k] for k in range(N_CHIPS)], axis=1)[None]
        if 'w_kv' in out:
            out['w_kv'] = out['w_kv'][0]
        return out

    first = ['w_in_a', 'conv_w']
    late = [n for n in kinds if n not in first]
    W = {**shards, **as_weights(first, _run_exchange(gather(first), 'gather_first'))}

    early = {}

    def reduce_early(items):
        early['owners'] = [(n, layer) for n, layer, _ in items]
        early['qs'] = _reduce_begin([g for _, _, g in items], [jnp.bfloat16] * len(items), 'early')
        return _send_exchange(early['qs'])

    loss_part, grad_x, grads, crossed = _step(x, mem, loss_target, W, (gather(late), functools.partial(as_weights, late)),
                                              reduce_early)
    loss = lax.psum(loss_part[0, 0], ('x', 'y', 'c'))

    g_layers = {n: [None] * (len(grads[n]) if n in BLOCKED else 1) for n in PARAM_NAMES}
    for (n, layer), g in zip(early['owners'], _reduce_end(early['qs'], crossed, 'early')):
        g_layers[n][layer] = g
    in_a, misc = _pack_late(grads, shards)
    qs = _reduce_begin([in_a, misc, grads['w_memkv'][0]], [jnp.bfloat16, F32, jnp.bfloat16], 'late')
    in_a_sum, misc_sum, memkv_sum = _reduce_end(qs, _run_exchange(_send_exchange(qs), 'send_chips_late'), 'late')
    g_layers['w_in_a'][0] = in_a_sum[:, :shards['w_in_a'].shape[2]]
    g_layers['w_memkv'][0] = memkv_sum
    for n, g in _unpack_misc(misc_sum, shards).items():
        g_layers[n][0] = g

    results = {}
    for name in PARAM_NAMES:
        w = shards[name]
        layers = len(g_layers[name])
        as_layers = (layers, -1, w.shape[-1])
        w3, m3, v3 = (t.reshape(as_layers) for t in (w, moments_m[name], moments_v[name]))
        res = None
        for layer, g in enumerate(g_layers[name]):
            res = _adamw(w3, g, m3, v3, layer, res, name=f'adamw_{name}_{layer}')
        results[name] = [t.reshape(w.shape) for t in res]

    return (loss, grad_x, *[results[n][k] for k in range(4) for n in PARAM_NAMES])
```

```python
import functools
import math

import jax
import jax.numpy as jnp
from jax import lax
from jax.experimental import pallas as pl
from jax.experimental.pallas import tpu as pltpu

F32 = jnp.float32
MXU_DTYPE = jnp.bfloat16
ACT_DTYPE = jnp.bfloat16

HEAD_DIM = 64
N_MAIN_HEADS = 12
N_MEM_HEADS = 4
MAIN_WIDTH = N_MAIN_HEADS * HEAD_DIM
MEM_WIDTH = N_MEM_HEADS * HEAD_DIM
EPS = 1e-6
SCALE = HEAD_DIM ** -0.5
NEG_BIG = -1e30
LANES = 128
PACK_COLS = 1024
N_CHIPS = 4

ADAM_LR = 0.001
ADAM_B1 = 0.9
ADAM_B2 = 0.999
ADAM_EPS = 1e-08
ADAM_WD = 0.01
ADAM_STEP = 10

MESH = pl.DeviceIdType.MESH
ANY = pl.BlockSpec(memory_space=pl.ANY)

PARAM_SHARD_AXIS = {
    'ln_mix_g': None, 'w_in_a': 2, 'b_f_a': None, 'w_in_b': 1, 'ln_kv_g': None, 'w_kv': 1,
    'ln_mem_g': None, 'w_memkv': 1, 'w_out': 1, 'ln_ffn_g': None, 'w_up': 2, 'conv_w': 2,
    'conv_b': None, 'w_down': 1, 'final_g': None,
}
PARAM_NAMES = list(PARAM_SHARD_AXIS)
F32_GATHERED = ('conv_w',)


def _tile(n, pref, unit=LANES):
    if n <= pref:
        return n
    best = None
    for t in range(unit, pref + 1, unit):
        if n % t == 0:
            best = t
    assert best is not None, (n, pref)
    return best


MM_ACC_ELEMS = 768 * 1024
MM_VMEM_MB = 56
MM_TILE_BYTES = 42 << 20


def _out_tiles(M, N):
    def divisors(n, cap):
        if n <= LANES:
            return [n]
        return [t for t in range(LANES, min(n, cap) + 1, LANES) if n % t == 0]

    best = None
    for tm in divisors(M, 1536):
        for tn in divisors(N, 2048):
            if tm * tn <= MM_ACC_ELEMS and (best is None or (tm * tn, tn) > (best[0] * best[1], best[1])):
                best = (tm, tn)
    assert best is not None, (M, N)
    return best


def _params(*sem, vmem_mb=None):
    kw = {}
    if sem:
        kw['dimension_semantics'] = sem
    if vmem_mb is not None:
        kw['vmem_limit_bytes'] = vmem_mb * 1024 * 1024
    return pltpu.CompilerParams(**kw)


def _dot(a, b, dims):
    return lax.dot_general(a.astype(MXU_DTYPE), b.astype(MXU_DTYPE), (dims, ((), ())),
                           preferred_element_type=F32)


NN = ((1,), (0,))
NT = ((1,), (1,))
TN = ((0,), (0,))


def _matmul(a, b, mode, out_dtype, res=None, slots=1, name='mm'):
    if mode == 'nn':
        (M, K), (K2, N) = a.shape, b.shape
    elif mode == 'nt':
        (M, K), (N, K2) = a.shape, b.shape
    else:
        (K, M), (K2, N) = a.shape, b.shape
    assert K == K2 and N % slots == 0, (a.shape, b.shape, mode, slots)
    slot_cols = N // slots
    tm, tn = _out_tiles(M, slot_cols)
    per_slot = slot_cols // tn
    fixed = tm * tn * (4 + 2 * jnp.dtype(out_dtype).itemsize + (8 if res is not None else 0))
    per_k = 2 * (tm * a.dtype.itemsize + tn * b.dtype.itemsize)
    tk = _tile(K, max(LANES, (MM_TILE_BYTES - fixed) // per_k))
    nk = K // tk
    dims = {'nn': NN, 'nt': NT, 'tn': TN}[mode]
    a_again = a.size * a.dtype.itemsize * (N // tn)
    b_again = b.size * b.dtype.itemsize * (M // tm)
    m_inner = nk == 1 and a_again < b_again

    def body(*refs):
        if res is None:
            (a_ref, b_ref, o_ref), r_ref = refs[:3], None
        else:
            a_ref, b_ref, r_ref, o_ref = refs[:4]

        def finish(out):
            if r_ref is not None:
                out = out + r_ref[...]
            o_ref[...] = out.astype(out_dtype)

        if nk == 1:
            finish(_dot(a_ref[...], b_ref[...], dims))
            return
        acc = refs[-1]
        k = pl.program_id(2)

        @pl.when(k == 0)
        def _():
            acc[...] = jnp.zeros_like(acc)

        acc[...] += _dot(a_ref[...], b_ref[...], dims)

        @pl.when(k == nk - 1)
        def _():
            finish(acc[...])

    def spec(shape, index):
        return pl.BlockSpec(shape, (lambda j, i, k: index(i, j, k)) if m_inner else index)

    a_spec = spec((tk, tm), lambda i, j, k: (k, i)) if mode == 'tn' else spec((tm, tk), lambda i, j, k: (i, k))
    b_spec = spec((tn, tk), lambda i, j, k: (j, k)) if mode == 'nt' else spec((tk, tn), lambda i, j, k: (k, j))
    if slots == 1:
        o_spec = spec((tm, tn), lambda i, j, k: (i, j))
        out_shape = jax.ShapeDtypeStruct((M, N), out_dtype)
    else:
        assert res is None
        o_spec = spec((None, tm, tn), lambda i, j, k: (j // per_slot, i, j % per_slot))
        out_shape = jax.ShapeDtypeStruct((slots, M, slot_cols), out_dtype)
    in_specs = [a_spec, b_spec] + ([o_spec] if res is not None else [])
    args = (a, b) + ((res,) if res is not None else ())
    return pl.pallas_call(
        body, name=name, grid=(N // tn, M // tm, nk) if m_inner else (M // tm, N // tn, nk),
        in_specs=in_specs, out_specs=o_spec,
        out_shape=out_shape,
        scratch_shapes=[] if nk == 1 else [pltpu.VMEM((tm, tn), F32)],
        compiler_params=_params('parallel', 'parallel', 'arbitrary', vmem_mb=MM_VMEM_MB),
    )(*args)


def _rms_fwd(x, g, name):
    T, D = x.shape
    tr = _tile(T, 512)

    def body(x_ref, g_ref, o_ref):
        xv = x_ref[...]
        r = lax.rsqrt(jnp.mean(xv * xv, axis=-1, keepdims=True) + EPS)
        o_ref[...] = (xv * r * g_ref[...]).astype(ACT_DTYPE)

    return pl.pallas_call(
        body, name=name, grid=(T // tr,),
        in_specs=[pl.BlockSpec((tr, D), lambda i: (i, 0)), pl.BlockSpec((1, D), lambda i: (0, 0))],
        out_specs=pl.BlockSpec((tr, D), lambda i: (i, 0)),
        out_shape=jax.ShapeDtypeStruct((T, D), ACT_DTYPE),
        compiler_params=_params('parallel'),
    )(x, g)


def _rms_bwd(x, g, dh, dres, name):
    T, D = x.shape
    tr = _tile(T, 512)
    want_dx = dres is not None

    def body(*refs):
        if want_dx:
            x_ref, g_ref, dh_ref, dres_ref, dx_ref, dg_ref = refs
        else:
            x_ref, g_ref, dh_ref, dg_ref = refs
        i = pl.program_id(0)

        @pl.when(i == 0)
        def _():
            dg_ref[...] = jnp.zeros_like(dg_ref)

        xv = x_ref[...]
        dhv = dh_ref[...].astype(F32)
        r = lax.rsqrt(jnp.mean(xv * xv, axis=-1, keepdims=True) + EPS)
        n = xv * r
        dg_ref[...] += jnp.sum(dhv * n, axis=0, keepdims=True)
        if want_dx:
            dn = dhv * g_ref[...]
            dx = r * (dn - n * jnp.mean(dn * n, axis=-1, keepdims=True))
            dx_ref[...] = dres_ref[...] + dx

    row = pl.BlockSpec((tr, D), lambda i: (i, 0))
    vec = pl.BlockSpec((1, D), lambda i: (0, 0))
    if want_dx:
        return pl.pallas_call(
            body, name=name, grid=(T // tr,),
            in_specs=[row, vec, row, row], out_specs=[row, vec],
            out_shape=[jax.ShapeDtypeStruct((T, D), F32), jax.ShapeDtypeStruct((1, D), F32)],
            compiler_params=_params('arbitrary'),
        )(x, g, dh, dres)
    dg = pl.pallas_call(
        body, name=name, grid=(T // tr,),
        in_specs=[row, vec, row], out_specs=vec,
        out_shape=jax.ShapeDtypeStruct((1, D), F32),
        compiler_params=_params('arbitrary'),
    )(x, g, dh)
    return None, dg


def _final_loss(x, g, tgt, name='final_loss'):
    T, D = x.shape
    tr = _tile(T, 512)

    def body(x_ref, g_ref, t_ref, loss_ref, dx_ref, dg_ref):
        i = pl.program_id(0)

        @pl.when(i == 0)
        def _():
            loss_ref[...] = jnp.zeros_like(loss_ref)
            dg_ref[...] = jnp.zeros_like(dg_ref)

        xv = x_ref[...]
        gv = g_ref[...]
        r = lax.rsqrt(jnp.mean(xv * xv, axis=-1, keepdims=True) + EPS)
        n = xv * r
        e = n * gv - t_ref[...]
        per_tok = jnp.mean(e * e, axis=-1, keepdims=True)
        loss_ref[...] += 0.5 * jnp.sum(per_tok, axis=0, keepdims=True)
        dy = e * (1.0 / D)
        dg_ref[...] += jnp.sum(dy * n, axis=0, keepdims=True)
        dn = dy * gv
        dx_ref[...] = r * (dn - n * jnp.mean(dn * n, axis=-1, keepdims=True))

    row = pl.BlockSpec((tr, D), lambda i: (i, 0))
    vec = pl.BlockSpec((1, D), lambda i: (0, 0))
    one = pl.BlockSpec((1, 1), lambda i: (0, 0))
    return pl.pallas_call(
        body, name=name, grid=(T // tr,),
        in_specs=[row, vec, row], out_specs=[one, row, vec],
        out_shape=[jax.ShapeDtypeStruct((1, 1), F32), jax.ShapeDtypeStruct((T, D), F32),
                   jax.ShapeDtypeStruct((1, D), F32)],
        compiler_params=_params('arbitrary'),
    )(x, g, tgt)


def _log_sigmoid(z):
    return jnp.minimum(z, 0.0) - jnp.log(1.0 + jnp.exp(-jnp.abs(z)))


def _tri(n, rel):
    j = lax.broadcasted_iota(jnp.int32, (n, n), 0)
    s = lax.broadcasted_iota(jnp.int32, (n, n), 1)
    return rel(j, s).astype(MXU_DTYPE)


def _split_dot(x, tri, terms):
    if MXU_DTYPE == F32:
        return jnp.dot(x, tri, preferred_element_type=F32), jnp.sum(x, axis=-1, keepdims=True)
    out = taken = None
    rem = x
    for _ in range(terms):
        piece = rem.astype(MXU_DTYPE)
        back = piece.astype(F32)
        part = jnp.dot(piece, tri, preferred_element_type=F32)
        rows = jnp.sum(back, axis=-1, keepdims=True)
        out, taken = (part, rows) if out is None else (out + part, taken + rows)
        rem = rem - back
    return out, taken


def _running_sums(x, tri, terms, earlier):
    w = tri.shape[0]
    parts = [_split_dot(x[:, at:at + w], tri, terms) for at in range(0, x.shape[1], w)]
    out = []
    for n, (r, _) in enumerate(parts):
        for _, whole in (parts[:n] if earlier else parts[n + 1:]):
            r = r + whole
        out.append(r)
    return jnp.concatenate(out, axis=1)


def _gate_fwd(zt, bcol, name='gate_fwd'):
    BH, S = zt.shape
    nb = S // LANES

    def body(z_ref, b_ref, c_ref):
        tri = _tri(LANES, lambda j, s: j <= s)
        carry = jnp.zeros((BH, 1), F32)
        for i in range(nb):
            sl = slice(i * LANES, (i + 1) * LANES)
            logf = _log_sigmoid(z_ref[:, sl] + b_ref[...])
            cs = _split_dot(logf, tri, 3)[0] + carry
            c_ref[:, sl] = cs
            carry = cs[:, LANES - 1:LANES]

    return pl.pallas_call(body, name=name, out_shape=jax.ShapeDtypeStruct((BH, S), F32))(zt, bcol)


def _gate_bwd(zt, bcol, dc, name='gate_bwd'):
    BH, S = zt.shape
    nb = S // LANES

    def body(z_ref, b_ref, dc_ref, dz_ref, db_ref):
        tri = _tri(LANES, lambda j, s: j >= s)
        carry = jnp.zeros((BH, 1), F32)
        dsum = jnp.zeros((BH, 1), F32)
        for i in reversed(range(nb)):
            sl = slice(i * LANES, (i + 1) * LANES)
            rs = _split_dot(dc_ref[:, sl], tri, 3)[0] + carry
            carry = rs[:, 0:1]
            z = z_ref[:, sl] + b_ref[...]
            dz = rs * (1.0 - 1.0 / (1.0 + jnp.exp(-z)))
            dz_ref[:, sl] = dz
            dsum = dsum + jnp.sum(dz, axis=-1, keepdims=True)
        db_ref[...] = dsum

    return pl.pallas_call(
        body, name=name,
        out_shape=[jax.ShapeDtypeStruct((BH, S), F32), jax.ShapeDtypeStruct((BH, 1), F32)],
    )(zt, bcol, dc)


FOX_ROWS, FOX_KEYS = 256, 512


PAIR = 2 * HEAD_DIM
N_MAIN_PAIRS = N_MAIN_HEADS // 2
N_MEM_PAIRS = N_MEM_HEADS // 2


def _lane0(shape):
    return lax.broadcasted_iota(jnp.int32, shape, len(shape) - 1) < HEAD_DIM


def _per_head(x):
    first = _lane0(x.shape)
    zero = jnp.zeros_like(x)
    return jnp.where(first, x, zero), jnp.where(first, zero, x)


def _pick(first, a, b):
    return jnp.where(first, a, b)


GROUP = 2
MAIN_STEPS = N_MAIN_PAIRS // GROUP


def _lanes(p):
    return slice(p * PAIR, (p + 1) * PAIR)


def _q_spec(bq, nq, off, group=1):
    assert off % group == 0
    return pl.BlockSpec((bq, group * PAIR), lambda b, j, i: (b * nq + i, off // group + j))


def _seq_spec(S, off, group=1):
    assert off % group == 0
    return pl.BlockSpec((S, group * PAIR), lambda b, j, i: (b, off // group + j))


def _gate_specs(bq, nk, bk):
    col = pl.BlockSpec((2 * GROUP, bq, 1), lambda b, j, i: (b * MAIN_STEPS + j, i, 0))
    rowv = pl.BlockSpec((2 * GROUP, nk, 1, bk), lambda b, j, i: (b * MAIN_STEPS + j, 0, 0, 0))
    return col, rowv


def _blocks(S, rows, keys):
    bq, bk = min(rows, S), min(keys, S)
    assert bk % bq == 0 and S % bk == 0
    return bq, bk


def _diagonal(i, bq, bk, strict):
    per = bk // bq
    row = lax.broadcasted_iota(jnp.int32, (bq, bk), 0) + (i % per) * bq
    col = lax.broadcasted_iota(jnp.int32, (bq, bk), 1)
    return i // per, ((col < row) if strict else (col <= row))


def _scaled(q):
    assert math.log2(SCALE).is_integer()
    return q * jnp.asarray(SCALE, q.dtype)


def _fox_fwd(qkv, offs, B, S, ccol, crow, beside=None, name='fox_fwd'):
    bq, bk = _blocks(S, FOX_ROWS, FOX_KEYS)
    nq = S // bq

    def body(q_ref, k_ref, v_ref, cc_ref, cr_ref, o_ref, lse_ref):
        i = pl.program_id(2)
        qv = _scaled(q_ref[...])
        qh = [_per_head(qv[:, _lanes(p)]) for p in range(GROUP)]
        first = _lane0((bq, PAIR))

        def step(kb, carry, mask=None):
            m, l, acc = carry
            sl = pl.ds(pl.multiple_of(kb * bk, bk), bk)
            m_new, l_new, acc_new = [], [], []
            for p in range(GROUP):
                ks, vs = k_ref[sl, _lanes(p)], v_ref[sl, _lanes(p)]
                alpha, pv = [], []
                for h in range(2):
                    n = 2 * p + h
                    s = _dot(qh[p][h], ks, NT) + cc_ref[n] - cr_ref[n, kb]
                    if mask is not None:
                        s = jnp.where(mask, s, NEG_BIG)
                    mh = jnp.maximum(m[n], jnp.max(s, axis=-1, keepdims=True))
                    pr = jnp.exp(s - mh)
                    ah = jnp.exp(m[n] - mh)
                    m_new.append(mh)
                    alpha.append(ah)
                    l_new.append(ah * l[n] + jnp.sum(pr, axis=-1, keepdims=True))
                    pv.append(_dot(pr, vs, NN))
                acc_new.append(_pick(first, alpha[0], alpha[1]) * acc[p] + _pick(first, pv[0], pv[1]))
            return tuple(m_new), tuple(l_new), tuple(acc_new)

        negs = tuple(jnp.full((bq, 1), NEG_BIG, F32) for _ in range(2 * GROUP))
        zeros = tuple(jnp.zeros((bq, 1), F32) for _ in range(2 * GROUP))
        acc0 = tuple(jnp.zeros((bq, PAIR), F32) for _ in range(GROUP))
        last, mask = _diagonal(i, bq, bk, False)
        m, l, acc = step(last, lax.fori_loop(0, last, step, (negs, zeros, acc0)), mask)
        for p in range(GROUP):
            o_ref[:, _lanes(p)] = (acc[p] / _pick(first, l[2 * p], l[2 * p + 1])).astype(ACT_DTYPE)
        for n in range(2 * GROUP):
            lse_ref[n] = m[n] + jnp.log(l[n])

    col, rowv = _gate_specs(bq, S // bk, bk)
    return _call_beside(
        body, name, (B, MAIN_STEPS, nq),
        [_q_spec(bq, nq, offs[0], GROUP), _seq_spec(S, offs[1], GROUP), _seq_spec(S, offs[2], GROUP), col, rowv],
        [_q_spec(bq, nq, 0, GROUP), col],
        [jax.ShapeDtypeStruct((B * S, MAIN_WIDTH), ACT_DTYPE), jax.ShapeDtypeStruct((B * N_MAIN_HEADS, S, 1), F32)],
        [], (*qkv, ccol, crow), ('parallel', 'parallel', 'arbitrary'), beside)


def _fox_bwd(qkv, offs, B, S, ccol, crow, o, lse, do, beside=None, name='fox_bwd'):
    bq, bk = _blocks(S, FOX_ROWS, FOX_KEYS)
    nq = S // bq

    def body(q_ref, k_ref, v_ref, cc_ref, cr_ref, o_ref, lse_ref, do_ref,
             dq_ref, dk_ref, dv_ref, dcc_ref, dcr_ref, dk_acc, dv_acc):
        i = pl.program_id(2)

        @pl.when(i == 0)
        def _():
            dk_acc[...] = jnp.zeros_like(dk_acc)
            dv_acc[...] = jnp.zeros_like(dv_acc)
            dcr_ref[...] = jnp.zeros_like(dcr_ref)

        qv = _scaled(q_ref[...])
        dov = do_ref[...]
        qp = [qv[:, _lanes(p)] for p in range(GROUP)]
        dop = [dov[:, _lanes(p)] for p in range(GROUP)]
        qh = [_per_head(t) for t in qp]
        doh = [_per_head(t) for t in dop]
        first, first_k = _lane0((bq, PAIR)), _lane0((bk, PAIR))
        prod = dov.astype(F32) * o_ref[...].astype(F32)
        dsum = [jnp.sum(t, axis=-1, keepdims=True) for p in range(GROUP) for t in _per_head(prod[:, _lanes(p)])]

        def step(kb, carry, mask=None):
            dq, dcc = carry
            sl = pl.ds(pl.multiple_of(kb * bk, bk), bk)
            dq_new, dcc_new = [], []
            for p in range(GROUP):
                ks, vs = k_ref[sl, _lanes(p)], v_ref[sl, _lanes(p)]
                dqh, dkh, dvh = [], [], []
                for h in range(2):
                    n = 2 * p + h
                    s = _dot(qh[p][h], ks, NT) + cc_ref[n] - cr_ref[n, kb]
                    pr = jnp.exp(s - lse_ref[n])
                    if mask is not None:
                        pr = jnp.where(mask, pr, 0.0)
                    ds = pr * (_dot(doh[p][h], vs, NT) - dsum[n])
                    dqh.append(_dot(ds, ks, NN))
                    dkh.append(_dot(ds, qp[p], TN))
                    dvh.append(_dot(pr, dop[p], TN))
                    dcr_ref[n, kb] -= jnp.sum(ds, axis=0, keepdims=True)
                    dcc_new.append(dcc[n] + jnp.sum(ds, axis=-1, keepdims=True))
                dk_acc[sl, _lanes(p)] += _pick(first_k, dkh[0], dkh[1])
                dv_acc[sl, _lanes(p)] += _pick(first_k, dvh[0], dvh[1])
                dq_new.append(dq[p] + _pick(first, dqh[0], dqh[1]))
            return tuple(dq_new), tuple(dcc_new)

        zeros = tuple(jnp.zeros((bq, 1), F32) for _ in range(2 * GROUP))
        dq0 = tuple(jnp.zeros((bq, PAIR), F32) for _ in range(GROUP))
        last, mask = _diagonal(i, bq, bk, False)
        dq, dcc = step(last, lax.fori_loop(0, last, step, (dq0, zeros)), mask)
        for p in range(GROUP):
            dq_ref[:, _lanes(p)] = (dq[p] * SCALE).astype(ACT_DTYPE)
        for n in range(2 * GROUP):
            dcc_ref[n] = dcc[n]

        @pl.when(i == nq - 1)
        def _():
            dk_ref[...] = dk_acc[...].astype(ACT_DTYPE)
            dv_ref[...] = dv_acc[...].astype(ACT_DTYPE)

    col, rowv = _gate_specs(bq, S // bk, bk)
    qs, seq = _q_spec(bq, nq, 0, GROUP), _seq_spec(S, 0, GROUP)
    full = jax.ShapeDtypeStruct((B * S, MAIN_WIDTH), ACT_DTYPE)
    wide = pltpu.VMEM((S, GROUP * PAIR), F32)
    return _call_beside(
        body, name, (B, MAIN_STEPS, nq),
        [_q_spec(bq, nq, offs[0], GROUP), _seq_spec(S, offs[1], GROUP), _seq_spec(S, offs[2], GROUP), col, rowv,
         qs, col, qs],
        [qs, seq, seq, col, rowv],
        [full, full, full, jax.ShapeDtypeStruct(ccol.shape, F32), jax.ShapeDtypeStruct(crow.shape, F32)],
        [wide, wide],
        (*qkv, ccol, crow, o, lse, do), ('parallel', 'parallel', 'arbitrary'), beside)


SB_ROWS, SB_KEYS = 256, 512
SB_TRIANGLE = 256


SB_SUM_TERMS = 2


def _sb_block(q_scaled, ks, mask):
    z = _dot(q_scaled, ks, NT)
    a = _log_sigmoid(z)
    l = a - z
    return a, (l if mask is None else jnp.where(mask, l, 0.0))


def _sb_fwd(qkv, offs, B, S, name='sb_fwd'):
    bq, bk = _blocks(S, SB_ROWS, SB_KEYS)
    nq = S // bq

    def body(q_ref, k_ref, v_ref, o_ref, tot_ref):
        i = pl.program_id(2)
        qv = _scaled(q_ref[...])
        qh = [_per_head(qv[:, _lanes(p)]) for p in range(GROUP)]
        first = _lane0((bq, PAIR))
        tri = _tri(min(bk, SB_TRIANGLE), lambda j, s: j > s)

        def step(kb, carry, mask=None):
            acc, right = carry
            sl = pl.ds(pl.multiple_of(kb * bk, bk), bk)
            acc_new, right_new = [], []
            for p in range(GROUP):
                ks, vs = k_ref[sl, _lanes(p)], v_ref[sl, _lanes(p)]
                pv = []
                for h in range(2):
                    n = 2 * p + h
                    a, l = _sb_block(qh[p][h], ks, mask)
                    w = jnp.exp(a + _running_sums(l, tri, SB_SUM_TERMS, False) + right[n])
                    if mask is not None:
                        w = jnp.where(mask, w, 0.0)
                    pv.append(_dot(w, vs, NN))
                    right_new.append(right[n] + jnp.sum(l, axis=-1, keepdims=True))
                acc_new.append(acc[p] + _pick(first, pv[0], pv[1]))
            return tuple(acc_new), tuple(right_new)

        zeros = tuple(jnp.zeros((bq, 1), F32) for _ in range(2 * GROUP))
        acc0 = tuple(jnp.zeros((bq, PAIR), F32) for _ in range(GROUP))
        last, mask = _diagonal(i, bq, bk, True)
        carry = step(last, (acc0, zeros), mask)
        acc, total = lax.fori_loop(0, last, lambda n, c: step(last - 1 - n, c), carry)
        for p in range(GROUP):
            o_ref[:, _lanes(p)] = acc[p].astype(ACT_DTYPE)
        for n in range(2 * GROUP):
            tot_ref[n] = total[n]

    col, _ = _gate_specs(bq, S // bk, bk)
    return pl.pallas_call(
        body, name=name, grid=(B, MAIN_STEPS, nq),
        in_specs=[_q_spec(bq, nq, offs[0], GROUP), _seq_spec(S, offs[1], GROUP), _seq_spec(S, offs[2], GROUP)],
        out_specs=[_q_spec(bq, nq, 0, GROUP), col],
        out_shape=[jax.ShapeDtypeStruct((B * S, MAIN_WIDTH), ACT_DTYPE),
                   jax.ShapeDtypeStruct((B * N_MAIN_HEADS, S, 1), F32)],
        compiler_params=_params('parallel', 'parallel', 'arbitrary'),
    )(*qkv)


def _sb_bwd(qkv, offs, B, S, tot, do, name='sb_bwd'):
    bq, bk = _blocks(S, SB_ROWS, SB_KEYS)
    nq = S // bq

    def body(q_ref, k_ref, v_ref, tot_ref, do_ref, dq_ref, dk_ref, dv_ref, dk_acc, dv_acc):
        i = pl.program_id(2)

        @pl.when(i == 0)
        def _():
            dk_acc[...] = jnp.zeros_like(dk_acc)
            dv_acc[...] = jnp.zeros_like(dv_acc)

        qv = _scaled(q_ref[...])
        dov = do_ref[...]
        qp = [qv[:, _lanes(p)] for p in range(GROUP)]
        dop = [dov[:, _lanes(p)] for p in range(GROUP)]
        qh = [_per_head(t) for t in qp]
        doh = [_per_head(t) for t in dop]
        heads = [(p, h) for p in range(GROUP) for h in range(2)]
        first, first_k = _lane0((bq, PAIR)), _lane0((bk, PAIR))
        last, diagonal = _diagonal(i, bq, bk, True)
        tri_incl = _tri(min(bk, SB_TRIANGLE), lambda j, s: j <= s)
        tri_excl = _tri(min(bk, SB_TRIANGLE), lambda j, s: j < s)
        zeros = tuple(jnp.zeros((bq, 1), F32) for _ in heads)
        tot = tuple(tot_ref[n] for n in range(len(heads)))

        def step(kb, carry, mask=None):
            dq, rest_l, left_g = carry
            sl = pl.ds(pl.multiple_of(kb * bk, bk), bk)
            new_dq, new_l, new_g = [], [], []
            for p in range(GROUP):
                ks, vs = k_ref[sl, _lanes(p)], v_ref[sl, _lanes(p)]
                dqh, dkh, dvh = [], [], []
                for h in range(2):
                    n = 2 * p + h
                    a, l = _sb_block(qh[p][h], ks, mask)
                    w = jnp.exp(a - _running_sums(l, tri_incl, SB_SUM_TERMS, True) + rest_l[n])
                    if mask is not None:
                        w = jnp.where(mask, w, 0.0)
                    g = w * _dot(doh[p][h], vs, NT)
                    beta = jnp.exp(a)
                    dz = g - beta * (g + _running_sums(g, tri_excl, 1, True) + left_g[n])
                    if mask is not None:
                        dz = jnp.where(mask, dz, 0.0)
                    dqh.append(_dot(dz, ks, NN))
                    dkh.append(_dot(dz, qp[p], TN))
                    dvh.append(_dot(w, dop[p], TN))
                    new_l.append(rest_l[n] - jnp.sum(l, axis=-1, keepdims=True))
                    new_g.append(left_g[n] + jnp.sum(g, axis=-1, keepdims=True))
                dk_acc[sl, _lanes(p)] += _pick(first_k, dkh[0], dkh[1])
                dv_acc[sl, _lanes(p)] += _pick(first_k, dvh[0], dvh[1])
                new_dq.append(dq[p] + _pick(first, dqh[0], dqh[1]))
            return tuple(new_dq), tuple(new_l), tuple(new_g)

        dq0 = tuple(jnp.zeros((bq, PAIR), F32) for _ in range(GROUP))
        dq, _, _ = step(last, lax.fori_loop(0, last, step, (dq0, tot, zeros)), diagonal)
        for p in range(GROUP):
            dq_ref[:, _lanes(p)] = (dq[p] * SCALE).astype(ACT_DTYPE)

        @pl.when(i == nq - 1)
        def _():
            dk_ref[...] = dk_acc[...].astype(ACT_DTYPE)
            dv_ref[...] = dv_acc[...].astype(ACT_DTYPE)

    qs, seq = _q_spec(bq, nq, 0, GROUP), _seq_spec(S, 0, GROUP)
    full = jax.ShapeDtypeStruct((B * S, MAIN_WIDTH), ACT_DTYPE)
    wide = pltpu.VMEM((S, GROUP * PAIR), F32)
    col, _ = _gate_specs(bq, S // bk, bk)
    return pl.pallas_call(
        body, name=name, grid=(B, MAIN_STEPS, nq),
        in_specs=[_q_spec(bq, nq, offs[0], GROUP), _seq_spec(S, offs[1], GROUP), _seq_spec(S, offs[2], GROUP), col,
                  qs],
        out_specs=[qs, seq, seq], out_shape=[full, full, full],
        scratch_shapes=[wide, wide],
        compiler_params=_params('parallel', 'parallel', 'arbitrary'),
    )(*qkv, tot, do)


def _mem_probs(qv, mk):
    s = _dot(qv, mk, NT) * SCALE
    p = jnp.exp(s - jnp.max(s, axis=-1, keepdims=True))
    return p / jnp.sum(p, axis=-1, keepdims=True)


def _mem_fwd(q, q_off, mkv, B, S, name='mem_fwd'):
    M = mkv.shape[0] // B
    bq = _tile(S, 512)
    nq = S // bq

    def body(q_ref, mk_ref, mv_ref, o_ref):
        first = _lane0((bq, PAIR))
        mk, mv = mk_ref[...], mv_ref[...]
        out = [_dot(_mem_probs(qh, mk), mv, NN) for qh in _per_head(q_ref[...])]
        o_ref[...] = _pick(first, out[0], out[1]).astype(ACT_DTYPE)

    return pl.pallas_call(
        body, name=name, grid=(B, N_MEM_PAIRS, nq),
        in_specs=[_q_spec(bq, nq, q_off), _seq_spec(M, 0), _seq_spec(M, N_MEM_PAIRS)],
        out_specs=_q_spec(bq, nq, 0),
        out_shape=jax.ShapeDtypeStruct((B * S, MEM_WIDTH), ACT_DTYPE),
        compiler_params=_params('parallel', 'parallel', 'parallel'),
    )(q, mkv, mkv)


def _mem_bwd(q, q_off, mkv, B, S, do, do_off, name='mem_bwd'):
    M = mkv.shape[0] // B
    bq = _tile(S, 512)
    nq = S // bq

    def body(q_ref, mk_ref, mv_ref, do_ref, dq_ref, dmk_ref, dmv_ref):
        i = pl.program_id(2)

        @pl.when(i == 0)
        def _():
            dmk_ref[...] = jnp.zeros_like(dmk_ref)
            dmv_ref[...] = jnp.zeros_like(dmv_ref)

        qv = q_ref[...]
        dov = do_ref[...]
        mk, mv = mk_ref[...], mv_ref[...]
        first = _lane0((bq, PAIR))
        first_m = _lane0((M, PAIR))
        dqh, dkh, dvh = [], [], []
        for qh, doh in zip(_per_head(qv), _per_head(dov)):
            p = _mem_probs(qh, mk)
            dp = _dot(doh, mv, NT)
            ds = p * (dp - jnp.sum(p * dp, axis=-1, keepdims=True))
            dqh.append(_dot(ds, mk, NN))
            dkh.append(_dot(ds, qv, TN))
            dvh.append(_dot(p, dov, TN))
        dq_ref[...] = (SCALE * _pick(first, dqh[0], dqh[1])).astype(ACT_DTYPE)
        dmk_ref[...] += SCALE * _pick(first_m, dkh[0], dkh[1])
        dmv_ref[...] += _pick(first_m, dvh[0], dvh[1])

    mem_out = jax.ShapeDtypeStruct((B * M, MEM_WIDTH), F32)
    return pl.pallas_call(
        body, name=name, grid=(B, N_MEM_PAIRS, nq),
        in_specs=[_q_spec(bq, nq, q_off), _seq_spec(M, 0), _seq_spec(M, N_MEM_PAIRS), _q_spec(bq, nq, do_off)],
        out_specs=[_q_spec(bq, nq, 0), _seq_spec(M, 0), _seq_spec(M, 0)],
        out_shape=[jax.ShapeDtypeStruct((B * S, MEM_WIDTH), ACT_DTYPE), mem_out, mem_out],
        compiler_params=_params('parallel', 'parallel', 'arbitrary'),
    )(q, mkv, mkv, do)


HALO = 8
CONV_CHUNK = 256


def _conv_chunk(scr, start, rows, w, b):
    at = HALO + start
    return (b + w[0:1, :] * scr[at - 2:at - 2 + rows, :] + w[1:2, :] * scr[at - 1:at - 1 + rows, :]
            + w[2:3, :] * scr[at:at + rows, :])


def _fill_frames(scr, ref):
    scr[0:HALO, :] = jnp.zeros((HALO, scr.shape[1]), F32)
    scr[HALO:, :] = ref[...].astype(F32)


def _sigmoid(x):
    return 0.5 + 0.5 * jnp.tanh(0.5 * x)


def _fold8(x):
    return jnp.sum(x.reshape(x.shape[0] // 8, 8, x.shape[1]), axis=0)


def _conv_specs(S, nf):
    ug = pl.BlockSpec((None, S, LANES), lambda b, j: (b, 0, j))
    uv = pl.BlockSpec((None, S, LANES), lambda b, j: (b, 0, j + nf))
    wg = pl.BlockSpec((3, LANES), lambda b, j: (0, j))
    wv = pl.BlockSpec((3, LANES), lambda b, j: (0, j + nf))
    bg = pl.BlockSpec((1, LANES), lambda b, j: (0, j))
    bv = pl.BlockSpec((1, LANES), lambda b, j: (0, j + nf))
    return ug, uv, wg, wv, bg, bv


def _conv_fwd(u, cw, cb, name='conv_fwd'):
    B, S, F2 = u.shape
    F = F2 // 2
    nf = F // LANES

    ch = min(CONV_CHUNK, S)

    def body(ug_ref, uv_ref, wg_ref, wv_ref, bg_ref, bv_ref, y_ref, g_scr, v_scr):
        _fill_frames(g_scr, ug_ref)
        _fill_frames(v_scr, uv_ref)
        wg, wv, bg, bv = wg_ref[...], wv_ref[...], bg_ref[...], bv_ref[...]
        for start in range(0, S, ch):
            gate = _conv_chunk(g_scr, start, ch, wg, bg)
            val = _conv_chunk(v_scr, start, ch, wv, bv)
            y_ref[start:start + ch, :] = (gate * _sigmoid(gate) * val).astype(ACT_DTYPE)

    specs = _conv_specs(S, nf)
    frames = pltpu.VMEM((HALO + S, LANES), F32)
    return pl.pallas_call(
        body, name=name, grid=(B, nf), in_specs=list(specs), out_specs=specs[0],
        out_shape=jax.ShapeDtypeStruct((B, S, F), ACT_DTYPE), scratch_shapes=[frames, frames],
        compiler_params=_params('parallel', 'parallel'),
    )(u, u, cw, cw, cb, cb)


def _conv_bwd(u, cw, cb, dy, name='conv_bwd'):
    B, S, F2 = u.shape
    F = F2 // 2
    nf = F // LANES

    ch = min(CONV_CHUNK, S)

    def body(ug_ref, uv_ref, wg_ref, wv_ref, bg_ref, bv_ref, dy_ref,
             dug_ref, duv_ref, dwg_ref, dwv_ref, dbg_ref, dbv_ref, g_scr, v_scr, dg_scr, dv_scr):
        b = pl.program_id(1)

        @pl.when(b == 0)
        def _():
            for r in (dwg_ref, dwv_ref, dbg_ref, dbv_ref):
                r[...] = jnp.zeros_like(r)

        _fill_frames(g_scr, ug_ref)
        _fill_frames(v_scr, uv_ref)
        wg, wv, bg, bv = wg_ref[...], wv_ref[...], bg_ref[...], bv_ref[...]
        for scr in (dg_scr, dv_scr):
            scr[S:, :] = jnp.zeros((HALO, LANES), F32)
        for start in range(0, S, ch):
            gate = _conv_chunk(g_scr, start, ch, wg, bg)
            val = _conv_chunk(v_scr, start, ch, wv, bv)
            dyv = dy_ref[start:start + ch, :].astype(F32)
            sg = _sigmoid(gate)
            dv_scr[start:start + ch, :] = dyv * (gate * sg)
            dg_scr[start:start + ch, :] = dyv * val * (sg * (1.0 + gate * (1.0 - sg)))

        for u_scr, d_scr, w, du_ref, dw_ref, db_ref in ((g_scr, dg_scr, wg, dug_ref, dwg_ref, dbg_ref),
                                                         (v_scr, dv_scr, wv, duv_ref, dwv_ref, dbv_ref)):
            sums = [jnp.zeros((8, LANES), F32) for _ in range(4)]
            for start in range(0, S, ch):
                x = u_scr[HALO + start:HALO + start + ch, :]
                d = [d_scr[start + n:start + n + ch, :] for n in range(3)]
                du_ref[start:start + ch, :] = (w[2:3, :] * d[0] + w[1:2, :] * d[1] + w[0:1, :] * d[2]).astype(ACT_DTYPE)
                sums = [sums[0] + _fold8(x * d[2]), sums[1] + _fold8(x * d[1]), sums[2] + _fold8(x * d[0]),
                        sums[3] + _fold8(d[0])]
            total = [jnp.sum(s, axis=0, keepdims=True) for s in sums]
            dw_ref[...] += jnp.concatenate(total[:3], axis=0)
            db_ref[...] += total[3]

    def swap(spec_fn):
        return lambda j, b: spec_fn(b, j)

    ug, uv, wg, wv, bg, bv = _conv_specs(S, nf)
    ins = [pl.BlockSpec(s.block_shape, swap(s.index_map)) for s in (ug, uv, wg, wv, bg, bv, ug)]
    outs = [ins[0], ins[0], ins[2], ins[2], ins[4], ins[4]]
    frames = pltpu.VMEM((HALO + S, LANES), F32)
    return pl.pallas_call(
        body, name=name, grid=(nf, B), in_specs=ins, out_specs=outs, scratch_shapes=[frames] * 4,
        out_shape=[jax.ShapeDtypeStruct((B, S, F), ACT_DTYPE), jax.ShapeDtypeStruct((B, S, F), ACT_DTYPE),
                   jax.ShapeDtypeStruct((3, F), F32), jax.ShapeDtypeStruct((3, F), F32),
                   jax.ShapeDtypeStruct((1, F), F32), jax.ShapeDtypeStruct((1, F), F32)],
        compiler_params=_params('parallel', 'arbitrary'),
    )(u, u, cw, cw, cb, cb, dy)


ADAM_BLOCK_BYTES = 512 * 1024


def _adamw(w, g, m, v, layer, earlier, name):
    L, r, c = w.shape
    tr = r
    if r * c * 4 > ADAM_BLOCK_BYTES and r % 8 == 0:
        tr = 8
        for t in range(8, r + 1, 8):
            if r % t == 0 and t * c * 4 <= ADAM_BLOCK_BYTES:
                tr = t

    def body(w_ref, g_ref, m_ref, v_ref, *rest):
        go_ref, d_ref, nm_ref, nv_ref = rest[-4:]
        gv = g_ref[...]
        nm = ADAM_B1 * m_ref[...] + (1.0 - ADAM_B1) * gv
        nv = ADAM_B2 * v_ref[...] + (1.0 - ADAM_B2) * (gv * gv)
        m_hat = nm / (1.0 - ADAM_B1 ** ADAM_STEP)
        v_hat = nv / (1.0 - ADAM_B2 ** ADAM_STEP)
        d_ref[...] = -ADAM_LR * (m_hat / (jnp.sqrt(v_hat) + ADAM_EPS) + ADAM_WD * w_ref[...])
        nm_ref[...] = nm
        nv_ref[...] = nv
        go_ref[...] = gv

    lay = pl.BlockSpec((None, tr, c), lambda i: (layer, i, 0))
    one = pl.BlockSpec((tr, c), lambda i: (i, 0))
    shp = jax.ShapeDtypeStruct((L, r, c), F32)
    in_specs = [lay, one, lay, lay]
    args = (w, g, m, v)
    aliases = {}
    if earlier is not None:
        in_specs += [ANY] * 4
        args += tuple(earlier)
        aliases = {4 + k: k for k in range(4)}
    return pl.pallas_call(
        body, name=name, grid=(r // tr,), in_specs=in_specs, out_specs=[lay] * 4, out_shape=[shp] * 4,
        input_output_aliases=aliases, compiler_params=_params('parallel'),
    )(*args)


def _my_place():
    return lax.axis_index('x'), lax.axis_index('y'), lax.axis_index('c')


def _other_chips(x, y):
    return [(1 - x, y), (x, 1 - y), (1 - x, 1 - y)]


def _remote(src, dst, send_sem, recv_sem, to):
    return pltpu.make_async_remote_copy(src_ref=src, dst_ref=dst, send_sem=send_sem, recv_sem=recv_sem,
                                        device_id=to, device_id_type=MESH)


def _hbm_call(body, n_in, out_shapes, scratch, name, aliases=None):
    return pl.pallas_call(body, name=name, in_specs=[ANY] * n_in, out_specs=[ANY] * len(out_shapes),
                          out_shape=out_shapes, scratch_shapes=scratch, input_output_aliases=aliases or {})


def _full_shape(shard_shape, kind):
    L, r, c = shard_shape
    return {'rows': (L, N_CHIPS * r, c), 'cols': (L, r, N_CHIPS * c), 'stack': (N_CHIPS * L, r, c)}[kind]


def _place_block(w, kind, out_dtype, chip_arr, name):
    L, r, c = w.shape
    tr = r if r % 16 else _tile(r, max(16, SUM_BLOCK_BYTES // (4 * c)), 16)
    nrt = r // tr

    def body(k_ref, w_ref, o_ref):
        o_ref[...] = w_ref[...].astype(out_dtype)

    out_map = {'rows': lambda l, i, k_ref: (l, k_ref[0] * nrt + i, 0),
               'cols': lambda l, i, k_ref: (l, i, k_ref[0]),
               'stack': lambda l, i, k_ref: (k_ref[0] * L + l, i, 0)}[kind]
    gs = pltpu.PrefetchScalarGridSpec(
        num_scalar_prefetch=1, grid=(L, nrt),
        in_specs=[pl.BlockSpec((None, tr, c), lambda l, i, k_ref: (l, i, 0))],
        out_specs=pl.BlockSpec((None, tr, c), out_map))
    return pl.pallas_call(
        body, name=name, grid_spec=gs, out_shape=jax.ShapeDtypeStruct(_full_shape(w.shape, kind), out_dtype),
        compiler_params=_params('parallel', 'parallel'),
    )(chip_arr, w)


class _Exchange:
    def __init__(self, inputs, out_shapes, aliases, scratch, start, finish):
        self.inputs, self.out_shapes, self.aliases, self.scratch = list(inputs), list(out_shapes), aliases, scratch
        self.start, self.finish = start, finish


def _run_exchange(ex, name):
    n_in, n_out = len(ex.inputs), len(ex.out_shapes)

    def body(*refs):
        parts = refs[:n_in], refs[n_in:n_in + n_out], refs[n_in + n_out:]
        ex.start(*parts)
        ex.finish(*parts)

    return _hbm_call(body, n_in, ex.out_shapes, ex.scratch, name, aliases=ex.aliases)(*ex.inputs)


def _call_beside(body, name, grid, in_specs, out_specs, out_shape, scratch, args, semantics, beside):
    if beside is None:
        outs = pl.pallas_call(body, name=name, grid=grid, in_specs=in_specs, out_specs=out_specs, out_shape=out_shape,
                              scratch_shapes=scratch, compiler_params=_params(*semantics))(*args)
        return outs, None
    n_in, n_out, n_scr = len(in_specs), len(out_specs), len(scratch)
    b_in, b_out = len(beside.inputs), len(beside.out_shapes)

    def carrier(*refs):
        cuts = [n_in, b_in, n_out, b_out, n_scr]
        parts, at = [], 0
        for size in cuts:
            parts.append(refs[at:at + size])
            at += size
        ins, ex_ins, outs, ex_outs, scr = parts
        ex_scr = refs[at:]
        ids = [pl.program_id(d) for d in range(len(grid))]
        first = functools.reduce(jnp.logical_and, [i == 0 for i in ids])
        last = functools.reduce(jnp.logical_and, [i == g - 1 for i, g in zip(ids, grid)])

        @pl.when(first)
        def _():
            beside.start(ex_ins, ex_outs, ex_scr)

        body(*ins, *outs, *scr)

        @pl.when(last)
        def _():
            beside.finish(ex_ins, ex_outs, ex_scr)

    res = pl.pallas_call(
        carrier, name=name, grid=grid, in_specs=list(in_specs) + [ANY] * b_in,
        out_specs=list(out_specs) + [ANY] * b_out, out_shape=list(out_shape) + beside.out_shapes,
        scratch_shapes=list(scratch) + beside.scratch,
        input_output_aliases={n_in + i: n_out + o for i, o in beside.aliases.items()},
        compiler_params=_params(*['arbitrary'] * len(grid)),
    )(*args, *beside.inputs)
    return res[:n_out], res[n_out:]


def _gather_exchange(fulls, shard_shapes, kinds, split):
    n = len(fulls)

    def plan(outs, send_sems, recv_sems):
        x, y, c = _my_place()
        chip = 2 * x + y
        sibling = (x, y, 1 - c)
        others = _other_chips(x, y)

        def window(a, k, half):
            L, r, cols = shard_shapes[a]
            first, count = (0, r) if half is None else (half * (r // 2), r // 2)
            if kinds[a] == 'rows':
                return outs[a].at[:, pl.ds(k * r + first, count), :]
            if kinds[a] == 'cols':
                return outs[a].at[:, pl.ds(first, count), pl.ds(pl.multiple_of(k * cols, LANES), cols)]
            return outs[a].at[pl.ds(k * L, L), pl.ds(first, count), :]

        sends, arrivals, forwards, forwarded = [], [], [], []
        for a in range(n):
            half = c if split[a] else None
            for j, (ox, oy) in enumerate(others):
                sems = (send_sems.at[6 * a + j], recv_sems.at[6 * a + j], (ox, oy, c))
                sends.append(_remote(window(a, chip, half), window(a, chip, half), *sems))
                got = window(a, 2 * ox + oy, half)
                arrivals.append(_remote(got, got, *sems))
                if split[a]:
                    sems = (send_sems.at[6 * a + 3 + j], recv_sems.at[6 * a + 3 + j], sibling)
                    forwards.append(_remote(got, got, *sems))
                    theirs = window(a, 2 * ox + oy, 1 - c)
                    forwarded.append(_remote(theirs, theirs, *sems))
                else:
                    forwards.append(None)
        return sends, arrivals, forwards, forwarded

    def start(ins, outs, scratch):
        sends, _, _, _ = plan(outs, *scratch)
        for cp in sends:
            cp.start()

    def finish(ins, outs, scratch):
        sends, arrivals, forwards, forwarded = plan(outs, *scratch)
        for arrived, fw in zip(arrivals, forwards):
            arrived.wait_recv()
            if fw is not None:
                fw.start()
        for cp in forwarded:
            cp.wait_recv()
        for cp in sends + [fw for fw in forwards if fw is not None]:
            cp.wait_send()

    scratch = [pltpu.SemaphoreType.DMA((6 * n,)), pltpu.SemaphoreType.DMA((6 * n,))]
    out_shapes = [jax.ShapeDtypeStruct(f.shape, f.dtype) for f in fulls]
    return _Exchange(fulls, out_shapes, {a: a for a in range(n)}, scratch, start, finish)


def _swap_cores(gs, name='swap_cores'):
    n = len(gs)
    out_shapes = [jax.ShapeDtypeStruct((g.shape[0], g.shape[1] // 2, g.shape[2]), g.dtype) for g in gs]

    def body(*refs):
        ins, outs = refs[:n], refs[n:2 * n]
        send_sems, recv_sems = refs[2 * n:]
        x, y, c = _my_place()
        cps = []
        for a in range(n):
            rh = gs[a].shape[1] // 2
            cp = _remote(ins[a].at[:, pl.ds((1 - c) * rh, rh), :], outs[a], send_sems.at[a], recv_sems.at[a],
                         (x, y, 1 - c))
            cp.start()
            cps.append(cp)
        for cp in cps:
            cp.wait()

    scratch = [pltpu.SemaphoreType.DMA((n,)), pltpu.SemaphoreType.DMA((n,))]
    return _hbm_call(body, n, out_shapes, scratch, name)(*gs)


SUM_BLOCK_BYTES = 2 * 1024 * 1024


def _sum_rows(rh, cols):
    return _tile(rh, max(16, SUM_BLOCK_BYTES // (4 * cols)), 16)


def _add_cores(g, other, c_arr, wire_dtype, name):
    n, r, cols = g.shape
    rh = r // 2
    tr = _sum_rows(rh, cols)
    nrt = rh // tr

    def body(c_ref, g_ref, o_ref, q_ref):
        q_ref[...] = (g_ref[...] + o_ref[...]).astype(wire_dtype)

    gs = pltpu.PrefetchScalarGridSpec(
        num_scalar_prefetch=1, grid=(n, nrt),
        in_specs=[pl.BlockSpec((None, tr, cols), lambda j, i, c_ref: (j, c_ref[0] * nrt + i, 0)),
                  pl.BlockSpec((None, tr, cols), lambda j, i, c_ref: (j, i, 0))],
        out_specs=pl.BlockSpec((None, tr, cols), lambda j, i, c_ref: (j, i, 0)))
    return pl.pallas_call(
        body, name=name, grid_spec=gs, out_shape=jax.ShapeDtypeStruct((n, rh, cols), wire_dtype),
        compiler_params=_params('parallel', 'parallel'),
    )(c_arr, g, other)


def _send_exchange(qs):
    n = len(qs)

    def plan(ins, outs, send_sems, recv_sems):
        x, y, c = _my_place()
        return [_remote(ins[a].at[2 * ox + oy], outs[a].at[j], send_sems.at[3 * a + j], recv_sems.at[3 * a + j],
                        (ox, oy, c))
                for a in range(n) for j, (ox, oy) in enumerate(_other_chips(x, y))]

    def start(ins, outs, scratch):
        for cp in plan(ins, outs, *scratch):
            cp.start()

    def finish(ins, outs, scratch):
        cps = plan(ins, outs, *scratch)
        for cp in cps:
            cp.wait_recv()
        for cp in cps:
            cp.wait_send()

    scratch = [pltpu.SemaphoreType.DMA((3 * n,)), pltpu.SemaphoreType.DMA((3 * n,))]
    out_shapes = [jax.ShapeDtypeStruct((3,) + q.shape[1:], q.dtype) for q in qs]
    return _Exchange(qs, out_shapes, {}, scratch, start, finish)


def _sum_chips(q, got, place_arr, name):
    n, rh, cols = q.shape
    tr = _sum_rows(rh, cols)

    def body(p_ref, q_ref, gx_ref, gy_ref, gxy_ref, o_ref):
        f = lambda r: r[...].astype(F32)
        o_ref[...] = (f(q_ref) + f(gxy_ref)) + (f(gx_ref) + f(gy_ref))

    def got_spec(j):
        return pl.BlockSpec((None, tr, cols), lambda i, p_ref: (j, i, 0))

    gs = pltpu.PrefetchScalarGridSpec(
        num_scalar_prefetch=1, grid=(rh // tr,),
        in_specs=[pl.BlockSpec((None, tr, cols), lambda i, p_ref: (p_ref[0], i, 0)),
                  got_spec(0), got_spec(1), got_spec(2)],
        out_specs=pl.BlockSpec((None, tr, cols), lambda i, p_ref: (p_ref[1], i, 0)))
    return pl.pallas_call(
        body, name=name, grid_spec=gs, out_shape=jax.ShapeDtypeStruct((2, rh, cols), F32),
        compiler_params=_params('parallel'),
    )(place_arr, q, got, got, got)


def _join_cores(rs, name='join_cores'):
    n = len(rs)
    out_shapes = [jax.ShapeDtypeStruct(r.shape, r.dtype) for r in rs]

    def body(*refs):
        outs = refs[n:2 * n]
        send_sems, recv_sems = refs[2 * n:]
        x, y, c = _my_place()
        cps = []
        for a in range(n):
            cp = _remote(outs[a].at[c], outs[a].at[c], send_sems.at[a], recv_sems.at[a], (x, y, 1 - c))
            cp.start()
            cps.append(cp)
        for cp in cps:
            cp.wait()

    scratch = [pltpu.SemaphoreType.DMA((n,)), pltpu.SemaphoreType.DMA((n,))]
    return _hbm_call(body, n, out_shapes, scratch, name, aliases={a: a for a in range(n)})(*rs)


def _gate_rows(t, B, S):
    return t.reshape(B, S, N_MAIN_HEADS).transpose(0, 2, 1).reshape(B * N_MAIN_HEADS, S)


def _gate_cols(t, B, S):
    return t.reshape(B, N_MAIN_HEADS, S).transpose(0, 2, 1).reshape(B * S, N_MAIN_HEADS)


def _mem_kv_fwd(mem2, g, w, tag):
    hm = _rms_fwd(mem2, g, name=f'rms_mem_{tag}')
    mkv = _matmul(hm, w, 'nn', ACT_DTYPE, name=f'mm_memkv_{tag}')
    return hm, mkv


def _mem_kv_bwd(mem2, g, w, hm, dmk, dmv, tag):
    dmkv = jnp.concatenate([dmk, dmv], axis=1)
    dw = _matmul(hm, dmkv, 'tn', F32, name=f'mm_memkv_dw_{tag}')
    dhm = _matmul(dmkv, w, 'nt', F32, name=f'mm_memkv_dx_{tag}')
    _, dg = _rms_bwd(mem2, g, dhm, None, name=f'rms_mem_bwd_{tag}')
    return dw, dg


def _ffn_fwd(x, g, w_up, cw, cb, w_down, B, S, tag):
    T = x.shape[0]
    h2 = _rms_fwd(x, g, name=f'rms_ffn_{tag}')
    u = _matmul(h2, w_up, 'nn', ACT_DTYPE, name=f'mm_up_{tag}')
    y = _conv_fwd(u.reshape(B, S, -1), cw, cb, name=f'conv_fwd_{tag}').reshape(T, -1)
    x2 = _matmul(y, w_down, 'nn', F32, res=x, name=f'mm_down_{tag}')
    return x2, (h2, u, y)


def _ffn_bwd(dx2, x, g, w_up, cw, cb, w_down, saved, B, S, tag):
    h2, u, y = saved
    T = x.shape[0]
    dy = _matmul(dx2, w_down, 'nt', ACT_DTYPE, name=f'mm_down_dx_{tag}')
    dw_down = _matmul(y, dx2, 'tn', F32, name=f'mm_down_dw_{tag}')
    dug, duv, dcwg, dcwv, dcbg, dcbv = _conv_bwd(u.reshape(B, S, -1), cw, cb, dy.reshape(B, S, -1),
                                                  name=f'conv_bwd_{tag}')
    du = jnp.concatenate([dug.reshape(T, -1), duv.reshape(T, -1)], axis=1)
    dh2 = _matmul(du, w_up, 'nt', F32, name=f'mm_up_dx_{tag}')
    dw_up = _matmul(h2, du, 'tn', F32, slots=N_CHIPS, name=f'mm_up_dw_{tag}')
    dx, dg = _rms_bwd(x, g, dh2, dx2, name=f'rms_ffn_bwd_{tag}')
    dcw = jnp.concatenate([dcwg, dcwv], axis=1)
    dcb = jnp.concatenate([dcbg, dcbv], axis=1)
    return dx, dg, dw_up, dcw, dcb, dw_down


def _step(x, mem, tgt, W, late_weights=None, reduce_early=None):
    B, S, D = x.shape
    T = B * S
    x0 = x.reshape(T, D)
    mem2 = mem.reshape(-1, D)
    tgt2 = tgt.reshape(T, D)
    row = lambda v: v.reshape(1, -1)
    q3 = 3 * MAIN_WIDTH

    w_in_a = W['w_in_a'][0]
    wa_main = jnp.concatenate([w_in_a[:, :q3], w_in_a[:, q3 + N_MAIN_HEADS:]], axis=1)
    wa_gate = jnp.pad(w_in_a[:, q3:q3 + N_MAIN_HEADS], ((0, 0), (0, LANES - N_MAIN_HEADS)))
    bcol = jnp.tile(W['b_f_a'][0], B).reshape(B * N_MAIN_HEADS, 1)
    nkb = S // _blocks(S, FOX_ROWS, FOX_KEYS)[1]

    h1a = _rms_fwd(x0, row(W['ln_mix_g'][0]), name='rms_mix_a')
    pa = _matmul(h1a, wa_main, 'nn', ACT_DTYPE, name='mm_in_a')
    flog = _matmul(h1a, wa_gate, 'nn', F32, name='mm_gate_a')
    qkv_a = (pa, pa, pa)
    offs_a = (0, N_MAIN_PAIRS, 2 * N_MAIN_PAIRS)
    qm_off_a = 3 * N_MAIN_PAIRS
    zt = _gate_rows(flog[:, :N_MAIN_HEADS], B, S)
    cum = _gate_fwd(zt, bcol)
    ccol = cum.reshape(B * N_MAIN_HEADS, S, 1)
    crow = cum.reshape(B * N_MAIN_HEADS, nkb, 1, S // nkb)
    (oa, lse), late = _fox_fwd(qkv_a, offs_a, B, S, ccol, crow, beside=late_weights[0] if late_weights else None)
    if late_weights:
        W = {**W, **late_weights[1](late)}
    w_in_b = W['w_in_b'][0]
    hma, mkva = _mem_kv_fwd(mem2, row(W['ln_mem_g'][0]), W['w_memkv'][0], 'a')
    oma = _mem_fwd(pa, qm_off_a, mkva, B, S, name='mem_fwd_a')
    ocat_a = jnp.concatenate([oa, oma], axis=1)
    x1 = _matmul(ocat_a, W['w_out'][0], 'nn', F32, res=x0, name='mm_out_a')
    x2, ffn_a = _ffn_fwd(x1, row(W['ln_ffn_g'][0]), W['w_up'][0], W['conv_w'][0], row(W['conv_b'][0]),
                         W['w_down'][0], B, S, 'a')
    hkv = _rms_fwd(x2, row(W['ln_kv_g']), name='rms_kv')
    kvs = _matmul(hkv, W['w_kv'], 'nn', ACT_DTYPE, name='mm_kv')
    h1b = _rms_fwd(x2, row(W['ln_mix_g'][1]), name='rms_mix_b')
    pb = _matmul(h1b, w_in_b, 'nn', ACT_DTYPE, name='mm_in_b')
    qkv_b = (pb, kvs, kvs)
    offs_b = (0, 0, N_MAIN_PAIRS)
    qm_off_b = N_MAIN_PAIRS
    ob, tot_b = _sb_fwd(qkv_b, offs_b, B, S)
    hmb, mkvb = _mem_kv_fwd(mem2, row(W['ln_mem_g'][1]), W['w_memkv'][1], 'b')
    omb = _mem_fwd(pb, qm_off_b, mkvb, B, S, name='mem_fwd_b')
    ocat_b = jnp.concatenate([ob, omb], axis=1)
    x3 = _matmul(ocat_b, W['w_out'][1], 'nn', F32, res=x2, name='mm_out_b')
    x4, ffn_b = _ffn_fwd(x3, row(W['ln_ffn_g'][1]), W['w_up'][1], W['conv_w'][1], row(W['conv_b'][1]),
                         W['w_down'][1], B, S, 'b')
    loss, dx4, d_final_g = _final_loss(x4, row(W['final_g']), tgt2)

    dx3, dg_ffn_b, dw_up_b, dcw_b, dcb_b, dw_down_b = _ffn_bwd(
        dx4, x3, row(W['ln_ffn_g'][1]), W['w_up'][1], W['conv_w'][1], row(W['conv_b'][1]), W['w_down'][1],
        ffn_b, B, S, 'b')
    docat = _matmul(dx3, W['w_out'][1], 'nt', ACT_DTYPE, name='mm_out_dx_b')
    dw_out_b = _matmul(ocat_b, dx3, 'tn', F32, name='mm_out_dw_b')
    dqb, dkb, dvb = _sb_bwd(qkv_b, offs_b, B, S, tot_b, docat)
    dqmb, dmkb, dmvb = _mem_bwd(pb, qm_off_b, mkvb, B, S, docat, N_MAIN_PAIRS, name='mem_bwd_b')
    dw_memkv_b, dg_mem_b = _mem_kv_bwd(mem2, row(W['ln_mem_g'][1]), W['w_memkv'][1], hmb, dmkb, dmvb, 'b')
    dpb = jnp.concatenate([dqb, dqmb], axis=1)
    dh1b = _matmul(dpb, w_in_b, 'nt', F32, name='mm_in_dx_b')
    dw_in_b = _matmul(h1b, dpb, 'tn', F32, name='mm_in_dw_b')
    dx2, dg_mix_b = _rms_bwd(x2, row(W['ln_mix_g'][1]), dh1b, dx3, name='rms_mix_bwd_b')
    dkvs = jnp.concatenate([dkb, dvb], axis=1)
    dhkv = _matmul(dkvs, W['w_kv'], 'nt', F32, name='mm_kv_dx')
    dw_kv = _matmul(hkv, dkvs, 'tn', F32, slots=N_CHIPS, name='mm_kv_dw')
    dx2, dg_kv = _rms_bwd(x2, row(W['ln_kv_g']), dhkv, dx2, name='rms_kv_bwd')

    dx1, dg_ffn_a, dw_up_a, dcw_a, dcb_a, dw_down_a = _ffn_bwd(
        dx2, x1, row(W['ln_ffn_g'][0]), W['w_up'][0], W['conv_w'][0], row(W['conv_b'][0]), W['w_down'][0],
        ffn_a, B, S, 'a')
    docat = _matmul(dx1, W['w_out'][0], 'nt', ACT_DTYPE, name='mm_out_dx_a')
    dw_out_a = _matmul(ocat_a, dx1, 'tn', F32, name='mm_out_dw_a')

    def by_rows(dw):
        return dw.reshape(N_CHIPS, dw.shape[0] // N_CHIPS, dw.shape[1])

    grads = {
        'w_in_b': [by_rows(dw_in_b)],
        'w_kv': [dw_kv],
        'w_out': [by_rows(dw_out_a), by_rows(dw_out_b)],
        'w_up': [dw_up_a, dw_up_b],
        'w_down': [by_rows(dw_down_a), by_rows(dw_down_b)],
    }
    early = [(n, layer, g) for n, gs in grads.items() for layer, g in enumerate(gs)]
    early.append(('w_memkv', 1, by_rows(dw_memkv_b)))
    beside = reduce_early(early) if reduce_early else None
    (dqa, dka, dva, dccol, dcrow), crossed = _fox_bwd(qkv_a, offs_a, B, S, ccol, crow, oa, lse, docat, beside=beside)
    dzt, dbrow = _gate_bwd(zt, bcol, dccol.reshape(B * N_MAIN_HEADS, S) + dcrow.reshape(B * N_MAIN_HEADS, S))
    dqma, dmka, dmva = _mem_bwd(pa, qm_off_a, mkva, B, S, docat, N_MAIN_PAIRS, name='mem_bwd_a')
    dw_memkv_a, dg_mem_a = _mem_kv_bwd(mem2, row(W['ln_mem_g'][0]), W['w_memkv'][0], hma, dmka, dmva, 'a')
    dpa = jnp.concatenate([dqa, dka, dva, dqma], axis=1)
    dflog = jnp.pad(_gate_cols(dzt, B, S), ((0, 0), (0, LANES - N_MAIN_HEADS)))
    dh1a = _matmul(dpa, wa_main, 'nt', F32, name='mm_in_dx_a')
    dh1a = _matmul(dflog, wa_gate, 'nt', F32, res=dh1a, name='mm_gate_dx_a')
    dwa_main = _matmul(h1a, dpa, 'tn', F32, name='mm_in_dw_a')
    dwa_gate = _matmul(h1a, dflog, 'tn', F32, name='mm_gate_dw_a')
    dx0, dg_mix_a = _rms_bwd(x0, row(W['ln_mix_g'][0]), dh1a, dx1, name='rms_mix_bwd_a')

    dw_in_a = jnp.concatenate([dwa_main[:, :q3], dwa_gate[:, :N_MAIN_HEADS], dwa_main[:, q3:]], axis=1)
    grads.update({
        'ln_mix_g': jnp.concatenate([dg_mix_a, dg_mix_b], axis=0),
        'w_in_a': dw_in_a[None],
        'b_f_a': dbrow.reshape(B, N_MAIN_HEADS).sum(axis=0)[None],
        'ln_kv_g': dg_kv[0],
        'ln_mem_g': jnp.concatenate([dg_mem_a, dg_mem_b], axis=0),
        'w_memkv': [by_rows(dw_memkv_a), early[-1][2]],
        'ln_ffn_g': jnp.concatenate([dg_ffn_a, dg_ffn_b], axis=0),
        'conv_w': jnp.stack([dcw_a, dcw_b]),
        'conv_b': jnp.concatenate([dcb_a, dcb_b], axis=0),
        'final_g': d_final_g[0],
    })
    return loss, dx0.reshape(B, S, D), grads, crossed


BLOCKED = ('w_in_b', 'w_kv', 'w_memkv', 'w_out', 'w_up', 'w_down')
MISC_ROWS = 32


def _misc_names():
    return [n for n in PARAM_NAMES if PARAM_SHARD_AXIS[n] is None] + ['conv_w']


def _reduce_begin(arrays, wire, tag):
    _, _, c = _my_place()
    c_arr = jnp.reshape(c, (1,)).astype(jnp.int32)
    others = _swap_cores(arrays, name=f'swap_cores_{tag}')
    return [_add_cores(g, o, c_arr, wire[i], name=f'add_cores_{tag}_{i}')
            for i, (g, o) in enumerate(zip(arrays, others))]


def _reduce_end(qs, crossed, tag):
    x, y, c = _my_place()
    place_arr = jnp.stack([2 * x + y, c]).astype(jnp.int32)
    sums = [_sum_chips(q, g, place_arr, name=f'sum_chips_{tag}_{i}') for i, (q, g) in enumerate(zip(qs, crossed))]
    return [j.reshape(-1, j.shape[-1]) for j in _join_cores(sums, name=f'join_cores_{tag}')]


def _pack_late(grads, shards):
    a_cols = shards['w_in_a'].shape[2]
    a_pad = -(-a_cols // LANES) * LANES
    dw_in_a = grads['w_in_a'][0]
    in_a = jnp.stack([jnp.pad(dw_in_a[:, k * a_cols:(k + 1) * a_cols], ((0, 0), (0, a_pad - a_cols)))
                      for k in range(N_CHIPS)])
    conv_cols = shards['conv_w'].shape[2]
    misc = []
    for k in range(N_CHIPS):
        parts = [grads[n].reshape(-1) for n in _misc_names()[:-1]]
        parts.append(grads['conv_w'][:, :, k * conv_cols:(k + 1) * conv_cols].reshape(-1))
        flat = jnp.concatenate(parts)
        assert flat.shape[0] <= MISC_ROWS * PACK_COLS
        misc.append(jnp.pad(flat, (0, MISC_ROWS * PACK_COLS - flat.shape[0])).reshape(MISC_ROWS, PACK_COLS))
    return in_a, jnp.stack(misc)


def _unpack_misc(rows, shards):
    flat = rows.reshape(-1)
    out, off = {}, 0
    for name in _misc_names():
        shape = shards[name].shape
        size = math.prod(shape)
        out[name] = flat[off:off + size].reshape(-1, shape[-1])
        off += size
    return out


def kernel(x, mem, ln_mix_g, w_in_a, b_f_a, w_in_b, ln_kv_g, w_kv, ln_mem_g, w_memkv, w_out, ln_ffn_g, w_up, conv_w, conv_b, w_down, final_g, loss_target, m_ln_mix_g, m_w_in_a, m_b_f_a, m_w_in_b, m_ln_kv_g, m_w_kv, m_ln_mem_g, m_w_memkv, m_w_out, m_ln_ffn_g, m_w_up, m_conv_w, m_conv_b, m_w_down, m_final_g, v_ln_mix_g, v_w_in_a, v_b_f_a, v_w_in_b, v_ln_kv_g, v_w_kv, v_ln_mem_g, v_w_memkv, v_w_out, v_ln_ffn_g, v_w_up, v_conv_w, v_conv_b, v_w_down, v_final_g):
    shards = dict(ln_mix_g=ln_mix_g, w_in_a=w_in_a, b_f_a=b_f_a, w_in_b=w_in_b, ln_kv_g=ln_kv_g, w_kv=w_kv,
                  ln_mem_g=ln_mem_g, w_memkv=w_memkv, w_out=w_out, ln_ffn_g=ln_ffn_g, w_up=w_up, conv_w=conv_w,
                  conv_b=conv_b, w_down=w_down, final_g=final_g)
    moments_m = dict(ln_mix_g=m_ln_mix_g, w_in_a=m_w_in_a, b_f_a=m_b_f_a, w_in_b=m_w_in_b, ln_kv_g=m_ln_kv_g,
                     w_kv=m_w_kv, ln_mem_g=m_ln_mem_g, w_memkv=m_w_memkv, w_out=m_w_out, ln_ffn_g=m_ln_ffn_g,
                     w_up=m_w_up, conv_w=m_conv_w, conv_b=m_conv_b, w_down=m_w_down, final_g=m_final_g)
    moments_v = dict(ln_mix_g=v_ln_mix_g, w_in_a=v_w_in_a, b_f_a=v_b_f_a, w_in_b=v_w_in_b, ln_kv_g=v_ln_kv_g,
                     w_kv=v_w_kv, ln_mem_g=v_ln_mem_g, w_memkv=v_w_memkv, w_out=v_w_out, ln_ffn_g=v_ln_ffn_g,
                     w_up=v_w_up, conv_w=v_conv_w, conv_b=v_conv_b, w_down=v_w_down, final_g=v_final_g)

    kinds = {'w_in_a': 'stack', 'w_in_b': 'rows', 'w_kv': 'cols', 'w_memkv': 'rows', 'w_out': 'rows', 'w_up': 'cols',
             'w_down': 'rows', 'conv_w': 'cols'}
    mx, my, _ = _my_place()
    chip_arr = jnp.reshape(2 * mx + my, (1,)).astype(jnp.int32)
    placed, shard_shapes = {}, {}
    for n, kind in kinds.items():
        w = shards[n].reshape((-1,) + shards[n].shape[-2:])
        shard_shapes[n] = w.shape
        placed[n] = _place_block(w, kind, F32 if n in F32_GATHERED else jnp.bfloat16, chip_arr, name=f'place_{n}')

    def gather(names):
        return _gather_exchange([placed[n] for n in names], [shard_shapes[n] for n in names],
                                [kinds[n] for n in names], [n not in F32_GATHERED for n in names])

    def as_weights(names, full):
        out = dict(zip(names, full))
        if 'w_in_a' in out:
            out['w_in_a'] = jnp.concatenate([out['w_in_a'][k] for k in range(N_CHIPS)], axis=1)[None]
        if 'w_kv' in out:
            out['w_kv'] = out['w_kv'][0]
        return out

    first = ['w_in_a', 'conv_w']
    late = [n for n in kinds if n not in first]
    W = {**shards, **as_weights(first, _run_exchange(gather(first), 'gather_first'))}

    early = {}

    def reduce_early(items):
        early['owners'] = [(n, layer) for n, layer, _ in items]
        early['qs'] = _reduce_begin([g for _, _, g in items], [jnp.bfloat16] * len(items), 'early')
        return _send_exchange(early['qs'])

    loss_part, grad_x, grads, crossed = _step(x, mem, loss_target, W, (gather(late), functools.partial(as_weights, late)),
                                              reduce_early)
    loss = lax.psum(loss_part[0, 0], ('x', 'y', 'c'))

    g_layers = {n: [None] * (len(grads[n]) if n in BLOCKED else 1) for n in PARAM_NAMES}
    for (n, layer), g in zip(early['owners'], _reduce_end(early['qs'], crossed, 'early')):
        g_layers[n][layer] = g
    in_a, misc = _pack_late(grads, shards)
    qs = _reduce_begin([in_a, misc, grads['w_memkv'][0]], [jnp.bfloat16, F32, jnp.bfloat16], 'late')
    in_a_sum, misc_sum, memkv_sum = _reduce_end(qs, _run_exchange(_send_exchange(qs), 'send_chips_late'), 'late')
    g_layers['w_in_a'][0] = in_a_sum[:, :shards['w_in_a'].shape[2]]
    g_layers['w_memkv'][0] = memkv_sum
    for n, g in _unpack_misc(misc_sum, shards).items():
        g_layers[n][0] = g

    results = {}
    for name in PARAM_NAMES:
        w = shards[name]
        layers = len(g_layers[name])
        as_layers = (layers, -1, w.shape[-1])
        w3, m3, v3 = (t.reshape(as_layers) for t in (w, moments_m[name], moments_v[name]))
        res = None
        for layer, g in enumerate(g_layers[name]):
            res = _adamw(w3, g, m3, v3, layer, res, name=f'adamw_{name}_{layer}')
        results[name] = [t.reshape(w.shape) for t in res]

    return (loss, grad_x, *[results[n][k] for k in range(4) for n in PARAM_NAMES])
```

```python
import functools
import math

import jax
import jax.numpy as jnp
from jax import lax
from jax.experimental import pallas as pl
from jax.experimental.pallas import tpu as pltpu

F32 = jnp.float32
MXU_DTYPE = jnp.bfloat16
ACT_DTYPE = jnp.bfloat16

HEAD_DIM = 64
N_MAIN_HEADS = 12
N_MEM_HEADS = 4
MAIN_WIDTH = N_MAIN_HEADS * HEAD_DIM
MEM_WIDTH = N_MEM_HEADS * HEAD_DIM
EPS = 1e-6
SCALE = HEAD_DIM ** -0.5
NEG_BIG = -1e30
LANES = 128
PACK_COLS = 1024
N_CHIPS = 4

ADAM_LR = 0.001
ADAM_B1 = 0.9
ADAM_B2 = 0.999
ADAM_EPS = 1e-08
ADAM_WD = 0.01
ADAM_STEP = 10

MESH = pl.DeviceIdType.MESH
ANY = pl.BlockSpec(memory_space=pl.ANY)

PARAM_SHARD_AXIS = {
    'ln_mix_g': None, 'w_in_a': 2, 'b_f_a': None, 'w_in_b': 1, 'ln_kv_g': None, 'w_kv': 1,
    'ln_mem_g': None, 'w_memkv': 1, 'w_out': 1, 'ln_ffn_g': None, 'w_up': 2, 'conv_w': 2,
    'conv_b': None, 'w_down': 1, 'final_g': None,
}
PARAM_NAMES = list(PARAM_SHARD_AXIS)
F32_GATHERED = ('conv_w',)


def _tile(n, pref, unit=LANES):
    if n <= pref:
        return n
    best = None
    for t in range(unit, pref + 1, unit):
        if n % t == 0:
            best = t
    assert best is not None, (n, pref)
    return best


MM_ACC_ELEMS = 768 * 1024
MM_VMEM_MB = 56
MM_TILE_BYTES = 42 << 20


def _out_tiles(M, N):
    def divisors(n, cap):
        if n <= LANES:
            return [n]
        return [t for t in range(LANES, min(n, cap) + 1, LANES) if n % t == 0]

    best = None
    for tm in divisors(M, 1536):
        for tn in divisors(N, 2048):
            if tm * tn <= MM_ACC_ELEMS and (best is None or (tm * tn, tn) > (best[0] * best[1], best[1])):
                best = (tm, tn)
    assert best is not None, (M, N)
    return best


def _params(*sem, vmem_mb=None):
    kw = {}
    if sem:
        kw['dimension_semantics'] = sem
    if vmem_mb is not None:
        kw['vmem_limit_bytes'] = vmem_mb * 1024 * 1024
    return pltpu.CompilerParams(**kw)


def _dot(a, b, dims):
    return lax.dot_general(a.astype(MXU_DTYPE), b.astype(MXU_DTYPE), (dims, ((), ())),
                           preferred_element_type=F32)


NN = ((1,), (0,))
NT = ((1,), (1,))
TN = ((0,), (0,))


def _matmul(a, b, mode, out_dtype, res=None, slots=1, name='mm'):
    if mode == 'nn':
        (M, K), (K2, N) = a.shape, b.shape
    elif mode == 'nt':
        (M, K), (N, K2) = a.shape, b.shape
    else:
        (K, M), (K2, N) = a.shape, b.shape
    assert K == K2 and N % slots == 0, (a.shape, b.shape, mode, slots)
    slot_cols = N // slots
    tm, tn = _out_tiles(M, slot_cols)
    per_slot = slot_cols // tn
    fixed = tm * tn * (4 + 2 * jnp.dtype(out_dtype).itemsize + (8 if res is not None else 0))
    per_k = 2 * (tm * a.dtype.itemsize + tn * b.dtype.itemsize)
    tk = _tile(K, max(LANES, (MM_TILE_BYTES - fixed) // per_k))
    nk = K // tk
    dims = {'nn': NN, 'nt': NT, 'tn': TN}[mode]
    a_again = a.size * a.dtype.itemsize * (N // tn)
    b_again = b.size * b.dtype.itemsize * (M // tm)
    m_inner = nk == 1 and a_again < b_again

    def body(*refs):
        if res is None:
            (a_ref, b_ref, o_ref), r_ref = refs[:3], None
        else:
            a_ref, b_ref, r_ref, o_ref = refs[:4]

        def finish(out):
            if r_ref is not None:
                out = out + r_ref[...]
            o_ref[...] = out.astype(out_dtype)

        if nk == 1:
            finish(_dot(a_ref[...], b_ref[...], dims))
            return
        acc = refs[-1]
        k = pl.program_id(2)

        @pl.when(k == 0)
        def _():
            acc[...] = jnp.zeros_like(acc)

        acc[...] += _dot(a_ref[...], b_ref[...], dims)

        @pl.when(k == nk - 1)
        def _():
            finish(acc[...])

    def spec(shape, index):
        return pl.BlockSpec(shape, (lambda j, i, k: index(i, j, k)) if m_inner else index)

    a_spec = spec((tk, tm), lambda i, j, k: (k, i)) if mode == 'tn' else spec((tm, tk), lambda i, j, k: (i, k))
    b_spec = spec((tn, tk), lambda i, j, k: (j, k)) if mode == 'nt' else spec((tk, tn), lambda i, j, k: (k, j))
    if slots == 1:
        o_spec = spec((tm, tn), lambda i, j, k: (i, j))
        out_shape = jax.ShapeDtypeStruct((M, N), out_dtype)
    else:
        assert res is None
        o_spec = spec((None, tm, tn), lambda i, j, k: (j // per_slot, i, j % per_slot))
        out_shape = jax.ShapeDtypeStruct((slots, M, slot_cols), out_dtype)
    in_specs = [a_spec, b_spec] + ([o_spec] if res is not None else [])
    args = (a, b) + ((res,) if res is not None else ())
    return pl.pallas_call(
        body, name=name, grid=(N // tn, M // tm, nk) if m_inner else (M // tm, N // tn, nk),
        in_specs=in_specs, out_specs=o_spec,
        out_shape=out_shape,
        scratch_shapes=[] if nk == 1 else [pltpu.VMEM((tm, tn), F32)],
        compiler_params=_params('parallel', 'parallel', 'arbitrary', vmem_mb=MM_VMEM_MB),
    )(*args)


def _rms_fwd(x, g, name):
    T, D = x.shape
    tr = _tile(T, 512)

    def body(x_ref, g_ref, o_ref):
        xv = x_ref[...]
        r = lax.rsqrt(jnp.mean(xv * xv, axis=-1, keepdims=True) + EPS)
        o_ref[...] = (xv * r * g_ref[...]).astype(ACT_DTYPE)

    return pl.pallas_call(
        body, name=name, grid=(T // tr,),
        in_specs=[pl.BlockSpec((tr, D), lambda i: (i, 0)), pl.BlockSpec((1, D), lambda i: (0, 0))],
        out_specs=pl.BlockSpec((tr, D), lambda i: (i, 0)),
        out_shape=jax.ShapeDtypeStruct((T, D), ACT_DTYPE),
        compiler_params=_params('parallel'),
    )(x, g)


def _rms_bwd(x, g, dh, dres, name):
    T, D = x.shape
    tr = _tile(T, 512)
    want_dx = dres is not None

    def body(*refs):
        if want_dx:
            x_ref, g_ref, dh_ref, dres_ref, dx_ref, dg_ref = refs
        else:
            x_ref, g_ref, dh_ref, dg_ref = refs
        i = pl.program_id(0)

        @pl.when(i == 0)
        def _():
            dg_ref[...] = jnp.zeros_like(dg_ref)

        xv = x_ref[...]
        dhv = dh_ref[...].astype(F32)
        r = lax.rsqrt(jnp.mean(xv * xv, axis=-1, keepdims=True) + EPS)
        n = xv * r
        dg_ref[...] += jnp.sum(dhv * n, axis=0, keepdims=True)
        if want_dx:
            dn = dhv * g_ref[...]
            dx = r * (dn - n * jnp.mean(dn * n, axis=-1, keepdims=True))
            dx_ref[...] = dres_ref[...] + dx

    row = pl.BlockSpec((tr, D), lambda i: (i, 0))
    vec = pl.BlockSpec((1, D), lambda i: (0, 0))
    if want_dx:
        return pl.pallas_call(
            body, name=name, grid=(T // tr,),
            in_specs=[row, vec, row, row], out_specs=[row, vec],
            out_shape=[jax.ShapeDtypeStruct((T, D), F32), jax.ShapeDtypeStruct((1, D), F32)],
            compiler_params=_params('arbitrary'),
        )(x, g, dh, dres)
    dg = pl.pallas_call(
        body, name=name, grid=(T // tr,),
        in_specs=[row, vec, row], out_specs=vec,
        out_shape=jax.ShapeDtypeStruct((1, D), F32),
        compiler_params=_params('arbitrary'),
    )(x, g, dh)
    return None, dg


def _final_loss(x, g, tgt, name='final_loss'):
    T, D = x.shape
    tr = _tile(T, 512)

    def body(x_ref, g_ref, t_ref, loss_ref, dx_ref, dg_ref):
        i = pl.program_id(0)

        @pl.when(i == 0)
        def _():
            loss_ref[...] = jnp.zeros_like(loss_ref)
            dg_ref[...] = jnp.zeros_like(dg_ref)

        xv = x_ref[...]
        gv = g_ref[...]
        r = lax.rsqrt(jnp.mean(xv * xv, axis=-1, keepdims=True) + EPS)
        n = xv * r
        e = n * gv - t_ref[...]
        per_tok = jnp.mean(e * e, axis=-1, keepdims=True)
        loss_ref[...] += 0.5 * jnp.sum(per_tok, axis=0, keepdims=True)
        dy = e * (1.0 / D)
        dg_ref[...] += jnp.sum(dy * n, axis=0, keepdims=True)
        dn = dy * gv
        dx_ref[...] = r * (dn - n * jnp.mean(dn * n, axis=-1, keepdims=True))

    row = pl.BlockSpec((tr, D), lambda i: (i, 0))
    vec = pl.BlockSpec((1, D), lambda i: (0, 0))
    one = pl.BlockSpec((1, 1), lambda i: (0, 0))
    return pl.pallas_call(
        body, name=name, grid=(T // tr,),
        in_specs=[row, vec, row], out_specs=[one, row, vec],
        out_shape=[jax.ShapeDtypeStruct((1, 1), F32), jax.ShapeDtypeStruct((T, D), F32),
                   jax.ShapeDtypeStruct((1, D), F32)],
        compiler_params=_params('arbitrary'),
    )(x, g, tgt)


def _log_sigmoid(z):
    return jnp.minimum(z, 0.0) - jnp.log(1.0 + jnp.exp(-jnp.abs(z)))


def _tri(n, rel):
    j = lax.broadcasted_iota(jnp.int32, (n, n), 0)
    s = lax.broadcasted_iota(jnp.int32, (n, n), 1)
    return rel(j, s).astype(MXU_DTYPE)


def _split_dot(x, tri, terms):
    if MXU_DTYPE == F32:
        return jnp.dot(x, tri, preferred_element_type=F32), jnp.sum(x, axis=-1, keepdims=True)
    out = taken = None
    rem = x
    for _ in range(terms):
        piece = rem.astype(MXU_DTYPE)
        back = piece.astype(F32)
        part = jnp.dot(piece, tri, preferred_element_type=F32)
        rows = jnp.sum(back, axis=-1, keepdims=True)
        out, taken = (part, rows) if out is None else (out + part, taken + rows)
        rem = rem - back
    return out, taken


def _running_sums(x, tri, terms, earlier):
    w = tri.shape[0]
    parts = [_split_dot(x[:, at:at + w], tri, terms) for at in range(0, x.shape[1], w)]
    out = []
    for n, (r, _) in enumerate(parts):
        for _, whole in (parts[:n] if earlier else parts[n + 1:]):
            r = r + whole
        out.append(r)
    return jnp.concatenate(out, axis=1)


def _gate_fwd(zt, bcol, name='gate_fwd'):
    BH, S = zt.shape
    nb = S // LANES

    def body(z_ref, b_ref, c_ref):
        tri = _tri(LANES, lambda j, s: j <= s)
        carry = jnp.zeros((BH, 1), F32)
        for i in range(nb):
            sl = slice(i * LANES, (i + 1) * LANES)
            logf = _log_sigmoid(z_ref[:, sl] + b_ref[...])
            cs = _split_dot(logf, tri, 3)[0] + carry
            c_ref[:, sl] = cs
            carry = cs[:, LANES - 1:LANES]

    return pl.pallas_call(body, name=name, out_shape=jax.ShapeDtypeStruct((BH, S), F32))(zt, bcol)


def _gate_bwd(zt, bcol, dc, name='gate_bwd'):
    BH, S = zt.shape
    nb = S // LANES

    def body(z_ref, b_ref, dc_ref, dz_ref, db_ref):
        tri = _tri(LANES, lambda j, s: j >= s)
        carry = jnp.zeros((BH, 1), F32)
        dsum = jnp.zeros((BH, 1), F32)
        for i in reversed(range(nb)):
            sl = slice(i * LANES, (i + 1) * LANES)
            rs = _split_dot(dc_ref[:, sl], tri, 3)[0] + carry
            carry = rs[:, 0:1]
            z = z_ref[:, sl] + b_ref[...]
            dz = rs * (1.0 - 1.0 / (1.0 + jnp.exp(-z)))
            dz_ref[:, sl] = dz
            dsum = dsum + jnp.sum(dz, axis=-1, keepdims=True)
        db_ref[...] = dsum

    return pl.pallas_call(
        body, name=name,
        out_shape=[jax.ShapeDtypeStruct((BH, S), F32), jax.ShapeDtypeStruct((BH, 1), F32)],
    )(zt, bcol, dc)


FOX_ROWS, FOX_KEYS = 256, 512


PAIR = 2 * HEAD_DIM
N_MAIN_PAIRS = N_MAIN_HEADS // 2
N_MEM_PAIRS = N_MEM_HEADS // 2


def _lane0(shape):
    return lax.broadcasted_iota(jnp.int32, shape, len(shape) - 1) < HEAD_DIM


def _per_head(x):
    first = _lane0(x.shape)
    zero = jnp.zeros_like(x)
    return jnp.where(first, x, zero), jnp.where(first, zero, x)


def _pick(first, a, b):
    return jnp.where(first, a, b)


GROUP = 2
MAIN_STEPS = N_MAIN_PAIRS // GROUP


def _lanes(p):
    return slice(p * PAIR, (p + 1) * PAIR)


def _q_spec(bq, nq, off, group=1):
    assert off % group == 0
    return pl.BlockSpec((bq, group * PAIR), lambda b, j, i: (b * nq + i, off // group + j))


def _seq_spec(S, off, group=1):
    assert off % group == 0
    return pl.BlockSpec((S, group * PAIR), lambda b, j, i: (b, off // group + j))


def _gate_specs(bq, nk, bk):
    col = pl.BlockSpec((2 * GROUP, bq, 1), lambda b, j, i: (b * MAIN_STEPS + j, i, 0))
    rowv = pl.BlockSpec((2 * GROUP, nk, 1, bk), lambda b, j, i: (b * MAIN_STEPS + j, 0, 0, 0))
    return col, rowv


def _blocks(S, rows, keys):
    bq, bk = min(rows, S), min(keys, S)
    assert bk % bq == 0 and S % bk == 0
    return bq, bk


def _last_block(i, bq, bk, strict, step, carry):
    per = bk // bq
    last = i // per

    def mask(keys, shift):
        row = lax.broadcasted_iota(jnp.int32, (bq, keys), 0) + shift
        col = lax.broadcasted_iota(jnp.int32, (bq, keys), 1)
        return (col < row) if strict else (col <= row)

    if per == 1:
        return step(last, carry, mask(bk, 0), bk)
    assert per == 2
    return lax.cond(i % 2 == 0, lambda c: step(last, c, mask(bq, 0), bq), lambda c: step(last, c, mask(bk, bq), bk),
                    carry)


def _scaled(q):
    assert math.log2(SCALE).is_integer()
    return q * jnp.asarray(SCALE, q.dtype)


def _fox_fwd(qkv, offs, B, S, ccol, crow, beside=None, name='fox_fwd'):
    bq, bk = _blocks(S, FOX_ROWS, FOX_KEYS)
    nq = S // bq

    def body(q_ref, k_ref, v_ref, cc_ref, cr_ref, o_ref, lse_ref):
        i = pl.program_id(2)
        qv = _scaled(q_ref[...])
        qh = [_per_head(qv[:, _lanes(p)]) for p in range(GROUP)]
        first = _lane0((bq, PAIR))

        def step(kb, carry, mask=None, keys=bk):
            m, l, acc = carry
            sl = pl.ds(pl.multiple_of(kb * bk, bk), keys)
            m_new, l_new, acc_new = [], [], []
            for p in range(GROUP):
                ks, vs = k_ref[sl, _lanes(p)], v_ref[sl, _lanes(p)]
                alpha, pv = [], []
                for h in range(2):
                    n = 2 * p + h
                    s = _dot(qh[p][h], ks, NT) + cc_ref[n] - cr_ref[n, kb][:, :keys]
                    if mask is not None:
                        s = jnp.where(mask, s, NEG_BIG)
                    mh = jnp.maximum(m[n], jnp.max(s, axis=-1, keepdims=True))
                    pr = jnp.exp(s - mh)
                    ah = jnp.exp(m[n] - mh)
                    m_new.append(mh)
                    alpha.append(ah)
                    l_new.append(ah * l[n] + jnp.sum(pr, axis=-1, keepdims=True))
                    pv.append(_dot(pr, vs, NN))
                acc_new.append(_pick(first, alpha[0], alpha[1]) * acc[p] + _pick(first, pv[0], pv[1]))
            return tuple(m_new), tuple(l_new), tuple(acc_new)

        negs = tuple(jnp.full((bq, 1), NEG_BIG, F32) for _ in range(2 * GROUP))
        zeros = tuple(jnp.zeros((bq, 1), F32) for _ in range(2 * GROUP))
        acc0 = tuple(jnp.zeros((bq, PAIR), F32) for _ in range(GROUP))
        carry = lax.fori_loop(0, i // (bk // bq), step, (negs, zeros, acc0))
        m, l, acc = _last_block(i, bq, bk, False, step, carry)
        for p in range(GROUP):
            o_ref[:, _lanes(p)] = (acc[p] / _pick(first, l[2 * p], l[2 * p + 1])).astype(ACT_DTYPE)
        for n in range(2 * GROUP):
            lse_ref[n] = m[n] + jnp.log(l[n])

    col, rowv = _gate_specs(bq, S // bk, bk)
    return _call_beside(
        body, name, (B, MAIN_STEPS, nq),
        [_q_spec(bq, nq, offs[0], GROUP), _seq_spec(S, offs[1], GROUP), _seq_spec(S, offs[2], GROUP), col, rowv],
        [_q_spec(bq, nq, 0, GROUP), col],
        [jax.ShapeDtypeStruct((B * S, MAIN_WIDTH), ACT_DTYPE), jax.ShapeDtypeStruct((B * N_MAIN_HEADS, S, 1), F32)],
        [], (*qkv, ccol, crow), ('parallel', 'parallel', 'arbitrary'), beside)


def _fox_bwd(qkv, offs, B, S, ccol, crow, o, lse, do, beside=None, name='fox_bwd'):
    bq, bk = _blocks(S, FOX_ROWS, FOX_KEYS)
    nq = S // bq

    def body(q_ref, k_ref, v_ref, cc_ref, cr_ref, o_ref, lse_ref, do_ref,
             dq_ref, dk_ref, dv_ref, dcc_ref, dcr_ref, dk_acc, dv_acc):
        i = pl.program_id(2)

        @pl.when(i == 0)
        def _():
            dk_acc[...] = jnp.zeros_like(dk_acc)
            dv_acc[...] = jnp.zeros_like(dv_acc)
            dcr_ref[...] = jnp.zeros_like(dcr_ref)

        qv = _scaled(q_ref[...])
        dov = do_ref[...]
        qp = [qv[:, _lanes(p)] for p in range(GROUP)]
        dop = [dov[:, _lanes(p)] for p in range(GROUP)]
        qh = [_per_head(t) for t in qp]
        doh = [_per_head(t) for t in dop]
        first = _lane0((bq, PAIR))
        prod = dov.astype(F32) * o_ref[...].astype(F32)
        dsum = [jnp.sum(t, axis=-1, keepdims=True) for p in range(GROUP) for t in _per_head(prod[:, _lanes(p)])]

        def step(kb, carry, mask=None, keys=bk):
            dq, dcc = carry
            sl = pl.ds(pl.multiple_of(kb * bk, bk), keys)
            first_k = _lane0((keys, PAIR))
            dq_new, dcc_new = [], []
            for p in range(GROUP):
                ks, vs = k_ref[sl, _lanes(p)], v_ref[sl, _lanes(p)]
                dqh, dkh, dvh = [], [], []
                for h in range(2):
                    n = 2 * p + h
                    s = _dot(qh[p][h], ks, NT) + cc_ref[n] - cr_ref[n, kb][:, :keys]
                    pr = jnp.exp(s - lse_ref[n])
                    if mask is not None:
                        pr = jnp.where(mask, pr, 0.0)
                    ds = pr * (_dot(doh[p][h], vs, NT) - dsum[n])
                    dqh.append(_dot(ds, ks, NN))
                    dkh.append(_dot(ds, qp[p], TN))
                    dvh.append(_dot(pr, dop[p], TN))
                    as_key = jnp.sum(ds, axis=0, keepdims=True)
                    if keys < bk:
                        as_key = jnp.concatenate([as_key, jnp.zeros((1, bk - keys), F32)], axis=1)
                    dcr_ref[n, kb] -= as_key
                    dcc_new.append(dcc[n] + jnp.sum(ds, axis=-1, keepdims=True))
                dk_acc[sl, _lanes(p)] += _pick(first_k, dkh[0], dkh[1])
                dv_acc[sl, _lanes(p)] += _pick(first_k, dvh[0], dvh[1])
                dq_new.append(dq[p] + _pick(first, dqh[0], dqh[1]))
            return tuple(dq_new), tuple(dcc_new)

        zeros = tuple(jnp.zeros((bq, 1), F32) for _ in range(2 * GROUP))
        dq0 = tuple(jnp.zeros((bq, PAIR), F32) for _ in range(GROUP))
        dq, dcc = _last_block(i, bq, bk, False, step, lax.fori_loop(0, i // (bk // bq), step, (dq0, zeros)))
        for p in range(GROUP):
            dq_ref[:, _lanes(p)] = (dq[p] * SCALE).astype(ACT_DTYPE)
        for n in range(2 * GROUP):
            dcc_ref[n] = dcc[n]

        @pl.when(i == nq - 1)
        def _():
            dk_ref[...] = dk_acc[...].astype(ACT_DTYPE)
            dv_ref[...] = dv_acc[...].astype(ACT_DTYPE)

    col, rowv = _gate_specs(bq, S // bk, bk)
    qs, seq = _q_spec(bq, nq, 0, GROUP), _seq_spec(S, 0, GROUP)
    full = jax.ShapeDtypeStruct((B * S, MAIN_WIDTH), ACT_DTYPE)
    wide = pltpu.VMEM((S, GROUP * PAIR), F32)
    return _call_beside(
        body, name, (B, MAIN_STEPS, nq),
        [_q_spec(bq, nq, offs[0], GROUP), _seq_spec(S, offs[1], GROUP), _seq_spec(S, offs[2], GROUP), col, rowv,
         qs, col, qs],
        [qs, seq, seq, col, rowv],
        [full, full, full, jax.ShapeDtypeStruct(ccol.shape, F32), jax.ShapeDtypeStruct(crow.shape, F32)],
        [wide, wide],
        (*qkv, ccol, crow, o, lse, do), ('parallel', 'parallel', 'arbitrary'), beside)


SB_ROWS, SB_KEYS = 256, 512
SB_TRIANGLE = 256


SB_SUM_TERMS = 2


def _sb_block(q_scaled, ks, mask):
    z = _dot(q_scaled, ks, NT)
    a = _log_sigmoid(z)
    l = a - z
    return a, (l if mask is None else jnp.where(mask, l, 0.0))


def _sb_fwd(qkv, offs, B, S, name='sb_fwd'):
    bq, bk = _blocks(S, SB_ROWS, SB_KEYS)
    nq = S // bq

    def body(q_ref, k_ref, v_ref, o_ref, tot_ref):
        i = pl.program_id(2)
        qv = _scaled(q_ref[...])
        qh = [_per_head(qv[:, _lanes(p)]) for p in range(GROUP)]
        first = _lane0((bq, PAIR))
        tri = _tri(min(bk, SB_TRIANGLE), lambda j, s: j > s)

        def step(kb, carry, mask=None, keys=bk):
            acc, right = carry
            sl = pl.ds(pl.multiple_of(kb * bk, bk), keys)
            acc_new, right_new = [], []
            for p in range(GROUP):
                ks, vs = k_ref[sl, _lanes(p)], v_ref[sl, _lanes(p)]
                pv = []
                for h in range(2):
                    n = 2 * p + h
                    a, l = _sb_block(qh[p][h], ks, mask)
                    w = jnp.exp(a + _running_sums(l, tri, SB_SUM_TERMS, False) + right[n])
                    if mask is not None:
                        w = jnp.where(mask, w, 0.0)
                    pv.append(_dot(w, vs, NN))
                    right_new.append(right[n] + jnp.sum(l, axis=-1, keepdims=True))
                acc_new.append(acc[p] + _pick(first, pv[0], pv[1]))
            return tuple(acc_new), tuple(right_new)

        zeros = tuple(jnp.zeros((bq, 1), F32) for _ in range(2 * GROUP))
        acc0 = tuple(jnp.zeros((bq, PAIR), F32) for _ in range(GROUP))
        last = i // (bk // bq)
        carry = _last_block(i, bq, bk, True, step, (acc0, zeros))
        acc, total = lax.fori_loop(0, last, lambda n, c: step(last - 1 - n, c), carry)
        for p in range(GROUP):
            o_ref[:, _lanes(p)] = acc[p].astype(ACT_DTYPE)
        for n in range(2 * GROUP):
            tot_ref[n] = total[n]

    col, _ = _gate_specs(bq, S // bk, bk)
    return pl.pallas_call(
        body, name=name, grid=(B, MAIN_STEPS, nq),
        in_specs=[_q_spec(bq, nq, offs[0], GROUP), _seq_spec(S, offs[1], GROUP), _seq_spec(S, offs[2], GROUP)],
        out_specs=[_q_spec(bq, nq, 0, GROUP), col],
        out_shape=[jax.ShapeDtypeStruct((B * S, MAIN_WIDTH), ACT_DTYPE),
                   jax.ShapeDtypeStruct((B * N_MAIN_HEADS, S, 1), F32)],
        compiler_params=_params('parallel', 'parallel', 'arbitrary'),
    )(*qkv)


def _sb_bwd(qkv, offs, B, S, tot, do, name='sb_bwd'):
    bq, bk = _blocks(S, SB_ROWS, SB_KEYS)
    nq = S // bq

    def body(q_ref, k_ref, v_ref, tot_ref, do_ref, dq_ref, dk_ref, dv_ref, dk_acc, dv_acc):
        i = pl.program_id(2)

        @pl.when(i == 0)
        def _():
            dk_acc[...] = jnp.zeros_like(dk_acc)
            dv_acc[...] = jnp.zeros_like(dv_acc)

        qv = _scaled(q_ref[...])
        dov = do_ref[...]
        qp = [qv[:, _lanes(p)] for p in range(GROUP)]
        dop = [dov[:, _lanes(p)] for p in range(GROUP)]
        qh = [_per_head(t) for t in qp]
        doh = [_per_head(t) for t in dop]
        heads = [(p, h) for p in range(GROUP) for h in range(2)]
        first = _lane0((bq, PAIR))
        tri_incl = _tri(min(bk, SB_TRIANGLE), lambda j, s: j <= s)
        tri_excl = _tri(min(bk, SB_TRIANGLE), lambda j, s: j < s)
        zeros = tuple(jnp.zeros((bq, 1), F32) for _ in heads)
        tot = tuple(tot_ref[n] for n in range(len(heads)))

        def step(kb, carry, mask=None, keys=bk):
            dq, rest_l, left_g = carry
            sl = pl.ds(pl.multiple_of(kb * bk, bk), keys)
            first_k = _lane0((keys, PAIR))
            new_dq, new_l, new_g = [], [], []
            for p in range(GROUP):
                ks, vs = k_ref[sl, _lanes(p)], v_ref[sl, _lanes(p)]
                dqh, dkh, dvh = [], [], []
                for h in range(2):
                    n = 2 * p + h
                    a, l = _sb_block(qh[p][h], ks, mask)
                    w = jnp.exp(a - _running_sums(l, tri_incl, SB_SUM_TERMS, True) + rest_l[n])
                    if mask is not None:
                        w = jnp.where(mask, w, 0.0)
                    g = w * _dot(doh[p][h], vs, NT)
                    beta = jnp.exp(a)
                    dz = g - beta * (g + _running_sums(g, tri_excl, 1, True) + left_g[n])
                    if mask is not None:
                        dz = jnp.where(mask, dz, 0.0)
                    dqh.append(_dot(dz, ks, NN))
                    dkh.append(_dot(dz, qp[p], TN))
                    dvh.append(_dot(w, dop[p], TN))
                    new_l.append(rest_l[n] - jnp.sum(l, axis=-1, keepdims=True))
                    new_g.append(left_g[n] + jnp.sum(g, axis=-1, keepdims=True))
                dk_acc[sl, _lanes(p)] += _pick(first_k, dkh[0], dkh[1])
                dv_acc[sl, _lanes(p)] += _pick(first_k, dvh[0], dvh[1])
                new_dq.append(dq[p] + _pick(first, dqh[0], dqh[1]))
            return tuple(new_dq), tuple(new_l), tuple(new_g)

        dq0 = tuple(jnp.zeros((bq, PAIR), F32) for _ in range(GROUP))
        dq, _, _ = _last_block(i, bq, bk, True, step, lax.fori_loop(0, i // (bk // bq), step, (dq0, tot, zeros)))
        for p in range(GROUP):
            dq_ref[:, _lanes(p)] = (dq[p] * SCALE).astype(ACT_DTYPE)

        @pl.when(i == nq - 1)
        def _():
            dk_ref[...] = dk_acc[...].astype(ACT_DTYPE)
            dv_ref[...] = dv_acc[...].astype(ACT_DTYPE)

    qs, seq = _q_spec(bq, nq, 0, GROUP), _seq_spec(S, 0, GROUP)
    full = jax.ShapeDtypeStruct((B * S, MAIN_WIDTH), ACT_DTYPE)
    wide = pltpu.VMEM((S, GROUP * PAIR), F32)
    col, _ = _gate_specs(bq, S // bk, bk)
    return pl.pallas_call(
        body, name=name, grid=(B, MAIN_STEPS, nq),
        in_specs=[_q_spec(bq, nq, offs[0], GROUP), _seq_spec(S, offs[1], GROUP), _seq_spec(S, offs[2], GROUP), col,
                  qs],
        out_specs=[qs, seq, seq], out_shape=[full, full, full],
        scratch_shapes=[wide, wide],
        compiler_params=_params('parallel', 'parallel', 'arbitrary'),
    )(*qkv, tot, do)


def _mem_probs(qv, mk):
    s = _dot(qv, mk, NT) * SCALE
    p = jnp.exp(s - jnp.max(s, axis=-1, keepdims=True))
    return p / jnp.sum(p, axis=-1, keepdims=True)


def _mem_fwd(q, q_off, mkv, B, S, name='mem_fwd'):
    M = mkv.shape[0] // B
    bq = _tile(S, 512)
    nq = S // bq

    def body(q_ref, mk_ref, mv_ref, o_ref):
        first = _lane0((bq, PAIR))
        mk, mv = mk_ref[...], mv_ref[...]
        out = [_dot(_mem_probs(qh, mk), mv, NN) for qh in _per_head(q_ref[...])]
        o_ref[...] = _pick(first, out[0], out[1]).astype(ACT_DTYPE)

    return pl.pallas_call(
        body, name=name, grid=(B, N_MEM_PAIRS, nq),
        in_specs=[_q_spec(bq, nq, q_off), _seq_spec(M, 0), _seq_spec(M, N_MEM_PAIRS)],
        out_specs=_q_spec(bq, nq, 0),
        out_shape=jax.ShapeDtypeStruct((B * S, MEM_WIDTH), ACT_DTYPE),
        compiler_params=_params('parallel', 'parallel', 'parallel'),
    )(q, mkv, mkv)


def _mem_bwd(q, q_off, mkv, B, S, do, do_off, name='mem_bwd'):
    M = mkv.shape[0] // B
    bq = _tile(S, 512)
    nq = S // bq

    def body(q_ref, mk_ref, mv_ref, do_ref, dq_ref, dmk_ref, dmv_ref):
        i = pl.program_id(2)

        @pl.when(i == 0)
        def _():
            dmk_ref[...] = jnp.zeros_like(dmk_ref)
            dmv_ref[...] = jnp.zeros_like(dmv_ref)

        qv = q_ref[...]
        dov = do_ref[...]
        mk, mv = mk_ref[...], mv_ref[...]
        first = _lane0((bq, PAIR))
        first_m = _lane0((M, PAIR))
        dqh, dkh, dvh = [], [], []
        for qh, doh in zip(_per_head(qv), _per_head(dov)):
            p = _mem_probs(qh, mk)
            dp = _dot(doh, mv, NT)
            ds = p * (dp - jnp.sum(p * dp, axis=-1, keepdims=True))
            dqh.append(_dot(ds, mk, NN))
            dkh.append(_dot(ds, qv, TN))
            dvh.append(_dot(p, dov, TN))
        dq_ref[...] = (SCALE * _pick(first, dqh[0], dqh[1])).astype(ACT_DTYPE)
        dmk_ref[...] += SCALE * _pick(first_m, dkh[0], dkh[1])
        dmv_ref[...] += _pick(first_m, dvh[0], dvh[1])

    mem_out = jax.ShapeDtypeStruct((B * M, MEM_WIDTH), F32)
    return pl.pallas_call(
        body, name=name, grid=(B, N_MEM_PAIRS, nq),
        in_specs=[_q_spec(bq, nq, q_off), _seq_spec(M, 0), _seq_spec(M, N_MEM_PAIRS), _q_spec(bq, nq, do_off)],
        out_specs=[_q_spec(bq, nq, 0), _seq_spec(M, 0), _seq_spec(M, 0)],
        out_shape=[jax.ShapeDtypeStruct((B * S, MEM_WIDTH), ACT_DTYPE), mem_out, mem_out],
        compiler_params=_params('parallel', 'parallel', 'arbitrary'),
    )(q, mkv, mkv, do)


HALO = 8
CONV_CHUNK = 256


def _conv_chunk(scr, start, rows, w, b):
    at = HALO + start
    return (b + w[0:1, :] * scr[at - 2:at - 2 + rows, :] + w[1:2, :] * scr[at - 1:at - 1 + rows, :]
            + w[2:3, :] * scr[at:at + rows, :])


def _fill_frames(scr, ref):
    scr[0:HALO, :] = jnp.zeros((HALO, scr.shape[1]), F32)
    scr[HALO:, :] = ref[...].astype(F32)


def _sigmoid(x):
    return 0.5 + 0.5 * jnp.tanh(0.5 * x)


def _fold8(x):
    return jnp.sum(x.reshape(x.shape[0] // 8, 8, x.shape[1]), axis=0)


def _conv_specs(S, nf):
    ug = pl.BlockSpec((None, S, LANES), lambda b, j: (b, 0, j))
    uv = pl.BlockSpec((None, S, LANES), lambda b, j: (b, 0, j + nf))
    wg = pl.BlockSpec((3, LANES), lambda b, j: (0, j))
    wv = pl.BlockSpec((3, LANES), lambda b, j: (0, j + nf))
    bg = pl.BlockSpec((1, LANES), lambda b, j: (0, j))
    bv = pl.BlockSpec((1, LANES), lambda b, j: (0, j + nf))
    return ug, uv, wg, wv, bg, bv


def _conv_fwd(u, cw, cb, name='conv_fwd'):
    B, S, F2 = u.shape
    F = F2 // 2
    nf = F // LANES

    ch = min(CONV_CHUNK, S)

    def body(ug_ref, uv_ref, wg_ref, wv_ref, bg_ref, bv_ref, y_ref, g_scr, v_scr):
        _fill_frames(g_scr, ug_ref)
        _fill_frames(v_scr, uv_ref)
        wg, wv, bg, bv = wg_ref[...], wv_ref[...], bg_ref[...], bv_ref[...]
        for start in range(0, S, ch):
            gate = _conv_chunk(g_scr, start, ch, wg, bg)
            val = _conv_chunk(v_scr, start, ch, wv, bv)
            y_ref[start:start + ch, :] = (gate * _sigmoid(gate) * val).astype(ACT_DTYPE)

    specs = _conv_specs(S, nf)
    frames = pltpu.VMEM((HALO + S, LANES), F32)
    return pl.pallas_call(
        body, name=name, grid=(B, nf), in_specs=list(specs), out_specs=specs[0],
        out_shape=jax.ShapeDtypeStruct((B, S, F), ACT_DTYPE), scratch_shapes=[frames, frames],
        compiler_params=_params('parallel', 'parallel'),
    )(u, u, cw, cw, cb, cb)


def _conv_bwd(u, cw, cb, dy, name='conv_bwd'):
    B, S, F2 = u.shape
    F = F2 // 2
    nf = F // LANES

    ch = min(CONV_CHUNK, S)

    def body(ug_ref, uv_ref, wg_ref, wv_ref, bg_ref, bv_ref, dy_ref,
             dug_ref, duv_ref, dwg_ref, dwv_ref, dbg_ref, dbv_ref, g_scr, v_scr, dg_scr, dv_scr):
        b = pl.program_id(1)

        @pl.when(b == 0)
        def _():
            for r in (dwg_ref, dwv_ref, dbg_ref, dbv_ref):
                r[...] = jnp.zeros_like(r)

        _fill_frames(g_scr, ug_ref)
        _fill_frames(v_scr, uv_ref)
        wg, wv, bg, bv = wg_ref[...], wv_ref[...], bg_ref[...], bv_ref[...]
        for scr in (dg_scr, dv_scr):
            scr[S:, :] = jnp.zeros((HALO, LANES), F32)
        for start in range(0, S, ch):
            gate = _conv_chunk(g_scr, start, ch, wg, bg)
            val = _conv_chunk(v_scr, start, ch, wv, bv)
            dyv = dy_ref[start:start + ch, :].astype(F32)
            sg = _sigmoid(gate)
            dv_scr[start:start + ch, :] = dyv * (gate * sg)
            dg_scr[start:start + ch, :] = dyv * val * (sg * (1.0 + gate * (1.0 - sg)))

        for u_scr, d_scr, w, du_ref, dw_ref, db_ref in ((g_scr, dg_scr, wg, dug_ref, dwg_ref, dbg_ref),
                                                         (v_scr, dv_scr, wv, duv_ref, dwv_ref, dbv_ref)):
            sums = [jnp.zeros((8, LANES), F32) for _ in range(4)]
            for start in range(0, S, ch):
                x = u_scr[HALO + start:HALO + start + ch, :]
                d = [d_scr[start + n:start + n + ch, :] for n in range(3)]
                du_ref[start:start + ch, :] = (w[2:3, :] * d[0] + w[1:2, :] * d[1] + w[0:1, :] * d[2]).astype(ACT_DTYPE)
                sums = [sums[0] + _fold8(x * d[2]), sums[1] + _fold8(x * d[1]), sums[2] + _fold8(x * d[0]),
                        sums[3] + _fold8(d[0])]
            total = [jnp.sum(s, axis=0, keepdims=True) for s in sums]
            dw_ref[...] += jnp.concatenate(total[:3], axis=0)
            db_ref[...] += total[3]

    def swap(spec_fn):
        return lambda j, b: spec_fn(b, j)

    ug, uv, wg, wv, bg, bv = _conv_specs(S, nf)
    ins = [pl.BlockSpec(s.block_shape, swap(s.index_map)) for s in (ug, uv, wg, wv, bg, bv, ug)]
    outs = [ins[0], ins[0], ins[2], ins[2], ins[4], ins[4]]
    frames = pltpu.VMEM((HALO + S, LANES), F32)
    return pl.pallas_call(
        body, name=name, grid=(nf, B), in_specs=ins, out_specs=outs, scratch_shapes=[frames] * 4,
        out_shape=[jax.ShapeDtypeStruct((B, S, F), ACT_DTYPE), jax.ShapeDtypeStruct((B, S, F), ACT_DTYPE),
                   jax.ShapeDtypeStruct((3, F), F32), jax.ShapeDtypeStruct((3, F), F32),
                   jax.ShapeDtypeStruct((1, F), F32), jax.ShapeDtypeStruct((1, F), F32)],
        compiler_params=_params('parallel', 'arbitrary'),
    )(u, u, cw, cw, cb, cb, dy)


ADAM_BLOCK_BYTES = 512 * 1024


def _adamw(w, g, m, v, layer, earlier, name):
    L, r, c = w.shape
    tr = r
    if r * c * 4 > ADAM_BLOCK_BYTES and r % 8 == 0:
        tr = 8
        for t in range(8, r + 1, 8):
            if r % t == 0 and t * c * 4 <= ADAM_BLOCK_BYTES:
                tr = t

    def body(w_ref, g_ref, m_ref, v_ref, *rest):
        go_ref, d_ref, nm_ref, nv_ref = rest[-4:]
        gv = g_ref[...]
        nm = ADAM_B1 * m_ref[...] + (1.0 - ADAM_B1) * gv
        nv = ADAM_B2 * v_ref[...] + (1.0 - ADAM_B2) * (gv * gv)
        m_hat = nm / (1.0 - ADAM_B1 ** ADAM_STEP)
        v_hat = nv / (1.0 - ADAM_B2 ** ADAM_STEP)
        d_ref[...] = -ADAM_LR * (m_hat / (jnp.sqrt(v_hat) + ADAM_EPS) + ADAM_WD * w_ref[...])
        nm_ref[...] = nm
        nv_ref[...] = nv
        go_ref[...] = gv

    lay = pl.BlockSpec((None, tr, c), lambda i: (layer, i, 0))
    one = pl.BlockSpec((tr, c), lambda i: (i, 0))
    shp = jax.ShapeDtypeStruct((L, r, c), F32)
    in_specs = [lay, one, lay, lay]
    args = (w, g, m, v)
    aliases = {}
    if earlier is not None:
        in_specs += [ANY] * 4
        args += tuple(earlier)
        aliases = {4 + k: k for k in range(4)}
    return pl.pallas_call(
        body, name=name, grid=(r // tr,), in_specs=in_specs, out_specs=[lay] * 4, out_shape=[shp] * 4,
        input_output_aliases=aliases, compiler_params=_params('parallel'),
    )(*args)


def _my_place():
    return lax.axis_index('x'), lax.axis_index('y'), lax.axis_index('c')


def _other_chips(x, y):
    return [(1 - x, y), (x, 1 - y), (1 - x, 1 - y)]


def _remote(src, dst, send_sem, recv_sem, to):
    return pltpu.make_async_remote_copy(src_ref=src, dst_ref=dst, send_sem=send_sem, recv_sem=recv_sem,
                                        device_id=to, device_id_type=MESH)


def _hbm_call(body, n_in, out_shapes, scratch, name, aliases=None):
    return pl.pallas_call(body, name=name, in_specs=[ANY] * n_in, out_specs=[ANY] * len(out_shapes),
                          out_shape=out_shapes, scratch_shapes=scratch, input_output_aliases=aliases or {})


def _full_shape(shard_shape, kind):
    L, r, c = shard_shape
    return {'rows': (L, N_CHIPS * r, c), 'cols': (L, r, N_CHIPS * c), 'stack': (N_CHIPS * L, r, c)}[kind]


def _place_block(w, kind, out_dtype, chip_arr, name):
    L, r, c = w.shape
    tr = r if r % 16 else _tile(r, max(16, SUM_BLOCK_BYTES // (4 * c)), 16)
    nrt = r // tr

    def body(k_ref, w_ref, o_ref):
        o_ref[...] = w_ref[...].astype(out_dtype)

    out_map = {'rows': lambda l, i, k_ref: (l, k_ref[0] * nrt + i, 0),
               'cols': lambda l, i, k_ref: (l, i, k_ref[0]),
               'stack': lambda l, i, k_ref: (k_ref[0] * L + l, i, 0)}[kind]
    gs = pltpu.PrefetchScalarGridSpec(
        num_scalar_prefetch=1, grid=(L, nrt),
        in_specs=[pl.BlockSpec((None, tr, c), lambda l, i, k_ref: (l, i, 0))],
        out_specs=pl.BlockSpec((None, tr, c), out_map))
    return pl.pallas_call(
        body, name=name, grid_spec=gs, out_shape=jax.ShapeDtypeStruct(_full_shape(w.shape, kind), out_dtype),
        compiler_params=_params('parallel', 'parallel'),
    )(chip_arr, w)


class _Exchange:
    def __init__(self, inputs, out_shapes, aliases, scratch, start, finish):
        self.inputs, self.out_shapes, self.aliases, self.scratch = list(inputs), list(out_shapes), aliases, scratch
        self.start, self.finish = start, finish


def _run_exchange(ex, name):
    n_in, n_out = len(ex.inputs), len(ex.out_shapes)

    def body(*refs):
        parts = refs[:n_in], refs[n_in:n_in + n_out], refs[n_in + n_out:]
        ex.start(*parts)
        ex.finish(*parts)

    return _hbm_call(body, n_in, ex.out_shapes, ex.scratch, name, aliases=ex.aliases)(*ex.inputs)


def _call_beside(body, name, grid, in_specs, out_specs, out_shape, scratch, args, semantics, beside):
    if beside is None:
        outs = pl.pallas_call(body, name=name, grid=grid, in_specs=in_specs, out_specs=out_specs, out_shape=out_shape,
                              scratch_shapes=scratch, compiler_params=_params(*semantics))(*args)
        return outs, None
    n_in, n_out, n_scr = len(in_specs), len(out_specs), len(scratch)
    b_in, b_out = len(beside.inputs), len(beside.out_shapes)

    def carrier(*refs):
        cuts = [n_in, b_in, n_out, b_out, n_scr]
        parts, at = [], 0
        for size in cuts:
            parts.append(refs[at:at + size])
            at += size
        ins, ex_ins, outs, ex_outs, scr = parts
        ex_scr = refs[at:]
        ids = [pl.program_id(d) for d in range(len(grid))]
        first = functools.reduce(jnp.logical_and, [i == 0 for i in ids])
        last = functools.reduce(jnp.logical_and, [i == g - 1 for i, g in zip(ids, grid)])

        @pl.when(first)
        def _():
            beside.start(ex_ins, ex_outs, ex_scr)

        body(*ins, *outs, *scr)

        @pl.when(last)
        def _():
            beside.finish(ex_ins, ex_outs, ex_scr)

    res = pl.pallas_call(
        carrier, name=name, grid=grid, in_specs=list(in_specs) + [ANY] * b_in,
        out_specs=list(out_specs) + [ANY] * b_out, out_shape=list(out_shape) + beside.out_shapes,
        scratch_shapes=list(scratch) + beside.scratch,
        input_output_aliases={n_in + i: n_out + o for i, o in beside.aliases.items()},
        compiler_params=_params(*['arbitrary'] * len(grid)),
    )(*args, *beside.inputs)
    return res[:n_out], res[n_out:]


def _gather_exchange(fulls, shard_shapes, kinds, split):
    n = len(fulls)

    def plan(outs, send_sems, recv_sems):
        x, y, c = _my_place()
        chip = 2 * x + y
        sibling = (x, y, 1 - c)
        others = _other_chips(x, y)

        def window(a, k, half):
            L, r, cols = shard_shapes[a]
            first, count = (0, r) if half is None else (half * (r // 2), r // 2)
            if kinds[a] == 'rows':
                return outs[a].at[:, pl.ds(k * r + first, count), :]
            if kinds[a] == 'cols':
                return outs[a].at[:, pl.ds(first, count), pl.ds(pl.multiple_of(k * cols, LANES), cols)]
            return outs[a].at[pl.ds(k * L, L), pl.ds(first, count), :]

        sends, arrivals, forwards, forwarded = [], [], [], []
        for a in range(n):
            half = c if split[a] else None
            for j, (ox, oy) in enumerate(others):
                sems = (send_sems.at[6 * a + j], recv_sems.at[6 * a + j], (ox, oy, c))
                sends.append(_remote(window(a, chip, half), window(a, chip, half), *sems))
                got = window(a, 2 * ox + oy, half)
                arrivals.append(_remote(got, got, *sems))
                if split[a]:
                    sems = (send_sems.at[6 * a + 3 + j], recv_sems.at[6 * a + 3 + j], sibling)
                    forwards.append(_remote(got, got, *sems))
                    theirs = window(a, 2 * ox + oy, 1 - c)
                    forwarded.append(_remote(theirs, theirs, *sems))
                else:
                    forwards.append(None)
        return sends, arrivals, forwards, forwarded

    def start(ins, outs, scratch):
        sends, _, _, _ = plan(outs, *scratch)
        for cp in sends:
            cp.start()

    def finish(ins, outs, scratch):
        sends, arrivals, forwards, forwarded = plan(outs, *scratch)
        for arrived, fw in zip(arrivals, forwards):
            arrived.wait_recv()
            if fw is not None:
                fw.start()
        for cp in forwarded:
            cp.wait_recv()
        for cp in sends + [fw for fw in forwards if fw is not None]:
            cp.wait_send()

    scratch = [pltpu.SemaphoreType.DMA((6 * n,)), pltpu.SemaphoreType.DMA((6 * n,))]
    out_shapes = [jax.ShapeDtypeStruct(f.shape, f.dtype) for f in fulls]
    return _Exchange(fulls, out_shapes, {a: a for a in range(n)}, scratch, start, finish)


def _swap_cores(gs, name='swap_cores'):
    n = len(gs)
    out_shapes = [jax.ShapeDtypeStruct((g.shape[0], g.shape[1] // 2, g.shape[2]), g.dtype) for g in gs]

    def body(*refs):
        ins, outs = refs[:n], refs[n:2 * n]
        send_sems, recv_sems = refs[2 * n:]
        x, y, c = _my_place()
        cps = []
        for a in range(n):
            rh = gs[a].shape[1] // 2
            cp = _remote(ins[a].at[:, pl.ds((1 - c) * rh, rh), :], outs[a], send_sems.at[a], recv_sems.at[a],
                         (x, y, 1 - c))
            cp.start()
            cps.append(cp)
        for cp in cps:
            cp.wait()

    scratch = [pltpu.SemaphoreType.DMA((n,)), pltpu.SemaphoreType.DMA((n,))]
    return _hbm_call(body, n, out_shapes, scratch, name)(*gs)


SUM_BLOCK_BYTES = 2 * 1024 * 1024


def _sum_rows(rh, cols):
    return _tile(rh, max(16, SUM_BLOCK_BYTES // (4 * cols)), 16)


def _add_cores(g, other, c_arr, wire_dtype, name):
    n, r, cols = g.shape
    rh = r // 2
    tr = _sum_rows(rh, cols)
    nrt = rh // tr

    def body(c_ref, g_ref, o_ref, q_ref):
        q_ref[...] = (g_ref[...] + o_ref[...]).astype(wire_dtype)

    gs = pltpu.PrefetchScalarGridSpec(
        num_scalar_prefetch=1, grid=(n, nrt),
        in_specs=[pl.BlockSpec((None, tr, cols), lambda j, i, c_ref: (j, c_ref[0] * nrt + i, 0)),
                  pl.BlockSpec((None, tr, cols), lambda j, i, c_ref: (j, i, 0))],
        out_specs=pl.BlockSpec((None, tr, cols), lambda j, i, c_ref: (j, i, 0)))
    return pl.pallas_call(
        body, name=name, grid_spec=gs, out_shape=jax.ShapeDtypeStruct((n, rh, cols), wire_dtype),
        compiler_params=_params('parallel', 'parallel'),
    )(c_arr, g, other)


def _send_exchange(qs):
    n = len(qs)

    def plan(ins, outs, send_sems, recv_sems):
        x, y, c = _my_place()
        return [_remote(ins[a].at[2 * ox + oy], outs[a].at[j], send_sems.at[3 * a + j], recv_sems.at[3 * a + j],
                        (ox, oy, c))
                for a in range(n) for j, (ox, oy) in enumerate(_other_chips(x, y))]

    def start(ins, outs, scratch):
        for cp in plan(ins, outs, *scratch):
            cp.start()

    def finish(ins, outs, scratch):
        cps = plan(ins, outs, *scratch)
        for cp in cps:
            cp.wait_recv()
        for cp in cps:
            cp.wait_send()

    scratch = [pltpu.SemaphoreType.DMA((3 * n,)), pltpu.SemaphoreType.DMA((3 * n,))]
    out_shapes = [jax.ShapeDtypeStruct((3,) + q.shape[1:], q.dtype) for q in qs]
    return _Exchange(qs, out_shapes, {}, scratch, start, finish)


def _sum_chips(q, got, place_arr, name):
    n, rh, cols = q.shape
    tr = _sum_rows(rh, cols)

    def body(p_ref, q_ref, gx_ref, gy_ref, gxy_ref, o_ref):
        f = lambda r: r[...].astype(F32)
        o_ref[...] = (f(q_ref) + f(gxy_ref)) + (f(gx_ref) + f(gy_ref))

    def got_spec(j):
        return pl.BlockSpec((None, tr, cols), lambda i, p_ref: (j, i, 0))

    gs = pltpu.PrefetchScalarGridSpec(
        num_scalar_prefetch=1, grid=(rh // tr,),
        in_specs=[pl.BlockSpec((None, tr, cols), lambda i, p_ref: (p_ref[0], i, 0)),
                  got_spec(0), got_spec(1), got_spec(2)],
        out_specs=pl.BlockSpec((None, tr, cols), lambda i, p_ref: (p_ref[1], i, 0)))
    return pl.pallas_call(
        body, name=name, grid_spec=gs, out_shape=jax.ShapeDtypeStruct((2, rh, cols), F32),
        compiler_params=_params('parallel'),
    )(place_arr, q, got, got, got)


def _join_cores(rs, name='join_cores'):
    n = len(rs)
    out_shapes = [jax.ShapeDtypeStruct(r.shape, r.dtype) for r in rs]

    def body(*refs):
        outs = refs[n:2 * n]
        send_sems, recv_sems = refs[2 * n:]
        x, y, c = _my_place()
        cps = []
        for a in range(n):
            cp = _remote(outs[a].at[c], outs[a].at[c], send_sems.at[a], recv_sems.at[a], (x, y, 1 - c))
            cp.start()
            cps.append(cp)
        for cp in cps:
            cp.wait()

    scratch = [pltpu.SemaphoreType.DMA((n,)), pltpu.SemaphoreType.DMA((n,))]
    return _hbm_call(body, n, out_shapes, scratch, name, aliases={a: a for a in range(n)})(*rs)


def _gate_rows(t, B, S):
    return t.reshape(B, S, N_MAIN_HEADS).transpose(0, 2, 1).reshape(B * N_MAIN_HEADS, S)


def _gate_cols(t, B, S):
    return t.reshape(B, N_MAIN_HEADS, S).transpose(0, 2, 1).reshape(B * S, N_MAIN_HEADS)


def _mem_kv_fwd(mem2, g, w, tag):
    hm = _rms_fwd(mem2, g, name=f'rms_mem_{tag}')
    mkv = _matmul(hm, w, 'nn', ACT_DTYPE, name=f'mm_memkv_{tag}')
    return hm, mkv


def _mem_kv_bwd(mem2, g, w, hm, dmk, dmv, tag):
    dmkv = jnp.concatenate([dmk, dmv], axis=1)
    dw = _matmul(hm, dmkv, 'tn', F32, name=f'mm_memkv_dw_{tag}')
    dhm = _matmul(dmkv, w, 'nt', F32, name=f'mm_memkv_dx_{tag}')
    _, dg = _rms_bwd(mem2, g, dhm, None, name=f'rms_mem_bwd_{tag}')
    return dw, dg


def _ffn_fwd(x, g, w_up, cw, cb, w_down, B, S, tag):
    T = x.shape[0]
    h2 = _rms_fwd(x, g, name=f'rms_ffn_{tag}')
    u = _matmul(h2, w_up, 'nn', ACT_DTYPE, name=f'mm_up_{tag}')
    y = _conv_fwd(u.reshape(B, S, -1), cw, cb, name=f'conv_fwd_{tag}').reshape(T, -1)
    x2 = _matmul(y, w_down, 'nn', F32, res=x, name=f'mm_down_{tag}')
    return x2, (h2, u, y)


def _ffn_bwd(dx2, x, g, w_up, cw, cb, w_down, saved, B, S, tag):
    h2, u, y = saved
    T = x.shape[0]
    dy = _matmul(dx2, w_down, 'nt', ACT_DTYPE, name=f'mm_down_dx_{tag}')
    dw_down = _matmul(y, dx2, 'tn', F32, name=f'mm_down_dw_{tag}')
    dug, duv, dcwg, dcwv, dcbg, dcbv = _conv_bwd(u.reshape(B, S, -1), cw, cb, dy.reshape(B, S, -1),
                                                  name=f'conv_bwd_{tag}')
    du = jnp.concatenate([dug.reshape(T, -1), duv.reshape(T, -1)], axis=1)
    dh2 = _matmul(du, w_up, 'nt', F32, name=f'mm_up_dx_{tag}')
    dw_up = _matmul(h2, du, 'tn', F32, slots=N_CHIPS, name=f'mm_up_dw_{tag}')
    dx, dg = _rms_bwd(x, g, dh2, dx2, name=f'rms_ffn_bwd_{tag}')
    dcw = jnp.concatenate([dcwg, dcwv], axis=1)
    dcb = jnp.concatenate([dcbg, dcbv], axis=1)
    return dx, dg, dw_up, dcw, dcb, dw_down


def _step(x, mem, tgt, W, late_weights=None, reduce_early=None):
    B, S, D = x.shape
    T = B * S
    x0 = x.reshape(T, D)
    mem2 = mem.reshape(-1, D)
    tgt2 = tgt.reshape(T, D)
    row = lambda v: v.reshape(1, -1)
    q3 = 3 * MAIN_WIDTH

    w_in_a = W['w_in_a'][0]
    wa_main = jnp.concatenate([w_in_a[:, :q3], w_in_a[:, q3 + N_MAIN_HEADS:]], axis=1)
    wa_gate = jnp.pad(w_in_a[:, q3:q3 + N_MAIN_HEADS], ((0, 0), (0, LANES - N_MAIN_HEADS)))
    bcol = jnp.tile(W['b_f_a'][0], B).reshape(B * N_MAIN_HEADS, 1)
    nkb = S // _blocks(S, FOX_ROWS, FOX_KEYS)[1]

    h1a = _rms_fwd(x0, row(W['ln_mix_g'][0]), name='rms_mix_a')
    pa = _matmul(h1a, wa_main, 'nn', ACT_DTYPE, name='mm_in_a')
    flog = _matmul(h1a, wa_gate, 'nn', F32, name='mm_gate_a')
    qkv_a = (pa, pa, pa)
    offs_a = (0, N_MAIN_PAIRS, 2 * N_MAIN_PAIRS)
    qm_off_a = 3 * N_MAIN_PAIRS
    zt = _gate_rows(flog[:, :N_MAIN_HEADS], B, S)
    cum = _gate_fwd(zt, bcol)
    ccol = cum.reshape(B * N_MAIN_HEADS, S, 1)
    crow = cum.reshape(B * N_MAIN_HEADS, nkb, 1, S // nkb)
    (oa, lse), late = _fox_fwd(qkv_a, offs_a, B, S, ccol, crow, beside=late_weights[0] if late_weights else None)
    if late_weights:
        W = {**W, **late_weights[1](late)}
    w_in_b = W['w_in_b'][0]
    hma, mkva = _mem_kv_fwd(mem2, row(W['ln_mem_g'][0]), W['w_memkv'][0], 'a')
    oma = _mem_fwd(pa, qm_off_a, mkva, B, S, name='mem_fwd_a')
    ocat_a = jnp.concatenate([oa, oma], axis=1)
    x1 = _matmul(ocat_a, W['w_out'][0], 'nn', F32, res=x0, name='mm_out_a')
    x2, ffn_a = _ffn_fwd(x1, row(W['ln_ffn_g'][0]), W['w_up'][0], W['conv_w'][0], row(W['conv_b'][0]),
                         W['w_down'][0], B, S, 'a')
    hkv = _rms_fwd(x2, row(W['ln_kv_g']), name='rms_kv')
    kvs = _matmul(hkv, W['w_kv'], 'nn', ACT_DTYPE, name='mm_kv')
    h1b = _rms_fwd(x2, row(W['ln_mix_g'][1]), name='rms_mix_b')
    pb = _matmul(h1b, w_in_b, 'nn', ACT_DTYPE, name='mm_in_b')
    qkv_b = (pb, kvs, kvs)
    offs_b = (0, 0, N_MAIN_PAIRS)
    qm_off_b = N_MAIN_PAIRS
    ob, tot_b = _sb_fwd(qkv_b, offs_b, B, S)
    hmb, mkvb = _mem_kv_fwd(mem2, row(W['ln_mem_g'][1]), W['w_memkv'][1], 'b')
    omb = _mem_fwd(pb, qm_off_b, mkvb, B, S, name='mem_fwd_b')
    ocat_b = jnp.concatenate([ob, omb], axis=1)
    x3 = _matmul(ocat_b, W['w_out'][1], 'nn', F32, res=x2, name='mm_out_b')
    x4, ffn_b = _ffn_fwd(x3, row(W['ln_ffn_g'][1]), W['w_up'][1], W['conv_w'][1], row(W['conv_b'][1]),
                         W['w_down'][1], B, S, 'b')
    loss, dx4, d_final_g = _final_loss(x4, row(W['final_g']), tgt2)

    dx3, dg_ffn_b, dw_up_b, dcw_b, dcb_b, dw_down_b = _ffn_bwd(
        dx4, x3, row(W['ln_ffn_g'][1]), W['w_up'][1], W['conv_w'][1], row(W['conv_b'][1]), W['w_down'][1],
        ffn_b, B, S, 'b')
    docat = _matmul(dx3, W['w_out'][1], 'nt', ACT_DTYPE, name='mm_out_dx_b')
    dw_out_b = _matmul(ocat_b, dx3, 'tn', F32, name='mm_out_dw_b')
    dqb, dkb, dvb = _sb_bwd(qkv_b, offs_b, B, S, tot_b, docat)
    dqmb, dmkb, dmvb = _mem_bwd(pb, qm_off_b, mkvb, B, S, docat, N_MAIN_PAIRS, name='mem_bwd_b')
    dw_memkv_b, dg_mem_b = _mem_kv_bwd(mem2, row(W['ln_mem_g'][1]), W['w_memkv'][1], hmb, dmkb, dmvb, 'b')
    dpb = jnp.concatenate([dqb, dqmb], axis=1)
    dh1b = _matmul(dpb, w_in_b, 'nt', F32, name='mm_in_dx_b')
    dw_in_b = _matmul(h1b, dpb, 'tn', F32, name='mm_in_dw_b')
    dx2, dg_mix_b = _rms_bwd(x2, row(W['ln_mix_g'][1]), dh1b, dx3, name='rms_mix_bwd_b')
    dkvs = jnp.concatenate([dkb, dvb], axis=1)
    dhkv = _matmul(dkvs, W['w_kv'], 'nt', F32, name='mm_kv_dx')
    dw_kv = _matmul(hkv, dkvs, 'tn', F32, slots=N_CHIPS, name='mm_kv_dw')
    dx2, dg_kv = _rms_bwd(x2, row(W['ln_kv_g']), dhkv, dx2, name='rms_kv_bwd')

    dx1, dg_ffn_a, dw_up_a, dcw_a, dcb_a, dw_down_a = _ffn_bwd(
        dx2, x1, row(W['ln_ffn_g'][0]), W['w_up'][0], W['conv_w'][0], row(W['conv_b'][0]), W['w_down'][0],
        ffn_a, B, S, 'a')
    docat = _matmul(dx1, W['w_out'][0], 'nt', ACT_DTYPE, name='mm_out_dx_a')
    dw_out_a = _matmul(ocat_a, dx1, 'tn', F32, name='mm_out_dw_a')

    def by_rows(dw):
        return dw.reshape(N_CHIPS, dw.shape[0] // N_CHIPS, dw.shape[1])

    grads = {
        'w_in_b': [by_rows(dw_in_b)],
        'w_kv': [dw_kv],
        'w_out': [by_rows(dw_out_a), by_rows(dw_out_b)],
        'w_up': [dw_up_a, dw_up_b],
        'w_down': [by_rows(dw_down_a), by_rows(dw_down_b)],
    }
    early = [(n, layer, g) for n, gs in grads.items() for layer, g in enumerate(gs)]
    early.append(('w_memkv', 1, by_rows(dw_memkv_b)))
    beside = reduce_early(early) if reduce_early else None
    (dqa, dka, dva, dccol, dcrow), crossed = _fox_bwd(qkv_a, offs_a, B, S, ccol, crow, oa, lse, docat, beside=beside)
    dzt, dbrow = _gate_bwd(zt, bcol, dccol.reshape(B * N_MAIN_HEADS, S) + dcrow.reshape(B * N_MAIN_HEADS, S))
    dqma, dmka, dmva = _mem_bwd(pa, qm_off_a, mkva, B, S, docat, N_MAIN_PAIRS, name='mem_bwd_a')
    dw_memkv_a, dg_mem_a = _mem_kv_bwd(mem2, row(W['ln_mem_g'][0]), W['w_memkv'][0], hma, dmka, dmva, 'a')
    dpa = jnp.concatenate([dqa, dka, dva, dqma], axis=1)
    dflog = jnp.pad(_gate_cols(dzt, B, S), ((0, 0), (0, LANES - N_MAIN_HEADS)))
    dh1a = _matmul(dpa, wa_main, 'nt', F32, name='mm_in_dx_a')
    dh1a = _matmul(dflog, wa_gate, 'nt', F32, res=dh1a, name='mm_gate_dx_a')
    dwa_main = _matmul(h1a, dpa, 'tn', F32, name='mm_in_dw_a')
    dwa_gate = _matmul(h1a, dflog, 'tn', F32, name='mm_gate_dw_a')
    dx0, dg_mix_a = _rms_bwd(x0, row(W['ln_mix_g'][0]), dh1a, dx1, name='rms_mix_bwd_a')

    dw_in_a = jnp.concatenate([dwa_main[:, :q3], dwa_gate[:, :N_MAIN_HEADS], dwa_main[:, q3:]], axis=1)
    grads.update({
        'ln_mix_g': jnp.concatenate([dg_mix_a, dg_mix_b], axis=0),
        'w_in_a': dw_in_a[None],
        'b_f_a': dbrow.reshape(B, N_MAIN_HEADS).sum(axis=0)[None],
        'ln_kv_g': dg_kv[0],
        'ln_mem_g': jnp.concatenate([dg_mem_a, dg_mem_b], axis=0),
        'w_memkv': [by_rows(dw_memkv_a), early[-1][2]],
        'ln_ffn_g': jnp.concatenate([dg_ffn_a, dg_ffn_b], axis=0),
        'conv_w': jnp.stack([dcw_a, dcw_b]),
        'conv_b': jnp.concatenate([dcb_a, dcb_b], axis=0),
        'final_g': d_final_g[0],
    })
    return loss, dx0.reshape(B, S, D), grads, crossed


BLOCKED = ('w_in_b', 'w_kv', 'w_memkv', 'w_out', 'w_up', 'w_down')
MISC_ROWS = 32


def _misc_names():
    return [n for n in PARAM_NAMES if PARAM_SHARD_AXIS[n] is None] + ['conv_w']


def _reduce_begin(arrays, wire, tag):
    _, _, c = _my_place()
    c_arr = jnp.reshape(c, (1,)).astype(jnp.int32)
    others = _swap_cores(arrays, name=f'swap_cores_{tag}')
    return [_add_cores(g, o, c_arr, wire[i], name=f'add_cores_{tag}_{i}')
            for i, (g, o) in enumerate(zip(arrays, others))]


def _reduce_end(qs, crossed, tag):
    x, y, c = _my_place()
    place_arr = jnp.stack([2 * x + y, c]).astype(jnp.int32)
    sums = [_sum_chips(q, g, place_arr, name=f'sum_chips_{tag}_{i}') for i, (q, g) in enumerate(zip(qs, crossed))]
    return [j.reshape(-1, j.shape[-1]) for j in _join_cores(sums, name=f'join_cores_{tag}')]


def _pack_late(grads, shards):
    a_cols = shards['w_in_a'].shape[2]
    a_pad = -(-a_cols // LANES) * LANES
    dw_in_a = grads['w_in_a'][0]
    in_a = jnp.stack([jnp.pad(dw_in_a[:, k * a_cols:(k + 1) * a_cols], ((0, 0), (0, a_pad - a_cols)))
                      for k in range(N_CHIPS)])
    conv_cols = shards['conv_w'].shape[2]
    misc = []
    for k in range(N_CHIPS):
        parts = [grads[n].reshape(-1) for n in _misc_names()[:-1]]
        parts.append(grads['conv_w'][:, :, k * conv_cols:(k + 1) * conv_cols].reshape(-1))
        flat = jnp.concatenate(parts)
        assert flat.shape[0] <= MISC_ROWS * PACK_COLS
        misc.append(jnp.pad(flat, (0, MISC_ROWS * PACK_COLS - flat.shape[0])).reshape(MISC_ROWS, PACK_COLS))
    return in_a, jnp.stack(misc)


def _unpack_misc(rows, shards):
    flat = rows.reshape(-1)
    out, off = {}, 0
    for name in _misc_names():
        shape = shards[name].shape
        size = math.prod(shape)
        out[name] = flat[off:off + size].reshape(-1, shape[-1])
        off += size
    return out


def kernel(x, mem, ln_mix_g, w_in_a, b_f_a, w_in_b, ln_kv_g, w_kv, ln_mem_g, w_memkv, w_out, ln_ffn_g, w_up, conv_w, conv_b, w_down, final_g, loss_target, m_ln_mix_g, m_w_in_a, m_b_f_a, m_w_in_b, m_ln_kv_g, m_w_kv, m_ln_mem_g, m_w_memkv, m_w_out, m_ln_ffn_g, m_w_up, m_conv_w, m_conv_b, m_w_down, m_final_g, v_ln_mix_g, v_w_in_a, v_b_f_a, v_w_in_b, v_ln_kv_g, v_w_kv, v_ln_mem_g, v_w_memkv, v_w_out, v_ln_ffn_g, v_w_up, v_conv_w, v_conv_b, v_w_down, v_final_g):
    shards = dict(ln_mix_g=ln_mix_g, w_in_a=w_in_a, b_f_a=b_f_a, w_in_b=w_in_b, ln_kv_g=ln_kv_g, w_kv=w_kv,
                  ln_mem_g=ln_mem_g, w_memkv=w_memkv, w_out=w_out, ln_ffn_g=ln_ffn_g, w_up=w_up, conv_w=conv_w,
                  conv_b=conv_b, w_down=w_down, final_g=final_g)
    moments_m = dict(ln_mix_g=m_ln_mix_g, w_in_a=m_w_in_a, b_f_a=m_b_f_a, w_in_b=m_w_in_b, ln_kv_g=m_ln_kv_g,
                     w_kv=m_w_kv, ln_mem_g=m_ln_mem_g, w_memkv=m_w_memkv, w_out=m_w_out, ln_ffn_g=m_ln_ffn_g,
                     w_up=m_w_up, conv_w=m_conv_w, conv_b=m_conv_b, w_down=m_w_down, final_g=m_final_g)
    moments_v = dict(ln_mix_g=v_ln_mix_g, w_in_a=v_w_in_a, b_f_a=v_b_f_a, w_in_b=v_w_in_b, ln_kv_g=v_ln_kv_g,
                     w_kv=v_w_kv, ln_mem_g=v_ln_mem_g, w_memkv=v_w_memkv, w_out=v_w_out, ln_ffn_g=v_ln_ffn_g,
                     w_up=v_w_up, conv_w=v_conv_w, conv_b=v_conv_b, w_down=v_w_down, final_g=v_final_g)

    kinds = {'w_in_a': 'stack', 'w_in_b': 'rows', 'w_kv': 'cols', 'w_memkv': 'rows', 'w_out': 'rows', 'w_up': 'cols',
             'w_down': 'rows', 'conv_w': 'cols'}
    mx, my, _ = _my_place()
    chip_arr = jnp.reshape(2 * mx + my, (1,)).astype(jnp.int32)
    placed, shard_shapes = {}, {}
    for n, kind in kinds.items():
        w = shards[n].reshape((-1,) + shards[n].shape[-2:])
        shard_shapes[n] = w.shape
        placed[n] = _place_block(w, kind, F32 if n in F32_GATHERED else jnp.bfloat16, chip_arr, name=f'place_{n}')

    def gather(names):
        return _gather_exchange([placed[n] for n in names], [shard_shapes[n] for n in names],
                                [kinds[n] for n in names], [n not in F32_GATHERED for n in names])

    def as_weights(names, full):
        out = dict(zip(names, full))
        if 'w_in_a' in out:
            out['w_in_a'] = jnp.concatenate([out['w_in_a'][k] for k in range(N_CHIPS)], axis=1)[None]
        if 'w_kv' in out:
            out['w_kv'] = out['w_kv'][0]
        return out

    first = ['w_in_a', 'conv_w']
    late = [n for n in kinds if n not in first]
    W = {**shards, **as_weights(first, _run_exchange(gather(first), 'gather_first'))}

    early = {}

    def reduce_early(items):
        early['owners'] = [(n, layer) for n, layer, _ in items]
        early['qs'] = _reduce_begin([g for _, _, g in items], [jnp.bfloat16] * len(items), 'early')
        return _send_exchange(early['qs'])

    loss_part, grad_x, grads, crossed = _step(x, mem, loss_target, W, (gather(late), functools.partial(as_weights, late)),
                                              reduce_early)
    loss = lax.psum(loss_part[0, 0], ('x', 'y', 'c'))

    g_layers = {n: [None] * (len(grads[n]) if n in BLOCKED else 1) for n in PARAM_NAMES}
    for (n, layer), g in zip(early['owners'], _reduce_end(early['qs'], crossed, 'early')):
        g_layers[n][layer] = g
    in_a, misc = _pack_late(grads, shards)
    qs = _reduce_begin([in_a, misc, grads['w_memkv'][0]], [jnp.bfloat16, F32, jnp.bfloat16], 'late')
    in_a_sum, misc_sum, memkv_sum = _reduce_end(qs, _run_exchange(_send_exchange(qs), 'send_chips_late'), 'late')
    g_layers['w_in_a'][0] = in_a_sum[:, :shards['w_in_a'].shape[2]]
    g_layers['w_memkv'][0] = memkv_sum
    for n, g in _unpack_misc(misc_sum, shards).items():
        g_layers[n][0] = g

    results = {}
    for name in PARAM_NAMES:
        w = shards[name]
        layers = len(g_layers[name])
        as_layers = (layers, -1, w.shape[-1])
        w3, m3, v3 = (t.reshape(as_layers) for t in (w, moments_m[name], moments_v[name]))
        res = None
        for layer, g in enumerate(g_layers[name]):
            res = _adamw(w3, g, m3, v3, layer, res, name=f'adamw_{name}_{layer}')
        results[name] = [t.reshape(w.shape) for t in res]

    return (loss, grad_x, *[results[n][k] for k in range(4) for n in PARAM_NAMES])
```

```python
import functools
import math

import jax
import jax.numpy as jnp
from jax import lax
from jax.experimental import pallas as pl
from jax.experimental.pallas import tpu as pltpu

F32 = jnp.float32
MXU_DTYPE = jnp.bfloat16
ACT_DTYPE = jnp.bfloat16

HEAD_DIM = 64
N_MAIN_HEADS = 12
N_MEM_HEADS = 4
MAIN_WIDTH = N_MAIN_HEADS * HEAD_DIM
MEM_WIDTH = N_MEM_HEADS * HEAD_DIM
EPS = 1e-6
SCALE = HEAD_DIM ** -0.5
NEG_BIG = -1e30
LANES = 128
PACK_COLS = 1024
N_CHIPS = 4

ADAM_LR = 0.001
ADAM_B1 = 0.9
ADAM_B2 = 0.999
ADAM_EPS = 1e-08
ADAM_WD = 0.01
ADAM_STEP = 10

MESH = pl.DeviceIdType.MESH
ANY = pl.BlockSpec(memory_space=pl.ANY)

PARAM_SHARD_AXIS = {
    'ln_mix_g': None, 'w_in_a': 2, 'b_f_a': None, 'w_in_b': 1, 'ln_kv_g': None, 'w_kv': 1,
    'ln_mem_g': None, 'w_memkv': 1, 'w_out': 1, 'ln_ffn_g': None, 'w_up': 2, 'conv_w': 2,
    'conv_b': None, 'w_down': 1, 'final_g': None,
}
PARAM_NAMES = list(PARAM_SHARD_AXIS)
F32_GATHERED = ('conv_w',)


def _tile(n, pref, unit=LANES):
    if n <= pref:
        return n
    best = None
    for t in range(unit, pref + 1, unit):
        if n % t == 0:
            best = t
    assert best is not None, (n, pref)
    return best


MM_ACC_ELEMS = 768 * 1024
MM_VMEM_MB = 56
MM_TILE_BYTES = 42 << 20


def _out_tiles(M, N):
    def divisors(n, cap):
        if n <= LANES:
            return [n]
        return [t for t in range(LANES, min(n, cap) + 1, LANES) if n % t == 0]

    best = None
    for tm in divisors(M, 1536):
        for tn in divisors(N, 2048):
            if tm * tn <= MM_ACC_ELEMS and (best is None or (tm * tn, tn) > (best[0] * best[1], best[1])):
                best = (tm, tn)
    assert best is not None, (M, N)
    return best


def _params(*sem, vmem_mb=None):
    kw = {}
    if sem:
        kw['dimension_semantics'] = sem
    if vmem_mb is not None:
        kw['vmem_limit_bytes'] = vmem_mb * 1024 * 1024
    return pltpu.CompilerParams(**kw)


def _dot(a, b, dims):
    return lax.dot_general(a.astype(MXU_DTYPE), b.astype(MXU_DTYPE), (dims, ((), ())),
                           preferred_element_type=F32)


NN = ((1,), (0,))
NT = ((1,), (1,))
TN = ((0,), (0,))


def _matmul(a, b, mode, out_dtype, res=None, slots=1, slot_base=0, total_slots=None, into=None, name='mm'):
    pieces = a if isinstance(a, tuple) else (a,)
    a = pieces[0]
    k_sizes = [p.shape[0 if mode == 'tn' else 1] for p in pieces]
    if mode == 'nn':
        (M, _), (K2, N) = a.shape, b.shape
    elif mode == 'nt':
        (M, _), (N, K2) = a.shape, b.shape
    else:
        (_, M), (K2, N) = a.shape, b.shape
    K = sum(k_sizes)
    assert K == K2 and N % slots == 0, (a.shape, b.shape, mode, slots)
    slot_cols = N // slots
    tm, tn = _out_tiles(M, slot_cols)
    per_slot = slot_cols // tn
    fixed = tm * tn * (4 + 2 * jnp.dtype(out_dtype).itemsize + (8 if res is not None else 0))
    per_k = 2 * (tm * a.dtype.itemsize + tn * b.dtype.itemsize)
    tk = _tile(K, max(LANES, (MM_TILE_BYTES - fixed) // per_k))
    nk = K // tk
    dims = {'nn': NN, 'nt': NT, 'tn': TN}[mode]
    a_again = M * K * a.dtype.itemsize * (N // tn)
    b_again = b.size * b.dtype.itemsize * (M // tm)
    m_inner = nk == 1 and a_again < b_again
    n_a = len(pieces)
    assert n_a == 1 or (nk == 1 and mode != 'tn')

    def body(*refs):
        a_refs, b_ref = refs[:n_a], refs[n_a]
        a_ref = a_refs[0]
        r_ref = refs[n_a + 1] if res is not None else None
        o_ref = refs[n_a + 1 + (res is not None) + (into is not None)]

        def finish(out):
            if r_ref is not None:
                out = out + r_ref[...]
            o_ref[...] = out.astype(out_dtype)

        if nk == 1:
            out, at = None, 0
            for ref, size in zip(a_refs, k_sizes):
                part = b_ref[...] if n_a == 1 else (b_ref[:, at:at + size] if mode == 'nt' else b_ref[at:at + size, :])
                term = _dot(ref[...], part, dims)
                out, at = (term if out is None else out + term), at + size
            finish(out)
            return
        acc = refs[-1]
        k = pl.program_id(2)

        @pl.when(k == 0)
        def _():
            acc[...] = jnp.zeros_like(acc)

        acc[...] += _dot(a_ref[...], b_ref[...], dims)

        @pl.when(k == nk - 1)
        def _():
            finish(acc[...])

    def spec(shape, index):
        return pl.BlockSpec(shape, (lambda j, i, k: index(i, j, k)) if m_inner else index)

    if n_a > 1:
        a_specs = [spec((tm, size), lambda i, j, k: (i, 0)) for size in k_sizes]
    else:
        a_specs = [spec((tk, tm), lambda i, j, k: (k, i)) if mode == 'tn' else spec((tm, tk), lambda i, j, k: (i, k))]
    b_spec = spec((tn, tk), lambda i, j, k: (j, k)) if mode == 'nt' else spec((tk, tn), lambda i, j, k: (k, j))
    if slots == 1 and total_slots is None:
        o_spec = spec((tm, tn), lambda i, j, k: (i, j))
        out_shape = jax.ShapeDtypeStruct((M, N), out_dtype)
    else:
        assert res is None
        o_spec = spec((None, tm, tn), lambda i, j, k: (slot_base + j // per_slot, i, j % per_slot))
        out_shape = jax.ShapeDtypeStruct((total_slots or slots, M, slot_cols), out_dtype)
    in_specs = a_specs + [b_spec] + ([o_spec] if res is not None else []) + ([ANY] if into is not None else [])
    args = pieces + (b,) + ((res,) if res is not None else ()) + ((into,) if into is not None else ())
    return pl.pallas_call(
        body, name=name, grid=(N // tn, M // tm, nk) if m_inner else (M // tm, N // tn, nk),
        in_specs=in_specs, out_specs=o_spec,
        out_shape=out_shape,
        input_output_aliases={len(args) - 1: 0} if into is not None else {},
        scratch_shapes=[] if nk == 1 else [pltpu.VMEM((tm, tn), F32)],
        compiler_params=_params('parallel', 'parallel', 'arbitrary', vmem_mb=MM_VMEM_MB),
    )(*args)


def _rms_fwd(x, g, name):
    T, D = x.shape
    tr = _tile(T, 512)

    def body(x_ref, g_ref, o_ref):
        xv = x_ref[...]
        r = lax.rsqrt(jnp.mean(xv * xv, axis=-1, keepdims=True) + EPS)
        o_ref[...] = (xv * r * g_ref[...]).astype(ACT_DTYPE)

    return pl.pallas_call(
        body, name=name, grid=(T // tr,),
        in_specs=[pl.BlockSpec((tr, D), lambda i: (i, 0)), pl.BlockSpec((1, D), lambda i: (0, 0))],
        out_specs=pl.BlockSpec((tr, D), lambda i: (i, 0)),
        out_shape=jax.ShapeDtypeStruct((T, D), ACT_DTYPE),
        compiler_params=_params('parallel'),
    )(x, g)


def _rms_bwd(x, g, dh, dres, name):
    T, D = x.shape
    tr = _tile(T, 512)
    want_dx = dres is not None

    def body(*refs):
        if want_dx:
            x_ref, g_ref, dh_ref, dres_ref, dx_ref, dg_ref = refs
        else:
            x_ref, g_ref, dh_ref, dg_ref = refs
        i = pl.program_id(0)

        @pl.when(i == 0)
        def _():
            dg_ref[...] = jnp.zeros_like(dg_ref)

        xv = x_ref[...]
        dhv = dh_ref[...].astype(F32)
        r = lax.rsqrt(jnp.mean(xv * xv, axis=-1, keepdims=True) + EPS)
        n = xv * r
        dg_ref[...] += jnp.sum(dhv * n, axis=0, keepdims=True)
        if want_dx:
            dn = dhv * g_ref[...]
            dx = r * (dn - n * jnp.mean(dn * n, axis=-1, keepdims=True))
            dx_ref[...] = dres_ref[...] + dx

    row = pl.BlockSpec((tr, D), lambda i: (i, 0))
    vec = pl.BlockSpec((1, D), lambda i: (0, 0))
    if want_dx:
        return pl.pallas_call(
            body, name=name, grid=(T // tr,),
            in_specs=[row, vec, row, row], out_specs=[row, vec],
            out_shape=[jax.ShapeDtypeStruct((T, D), F32), jax.ShapeDtypeStruct((1, D), F32)],
            compiler_params=_params('arbitrary'),
        )(x, g, dh, dres)
    dg = pl.pallas_call(
        body, name=name, grid=(T // tr,),
        in_specs=[row, vec, row], out_specs=vec,
        out_shape=jax.ShapeDtypeStruct((1, D), F32),
        compiler_params=_params('arbitrary'),
    )(x, g, dh)
    return None, dg


def _final_loss(x, g, tgt, name='final_loss'):
    T, D = x.shape
    tr = _tile(T, 512)

    def body(x_ref, g_ref, t_ref, loss_ref, dx_ref, dg_ref):
        i = pl.program_id(0)

        @pl.when(i == 0)
        def _():
            loss_ref[...] = jnp.zeros_like(loss_ref)
            dg_ref[...] = jnp.zeros_like(dg_ref)

        xv = x_ref[...]
        gv = g_ref[...]
        r = lax.rsqrt(jnp.mean(xv * xv, axis=-1, keepdims=True) + EPS)
        n = xv * r
        e = n * gv - t_ref[...]
        per_tok = jnp.mean(e * e, axis=-1, keepdims=True)
        loss_ref[...] += 0.5 * jnp.sum(per_tok, axis=0, keepdims=True)
        dy = e * (1.0 / D)
        dg_ref[...] += jnp.sum(dy * n, axis=0, keepdims=True)
        dn = dy * gv
        dx_ref[...] = r * (dn - n * jnp.mean(dn * n, axis=-1, keepdims=True))

    row = pl.BlockSpec((tr, D), lambda i: (i, 0))
    vec = pl.BlockSpec((1, D), lambda i: (0, 0))
    one = pl.BlockSpec((1, 1), lambda i: (0, 0))
    return pl.pallas_call(
        body, name=name, grid=(T // tr,),
        in_specs=[row, vec, row], out_specs=[one, row, vec],
        out_shape=[jax.ShapeDtypeStruct((1, 1), F32), jax.ShapeDtypeStruct((T, D), F32),
                   jax.ShapeDtypeStruct((1, D), F32)],
        compiler_params=_params('arbitrary'),
    )(x, g, tgt)


def _log_sigmoid(z):
    return jnp.minimum(z, 0.0) - jnp.log(1.0 + jnp.exp(-jnp.abs(z)))


def _tri(n, rel):
    j = lax.broadcasted_iota(jnp.int32, (n, n), 0)
    s = lax.broadcasted_iota(jnp.int32, (n, n), 1)
    return rel(j, s).astype(MXU_DTYPE)


def _split_dot(x, tri, terms):
    if MXU_DTYPE == F32:
        return jnp.dot(x, tri, preferred_element_type=F32), jnp.sum(x, axis=-1, keepdims=True)
    out = taken = None
    rem = x
    for _ in range(terms):
        piece = rem.astype(MXU_DTYPE)
        back = piece.astype(F32)
        part = jnp.dot(piece, tri, preferred_element_type=F32)
        rows = jnp.sum(back, axis=-1, keepdims=True)
        out, taken = (part, rows) if out is None else (out + part, taken + rows)
        rem = rem - back
    return out, taken


def _running_sums(x, tri, terms, earlier):
    w = tri.shape[0]
    parts = [_split_dot(x[:, at:at + w], tri, terms) for at in range(0, x.shape[1], w)]
    out = []
    for n, (r, _) in enumerate(parts):
        for _, whole in (parts[:n] if earlier else parts[n + 1:]):
            r = r + whole
        out.append(r)
    return jnp.concatenate(out, axis=1)


def _gate_fwd(zt, bcol, name='gate_fwd'):
    BH, S = zt.shape
    nb = S // LANES

    def body(z_ref, b_ref, c_ref):
        tri = _tri(LANES, lambda j, s: j <= s)
        carry = jnp.zeros((BH, 1), F32)
        for i in range(nb):
            sl = slice(i * LANES, (i + 1) * LANES)
            logf = _log_sigmoid(z_ref[:, sl] + b_ref[...])
            cs = _split_dot(logf, tri, 3)[0] + carry
            c_ref[:, sl] = cs
            carry = cs[:, LANES - 1:LANES]

    return pl.pallas_call(body, name=name, out_shape=jax.ShapeDtypeStruct((BH, S), F32))(zt, bcol)


def _gate_bwd(zt, bcol, dc, name='gate_bwd'):
    BH, S = zt.shape
    nb = S // LANES

    def body(z_ref, b_ref, dc_ref, dz_ref, db_ref):
        tri = _tri(LANES, lambda j, s: j >= s)
        carry = jnp.zeros((BH, 1), F32)
        dsum = jnp.zeros((BH, 1), F32)
        for i in reversed(range(nb)):
            sl = slice(i * LANES, (i + 1) * LANES)
            rs = _split_dot(dc_ref[:, sl], tri, 3)[0] + carry
            carry = rs[:, 0:1]
            z = z_ref[:, sl] + b_ref[...]
            dz = rs * (1.0 - 1.0 / (1.0 + jnp.exp(-z)))
            dz_ref[:, sl] = dz
            dsum = dsum + jnp.sum(dz, axis=-1, keepdims=True)
        db_ref[...] = dsum

    return pl.pallas_call(
        body, name=name,
        out_shape=[jax.ShapeDtypeStruct((BH, S), F32), jax.ShapeDtypeStruct((BH, 1), F32)],
    )(zt, bcol, dc)


FOX_ROWS, FOX_KEYS = 256, 512


PAIR = 2 * HEAD_DIM
N_MAIN_PAIRS = N_MAIN_HEADS // 2
N_MEM_PAIRS = N_MEM_HEADS // 2


def _lane0(shape):
    return lax.broadcasted_iota(jnp.int32, shape, len(shape) - 1) < HEAD_DIM


def _per_head(x):
    first = _lane0(x.shape)
    zero = jnp.zeros_like(x)
    return jnp.where(first, x, zero), jnp.where(first, zero, x)


def _pick(first, a, b):
    return jnp.where(first, a, b)


GROUP = 3
MAIN_STEPS = N_MAIN_PAIRS // GROUP


def _lanes(p):
    return slice(p * PAIR, (p + 1) * PAIR)


def _q_spec(bq, nq, off, group=1):
    assert off % group == 0
    return pl.BlockSpec((bq, group * PAIR), lambda b, j, i: (b * nq + i, off // group + j))


def _seq_spec(S, off, group=1):
    assert off % group == 0
    return pl.BlockSpec((S, group * PAIR), lambda b, j, i: (b, off // group + j))


def _gate_specs(bq, nk, bk):
    col = pl.BlockSpec((2 * GROUP, bq, 1), lambda b, j, i: (b * MAIN_STEPS + j, i, 0))
    rowv = pl.BlockSpec((2 * GROUP, nk, 1, bk), lambda b, j, i: (b * MAIN_STEPS + j, 0, 0, 0))
    return col, rowv


def _blocks(S, rows, keys):
    bq, bk = min(rows, S), min(keys, S)
    assert bk % bq == 0 and S % bk == 0
    return bq, bk


def _last_block(i, bq, bk, strict, step, carry):
    per = bk // bq
    last = i // per

    def mask(keys, shift):
        row = lax.broadcasted_iota(jnp.int32, (bq, keys), 0) + shift
        col = lax.broadcasted_iota(jnp.int32, (bq, keys), 1)
        return (col < row) if strict else (col <= row)

    if per == 1:
        return step(last, carry, mask(bk, 0), bk)
    assert per == 2
    return lax.cond(i % 2 == 0, lambda c: step(last, c, mask(bq, 0), bq), lambda c: step(last, c, mask(bk, bq), bk),
                    carry)


def _scaled(q):
    assert math.log2(SCALE).is_integer()
    return q * jnp.asarray(SCALE, q.dtype)


def _fox_fwd(qkv, offs, B, S, ccol, crow, beside=None, name='fox_fwd'):
    bq, bk = _blocks(S, FOX_ROWS, FOX_KEYS)
    nq = S // bq

    def body(q_ref, k_ref, v_ref, cc_ref, cr_ref, o_ref, lse_ref):
        i = pl.program_id(2)
        qv = _scaled(q_ref[...])
        qh = [_per_head(qv[:, _lanes(p)]) for p in range(GROUP)]
        first = _lane0((bq, PAIR))

        def step(kb, carry, mask=None, keys=bk):
            m, l, acc = carry
            sl = pl.ds(pl.multiple_of(kb * bk, bk), keys)
            m_new, l_new, acc_new = [], [], []
            for p in range(GROUP):
                ks, vs = k_ref[sl, _lanes(p)], v_ref[sl, _lanes(p)]
                alpha, pv = [], []
                for h in range(2):
                    n = 2 * p + h
                    s = _dot(qh[p][h], ks, NT) + cc_ref[n] - cr_ref[n, kb][:, :keys]
                    if mask is not None:
                        s = jnp.where(mask, s, NEG_BIG)
                    mh = jnp.maximum(m[n], jnp.max(s, axis=-1, keepdims=True))
                    pr = jnp.exp(s - mh)
                    ah = jnp.exp(m[n] - mh)
                    m_new.append(mh)
                    alpha.append(ah)
                    l_new.append(ah * l[n] + jnp.sum(pr, axis=-1, keepdims=True))
                    pv.append(_dot(pr, vs, NN))
                acc_new.append(_pick(first, alpha[0], alpha[1]) * acc[p] + _pick(first, pv[0], pv[1]))
            return tuple(m_new), tuple(l_new), tuple(acc_new)

        negs = tuple(jnp.full((bq, 1), NEG_BIG, F32) for _ in range(2 * GROUP))
        zeros = tuple(jnp.zeros((bq, 1), F32) for _ in range(2 * GROUP))
        acc0 = tuple(jnp.zeros((bq, PAIR), F32) for _ in range(GROUP))
        carry = lax.fori_loop(0, i // (bk // bq), step, (negs, zeros, acc0))
        m, l, acc = _last_block(i, bq, bk, False, step, carry)
        for p in range(GROUP):
            o_ref[:, _lanes(p)] = (acc[p] / _pick(first, l[2 * p], l[2 * p + 1])).astype(ACT_DTYPE)
        for n in range(2 * GROUP):
            lse_ref[n] = m[n] + jnp.log(l[n])

    col, rowv = _gate_specs(bq, S // bk, bk)
    return _call_beside(
        body, name, (B, MAIN_STEPS, nq),
        [_q_spec(bq, nq, offs[0], GROUP), _seq_spec(S, offs[1], GROUP), _seq_spec(S, offs[2], GROUP), col, rowv],
        [_q_spec(bq, nq, 0, GROUP), col],
        [jax.ShapeDtypeStruct((B * S, MAIN_WIDTH), ACT_DTYPE), jax.ShapeDtypeStruct((B * N_MAIN_HEADS, S, 1), F32)],
        [], (*qkv, ccol, crow), ('parallel', 'parallel', 'arbitrary'), beside)


def _fox_bwd(qkv, offs, B, S, ccol, crow, o, lse, do, beside=None, name='fox_bwd'):
    bq, bk = _blocks(S, FOX_ROWS, FOX_KEYS)
    nq = S // bq

    def body(q_ref, k_ref, v_ref, cc_ref, cr_ref, o_ref, lse_ref, do_ref,
             dq_ref, dk_ref, dv_ref, dcc_ref, dcr_ref, dk_acc, dv_acc):
        i = pl.program_id(2)

        @pl.when(i == 0)
        def _():
            dk_acc[...] = jnp.zeros_like(dk_acc)
            dv_acc[...] = jnp.zeros_like(dv_acc)
            dcr_ref[...] = jnp.zeros_like(dcr_ref)

        qv = _scaled(q_ref[...])
        dov = do_ref[...]
        qp = [qv[:, _lanes(p)] for p in range(GROUP)]
        dop = [dov[:, _lanes(p)] for p in range(GROUP)]
        qh = [_per_head(t) for t in qp]
        doh = [_per_head(t) for t in dop]
        first = _lane0((bq, PAIR))
        prod = dov.astype(F32) * o_ref[...].astype(F32)
        dsum = [jnp.sum(t, axis=-1, keepdims=True) for p in range(GROUP) for t in _per_head(prod[:, _lanes(p)])]

        def step(kb, carry, mask=None, keys=bk):
            dq, dcc = carry
            sl = pl.ds(pl.multiple_of(kb * bk, bk), keys)
            first_k = _lane0((keys, PAIR))
            dq_new, dcc_new = [], []
            for p in range(GROUP):
                ks, vs = k_ref[sl, _lanes(p)], v_ref[sl, _lanes(p)]
                dqh, dkh, dvh = [], [], []
                for h in range(2):
                    n = 2 * p + h
                    s = _dot(qh[p][h], ks, NT) + cc_ref[n] - cr_ref[n, kb][:, :keys]
                    pr = jnp.exp(s - lse_ref[n])
                    if mask is not None:
                        pr = jnp.where(mask, pr, 0.0)
                    ds = pr * (_dot(doh[p][h], vs, NT) - dsum[n])
                    dqh.append(_dot(ds, ks, NN))
                    dkh.append(_dot(ds, qp[p], TN))
                    dvh.append(_dot(pr, dop[p], TN))
                    as_key = jnp.sum(ds, axis=0, keepdims=True)
                    if keys < bk:
                        as_key = jnp.concatenate([as_key, jnp.zeros((1, bk - keys), F32)], axis=1)
                    dcr_ref[n, kb] -= as_key
                    dcc_new.append(dcc[n] + jnp.sum(ds, axis=-1, keepdims=True))
                dk_acc[sl, _lanes(p)] += _pick(first_k, dkh[0], dkh[1])
                dv_acc[sl, _lanes(p)] += _pick(first_k, dvh[0], dvh[1])
                dq_new.append(dq[p] + _pick(first, dqh[0], dqh[1]))
            return tuple(dq_new), tuple(dcc_new)

        zeros = tuple(jnp.zeros((bq, 1), F32) for _ in range(2 * GROUP))
        dq0 = tuple(jnp.zeros((bq, PAIR), F32) for _ in range(GROUP))
        dq, dcc = _last_block(i, bq, bk, False, step, lax.fori_loop(0, i // (bk // bq), step, (dq0, zeros)))
        for p in range(GROUP):
            dq_ref[:, _lanes(p)] = (dq[p] * SCALE).astype(ACT_DTYPE)
        for n in range(2 * GROUP):
            dcc_ref[n] = dcc[n]

        @pl.when(i == nq - 1)
        def _():
            dk_ref[...] = dk_acc[...].astype(ACT_DTYPE)
            dv_ref[...] = dv_acc[...].astype(ACT_DTYPE)

    col, rowv = _gate_specs(bq, S // bk, bk)
    qs, seq = _q_spec(bq, nq, 0, GROUP), _seq_spec(S, 0, GROUP)
    full = jax.ShapeDtypeStruct((B * S, MAIN_WIDTH), ACT_DTYPE)
    wide = pltpu.VMEM((S, GROUP * PAIR), F32)
    return _call_beside(
        body, name, (B, MAIN_STEPS, nq),
        [_q_spec(bq, nq, offs[0], GROUP), _seq_spec(S, offs[1], GROUP), _seq_spec(S, offs[2], GROUP), col, rowv,
         qs, col, qs],
        [qs, seq, seq, col, rowv],
        [full, full, full, jax.ShapeDtypeStruct(ccol.shape, F32), jax.ShapeDtypeStruct(crow.shape, F32)],
        [wide, wide],
        (*qkv, ccol, crow, o, lse, do), ('parallel', 'parallel', 'arbitrary'), beside)


SB_ROWS, SB_KEYS = 256, 512
SB_TRIANGLE = 256


SB_SUM_TERMS = 2


def _sb_block(q_scaled, ks, mask):
    z = _dot(q_scaled, ks, NT)
    a = _log_sigmoid(z)
    l = a - z
    return a, (l if mask is None else jnp.where(mask, l, 0.0))


def _sb_fwd(qkv, offs, B, S, name='sb_fwd'):
    bq, bk = _blocks(S, SB_ROWS, SB_KEYS)
    nq = S // bq

    def body(q_ref, k_ref, v_ref, o_ref, tot_ref):
        i = pl.program_id(2)
        qv = _scaled(q_ref[...])
        qh = [_per_head(qv[:, _lanes(p)]) for p in range(GROUP)]
        first = _lane0((bq, PAIR))
        tri = _tri(min(bk, SB_TRIANGLE), lambda j, s: j > s)

        def step(kb, carry, mask=None, keys=bk):
            acc, right = carry
            sl = pl.ds(pl.multiple_of(kb * bk, bk), keys)
            acc_new, right_new = [], []
            for p in range(GROUP):
                ks, vs = k_ref[sl, _lanes(p)], v_ref[sl, _lanes(p)]
                pv = []
                for h in range(2):
                    n = 2 * p + h
                    a, l = _sb_block(qh[p][h], ks, mask)
                    w = jnp.exp(a + _running_sums(l, tri, SB_SUM_TERMS, False) + right[n])
                    if mask is not None:
                        w = jnp.where(mask, w, 0.0)
                    pv.append(_dot(w, vs, NN))
                    right_new.append(right[n] + jnp.sum(l, axis=-1, keepdims=True))
                acc_new.append(acc[p] + _pick(first, pv[0], pv[1]))
            return tuple(acc_new), tuple(right_new)

        zeros = tuple(jnp.zeros((bq, 1), F32) for _ in range(2 * GROUP))
        acc0 = tuple(jnp.zeros((bq, PAIR), F32) for _ in range(GROUP))
        last = i // (bk // bq)
        carry = _last_block(i, bq, bk, True, step, (acc0, zeros))
        acc, total = lax.fori_loop(0, last, lambda n, c: step(last - 1 - n, c), carry)
        for p in range(GROUP):
            o_ref[:, _lanes(p)] = acc[p].astype(ACT_DTYPE)
        for n in range(2 * GROUP):
            tot_ref[n] = total[n]

    col, _ = _gate_specs(bq, S // bk, bk)
    return pl.pallas_call(
        body, name=name, grid=(B, MAIN_STEPS, nq),
        in_specs=[_q_spec(bq, nq, offs[0], GROUP), _seq_spec(S, offs[1], GROUP), _seq_spec(S, offs[2], GROUP)],
        out_specs=[_q_spec(bq, nq, 0, GROUP), col],
        out_shape=[jax.ShapeDtypeStruct((B * S, MAIN_WIDTH), ACT_DTYPE),
                   jax.ShapeDtypeStruct((B * N_MAIN_HEADS, S, 1), F32)],
        compiler_params=_params('parallel', 'parallel', 'arbitrary'),
    )(*qkv)


def _sb_bwd(qkv, offs, B, S, tot, do, name='sb_bwd'):
    bq, bk = _blocks(S, SB_ROWS, SB_KEYS)
    nq = S // bq

    def body(q_ref, k_ref, v_ref, tot_ref, do_ref, dq_ref, dk_ref, dv_ref, dk_acc, dv_acc):
        i = pl.program_id(2)

        @pl.when(i == 0)
        def _():
            dk_acc[...] = jnp.zeros_like(dk_acc)
            dv_acc[...] = jnp.zeros_like(dv_acc)

        qv = _scaled(q_ref[...])
        dov = do_ref[...]
        qp = [qv[:, _lanes(p)] for p in range(GROUP)]
        dop = [dov[:, _lanes(p)] for p in range(GROUP)]
        qh = [_per_head(t) for t in qp]
        doh = [_per_head(t) for t in dop]
        heads = [(p, h) for p in range(GROUP) for h in range(2)]
        first = _lane0((bq, PAIR))
        tri_incl = _tri(min(bk, SB_TRIANGLE), lambda j, s: j <= s)
        tri_excl = _tri(min(bk, SB_TRIANGLE), lambda j, s: j < s)
        zeros = tuple(jnp.zeros((bq, 1), F32) for _ in heads)
        tot = tuple(tot_ref[n] for n in range(len(heads)))

        def step(kb, carry, mask=None, keys=bk):
            dq, rest_l, left_g = carry
            sl = pl.ds(pl.multiple_of(kb * bk, bk), keys)
            first_k = _lane0((keys, PAIR))
            new_dq, new_l, new_g = [], [], []
            for p in range(GROUP):
                ks, vs = k_ref[sl, _lanes(p)], v_ref[sl, _lanes(p)]
                dqh, dkh, dvh = [], [], []
                for h in range(2):
                    n = 2 * p + h
                    a, l = _sb_block(qh[p][h], ks, mask)
                    w = jnp.exp(a - _running_sums(l, tri_incl, SB_SUM_TERMS, True) + rest_l[n])
                    if mask is not None:
                        w = jnp.where(mask, w, 0.0)
                    g = w * _dot(doh[p][h], vs, NT)
                    beta = jnp.exp(a)
                    dz = g - beta * (g + _running_sums(g, tri_excl, 1, True) + left_g[n])
                    if mask is not None:
                        dz = jnp.where(mask, dz, 0.0)
                    dqh.append(_dot(dz, ks, NN))
                    dkh.append(_dot(dz, qp[p], TN))
                    dvh.append(_dot(w, dop[p], TN))
                    new_l.append(rest_l[n] - jnp.sum(l, axis=-1, keepdims=True))
                    new_g.append(left_g[n] + jnp.sum(g, axis=-1, keepdims=True))
                dk_acc[sl, _lanes(p)] += _pick(first_k, dkh[0], dkh[1])
                dv_acc[sl, _lanes(p)] += _pick(first_k, dvh[0], dvh[1])
                new_dq.append(dq[p] + _pick(first, dqh[0], dqh[1]))
            return tuple(new_dq), tuple(new_l), tuple(new_g)

        dq0 = tuple(jnp.zeros((bq, PAIR), F32) for _ in range(GROUP))
        dq, _, _ = _last_block(i, bq, bk, True, step, lax.fori_loop(0, i // (bk // bq), step, (dq0, tot, zeros)))
        for p in range(GROUP):
            dq_ref[:, _lanes(p)] = (dq[p] * SCALE).astype(ACT_DTYPE)

        @pl.when(i == nq - 1)
        def _():
            dk_ref[...] = dk_acc[...].astype(ACT_DTYPE)
            dv_ref[...] = dv_acc[...].astype(ACT_DTYPE)

    qs, seq = _q_spec(bq, nq, 0, GROUP), _seq_spec(S, 0, GROUP)
    full = jax.ShapeDtypeStruct((B * S, MAIN_WIDTH), ACT_DTYPE)
    wide = pltpu.VMEM((S, GROUP * PAIR), F32)
    col, _ = _gate_specs(bq, S // bk, bk)
    return pl.pallas_call(
        body, name=name, grid=(B, MAIN_STEPS, nq),
        in_specs=[_q_spec(bq, nq, offs[0], GROUP), _seq_spec(S, offs[1], GROUP), _seq_spec(S, offs[2], GROUP), col,
                  qs],
        out_specs=[qs, seq, seq], out_shape=[full, full, full],
        scratch_shapes=[wide, wide],
        compiler_params=_params('parallel', 'parallel', 'arbitrary'),
    )(*qkv, tot, do)


def _mem_probs(qv, mk):
    s = _dot(qv, mk, NT) * SCALE
    p = jnp.exp(s - jnp.max(s, axis=-1, keepdims=True))
    return p / jnp.sum(p, axis=-1, keepdims=True)


def _mem_fwd(q, q_off, mkv, B, S, name='mem_fwd'):
    M = mkv.shape[0] // B
    bq = _tile(S, 512)
    nq = S // bq

    def body(q_ref, mk_ref, mv_ref, o_ref):
        first = _lane0((bq, PAIR))
        mk, mv = mk_ref[...], mv_ref[...]
        out = [_dot(_mem_probs(qh, mk), mv, NN) for qh in _per_head(q_ref[...])]
        o_ref[...] = _pick(first, out[0], out[1]).astype(ACT_DTYPE)

    return pl.pallas_call(
        body, name=name, grid=(B, N_MEM_PAIRS, nq),
        in_specs=[_q_spec(bq, nq, q_off), _seq_spec(M, 0), _seq_spec(M, N_MEM_PAIRS)],
        out_specs=_q_spec(bq, nq, 0),
        out_shape=jax.ShapeDtypeStruct((B * S, MEM_WIDTH), ACT_DTYPE),
        compiler_params=_params('parallel', 'parallel', 'parallel'),
    )(q, mkv, mkv)


def _mem_bwd(q, q_off, mkv, B, S, do, do_off, name='mem_bwd'):
    M = mkv.shape[0] // B
    bq = _tile(S, 512)
    nq = S // bq

    def body(q_ref, mk_ref, mv_ref, do_ref, dq_ref, dmk_ref, dmv_ref):
        i = pl.program_id(2)

        @pl.when(i == 0)
        def _():
            dmk_ref[...] = jnp.zeros_like(dmk_ref)
            dmv_ref[...] = jnp.zeros_like(dmv_ref)

        qv = q_ref[...]
        dov = do_ref[...]
        mk, mv = mk_ref[...], mv_ref[...]
        first = _lane0((bq, PAIR))
        first_m = _lane0((M, PAIR))
        dqh, dkh, dvh = [], [], []
        for qh, doh in zip(_per_head(qv), _per_head(dov)):
            p = _mem_probs(qh, mk)
            dp = _dot(doh, mv, NT)
            ds = p * (dp - jnp.sum(p * dp, axis=-1, keepdims=True))
            dqh.append(_dot(ds, mk, NN))
            dkh.append(_dot(ds, qv, TN))
            dvh.append(_dot(p, dov, TN))
        dq_ref[...] = (SCALE * _pick(first, dqh[0], dqh[1])).astype(ACT_DTYPE)
        dmk_ref[...] += SCALE * _pick(first_m, dkh[0], dkh[1])
        dmv_ref[...] += _pick(first_m, dvh[0], dvh[1])

    mem_out = jax.ShapeDtypeStruct((B * M, MEM_WIDTH), F32)
    return pl.pallas_call(
        body, name=name, grid=(B, N_MEM_PAIRS, nq),
        in_specs=[_q_spec(bq, nq, q_off), _seq_spec(M, 0), _seq_spec(M, N_MEM_PAIRS), _q_spec(bq, nq, do_off)],
        out_specs=[_q_spec(bq, nq, 0), _seq_spec(M, 0), _seq_spec(M, 0)],
        out_shape=[jax.ShapeDtypeStruct((B * S, MEM_WIDTH), ACT_DTYPE), mem_out, mem_out],
        compiler_params=_params('parallel', 'parallel', 'arbitrary'),
    )(q, mkv, mkv, do)


HALO = 8
CONV_CHUNK = 256


def _conv_chunk(scr, start, rows, w, b):
    at = HALO + start
    return (b + w[0:1, :] * scr[at - 2:at - 2 + rows, :] + w[1:2, :] * scr[at - 1:at - 1 + rows, :]
            + w[2:3, :] * scr[at:at + rows, :])


def _fill_frames(scr, ref):
    scr[0:HALO, :] = jnp.zeros((HALO, scr.shape[1]), F32)
    scr[HALO:, :] = ref[...].astype(F32)


def _sigmoid(x):
    return 0.5 + 0.5 * jnp.tanh(0.5 * x)


def _fold8(x):
    return jnp.sum(x.reshape(x.shape[0] // 8, 8, x.shape[1]), axis=0)


def _conv_specs(S, nf):
    ug = pl.BlockSpec((None, S, LANES), lambda b, j: (b, 0, j))
    uv = pl.BlockSpec((None, S, LANES), lambda b, j: (b, 0, j + nf))
    wg = pl.BlockSpec((3, LANES), lambda b, j: (0, j))
    wv = pl.BlockSpec((3, LANES), lambda b, j: (0, j + nf))
    bg = pl.BlockSpec((1, LANES), lambda b, j: (0, j))
    bv = pl.BlockSpec((1, LANES), lambda b, j: (0, j + nf))
    return ug, uv, wg, wv, bg, bv


def _conv_fwd(u, cw, cb, name='conv_fwd'):
    B, S, F2 = u.shape
    F = F2 // 2
    nf = F // LANES

    ch = min(CONV_CHUNK, S)

    def body(ug_ref, uv_ref, wg_ref, wv_ref, bg_ref, bv_ref, y_ref, g_scr, v_scr):
        _fill_frames(g_scr, ug_ref)
        _fill_frames(v_scr, uv_ref)
        wg, wv, bg, bv = wg_ref[...], wv_ref[...], bg_ref[...], bv_ref[...]
        for start in range(0, S, ch):
            gate = _conv_chunk(g_scr, start, ch, wg, bg)
            val = _conv_chunk(v_scr, start, ch, wv, bv)
            y_ref[start:start + ch, :] = (gate * _sigmoid(gate) * val).astype(ACT_DTYPE)

    specs = _conv_specs(S, nf)
    frames = pltpu.VMEM((HALO + S, LANES), F32)
    return pl.pallas_call(
        body, name=name, grid=(B, nf), in_specs=list(specs), out_specs=specs[0],
        out_shape=jax.ShapeDtypeStruct((B, S, F), ACT_DTYPE), scratch_shapes=[frames, frames],
        compiler_params=_params('parallel', 'parallel'),
    )(u, u, cw, cw, cb, cb)


def _conv_bwd(u, cw, cb, dy, name='conv_bwd'):
    B, S, F2 = u.shape
    F = F2 // 2
    nf = F // LANES

    ch = min(CONV_CHUNK, S)

    def body(ug_ref, uv_ref, wg_ref, wv_ref, bg_ref, bv_ref, dy_ref,
             dug_ref, duv_ref, dwg_ref, dwv_ref, dbg_ref, dbv_ref, g_scr, v_scr, dg_scr, dv_scr):
        b = pl.program_id(1)

        @pl.when(b == 0)
        def _():
            for r in (dwg_ref, dwv_ref, dbg_ref, dbv_ref):
                r[...] = jnp.zeros_like(r)

        _fill_frames(g_scr, ug_ref)
        _fill_frames(v_scr, uv_ref)
        wg, wv, bg, bv = wg_ref[...], wv_ref[...], bg_ref[...], bv_ref[...]
        for scr in (dg_scr, dv_scr):
            scr[S:, :] = jnp.zeros((HALO, LANES), F32)
        for start in range(0, S, ch):
            gate = _conv_chunk(g_scr, start, ch, wg, bg)
            val = _conv_chunk(v_scr, start, ch, wv, bv)
            dyv = dy_ref[start:start + ch, :].astype(F32)
            sg = _sigmoid(gate)
            dv_scr[start:start + ch, :] = dyv * (gate * sg)
            dg_scr[start:start + ch, :] = dyv * val * (sg * (1.0 + gate * (1.0 - sg)))

        for u_scr, d_scr, w, du_ref, dw_ref, db_ref in ((g_scr, dg_scr, wg, dug_ref, dwg_ref, dbg_ref),
                                                         (v_scr, dv_scr, wv, duv_ref, dwv_ref, dbv_ref)):
            sums = [jnp.zeros((8, LANES), F32) for _ in range(4)]
            for start in range(0, S, ch):
                x = u_scr[HALO + start:HALO + start + ch, :]
                d = [d_scr[start + n:start + n + ch, :] for n in range(3)]
                du_ref[start:start + ch, :] = (w[2:3, :] * d[0] + w[1:2, :] * d[1] + w[0:1, :] * d[2]).astype(ACT_DTYPE)
                sums = [sums[0] + _fold8(x * d[2]), sums[1] + _fold8(x * d[1]), sums[2] + _fold8(x * d[0]),
                        sums[3] + _fold8(d[0])]
            total = [jnp.sum(s, axis=0, keepdims=True) for s in sums]
            dw_ref[...] += jnp.concatenate(total[:3], axis=0)
            db_ref[...] += total[3]

    def swap(spec_fn):
        return lambda j, b: spec_fn(b, j)

    ug, uv, wg, wv, bg, bv = _conv_specs(S, nf)
    ins = [pl.BlockSpec(s.block_shape, swap(s.index_map)) for s in (ug, uv, wg, wv, bg, bv, ug)]
    outs = [ins[0], ins[0], ins[2], ins[2], ins[4], ins[4]]
    frames = pltpu.VMEM((HALO + S, LANES), F32)
    return pl.pallas_call(
        body, name=name, grid=(nf, B), in_specs=ins, out_specs=outs, scratch_shapes=[frames] * 4,
        out_shape=[jax.ShapeDtypeStruct((B, S, F), ACT_DTYPE), jax.ShapeDtypeStruct((B, S, F), ACT_DTYPE),
                   jax.ShapeDtypeStruct((3, F), F32), jax.ShapeDtypeStruct((3, F), F32),
                   jax.ShapeDtypeStruct((1, F), F32), jax.ShapeDtypeStruct((1, F), F32)],
        compiler_params=_params('parallel', 'arbitrary'),
    )(u, u, cw, cw, cb, cb, dy)


ADAM_BLOCK_BYTES = 512 * 1024


def _adamw(w, g, m, v, layer, earlier, name):
    L, r, c = w.shape
    tr = r
    if r * c * 4 > ADAM_BLOCK_BYTES and r % 8 == 0:
        tr = 8
        for t in range(8, r + 1, 8):
            if r % t == 0 and t * c * 4 <= ADAM_BLOCK_BYTES:
                tr = t

    def body(w_ref, g_ref, m_ref, v_ref, *rest):
        go_ref, d_ref, nm_ref, nv_ref = rest[-4:]
        gv = g_ref[...]
        nm = ADAM_B1 * m_ref[...] + (1.0 - ADAM_B1) * gv
        nv = ADAM_B2 * v_ref[...] + (1.0 - ADAM_B2) * (gv * gv)
        m_hat = nm / (1.0 - ADAM_B1 ** ADAM_STEP)
        v_hat = nv / (1.0 - ADAM_B2 ** ADAM_STEP)
        d_ref[...] = -ADAM_LR * (m_hat / (jnp.sqrt(v_hat) + ADAM_EPS) + ADAM_WD * w_ref[...])
        nm_ref[...] = nm
        nv_ref[...] = nv
        go_ref[...] = gv

    lay = pl.BlockSpec((None, tr, c), lambda i: (layer, i, 0))
    one = pl.BlockSpec((tr, c), lambda i: (i, 0))
    shp = jax.ShapeDtypeStruct((L, r, c), F32)
    in_specs = [lay, one, lay, lay]
    args = (w, g, m, v)
    aliases = {}
    if earlier is not None:
        in_specs += [ANY] * 4
        args += tuple(earlier)
        aliases = {4 + k: k for k in range(4)}
    return pl.pallas_call(
        body, name=name, grid=(r // tr,), in_specs=in_specs, out_specs=[lay] * 4, out_shape=[shp] * 4,
        input_output_aliases=aliases, compiler_params=_params('parallel'),
    )(*args)


def _my_place():
    return lax.axis_index('x'), lax.axis_index('y'), lax.axis_index('c')


def _other_chips(x, y):
    return [(1 - x, y), (x, 1 - y), (1 - x, 1 - y)]


def _remote(src, dst, send_sem, recv_sem, to):
    return pltpu.make_async_remote_copy(src_ref=src, dst_ref=dst, send_sem=send_sem, recv_sem=recv_sem,
                                        device_id=to, device_id_type=MESH)


def _hbm_call(body, n_in, out_shapes, scratch, name, aliases=None):
    return pl.pallas_call(body, name=name, in_specs=[ANY] * n_in, out_specs=[ANY] * len(out_shapes),
                          out_shape=out_shapes, scratch_shapes=scratch, input_output_aliases=aliases or {})


def _full_shape(shard_shape, kind):
    L, r, c = shard_shape
    return {'rows': (L, N_CHIPS * r, c), 'cols': (L, r, N_CHIPS * c), 'stack': (N_CHIPS * L, r, c)}[kind]


def _place_block(w, kind, out_dtype, chip_arr, name):
    L, r, c = w.shape
    tr = r if r % 16 else _tile(r, max(16, SUM_BLOCK_BYTES // (4 * c)), 16)
    nrt = r // tr

    def body(k_ref, w_ref, o_ref):
        o_ref[...] = w_ref[...].astype(out_dtype)

    out_map = {'rows': lambda l, i, k_ref: (l, k_ref[0] * nrt + i, 0),
               'cols': lambda l, i, k_ref: (l, i, k_ref[0]),
               'stack': lambda l, i, k_ref: (k_ref[0] * L + l, i, 0)}[kind]
    gs = pltpu.PrefetchScalarGridSpec(
        num_scalar_prefetch=1, grid=(L, nrt),
        in_specs=[pl.BlockSpec((None, tr, c), lambda l, i, k_ref: (l, i, 0))],
        out_specs=pl.BlockSpec((None, tr, c), out_map))
    return pl.pallas_call(
        body, name=name, grid_spec=gs, out_shape=jax.ShapeDtypeStruct(_full_shape(w.shape, kind), out_dtype),
        compiler_params=_params('parallel', 'parallel'),
    )(chip_arr, w)


class _Exchange:
    def __init__(self, inputs, out_shapes, aliases, scratch, start, finish):
        self.inputs, self.out_shapes, self.aliases, self.scratch = list(inputs), list(out_shapes), aliases, scratch
        self.start, self.finish = start, finish


def _run_exchange(ex, name):
    n_in, n_out = len(ex.inputs), len(ex.out_shapes)

    def body(*refs):
        parts = refs[:n_in], refs[n_in:n_in + n_out], refs[n_in + n_out:]
        ex.start(*parts)
        ex.finish(*parts)

    return _hbm_call(body, n_in, ex.out_shapes, ex.scratch, name, aliases=ex.aliases)(*ex.inputs)


def _call_beside(body, name, grid, in_specs, out_specs, out_shape, scratch, args, semantics, beside):
    if beside is None:
        outs = pl.pallas_call(body, name=name, grid=grid, in_specs=in_specs, out_specs=out_specs, out_shape=out_shape,
                              scratch_shapes=scratch, compiler_params=_params(*semantics))(*args)
        return outs, None
    n_in, n_out, n_scr = len(in_specs), len(out_specs), len(scratch)
    b_in, b_out = len(beside.inputs), len(beside.out_shapes)

    def carrier(*refs):
        cuts = [n_in, b_in, n_out, b_out, n_scr]
        parts, at = [], 0
        for size in cuts:
            parts.append(refs[at:at + size])
            at += size
        ins, ex_ins, outs, ex_outs, scr = parts
        ex_scr = refs[at:]
        ids = [pl.program_id(d) for d in range(len(grid))]
        first = functools.reduce(jnp.logical_and, [i == 0 for i in ids])
        last = functools.reduce(jnp.logical_and, [i == g - 1 for i, g in zip(ids, grid)])

        @pl.when(first)
        def _():
            beside.start(ex_ins, ex_outs, ex_scr)

        body(*ins, *outs, *scr)

        @pl.when(last)
        def _():
            beside.finish(ex_ins, ex_outs, ex_scr)

    res = pl.pallas_call(
        carrier, name=name, grid=grid, in_specs=list(in_specs) + [ANY] * b_in,
        out_specs=list(out_specs) + [ANY] * b_out, out_shape=list(out_shape) + beside.out_shapes,
        scratch_shapes=list(scratch) + beside.scratch,
        input_output_aliases={n_in + i: n_out + o for i, o in beside.aliases.items()},
        compiler_params=_params(*['arbitrary'] * len(grid)),
    )(*args, *beside.inputs)
    return res[:n_out], res[n_out:]


def _gather_exchange(fulls, shard_shapes, kinds, split):
    n = len(fulls)

    def plan(outs, send_sems, recv_sems):
        x, y, c = _my_place()
        chip = 2 * x + y
        sibling = (x, y, 1 - c)
        others = _other_chips(x, y)

        def window(a, k, half):
            L, r, cols = shard_shapes[a]
            first, count = (0, r) if half is None else (half * (r // 2), r // 2)
            if kinds[a] == 'rows':
                return outs[a].at[:, pl.ds(k * r + first, count), :]
            if kinds[a] == 'cols':
                return outs[a].at[:, pl.ds(first, count), pl.ds(pl.multiple_of(k * cols, LANES), cols)]
            return outs[a].at[pl.ds(k * L, L), pl.ds(first, count), :]

        sends, arrivals, forwards, forwarded = [], [], [], []
        for a in range(n):
            half = c if split[a] else None
            for j, (ox, oy) in enumerate(others):
                sems = (send_sems.at[6 * a + j], recv_sems.at[6 * a + j], (ox, oy, c))
                sends.append(_remote(window(a, chip, half), window(a, chip, half), *sems))
                got = window(a, 2 * ox + oy, half)
                arrivals.append(_remote(got, got, *sems))
                if split[a]:
                    sems = (send_sems.at[6 * a + 3 + j], recv_sems.at[6 * a + 3 + j], sibling)
                    forwards.append(_remote(got, got, *sems))
                    theirs = window(a, 2 * ox + oy, 1 - c)
                    forwarded.append(_remote(theirs, theirs, *sems))
                else:
                    forwards.append(None)
        return sends, arrivals, forwards, forwarded

    def start(ins, outs, scratch):
        sends, _, _, _ = plan(outs, *scratch)
        for cp in sends:
            cp.start()

    def finish(ins, outs, scratch):
        sends, arrivals, forwards, forwarded = plan(outs, *scratch)
        for arrived, fw in zip(arrivals, forwards):
            arrived.wait_recv()
            if fw is not None:
                fw.start()
        for cp in forwarded:
            cp.wait_recv()
        for cp in sends + [fw for fw in forwards if fw is not None]:
            cp.wait_send()

    scratch = [pltpu.SemaphoreType.DMA((6 * n,)), pltpu.SemaphoreType.DMA((6 * n,))]
    out_shapes = [jax.ShapeDtypeStruct(f.shape, f.dtype) for f in fulls]
    return _Exchange(fulls, out_shapes, {a: a for a in range(n)}, scratch, start, finish)


def _swap_cores(gs, name='swap_cores'):
    n = len(gs)
    out_shapes = [jax.ShapeDtypeStruct((g.shape[0], g.shape[1] // 2, g.shape[2]), g.dtype) for g in gs]

    def body(*refs):
        ins, outs = refs[:n], refs[n:2 * n]
        send_sems, recv_sems = refs[2 * n:]
        x, y, c = _my_place()
        cps = []
        for a in range(n):
            rh = gs[a].shape[1] // 2
            cp = _remote(ins[a].at[:, pl.ds((1 - c) * rh, rh), :], outs[a], send_sems.at[a], recv_sems.at[a],
                         (x, y, 1 - c))
            cp.start()
            cps.append(cp)
        for cp in cps:
            cp.wait()

    scratch = [pltpu.SemaphoreType.DMA((n,)), pltpu.SemaphoreType.DMA((n,))]
    return _hbm_call(body, n, out_shapes, scratch, name)(*gs)


SUM_BLOCK_BYTES = 2 * 1024 * 1024


def _sum_rows(rh, cols):
    return _tile(rh, max(16, SUM_BLOCK_BYTES // (4 * cols)), 16)


def _add_cores(g, other, c_arr, wire_dtype, name):
    n, r, cols = g.shape
    rh = r // 2
    tr = _sum_rows(rh, cols)
    nrt = rh // tr

    def body(c_ref, g_ref, o_ref, q_ref):
        q_ref[...] = (g_ref[...] + o_ref[...]).astype(wire_dtype)

    gs = pltpu.PrefetchScalarGridSpec(
        num_scalar_prefetch=1, grid=(n, nrt),
        in_specs=[pl.BlockSpec((None, tr, cols), lambda j, i, c_ref: (j, c_ref[0] * nrt + i, 0)),
                  pl.BlockSpec((None, tr, cols), lambda j, i, c_ref: (j, i, 0))],
        out_specs=pl.BlockSpec((None, tr, cols), lambda j, i, c_ref: (j, i, 0)))
    return pl.pallas_call(
        body, name=name, grid_spec=gs, out_shape=jax.ShapeDtypeStruct((n, rh, cols), wire_dtype),
        compiler_params=_params('parallel', 'parallel'),
    )(c_arr, g, other)


def _send_exchange(qs):
    n = len(qs)

    def plan(ins, outs, send_sems, recv_sems):
        x, y, c = _my_place()
        return [_remote(ins[a].at[2 * ox + oy], outs[a].at[j], send_sems.at[3 * a + j], recv_sems.at[3 * a + j],
                        (ox, oy, c))
                for a in range(n) for j, (ox, oy) in enumerate(_other_chips(x, y))]

    def start(ins, outs, scratch):
        for cp in plan(ins, outs, *scratch):
            cp.start()

    def finish(ins, outs, scratch):
        cps = plan(ins, outs, *scratch)
        for cp in cps:
            cp.wait_recv()
        for cp in cps:
            cp.wait_send()

    scratch = [pltpu.SemaphoreType.DMA((3 * n,)), pltpu.SemaphoreType.DMA((3 * n,))]
    out_shapes = [jax.ShapeDtypeStruct((3,) + q.shape[1:], q.dtype) for q in qs]
    return _Exchange(qs, out_shapes, {}, scratch, start, finish)


def _sum_chips(q, got, place_arr, name):
    n, rh, cols = q.shape
    tr = _sum_rows(rh, cols)

    def body(p_ref, q_ref, gx_ref, gy_ref, gxy_ref, o_ref):
        f = lambda r: r[...].astype(F32)
        o_ref[...] = (f(q_ref) + f(gxy_ref)) + (f(gx_ref) + f(gy_ref))

    def got_spec(j):
        return pl.BlockSpec((None, tr, cols), lambda i, p_ref: (j, i, 0))

    gs = pltpu.PrefetchScalarGridSpec(
        num_scalar_prefetch=1, grid=(rh // tr,),
        in_specs=[pl.BlockSpec((None, tr, cols), lambda i, p_ref: (p_ref[0], i, 0)),
                  got_spec(0), got_spec(1), got_spec(2)],
        out_specs=pl.BlockSpec((None, tr, cols), lambda i, p_ref: (p_ref[1], i, 0)))
    return pl.pallas_call(
        body, name=name, grid_spec=gs, out_shape=jax.ShapeDtypeStruct((2, rh, cols), F32),
        compiler_params=_params('parallel'),
    )(place_arr, q, got, got, got)


def _join_cores(rs, name='join_cores'):
    n = len(rs)
    out_shapes = [jax.ShapeDtypeStruct(r.shape, r.dtype) for r in rs]

    def body(*refs):
        outs = refs[n:2 * n]
        send_sems, recv_sems = refs[2 * n:]
        x, y, c = _my_place()
        cps = []
        for a in range(n):
            cp = _remote(outs[a].at[c], outs[a].at[c], send_sems.at[a], recv_sems.at[a], (x, y, 1 - c))
            cp.start()
            cps.append(cp)
        for cp in cps:
            cp.wait()

    scratch = [pltpu.SemaphoreType.DMA((n,)), pltpu.SemaphoreType.DMA((n,))]
    return _hbm_call(body, n, out_shapes, scratch, name, aliases={a: a for a in range(n)})(*rs)


def _gate_rows(t, B, S):
    return t.reshape(B, S, N_MAIN_HEADS).transpose(0, 2, 1).reshape(B * N_MAIN_HEADS, S)


def _gate_cols(t, B, S):
    return t.reshape(B, N_MAIN_HEADS, S).transpose(0, 2, 1).reshape(B * S, N_MAIN_HEADS)


def _mem_kv_fwd(mem2, g, w, tag):
    hm = _rms_fwd(mem2, g, name=f'rms_mem_{tag}')
    mkv = _matmul(hm, w, 'nn', ACT_DTYPE, name=f'mm_memkv_{tag}')
    return hm, mkv


def _mem_kv_bwd(mem2, g, w, hm, dmk, dmv, tag):
    dmkv = jnp.concatenate([dmk, dmv], axis=1)
    dw = _matmul(hm, dmkv, 'tn', F32, name=f'mm_memkv_dw_{tag}')
    dhm = _matmul(dmkv, w, 'nt', F32, name=f'mm_memkv_dx_{tag}')
    _, dg = _rms_bwd(mem2, g, dhm, None, name=f'rms_mem_bwd_{tag}')
    return dw, dg


def _ffn_fwd(x, g, w_up, cw, cb, w_down, B, S, tag):
    T = x.shape[0]
    h2 = _rms_fwd(x, g, name=f'rms_ffn_{tag}')
    u = _matmul(h2, w_up, 'nn', ACT_DTYPE, name=f'mm_up_{tag}')
    y = _conv_fwd(u.reshape(B, S, -1), cw, cb, name=f'conv_fwd_{tag}').reshape(T, -1)
    x2 = _matmul(y, w_down, 'nn', F32, res=x, name=f'mm_down_{tag}')
    return x2, (h2, u, y)


def _ffn_bwd(dx2, x, g, w_up, cw, cb, w_down, saved, B, S, tag):
    h2, u, y = saved
    T = x.shape[0]
    dy = _matmul(dx2, w_down, 'nt', ACT_DTYPE, name=f'mm_down_dx_{tag}')
    dw_down = _matmul(y, dx2, 'tn', F32, name=f'mm_down_dw_{tag}')
    dug, duv, dcwg, dcwv, dcbg, dcbv = _conv_bwd(u.reshape(B, S, -1), cw, cb, dy.reshape(B, S, -1),
                                                  name=f'conv_bwd_{tag}')
    dug, duv = dug.reshape(T, -1), duv.reshape(T, -1)
    dh2 = _matmul((dug, duv), w_up, 'nt', F32, name=f'mm_up_dx_{tag}')
    half = N_CHIPS // 2
    dw_gate = _matmul(h2, dug, 'tn', F32, slots=half, total_slots=N_CHIPS, name=f'mm_up_dw_gate_{tag}')
    dw_up = _matmul(h2, duv, 'tn', F32, slots=half, slot_base=half, total_slots=N_CHIPS, into=dw_gate,
                    name=f'mm_up_dw_val_{tag}')
    dx, dg = _rms_bwd(x, g, dh2, dx2, name=f'rms_ffn_bwd_{tag}')
    dcw = jnp.concatenate([dcwg, dcwv], axis=1)
    dcb = jnp.concatenate([dcbg, dcbv], axis=1)
    return dx, dg, dw_up, dcw, dcb, dw_down


def _step(x, mem, tgt, W, late_weights=None, reduce_early=None):
    B, S, D = x.shape
    T = B * S
    x0 = x.reshape(T, D)
    mem2 = mem.reshape(-1, D)
    tgt2 = tgt.reshape(T, D)
    row = lambda v: v.reshape(1, -1)
    q3 = 3 * MAIN_WIDTH

    w_in_a = W['w_in_a'][0]
    wa_main = jnp.concatenate([w_in_a[:, :q3], w_in_a[:, q3 + N_MAIN_HEADS:]], axis=1)
    wa_gate = jnp.pad(w_in_a[:, q3:q3 + N_MAIN_HEADS], ((0, 0), (0, LANES - N_MAIN_HEADS)))
    bcol = jnp.tile(W['b_f_a'][0], B).reshape(B * N_MAIN_HEADS, 1)
    nkb = S // _blocks(S, FOX_ROWS, FOX_KEYS)[1]

    h1a = _rms_fwd(x0, row(W['ln_mix_g'][0]), name='rms_mix_a')
    pa = _matmul(h1a, wa_main, 'nn', ACT_DTYPE, name='mm_in_a')
    flog = _matmul(h1a, wa_gate, 'nn', F32, name='mm_gate_a')
    qkv_a = (pa, pa, pa)
    offs_a = (0, N_MAIN_PAIRS, 2 * N_MAIN_PAIRS)
    qm_off_a = 3 * N_MAIN_PAIRS
    zt = _gate_rows(flog[:, :N_MAIN_HEADS], B, S)
    cum = _gate_fwd(zt, bcol)
    ccol = cum.reshape(B * N_MAIN_HEADS, S, 1)
    crow = cum.reshape(B * N_MAIN_HEADS, nkb, 1, S // nkb)
    (oa, lse), late = _fox_fwd(qkv_a, offs_a, B, S, ccol, crow, beside=late_weights[0] if late_weights else None)
    if late_weights:
        W = {**W, **late_weights[1](late)}
    w_in_b = W['w_in_b'][0]
    hma, mkva = _mem_kv_fwd(mem2, row(W['ln_mem_g'][0]), W['w_memkv'][0], 'a')
    oma = _mem_fwd(pa, qm_off_a, mkva, B, S, name='mem_fwd_a')
    ocat_a = jnp.concatenate([oa, oma], axis=1)
    x1 = _matmul(ocat_a, W['w_out'][0], 'nn', F32, res=x0, name='mm_out_a')
    x2, ffn_a = _ffn_fwd(x1, row(W['ln_ffn_g'][0]), W['w_up'][0], W['conv_w'][0], row(W['conv_b'][0]),
                         W['w_down'][0], B, S, 'a')
    hkv = _rms_fwd(x2, row(W['ln_kv_g']), name='rms_kv')
    kvs = _matmul(hkv, W['w_kv'], 'nn', ACT_DTYPE, name='mm_kv')
    h1b = _rms_fwd(x2, row(W['ln_mix_g'][1]), name='rms_mix_b')
    pb = _matmul(h1b, w_in_b, 'nn', ACT_DTYPE, name='mm_in_b')
    qkv_b = (pb, kvs, kvs)
    offs_b = (0, 0, N_MAIN_PAIRS)
    qm_off_b = N_MAIN_PAIRS
    ob, tot_b = _sb_fwd(qkv_b, offs_b, B, S)
    hmb, mkvb = _mem_kv_fwd(mem2, row(W['ln_mem_g'][1]), W['w_memkv'][1], 'b')
    omb = _mem_fwd(pb, qm_off_b, mkvb, B, S, name='mem_fwd_b')
    ocat_b = jnp.concatenate([ob, omb], axis=1)
    x3 = _matmul(ocat_b, W['w_out'][1], 'nn', F32, res=x2, name='mm_out_b')
    x4, ffn_b = _ffn_fwd(x3, row(W['ln_ffn_g'][1]), W['w_up'][1], W['conv_w'][1], row(W['conv_b'][1]),
                         W['w_down'][1], B, S, 'b')
    loss, dx4, d_final_g = _final_loss(x4, row(W['final_g']), tgt2)

    dx3, dg_ffn_b, dw_up_b, dcw_b, dcb_b, dw_down_b = _ffn_bwd(
        dx4, x3, row(W['ln_ffn_g'][1]), W['w_up'][1], W['conv_w'][1], row(W['conv_b'][1]), W['w_down'][1],
        ffn_b, B, S, 'b')
    docat = _matmul(dx3, W['w_out'][1], 'nt', ACT_DTYPE, name='mm_out_dx_b')
    dw_out_b = _matmul(ocat_b, dx3, 'tn', F32, name='mm_out_dw_b')
    dqb, dkb, dvb = _sb_bwd(qkv_b, offs_b, B, S, tot_b, docat)
    dqmb, dmkb, dmvb = _mem_bwd(pb, qm_off_b, mkvb, B, S, docat, N_MAIN_PAIRS, name='mem_bwd_b')
    dw_memkv_b, dg_mem_b = _mem_kv_bwd(mem2, row(W['ln_mem_g'][1]), W['w_memkv'][1], hmb, dmkb, dmvb, 'b')
    dpb = jnp.concatenate([dqb, dqmb], axis=1)
    dh1b = _matmul(dpb, w_in_b, 'nt', F32, name='mm_in_dx_b')
    dw_in_b = _matmul(h1b, dpb, 'tn', F32, name='mm_in_dw_b')
    dx2, dg_mix_b = _rms_bwd(x2, row(W['ln_mix_g'][1]), dh1b, dx3, name='rms_mix_bwd_b')
    dkvs = jnp.concatenate([dkb, dvb], axis=1)
    dhkv = _matmul(dkvs, W['w_kv'], 'nt', F32, name='mm_kv_dx')
    dw_kv = _matmul(hkv, dkvs, 'tn', F32, slots=N_CHIPS, name='mm_kv_dw')
    dx2, dg_kv = _rms_bwd(x2, row(W['ln_kv_g']), dhkv, dx2, name='rms_kv_bwd')

    dx1, dg_ffn_a, dw_up_a, dcw_a, dcb_a, dw_down_a = _ffn_bwd(
        dx2, x1, row(W['ln_ffn_g'][0]), W['w_up'][0], W['conv_w'][0], row(W['conv_b'][0]), W['w_down'][0],
        ffn_a, B, S, 'a')
    docat = _matmul(dx1, W['w_out'][0], 'nt', ACT_DTYPE, name='mm_out_dx_a')
    dw_out_a = _matmul(ocat_a, dx1, 'tn', F32, name='mm_out_dw_a')

    def by_rows(dw):
        return dw.reshape(N_CHIPS, dw.shape[0] // N_CHIPS, dw.shape[1])

    grads = {
        'w_in_b': [by_rows(dw_in_b)],
        'w_kv': [dw_kv],
        'w_out': [by_rows(dw_out_a), by_rows(dw_out_b)],
        'w_up': [dw_up_a, dw_up_b],
        'w_down': [by_rows(dw_down_a), by_rows(dw_down_b)],
    }
    early = [(n, layer, g) for n, gs in grads.items() for layer, g in enumerate(gs)]
    early.append(('w_memkv', 1, by_rows(dw_memkv_b)))
    beside = reduce_early(early) if reduce_early else None
    (dqa, dka, dva, dccol, dcrow), crossed = _fox_bwd(qkv_a, offs_a, B, S, ccol, crow, oa, lse, docat, beside=beside)
    dzt, dbrow = _gate_bwd(zt, bcol, dccol.reshape(B * N_MAIN_HEADS, S) + dcrow.reshape(B * N_MAIN_HEADS, S))
    dqma, dmka, dmva = _mem_bwd(pa, qm_off_a, mkva, B, S, docat, N_MAIN_PAIRS, name='mem_bwd_a')
    dw_memkv_a, dg_mem_a = _mem_kv_bwd(mem2, row(W['ln_mem_g'][0]), W['w_memkv'][0], hma, dmka, dmva, 'a')
    dpa = jnp.concatenate([dqa, dka, dva, dqma], axis=1)
    dflog = jnp.pad(_gate_cols(dzt, B, S), ((0, 0), (0, LANES - N_MAIN_HEADS)))
    dh1a = _matmul(dpa, wa_main, 'nt', F32, name='mm_in_dx_a')
    dh1a = _matmul(dflog, wa_gate, 'nt', F32, res=dh1a, name='mm_gate_dx_a')
    dwa_main = _matmul(h1a, dpa, 'tn', F32, name='mm_in_dw_a')
    dwa_gate = _matmul(h1a, dflog, 'tn', F32, name='mm_gate_dw_a')
    dx0, dg_mix_a = _rms_bwd(x0, row(W['ln_mix_g'][0]), dh1a, dx1, name='rms_mix_bwd_a')

    dw_in_a = jnp.concatenate([dwa_main[:, :q3], dwa_gate[:, :N_MAIN_HEADS], dwa_main[:, q3:]], axis=1)
    grads.update({
        'ln_mix_g': jnp.concatenate([dg_mix_a, dg_mix_b], axis=0),
        'w_in_a': dw_in_a[None],
        'b_f_a': dbrow.reshape(B, N_MAIN_HEADS).sum(axis=0)[None],
        'ln_kv_g': dg_kv[0],
        'ln_mem_g': jnp.concatenate([dg_mem_a, dg_mem_b], axis=0),
        'w_memkv': [by_rows(dw_memkv_a), early[-1][2]],
        'ln_ffn_g': jnp.concatenate([dg_ffn_a, dg_ffn_b], axis=0),
        'conv_w': jnp.stack([dcw_a, dcw_b]),
        'conv_b': jnp.concatenate([dcb_a, dcb_b], axis=0),
        'final_g': d_final_g[0],
    })
    return loss, dx0.reshape(B, S, D), grads, crossed


BLOCKED = ('w_in_b', 'w_kv', 'w_memkv', 'w_out', 'w_up', 'w_down')
MISC_ROWS = 32


def _misc_names():
    return [n for n in PARAM_NAMES if PARAM_SHARD_AXIS[n] is None] + ['conv_w']


def _reduce_begin(arrays, wire, tag):
    _, _, c = _my_place()
    c_arr = jnp.reshape(c, (1,)).astype(jnp.int32)
    others = _swap_cores(arrays, name=f'swap_cores_{tag}')
    return [_add_cores(g, o, c_arr, wire[i], name=f'add_cores_{tag}_{i}')
            for i, (g, o) in enumerate(zip(arrays, others))]


def _reduce_end(qs, crossed, tag):
    x, y, c = _my_place()
    place_arr = jnp.stack([2 * x + y, c]).astype(jnp.int32)
    sums = [_sum_chips(q, g, place_arr, name=f'sum_chips_{tag}_{i}') for i, (q, g) in enumerate(zip(qs, crossed))]
    return [j.reshape(-1, j.shape[-1]) for j in _join_cores(sums, name=f'join_cores_{tag}')]


def _pack_late(grads, shards):
    a_cols = shards['w_in_a'].shape[2]
    a_pad = -(-a_cols // LANES) * LANES
    dw_in_a = grads['w_in_a'][0]
    in_a = jnp.stack([jnp.pad(dw_in_a[:, k * a_cols:(k + 1) * a_cols], ((0, 0), (0, a_pad - a_cols)))
                      for k in range(N_CHIPS)])
    conv_cols = shards['conv_w'].shape[2]
    misc = []
    for k in range(N_CHIPS):
        parts = [grads[n].reshape(-1) for n in _misc_names()[:-1]]
        parts.append(grads['conv_w'][:, :, k * conv_cols:(k + 1) * conv_cols].reshape(-1))
        flat = jnp.concatenate(parts)
        assert flat.shape[0] <= MISC_ROWS * PACK_COLS
        misc.append(jnp.pad(flat, (0, MISC_ROWS * PACK_COLS - flat.shape[0])).reshape(MISC_ROWS, PACK_COLS))
    return in_a, jnp.stack(misc)


def _unpack_misc(rows, shards):
    flat = rows.reshape(-1)
    out, off = {}, 0
    for name in _misc_names():
        shape = shards[name].shape
        size = math.prod(shape)
        out[name] = flat[off:off + size].reshape(-1, shape[-1])
        off += size
    return out


def kernel(x, mem, ln_mix_g, w_in_a, b_f_a, w_in_b, ln_kv_g, w_kv, ln_mem_g, w_memkv, w_out, ln_ffn_g, w_up, conv_w, conv_b, w_down, final_g, loss_target, m_ln_mix_g, m_w_in_a, m_b_f_a, m_w_in_b, m_ln_kv_g, m_w_kv, m_ln_mem_g, m_w_memkv, m_w_out, m_ln_ffn_g, m_w_up, m_conv_w, m_conv_b, m_w_down, m_final_g, v_ln_mix_g, v_w_in_a, v_b_f_a, v_w_in_b, v_ln_kv_g, v_w_kv, v_ln_mem_g, v_w_memkv, v_w_out, v_ln_ffn_g, v_w_up, v_conv_w, v_conv_b, v_w_down, v_final_g):
    shards = dict(ln_mix_g=ln_mix_g, w_in_a=w_in_a, b_f_a=b_f_a, w_in_b=w_in_b, ln_kv_g=ln_kv_g, w_kv=w_kv,
                  ln_mem_g=ln_mem_g, w_memkv=w_memkv, w_out=w_out, ln_ffn_g=ln_ffn_g, w_up=w_up, conv_w=conv_w,
                  conv_b=conv_b, w_down=w_down, final_g=final_g)
    moments_m = dict(ln_mix_g=m_ln_mix_g, w_in_a=m_w_in_a, b_f_a=m_b_f_a, w_in_b=m_w_in_b, ln_kv_g=m_ln_kv_g,
                     w_kv=m_w_kv, ln_mem_g=m_ln_mem_g, w_memkv=m_w_memkv, w_out=m_w_out, ln_ffn_g=m_ln_ffn_g,
                     w_up=m_w_up, conv_w=m_conv_w, conv_b=m_conv_b, w_down=m_w_down, final_g=m_final_g)
    moments_v = dict(ln_mix_g=v_ln_mix_g, w_in_a=v_w_in_a, b_f_a=v_b_f_a, w_in_b=v_w_in_b, ln_kv_g=v_ln_kv_g,
                     w_kv=v_w_kv, ln_mem_g=v_ln_mem_g, w_memkv=v_w_memkv, w_out=v_w_out, ln_ffn_g=v_ln_ffn_g,
                     w_up=v_w_up, conv_w=v_conv_w, conv_b=v_conv_b, w_down=v_w_down, final_g=v_final_g)

    kinds = {'w_in_a': 'stack', 'w_in_b': 'rows', 'w_kv': 'cols', 'w_memkv': 'rows', 'w_out': 'rows', 'w_up': 'cols',
             'w_down': 'rows', 'conv_w': 'cols'}
    mx, my, _ = _my_place()
    chip_arr = jnp.reshape(2 * mx + my, (1,)).astype(jnp.int32)
    placed, shard_shapes = {}, {}
    for n, kind in kinds.items():
        w = shards[n].reshape((-1,) + shards[n].shape[-2:])
        shard_shapes[n] = w.shape
        placed[n] = _place_block(w, kind, F32 if n in F32_GATHERED else jnp.bfloat16, chip_arr, name=f'place_{n}')

    def gather(names):
        return _gather_exchange([placed[n] for n in names], [shard_shapes[n] for n in names],
                                [kinds[n] for n in names], [n not in F32_GATHERED for n in names])

    def as_weights(names, full):
        out = dict(zip(names, full))
        if 'w_in_a' in out:
            out['w_in_a'] = jnp.concatenate([out['w_in_a'][k] for k in range(N_CHIPS)], axis=1)[None]
        if 'w_kv' in out:
            out['w_kv'] = out['w_kv'][0]
        return out

    first = ['w_in_a', 'conv_w']
    late = [n for n in kinds if n not in first]
    W = {**shards, **as_weights(first, _run_exchange(gather(first), 'gather_first'))}

    early = {}

    def reduce_early(items):
        early['owners'] = [(n, layer) for n, layer, _ in items]
        early['qs'] = _reduce_begin([g for _, _, g in items], [jnp.bfloat16] * len(items), 'early')
        return _send_exchange(early['qs'])

    loss_part, grad_x, grads, crossed = _step(x, mem, loss_target, W, (gather(late), functools.partial(as_weights, late)),
                                              reduce_early)
    loss = lax.psum(loss_part[0, 0], ('x', 'y', 'c'))

    g_layers = {n: [None] * (len(grads[n]) if n in BLOCKED else 1) for n in PARAM_NAMES}
    for (n, layer), g in zip(early['owners'], _reduce_end(early['qs'], crossed, 'early')):
        g_layers[n][layer] = g
    in_a, misc = _pack_late(grads, shards)
    qs = _reduce_begin([in_a, misc, grads['w_memkv'][0]], [jnp.bfloat16, F32, jnp.bfloat16], 'late')
    in_a_sum, misc_sum, memkv_sum = _reduce_end(qs, _run_exchange(_send_exchange(qs), 'send_chips_late'), 'late')
    g_layers['w_in_a'][0] = in_a_sum[:, :shards['w_in_a'].shape[2]]
    g_layers['w_memkv'][0] = memkv_sum
    for n, g in _unpack_misc(misc_sum, shards).items():
        g_layers[n][0] = g

    results = {}
    for name in PARAM_NAMES:
        w = shards[name]
        layers = len(g_layers[name])
        as_layers = (layers, -1, w.shape[-1])
        w3, m3, v3 = (t.reshape(as_layers) for t in (w, moments_m[name], moments_v[name]))
        res = None
        for layer, g in enumerate(g_layers[name]):
            res = _adamw(w3, g, m3, v3, layer, res, name=f'adamw_{name}_{layer}')
        results[name] = [t.reshape(w.shape) for t in res]

    return (loss, grad_x, *[results[n][k] for k in range(4) for n in PARAM_NAMES])
```

```python
import functools
import math

import jax
import jax.numpy as jnp
from jax import lax
from jax.experimental import pallas as pl
from jax.experimental.pallas import tpu as pltpu

F32 = jnp.float32
MXU_DTYPE = jnp.bfloat16
ACT_DTYPE = jnp.bfloat16

HEAD_DIM = 64
N_MAIN_HEADS = 12
N_MEM_HEADS = 4
MAIN_WIDTH = N_MAIN_HEADS * HEAD_DIM
MEM_WIDTH = N_MEM_HEADS * HEAD_DIM
EPS = 1e-6
SCALE = HEAD_DIM ** -0.5
NEG_BIG = -1e30
LANES = 128
PACK_COLS = 1024
N_CHIPS = 4

ADAM_LR = 0.001
ADAM_B1 = 0.9
ADAM_B2 = 0.999
ADAM_EPS = 1e-08
ADAM_WD = 0.01
ADAM_STEP = 10

MESH = pl.DeviceIdType.MESH
ANY = pl.BlockSpec(memory_space=pl.ANY)

PARAM_SHARD_AXIS = {
    'ln_mix_g': None, 'w_in_a': 2, 'b_f_a': None, 'w_in_b': 1, 'ln_kv_g': None, 'w_kv': 1,
    'ln_mem_g': None, 'w_memkv': 1, 'w_out': 1, 'ln_ffn_g': None, 'w_up': 2, 'conv_w': 2,
    'conv_b': None, 'w_down': 1, 'final_g': None,
}
PARAM_NAMES = list(PARAM_SHARD_AXIS)
F32_GATHERED = ('conv_w',)


def _tile(n, pref, unit=LANES):
    if n <= pref:
        return n
    best = None
    for t in range(unit, pref + 1, unit):
        if n % t == 0:
            best = t
    assert best is not None, (n, pref)
    return best


MM_ACC_ELEMS = 768 * 1024
MM_VMEM_MB = 56
MM_TILE_BYTES = 42 << 20


def _out_tiles(M, N):
    def divisors(n, cap):
        if n <= LANES:
            return [n]
        return [t for t in range(LANES, min(n, cap) + 1, LANES) if n % t == 0]

    best = None
    for tm in divisors(M, 1536):
        for tn in divisors(N, 2048):
            if tm * tn <= MM_ACC_ELEMS and (best is None or (tm * tn, tn) > (best[0] * best[1], best[1])):
                best = (tm, tn)
    assert best is not None, (M, N)
    return best


def _params(*sem, vmem_mb=None):
    kw = {}
    if sem:
        kw['dimension_semantics'] = sem
    if vmem_mb is not None:
        kw['vmem_limit_bytes'] = vmem_mb * 1024 * 1024
    return pltpu.CompilerParams(**kw)


def _dot(a, b, dims):
    return lax.dot_general(a.astype(MXU_DTYPE), b.astype(MXU_DTYPE), (dims, ((), ())),
                           preferred_element_type=F32)


NN = ((1,), (0,))
NT = ((1,), (1,))
TN = ((0,), (0,))


def _matmul(a, b, mode, out_dtype, res=None, slots=1, slot_base=0, total_slots=None, into=None, name='mm'):
    pieces = a if isinstance(a, tuple) else (a,)
    a = pieces[0]
    k_sizes = [p.shape[0 if mode == 'tn' else 1] for p in pieces]
    if mode == 'nn':
        (M, _), (K2, N) = a.shape, b.shape
    elif mode == 'nt':
        (M, _), (N, K2) = a.shape, b.shape
    else:
        (_, M), (K2, N) = a.shape, b.shape
    K = sum(k_sizes)
    assert K == K2 and N % slots == 0, (a.shape, b.shape, mode, slots)
    slot_cols = N // slots
    tm, tn = _out_tiles(M, slot_cols)
    per_slot = slot_cols // tn
    fixed = tm * tn * (4 + 2 * jnp.dtype(out_dtype).itemsize + (8 if res is not None else 0))
    per_k = 2 * (tm * a.dtype.itemsize + tn * b.dtype.itemsize)
    tk = _tile(K, max(LANES, (MM_TILE_BYTES - fixed) // per_k))
    nk = K // tk
    dims = {'nn': NN, 'nt': NT, 'tn': TN}[mode]
    a_again = M * K * a.dtype.itemsize * (N // tn)
    b_again = b.size * b.dtype.itemsize * (M // tm)
    m_inner = nk == 1 and a_again < b_again
    n_a = len(pieces)
    assert n_a == 1 or (nk == 1 and mode != 'tn')

    def body(*refs):
        a_refs, b_ref = refs[:n_a], refs[n_a]
        a_ref = a_refs[0]
        r_ref = refs[n_a + 1] if res is not None else None
        o_ref = refs[n_a + 1 + (res is not None) + (into is not None)]

        def finish(out):
            if r_ref is not None:
                out = out + r_ref[...]
            o_ref[...] = out.astype(out_dtype)

        if nk == 1:
            out, at = None, 0
            for ref, size in zip(a_refs, k_sizes):
                part = b_ref[...] if n_a == 1 else (b_ref[:, at:at + size] if mode == 'nt' else b_ref[at:at + size, :])
                term = _dot(ref[...], part, dims)
                out, at = (term if out is None else out + term), at + size
            finish(out)
            return
        acc = refs[-1]
        k = pl.program_id(2)

        @pl.when(k == 0)
        def _():
            acc[...] = jnp.zeros_like(acc)

        acc[...] += _dot(a_ref[...], b_ref[...], dims)

        @pl.when(k == nk - 1)
        def _():
            finish(acc[...])

    def spec(shape, index):
        return pl.BlockSpec(shape, (lambda j, i, k: index(i, j, k)) if m_inner else index)

    if n_a > 1:
        a_specs = [spec((tm, size), lambda i, j, k: (i, 0)) for size in k_sizes]
    else:
        a_specs = [spec((tk, tm), lambda i, j, k: (k, i)) if mode == 'tn' else spec((tm, tk), lambda i, j, k: (i, k))]
    b_spec = spec((tn, tk), lambda i, j, k: (j, k)) if mode == 'nt' else spec((tk, tn), lambda i, j, k: (k, j))
    if slots == 1 and total_slots is None:
        o_spec = spec((tm, tn), lambda i, j, k: (i, j))
        out_shape = jax.ShapeDtypeStruct((M, N), out_dtype)
    else:
        assert res is None
        o_spec = spec((None, tm, tn), lambda i, j, k: (slot_base + j // per_slot, i, j % per_slot))
        out_shape = jax.ShapeDtypeStruct((total_slots or slots, M, slot_cols), out_dtype)
    in_specs = a_specs + [b_spec] + ([o_spec] if res is not None else []) + ([ANY] if into is not None else [])
    args = pieces + (b,) + ((res,) if res is not None else ()) + ((into,) if into is not None else ())
    return pl.pallas_call(
        body, name=name, grid=(N // tn, M // tm, nk) if m_inner else (M // tm, N // tn, nk),
        in_specs=in_specs, out_specs=o_spec,
        out_shape=out_shape,
        input_output_aliases={len(args) - 1: 0} if into is not None else {},
        scratch_shapes=[] if nk == 1 else [pltpu.VMEM((tm, tn), F32)],
        compiler_params=_params('parallel', 'parallel', 'arbitrary', vmem_mb=MM_VMEM_MB),
    )(*args)


def _rms_fwd(x, g, name):
    T, D = x.shape
    tr = _tile(T, 512)

    def body(x_ref, g_ref, o_ref):
        xv = x_ref[...]
        r = lax.rsqrt(jnp.mean(xv * xv, axis=-1, keepdims=True) + EPS)
        o_ref[...] = (xv * r * g_ref[...]).astype(ACT_DTYPE)

    return pl.pallas_call(
        body, name=name, grid=(T // tr,),
        in_specs=[pl.BlockSpec((tr, D), lambda i: (i, 0)), pl.BlockSpec((1, D), lambda i: (0, 0))],
        out_specs=pl.BlockSpec((tr, D), lambda i: (i, 0)),
        out_shape=jax.ShapeDtypeStruct((T, D), ACT_DTYPE),
        compiler_params=_params('parallel'),
    )(x, g)


def _rms_bwd(x, g, dh, dres, name):
    T, D = x.shape
    tr = _tile(T, 512)
    want_dx = dres is not None

    def body(*refs):
        if want_dx:
            x_ref, g_ref, dh_ref, dres_ref, dx_ref, dg_ref = refs
        else:
            x_ref, g_ref, dh_ref, dg_ref = refs
        i = pl.program_id(0)

        @pl.when(i == 0)
        def _():
            dg_ref[...] = jnp.zeros_like(dg_ref)

        xv = x_ref[...]
        dhv = dh_ref[...].astype(F32)
        r = lax.rsqrt(jnp.mean(xv * xv, axis=-1, keepdims=True) + EPS)
        n = xv * r
        dg_ref[...] += jnp.sum(dhv * n, axis=0, keepdims=True)
        if want_dx:
            dn = dhv * g_ref[...]
            dx = r * (dn - n * jnp.mean(dn * n, axis=-1, keepdims=True))
            dx_ref[...] = dres_ref[...] + dx

    row = pl.BlockSpec((tr, D), lambda i: (i, 0))
    vec = pl.BlockSpec((1, D), lambda i: (0, 0))
    if want_dx:
        return pl.pallas_call(
            body, name=name, grid=(T // tr,),
            in_specs=[row, vec, row, row], out_specs=[row, vec],
            out_shape=[jax.ShapeDtypeStruct((T, D), F32), jax.ShapeDtypeStruct((1, D), F32)],
            compiler_params=_params('arbitrary'),
        )(x, g, dh, dres)
    dg = pl.pallas_call(
        body, name=name, grid=(T // tr,),
        in_specs=[row, vec, row], out_specs=vec,
        out_shape=jax.ShapeDtypeStruct((1, D), F32),
        compiler_params=_params('arbitrary'),
    )(x, g, dh)
    return None, dg


def _final_loss(x, g, tgt, name='final_loss'):
    T, D = x.shape
    tr = _tile(T, 512)

    def body(x_ref, g_ref, t_ref, loss_ref, dx_ref, dg_ref):
        i = pl.program_id(0)

        @pl.when(i == 0)
        def _():
            loss_ref[...] = jnp.zeros_like(loss_ref)
            dg_ref[...] = jnp.zeros_like(dg_ref)

        xv = x_ref[...]
        gv = g_ref[...]
        r = lax.rsqrt(jnp.mean(xv * xv, axis=-1, keepdims=True) + EPS)
        n = xv * r
        e = n * gv - t_ref[...]
        per_tok = jnp.mean(e * e, axis=-1, keepdims=True)
        loss_ref[...] += 0.5 * jnp.sum(per_tok, axis=0, keepdims=True)
        dy = e * (1.0 / D)
        dg_ref[...] += jnp.sum(dy * n, axis=0, keepdims=True)
        dn = dy * gv
        dx_ref[...] = r * (dn - n * jnp.mean(dn * n, axis=-1, keepdims=True))

    row = pl.BlockSpec((tr, D), lambda i: (i, 0))
    vec = pl.BlockSpec((1, D), lambda i: (0, 0))
    one = pl.BlockSpec((1, 1), lambda i: (0, 0))
    return pl.pallas_call(
        body, name=name, grid=(T // tr,),
        in_specs=[row, vec, row], out_specs=[one, row, vec],
        out_shape=[jax.ShapeDtypeStruct((1, 1), F32), jax.ShapeDtypeStruct((T, D), F32),
                   jax.ShapeDtypeStruct((1, D), F32)],
        compiler_params=_params('arbitrary'),
    )(x, g, tgt)


def _log_sigmoid(z):
    return jnp.minimum(z, 0.0) - jnp.log(1.0 + jnp.exp(-jnp.abs(z)))


def _tri(n, rel):
    j = lax.broadcasted_iota(jnp.int32, (n, n), 0)
    s = lax.broadcasted_iota(jnp.int32, (n, n), 1)
    return rel(j, s).astype(MXU_DTYPE)


def _split_dot(x, tri, terms):
    if MXU_DTYPE == F32:
        return jnp.dot(x, tri, preferred_element_type=F32), jnp.sum(x, axis=-1, keepdims=True)
    out = taken = None
    rem = x
    for _ in range(terms):
        piece = rem.astype(MXU_DTYPE)
        back = piece.astype(F32)
        part = jnp.dot(piece, tri, preferred_element_type=F32)
        rows = jnp.sum(back, axis=-1, keepdims=True)
        out, taken = (part, rows) if out is None else (out + part, taken + rows)
        rem = rem - back
    return out, taken


def _running_sums(x, tri, terms, earlier):
    w = tri.shape[0]
    parts = [_split_dot(x[:, at:at + w], tri, terms) for at in range(0, x.shape[1], w)]
    out = []
    for n, (r, _) in enumerate(parts):
        for _, whole in (parts[:n] if earlier else parts[n + 1:]):
            r = r + whole
        out.append(r)
    return jnp.concatenate(out, axis=1)


def _gate_fwd(zt, bcol, name='gate_fwd'):
    BH, S = zt.shape
    nb = S // LANES

    def body(z_ref, b_ref, c_ref):
        tri = _tri(LANES, lambda j, s: j <= s)
        carry = jnp.zeros((BH, 1), F32)
        for i in range(nb):
            sl = slice(i * LANES, (i + 1) * LANES)
            logf = _log_sigmoid(z_ref[:, sl] + b_ref[...])
            cs = _split_dot(logf, tri, 3)[0] + carry
            c_ref[:, sl] = cs
            carry = cs[:, LANES - 1:LANES]

    return pl.pallas_call(body, name=name, out_shape=jax.ShapeDtypeStruct((BH, S), F32))(zt, bcol)


def _gate_bwd(zt, bcol, dc, name='gate_bwd'):
    BH, S = zt.shape
    nb = S // LANES

    def body(z_ref, b_ref, dc_ref, dz_ref, db_ref):
        tri = _tri(LANES, lambda j, s: j >= s)
        carry = jnp.zeros((BH, 1), F32)
        dsum = jnp.zeros((BH, 1), F32)
        for i in reversed(range(nb)):
            sl = slice(i * LANES, (i + 1) * LANES)
            rs = _split_dot(dc_ref[:, sl], tri, 3)[0] + carry
            carry = rs[:, 0:1]
            z = z_ref[:, sl] + b_ref[...]
            dz = rs * (1.0 - 1.0 / (1.0 + jnp.exp(-z)))
            dz_ref[:, sl] = dz
            dsum = dsum + jnp.sum(dz, axis=-1, keepdims=True)
        db_ref[...] = dsum

    return pl.pallas_call(
        body, name=name,
        out_shape=[jax.ShapeDtypeStruct((BH, S), F32), jax.ShapeDtypeStruct((BH, 1), F32)],
    )(zt, bcol, dc)


FOX_ROWS, FOX_KEYS = 256, 512


PAIR = 2 * HEAD_DIM
N_MAIN_PAIRS = N_MAIN_HEADS // 2
N_MEM_PAIRS = N_MEM_HEADS // 2


def _lane0(shape):
    return lax.broadcasted_iota(jnp.int32, shape, len(shape) - 1) < HEAD_DIM


def _per_head(x):
    first = _lane0(x.shape)
    zero = jnp.zeros_like(x)
    return jnp.where(first, x, zero), jnp.where(first, zero, x)


def _pick(first, a, b):
    return jnp.where(first, a, b)


GROUP = 3
MAIN_STEPS = N_MAIN_PAIRS // GROUP


def _lanes(p):
    return slice(p * PAIR, (p + 1) * PAIR)


def _q_spec(bq, nq, off, group=1):
    assert off % group == 0
    return pl.BlockSpec((bq, group * PAIR), lambda b, j, i: (b * nq + i, off // group + j))


def _seq_spec(S, off, group=1):
    assert off % group == 0
    return pl.BlockSpec((S, group * PAIR), lambda b, j, i: (b, off // group + j))


def _gate_specs(bq, nk, bk):
    col = pl.BlockSpec((2 * GROUP, bq, 1), lambda b, j, i: (b * MAIN_STEPS + j, i, 0))
    rowv = pl.BlockSpec((2 * GROUP, nk, 1, bk), lambda b, j, i: (b * MAIN_STEPS + j, 0, 0, 0))
    return col, rowv


def _blocks(S, rows, keys):
    bq, bk = min(rows, S), min(keys, S)
    assert bk % bq == 0 and S % bk == 0
    return bq, bk


def _last_block(i, bq, bk, strict, step, carry):
    per = bk // bq
    last = i // per

    def mask(keys, shift):
        row = lax.broadcasted_iota(jnp.int32, (bq, keys), 0) + shift
        col = lax.broadcasted_iota(jnp.int32, (bq, keys), 1)
        return (col < row) if strict else (col <= row)

    if per == 1:
        return step(last, carry, mask(bk, 0), bk)
    assert per == 2
    return lax.cond(i % 2 == 0, lambda c: step(last, c, mask(bq, 0), bq), lambda c: step(last, c, mask(bk, bq), bk),
                    carry)


def _scaled(q):
    assert math.log2(SCALE).is_integer()
    return q * jnp.asarray(SCALE, q.dtype)


def _fox_fwd(qkv, offs, B, S, ccol, crow, beside=None, name='fox_fwd'):
    bq, bk = _blocks(S, FOX_ROWS, FOX_KEYS)
    nq = S // bq

    def body(q_ref, k_ref, v_ref, cc_ref, cr_ref, o_ref, lse_ref):
        i = pl.program_id(2)
        qv = _scaled(q_ref[...])
        qh = [_per_head(qv[:, _lanes(p)]) for p in range(GROUP)]
        first = _lane0((bq, PAIR))

        def step(kb, carry, mask=None, keys=bk):
            m, l, acc = carry
            sl = pl.ds(pl.multiple_of(kb * bk, bk), keys)
            m_new, l_new, acc_new = [], [], []
            for p in range(GROUP):
                ks, vs = k_ref[sl, _lanes(p)], v_ref[sl, _lanes(p)]
                alpha, pv = [], []
                for h in range(2):
                    n = 2 * p + h
                    s = _dot(qh[p][h], ks, NT) + cc_ref[n] - cr_ref[n, kb][:, :keys]
                    if mask is not None:
                        s = jnp.where(mask, s, NEG_BIG)
                    mh = jnp.maximum(m[n], jnp.max(s, axis=-1, keepdims=True))
                    pr = jnp.exp(s - mh)
                    ah = jnp.exp(m[n] - mh)
                    m_new.append(mh)
                    alpha.append(ah)
                    l_new.append(ah * l[n] + jnp.sum(pr, axis=-1, keepdims=True))
                    pv.append(_dot(pr, vs, NN))
                acc_new.append(_pick(first, alpha[0], alpha[1]) * acc[p] + _pick(first, pv[0], pv[1]))
            return tuple(m_new), tuple(l_new), tuple(acc_new)

        negs = tuple(jnp.full((bq, 1), NEG_BIG, F32) for _ in range(2 * GROUP))
        zeros = tuple(jnp.zeros((bq, 1), F32) for _ in range(2 * GROUP))
        acc0 = tuple(jnp.zeros((bq, PAIR), F32) for _ in range(GROUP))
        carry = lax.fori_loop(0, i // (bk // bq), step, (negs, zeros, acc0))
        m, l, acc = _last_block(i, bq, bk, False, step, carry)
        for p in range(GROUP):
            o_ref[:, _lanes(p)] = (acc[p] / _pick(first, l[2 * p], l[2 * p + 1])).astype(ACT_DTYPE)
        for n in range(2 * GROUP):
            lse_ref[n] = m[n] + jnp.log(l[n])

    col, rowv = _gate_specs(bq, S // bk, bk)
    return _call_beside(
        body, name, (B, MAIN_STEPS, nq),
        [_q_spec(bq, nq, offs[0], GROUP), _seq_spec(S, offs[1], GROUP), _seq_spec(S, offs[2], GROUP), col, rowv],
        [_q_spec(bq, nq, 0, GROUP), col],
        [jax.ShapeDtypeStruct((B * S, MAIN_WIDTH), ACT_DTYPE), jax.ShapeDtypeStruct((B * N_MAIN_HEADS, S, 1), F32)],
        [], (*qkv, ccol, crow), ('parallel', 'parallel', 'arbitrary'), beside)


def _fox_bwd(qkv, offs, B, S, ccol, crow, o, lse, do, beside=None, name='fox_bwd'):
    bq, bk = _blocks(S, FOX_ROWS, FOX_KEYS)
    nq = S // bq

    def body(q_ref, k_ref, v_ref, cc_ref, cr_ref, o_ref, lse_ref, do_ref,
             dq_ref, dk_ref, dv_ref, dcc_ref, dcr_ref, dk_acc, dv_acc):
        i = pl.program_id(2)

        @pl.when(i == 0)
        def _():
            dk_acc[...] = jnp.zeros_like(dk_acc)
            dv_acc[...] = jnp.zeros_like(dv_acc)
            dcr_ref[...] = jnp.zeros_like(dcr_ref)

        qv = _scaled(q_ref[...])
        dov = do_ref[...]
        qp = [qv[:, _lanes(p)] for p in range(GROUP)]
        dop = [dov[:, _lanes(p)] for p in range(GROUP)]
        qh = [_per_head(t) for t in qp]
        doh = [_per_head(t) for t in dop]
        first = _lane0((bq, PAIR))
        prod = dov.astype(F32) * o_ref[...].astype(F32)
        dsum = [jnp.sum(t, axis=-1, keepdims=True) for p in range(GROUP) for t in _per_head(prod[:, _lanes(p)])]

        def step(kb, carry, mask=None, keys=bk):
            dq, dcc = carry
            sl = pl.ds(pl.multiple_of(kb * bk, bk), keys)
            first_k = _lane0((keys, PAIR))
            dq_new, dcc_new = [], []
            for p in range(GROUP):
                ks, vs = k_ref[sl, _lanes(p)], v_ref[sl, _lanes(p)]
                dqh, dkh, dvh = [], [], []
                for h in range(2):
                    n = 2 * p + h
                    s = _dot(qh[p][h], ks, NT) + cc_ref[n] - cr_ref[n, kb][:, :keys]
                    pr = jnp.exp(s - lse_ref[n])
                    if mask is not None:
                        pr = jnp.where(mask, pr, 0.0)
                    ds = pr * (_dot(doh[p][h], vs, NT) - dsum[n])
                    dqh.append(_dot(ds, ks, NN))
                    dkh.append(_dot(ds, qp[p], TN))
                    dvh.append(_dot(pr, dop[p], TN))
                    as_key = jnp.sum(ds, axis=0, keepdims=True)
                    if keys < bk:
                        as_key = jnp.concatenate([as_key, jnp.zeros((1, bk - keys), F32)], axis=1)
                    dcr_ref[n, kb] -= as_key
                    dcc_new.append(dcc[n] + jnp.sum(ds, axis=-1, keepdims=True))
                dk_acc[sl, _lanes(p)] += _pick(first_k, dkh[0], dkh[1])
                dv_acc[sl, _lanes(p)] += _pick(first_k, dvh[0], dvh[1])
                dq_new.append(dq[p] + _pick(first, dqh[0], dqh[1]))
            return tuple(dq_new), tuple(dcc_new)

        zeros = tuple(jnp.zeros((bq, 1), F32) for _ in range(2 * GROUP))
        dq0 = tuple(jnp.zeros((bq, PAIR), F32) for _ in range(GROUP))
        dq, dcc = _last_block(i, bq, bk, False, step, lax.fori_loop(0, i // (bk // bq), step, (dq0, zeros)))
        for p in range(GROUP):
            dq_ref[:, _lanes(p)] = (dq[p] * SCALE).astype(ACT_DTYPE)
        for n in range(2 * GROUP):
            dcc_ref[n] = dcc[n]

        @pl.when(i == nq - 1)
        def _():
            dk_ref[...] = dk_acc[...].astype(ACT_DTYPE)
            dv_ref[...] = dv_acc[...].astype(ACT_DTYPE)

    col, rowv = _gate_specs(bq, S // bk, bk)
    qs, seq = _q_spec(bq, nq, 0, GROUP), _seq_spec(S, 0, GROUP)
    full = jax.ShapeDtypeStruct((B * S, MAIN_WIDTH), ACT_DTYPE)
    wide = pltpu.VMEM((S, GROUP * PAIR), F32)
    return _call_beside(
        body, name, (B, MAIN_STEPS, nq),
        [_q_spec(bq, nq, offs[0], GROUP), _seq_spec(S, offs[1], GROUP), _seq_spec(S, offs[2], GROUP), col, rowv,
         qs, col, qs],
        [qs, seq, seq, col, rowv],
        [full, full, full, jax.ShapeDtypeStruct(ccol.shape, F32), jax.ShapeDtypeStruct(crow.shape, F32)],
        [wide, wide],
        (*qkv, ccol, crow, o, lse, do), ('parallel', 'parallel', 'arbitrary'), beside)


SB_ROWS, SB_KEYS = 256, 512
SB_TRIANGLE = 256


SB_SUM_TERMS = 2


def _sb_block(q_scaled, ks, mask):
    z = _dot(q_scaled, ks, NT)
    a = _log_sigmoid(z)
    l = a - z
    return a, (l if mask is None else jnp.where(mask, l, 0.0))


def _sb_fwd(qkv, offs, B, S, name='sb_fwd'):
    bq, bk = _blocks(S, SB_ROWS, SB_KEYS)
    nq = S // bq

    def body(q_ref, k_ref, v_ref, o_ref, tot_ref):
        i = pl.program_id(2)
        qv = _scaled(q_ref[...])
        qh = [_per_head(qv[:, _lanes(p)]) for p in range(GROUP)]
        first = _lane0((bq, PAIR))
        tri = _tri(min(bk, SB_TRIANGLE), lambda j, s: j > s)

        def step(kb, carry, mask=None, keys=bk):
            acc, right = carry
            sl = pl.ds(pl.multiple_of(kb * bk, bk), keys)
            acc_new, right_new = [], []
            for p in range(GROUP):
                ks, vs = k_ref[sl, _lanes(p)], v_ref[sl, _lanes(p)]
                pv = []
                for h in range(2):
                    n = 2 * p + h
                    a, l = _sb_block(qh[p][h], ks, mask)
                    w = jnp.exp(a + _running_sums(l, tri, SB_SUM_TERMS, False) + right[n])
                    if mask is not None:
                        w = jnp.where(mask, w, 0.0)
                    pv.append(_dot(w, vs, NN))
                    right_new.append(right[n] + jnp.sum(l, axis=-1, keepdims=True))
                acc_new.append(acc[p] + _pick(first, pv[0], pv[1]))
            return tuple(acc_new), tuple(right_new)

        zeros = tuple(jnp.zeros((bq, 1), F32) for _ in range(2 * GROUP))
        acc0 = tuple(jnp.zeros((bq, PAIR), F32) for _ in range(GROUP))
        last = i // (bk // bq)
        carry = _last_block(i, bq, bk, True, step, (acc0, zeros))
        acc, total = lax.fori_loop(0, last, lambda n, c: step(last - 1 - n, c), carry)
        for p in range(GROUP):
            o_ref[:, _lanes(p)] = acc[p].astype(ACT_DTYPE)
        for n in range(2 * GROUP):
            tot_ref[n] = total[n]

    col, _ = _gate_specs(bq, S // bk, bk)
    return pl.pallas_call(
        body, name=name, grid=(B, MAIN_STEPS, nq),
        in_specs=[_q_spec(bq, nq, offs[0], GROUP), _seq_spec(S, offs[1], GROUP), _seq_spec(S, offs[2], GROUP)],
        out_specs=[_q_spec(bq, nq, 0, GROUP), col],
        out_shape=[jax.ShapeDtypeStruct((B * S, MAIN_WIDTH), ACT_DTYPE),
                   jax.ShapeDtypeStruct((B * N_MAIN_HEADS, S, 1), F32)],
        compiler_params=_params('parallel', 'parallel', 'arbitrary'),
    )(*qkv)


def _sb_bwd(qkv, offs, B, S, tot, do, name='sb_bwd'):
    bq, bk = _blocks(S, SB_ROWS, SB_KEYS)
    nq = S // bq

    def body(q_ref, k_ref, v_ref, tot_ref, do_ref, dq_ref, dk_ref, dv_ref, dk_acc, dv_acc):
        i = pl.program_id(2)

        @pl.when(i == 0)
        def _():
            dk_acc[...] = jnp.zeros_like(dk_acc)
            dv_acc[...] = jnp.zeros_like(dv_acc)

        qv = _scaled(q_ref[...])
        dov = do_ref[...]
        qp = [qv[:, _lanes(p)] for p in range(GROUP)]
        dop = [dov[:, _lanes(p)] for p in range(GROUP)]
        qh = [_per_head(t) for t in qp]
        doh = [_per_head(t) for t in dop]
        heads = [(p, h) for p in range(GROUP) for h in range(2)]
        first = _lane0((bq, PAIR))
        tri_incl = _tri(min(bk, SB_TRIANGLE), lambda j, s: j <= s)
        tri_excl = _tri(min(bk, SB_TRIANGLE), lambda j, s: j < s)
        zeros = tuple(jnp.zeros((bq, 1), F32) for _ in heads)
        tot = tuple(tot_ref[n] for n in range(len(heads)))

        def step(kb, carry, mask=None, keys=bk):
            dq, rest_l, left_g = carry
            sl = pl.ds(pl.multiple_of(kb * bk, bk), keys)
            first_k = _lane0((keys, PAIR))
            new_dq, new_l, new_g = [], [], []
            for p in range(GROUP):
                ks, vs = k_ref[sl, _lanes(p)], v_ref[sl, _lanes(p)]
                dqh, dkh, dvh = [], [], []
                for h in range(2):
                    n = 2 * p + h
                    a, l = _sb_block(qh[p][h], ks, mask)
                    w = jnp.exp(a - _running_sums(l, tri_incl, SB_SUM_TERMS, True) + rest_l[n])
                    if mask is not None:
                        w = jnp.where(mask, w, 0.0)
                    g = w * _dot(doh[p][h], vs, NT)
                    beta = jnp.exp(a)
                    dz = g - beta * (g + _running_sums(g, tri_excl, 1, True) + left_g[n])
                    if mask is not None:
                        dz = jnp.where(mask, dz, 0.0)
                    dqh.append(_dot(dz, ks, NN))
                    dkh.append(_dot(dz, qp[p], TN))
                    dvh.append(_dot(w, dop[p], TN))
                    new_l.append(rest_l[n] - jnp.sum(l, axis=-1, keepdims=True))
                    new_g.append(left_g[n] + jnp.sum(g, axis=-1, keepdims=True))
                dk_acc[sl, _lanes(p)] += _pick(first_k, dkh[0], dkh[1])
                dv_acc[sl, _lanes(p)] += _pick(first_k, dvh[0], dvh[1])
                new_dq.append(dq[p] + _pick(first, dqh[0], dqh[1]))
            return tuple(new_dq), tuple(new_l), tuple(new_g)

        dq0 = tuple(jnp.zeros((bq, PAIR), F32) for _ in range(GROUP))
        dq, _, _ = _last_block(i, bq, bk, True, step, lax.fori_loop(0, i // (bk // bq), step, (dq0, tot, zeros)))
        for p in range(GROUP):
            dq_ref[:, _lanes(p)] = (dq[p] * SCALE).astype(ACT_DTYPE)

        @pl.when(i == nq - 1)
        def _():
            dk_ref[...] = dk_acc[...].astype(ACT_DTYPE)
            dv_ref[...] = dv_acc[...].astype(ACT_DTYPE)

    qs, seq = _q_spec(bq, nq, 0, GROUP), _seq_spec(S, 0, GROUP)
    full = jax.ShapeDtypeStruct((B * S, MAIN_WIDTH), ACT_DTYPE)
    wide = pltpu.VMEM((S, GROUP * PAIR), F32)
    col, _ = _gate_specs(bq, S // bk, bk)
    return pl.pallas_call(
        body, name=name, grid=(B, MAIN_STEPS, nq),
        in_specs=[_q_spec(bq, nq, offs[0], GROUP), _seq_spec(S, offs[1], GROUP), _seq_spec(S, offs[2], GROUP), col,
                  qs],
        out_specs=[qs, seq, seq], out_shape=[full, full, full],
        scratch_shapes=[wide, wide],
        compiler_params=_params('parallel', 'parallel', 'arbitrary'),
    )(*qkv, tot, do)


def _mem_probs(qv, mk):
    s = _dot(qv, mk, NT) * SCALE
    p = jnp.exp(s - jnp.max(s, axis=-1, keepdims=True))
    return p / jnp.sum(p, axis=-1, keepdims=True)


def _mem_fwd(q, q_off, mkv, B, S, name='mem_fwd'):
    M = mkv.shape[0] // B
    bq = _tile(S, 512)
    nq = S // bq

    def body(q_ref, mk_ref, mv_ref, o_ref):
        first = _lane0((bq, PAIR))
        mk, mv = mk_ref[...], mv_ref[...]
        out = [_dot(_mem_probs(qh, mk), mv, NN) for qh in _per_head(q_ref[...])]
        o_ref[...] = _pick(first, out[0], out[1]).astype(ACT_DTYPE)

    return pl.pallas_call(
        body, name=name, grid=(B, N_MEM_PAIRS, nq),
        in_specs=[_q_spec(bq, nq, q_off), _seq_spec(M, 0), _seq_spec(M, N_MEM_PAIRS)],
        out_specs=_q_spec(bq, nq, 0),
        out_shape=jax.ShapeDtypeStruct((B * S, MEM_WIDTH), ACT_DTYPE),
        compiler_params=_params('parallel', 'parallel', 'parallel'),
    )(q, mkv, mkv)


def _mem_bwd(q, q_off, mkv, B, S, do, do_off, name='mem_bwd'):
    M = mkv.shape[0] // B
    bq = _tile(S, 512)
    nq = S // bq

    def body(q_ref, mk_ref, mv_ref, do_ref, dq_ref, dmk_ref, dmv_ref):
        i = pl.program_id(2)

        @pl.when(i == 0)
        def _():
            dmk_ref[...] = jnp.zeros_like(dmk_ref)
            dmv_ref[...] = jnp.zeros_like(dmv_ref)

        qv = q_ref[...]
        dov = do_ref[...]
        mk, mv = mk_ref[...], mv_ref[...]
        first = _lane0((bq, PAIR))
        first_m = _lane0((M, PAIR))
        dqh, dkh, dvh = [], [], []
        for qh, doh in zip(_per_head(qv), _per_head(dov)):
            p = _mem_probs(qh, mk)
            dp = _dot(doh, mv, NT)
            ds = p * (dp - jnp.sum(p * dp, axis=-1, keepdims=True))
            dqh.append(_dot(ds, mk, NN))
            dkh.append(_dot(ds, qv, TN))
            dvh.append(_dot(p, dov, TN))
        dq_ref[...] = (SCALE * _pick(first, dqh[0], dqh[1])).astype(ACT_DTYPE)
        dmk_ref[...] += SCALE * _pick(first_m, dkh[0], dkh[1])
        dmv_ref[...] += _pick(first_m, dvh[0], dvh[1])

    mem_out = jax.ShapeDtypeStruct((B * M, MEM_WIDTH), F32)
    return pl.pallas_call(
        body, name=name, grid=(B, N_MEM_PAIRS, nq),
        in_specs=[_q_spec(bq, nq, q_off), _seq_spec(M, 0), _seq_spec(M, N_MEM_PAIRS), _q_spec(bq, nq, do_off)],
        out_specs=[_q_spec(bq, nq, 0), _seq_spec(M, 0), _seq_spec(M, 0)],
        out_shape=[jax.ShapeDtypeStruct((B * S, MEM_WIDTH), ACT_DTYPE), mem_out, mem_out],
        compiler_params=_params('parallel', 'parallel', 'arbitrary'),
    )(q, mkv, mkv, do)


HALO = 8
CONV_CHUNK = 256


def _conv_chunk(scr, start, rows, w, b):
    at = HALO + start
    return (b + w[0:1, :] * scr[at - 2:at - 2 + rows, :] + w[1:2, :] * scr[at - 1:at - 1 + rows, :]
            + w[2:3, :] * scr[at:at + rows, :])


def _fill_frames(scr, ref):
    scr[0:HALO, :] = jnp.zeros((HALO, scr.shape[1]), F32)
    scr[HALO:, :] = ref[...].astype(F32)


def _sigmoid(x):
    return 0.5 + 0.5 * jnp.tanh(0.5 * x)


def _fold8(x):
    return jnp.sum(x.reshape(x.shape[0] // 8, 8, x.shape[1]), axis=0)


def _conv_specs(S, nf):
    ug = pl.BlockSpec((None, S, LANES), lambda b, j: (b, 0, j))
    uv = pl.BlockSpec((None, S, LANES), lambda b, j: (b, 0, j + nf))
    wg = pl.BlockSpec((3, LANES), lambda b, j: (0, j))
    wv = pl.BlockSpec((3, LANES), lambda b, j: (0, j + nf))
    bg = pl.BlockSpec((1, LANES), lambda b, j: (0, j))
    bv = pl.BlockSpec((1, LANES), lambda b, j: (0, j + nf))
    return ug, uv, wg, wv, bg, bv


def _conv_fwd(u, cw, cb, name='conv_fwd'):
    B, S, F2 = u.shape
    F = F2 // 2
    nf = F // LANES

    ch = min(CONV_CHUNK, S)

    def body(ug_ref, uv_ref, wg_ref, wv_ref, bg_ref, bv_ref, y_ref, g_scr, v_scr):
        _fill_frames(g_scr, ug_ref)
        _fill_frames(v_scr, uv_ref)
        wg, wv, bg, bv = wg_ref[...], wv_ref[...], bg_ref[...], bv_ref[...]
        for start in range(0, S, ch):
            gate = _conv_chunk(g_scr, start, ch, wg, bg)
            val = _conv_chunk(v_scr, start, ch, wv, bv)
            y_ref[start:start + ch, :] = (gate * _sigmoid(gate) * val).astype(ACT_DTYPE)

    specs = _conv_specs(S, nf)
    frames = pltpu.VMEM((HALO + S, LANES), F32)
    return pl.pallas_call(
        body, name=name, grid=(B, nf), in_specs=list(specs), out_specs=specs[0],
        out_shape=jax.ShapeDtypeStruct((B, S, F), ACT_DTYPE), scratch_shapes=[frames, frames],
        compiler_params=_params('parallel', 'parallel'),
    )(u, u, cw, cw, cb, cb)


def _conv_bwd(u, cw, cb, dy, name='conv_bwd'):
    B, S, F2 = u.shape
    F = F2 // 2
    nf = F // LANES

    ch = min(CONV_CHUNK, S)

    def body(ug_ref, uv_ref, wg_ref, wv_ref, bg_ref, bv_ref, dy_ref,
             dug_ref, duv_ref, dwg_ref, dwv_ref, dbg_ref, dbv_ref, g_scr, v_scr, dg_scr, dv_scr):
        b = pl.program_id(1)

        @pl.when(b == 0)
        def _():
            for r in (dwg_ref, dwv_ref, dbg_ref, dbv_ref):
                r[...] = jnp.zeros_like(r)

        _fill_frames(g_scr, ug_ref)
        _fill_frames(v_scr, uv_ref)
        wg, wv, bg, bv = wg_ref[...], wv_ref[...], bg_ref[...], bv_ref[...]
        for scr in (dg_scr, dv_scr):
            scr[S:, :] = jnp.zeros((HALO, LANES), F32)
        for start in range(0, S, ch):
            gate = _conv_chunk(g_scr, start, ch, wg, bg)
            val = _conv_chunk(v_scr, start, ch, wv, bv)
            dyv = dy_ref[start:start + ch, :].astype(F32)
            sg = _sigmoid(gate)
            dv_scr[start:start + ch, :] = dyv * (gate * sg)
            dg_scr[start:start + ch, :] = dyv * val * (sg * (1.0 + gate * (1.0 - sg)))

        for u_scr, d_scr, w, du_ref, dw_ref, db_ref in ((g_scr, dg_scr, wg, dug_ref, dwg_ref, dbg_ref),
                                                         (v_scr, dv_scr, wv, duv_ref, dwv_ref, dbv_ref)):
            sums = [jnp.zeros((8, LANES), F32) for _ in range(4)]
            for start in range(0, S, ch):
                x = u_scr[HALO + start:HALO + start + ch, :]
                d = [d_scr[start + n:start + n + ch, :] for n in range(3)]
                du_ref[start:start + ch, :] = (w[2:3, :] * d[0] + w[1:2, :] * d[1] + w[0:1, :] * d[2]).astype(ACT_DTYPE)
                sums = [sums[0] + _fold8(x * d[2]), sums[1] + _fold8(x * d[1]), sums[2] + _fold8(x * d[0]),
                        sums[3] + _fold8(d[0])]
            total = [jnp.sum(s, axis=0, keepdims=True) for s in sums]
            dw_ref[...] += jnp.concatenate(total[:3], axis=0)
            db_ref[...] += total[3]

    def swap(spec_fn):
        return lambda j, b: spec_fn(b, j)

    ug, uv, wg, wv, bg, bv = _conv_specs(S, nf)
    ins = [pl.BlockSpec(s.block_shape, swap(s.index_map)) for s in (ug, uv, wg, wv, bg, bv, ug)]
    outs = [ins[0], ins[0], ins[2], ins[2], ins[4], ins[4]]
    frames = pltpu.VMEM((HALO + S, LANES), F32)
    return pl.pallas_call(
        body, name=name, grid=(nf, B), in_specs=ins, out_specs=outs, scratch_shapes=[frames] * 4,
        out_shape=[jax.ShapeDtypeStruct((B, S, F), ACT_DTYPE), jax.ShapeDtypeStruct((B, S, F), ACT_DTYPE),
                   jax.ShapeDtypeStruct((3, F), F32), jax.ShapeDtypeStruct((3, F), F32),
                   jax.ShapeDtypeStruct((1, F), F32), jax.ShapeDtypeStruct((1, F), F32)],
        compiler_params=_params('parallel', 'arbitrary'),
    )(u, u, cw, cw, cb, cb, dy)


ADAM_BLOCK_BYTES = 1024 * 1024


def _adamw(w, g, m, v, layer, earlier, name):
    L, r, c = w.shape
    tr = r
    if r * c * 4 > ADAM_BLOCK_BYTES and r % 8 == 0:
        tr = 8
        for t in range(8, r + 1, 8):
            if r % t == 0 and t * c * 4 <= ADAM_BLOCK_BYTES:
                tr = t

    def body(w_ref, g_ref, m_ref, v_ref, *rest):
        go_ref, d_ref, nm_ref, nv_ref = rest[-4:]
        gv = g_ref[...]
        nm = ADAM_B1 * m_ref[...] + (1.0 - ADAM_B1) * gv
        nv = ADAM_B2 * v_ref[...] + (1.0 - ADAM_B2) * (gv * gv)
        m_hat = nm / (1.0 - ADAM_B1 ** ADAM_STEP)
        v_hat = nv / (1.0 - ADAM_B2 ** ADAM_STEP)
        d_ref[...] = -ADAM_LR * (m_hat / (jnp.sqrt(v_hat) + ADAM_EPS) + ADAM_WD * w_ref[...])
        nm_ref[...] = nm
        nv_ref[...] = nv
        go_ref[...] = gv

    lay = pl.BlockSpec((None, tr, c), lambda i: (layer, i, 0))
    one = pl.BlockSpec((tr, c), lambda i: (i, 0))
    shp = jax.ShapeDtypeStruct((L, r, c), F32)
    in_specs = [lay, one, lay, lay]
    args = (w, g, m, v)
    aliases = {}
    if earlier is not None:
        in_specs += [ANY] * 4
        args += tuple(earlier)
        aliases = {4 + k: k for k in range(4)}
    return pl.pallas_call(
        body, name=name, grid=(r // tr,), in_specs=in_specs, out_specs=[lay] * 4, out_shape=[shp] * 4,
        input_output_aliases=aliases, compiler_params=_params('parallel'),
    )(*args)


def _my_place():
    return lax.axis_index('x'), lax.axis_index('y'), lax.axis_index('c')


def _other_chips(x, y):
    return [(1 - x, y), (x, 1 - y), (1 - x, 1 - y)]


def _remote(src, dst, send_sem, recv_sem, to):
    return pltpu.make_async_remote_copy(src_ref=src, dst_ref=dst, send_sem=send_sem, recv_sem=recv_sem,
                                        device_id=to, device_id_type=MESH)


def _hbm_call(body, n_in, out_shapes, scratch, name, aliases=None):
    return pl.pallas_call(body, name=name, in_specs=[ANY] * n_in, out_specs=[ANY] * len(out_shapes),
                          out_shape=out_shapes, scratch_shapes=scratch, input_output_aliases=aliases or {})


def _full_shape(shard_shape, kind):
    L, r, c = shard_shape
    return {'rows': (L, N_CHIPS * r, c), 'cols': (L, r, N_CHIPS * c), 'stack': (N_CHIPS * L, r, c)}[kind]


def _place_block(w, kind, out_dtype, chip_arr, name):
    L, r, c = w.shape
    tr = r if r % 16 else _tile(r, max(16, SUM_BLOCK_BYTES // (4 * c)), 16)
    nrt = r // tr

    def body(k_ref, w_ref, o_ref):
        o_ref[...] = w_ref[...].astype(out_dtype)

    out_map = {'rows': lambda l, i, k_ref: (l, k_ref[0] * nrt + i, 0),
               'cols': lambda l, i, k_ref: (l, i, k_ref[0]),
               'stack': lambda l, i, k_ref: (k_ref[0] * L + l, i, 0)}[kind]
    gs = pltpu.PrefetchScalarGridSpec(
        num_scalar_prefetch=1, grid=(L, nrt),
        in_specs=[pl.BlockSpec((None, tr, c), lambda l, i, k_ref: (l, i, 0))],
        out_specs=pl.BlockSpec((None, tr, c), out_map))
    return pl.pallas_call(
        body, name=name, grid_spec=gs, out_shape=jax.ShapeDtypeStruct(_full_shape(w.shape, kind), out_dtype),
        compiler_params=_params('parallel', 'parallel'),
    )(chip_arr, w)


class _Exchange:
    def __init__(self, inputs, out_shapes, aliases, scratch, start, finish):
        self.inputs, self.out_shapes, self.aliases, self.scratch = list(inputs), list(out_shapes), aliases, scratch
        self.start, self.finish = start, finish


def _run_exchange(ex, name):
    n_in, n_out = len(ex.inputs), len(ex.out_shapes)

    def body(*refs):
        parts = refs[:n_in], refs[n_in:n_in + n_out], refs[n_in + n_out:]
        ex.start(*parts)
        ex.finish(*parts)

    return _hbm_call(body, n_in, ex.out_shapes, ex.scratch, name, aliases=ex.aliases)(*ex.inputs)


def _call_beside(body, name, grid, in_specs, out_specs, out_shape, scratch, args, semantics, beside):
    if beside is None:
        outs = pl.pallas_call(body, name=name, grid=grid, in_specs=in_specs, out_specs=out_specs, out_shape=out_shape,
                              scratch_shapes=scratch, compiler_params=_params(*semantics))(*args)
        return outs, None
    n_in, n_out, n_scr = len(in_specs), len(out_specs), len(scratch)
    b_in, b_out = len(beside.inputs), len(beside.out_shapes)

    def carrier(*refs):
        cuts = [n_in, b_in, n_out, b_out, n_scr]
        parts, at = [], 0
        for size in cuts:
            parts.append(refs[at:at + size])
            at += size
        ins, ex_ins, outs, ex_outs, scr = parts
        ex_scr = refs[at:]
        ids = [pl.program_id(d) for d in range(len(grid))]
        first = functools.reduce(jnp.logical_and, [i == 0 for i in ids])
        last = functools.reduce(jnp.logical_and, [i == g - 1 for i, g in zip(ids, grid)])

        @pl.when(first)
        def _():
            beside.start(ex_ins, ex_outs, ex_scr)

        body(*ins, *outs, *scr)

        @pl.when(last)
        def _():
            beside.finish(ex_ins, ex_outs, ex_scr)

    res = pl.pallas_call(
        carrier, name=name, grid=grid, in_specs=list(in_specs) + [ANY] * b_in,
        out_specs=list(out_specs) + [ANY] * b_out, out_shape=list(out_shape) + beside.out_shapes,
        scratch_shapes=list(scratch) + beside.scratch,
        input_output_aliases={n_in + i: n_out + o for i, o in beside.aliases.items()},
        compiler_params=_params(*['arbitrary'] * len(grid)),
    )(*args, *beside.inputs)
    return res[:n_out], res[n_out:]


def _gather_exchange(fulls, shard_shapes, kinds, split):
    n = len(fulls)

    def plan(outs, send_sems, recv_sems):
        x, y, c = _my_place()
        chip = 2 * x + y
        sibling = (x, y, 1 - c)
        others = _other_chips(x, y)

        def window(a, k, half):
            L, r, cols = shard_shapes[a]
            first, count = (0, r) if half is None else (half * (r // 2), r // 2)
            if kinds[a] == 'rows':
                return outs[a].at[:, pl.ds(k * r + first, count), :]
            if kinds[a] == 'cols':
                return outs[a].at[:, pl.ds(first, count), pl.ds(pl.multiple_of(k * cols, LANES), cols)]
            return outs[a].at[pl.ds(k * L, L), pl.ds(first, count), :]

        sends, arrivals, forwards, forwarded = [], [], [], []
        for a in range(n):
            half = c if split[a] else None
            for j, (ox, oy) in enumerate(others):
                sems = (send_sems.at[6 * a + j], recv_sems.at[6 * a + j], (ox, oy, c))
                sends.append(_remote(window(a, chip, half), window(a, chip, half), *sems))
                got = window(a, 2 * ox + oy, half)
                arrivals.append(_remote(got, got, *sems))
                if split[a]:
                    sems = (send_sems.at[6 * a + 3 + j], recv_sems.at[6 * a + 3 + j], sibling)
                    forwards.append(_remote(got, got, *sems))
                    theirs = window(a, 2 * ox + oy, 1 - c)
                    forwarded.append(_remote(theirs, theirs, *sems))
                else:
                    forwards.append(None)
        return sends, arrivals, forwards, forwarded

    def start(ins, outs, scratch):
        sends, _, _, _ = plan(outs, *scratch)
        for cp in sends:
            cp.start()

    def finish(ins, outs, scratch):
        sends, arrivals, forwards, forwarded = plan(outs, *scratch)
        for arrived, fw in zip(arrivals, forwards):
            arrived.wait_recv()
            if fw is not None:
                fw.start()
        for cp in forwarded:
            cp.wait_recv()
        for cp in sends + [fw for fw in forwards if fw is not None]:
            cp.wait_send()

    scratch = [pltpu.SemaphoreType.DMA((6 * n,)), pltpu.SemaphoreType.DMA((6 * n,))]
    out_shapes = [jax.ShapeDtypeStruct(f.shape, f.dtype) for f in fulls]
    return _Exchange(fulls, out_shapes, {a: a for a in range(n)}, scratch, start, finish)


def _swap_cores(gs, name='swap_cores'):
    n = len(gs)
    out_shapes = [jax.ShapeDtypeStruct((g.shape[0], g.shape[1] // 2, g.shape[2]), g.dtype) for g in gs]

    def body(*refs):
        ins, outs = refs[:n], refs[n:2 * n]
        send_sems, recv_sems = refs[2 * n:]
        x, y, c = _my_place()
        cps = []
        for a in range(n):
            rh = gs[a].shape[1] // 2
            cp = _remote(ins[a].at[:, pl.ds((1 - c) * rh, rh), :], outs[a], send_sems.at[a], recv_sems.at[a],
                         (x, y, 1 - c))
            cp.start()
            cps.append(cp)
        for cp in cps:
            cp.wait()

    scratch = [pltpu.SemaphoreType.DMA((n,)), pltpu.SemaphoreType.DMA((n,))]
    return _hbm_call(body, n, out_shapes, scratch, name)(*gs)


SUM_BLOCK_BYTES = 2 * 1024 * 1024


def _sum_rows(rh, cols):
    return _tile(rh, max(16, SUM_BLOCK_BYTES // (4 * cols)), 16)


def _add_cores(g, other, c_arr, wire_dtype, name):
    n, r, cols = g.shape
    rh = r // 2
    tr = _sum_rows(rh, cols)
    nrt = rh // tr

    def body(c_ref, g_ref, o_ref, q_ref):
        q_ref[...] = (g_ref[...] + o_ref[...]).astype(wire_dtype)

    gs = pltpu.PrefetchScalarGridSpec(
        num_scalar_prefetch=1, grid=(n, nrt),
        in_specs=[pl.BlockSpec((None, tr, cols), lambda j, i, c_ref: (j, c_ref[0] * nrt + i, 0)),
                  pl.BlockSpec((None, tr, cols), lambda j, i, c_ref: (j, i, 0))],
        out_specs=pl.BlockSpec((None, tr, cols), lambda j, i, c_ref: (j, i, 0)))
    return pl.pallas_call(
        body, name=name, grid_spec=gs, out_shape=jax.ShapeDtypeStruct((n, rh, cols), wire_dtype),
        compiler_params=_params('parallel', 'parallel'),
    )(c_arr, g, other)


def _send_exchange(qs):
    n = len(qs)

    def plan(ins, outs, send_sems, recv_sems):
        x, y, c = _my_place()
        return [_remote(ins[a].at[2 * ox + oy], outs[a].at[j], send_sems.at[3 * a + j], recv_sems.at[3 * a + j],
                        (ox, oy, c))
                for a in range(n) for j, (ox, oy) in enumerate(_other_chips(x, y))]

    def start(ins, outs, scratch):
        for cp in plan(ins, outs, *scratch):
            cp.start()

    def finish(ins, outs, scratch):
        cps = plan(ins, outs, *scratch)
        for cp in cps:
            cp.wait_recv()
        for cp in cps:
            cp.wait_send()

    scratch = [pltpu.SemaphoreType.DMA((3 * n,)), pltpu.SemaphoreType.DMA((3 * n,))]
    out_shapes = [jax.ShapeDtypeStruct((3,) + q.shape[1:], q.dtype) for q in qs]
    return _Exchange(qs, out_shapes, {}, scratch, start, finish)


def _sum_chips(q, got, place_arr, name):
    n, rh, cols = q.shape
    tr = _sum_rows(rh, cols)

    def body(p_ref, q_ref, gx_ref, gy_ref, gxy_ref, o_ref):
        f = lambda r: r[...].astype(F32)
        o_ref[...] = (f(q_ref) + f(gxy_ref)) + (f(gx_ref) + f(gy_ref))

    def got_spec(j):
        return pl.BlockSpec((None, tr, cols), lambda i, p_ref: (j, i, 0))

    gs = pltpu.PrefetchScalarGridSpec(
        num_scalar_prefetch=1, grid=(rh // tr,),
        in_specs=[pl.BlockSpec((None, tr, cols), lambda i, p_ref: (p_ref[0], i, 0)),
                  got_spec(0), got_spec(1), got_spec(2)],
        out_specs=pl.BlockSpec((None, tr, cols), lambda i, p_ref: (p_ref[1], i, 0)))
    return pl.pallas_call(
        body, name=name, grid_spec=gs, out_shape=jax.ShapeDtypeStruct((2, rh, cols), F32),
        compiler_params=_params('parallel'),
    )(place_arr, q, got, got, got)


def _join_cores(rs, name='join_cores'):
    n = len(rs)
    out_shapes = [jax.ShapeDtypeStruct(r.shape, r.dtype) for r in rs]

    def body(*refs):
        outs = refs[n:2 * n]
        send_sems, recv_sems = refs[2 * n:]
        x, y, c = _my_place()
        cps = []
        for a in range(n):
            cp = _remote(outs[a].at[c], outs[a].at[c], send_sems.at[a], recv_sems.at[a], (x, y, 1 - c))
            cp.start()
            cps.append(cp)
        for cp in cps:
            cp.wait()

    scratch = [pltpu.SemaphoreType.DMA((n,)), pltpu.SemaphoreType.DMA((n,))]
    return _hbm_call(body, n, out_shapes, scratch, name, aliases={a: a for a in range(n)})(*rs)


def _gate_rows(t, B, S):
    return t.reshape(B, S, N_MAIN_HEADS).transpose(0, 2, 1).reshape(B * N_MAIN_HEADS, S)


def _gate_cols(t, B, S):
    return t.reshape(B, N_MAIN_HEADS, S).transpose(0, 2, 1).reshape(B * S, N_MAIN_HEADS)


def _mem_kv_fwd(mem2, g, w, tag):
    hm = _rms_fwd(mem2, g, name=f'rms_mem_{tag}')
    mkv = _matmul(hm, w, 'nn', ACT_DTYPE, name=f'mm_memkv_{tag}')
    return hm, mkv


def _mem_kv_bwd(mem2, g, w, hm, dmk, dmv, tag):
    dmkv = jnp.concatenate([dmk, dmv], axis=1)
    dw = _matmul(hm, dmkv, 'tn', F32, name=f'mm_memkv_dw_{tag}')
    dhm = _matmul(dmkv, w, 'nt', F32, name=f'mm_memkv_dx_{tag}')
    _, dg = _rms_bwd(mem2, g, dhm, None, name=f'rms_mem_bwd_{tag}')
    return dw, dg


def _ffn_fwd(x, g, w_up, cw, cb, w_down, B, S, tag):
    T = x.shape[0]
    h2 = _rms_fwd(x, g, name=f'rms_ffn_{tag}')
    u = _matmul(h2, w_up, 'nn', ACT_DTYPE, name=f'mm_up_{tag}')
    y = _conv_fwd(u.reshape(B, S, -1), cw, cb, name=f'conv_fwd_{tag}').reshape(T, -1)
    x2 = _matmul(y, w_down, 'nn', F32, res=x, name=f'mm_down_{tag}')
    return x2, (h2, u, y)


def _ffn_bwd(dx2, x, g, w_up, cw, cb, w_down, saved, B, S, tag):
    h2, u, y = saved
    T = x.shape[0]
    dy = _matmul(dx2, w_down, 'nt', ACT_DTYPE, name=f'mm_down_dx_{tag}')
    dw_down = _matmul(y, dx2, 'tn', F32, name=f'mm_down_dw_{tag}')
    dug, duv, dcwg, dcwv, dcbg, dcbv = _conv_bwd(u.reshape(B, S, -1), cw, cb, dy.reshape(B, S, -1),
                                                  name=f'conv_bwd_{tag}')
    dug, duv = dug.reshape(T, -1), duv.reshape(T, -1)
    dh2 = _matmul((dug, duv), w_up, 'nt', F32, name=f'mm_up_dx_{tag}')
    half = N_CHIPS // 2
    dw_gate = _matmul(h2, dug, 'tn', F32, slots=half, total_slots=N_CHIPS, name=f'mm_up_dw_gate_{tag}')
    dw_up = _matmul(h2, duv, 'tn', F32, slots=half, slot_base=half, total_slots=N_CHIPS, into=dw_gate,
                    name=f'mm_up_dw_val_{tag}')
    dx, dg = _rms_bwd(x, g, dh2, dx2, name=f'rms_ffn_bwd_{tag}')
    dcw = jnp.concatenate([dcwg, dcwv], axis=1)
    dcb = jnp.concatenate([dcbg, dcbv], axis=1)
    return dx, dg, dw_up, dcw, dcb, dw_down


def _step(x, mem, tgt, W, late_weights=None, reduce_early=None):
    B, S, D = x.shape
    T = B * S
    x0 = x.reshape(T, D)
    mem2 = mem.reshape(-1, D)
    tgt2 = tgt.reshape(T, D)
    row = lambda v: v.reshape(1, -1)
    q3 = 3 * MAIN_WIDTH

    w_in_a = W['w_in_a'][0]
    wa_main = jnp.concatenate([w_in_a[:, :q3], w_in_a[:, q3 + N_MAIN_HEADS:]], axis=1)
    wa_gate = jnp.pad(w_in_a[:, q3:q3 + N_MAIN_HEADS], ((0, 0), (0, LANES - N_MAIN_HEADS)))
    bcol = jnp.tile(W['b_f_a'][0], B).reshape(B * N_MAIN_HEADS, 1)
    nkb = S // _blocks(S, FOX_ROWS, FOX_KEYS)[1]

    h1a = _rms_fwd(x0, row(W['ln_mix_g'][0]), name='rms_mix_a')
    pa = _matmul(h1a, wa_main, 'nn', ACT_DTYPE, name='mm_in_a')
    flog = _matmul(h1a, wa_gate, 'nn', F32, name='mm_gate_a')
    qkv_a = (pa, pa, pa)
    offs_a = (0, N_MAIN_PAIRS, 2 * N_MAIN_PAIRS)
    qm_off_a = 3 * N_MAIN_PAIRS
    zt = _gate_rows(flog[:, :N_MAIN_HEADS], B, S)
    cum = _gate_fwd(zt, bcol)
    ccol = cum.reshape(B * N_MAIN_HEADS, S, 1)
    crow = cum.reshape(B * N_MAIN_HEADS, nkb, 1, S // nkb)
    (oa, lse), late = _fox_fwd(qkv_a, offs_a, B, S, ccol, crow, beside=late_weights[0] if late_weights else None)
    if late_weights:
        W = {**W, **late_weights[1](late)}
    w_in_b = W['w_in_b'][0]
    hma, mkva = _mem_kv_fwd(mem2, row(W['ln_mem_g'][0]), W['w_memkv'][0], 'a')
    oma = _mem_fwd(pa, qm_off_a, mkva, B, S, name='mem_fwd_a')
    ocat_a = jnp.concatenate([oa, oma], axis=1)
    x1 = _matmul(ocat_a, W['w_out'][0], 'nn', F32, res=x0, name='mm_out_a')
    x2, ffn_a = _ffn_fwd(x1, row(W['ln_ffn_g'][0]), W['w_up'][0], W['conv_w'][0], row(W['conv_b'][0]),
                         W['w_down'][0], B, S, 'a')
    hkv = _rms_fwd(x2, row(W['ln_kv_g']), name='rms_kv')
    kvs = _matmul(hkv, W['w_kv'], 'nn', ACT_DTYPE, name='mm_kv')
    h1b = _rms_fwd(x2, row(W['ln_mix_g'][1]), name='rms_mix_b')
    pb = _matmul(h1b, w_in_b, 'nn', ACT_DTYPE, name='mm_in_b')
    qkv_b = (pb, kvs, kvs)
    offs_b = (0, 0, N_MAIN_PAIRS)
    qm_off_b = N_MAIN_PAIRS
    ob, tot_b = _sb_fwd(qkv_b, offs_b, B, S)
    hmb, mkvb = _mem_kv_fwd(mem2, row(W['ln_mem_g'][1]), W['w_memkv'][1], 'b')
    omb = _mem_fwd(pb, qm_off_b, mkvb, B, S, name='mem_fwd_b')
    ocat_b = jnp.concatenate([ob, omb], axis=1)
    x3 = _matmul(ocat_b, W['w_out'][1], 'nn', F32, res=x2, name='mm_out_b')
    x4, ffn_b = _ffn_fwd(x3, row(W['ln_ffn_g'][1]), W['w_up'][1], W['conv_w'][1], row(W['conv_b'][1]),
                         W['w_down'][1], B, S, 'b')
    loss, dx4, d_final_g = _final_loss(x4, row(W['final_g']), tgt2)

    dx3, dg_ffn_b, dw_up_b, dcw_b, dcb_b, dw_down_b = _ffn_bwd(
        dx4, x3, row(W['ln_ffn_g'][1]), W['w_up'][1], W['conv_w'][1], row(W['conv_b'][1]), W['w_down'][1],
        ffn_b, B, S, 'b')
    docat = _matmul(dx3, W['w_out'][1], 'nt', ACT_DTYPE, name='mm_out_dx_b')
    dw_out_b = _matmul(ocat_b, dx3, 'tn', F32, name='mm_out_dw_b')
    dqb, dkb, dvb = _sb_bwd(qkv_b, offs_b, B, S, tot_b, docat)
    dqmb, dmkb, dmvb = _mem_bwd(pb, qm_off_b, mkvb, B, S, docat, N_MAIN_PAIRS, name='mem_bwd_b')
    dw_memkv_b, dg_mem_b = _mem_kv_bwd(mem2, row(W['ln_mem_g'][1]), W['w_memkv'][1], hmb, dmkb, dmvb, 'b')
    dpb = jnp.concatenate([dqb, dqmb], axis=1)
    dh1b = _matmul(dpb, w_in_b, 'nt', F32, name='mm_in_dx_b')
    dw_in_b = _matmul(h1b, dpb, 'tn', F32, name='mm_in_dw_b')
    dx2, dg_mix_b = _rms_bwd(x2, row(W['ln_mix_g'][1]), dh1b, dx3, name='rms_mix_bwd_b')
    dkvs = jnp.concatenate([dkb, dvb], axis=1)
    dhkv = _matmul(dkvs, W['w_kv'], 'nt', F32, name='mm_kv_dx')
    dw_kv = _matmul(hkv, dkvs, 'tn', F32, slots=N_CHIPS, name='mm_kv_dw')
    dx2, dg_kv = _rms_bwd(x2, row(W['ln_kv_g']), dhkv, dx2, name='rms_kv_bwd')

    dx1, dg_ffn_a, dw_up_a, dcw_a, dcb_a, dw_down_a = _ffn_bwd(
        dx2, x1, row(W['ln_ffn_g'][0]), W['w_up'][0], W['conv_w'][0], row(W['conv_b'][0]), W['w_down'][0],
        ffn_a, B, S, 'a')
    docat = _matmul(dx1, W['w_out'][0], 'nt', ACT_DTYPE, name='mm_out_dx_a')
    dw_out_a = _matmul(ocat_a, dx1, 'tn', F32, name='mm_out_dw_a')

    def by_rows(dw):
        return dw.reshape(N_CHIPS, dw.shape[0] // N_CHIPS, dw.shape[1])

    grads = {
        'w_in_b': [by_rows(dw_in_b)],
        'w_kv': [dw_kv],
        'w_out': [by_rows(dw_out_a), by_rows(dw_out_b)],
        'w_up': [dw_up_a, dw_up_b],
        'w_down': [by_rows(dw_down_a), by_rows(dw_down_b)],
    }
    early = [(n, layer, g) for n, gs in grads.items() for layer, g in enumerate(gs)]
    early.append(('w_memkv', 1, by_rows(dw_memkv_b)))
    beside = reduce_early(early) if reduce_early else None
    (dqa, dka, dva, dccol, dcrow), crossed = _fox_bwd(qkv_a, offs_a, B, S, ccol, crow, oa, lse, docat, beside=beside)
    dzt, dbrow = _gate_bwd(zt, bcol, dccol.reshape(B * N_MAIN_HEADS, S) + dcrow.reshape(B * N_MAIN_HEADS, S))
    dqma, dmka, dmva = _mem_bwd(pa, qm_off_a, mkva, B, S, docat, N_MAIN_PAIRS, name='mem_bwd_a')
    dw_memkv_a, dg_mem_a = _mem_kv_bwd(mem2, row(W['ln_mem_g'][0]), W['w_memkv'][0], hma, dmka, dmva, 'a')
    dflog = jnp.pad(_gate_cols(dzt, B, S), ((0, 0), (0, LANES - N_MAIN_HEADS))).astype(ACT_DTYPE)
    dpa = jnp.concatenate([dqa, dka, dva, dqma, dflog], axis=1)
    wa_all = jnp.concatenate([wa_main, wa_gate], axis=1)
    dh1a = _matmul(dpa, wa_all, 'nt', F32, name='mm_in_dx_a')
    dwa = _matmul(h1a, dpa, 'tn', F32, name='mm_in_dw_a')
    dx0, dg_mix_a = _rms_bwd(x0, row(W['ln_mix_g'][0]), dh1a, dx1, name='rms_mix_bwd_a')

    n_main = wa_main.shape[1]
    dw_in_a = jnp.concatenate([dwa[:, :q3], dwa[:, n_main:n_main + N_MAIN_HEADS], dwa[:, q3:n_main]], axis=1)
    grads.update({
        'ln_mix_g': jnp.concatenate([dg_mix_a, dg_mix_b], axis=0),
        'w_in_a': dw_in_a[None],
        'b_f_a': dbrow.reshape(B, N_MAIN_HEADS).sum(axis=0)[None],
        'ln_kv_g': dg_kv[0],
        'ln_mem_g': jnp.concatenate([dg_mem_a, dg_mem_b], axis=0),
        'w_memkv': [by_rows(dw_memkv_a), early[-1][2]],
        'ln_ffn_g': jnp.concatenate([dg_ffn_a, dg_ffn_b], axis=0),
        'conv_w': jnp.stack([dcw_a, dcw_b]),
        'conv_b': jnp.concatenate([dcb_a, dcb_b], axis=0),
        'final_g': d_final_g[0],
    })
    return loss, dx0.reshape(B, S, D), grads, crossed


BLOCKED = ('w_in_b', 'w_kv', 'w_memkv', 'w_out', 'w_up', 'w_down')
MISC_ROWS = 32


def _misc_names():
    return [n for n in PARAM_NAMES if PARAM_SHARD_AXIS[n] is None] + ['conv_w']


def _reduce_begin(arrays, wire, tag):
    _, _, c = _my_place()
    c_arr = jnp.reshape(c, (1,)).astype(jnp.int32)
    others = _swap_cores(arrays, name=f'swap_cores_{tag}')
    return [_add_cores(g, o, c_arr, wire[i], name=f'add_cores_{tag}_{i}')
            for i, (g, o) in enumerate(zip(arrays, others))]


def _reduce_end(qs, crossed, tag):
    x, y, c = _my_place()
    place_arr = jnp.stack([2 * x + y, c]).astype(jnp.int32)
    sums = [_sum_chips(q, g, place_arr, name=f'sum_chips_{tag}_{i}') for i, (q, g) in enumerate(zip(qs, crossed))]
    return [j.reshape(-1, j.shape[-1]) for j in _join_cores(sums, name=f'join_cores_{tag}')]


def _pack_late(grads, shards):
    a_cols = shards['w_in_a'].shape[2]
    a_pad = -(-a_cols // LANES) * LANES
    dw_in_a = grads['w_in_a'][0]
    in_a = jnp.stack([jnp.pad(dw_in_a[:, k * a_cols:(k + 1) * a_cols], ((0, 0), (0, a_pad - a_cols)))
                      for k in range(N_CHIPS)])
    conv_cols = shards['conv_w'].shape[2]
    misc = []
    for k in range(N_CHIPS):
        parts = [grads[n].reshape(-1) for n in _misc_names()[:-1]]
        parts.append(grads['conv_w'][:, :, k * conv_cols:(k + 1) * conv_cols].reshape(-1))
        flat = jnp.concatenate(parts)
        assert flat.shape[0] <= MISC_ROWS * PACK_COLS
        misc.append(jnp.pad(flat, (0, MISC_ROWS * PACK_COLS - flat.shape[0])).reshape(MISC_ROWS, PACK_COLS))
    return in_a, jnp.stack(misc)


def _unpack_misc(rows, shards):
    flat = rows.reshape(-1)
    out, off = {}, 0
    for name in _misc_names():
        shape = shards[name].shape
        size = math.prod(shape)
        out[name] = flat[off:off + size].reshape(-1, shape[-1])
        off += size
    return out


def kernel(x, mem, ln_mix_g, w_in_a, b_f_a, w_in_b, ln_kv_g, w_kv, ln_mem_g, w_memkv, w_out, ln_ffn_g, w_up, conv_w, conv_b, w_down, final_g, loss_target, m_ln_mix_g, m_w_in_a, m_b_f_a, m_w_in_b, m_ln_kv_g, m_w_kv, m_ln_mem_g, m_w_memkv, m_w_out, m_ln_ffn_g, m_w_up, m_conv_w, m_conv_b, m_w_down, m_final_g, v_ln_mix_g, v_w_in_a, v_b_f_a, v_w_in_b, v_ln_kv_g, v_w_kv, v_ln_mem_g, v_w_memkv, v_w_out, v_ln_ffn_g, v_w_up, v_conv_w, v_conv_b, v_w_down, v_final_g):
    shards = dict(ln_mix_g=ln_mix_g, w_in_a=w_in_a, b_f_a=b_f_a, w_in_b=w_in_b, ln_kv_g=ln_kv_g, w_kv=w_kv,
                  ln_mem_g=ln_mem_g, w_memkv=w_memkv, w_out=w_out, ln_ffn_g=ln_ffn_g, w_up=w_up, conv_w=conv_w,
                  conv_b=conv_b, w_down=w_down, final_g=final_g)
    moments_m = dict(ln_mix_g=m_ln_mix_g, w_in_a=m_w_in_a, b_f_a=m_b_f_a, w_in_b=m_w_in_b, ln_kv_g=m_ln_kv_g,
                     w_kv=m_w_kv, ln_mem_g=m_ln_mem_g, w_memkv=m_w_memkv, w_out=m_w_out, ln_ffn_g=m_ln_ffn_g,
                     w_up=m_w_up, conv_w=m_conv_w, conv_b=m_conv_b, w_down=m_w_down, final_g=m_final_g)
    moments_v = dict(ln_mix_g=v_ln_mix_g, w_in_a=v_w_in_a, b_f_a=v_b_f_a, w_in_b=v_w_in_b, ln_kv_g=v_ln_kv_g,
                     w_kv=v_w_kv, ln_mem_g=v_ln_mem_g, w_memkv=v_w_memkv, w_out=v_w_out, ln_ffn_g=v_ln_ffn_g,
                     w_up=v_w_up, conv_w=v_conv_w, conv_b=v_conv_b, w_down=v_w_down, final_g=v_final_g)

    kinds = {'w_in_a': 'stack', 'w_in_b': 'rows', 'w_kv': 'cols', 'w_memkv': 'rows', 'w_out': 'rows', 'w_up': 'cols',
             'w_down': 'rows', 'conv_w': 'cols'}
    mx, my, _ = _my_place()
    chip_arr = jnp.reshape(2 * mx + my, (1,)).astype(jnp.int32)
    placed, shard_shapes = {}, {}
    for n, kind in kinds.items():
        w = shards[n].reshape((-1,) + shards[n].shape[-2:])
        shard_shapes[n] = w.shape
        placed[n] = _place_block(w, kind, F32 if n in F32_GATHERED else jnp.bfloat16, chip_arr, name=f'place_{n}')

    def gather(names):
        return _gather_exchange([placed[n] for n in names], [shard_shapes[n] for n in names],
                                [kinds[n] for n in names], [n not in F32_GATHERED for n in names])

    def as_weights(names, full):
        out = dict(zip(names, full))
        if 'w_in_a' in out:
            out['w_in_a'] = jnp.concatenate([out['w_in_a'][k] for k in range(N_CHIPS)], axis=1)[None]
        if 'w_kv' in out:
            out['w_kv'] = out['w_kv'][0]
        return out

    first = ['w_in_a', 'conv_w']
    late = [n for n in kinds if n not in first]
    W = {**shards, **as_weights(first, _run_exchange(gather(first), 'gather_first'))}

    early = {}

    def reduce_early(items):
        early['owners'] = [(n, layer) for n, layer, _ in items]
        early['qs'] = _reduce_begin([g for _, _, g in items], [jnp.bfloat16] * len(items), 'early')
        return _send_exchange(early['qs'])

    loss_part, grad_x, grads, crossed = _step(x, mem, loss_target, W, (gather(late), functools.partial(as_weights, late)),
                                              reduce_early)
    loss = lax.psum(loss_part[0, 0], ('x', 'y', 'c'))

    g_layers = {n: [None] * (len(grads[n]) if n in BLOCKED else 1) for n in PARAM_NAMES}
    for (n, layer), g in zip(early['owners'], _reduce_end(early['qs'], crossed, 'early')):
        g_layers[n][layer] = g
    in_a, misc = _pack_late(grads, shards)
    qs = _reduce_begin([in_a, misc, grads['w_memkv'][0]], [jnp.bfloat16, F32, jnp.bfloat16], 'late')
    in_a_sum, misc_sum, memkv_sum = _reduce_end(qs, _run_exchange(_send_exchange(qs), 'send_chips_late'), 'late')
    g_layers['w_in_a'][0] = in_a_sum[:, :shards['w_in_a'].shape[2]]
    g_layers['w_memkv'][0] = memkv_sum
    for n, g in _unpack_misc(misc_sum, shards).items():
        g_layers[n][0] = g

    results = {}
    for name in PARAM_NAMES:
        w = shards[name]
        layers = len(g_layers[name])
        as_layers = (layers, -1, w.shape[-1])
        w3, m3, v3 = (t.reshape(as_layers) for t in (w, moments_m[name], moments_v[name]))
        res = None
        for layer, g in enumerate(g_layers[name]):
            res = _adamw(w3, g, m3, v3, layer, res, name=f'adamw_{name}_{layer}')
        results[name] = [t.reshape(w.shape) for t in res]

    return (loss, grad_x, *[results[n][k] for k in range(4) for n in PARAM_NAMES])
```

```python
import functools
import math

import jax
import jax.numpy as jnp
from jax import lax
from jax.experimental import pallas as pl
from jax.experimental.pallas import tpu as pltpu

F32 = jnp.float32
MXU_DTYPE = jnp.bfloat16
ACT_DTYPE = jnp.bfloat16

HEAD_DIM = 64
N_MAIN_HEADS = 12
N_MEM_HEADS = 4
MAIN_WIDTH = N_MAIN_HEADS * HEAD_DIM
MEM_WIDTH = N_MEM_HEADS * HEAD_DIM
EPS = 1e-6
SCALE = HEAD_DIM ** -0.5
NEG_BIG = -1e30
LANES = 128
PACK_COLS = 1024
N_CHIPS = 4

ADAM_LR = 0.001
ADAM_B1 = 0.9
ADAM_B2 = 0.999
ADAM_EPS = 1e-08
ADAM_WD = 0.01
ADAM_STEP = 10

MESH = pl.DeviceIdType.MESH
ANY = pl.BlockSpec(memory_space=pl.ANY)

PARAM_SHARD_AXIS = {
    'ln_mix_g': None, 'w_in_a': 2, 'b_f_a': None, 'w_in_b': 1, 'ln_kv_g': None, 'w_kv': 1,
    'ln_mem_g': None, 'w_memkv': 1, 'w_out': 1, 'ln_ffn_g': None, 'w_up': 2, 'conv_w': 2,
    'conv_b': None, 'w_down': 1, 'final_g': None,
}
PARAM_NAMES = list(PARAM_SHARD_AXIS)
F32_GATHERED = ('conv_w',)


def _tile(n, pref, unit=LANES):
    if n <= pref:
        return n
    best = None
    for t in range(unit, pref + 1, unit):
        if n % t == 0:
            best = t
    assert best is not None, (n, pref)
    return best


MM_ACC_ELEMS = 768 * 1024
MM_VMEM_MB = 56
MM_TILE_BYTES = 42 << 20


def _out_tiles(M, N):
    def divisors(n, cap):
        if n <= LANES:
            return [n]
        return [t for t in range(LANES, min(n, cap) + 1, LANES) if n % t == 0]

    best = None
    for tm in divisors(M, 1536):
        for tn in divisors(N, 2048):
            if tm * tn <= MM_ACC_ELEMS and (best is None or (tm * tn, tn) > (best[0] * best[1], best[1])):
                best = (tm, tn)
    assert best is not None, (M, N)
    return best


def _params(*sem, vmem_mb=None):
    kw = {}
    if sem:
        kw['dimension_semantics'] = sem
    if vmem_mb is not None:
        kw['vmem_limit_bytes'] = vmem_mb * 1024 * 1024
    return pltpu.CompilerParams(**kw)


def _dot(a, b, dims):
    return lax.dot_general(a.astype(MXU_DTYPE), b.astype(MXU_DTYPE), (dims, ((), ())),
                           preferred_element_type=F32)


NN = ((1,), (0,))
NT = ((1,), (1,))
TN = ((0,), (0,))


def _matmul(a, b, mode, out_dtype, res=None, slots=1, slot_base=0, total_slots=None, into=None, name='mm'):
    pieces = a if isinstance(a, tuple) else (a,)
    a = pieces[0]
    k_sizes = [p.shape[0 if mode == 'tn' else 1] for p in pieces]
    if mode == 'nn':
        (M, _), (K2, N) = a.shape, b.shape
    elif mode == 'nt':
        (M, _), (N, K2) = a.shape, b.shape
    else:
        (_, M), (K2, N) = a.shape, b.shape
    K = sum(k_sizes)
    assert K == K2 and N % slots == 0, (a.shape, b.shape, mode, slots)
    slot_cols = N // slots
    tm, tn = _out_tiles(M, slot_cols)
    per_slot = slot_cols // tn
    fixed = tm * tn * (4 + 2 * jnp.dtype(out_dtype).itemsize + (8 if res is not None else 0))
    per_k = 2 * (tm * a.dtype.itemsize + tn * b.dtype.itemsize)
    tk = _tile(K, max(LANES, (MM_TILE_BYTES - fixed) // per_k))
    nk = K // tk
    dims = {'nn': NN, 'nt': NT, 'tn': TN}[mode]
    a_again = M * K * a.dtype.itemsize * (N // tn)
    b_again = b.size * b.dtype.itemsize * (M // tm)
    m_inner = nk == 1 and a_again < b_again
    n_a = len(pieces)
    assert n_a == 1 or (nk == 1 and mode != 'tn')

    def body(*refs):
        a_refs, b_ref = refs[:n_a], refs[n_a]
        a_ref = a_refs[0]
        r_ref = refs[n_a + 1] if res is not None else None
        o_ref = refs[n_a + 1 + (res is not None) + (into is not None)]

        def finish(out):
            if r_ref is not None:
                out = out + r_ref[...]
            o_ref[...] = out.astype(out_dtype)

        if nk == 1:
            out, at = None, 0
            for ref, size in zip(a_refs, k_sizes):
                part = b_ref[...] if n_a == 1 else (b_ref[:, at:at + size] if mode == 'nt' else b_ref[at:at + size, :])
                term = _dot(ref[...], part, dims)
                out, at = (term if out is None else out + term), at + size
            finish(out)
            return
        acc = refs[-1]
        k = pl.program_id(2)

        @pl.when(k == 0)
        def _():
            acc[...] = jnp.zeros_like(acc)

        acc[...] += _dot(a_ref[...], b_ref[...], dims)

        @pl.when(k == nk - 1)
        def _():
            finish(acc[...])

    def spec(shape, index):
        return pl.BlockSpec(shape, (lambda j, i, k: index(i, j, k)) if m_inner else index)

    if n_a > 1:
        a_specs = [spec((tm, size), lambda i, j, k: (i, 0)) for size in k_sizes]
    else:
        a_specs = [spec((tk, tm), lambda i, j, k: (k, i)) if mode == 'tn' else spec((tm, tk), lambda i, j, k: (i, k))]
    b_spec = spec((tn, tk), lambda i, j, k: (j, k)) if mode == 'nt' else spec((tk, tn), lambda i, j, k: (k, j))
    if slots == 1 and total_slots is None:
        o_spec = spec((tm, tn), lambda i, j, k: (i, j))
        out_shape = jax.ShapeDtypeStruct((M, N), out_dtype)
    else:
        assert res is None
        o_spec = spec((None, tm, tn), lambda i, j, k: (slot_base + j // per_slot, i, j % per_slot))
        out_shape = jax.ShapeDtypeStruct((total_slots or slots, M, slot_cols), out_dtype)
    in_specs = a_specs + [b_spec] + ([o_spec] if res is not None else []) + ([ANY] if into is not None else [])
    args = pieces + (b,) + ((res,) if res is not None else ()) + ((into,) if into is not None else ())
    return pl.pallas_call(
        body, name=name, grid=(N // tn, M // tm, nk) if m_inner else (M // tm, N // tn, nk),
        in_specs=in_specs, out_specs=o_spec,
        out_shape=out_shape,
        input_output_aliases={len(args) - 1: 0} if into is not None else {},
        scratch_shapes=[] if nk == 1 else [pltpu.VMEM((tm, tn), F32)],
        compiler_params=_params('parallel', 'parallel', 'arbitrary', vmem_mb=MM_VMEM_MB),
    )(*args)


def _rms_fwd(x, g, name):
    T, D = x.shape
    tr = _tile(T, 512)

    def body(x_ref, g_ref, o_ref):
        xv = x_ref[...]
        r = lax.rsqrt(jnp.mean(xv * xv, axis=-1, keepdims=True) + EPS)
        o_ref[...] = (xv * r * g_ref[...]).astype(ACT_DTYPE)

    return pl.pallas_call(
        body, name=name, grid=(T // tr,),
        in_specs=[pl.BlockSpec((tr, D), lambda i: (i, 0)), pl.BlockSpec((1, D), lambda i: (0, 0))],
        out_specs=pl.BlockSpec((tr, D), lambda i: (i, 0)),
        out_shape=jax.ShapeDtypeStruct((T, D), ACT_DTYPE),
        compiler_params=_params('parallel'),
    )(x, g)


def _rms_bwd(x, g, dh, dres, name):
    T, D = x.shape
    tr = _tile(T, 512)
    want_dx = dres is not None

    def body(*refs):
        if want_dx:
            x_ref, g_ref, dh_ref, dres_ref, dx_ref, dg_ref = refs
        else:
            x_ref, g_ref, dh_ref, dg_ref = refs
        i = pl.program_id(0)

        @pl.when(i == 0)
        def _():
            dg_ref[...] = jnp.zeros_like(dg_ref)

        xv = x_ref[...]
        dhv = dh_ref[...].astype(F32)
        r = lax.rsqrt(jnp.mean(xv * xv, axis=-1, keepdims=True) + EPS)
        n = xv * r
        dg_ref[...] += jnp.sum(dhv * n, axis=0, keepdims=True)
        if want_dx:
            dn = dhv * g_ref[...]
            dx = r * (dn - n * jnp.mean(dn * n, axis=-1, keepdims=True))
            dx_ref[...] = dres_ref[...] + dx

    row = pl.BlockSpec((tr, D), lambda i: (i, 0))
    vec = pl.BlockSpec((1, D), lambda i: (0, 0))
    if want_dx:
        return pl.pallas_call(
            body, name=name, grid=(T // tr,),
            in_specs=[row, vec, row, row], out_specs=[row, vec],
            out_shape=[jax.ShapeDtypeStruct((T, D), F32), jax.ShapeDtypeStruct((1, D), F32)],
            compiler_params=_params('arbitrary'),
        )(x, g, dh, dres)
    dg = pl.pallas_call(
        body, name=name, grid=(T // tr,),
        in_specs=[row, vec, row], out_specs=vec,
        out_shape=jax.ShapeDtypeStruct((1, D), F32),
        compiler_params=_params('arbitrary'),
    )(x, g, dh)
    return None, dg


def _final_loss(x, g, tgt, name='final_loss'):
    T, D = x.shape
    tr = _tile(T, 512)

    def body(x_ref, g_ref, t_ref, loss_ref, dx_ref, dg_ref):
        i = pl.program_id(0)

        @pl.when(i == 0)
        def _():
            loss_ref[...] = jnp.zeros_like(loss_ref)
            dg_ref[...] = jnp.zeros_like(dg_ref)

        xv = x_ref[...]
        gv = g_ref[...]
        r = lax.rsqrt(jnp.mean(xv * xv, axis=-1, keepdims=True) + EPS)
        n = xv * r
        e = n * gv - t_ref[...]
        per_tok = jnp.mean(e * e, axis=-1, keepdims=True)
        loss_ref[...] += 0.5 * jnp.sum(per_tok, axis=0, keepdims=True)
        dy = e * (1.0 / D)
        dg_ref[...] += jnp.sum(dy * n, axis=0, keepdims=True)
        dn = dy * gv
        dx_ref[...] = r * (dn - n * jnp.mean(dn * n, axis=-1, keepdims=True))

    row = pl.BlockSpec((tr, D), lambda i: (i, 0))
    vec = pl.BlockSpec((1, D), lambda i: (0, 0))
    one = pl.BlockSpec((1, 1), lambda i: (0, 0))
    return pl.pallas_call(
        body, name=name, grid=(T // tr,),
        in_specs=[row, vec, row], out_specs=[one, row, vec],
        out_shape=[jax.ShapeDtypeStruct((1, 1), F32), jax.ShapeDtypeStruct((T, D), F32),
                   jax.ShapeDtypeStruct((1, D), F32)],
        compiler_params=_params('arbitrary'),
    )(x, g, tgt)


def _log_sigmoid(z):
    return jnp.minimum(z, 0.0) - jnp.log(1.0 + jnp.exp(-jnp.abs(z)))


def _tri(n, rel):
    j = lax.broadcasted_iota(jnp.int32, (n, n), 0)
    s = lax.broadcasted_iota(jnp.int32, (n, n), 1)
    return rel(j, s).astype(MXU_DTYPE)


def _split_dot(x, tri, terms):
    if MXU_DTYPE == F32:
        return jnp.dot(x, tri, preferred_element_type=F32), jnp.sum(x, axis=-1, keepdims=True)
    out = taken = None
    rem = x
    for _ in range(terms):
        piece = rem.astype(MXU_DTYPE)
        back = piece.astype(F32)
        part = jnp.dot(piece, tri, preferred_element_type=F32)
        rows = jnp.sum(back, axis=-1, keepdims=True)
        out, taken = (part, rows) if out is None else (out + part, taken + rows)
        rem = rem - back
    return out, taken


def _running_sums(x, tri, terms, earlier):
    w = tri.shape[0]
    parts = [_split_dot(x[:, at:at + w], tri, terms) for at in range(0, x.shape[1], w)]
    out = []
    for n, (r, _) in enumerate(parts):
        for _, whole in (parts[:n] if earlier else parts[n + 1:]):
            r = r + whole
        out.append(r)
    return jnp.concatenate(out, axis=1)


def _gate_fwd(zt, bcol, name='gate_fwd'):
    BH, S = zt.shape
    nb = S // LANES

    def body(z_ref, b_ref, c_ref):
        tri = _tri(LANES, lambda j, s: j <= s)
        carry = jnp.zeros((BH, 1), F32)
        for i in range(nb):
            sl = slice(i * LANES, (i + 1) * LANES)
            logf = _log_sigmoid(z_ref[:, sl] + b_ref[...])
            cs = _split_dot(logf, tri, 3)[0] + carry
            c_ref[:, sl] = cs
            carry = cs[:, LANES - 1:LANES]

    return pl.pallas_call(body, name=name, out_shape=jax.ShapeDtypeStruct((BH, S), F32))(zt, bcol)


def _gate_bwd(zt, bcol, dc, name='gate_bwd'):
    BH, S = zt.shape
    nb = S // LANES

    def body(z_ref, b_ref, dc_ref, dz_ref, db_ref):
        tri = _tri(LANES, lambda j, s: j >= s)
        carry = jnp.zeros((BH, 1), F32)
        dsum = jnp.zeros((BH, 1), F32)
        for i in reversed(range(nb)):
            sl = slice(i * LANES, (i + 1) * LANES)
            rs = _split_dot(dc_ref[:, sl], tri, 3)[0] + carry
            carry = rs[:, 0:1]
            z = z_ref[:, sl] + b_ref[...]
            dz = rs * (1.0 - 1.0 / (1.0 + jnp.exp(-z)))
            dz_ref[:, sl] = dz
            dsum = dsum + jnp.sum(dz, axis=-1, keepdims=True)
        db_ref[...] = dsum

    return pl.pallas_call(
        body, name=name,
        out_shape=[jax.ShapeDtypeStruct((BH, S), F32), jax.ShapeDtypeStruct((BH, 1), F32)],
    )(zt, bcol, dc)


FOX_ROWS, FOX_KEYS = 256, 512


PAIR = 2 * HEAD_DIM
N_MAIN_PAIRS = N_MAIN_HEADS // 2
N_MEM_PAIRS = N_MEM_HEADS // 2


def _lane0(shape):
    return lax.broadcasted_iota(jnp.int32, shape, len(shape) - 1) < HEAD_DIM


def _per_head(x):
    first = _lane0(x.shape)
    zero = jnp.zeros_like(x)
    return jnp.where(first, x, zero), jnp.where(first, zero, x)


def _pick(first, a, b):
    return jnp.where(first, a, b)


GROUP = 3
MAIN_STEPS = N_MAIN_PAIRS // GROUP


def _lanes(p):
    return slice(p * PAIR, (p + 1) * PAIR)


def _q_spec(bq, nq, off, group=1):
    assert off % group == 0
    return pl.BlockSpec((bq, group * PAIR), lambda b, j, i: (b * nq + i, off // group + j))


def _seq_spec(S, off, group=1):
    assert off % group == 0
    return pl.BlockSpec((S, group * PAIR), lambda b, j, i: (b, off // group + j))


def _gate_specs(bq, nk, bk):
    col = pl.BlockSpec((2 * GROUP, bq, 1), lambda b, j, i: (b * MAIN_STEPS + j, i, 0))
    rowv = pl.BlockSpec((2 * GROUP, nk, 1, bk), lambda b, j, i: (b * MAIN_STEPS + j, 0, 0, 0))
    return col, rowv


def _blocks(S, rows, keys):
    bq, bk = min(rows, S), min(keys, S)
    assert bk % bq == 0 and S % bk == 0
    return bq, bk


def _last_block(i, bq, bk, strict, step, carry):
    per = bk // bq
    last = i // per

    def mask(keys, shift):
        row = lax.broadcasted_iota(jnp.int32, (bq, keys), 0) + shift
        col = lax.broadcasted_iota(jnp.int32, (bq, keys), 1)
        return (col < row) if strict else (col <= row)

    if per == 1:
        return step(last, carry, mask(bk, 0), bk)
    assert per == 2
    return lax.cond(i % 2 == 0, lambda c: step(last, c, mask(bq, 0), bq), lambda c: step(last, c, mask(bk, bq), bk),
                    carry)


def _scaled(q):
    assert math.log2(SCALE).is_integer()
    return q * jnp.asarray(SCALE, q.dtype)


def _fox_fwd(qkv, offs, B, S, ccol, crow, beside=None, name='fox_fwd'):
    bq, bk = _blocks(S, FOX_ROWS, FOX_KEYS)
    nq = S // bq

    def body(q_ref, k_ref, v_ref, cc_ref, cr_ref, o_ref, lse_ref):
        i = pl.program_id(2)
        qv = _scaled(q_ref[...])
        qh = [_per_head(qv[:, _lanes(p)]) for p in range(GROUP)]
        first = _lane0((bq, PAIR))

        def step(kb, carry, mask=None, keys=bk):
            m, l, acc = carry
            sl = pl.ds(pl.multiple_of(kb * bk, bk), keys)
            m_new, l_new, acc_new = [], [], []
            for p in range(GROUP):
                ks, vs = k_ref[sl, _lanes(p)], v_ref[sl, _lanes(p)]
                alpha, pv = [], []
                for h in range(2):
                    n = 2 * p + h
                    s = _dot(qh[p][h], ks, NT) + cc_ref[n] - cr_ref[n, kb][:, :keys]
                    if mask is not None:
                        s = jnp.where(mask, s, NEG_BIG)
                    mh = jnp.maximum(m[n], jnp.max(s, axis=-1, keepdims=True))
                    pr = jnp.exp(s - mh)
                    ah = jnp.exp(m[n] - mh)
                    m_new.append(mh)
                    alpha.append(ah)
                    l_new.append(ah * l[n] + jnp.sum(pr, axis=-1, keepdims=True))
                    pv.append(_dot(pr, vs, NN))
                acc_new.append(_pick(first, alpha[0], alpha[1]) * acc[p] + _pick(first, pv[0], pv[1]))
            return tuple(m_new), tuple(l_new), tuple(acc_new)

        negs = tuple(jnp.full((bq, 1), NEG_BIG, F32) for _ in range(2 * GROUP))
        zeros = tuple(jnp.zeros((bq, 1), F32) for _ in range(2 * GROUP))
        acc0 = tuple(jnp.zeros((bq, PAIR), F32) for _ in range(GROUP))
        carry = lax.fori_loop(0, i // (bk // bq), step, (negs, zeros, acc0))
        m, l, acc = _last_block(i, bq, bk, False, step, carry)
        for p in range(GROUP):
            o_ref[:, _lanes(p)] = (acc[p] / _pick(first, l[2 * p], l[2 * p + 1])).astype(ACT_DTYPE)
        for n in range(2 * GROUP):
            lse_ref[n] = m[n] + jnp.log(l[n])

    col, rowv = _gate_specs(bq, S // bk, bk)
    return _call_beside(
        body, name, (B, MAIN_STEPS, nq),
        [_q_spec(bq, nq, offs[0], GROUP), _seq_spec(S, offs[1], GROUP), _seq_spec(S, offs[2], GROUP), col, rowv],
        [_q_spec(bq, nq, 0, GROUP), col],
        [jax.ShapeDtypeStruct((B * S, MAIN_WIDTH), ACT_DTYPE), jax.ShapeDtypeStruct((B * N_MAIN_HEADS, S, 1), F32)],
        [], (*qkv, ccol, crow), ('parallel', 'parallel', 'arbitrary'), beside)


def _fox_bwd(qkv, offs, B, S, ccol, crow, o, lse, do, beside=None, name='fox_bwd'):
    bq, bk = _blocks(S, FOX_ROWS, FOX_KEYS)
    nq = S // bq

    def body(q_ref, k_ref, v_ref, cc_ref, cr_ref, o_ref, lse_ref, do_ref,
             dq_ref, dk_ref, dv_ref, dcc_ref, dcr_ref, dk_acc, dv_acc):
        i = pl.program_id(2)

        @pl.when(i == 0)
        def _():
            dk_acc[...] = jnp.zeros_like(dk_acc)
            dv_acc[...] = jnp.zeros_like(dv_acc)
            dcr_ref[...] = jnp.zeros_like(dcr_ref)

        qv = _scaled(q_ref[...])
        dov = do_ref[...]
        qp = [qv[:, _lanes(p)] for p in range(GROUP)]
        dop = [dov[:, _lanes(p)] for p in range(GROUP)]
        qh = [_per_head(t) for t in qp]
        doh = [_per_head(t) for t in dop]
        first = _lane0((bq, PAIR))
        prod = dov.astype(F32) * o_ref[...].astype(F32)
        dsum = [jnp.sum(t, axis=-1, keepdims=True) for p in range(GROUP) for t in _per_head(prod[:, _lanes(p)])]

        def step(kb, carry, mask=None, keys=bk):
            dq, dcc = carry
            sl = pl.ds(pl.multiple_of(kb * bk, bk), keys)
            first_k = _lane0((keys, PAIR))
            dq_new, dcc_new = [], []
            for p in range(GROUP):
                ks, vs = k_ref[sl, _lanes(p)], v_ref[sl, _lanes(p)]
                dqh, dkh, dvh = [], [], []
                for h in range(2):
                    n = 2 * p + h
                    s = _dot(qh[p][h], ks, NT) + cc_ref[n] - cr_ref[n, kb][:, :keys]
                    pr = jnp.exp(s - lse_ref[n])
                    if mask is not None:
                        pr = jnp.where(mask, pr, 0.0)
                    ds = pr * (_dot(doh[p][h], vs, NT) - dsum[n])
                    dqh.append(_dot(ds, ks, NN))
                    dkh.append(_dot(ds, qp[p], TN))
                    dvh.append(_dot(pr, dop[p], TN))
                    as_key = jnp.sum(ds, axis=0, keepdims=True)
                    if keys < bk:
                        as_key = jnp.concatenate([as_key, jnp.zeros((1, bk - keys), F32)], axis=1)
                    dcr_ref[n, kb] -= as_key
                    dcc_new.append(dcc[n] + jnp.sum(ds, axis=-1, keepdims=True))
                dk_acc[sl, _lanes(p)] += _pick(first_k, dkh[0], dkh[1])
                dv_acc[sl, _lanes(p)] += _pick(first_k, dvh[0], dvh[1])
                dq_new.append(dq[p] + _pick(first, dqh[0], dqh[1]))
            return tuple(dq_new), tuple(dcc_new)

        zeros = tuple(jnp.zeros((bq, 1), F32) for _ in range(2 * GROUP))
        dq0 = tuple(jnp.zeros((bq, PAIR), F32) for _ in range(GROUP))
        dq, dcc = _last_block(i, bq, bk, False, step, lax.fori_loop(0, i // (bk // bq), step, (dq0, zeros)))
        for p in range(GROUP):
            dq_ref[:, _lanes(p)] = (dq[p] * SCALE).astype(ACT_DTYPE)
        for n in range(2 * GROUP):
            dcc_ref[n] = dcc[n]

        @pl.when(i == nq - 1)
        def _():
            dk_ref[...] = dk_acc[...].astype(ACT_DTYPE)
            dv_ref[...] = dv_acc[...].astype(ACT_DTYPE)

    col, rowv = _gate_specs(bq, S // bk, bk)
    qs, seq = _q_spec(bq, nq, 0, GROUP), _seq_spec(S, 0, GROUP)
    full = jax.ShapeDtypeStruct((B * S, MAIN_WIDTH), ACT_DTYPE)
    wide = pltpu.VMEM((S, GROUP * PAIR), F32)
    return _call_beside(
        body, name, (B, MAIN_STEPS, nq),
        [_q_spec(bq, nq, offs[0], GROUP), _seq_spec(S, offs[1], GROUP), _seq_spec(S, offs[2], GROUP), col, rowv,
         qs, col, qs],
        [qs, seq, seq, col, rowv],
        [full, full, full, jax.ShapeDtypeStruct(ccol.shape, F32), jax.ShapeDtypeStruct(crow.shape, F32)],
        [wide, wide],
        (*qkv, ccol, crow, o, lse, do), ('parallel', 'parallel', 'arbitrary'), beside)


SB_ROWS, SB_KEYS = 256, 512
SB_TRIANGLE = 256


SB_SUM_TERMS = 2


def _sb_block(q_scaled, ks, mask):
    z = _dot(q_scaled, ks, NT)
    a = _log_sigmoid(z)
    l = a - z
    return a, (l if mask is None else jnp.where(mask, l, 0.0))


def _sb_fwd(qkv, offs, B, S, name='sb_fwd'):
    bq, bk = _blocks(S, SB_ROWS, SB_KEYS)
    nq = S // bq

    def body(q_ref, k_ref, v_ref, o_ref, tot_ref):
        i = pl.program_id(2)
        qv = _scaled(q_ref[...])
        qh = [_per_head(qv[:, _lanes(p)]) for p in range(GROUP)]
        first = _lane0((bq, PAIR))
        tri = _tri(min(bk, SB_TRIANGLE), lambda j, s: j > s)

        def step(kb, carry, mask=None, keys=bk):
            acc, right = carry
            sl = pl.ds(pl.multiple_of(kb * bk, bk), keys)
            acc_new, right_new = [], []
            for p in range(GROUP):
                ks, vs = k_ref[sl, _lanes(p)], v_ref[sl, _lanes(p)]
                pv = []
                for h in range(2):
                    n = 2 * p + h
                    a, l = _sb_block(qh[p][h], ks, mask)
                    w = jnp.exp(a + _running_sums(l, tri, SB_SUM_TERMS, False) + right[n])
                    if mask is not None:
                        w = jnp.where(mask, w, 0.0)
                    pv.append(_dot(w, vs, NN))
                    right_new.append(right[n] + jnp.sum(l, axis=-1, keepdims=True))
                acc_new.append(acc[p] + _pick(first, pv[0], pv[1]))
            return tuple(acc_new), tuple(right_new)

        zeros = tuple(jnp.zeros((bq, 1), F32) for _ in range(2 * GROUP))
        acc0 = tuple(jnp.zeros((bq, PAIR), F32) for _ in range(GROUP))
        last = i // (bk // bq)
        carry = _last_block(i, bq, bk, True, step, (acc0, zeros))
        acc, total = lax.fori_loop(0, last, lambda n, c: step(last - 1 - n, c), carry)
        for p in range(GROUP):
            o_ref[:, _lanes(p)] = acc[p].astype(ACT_DTYPE)
        for n in range(2 * GROUP):
            tot_ref[n] = total[n]

    col, _ = _gate_specs(bq, S // bk, bk)
    return pl.pallas_call(
        body, name=name, grid=(B, MAIN_STEPS, nq),
        in_specs=[_q_spec(bq, nq, offs[0], GROUP), _seq_spec(S, offs[1], GROUP), _seq_spec(S, offs[2], GROUP)],
        out_specs=[_q_spec(bq, nq, 0, GROUP), col],
        out_shape=[jax.ShapeDtypeStruct((B * S, MAIN_WIDTH), ACT_DTYPE),
                   jax.ShapeDtypeStruct((B * N_MAIN_HEADS, S, 1), F32)],
        compiler_params=_params('parallel', 'parallel', 'arbitrary'),
    )(*qkv)


def _sb_bwd(qkv, offs, B, S, tot, do, name='sb_bwd'):
    bq, bk = _blocks(S, SB_ROWS, SB_KEYS)
    nq = S // bq

    def body(q_ref, k_ref, v_ref, tot_ref, do_ref, dq_ref, dk_ref, dv_ref, dk_acc, dv_acc):
        i = pl.program_id(2)

        @pl.when(i == 0)
        def _():
            dk_acc[...] = jnp.zeros_like(dk_acc)
            dv_acc[...] = jnp.zeros_like(dv_acc)

        qv = _scaled(q_ref[...])
        dov = do_ref[...]
        qp = [qv[:, _lanes(p)] for p in range(GROUP)]
        dop = [dov[:, _lanes(p)] for p in range(GROUP)]
        qh = [_per_head(t) for t in qp]
        doh = [_per_head(t) for t in dop]
        heads = [(p, h) for p in range(GROUP) for h in range(2)]
        first = _lane0((bq, PAIR))
        tri_incl = _tri(min(bk, SB_TRIANGLE), lambda j, s: j <= s)
        tri_excl = _tri(min(bk, SB_TRIANGLE), lambda j, s: j < s)
        zeros = tuple(jnp.zeros((bq, 1), F32) for _ in heads)
        tot = tuple(tot_ref[n] for n in range(len(heads)))

        def step(kb, carry, mask=None, keys=bk):
            dq, rest_l, left_g = carry
            sl = pl.ds(pl.multiple_of(kb * bk, bk), keys)
            first_k = _lane0((keys, PAIR))
            new_dq, new_l, new_g = [], [], []
            for p in range(GROUP):
                ks, vs = k_ref[sl, _lanes(p)], v_ref[sl, _lanes(p)]
                dqh, dkh, dvh = [], [], []
                for h in range(2):
                    n = 2 * p + h
                    a, l = _sb_block(qh[p][h], ks, mask)
                    w = jnp.exp(a - _running_sums(l, tri_incl, SB_SUM_TERMS, True) + rest_l[n])
                    if mask is not None:
                        w = jnp.where(mask, w, 0.0)
                    g = w * _dot(doh[p][h], vs, NT)
                    beta = jnp.exp(a)
                    dz = g - beta * (g + _running_sums(g, tri_excl, 1, True) + left_g[n])
                    if mask is not None:
                        dz = jnp.where(mask, dz, 0.0)
                    dqh.append(_dot(dz, ks, NN))
                    dkh.append(_dot(dz, qp[p], TN))
                    dvh.append(_dot(w, dop[p], TN))
                    new_l.append(rest_l[n] - jnp.sum(l, axis=-1, keepdims=True))
                    new_g.append(left_g[n] + jnp.sum(g, axis=-1, keepdims=True))
                dk_acc[sl, _lanes(p)] += _pick(first_k, dkh[0], dkh[1])
                dv_acc[sl, _lanes(p)] += _pick(first_k, dvh[0], dvh[1])
                new_dq.append(dq[p] + _pick(first, dqh[0], dqh[1]))
            return tuple(new_dq), tuple(new_l), tuple(new_g)

        dq0 = tuple(jnp.zeros((bq, PAIR), F32) for _ in range(GROUP))
        dq, _, _ = _last_block(i, bq, bk, True, step, lax.fori_loop(0, i // (bk // bq), step, (dq0, tot, zeros)))
        for p in range(GROUP):
            dq_ref[:, _lanes(p)] = (dq[p] * SCALE).astype(ACT_DTYPE)

        @pl.when(i == nq - 1)
        def _():
            dk_ref[...] = dk_acc[...].astype(ACT_DTYPE)
            dv_ref[...] = dv_acc[...].astype(ACT_DTYPE)

    qs, seq = _q_spec(bq, nq, 0, GROUP), _seq_spec(S, 0, GROUP)
    full = jax.ShapeDtypeStruct((B * S, MAIN_WIDTH), ACT_DTYPE)
    wide = pltpu.VMEM((S, GROUP * PAIR), F32)
    col, _ = _gate_specs(bq, S // bk, bk)
    return pl.pallas_call(
        body, name=name, grid=(B, MAIN_STEPS, nq),
        in_specs=[_q_spec(bq, nq, offs[0], GROUP), _seq_spec(S, offs[1], GROUP), _seq_spec(S, offs[2], GROUP), col,
                  qs],
        out_specs=[qs, seq, seq], out_shape=[full, full, full],
        scratch_shapes=[wide, wide],
        compiler_params=_params('parallel', 'parallel', 'arbitrary'),
    )(*qkv, tot, do)


def _mem_probs(qv, mk):
    s = _dot(qv, mk, NT) * SCALE
    p = jnp.exp(s - jnp.max(s, axis=-1, keepdims=True))
    return p / jnp.sum(p, axis=-1, keepdims=True)


def _mem_fwd(q, q_off, mkv, B, S, name='mem_fwd'):
    M = mkv.shape[0] // B
    bq = _tile(S, 512)
    nq = S // bq

    def body(q_ref, mk_ref, mv_ref, o_ref):
        first = _lane0((bq, PAIR))
        mk, mv = mk_ref[...], mv_ref[...]
        out = [_dot(_mem_probs(qh, mk), mv, NN) for qh in _per_head(q_ref[...])]
        o_ref[...] = _pick(first, out[0], out[1]).astype(ACT_DTYPE)

    return pl.pallas_call(
        body, name=name, grid=(B, N_MEM_PAIRS, nq),
        in_specs=[_q_spec(bq, nq, q_off), _seq_spec(M, 0), _seq_spec(M, N_MEM_PAIRS)],
        out_specs=_q_spec(bq, nq, 0),
        out_shape=jax.ShapeDtypeStruct((B * S, MEM_WIDTH), ACT_DTYPE),
        compiler_params=_params('parallel', 'parallel', 'parallel'),
    )(q, mkv, mkv)


def _mem_bwd(q, q_off, mkv, B, S, do, do_off, name='mem_bwd'):
    M = mkv.shape[0] // B
    bq = _tile(S, 512)
    nq = S // bq

    def body(q_ref, mk_ref, mv_ref, do_ref, dq_ref, dmk_ref, dmv_ref):
        i = pl.program_id(2)

        @pl.when(i == 0)
        def _():
            dmk_ref[...] = jnp.zeros_like(dmk_ref)
            dmv_ref[...] = jnp.zeros_like(dmv_ref)

        qv = q_ref[...]
        dov = do_ref[...]
        mk, mv = mk_ref[...], mv_ref[...]
        first = _lane0((bq, PAIR))
        first_m = _lane0((M, PAIR))
        dqh, dkh, dvh = [], [], []
        for qh, doh in zip(_per_head(qv), _per_head(dov)):
            p = _mem_probs(qh, mk)
            dp = _dot(doh, mv, NT)
            ds = p * (dp - jnp.sum(p * dp, axis=-1, keepdims=True))
            dqh.append(_dot(ds, mk, NN))
            dkh.append(_dot(ds, qv, TN))
            dvh.append(_dot(p, dov, TN))
        dq_ref[...] = (SCALE * _pick(first, dqh[0], dqh[1])).astype(ACT_DTYPE)
        dmk_ref[...] += SCALE * _pick(first_m, dkh[0], dkh[1])
        dmv_ref[...] += _pick(first_m, dvh[0], dvh[1])

    mem_out = jax.ShapeDtypeStruct((B * M, MEM_WIDTH), F32)
    return pl.pallas_call(
        body, name=name, grid=(B, N_MEM_PAIRS, nq),
        in_specs=[_q_spec(bq, nq, q_off), _seq_spec(M, 0), _seq_spec(M, N_MEM_PAIRS), _q_spec(bq, nq, do_off)],
        out_specs=[_q_spec(bq, nq, 0), _seq_spec(M, 0), _seq_spec(M, 0)],
        out_shape=[jax.ShapeDtypeStruct((B * S, MEM_WIDTH), ACT_DTYPE), mem_out, mem_out],
        compiler_params=_params('parallel', 'parallel', 'arbitrary'),
    )(q, mkv, mkv, do)


HALO = 8
CONV_CHUNK = 64


def _conv_chunk(scr, start, rows, w, b):
    at = HALO + start
    return (b + w[0:1, :] * scr[at - 2:at - 2 + rows, :] + w[1:2, :] * scr[at - 1:at - 1 + rows, :]
            + w[2:3, :] * scr[at:at + rows, :])


def _fill_frames(scr, ref):
    scr[0:HALO, :] = jnp.zeros((HALO, scr.shape[1]), F32)
    scr[HALO:, :] = ref[...].astype(F32)


def _sigmoid(x):
    return 0.5 + 0.5 * jnp.tanh(0.5 * x)


def _fold8(x):
    return jnp.sum(x.reshape(x.shape[0] // 8, 8, x.shape[1]), axis=0)


def _conv_specs(S, nf):
    ug = pl.BlockSpec((None, S, LANES), lambda b, j: (b, 0, j))
    uv = pl.BlockSpec((None, S, LANES), lambda b, j: (b, 0, j + nf))
    wg = pl.BlockSpec((3, LANES), lambda b, j: (0, j))
    wv = pl.BlockSpec((3, LANES), lambda b, j: (0, j + nf))
    bg = pl.BlockSpec((1, LANES), lambda b, j: (0, j))
    bv = pl.BlockSpec((1, LANES), lambda b, j: (0, j + nf))
    return ug, uv, wg, wv, bg, bv


def _conv_fwd(u, cw, cb, name='conv_fwd'):
    B, S, F2 = u.shape
    F = F2 // 2
    nf = F // LANES

    ch = min(CONV_CHUNK, S)

    def body(ug_ref, uv_ref, wg_ref, wv_ref, bg_ref, bv_ref, y_ref, g_scr, v_scr):
        _fill_frames(g_scr, ug_ref)
        _fill_frames(v_scr, uv_ref)
        wg, wv, bg, bv = wg_ref[...], wv_ref[...], bg_ref[...], bv_ref[...]
        for start in range(0, S, ch):
            gate = _conv_chunk(g_scr, start, ch, wg, bg)
            val = _conv_chunk(v_scr, start, ch, wv, bv)
            y_ref[start:start + ch, :] = (gate * _sigmoid(gate) * val).astype(ACT_DTYPE)

    specs = _conv_specs(S, nf)
    frames = pltpu.VMEM((HALO + S, LANES), F32)
    return pl.pallas_call(
        body, name=name, grid=(B, nf), in_specs=list(specs), out_specs=specs[0],
        out_shape=jax.ShapeDtypeStruct((B, S, F), ACT_DTYPE), scratch_shapes=[frames, frames],
        compiler_params=_params('parallel', 'parallel'),
    )(u, u, cw, cw, cb, cb)


def _conv_bwd(u, cw, cb, dy, name='conv_bwd'):
    B, S, F2 = u.shape
    F = F2 // 2
    nf = F // LANES

    ch = min(CONV_CHUNK, S)

    def body(ug_ref, uv_ref, wg_ref, wv_ref, bg_ref, bv_ref, dy_ref,
             dug_ref, duv_ref, dwg_ref, dwv_ref, dbg_ref, dbv_ref, g_scr, v_scr, dg_scr, dv_scr):
        b = pl.program_id(1)

        @pl.when(b == 0)
        def _():
            for r in (dwg_ref, dwv_ref, dbg_ref, dbv_ref):
                r[...] = jnp.zeros_like(r)

        _fill_frames(g_scr, ug_ref)
        _fill_frames(v_scr, uv_ref)
        wg, wv, bg, bv = wg_ref[...], wv_ref[...], bg_ref[...], bv_ref[...]
        for scr in (dg_scr, dv_scr):
            scr[S:, :] = jnp.zeros((HALO, LANES), F32)
        for start in range(0, S, ch):
            gate = _conv_chunk(g_scr, start, ch, wg, bg)
            val = _conv_chunk(v_scr, start, ch, wv, bv)
            dyv = dy_ref[start:start + ch, :].astype(F32)
            sg = _sigmoid(gate)
            dv_scr[start:start + ch, :] = dyv * (gate * sg)
            dg_scr[start:start + ch, :] = dyv * val * (sg * (1.0 + gate * (1.0 - sg)))

        for u_scr, d_scr, w, du_ref, dw_ref, db_ref in ((g_scr, dg_scr, wg, dug_ref, dwg_ref, dbg_ref),
                                                         (v_scr, dv_scr, wv, duv_ref, dwv_ref, dbv_ref)):
            sums = [jnp.zeros((8, LANES), F32) for _ in range(4)]
            for start in range(0, S, ch):
                x = u_scr[HALO + start:HALO + start + ch, :]
                d = [d_scr[start + n:start + n + ch, :] for n in range(3)]
                du_ref[start:start + ch, :] = (w[2:3, :] * d[0] + w[1:2, :] * d[1] + w[0:1, :] * d[2]).astype(ACT_DTYPE)
                sums = [sums[0] + _fold8(x * d[2]), sums[1] + _fold8(x * d[1]), sums[2] + _fold8(x * d[0]),
                        sums[3] + _fold8(d[0])]
            total = [jnp.sum(s, axis=0, keepdims=True) for s in sums]
            dw_ref[...] += jnp.concatenate(total[:3], axis=0)
            db_ref[...] += total[3]

    def swap(spec_fn):
        return lambda j, b: spec_fn(b, j)

    ug, uv, wg, wv, bg, bv = _conv_specs(S, nf)
    ins = [pl.BlockSpec(s.block_shape, swap(s.index_map)) for s in (ug, uv, wg, wv, bg, bv, ug)]
    outs = [ins[0], ins[0], ins[2], ins[2], ins[4], ins[4]]
    frames = pltpu.VMEM((HALO + S, LANES), F32)
    return pl.pallas_call(
        body, name=name, grid=(nf, B), in_specs=ins, out_specs=outs, scratch_shapes=[frames] * 4,
        out_shape=[jax.ShapeDtypeStruct((B, S, F), ACT_DTYPE), jax.ShapeDtypeStruct((B, S, F), ACT_DTYPE),
                   jax.ShapeDtypeStruct((3, F), F32), jax.ShapeDtypeStruct((3, F), F32),
                   jax.ShapeDtypeStruct((1, F), F32), jax.ShapeDtypeStruct((1, F), F32)],
        compiler_params=_params('parallel', 'arbitrary'),
    )(u, u, cw, cw, cb, cb, dy)


ADAM_BLOCK_BYTES = 1024 * 1024


def _adamw(w, g, m, v, layer, earlier, name):
    L, r, c = w.shape
    tr = r
    if r * c * 4 > ADAM_BLOCK_BYTES and r % 8 == 0:
        tr = 8
        for t in range(8, r + 1, 8):
            if r % t == 0 and t * c * 4 <= ADAM_BLOCK_BYTES:
                tr = t

    def body(w_ref, g_ref, m_ref, v_ref, *rest):
        go_ref, d_ref, nm_ref, nv_ref = rest[-4:]
        gv = g_ref[...]
        nm = ADAM_B1 * m_ref[...] + (1.0 - ADAM_B1) * gv
        nv = ADAM_B2 * v_ref[...] + (1.0 - ADAM_B2) * (gv * gv)
        m_hat = nm / (1.0 - ADAM_B1 ** ADAM_STEP)
        v_hat = nv / (1.0 - ADAM_B2 ** ADAM_STEP)
        d_ref[...] = -ADAM_LR * (m_hat / (jnp.sqrt(v_hat) + ADAM_EPS) + ADAM_WD * w_ref[...])
        nm_ref[...] = nm
        nv_ref[...] = nv
        go_ref[...] = gv

    lay = pl.BlockSpec((None, tr, c), lambda i: (layer, i, 0))
    one = pl.BlockSpec((tr, c), lambda i: (i, 0))
    shp = jax.ShapeDtypeStruct((L, r, c), F32)
    in_specs = [lay, one, lay, lay]
    args = (w, g, m, v)
    aliases = {}
    if earlier is not None:
        in_specs += [ANY] * 4
        args += tuple(earlier)
        aliases = {4 + k: k for k in range(4)}
    return pl.pallas_call(
        body, name=name, grid=(r // tr,), in_specs=in_specs, out_specs=[lay] * 4, out_shape=[shp] * 4,
        input_output_aliases=aliases, compiler_params=_params('parallel'),
    )(*args)


def _my_place():
    return lax.axis_index('x'), lax.axis_index('y'), lax.axis_index('c')


def _other_chips(x, y):
    return [(1 - x, y), (x, 1 - y), (1 - x, 1 - y)]


def _remote(src, dst, send_sem, recv_sem, to):
    return pltpu.make_async_remote_copy(src_ref=src, dst_ref=dst, send_sem=send_sem, recv_sem=recv_sem,
                                        device_id=to, device_id_type=MESH)


def _hbm_call(body, n_in, out_shapes, scratch, name, aliases=None):
    return pl.pallas_call(body, name=name, in_specs=[ANY] * n_in, out_specs=[ANY] * len(out_shapes),
                          out_shape=out_shapes, scratch_shapes=scratch, input_output_aliases=aliases or {})


def _full_shape(shard_shape, kind):
    L, r, c = shard_shape
    return {'rows': (L, N_CHIPS * r, c), 'cols': (L, r, N_CHIPS * c), 'stack': (N_CHIPS * L, r, c)}[kind]


def _place_block(w, kind, out_dtype, chip_arr, name):
    L, r, c = w.shape
    tr = r if r % 16 else _tile(r, max(16, SUM_BLOCK_BYTES // (4 * c)), 16)
    nrt = r // tr

    def body(k_ref, w_ref, o_ref):
        o_ref[...] = w_ref[...].astype(out_dtype)

    out_map = {'rows': lambda l, i, k_ref: (l, k_ref[0] * nrt + i, 0),
               'cols': lambda l, i, k_ref: (l, i, k_ref[0]),
               'stack': lambda l, i, k_ref: (k_ref[0] * L + l, i, 0)}[kind]
    gs = pltpu.PrefetchScalarGridSpec(
        num_scalar_prefetch=1, grid=(L, nrt),
        in_specs=[pl.BlockSpec((None, tr, c), lambda l, i, k_ref: (l, i, 0))],
        out_specs=pl.BlockSpec((None, tr, c), out_map))
    return pl.pallas_call(
        body, name=name, grid_spec=gs, out_shape=jax.ShapeDtypeStruct(_full_shape(w.shape, kind), out_dtype),
        compiler_params=_params('parallel', 'parallel'),
    )(chip_arr, w)


class _Exchange:
    def __init__(self, inputs, out_shapes, aliases, scratch, start, finish):
        self.inputs, self.out_shapes, self.aliases, self.scratch = list(inputs), list(out_shapes), aliases, scratch
        self.start, self.finish = start, finish


def _run_exchange(ex, name):
    n_in, n_out = len(ex.inputs), len(ex.out_shapes)

    def body(*refs):
        parts = refs[:n_in], refs[n_in:n_in + n_out], refs[n_in + n_out:]
        ex.start(*parts)
        ex.finish(*parts)

    return _hbm_call(body, n_in, ex.out_shapes, ex.scratch, name, aliases=ex.aliases)(*ex.inputs)


def _call_beside(body, name, grid, in_specs, out_specs, out_shape, scratch, args, semantics, beside):
    if beside is None:
        outs = pl.pallas_call(body, name=name, grid=grid, in_specs=in_specs, out_specs=out_specs, out_shape=out_shape,
                              scratch_shapes=scratch, compiler_params=_params(*semantics))(*args)
        return outs, None
    n_in, n_out, n_scr = len(in_specs), len(out_specs), len(scratch)
    b_in, b_out = len(beside.inputs), len(beside.out_shapes)

    def carrier(*refs):
        cuts = [n_in, b_in, n_out, b_out, n_scr]
        parts, at = [], 0
        for size in cuts:
            parts.append(refs[at:at + size])
            at += size
        ins, ex_ins, outs, ex_outs, scr = parts
        ex_scr = refs[at:]
        ids = [pl.program_id(d) for d in range(len(grid))]
        first = functools.reduce(jnp.logical_and, [i == 0 for i in ids])
        last = functools.reduce(jnp.logical_and, [i == g - 1 for i, g in zip(ids, grid)])

        @pl.when(first)
        def _():
            beside.start(ex_ins, ex_outs, ex_scr)

        body(*ins, *outs, *scr)

        @pl.when(last)
        def _():
            beside.finish(ex_ins, ex_outs, ex_scr)

    res = pl.pallas_call(
        carrier, name=name, grid=grid, in_specs=list(in_specs) + [ANY] * b_in,
        out_specs=list(out_specs) + [ANY] * b_out, out_shape=list(out_shape) + beside.out_shapes,
        scratch_shapes=list(scratch) + beside.scratch,
        input_output_aliases={n_in + i: n_out + o for i, o in beside.aliases.items()},
        compiler_params=_params(*['arbitrary'] * len(grid)),
    )(*args, *beside.inputs)
    return res[:n_out], res[n_out:]


def _gather_exchange(fulls, shard_shapes, kinds, split):
    n = len(fulls)

    def plan(outs, send_sems, recv_sems):
        x, y, c = _my_place()
        chip = 2 * x + y
        sibling = (x, y, 1 - c)
        others = _other_chips(x, y)

        def window(a, k, half):
            L, r, cols = shard_shapes[a]
            first, count = (0, r) if half is None else (half * (r // 2), r // 2)
            if kinds[a] == 'rows':
                return outs[a].at[:, pl.ds(k * r + first, count), :]
            if kinds[a] == 'cols':
                return outs[a].at[:, pl.ds(first, count), pl.ds(pl.multiple_of(k * cols, LANES), cols)]
            return outs[a].at[pl.ds(k * L, L), pl.ds(first, count), :]

        sends, arrivals, forwards, forwarded = [], [], [], []
        for a in range(n):
            half = c if split[a] else None
            for j, (ox, oy) in enumerate(others):
                sems = (send_sems.at[6 * a + j], recv_sems.at[6 * a + j], (ox, oy, c))
                sends.append(_remote(window(a, chip, half), window(a, chip, half), *sems))
                got = window(a, 2 * ox + oy, half)
                arrivals.append(_remote(got, got, *sems))
                if split[a]:
                    sems = (send_sems.at[6 * a + 3 + j], recv_sems.at[6 * a + 3 + j], sibling)
                    forwards.append(_remote(got, got, *sems))
                    theirs = window(a, 2 * ox + oy, 1 - c)
                    forwarded.append(_remote(theirs, theirs, *sems))
                else:
                    forwards.append(None)
        return sends, arrivals, forwards, forwarded

    def start(ins, outs, scratch):
        sends, _, _, _ = plan(outs, *scratch)
        for cp in sends:
            cp.start()

    def finish(ins, outs, scratch):
        sends, arrivals, forwards, forwarded = plan(outs, *scratch)
        for arrived, fw in zip(arrivals, forwards):
            arrived.wait_recv()
            if fw is not None:
                fw.start()
        for cp in forwarded:
            cp.wait_recv()
        for cp in sends + [fw for fw in forwards if fw is not None]:
            cp.wait_send()

    scratch = [pltpu.SemaphoreType.DMA((6 * n,)), pltpu.SemaphoreType.DMA((6 * n,))]
    out_shapes = [jax.ShapeDtypeStruct(f.shape, f.dtype) for f in fulls]
    return _Exchange(fulls, out_shapes, {a: a for a in range(n)}, scratch, start, finish)


def _swap_cores(gs, name='swap_cores'):
    n = len(gs)
    out_shapes = [jax.ShapeDtypeStruct((g.shape[0], g.shape[1] // 2, g.shape[2]), g.dtype) for g in gs]

    def body(*refs):
        ins, outs = refs[:n], refs[n:2 * n]
        send_sems, recv_sems = refs[2 * n:]
        x, y, c = _my_place()
        cps = []
        for a in range(n):
            rh = gs[a].shape[1] // 2
            cp = _remote(ins[a].at[:, pl.ds((1 - c) * rh, rh), :], outs[a], send_sems.at[a], recv_sems.at[a],
                         (x, y, 1 - c))
            cp.start()
            cps.append(cp)
        for cp in cps:
            cp.wait()

    scratch = [pltpu.SemaphoreType.DMA((n,)), pltpu.SemaphoreType.DMA((n,))]
    return _hbm_call(body, n, out_shapes, scratch, name)(*gs)


SUM_BLOCK_BYTES = 2 * 1024 * 1024


def _sum_rows(rh, cols):
    return _tile(rh, max(16, SUM_BLOCK_BYTES // (4 * cols)), 16)


def _add_cores(g, other, c_arr, wire_dtype, name):
    n, r, cols = g.shape
    rh = r // 2
    tr = _sum_rows(rh, cols)
    nrt = rh // tr

    def body(c_ref, g_ref, o_ref, q_ref):
        q_ref[...] = (g_ref[...] + o_ref[...]).astype(wire_dtype)

    gs = pltpu.PrefetchScalarGridSpec(
        num_scalar_prefetch=1, grid=(n, nrt),
        in_specs=[pl.BlockSpec((None, tr, cols), lambda j, i, c_ref: (j, c_ref[0] * nrt + i, 0)),
                  pl.BlockSpec((None, tr, cols), lambda j, i, c_ref: (j, i, 0))],
        out_specs=pl.BlockSpec((None, tr, cols), lambda j, i, c_ref: (j, i, 0)))
    return pl.pallas_call(
        body, name=name, grid_spec=gs, out_shape=jax.ShapeDtypeStruct((n, rh, cols), wire_dtype),
        compiler_params=_params('parallel', 'parallel'),
    )(c_arr, g, other)


def _send_exchange(qs):
    n = len(qs)

    def plan(ins, outs, send_sems, recv_sems):
        x, y, c = _my_place()
        return [_remote(ins[a].at[2 * ox + oy], outs[a].at[j], send_sems.at[3 * a + j], recv_sems.at[3 * a + j],
                        (ox, oy, c))
                for a in range(n) for j, (ox, oy) in enumerate(_other_chips(x, y))]

    def start(ins, outs, scratch):
        for cp in plan(ins, outs, *scratch):
            cp.start()

    def finish(ins, outs, scratch):
        cps = plan(ins, outs, *scratch)
        for cp in cps:
            cp.wait_recv()
        for cp in cps:
            cp.wait_send()

    scratch = [pltpu.SemaphoreType.DMA((3 * n,)), pltpu.SemaphoreType.DMA((3 * n,))]
    out_shapes = [jax.ShapeDtypeStruct((3,) + q.shape[1:], q.dtype) for q in qs]
    return _Exchange(qs, out_shapes, {}, scratch, start, finish)


def _sum_chips(q, got, place_arr, name):
    n, rh, cols = q.shape
    tr = _sum_rows(rh, cols)

    def body(p_ref, q_ref, gx_ref, gy_ref, gxy_ref, o_ref):
        f = lambda r: r[...].astype(F32)
        o_ref[...] = (f(q_ref) + f(gxy_ref)) + (f(gx_ref) + f(gy_ref))

    def got_spec(j):
        return pl.BlockSpec((None, tr, cols), lambda i, p_ref: (j, i, 0))

    gs = pltpu.PrefetchScalarGridSpec(
        num_scalar_prefetch=1, grid=(rh // tr,),
        in_specs=[pl.BlockSpec((None, tr, cols), lambda i, p_ref: (p_ref[0], i, 0)),
                  got_spec(0), got_spec(1), got_spec(2)],
        out_specs=pl.BlockSpec((None, tr, cols), lambda i, p_ref: (p_ref[1], i, 0)))
    return pl.pallas_call(
        body, name=name, grid_spec=gs, out_shape=jax.ShapeDtypeStruct((2, rh, cols), F32),
        compiler_params=_params('parallel'),
    )(place_arr, q, got, got, got)


def _join_cores(rs, name='join_cores'):
    n = len(rs)
    out_shapes = [jax.ShapeDtypeStruct(r.shape, r.dtype) for r in rs]

    def body(*refs):
        outs = refs[n:2 * n]
        send_sems, recv_sems = refs[2 * n:]
        x, y, c = _my_place()
        cps = []
        for a in range(n):
            cp = _remote(outs[a].at[c], outs[a].at[c], send_sems.at[a], recv_sems.at[a], (x, y, 1 - c))
            cp.start()
            cps.append(cp)
        for cp in cps:
            cp.wait()

    scratch = [pltpu.SemaphoreType.DMA((n,)), pltpu.SemaphoreType.DMA((n,))]
    return _hbm_call(body, n, out_shapes, scratch, name, aliases={a: a for a in range(n)})(*rs)


def _gate_rows(t, B, S):
    return t.reshape(B, S, N_MAIN_HEADS).transpose(0, 2, 1).reshape(B * N_MAIN_HEADS, S)


def _gate_cols(t, B, S):
    return t.reshape(B, N_MAIN_HEADS, S).transpose(0, 2, 1).reshape(B * S, N_MAIN_HEADS)


def _mem_kv_fwd(mem2, g, w, tag):
    hm = _rms_fwd(mem2, g, name=f'rms_mem_{tag}')
    mkv = _matmul(hm, w, 'nn', ACT_DTYPE, name=f'mm_memkv_{tag}')
    return hm, mkv


def _mem_kv_bwd(mem2, g, w, hm, dmk, dmv, tag):
    dmkv = jnp.concatenate([dmk, dmv], axis=1)
    dw = _matmul(hm, dmkv, 'tn', F32, name=f'mm_memkv_dw_{tag}')
    dhm = _matmul(dmkv, w, 'nt', F32, name=f'mm_memkv_dx_{tag}')
    _, dg = _rms_bwd(mem2, g, dhm, None, name=f'rms_mem_bwd_{tag}')
    return dw, dg


def _ffn_fwd(x, g, w_up, cw, cb, w_down, B, S, tag):
    T = x.shape[0]
    h2 = _rms_fwd(x, g, name=f'rms_ffn_{tag}')
    u = _matmul(h2, w_up, 'nn', ACT_DTYPE, name=f'mm_up_{tag}')
    y = _conv_fwd(u.reshape(B, S, -1), cw, cb, name=f'conv_fwd_{tag}').reshape(T, -1)
    x2 = _matmul(y, w_down, 'nn', F32, res=x, name=f'mm_down_{tag}')
    return x2, (h2, u, y)


def _ffn_bwd(dx2, x, g, w_up, cw, cb, w_down, saved, B, S, tag):
    h2, u, y = saved
    T = x.shape[0]
    dy = _matmul(dx2, w_down, 'nt', ACT_DTYPE, name=f'mm_down_dx_{tag}')
    dw_down = _matmul(y, dx2, 'tn', F32, name=f'mm_down_dw_{tag}')
    dug, duv, dcwg, dcwv, dcbg, dcbv = _conv_bwd(u.reshape(B, S, -1), cw, cb, dy.reshape(B, S, -1),
                                                  name=f'conv_bwd_{tag}')
    dug, duv = dug.reshape(T, -1), duv.reshape(T, -1)
    dh2 = _matmul((dug, duv), w_up, 'nt', F32, name=f'mm_up_dx_{tag}')
    half = N_CHIPS // 2
    dw_gate = _matmul(h2, dug, 'tn', F32, slots=half, total_slots=N_CHIPS, name=f'mm_up_dw_gate_{tag}')
    dw_up = _matmul(h2, duv, 'tn', F32, slots=half, slot_base=half, total_slots=N_CHIPS, into=dw_gate,
                    name=f'mm_up_dw_val_{tag}')
    dx, dg = _rms_bwd(x, g, dh2, dx2, name=f'rms_ffn_bwd_{tag}')
    dcw = jnp.concatenate([dcwg, dcwv], axis=1)
    dcb = jnp.concatenate([dcbg, dcbv], axis=1)
    return dx, dg, dw_up, dcw, dcb, dw_down


def _step(x, mem, tgt, W, late_weights=None, reduce_early=None):
    B, S, D = x.shape
    T = B * S
    x0 = x.reshape(T, D)
    mem2 = mem.reshape(-1, D)
    tgt2 = tgt.reshape(T, D)
    row = lambda v: v.reshape(1, -1)
    q3 = 3 * MAIN_WIDTH

    w_in_a = W['w_in_a'][0]
    wa_main = jnp.concatenate([w_in_a[:, :q3], w_in_a[:, q3 + N_MAIN_HEADS:]], axis=1)
    wa_gate = jnp.pad(w_in_a[:, q3:q3 + N_MAIN_HEADS], ((0, 0), (0, LANES - N_MAIN_HEADS)))
    bcol = jnp.tile(W['b_f_a'][0], B).reshape(B * N_MAIN_HEADS, 1)
    nkb = S // _blocks(S, FOX_ROWS, FOX_KEYS)[1]

    h1a = _rms_fwd(x0, row(W['ln_mix_g'][0]), name='rms_mix_a')
    pa = _matmul(h1a, wa_main, 'nn', ACT_DTYPE, name='mm_in_a')
    flog = _matmul(h1a, wa_gate, 'nn', F32, name='mm_gate_a')
    qkv_a = (pa, pa, pa)
    offs_a = (0, N_MAIN_PAIRS, 2 * N_MAIN_PAIRS)
    qm_off_a = 3 * N_MAIN_PAIRS
    zt = _gate_rows(flog[:, :N_MAIN_HEADS], B, S)
    cum = _gate_fwd(zt, bcol)
    ccol = cum.reshape(B * N_MAIN_HEADS, S, 1)
    crow = cum.reshape(B * N_MAIN_HEADS, nkb, 1, S // nkb)
    (oa, lse), late = _fox_fwd(qkv_a, offs_a, B, S, ccol, crow, beside=late_weights[0] if late_weights else None)
    if late_weights:
        W = {**W, **late_weights[1](late)}
    w_in_b = W['w_in_b'][0]
    hma, mkva = _mem_kv_fwd(mem2, row(W['ln_mem_g'][0]), W['w_memkv'][0], 'a')
    oma = _mem_fwd(pa, qm_off_a, mkva, B, S, name='mem_fwd_a')
    ocat_a = jnp.concatenate([oa, oma], axis=1)
    x1 = _matmul(ocat_a, W['w_out'][0], 'nn', F32, res=x0, name='mm_out_a')
    x2, ffn_a = _ffn_fwd(x1, row(W['ln_ffn_g'][0]), W['w_up'][0], W['conv_w'][0], row(W['conv_b'][0]),
                         W['w_down'][0], B, S, 'a')
    hkv = _rms_fwd(x2, row(W['ln_kv_g']), name='rms_kv')
    kvs = _matmul(hkv, W['w_kv'], 'nn', ACT_DTYPE, name='mm_kv')
    h1b = _rms_fwd(x2, row(W['ln_mix_g'][1]), name='rms_mix_b')
    pb = _matmul(h1b, w_in_b, 'nn', ACT_DTYPE, name='mm_in_b')
    qkv_b = (pb, kvs, kvs)
    offs_b = (0, 0, N_MAIN_PAIRS)
    qm_off_b = N_MAIN_PAIRS
    ob, tot_b = _sb_fwd(qkv_b, offs_b, B, S)
    hmb, mkvb = _mem_kv_fwd(mem2, row(W['ln_mem_g'][1]), W['w_memkv'][1], 'b')
    omb = _mem_fwd(pb, qm_off_b, mkvb, B, S, name='mem_fwd_b')
    ocat_b = jnp.concatenate([ob, omb], axis=1)
    x3 = _matmul(ocat_b, W['w_out'][1], 'nn', F32, res=x2, name='mm_out_b')
    x4, ffn_b = _ffn_fwd(x3, row(W['ln_ffn_g'][1]), W['w_up'][1], W['conv_w'][1], row(W['conv_b'][1]),
                         W['w_down'][1], B, S, 'b')
    loss, dx4, d_final_g = _final_loss(x4, row(W['final_g']), tgt2)

    dx3, dg_ffn_b, dw_up_b, dcw_b, dcb_b, dw_down_b = _ffn_bwd(
        dx4, x3, row(W['ln_ffn_g'][1]), W['w_up'][1], W['conv_w'][1], row(W['conv_b'][1]), W['w_down'][1],
        ffn_b, B, S, 'b')
    docat = _matmul(dx3, W['w_out'][1], 'nt', ACT_DTYPE, name='mm_out_dx_b')
    dw_out_b = _matmul(ocat_b, dx3, 'tn', F32, name='mm_out_dw_b')
    dqb, dkb, dvb = _sb_bwd(qkv_b, offs_b, B, S, tot_b, docat)
    dqmb, dmkb, dmvb = _mem_bwd(pb, qm_off_b, mkvb, B, S, docat, N_MAIN_PAIRS, name='mem_bwd_b')
    dw_memkv_b, dg_mem_b = _mem_kv_bwd(mem2, row(W['ln_mem_g'][1]), W['w_memkv'][1], hmb, dmkb, dmvb, 'b')
    dpb = jnp.concatenate([dqb, dqmb], axis=1)
    dh1b = _matmul(dpb, w_in_b, 'nt', F32, name='mm_in_dx_b')
    dw_in_b = _matmul(h1b, dpb, 'tn', F32, name='mm_in_dw_b')
    dx2, dg_mix_b = _rms_bwd(x2, row(W['ln_mix_g'][1]), dh1b, dx3, name='rms_mix_bwd_b')
    dkvs = jnp.concatenate([dkb, dvb], axis=1)
    dhkv = _matmul(dkvs, W['w_kv'], 'nt', F32, name='mm_kv_dx')
    dw_kv = _matmul(hkv, dkvs, 'tn', F32, slots=N_CHIPS, name='mm_kv_dw')
    dx2, dg_kv = _rms_bwd(x2, row(W['ln_kv_g']), dhkv, dx2, name='rms_kv_bwd')

    dx1, dg_ffn_a, dw_up_a, dcw_a, dcb_a, dw_down_a = _ffn_bwd(
        dx2, x1, row(W['ln_ffn_g'][0]), W['w_up'][0], W['conv_w'][0], row(W['conv_b'][0]), W['w_down'][0],
        ffn_a, B, S, 'a')
    docat = _matmul(dx1, W['w_out'][0], 'nt', ACT_DTYPE, name='mm_out_dx_a')
    dw_out_a = _matmul(ocat_a, dx1, 'tn', F32, name='mm_out_dw_a')

    def by_rows(dw):
        return dw.reshape(N_CHIPS, dw.shape[0] // N_CHIPS, dw.shape[1])

    grads = {
        'w_in_b': [by_rows(dw_in_b)],
        'w_kv': [dw_kv],
        'w_out': [by_rows(dw_out_a), by_rows(dw_out_b)],
        'w_up': [dw_up_a, dw_up_b],
        'w_down': [by_rows(dw_down_a), by_rows(dw_down_b)],
    }
    early = [(n, layer, g) for n, gs in grads.items() for layer, g in enumerate(gs)]
    early.append(('w_memkv', 1, by_rows(dw_memkv_b)))
    beside = reduce_early(early) if reduce_early else None
    (dqa, dka, dva, dccol, dcrow), crossed = _fox_bwd(qkv_a, offs_a, B, S, ccol, crow, oa, lse, docat, beside=beside)
    dzt, dbrow = _gate_bwd(zt, bcol, dccol.reshape(B * N_MAIN_HEADS, S) + dcrow.reshape(B * N_MAIN_HEADS, S))
    dqma, dmka, dmva = _mem_bwd(pa, qm_off_a, mkva, B, S, docat, N_MAIN_PAIRS, name='mem_bwd_a')
    dw_memkv_a, dg_mem_a = _mem_kv_bwd(mem2, row(W['ln_mem_g'][0]), W['w_memkv'][0], hma, dmka, dmva, 'a')
    dflog = jnp.pad(_gate_cols(dzt, B, S), ((0, 0), (0, LANES - N_MAIN_HEADS))).astype(ACT_DTYPE)
    dpa = jnp.concatenate([dqa, dka, dva, dqma, dflog], axis=1)
    wa_all = jnp.concatenate([wa_main, wa_gate], axis=1)
    dh1a = _matmul(dpa, wa_all, 'nt', F32, name='mm_in_dx_a')
    dwa = _matmul(h1a, dpa, 'tn', F32, name='mm_in_dw_a')
    dx0, dg_mix_a = _rms_bwd(x0, row(W['ln_mix_g'][0]), dh1a, dx1, name='rms_mix_bwd_a')

    n_main = wa_main.shape[1]
    dw_in_a = jnp.concatenate([dwa[:, :q3], dwa[:, n_main:n_main + N_MAIN_HEADS], dwa[:, q3:n_main]], axis=1)
    grads.update({
        'ln_mix_g': jnp.concatenate([dg_mix_a, dg_mix_b], axis=0),
        'w_in_a': dw_in_a[None],
        'b_f_a': dbrow.reshape(B, N_MAIN_HEADS).sum(axis=0)[None],
        'ln_kv_g': dg_kv[0],
        'ln_mem_g': jnp.concatenate([dg_mem_a, dg_mem_b], axis=0),
        'w_memkv': [by_rows(dw_memkv_a), early[-1][2]],
        'ln_ffn_g': jnp.concatenate([dg_ffn_a, dg_ffn_b], axis=0),
        'conv_w': jnp.stack([dcw_a, dcw_b]),
        'conv_b': jnp.concatenate([dcb_a, dcb_b], axis=0),
        'final_g': d_final_g[0],
    })
    return loss, dx0.reshape(B, S, D), grads, crossed


BLOCKED = ('w_in_b', 'w_kv', 'w_memkv', 'w_out', 'w_up', 'w_down')
MISC_ROWS = 32


def _misc_names():
    return [n for n in PARAM_NAMES if PARAM_SHARD_AXIS[n] is None] + ['conv_w']


def _reduce_begin(arrays, wire, tag):
    _, _, c = _my_place()
    c_arr = jnp.reshape(c, (1,)).astype(jnp.int32)
    others = _swap_cores(arrays, name=f'swap_cores_{tag}')
    return [_add_cores(g, o, c_arr, wire[i], name=f'add_cores_{tag}_{i}')
            for i, (g, o) in enumerate(zip(arrays, others))]


def _reduce_end(qs, crossed, tag):
    x, y, c = _my_place()
    place_arr = jnp.stack([2 * x + y, c]).astype(jnp.int32)
    sums = [_sum_chips(q, g, place_arr, name=f'sum_chips_{tag}_{i}') for i, (q, g) in enumerate(zip(qs, crossed))]
    return [j.reshape(-1, j.shape[-1]) for j in _join_cores(sums, name=f'join_cores_{tag}')]


def _pack_late(grads, shards):
    a_cols = shards['w_in_a'].shape[2]
    a_pad = -(-a_cols // LANES) * LANES
    dw_in_a = grads['w_in_a'][0]
    in_a = jnp.stack([jnp.pad(dw_in_a[:, k * a_cols:(k + 1) * a_cols], ((0, 0), (0, a_pad - a_cols)))
                      for k in range(N_CHIPS)])
    conv_cols = shards['conv_w'].shape[2]
    misc = []
    for k in range(N_CHIPS):
        parts = [grads[n].reshape(-1) for n in _misc_names()[:-1]]
        parts.append(grads['conv_w'][:, :, k * conv_cols:(k + 1) * conv_cols].reshape(-1))
        flat = jnp.concatenate(parts)
        assert flat.shape[0] <= MISC_ROWS * PACK_COLS
        misc.append(jnp.pad(flat, (0, MISC_ROWS * PACK_COLS - flat.shape[0])).reshape(MISC_ROWS, PACK_COLS))
    return in_a, jnp.stack(misc)


def _unpack_misc(rows, shards):
    flat = rows.reshape(-1)
    out, off = {}, 0
    for name in _misc_names():
        shape = shards[name].shape
        size = math.prod(shape)
        out[name] = flat[off:off + size].reshape(-1, shape[-1])
        off += size
    return out


def kernel(x, mem, ln_mix_g, w_in_a, b_f_a, w_in_b, ln_kv_g, w_kv, ln_mem_g, w_memkv, w_out, ln_ffn_g, w_up, conv_w, conv_b, w_down, final_g, loss_target, m_ln_mix_g, m_w_in_a, m_b_f_a, m_w_in_b, m_ln_kv_g, m_w_kv, m_ln_mem_g, m_w_memkv, m_w_out, m_ln_ffn_g, m_w_up, m_conv_w, m_conv_b, m_w_down, m_final_g, v_ln_mix_g, v_w_in_a, v_b_f_a, v_w_in_b, v_ln_kv_g, v_w_kv, v_ln_mem_g, v_w_memkv, v_w_out, v_ln_ffn_g, v_w_up, v_conv_w, v_conv_b, v_w_down, v_final_g):
    shards = dict(ln_mix_g=ln_mix_g, w_in_a=w_in_a, b_f_a=b_f_a, w_in_b=w_in_b, ln_kv_g=ln_kv_g, w_kv=w_kv,
                  ln_mem_g=ln_mem_g, w_memkv=w_memkv, w_out=w_out, ln_ffn_g=ln_ffn_g, w_up=w_up, conv_w=conv_w,
                  conv_b=conv_b, w_down=w_down, final_g=final_g)
    moments_m = dict(ln_mix_g=m_ln_mix_g, w_in_a=m_w_in_a, b_f_a=m_b_f_a, w_in_b=m_w_in_b, ln_kv_g=m_ln_kv_g,
                     w_kv=m_w_kv, ln_mem_g=m_ln_mem_g, w_memkv=m_w_memkv, w_out=m_w_out, ln_ffn_g=m_ln_ffn_g,
                     w_up=m_w_up, conv_w=m_conv_w, conv_b=m_conv_b, w_down=m_w_down, final_g=m_final_g)
    moments_v = dict(ln_mix_g=v_ln_mix_g, w_in_a=v_w_in_a, b_f_a=v_b_f_a, w_in_b=v_w_in_b, ln_kv_g=v_ln_kv_g,
                     w_kv=v_w_kv, ln_mem_g=v_ln_mem_g, w_memkv=v_w_memkv, w_out=v_w_out, ln_ffn_g=v_ln_ffn_g,
                     w_up=v_w_up, conv_w=v_conv_w, conv_b=v_conv_b, w_down=v_w_down, final_g=v_final_g)

    kinds = {'w_in_a': 'stack', 'w_in_b': 'rows', 'w_kv': 'cols', 'w_memkv': 'rows', 'w_out': 'rows', 'w_up': 'cols',
             'w_down': 'rows', 'conv_w': 'cols'}
    mx, my, _ = _my_place()
    chip_arr = jnp.reshape(2 * mx + my, (1,)).astype(jnp.int32)
    placed, shard_shapes = {}, {}
    for n, kind in kinds.items():
        w = shards[n].reshape((-1,) + shards[n].shape[-2:])
        shard_shapes[n] = w.shape
        placed[n] = _place_block(w, kind, F32 if n in F32_GATHERED else jnp.bfloat16, chip_arr, name=f'place_{n}')

    def gather(names):
        return _gather_exchange([placed[n] for n in names], [shard_shapes[n] for n in names],
                                [kinds[n] for n in names], [n not in F32_GATHERED for n in names])

    def as_weights(names, full):
        out = dict(zip(names, full))
        if 'w_in_a' in out:
            out['w_in_a'] = jnp.concatenate([out['w_in_a'][k] for k in range(N_CHIPS)], axis=1)[None]
        if 'w_kv' in out:
            out['w_kv'] = out['w_kv'][0]
        return out

    first = ['w_in_a', 'conv_w']
    late = [n for n in kinds if n not in first]
    W = {**shards, **as_weights(first, _run_exchange(gather(first), 'gather_first'))}

    early = {}

    def reduce_early(items):
        early['owners'] = [(n, layer) for n, layer, _ in items]
        early['qs'] = _reduce_begin([g for _, _, g in items], [jnp.bfloat16] * len(items), 'early')
        return _send_exchange(early['qs'])

    loss_part, grad_x, grads, crossed = _step(x, mem, loss_target, W, (gather(late), functools.partial(as_weights, late)),
                                              reduce_early)
    loss = lax.psum(loss_part[0, 0], ('x', 'y', 'c'))

    g_layers = {n: [None] * (len(grads[n]) if n in BLOCKED else 1) for n in PARAM_NAMES}
    for (n, layer), g in zip(early['owners'], _reduce_end(early['qs'], crossed, 'early')):
        g_layers[n][layer] = g
    in_a, misc = _pack_late(grads, shards)
    qs = _reduce_begin([in_a, misc, grads['w_memkv'][0]], [jnp.bfloat16, F32, jnp.bfloat16], 'late')
    in_a_sum, misc_sum, memkv_sum = _reduce_end(qs, _run_exchange(_send_exchange(qs), 'send_chips_late'), 'late')
    g_layers['w_in_a'][0] = in_a_sum[:, :shards['w_in_a'].shape[2]]
    g_layers['w_memkv'][0] = memkv_sum
    for n, g in _unpack_misc(misc_sum, shards).items():
        g_layers[n][0] = g

    results = {}
    for name in PARAM_NAMES:
        w = shards[name]
        layers = len(g_layers[name])
        as_layers = (layers, -1, w.shape[-1])
        w3, m3, v3 = (t.reshape(as_layers) for t in (w, moments_m[name], moments_v[name]))
        res = None
        for layer, g in enumerate(g_layers[name]):
            res = _adamw(w3, g, m3, v3, layer, res, name=f'adamw_{name}_{layer}')
        results[name] = [t.reshape(w.shape) for t in res]

    return (loss, grad_x, *[results[n][k] for k in range(4) for n in PARAM_NAMES])
```
